```python
import jax
import jax.numpy as jnp
from jax import lax
import numpy as np

D_MODEL = 1024
BATCH = 8
SEQ = 8192
DEPTH = 2

GRID_W = 64
CTX_LEN = 256
SSD_D_INNER = D_MODEL
SSD_HEAD_DIM = 64
SSD_HEADS = SSD_D_INNER // SSD_HEAD_DIM
SSD_GROUPS = 2
SSD_STATE = 128
SSD_CONV = 5
SSD_CHUNK = 128
SSD_XBC = SSD_D_INNER + 2 * SSD_GROUPS * SSD_STATE
SG_WIDTH = D_MODEL
SG_CHUNK = 128
SG_GROUP_DIM = 128
SG_GROUPS = SG_WIDTH // SG_GROUP_DIM
ROWS_PER_CHUNK = SG_CHUNK // GRID_W
MIX_WIDTH = SSD_D_INNER + SG_WIDTH
IN_PROJ = SSD_D_INNER + SSD_XBC + 2 * SSD_HEADS + 2 * SG_WIDTH
CONF_WIDTH = D_MODEL
CONF_CONV_WIDTH = 31
MLP_HIDDEN = 4 * D_MODEL
EPS = 1e-6

kernel_name = "hybrid_ssd_gmlp_conformer_dit"


def rmsnorm(x, g):
    xf = x.astype(jnp.float32)
    y = xf * lax.rsqrt(jnp.mean(xf * xf, axis=-1, keepdims=True) + EPS)
    return (y * g.astype(jnp.float32)).astype(x.dtype)


def layernorm(x, g, b):
    xf = x.astype(jnp.float32)
    mu = jnp.mean(xf, axis=-1, keepdims=True)
    xc = xf - mu
    y = xc * lax.rsqrt(jnp.mean(xc * xc, axis=-1, keepdims=True) + EPS)
    return (y * g.astype(jnp.float32) + b.astype(jnp.float32)).astype(x.dtype)


def modulate(x, g, shift, scale):
    return rmsnorm(x, g) * (1 + scale) + shift


def dwconv_centred(x, w, b):
    k = w.shape[0]
    y = lax.conv_general_dilated(
        x, w[:, None, :].astype(x.dtype), window_strides=(1,),
        padding=[(k // 2, k // 2)], dimension_numbers=('NWC', 'WIO', 'NWC'),
        feature_group_count=x.shape[-1])
    return y + b.astype(x.dtype)


def ssd_chunked(x, dt, a, bm, cm, h0):
    f32 = jnp.float32
    bsz, seq_len, n_heads, p_dim = x.shape
    n_groups, n_state = bm.shape[-2], bm.shape[-1]
    r = n_heads // n_groups
    nc = seq_len // SSD_CHUNK
    xc = x.astype(f32).reshape(bsz, nc, SSD_CHUNK, n_groups, r, p_dim)
    dtc = dt.reshape(bsz, nc, SSD_CHUNK, n_groups, r)
    bc = bm.astype(f32).reshape(bsz, nc, SSD_CHUNK, n_groups, n_state)
    cc = cm.astype(f32).reshape(bsz, nc, SSD_CHUNK, n_groups, n_state)
    acs = jnp.cumsum(dtc * a.astype(f32).reshape(n_groups, r), axis=2)
    lower = jnp.tril(jnp.ones((SSD_CHUNK, SSD_CHUNK), dtype=bool))[:, :, None, None]
    seg = acs[:, :, :, None] - acs[:, :, None, :]
    decay = jnp.exp(jnp.where(lower, seg, -jnp.inf))
    scores = jnp.einsum('bcign,bcjgn->bcijg', cc, bc)
    w = scores[..., None] * decay * dtc[:, :, None]
    y_diag = jnp.einsum('bcijgr,bcjgrp->bcigrp', w, xc)
    to_end = jnp.exp(acs[:, :, -1:] - acs) * dtc
    states = jnp.einsum('bcjgn,bcjgrp->bcgrpn', bc, xc * to_end[..., None])
    chunk_decay = jnp.exp(acs[:, :, -1])

    def step(h, inp):
        s, d = inp
        return h * d[..., None, None] + s, h

    h_last, h_in = lax.scan(step, h0, (jnp.moveaxis(states, 1, 0),
                                       jnp.moveaxis(chunk_decay, 1, 0)))
    h_in = jnp.moveaxis(h_in, 0, 1)
    y_off = jnp.einsum('bcign,bcgrpn->bcigrp', cc, h_in) * jnp.exp(acs)[..., None]
    y = (y_diag + y_off).reshape(bsz, seq_len, n_heads, p_dim).astype(x.dtype)
    return y, h_last


def ssd_bidirectional(xs, dt_f, dt_b, a_f, a_b, bm, cm, h0_f, h0_b):
    flip = lambda t: jnp.flip(t, axis=1)
    y_f, h_f = ssd_chunked(xs, dt_f, a_f, bm, cm, h0_f)
    y_b, h_b = ssd_chunked(flip(xs), flip(dt_b), a_b, flip(bm), flip(cm), h0_b)
    return y_f + flip(y_b), h_f, h_b


def hybrid_mixer(u, n_chunks, h0_f, h0_b, w_in, conv_w, conv_b, dt_bias, a_log, d_skip,
                 ssd_norm_g, sg_ln_g, sg_ln_b, sg_w, sg_b, w_out):
    bsz, seq_len, _ = u.shape
    proj = u @ w_in
    o1 = SSD_D_INNER
    o2 = o1 + SSD_XBC
    o3 = o2 + 2 * SSD_HEADS
    z, xbc, dt_raw, uv = jnp.split(proj, [o1, o2, o3], axis=-1)
    xbc = jax.nn.silu(dwconv_centred(xbc, conv_w, conv_b))
    xs, bm, cm = jnp.split(xbc, [SSD_D_INNER, SSD_D_INNER + SSD_GROUPS * SSD_STATE], axis=-1)
    xs = xs.reshape(bsz, seq_len, SSD_HEADS, SSD_HEAD_DIM)
    bm = bm.reshape(bsz, seq_len, SSD_GROUPS, SSD_STATE)
    cm = cm.reshape(bsz, seq_len, SSD_GROUPS, SSD_STATE)
    dt = jax.nn.softplus(dt_raw.astype(jnp.float32).reshape(bsz, seq_len, 2, SSD_HEADS)
                         + dt_bias.astype(jnp.float32))
    a = -jnp.exp(a_log.astype(jnp.float32))
    y, h_f, h_b = ssd_bidirectional(xs, dt[:, :, 0], dt[:, :, 1], a[0], a[1], bm, cm, h0_f, h0_b)
    y = y + d_skip[:, None].astype(y.dtype) * xs
    y = y.reshape(bsz, seq_len, SSD_D_INNER) * jax.nn.silu(z)
    y_ssd = rmsnorm(y.reshape(bsz, seq_len, SSD_GROUPS, -1),
                    ssd_norm_g.reshape(SSD_GROUPS, -1)).reshape(bsz, seq_len, SSD_D_INNER)
    gate_u, v = jnp.split(jax.nn.gelu(uv), 2, axis=-1)
    v = layernorm(v, sg_ln_g, sg_ln_b).reshape(bsz, n_chunks, SG_CHUNK, SG_GROUPS, SG_GROUP_DIM)
    s = jnp.einsum('gqk,bnkgc->bnqgc', sg_w, v) + sg_b.T[:, :, None]
    y_sg = gate_u * s.reshape(bsz, seq_len, SG_WIDTH)
    out = jnp.concatenate([y_ssd, y_sg], axis=-1) @ w_out
    return out, h_f, h_b


def conformer_conv(u, w_pw1, b_pw1, w_dw, b_dw, ln_g, ln_b, w_pw2, b_pw2):
    a = u @ w_pw1 + b_pw1
    a = a[..., :CONF_WIDTH] * jax.nn.sigmoid(a[..., CONF_WIDTH:])
    a = dwconv_centred(a, w_dw, b_dw)
    a = jax.nn.silu(layernorm(a, ln_g, ln_b))
    return a @ w_pw2 + b_pw2


def sq_relu_mlp(u, w1, w2):
    return jnp.square(jax.nn.relu(u @ w1)) @ w2


def _fwd_setup_inputs(seed: int = 0) -> dict:
    key = jax.random.key(seed)
    ks = jax.random.split(key, 40)
    n_even = (DEPTH + 1) // 2
    n_odd = DEPTH // 2
    nrm = lambda k, shape, s: jax.random.normal(k, shape, jnp.float32) * s
    gain = lambda k, shape: 1.0 + 0.05 * jax.random.normal(k, shape, jnp.float32)
    dt0 = jnp.exp(jax.random.uniform(ks[13], (n_even, 2, SSD_HEADS), jnp.float32,
                                     np.log(1e-3), np.log(1e-1)))
    return {
        "x": nrm(ks[0], (BATCH, SEQ, D_MODEL), 1.0),
        "c": nrm(ks[1], (BATCH, D_MODEL), 1.0),
        "ctx": nrm(ks[2], (BATCH, CTX_LEN, D_MODEL), 1.0),
        "c_ctx": nrm(ks[3], (D_MODEL,), 1.0),
        "ada_w": nrm(ks[4], (DEPTH, D_MODEL, 6 * D_MODEL), 0.5 * D_MODEL ** -0.5),
        "ada_b": nrm(ks[5], (DEPTH, 6 * D_MODEL), 0.02),
        "norm_mix_g": gain(ks[6], (DEPTH, D_MODEL)),
        "norm_mlp_g": gain(ks[7], (DEPTH, D_MODEL)),
        "mlp_w1": nrm(ks[8], (DEPTH, D_MODEL, MLP_HIDDEN), D_MODEL ** -0.5),
        "mlp_w2": nrm(ks[9], (DEPTH, MLP_HIDDEN, D_MODEL), MLP_HIDDEN ** -0.5),
        "hy_w_in": nrm(ks[10], (n_even, D_MODEL, IN_PROJ), D_MODEL ** -0.5),
        "ssd_conv_w": nrm(ks[11], (n_even, SSD_CONV, SSD_XBC), SSD_CONV ** -0.5),
        "ssd_conv_b": nrm(ks[12], (n_even, SSD_XBC), 0.02),
        "ssd_dt_bias": dt0 + jnp.log(-jnp.expm1(-dt0)),
        "ssd_a_log": jnp.log(jax.random.uniform(ks[14], (n_even, 2, SSD_HEADS), jnp.float32, 1.0, 16.0)),
        "ssd_d": 1.0 + 0.1 * jax.random.normal(ks[15], (n_even, SSD_HEADS), jnp.float32),
        "ssd_norm_g": gain(ks[16], (n_even, SSD_D_INNER)),
        "sg_ln_g": gain(ks[17], (n_even, SG_WIDTH)),
        "sg_ln_b": nrm(ks[18], (n_even, SG_WIDTH), 0.02),
        "sg_w": nrm(ks[19], (n_even, SG_GROUPS, SG_CHUNK, SG_CHUNK), 0.5 * SG_CHUNK ** -0.5),
        "sg_b": gain(ks[20], (n_even, SG_GROUPS, SG_CHUNK)),
        "hy_w_out": nrm(ks[21], (n_even, MIX_WIDTH, D_MODEL), MIX_WIDTH ** -0.5),
        "cf_w_pw1": nrm(ks[22], (n_odd, D_MODEL, 2 * CONF_WIDTH), D_MODEL ** -0.5),
        "cf_b_pw1": nrm(ks[23], (n_odd, 2 * CONF_WIDTH), 0.02),
        "cf_w_dw": nrm(ks[24], (n_odd, CONF_CONV_WIDTH, CONF_WIDTH), CONF_CONV_WIDTH ** -0.5),
        "cf_b_dw": nrm(ks[25], (n_odd, CONF_WIDTH), 0.02),
        "cf_ln_g": gain(ks[26], (n_odd, CONF_WIDTH)),
        "cf_ln_b": nrm(ks[27], (n_odd, CONF_WIDTH), 0.02),
        "cf_w_pw2": nrm(ks[28], (n_odd, CONF_WIDTH, D_MODEL), CONF_WIDTH ** -0.5),
        "cf_b_pw2": nrm(ks[29], (n_odd, D_MODEL), 0.02),
        "final_norm_g": gain(ks[30], (D_MODEL,)),
    }


def _fwd_reference(x, c, ctx, c_ctx, ada_w, ada_b, norm_mix_g, norm_mlp_g, mlp_w1, mlp_w2,
              hy_w_in, ssd_conv_w, ssd_conv_b, ssd_dt_bias, ssd_a_log, ssd_d, ssd_norm_g,
              sg_ln_g, sg_ln_b, sg_w, sg_b, hy_w_out, cf_w_pw1, cf_b_pw1, cf_w_dw, cf_b_dw,
              cf_ln_g, cf_ln_b, cf_w_pw2, cf_b_pw2, final_norm_g):
    bsz = x.shape[0]
    rows = x.shape[1] // GRID_W
    lat_chunks = rows // ROWS_PER_CHUNK
    ctx_chunks = ctx.shape[1] // SG_CHUNK
    silu_c = jax.nn.silu(c)[:, None, :]
    silu_cc = jax.nn.silu(c_ctx)[None, None, :]
    h, hc = x, ctx
    for i in range(DEPTH):
        ctx_needed = i < DEPTH - 1
        sh1, sc1, g1, sh2, sc2, g2 = jnp.split(silu_c @ ada_w[i] + ada_b[i], 6, axis=-1)
        csh1, csc1, cg1, csh2, csc2, cg2 = jnp.split(silu_cc @ ada_w[i] + ada_b[i], 6, axis=-1)
        j = i // 2
        if i % 2 == 0:
            p = (hy_w_in[j], ssd_conv_w[j], ssd_conv_b[j], ssd_dt_bias[j], ssd_a_log[j], ssd_d[j],
                 ssd_norm_g[j], sg_ln_g[j], sg_ln_b[j], sg_w[j], sg_b[j], hy_w_out[j])
            zeros = jnp.zeros((bsz, SSD_GROUPS, SSD_HEADS // SSD_GROUPS, SSD_HEAD_DIM, SSD_STATE),
                              jnp.float32)
            ctx_mix, h0_f, h0_b = hybrid_mixer(modulate(hc, norm_mix_g[i], csh1, csc1),
                                               ctx_chunks, zeros, zeros, *p)
            lat_mix, _, _ = hybrid_mixer(modulate(h, norm_mix_g[i], sh1, sc1),
                                         lat_chunks, h0_f, h0_b, *p)
            h = h + g1 * lat_mix
            if ctx_needed:
                hc = hc + cg1 * ctx_mix
        else:
            p = (cf_w_pw1[j], cf_b_pw1[j], cf_w_dw[j], cf_b_dw[j], cf_ln_g[j], cf_ln_b[j],
                 cf_w_pw2[j], cf_b_pw2[j])
            h = h + g1 * conformer_conv(modulate(h, norm_mix_g[i], sh1, sc1), *p)
            if ctx_needed:
                hc = hc + cg1 * conformer_conv(modulate(hc, norm_mix_g[i], csh1, csc1), *p)
        h = h + g2 * sq_relu_mlp(modulate(h, norm_mlp_g[i], sh2, sc2), mlp_w1[i], mlp_w2[i])
        if ctx_needed:
            hc = hc + cg2 * sq_relu_mlp(modulate(hc, norm_mlp_g[i], csh2, csc2), mlp_w1[i], mlp_w2[i])
    return rmsnorm(h, final_norm_g)


import jax as _jax
import jax.numpy as _jnp

TWIN_FORMAT = 'train_step'
FWD_PARAMS = ['x', 'c', 'ctx', 'c_ctx', 'ada_w', 'ada_b', 'norm_mix_g', 'norm_mlp_g', 'mlp_w1', 'mlp_w2', 'hy_w_in', 'ssd_conv_w', 'ssd_conv_b', 'ssd_dt_bias', 'ssd_a_log', 'ssd_d', 'ssd_norm_g', 'sg_ln_g', 'sg_ln_b', 'sg_w', 'sg_b', 'hy_w_out', 'cf_w_pw1', 'cf_b_pw1', 'cf_w_dw', 'cf_b_dw', 'cf_ln_g', 'cf_ln_b', 'cf_w_pw2', 'cf_b_pw2', 'final_norm_g']
TWIN_WEIGHTS = ['c_ctx', 'ada_w', 'ada_b', 'norm_mix_g', 'norm_mlp_g', 'mlp_w1', 'mlp_w2', 'hy_w_in', 'ssd_conv_w', 'ssd_conv_b', 'ssd_dt_bias', 'ssd_a_log', 'ssd_d', 'ssd_norm_g', 'sg_ln_g', 'sg_ln_b', 'sg_w', 'sg_b', 'hy_w_out', 'cf_w_pw1', 'cf_b_pw1', 'cf_w_dw', 'cf_b_dw', 'cf_ln_g', 'cf_ln_b', 'cf_w_pw2', 'cf_b_pw2', 'final_norm_g']
TWIN_DIFF_INPUT = 'x'
TWIN_INPUTS = ['x', 'c', 'ctx', 'c_ctx', 'ada_w', 'ada_b', 'norm_mix_g', 'norm_mlp_g', 'mlp_w1', 'mlp_w2', 'hy_w_in', 'ssd_conv_w', 'ssd_conv_b', 'ssd_dt_bias', 'ssd_a_log', 'ssd_d', 'ssd_norm_g', 'sg_ln_g', 'sg_ln_b', 'sg_w', 'sg_b', 'hy_w_out', 'cf_w_pw1', 'cf_b_pw1', 'cf_w_dw', 'cf_b_dw', 'cf_ln_g', 'cf_ln_b', 'cf_w_pw2', 'cf_b_pw2', 'final_norm_g', 'loss_target', 'm_c_ctx', 'm_ada_w', 'm_ada_b', 'm_norm_mix_g', 'm_norm_mlp_g', 'm_mlp_w1', 'm_mlp_w2', 'm_hy_w_in', 'm_ssd_conv_w', 'm_ssd_conv_b', 'm_ssd_dt_bias', 'm_ssd_a_log', 'm_ssd_d', 'm_ssd_norm_g', 'm_sg_ln_g', 'm_sg_ln_b', 'm_sg_w', 'm_sg_b', 'm_hy_w_out', 'm_cf_w_pw1', 'm_cf_b_pw1', 'm_cf_w_dw', 'm_cf_b_dw', 'm_cf_ln_g', 'm_cf_ln_b', 'm_cf_w_pw2', 'm_cf_b_pw2', 'm_final_norm_g', 'v_c_ctx', 'v_ada_w', 'v_ada_b', 'v_norm_mix_g', 'v_norm_mlp_g', 'v_mlp_w1', 'v_mlp_w2', 'v_hy_w_in', 'v_ssd_conv_w', 'v_ssd_conv_b', 'v_ssd_dt_bias', 'v_ssd_a_log', 'v_ssd_d', 'v_ssd_norm_g', 'v_sg_ln_g', 'v_sg_ln_b', 'v_sg_w', 'v_sg_b', 'v_hy_w_out', 'v_cf_w_pw1', 'v_cf_b_pw1', 'v_cf_w_dw', 'v_cf_b_dw', 'v_cf_ln_g', 'v_cf_ln_b', 'v_cf_w_pw2', 'v_cf_b_pw2', 'v_final_norm_g']
TWIN_OUTPUTS = ['loss', 'grad_x', 'grad_c_ctx', 'grad_ada_w', 'grad_ada_b', 'grad_norm_mix_g', 'grad_norm_mlp_g', 'grad_mlp_w1', 'grad_mlp_w2', 'grad_hy_w_in', 'grad_ssd_conv_w', 'grad_ssd_conv_b', 'grad_ssd_dt_bias', 'grad_ssd_a_log', 'grad_ssd_d', 'grad_ssd_norm_g', 'grad_sg_ln_g', 'grad_sg_ln_b', 'grad_sg_w', 'grad_sg_b', 'grad_hy_w_out', 'grad_cf_w_pw1', 'grad_cf_b_pw1', 'grad_cf_w_dw', 'grad_cf_b_dw', 'grad_cf_ln_g', 'grad_cf_ln_b', 'grad_cf_w_pw2', 'grad_cf_b_pw2', 'grad_final_norm_g', 'delta_c_ctx', 'delta_ada_w', 'delta_ada_b', 'delta_norm_mix_g', 'delta_norm_mlp_g', 'delta_mlp_w1', 'delta_mlp_w2', 'delta_hy_w_in', 'delta_ssd_conv_w', 'delta_ssd_conv_b', 'delta_ssd_dt_bias', 'delta_ssd_a_log', 'delta_ssd_d', 'delta_ssd_norm_g', 'delta_sg_ln_g', 'delta_sg_ln_b', 'delta_sg_w', 'delta_sg_b', 'delta_hy_w_out', 'delta_cf_w_pw1', 'delta_cf_b_pw1', 'delta_cf_w_dw', 'delta_cf_b_dw', 'delta_cf_ln_g', 'delta_cf_ln_b', 'delta_cf_w_pw2', 'delta_cf_b_pw2', 'delta_final_norm_g', 'new_m_c_ctx', 'new_m_ada_w', 'new_m_ada_b', 'new_m_norm_mix_g', 'new_m_norm_mlp_g', 'new_m_mlp_w1', 'new_m_mlp_w2', 'new_m_hy_w_in', 'new_m_ssd_conv_w', 'new_m_ssd_conv_b', 'new_m_ssd_dt_bias', 'new_m_ssd_a_log', 'new_m_ssd_d', 'new_m_ssd_norm_g', 'new_m_sg_ln_g', 'new_m_sg_ln_b', 'new_m_sg_w', 'new_m_sg_b', 'new_m_hy_w_out', 'new_m_cf_w_pw1', 'new_m_cf_b_pw1', 'new_m_cf_w_dw', 'new_m_cf_b_dw', 'new_m_cf_ln_g', 'new_m_cf_ln_b', 'new_m_cf_w_pw2', 'new_m_cf_b_pw2', 'new_m_final_norm_g', 'new_v_c_ctx', 'new_v_ada_w', 'new_v_ada_b', 'new_v_norm_mix_g', 'new_v_norm_mlp_g', 'new_v_mlp_w1', 'new_v_mlp_w2', 'new_v_hy_w_in', 'new_v_ssd_conv_w', 'new_v_ssd_conv_b', 'new_v_ssd_dt_bias', 'new_v_ssd_a_log', 'new_v_ssd_d', 'new_v_ssd_norm_g', 'new_v_sg_ln_g', 'new_v_sg_ln_b', 'new_v_sg_w', 'new_v_sg_b', 'new_v_hy_w_out', 'new_v_cf_w_pw1', 'new_v_cf_b_pw1', 'new_v_cf_w_dw', 'new_v_cf_b_dw', 'new_v_cf_ln_g', 'new_v_cf_ln_b', 'new_v_cf_w_pw2', 'new_v_cf_b_pw2', 'new_v_final_norm_g']
TWIN_LEAF_KINDS = {'loss': 'loss', 'grad_x': 'grad_x', 'grad_c_ctx': 'grad_w', 'grad_ada_w': 'grad_w', 'grad_ada_b': 'grad_w', 'grad_norm_mix_g': 'grad_w', 'grad_norm_mlp_g': 'grad_w', 'grad_mlp_w1': 'grad_w', 'grad_mlp_w2': 'grad_w', 'grad_hy_w_in': 'grad_w', 'grad_ssd_conv_w': 'grad_w', 'grad_ssd_conv_b': 'grad_w', 'grad_ssd_dt_bias': 'grad_w', 'grad_ssd_a_log': 'grad_w', 'grad_ssd_d': 'grad_w', 'grad_ssd_norm_g': 'grad_w', 'grad_sg_ln_g': 'grad_w', 'grad_sg_ln_b': 'grad_w', 'grad_sg_w': 'grad_w', 'grad_sg_b': 'grad_w', 'grad_hy_w_out': 'grad_w', 'grad_cf_w_pw1': 'grad_w', 'grad_cf_b_pw1': 'grad_w', 'grad_cf_w_dw': 'grad_w', 'grad_cf_b_dw': 'grad_w', 'grad_cf_ln_g': 'grad_w', 'grad_cf_ln_b': 'grad_w', 'grad_cf_w_pw2': 'grad_w', 'grad_cf_b_pw2': 'grad_w', 'grad_final_norm_g': 'grad_w', 'delta_c_ctx': 'delta_w', 'delta_ada_w': 'delta_w', 'delta_ada_b': 'delta_w', 'delta_norm_mix_g': 'delta_w', 'delta_norm_mlp_g': 'delta_w', 'delta_mlp_w1': 'delta_w', 'delta_mlp_w2': 'delta_w', 'delta_hy_w_in': 'delta_w', 'delta_ssd_conv_w': 'delta_w', 'delta_ssd_conv_b': 'delta_w', 'delta_ssd_dt_bias': 'delta_w', 'delta_ssd_a_log': 'delta_w', 'delta_ssd_d': 'delta_w', 'delta_ssd_norm_g': 'delta_w', 'delta_sg_ln_g': 'delta_w', 'delta_sg_ln_b': 'delta_w', 'delta_sg_w': 'delta_w', 'delta_sg_b': 'delta_w', 'delta_hy_w_out': 'delta_w', 'delta_cf_w_pw1': 'delta_w', 'delta_cf_b_pw1': 'delta_w', 'delta_cf_w_dw': 'delta_w', 'delta_cf_b_dw': 'delta_w', 'delta_cf_ln_g': 'delta_w', 'delta_cf_ln_b': 'delta_w', 'delta_cf_w_pw2': 'delta_w', 'delta_cf_b_pw2': 'delta_w', 'delta_final_norm_g': 'delta_w', 'new_m_c_ctx': 'new_m', 'new_m_ada_w': 'new_m', 'new_m_ada_b': 'new_m', 'new_m_norm_mix_g': 'new_m', 'new_m_norm_mlp_g': 'new_m', 'new_m_mlp_w1': 'new_m', 'new_m_mlp_w2': 'new_m', 'new_m_hy_w_in': 'new_m', 'new_m_ssd_conv_w': 'new_m', 'new_m_ssd_conv_b': 'new_m', 'new_m_ssd_dt_bias': 'new_m', 'new_m_ssd_a_log': 'new_m', 'new_m_ssd_d': 'new_m', 'new_m_ssd_norm_g': 'new_m', 'new_m_sg_ln_g': 'new_m', 'new_m_sg_ln_b': 'new_m', 'new_m_sg_w': 'new_m', 'new_m_sg_b': 'new_m', 'new_m_hy_w_out': 'new_m', 'new_m_cf_w_pw1': 'new_m', 'new_m_cf_b_pw1': 'new_m', 'new_m_cf_w_dw': 'new_m', 'new_m_cf_b_dw': 'new_m', 'new_m_cf_ln_g': 'new_m', 'new_m_cf_ln_b': 'new_m', 'new_m_cf_w_pw2': 'new_m', 'new_m_cf_b_pw2': 'new_m', 'new_m_final_norm_g': 'new_m', 'new_v_c_ctx': 'new_v', 'new_v_ada_w': 'new_v', 'new_v_ada_b': 'new_v', 'new_v_norm_mix_g': 'new_v', 'new_v_norm_mlp_g': 'new_v', 'new_v_mlp_w1': 'new_v', 'new_v_mlp_w2': 'new_v', 'new_v_hy_w_in': 'new_v', 'new_v_ssd_conv_w': 'new_v', 'new_v_ssd_conv_b': 'new_v', 'new_v_ssd_dt_bias': 'new_v', 'new_v_ssd_a_log': 'new_v', 'new_v_ssd_d': 'new_v', 'new_v_ssd_norm_g': 'new_v', 'new_v_sg_ln_g': 'new_v', 'new_v_sg_ln_b': 'new_v', 'new_v_sg_w': 'new_v', 'new_v_sg_b': 'new_v', 'new_v_hy_w_out': 'new_v', 'new_v_cf_w_pw1': 'new_v', 'new_v_cf_b_pw1': 'new_v', 'new_v_cf_w_dw': 'new_v', 'new_v_cf_b_dw': 'new_v', 'new_v_cf_ln_g': 'new_v', 'new_v_cf_ln_b': 'new_v', 'new_v_cf_w_pw2': 'new_v', 'new_v_cf_b_pw2': 'new_v', 'new_v_final_norm_g': 'new_v'}


def _forward(args):
    return _fwd_reference(*[args[k] for k in FWD_PARAMS])


def _output_shape():
    def fwd():
        inp = _fwd_setup_inputs(0)
        return _fwd_reference(*[inp[k] for k in FWD_PARAMS])
    out = _jax.eval_shape(fwd)
    return out.shape, out.dtype

N_MICROBATCH = 1
ADAM_LR = 0.001
ADAM_B1 = 0.9
ADAM_B2 = 0.999
ADAM_EPS = 1e-08
ADAM_WD = 0.01
ADAM_STEP = 10
PER_EXAMPLE_BATCH_AXIS = {'x': 0, 'c': 0, 'ctx': 0, 'loss_target': 0}
SHARED_INPUTS = []
_WEIGHT_DTYPES = {'c_ctx': _jnp.float32, 'ada_w': _jnp.float32, 'ada_b': _jnp.float32, 'norm_mix_g': _jnp.float32, 'norm_mlp_g': _jnp.float32, 'mlp_w1': _jnp.float32, 'mlp_w2': _jnp.float32, 'hy_w_in': _jnp.float32, 'ssd_conv_w': _jnp.float32, 'ssd_conv_b': _jnp.float32, 'ssd_dt_bias': _jnp.float32, 'ssd_a_log': _jnp.float32, 'ssd_d': _jnp.float32, 'ssd_norm_g': _jnp.float32, 'sg_ln_g': _jnp.float32, 'sg_ln_b': _jnp.float32, 'sg_w': _jnp.float32, 'sg_b': _jnp.float32, 'hy_w_out': _jnp.float32, 'cf_w_pw1': _jnp.float32, 'cf_b_pw1': _jnp.float32, 'cf_w_dw': _jnp.float32, 'cf_b_dw': _jnp.float32, 'cf_ln_g': _jnp.float32, 'cf_ln_b': _jnp.float32, 'cf_w_pw2': _jnp.float32, 'cf_b_pw2': _jnp.float32, 'final_norm_g': _jnp.float32}
MOMENT_SCALE = {'c_ctx': 3.352720e-03, 'ada_w': 2.019452e-01, 'ada_b': 4.439824e-01, 'norm_mix_g': 7.180389e-02, 'norm_mlp_g': 1.079204e-01, 'mlp_w1': 5.816291e-02, 'mlp_w2': 1.381328e-01, 'hy_w_in': 4.612203e-02, 'ssd_conv_w': 4.738632e-02, 'ssd_conv_b': 6.754710e-02, 'ssd_dt_bias': 1.005007e-01, 'ssd_a_log': 1.218985e-01, 'ssd_d': 1.677698e-01, 'ssd_norm_g': 6.421562e-02, 'sg_ln_g': 1.907958e-02, 'sg_ln_b': 1.894257e-02, 'sg_w': 3.873832e-02, 'sg_b': 3.955678e-02, 'hy_w_out': 7.952490e-02, 'cf_w_pw1': 3.398007e-02, 'cf_b_pw1': 4.684224e-02, 'cf_w_dw': 4.345950e-02, 'cf_b_dw': 1.189305e-01, 'cf_ln_g': 6.018453e-02, 'cf_ln_b': 7.307258e-02, 'cf_w_pw2': 4.844668e-02, 'cf_b_pw2': 1.347713e-01, 'final_norm_g': 6.440106e+01}


def _to_microbatches(a, axis):
    t = _jnp.moveaxis(a, axis, 0)
    t = t.reshape((N_MICROBATCH, t.shape[0] // N_MICROBATCH) + t.shape[1:])
    return _jnp.moveaxis(t, 1, axis + 1)


def setup_inputs(seed: int = 0) -> dict:
    inp = _fwd_setup_inputs(seed)
    key = _jax.random.fold_in(_jax.random.key(seed), 7919)
    shape, _ = _output_shape()
    out = dict(inp)
    out["loss_target"] = _jax.random.normal(_jax.random.fold_in(key, 0), shape, _jnp.float32)
    for i, name in enumerate(TWIN_WEIGHTS):
        w = inp[name].astype(_jnp.float32)
        if MOMENT_SCALE is None:
            s = _jnp.sqrt(_jnp.mean(_jnp.square(w)) + 1e-30)
        else:
            s = MOMENT_SCALE[name]
        km, kv = _jax.random.split(_jax.random.fold_in(key, i + 1))
        out[name] = w
        out["m_" + name] = s * _jax.random.normal(km, w.shape, _jnp.float32)
        out["v_" + name] = (s * s) * _jax.random.uniform(kv, w.shape, _jnp.float32, 0.5, 1.5)
    if N_MICROBATCH > 1:
        for name, axis in PER_EXAMPLE_BATCH_AXIS.items():
            out[name] = _to_microbatches(out[name], axis)
    return {'x': out['x'], 'c': out['c'], 'ctx': out['ctx'], 'c_ctx': out['c_ctx'], 'ada_w': out['ada_w'], 'ada_b': out['ada_b'], 'norm_mix_g': out['norm_mix_g'], 'norm_mlp_g': out['norm_mlp_g'], 'mlp_w1': out['mlp_w1'], 'mlp_w2': out['mlp_w2'], 'hy_w_in': out['hy_w_in'], 'ssd_conv_w': out['ssd_conv_w'], 'ssd_conv_b': out['ssd_conv_b'], 'ssd_dt_bias': out['ssd_dt_bias'], 'ssd_a_log': out['ssd_a_log'], 'ssd_d': out['ssd_d'], 'ssd_norm_g': out['ssd_norm_g'], 'sg_ln_g': out['sg_ln_g'], 'sg_ln_b': out['sg_ln_b'], 'sg_w': out['sg_w'], 'sg_b': out['sg_b'], 'hy_w_out': out['hy_w_out'], 'cf_w_pw1': out['cf_w_pw1'], 'cf_b_pw1': out['cf_b_pw1'], 'cf_w_dw': out['cf_w_dw'], 'cf_b_dw': out['cf_b_dw'], 'cf_ln_g': out['cf_ln_g'], 'cf_ln_b': out['cf_ln_b'], 'cf_w_pw2': out['cf_w_pw2'], 'cf_b_pw2': out['cf_b_pw2'], 'final_norm_g': out['final_norm_g'], 'loss_target': out['loss_target'], 'm_c_ctx': out['m_c_ctx'], 'm_ada_w': out['m_ada_w'], 'm_ada_b': out['m_ada_b'], 'm_norm_mix_g': out['m_norm_mix_g'], 'm_norm_mlp_g': out['m_norm_mlp_g'], 'm_mlp_w1': out['m_mlp_w1'], 'm_mlp_w2': out['m_mlp_w2'], 'm_hy_w_in': out['m_hy_w_in'], 'm_ssd_conv_w': out['m_ssd_conv_w'], 'm_ssd_conv_b': out['m_ssd_conv_b'], 'm_ssd_dt_bias': out['m_ssd_dt_bias'], 'm_ssd_a_log': out['m_ssd_a_log'], 'm_ssd_d': out['m_ssd_d'], 'm_ssd_norm_g': out['m_ssd_norm_g'], 'm_sg_ln_g': out['m_sg_ln_g'], 'm_sg_ln_b': out['m_sg_ln_b'], 'm_sg_w': out['m_sg_w'], 'm_sg_b': out['m_sg_b'], 'm_hy_w_out': out['m_hy_w_out'], 'm_cf_w_pw1': out['m_cf_w_pw1'], 'm_cf_b_pw1': out['m_cf_b_pw1'], 'm_cf_w_dw': out['m_cf_w_dw'], 'm_cf_b_dw': out['m_cf_b_dw'], 'm_cf_ln_g': out['m_cf_ln_g'], 'm_cf_ln_b': out['m_cf_ln_b'], 'm_cf_w_pw2': out['m_cf_w_pw2'], 'm_cf_b_pw2': out['m_cf_b_pw2'], 'm_final_norm_g': out['m_final_norm_g'], 'v_c_ctx': out['v_c_ctx'], 'v_ada_w': out['v_ada_w'], 'v_ada_b': out['v_ada_b'], 'v_norm_mix_g': out['v_norm_mix_g'], 'v_norm_mlp_g': out['v_norm_mlp_g'], 'v_mlp_w1': out['v_mlp_w1'], 'v_mlp_w2': out['v_mlp_w2'], 'v_hy_w_in': out['v_hy_w_in'], 'v_ssd_conv_w': out['v_ssd_conv_w'], 'v_ssd_conv_b': out['v_ssd_conv_b'], 'v_ssd_dt_bias': out['v_ssd_dt_bias'], 'v_ssd_a_log': out['v_ssd_a_log'], 'v_ssd_d': out['v_ssd_d'], 'v_ssd_norm_g': out['v_ssd_norm_g'], 'v_sg_ln_g': out['v_sg_ln_g'], 'v_sg_ln_b': out['v_sg_ln_b'], 'v_sg_w': out['v_sg_w'], 'v_sg_b': out['v_sg_b'], 'v_hy_w_out': out['v_hy_w_out'], 'v_cf_w_pw1': out['v_cf_w_pw1'], 'v_cf_b_pw1': out['v_cf_b_pw1'], 'v_cf_w_dw': out['v_cf_w_dw'], 'v_cf_b_dw': out['v_cf_b_dw'], 'v_cf_ln_g': out['v_cf_ln_g'], 'v_cf_ln_b': out['v_cf_ln_b'], 'v_cf_w_pw2': out['v_cf_w_pw2'], 'v_cf_b_pw2': out['v_cf_b_pw2'], 'v_final_norm_g': out['v_final_norm_g']}


def _loss(weights, diff, rest, loss_target):
    with _jax.named_scope("forward"):
        args = {**rest, TWIN_DIFF_INPUT: diff, **{k: w.astype(_WEIGHT_DTYPES[k]) for k, w in weights.items()}}
        y = _forward(args)
    with _jax.named_scope("loss_head"):
        err = _jnp.square(y.astype(_jnp.float32) - loss_target)
        return 0.5 * _jnp.sum(_jnp.mean(err, axis=-1)) if err.ndim else 0.5 * err


def _adamw(w, g, m, v):
    m = ADAM_B1 * m + (1.0 - ADAM_B1) * g
    v = ADAM_B2 * v + (1.0 - ADAM_B2) * _jnp.square(g)
    m_hat = m / (1.0 - ADAM_B1 ** ADAM_STEP)
    v_hat = v / (1.0 - ADAM_B2 ** ADAM_STEP)
    delta = -ADAM_LR * (m_hat / (_jnp.sqrt(v_hat) + ADAM_EPS) + ADAM_WD * w)
    return delta, m, v


def reference(x, c, ctx, c_ctx, ada_w, ada_b, norm_mix_g, norm_mlp_g, mlp_w1, mlp_w2, hy_w_in, ssd_conv_w, ssd_conv_b, ssd_dt_bias, ssd_a_log, ssd_d, ssd_norm_g, sg_ln_g, sg_ln_b, sg_w, sg_b, hy_w_out, cf_w_pw1, cf_b_pw1, cf_w_dw, cf_b_dw, cf_ln_g, cf_ln_b, cf_w_pw2, cf_b_pw2, final_norm_g, loss_target, m_c_ctx, m_ada_w, m_ada_b, m_norm_mix_g, m_norm_mlp_g, m_mlp_w1, m_mlp_w2, m_hy_w_in, m_ssd_conv_w, m_ssd_conv_b, m_ssd_dt_bias, m_ssd_a_log, m_ssd_d, m_ssd_norm_g, m_sg_ln_g, m_sg_ln_b, m_sg_w, m_sg_b, m_hy_w_out, m_cf_w_pw1, m_cf_b_pw1, m_cf_w_dw, m_cf_b_dw, m_cf_ln_g, m_cf_ln_b, m_cf_w_pw2, m_cf_b_pw2, m_final_norm_g, v_c_ctx, v_ada_w, v_ada_b, v_norm_mix_g, v_norm_mlp_g, v_mlp_w1, v_mlp_w2, v_hy_w_in, v_ssd_conv_w, v_ssd_conv_b, v_ssd_dt_bias, v_ssd_a_log, v_ssd_d, v_ssd_norm_g, v_sg_ln_g, v_sg_ln_b, v_sg_w, v_sg_b, v_hy_w_out, v_cf_w_pw1, v_cf_b_pw1, v_cf_w_dw, v_cf_b_dw, v_cf_ln_g, v_cf_ln_b, v_cf_w_pw2, v_cf_b_pw2, v_final_norm_g):
    given = dict(x=x, c=c, ctx=ctx, c_ctx=c_ctx, ada_w=ada_w, ada_b=ada_b, norm_mix_g=norm_mix_g, norm_mlp_g=norm_mlp_g, mlp_w1=mlp_w1, mlp_w2=mlp_w2, hy_w_in=hy_w_in, ssd_conv_w=ssd_conv_w, ssd_conv_b=ssd_conv_b, ssd_dt_bias=ssd_dt_bias, ssd_a_log=ssd_a_log, ssd_d=ssd_d, ssd_norm_g=ssd_norm_g, sg_ln_g=sg_ln_g, sg_ln_b=sg_ln_b, sg_w=sg_w, sg_b=sg_b, hy_w_out=hy_w_out, cf_w_pw1=cf_w_pw1, cf_b_pw1=cf_b_pw1, cf_w_dw=cf_w_dw, cf_b_dw=cf_b_dw, cf_ln_g=cf_ln_g, cf_ln_b=cf_ln_b, cf_w_pw2=cf_w_pw2, cf_b_pw2=cf_b_pw2, final_norm_g=final_norm_g, loss_target=loss_target, m_c_ctx=m_c_ctx, m_ada_w=m_ada_w, m_ada_b=m_ada_b, m_norm_mix_g=m_norm_mix_g, m_norm_mlp_g=m_norm_mlp_g, m_mlp_w1=m_mlp_w1, m_mlp_w2=m_mlp_w2, m_hy_w_in=m_hy_w_in, m_ssd_conv_w=m_ssd_conv_w, m_ssd_conv_b=m_ssd_conv_b, m_ssd_dt_bias=m_ssd_dt_bias, m_ssd_a_log=m_ssd_a_log, m_ssd_d=m_ssd_d, m_ssd_norm_g=m_ssd_norm_g, m_sg_ln_g=m_sg_ln_g, m_sg_ln_b=m_sg_ln_b, m_sg_w=m_sg_w, m_sg_b=m_sg_b, m_hy_w_out=m_hy_w_out, m_cf_w_pw1=m_cf_w_pw1, m_cf_b_pw1=m_cf_b_pw1, m_cf_w_dw=m_cf_w_dw, m_cf_b_dw=m_cf_b_dw, m_cf_ln_g=m_cf_ln_g, m_cf_ln_b=m_cf_ln_b, m_cf_w_pw2=m_cf_w_pw2, m_cf_b_pw2=m_cf_b_pw2, m_final_norm_g=m_final_norm_g, v_c_ctx=v_c_ctx, v_ada_w=v_ada_w, v_ada_b=v_ada_b, v_norm_mix_g=v_norm_mix_g, v_norm_mlp_g=v_norm_mlp_g, v_mlp_w1=v_mlp_w1, v_mlp_w2=v_mlp_w2, v_hy_w_in=v_hy_w_in, v_ssd_conv_w=v_ssd_conv_w, v_ssd_conv_b=v_ssd_conv_b, v_ssd_dt_bias=v_ssd_dt_bias, v_ssd_a_log=v_ssd_a_log, v_ssd_d=v_ssd_d, v_ssd_norm_g=v_ssd_norm_g, v_sg_ln_g=v_sg_ln_g, v_sg_ln_b=v_sg_ln_b, v_sg_w=v_sg_w, v_sg_b=v_sg_b, v_hy_w_out=v_hy_w_out, v_cf_w_pw1=v_cf_w_pw1, v_cf_b_pw1=v_cf_b_pw1, v_cf_w_dw=v_cf_w_dw, v_cf_b_dw=v_cf_b_dw, v_cf_ln_g=v_cf_ln_g, v_cf_ln_b=v_cf_ln_b, v_cf_w_pw2=v_cf_w_pw2, v_cf_b_pw2=v_cf_b_pw2, v_final_norm_g=v_final_norm_g)
    weights = {n: given[n] for n in TWIN_WEIGHTS}
    shared = {n: given[n] for n in SHARED_INPUTS}
    per_example = {n: given[n] for n in ['x', 'c', 'ctx']}
    grad_fn = _jax.value_and_grad(_loss, argnums=(0, 1))

    def one_microbatch(ex, loss_target):
        ex = dict(ex)
        diff = ex.pop(TWIN_DIFF_INPUT)
        return grad_fn(weights, diff, {**shared, **ex}, loss_target)

    if N_MICROBATCH == 1:
        loss, (grad_w, grad_x) = one_microbatch(per_example, given["loss_target"])
    else:
        def body(carry, xs):
            loss_sum, grad_sum = carry
            l_k, (gw_k, gx_k) = one_microbatch(xs[0], xs[1])
            with _jax.named_scope("update"):
                return (loss_sum + l_k, _jax.tree.map(_jnp.add, grad_sum, gw_k)), gx_k

        init = (_jnp.zeros((), _jnp.float32), _jax.tree.map(_jnp.zeros_like, weights))
        (loss, grad_w), grad_x = _jax.lax.scan(body, init, (per_example, given["loss_target"]))
    with _jax.named_scope("update"):
        delta_w, new_m, new_v = {}, {}, {}
        for n in TWIN_WEIGHTS:
            delta_w[n], new_m[n], new_v[n] = _adamw(weights[n], grad_w[n], given["m_" + n], given["v_" + n])
    return (loss, grad_x, *[grad_w[n] for n in TWIN_WEIGHTS], *[delta_w[n] for n in TWIN_WEIGHTS],
            *[new_m[n] for n in TWIN_WEIGHTS], *[new_v[n] for n in TWIN_WEIGHTS])
```

```python
import functools

import jax
import jax.numpy as jnp
from jax import lax
from jax.experimental import pallas as pl
from jax.experimental.pallas import tpu as pltpu

F32 = jnp.float32
MXU = jnp.bfloat16
ACT = jnp.bfloat16
HI = lax.Precision.HIGHEST
EPS = 1e-6

D = 1024
HID = 4096
XBC = 1536
NPAIR = 8
Q = 128
SGG = 8
CK = 31
SK = 5
T = 256
LANE = 128
VMEM_LIMIT = 56 * 1024 * 1024

ADAM_LR, ADAM_B1, ADAM_B2, ADAM_EPS, ADAM_WD, ADAM_STEP = 0.001, 0.9, 0.999, 1e-08, 0.01, 10


def _call(body, name, grid, in_specs, out_specs, out_shape, scratch=()):
    return pl.pallas_call(
        body, name=name, grid=grid, in_specs=in_specs, out_specs=out_specs, out_shape=out_shape,
        scratch_shapes=list(scratch),
        compiler_params=pltpu.CompilerParams(dimension_semantics=("arbitrary",) * len(grid),
                                             vmem_limit_bytes=VMEM_LIMIT))


def _sds(shape, dt=F32):
    return jax.ShapeDtypeStruct(tuple(shape), dt)


def _rows(t, w, off=0, lane_blk=0):
    return pl.BlockSpec((t, w), lambda i: (i + off, lane_blk))


def _rows_lat(t, w, nct):
    return pl.BlockSpec((t, w), lambda i: (jnp.maximum(i - nct, 0), 0))


def _full(*shape):
    return pl.BlockSpec(shape, lambda *_: (0,) * len(shape))


def _wfull(*shape):
    return pl.BlockSpec(shape, lambda *_: (0,) * len(shape), pipeline_mode=pl.Buffered(1))


def _halo(w, hb, nrows):
    r, nb = T // hb, nrows // hb
    prev = pl.BlockSpec((hb, w), lambda i: (jnp.maximum(i * r - 1, 0), 0))
    nxt = pl.BlockSpec((hb, w), lambda i: (jnp.minimum((i + 1) * r, nb - 1), 0))
    return prev, nxt


def _nn(a, b):
    return jnp.dot(a.astype(MXU), b.astype(MXU), preferred_element_type=F32)


def _nt(a, b):
    return lax.dot_general(a.astype(MXU), b.astype(MXU), (((1,), (1,)), ((), ())), preferred_element_type=F32)


def _tn_dot(a, b):
    return lax.dot_general(a.astype(MXU), b.astype(MXU), (((0,), (0,)), ((), ())), preferred_element_type=F32)


def _sum0(x):
    return jnp.sum(x, axis=0, keepdims=True)


def _silu(x):
    return x * jax.nn.sigmoid(x)


def _gelu(x):
    return jax.nn.gelu(x, approximate=True)


def _mod(h, g, sh, sc):
    n = h * lax.rsqrt(jnp.mean(h * h, axis=-1, keepdims=True) + EPS)
    return n * g * (1.0 + sc) + sh


def _ln(x, g, b):
    xc = x - jnp.mean(x, axis=-1, keepdims=True)
    return xc * lax.rsqrt(jnp.mean(xc * xc, axis=-1, keepdims=True) + EPS) * g + b


def _gate_norm(ytot, z, ng):
    yg = ytot * _silu(z)
    halves = []
    for k in range(2):
        seg = yg[:, k * 512:(k + 1) * 512]
        halves.append(seg * lax.rsqrt(jnp.mean(seg * seg, axis=-1, keepdims=True) + EPS) * ng[:, k * 512:(k + 1) * 512])
    return jnp.concatenate(halves, axis=-1)


def _fill_ext(ext_ref, prev_ref, cur_ref, next_ref, hb, first, last):
    ext_ref[0:hb, :] = jnp.where(first, 0.0, prev_ref[...])
    ext_ref[hb:hb + T, :] = cur_ref[...]
    ext_ref[hb + T:hb + T + hb, :] = jnp.where(last, 0.0, next_ref[...])


def _conv(ext_ref, w_ref, k_taps, hb, lanes):
    off = hb - k_taps // 2
    acc = ext_ref[pl.ds(off, T), lanes] * w_ref[0:1, lanes]
    for k in range(1, k_taps):
        acc = acc + ext_ref[pl.ds(off + k, T), lanes] * w_ref[k:k + 1, lanes]
    return acc


def _conv_tr(ext_ref, w_ref, k_taps, hb, lanes):
    off = hb + k_taps // 2
    acc = ext_ref[pl.ds(off, T), lanes] * w_ref[0:1, lanes]
    for k in range(1, k_taps):
        acc = acc + ext_ref[pl.ds(off - k, T), lanes] * w_ref[k:k + 1, lanes]
    return acc


def _conv_dw(dw_ref, d_ref, xext_ref, k_taps, hb, lanes):
    off = hb - k_taps // 2
    d = d_ref[:, lanes]
    for k in range(k_taps):
        dw_ref[k:k + 1, lanes] += _sum0(d * xext_ref[pl.ds(off + k, T), lanes])


def _tn(a, b, name, relu2=False):
    m_rows, ka = a.shape
    n = b.shape[1]
    tm = next(t for t in (1024, 768, 512, 256) if m_rows % t == 0)
    tk = min(ka, 1024)
    tn = n if n <= 1024 else next(t for t in (1024, 768, 512, 384, 256, 128) if n % t == 0)

    def body(a_ref, b_ref, o_ref):
        @pl.when(pl.program_id(2) == 0)
        def _():
            o_ref[...] = jnp.zeros_like(o_ref)
        av = a_ref[...]
        if relu2:
            av = jnp.square(jnp.maximum(av.astype(F32), 0.0))
        o_ref[...] += _tn_dot(av, b_ref[...])

    return _call(body, name, (ka // tk, n // tn, m_rows // tm),
                 [pl.BlockSpec((tm, tk), lambda k, j, m: (m, k)), pl.BlockSpec((tm, tn), lambda k, j, m: (m, j))],
                 pl.BlockSpec((tk, tn), lambda k, j, m: (k, j)), _sds((ka, n)))(a, b)


def _mlp_fwd(h, mp, w1, w2, name):
    n_rows = h.shape[0]

    def body(h_ref, mp_ref, w1_ref, w2_ref, hn_ref, a_ref, y_ref):
        hv = h_ref[...]
        u = _mod(hv, mp_ref[0:1], mp_ref[1:2], mp_ref[2:3]).astype(MXU)
        acc = jnp.zeros((T, D), F32)
        for j in range(HID // 1024):
            cs = slice(j * 1024, (j + 1) * 1024)
            a = jnp.dot(u, w1_ref[:, cs], preferred_element_type=F32)
            a_ref[:, cs] = a.astype(ACT)
            acc = acc + jnp.dot(jnp.square(jnp.maximum(a, 0.0)).astype(MXU), w2_ref[cs, :], preferred_element_type=F32)
        y_ref[...] = acc
        hn_ref[...] = hv + mp_ref[3:4] * acc

    return _call(body, name, (n_rows // T,),
                 [_rows(T, D), _full(8, D), _wfull(D, HID), _wfull(HID, D)],
                 [_rows(T, D), _rows(T, HID), _rows(T, D)],
                 [_sds((n_rows, D)), _sds((n_rows, HID), ACT), _sds((n_rows, D))])(h, mp, w1, w2)


def _mlp_bwd(dh, h, a, y, mp, w1, w2, name):
    n_rows = h.shape[0]

    def body(dh_ref, h_ref, a_ref, y_ref, mp_ref, w1_ref, w2_ref, dho_ref, da_ref, dyb_ref, ub_ref, pg_ref):
        dhp = dh_ref[...]
        u, vjp = jax.vjp(_mod, h_ref[...], mp_ref[0:1], mp_ref[1:2], mp_ref[2:3])
        ub_ref[...] = u.astype(ACT)
        dyb = (mp_ref[3:4] * dhp).astype(MXU)
        dyb_ref[...] = dyb.astype(ACT)
        du = jnp.zeros((T, D), F32)
        for j in range(HID // 1024):
            cs = slice(j * 1024, (j + 1) * 1024)
            dp = _nt(dyb, w2_ref[cs, :])
            da = dp * 2.0 * jnp.maximum(a_ref[:, cs].astype(F32), 0.0)
            da_ref[:, cs] = da.astype(ACT)
            du = du + _nt(da, w1_ref[:, cs])
        dhn, dg, dsh, dsc = vjp(du)
        dho_ref[...] = dhp + dhn

        @pl.when(pl.program_id(0) == 0)
        def _():
            pg_ref[...] = jnp.zeros_like(pg_ref)
        pg_ref[0:1] += dg
        pg_ref[1:2] += dsh
        pg_ref[2:3] += dsc
        pg_ref[3:4] += _sum0(dhp * y_ref[...])

    return _call(body, name, (n_rows // T,),
                 [_rows(T, D), _rows(T, D), _rows(T, HID), _rows(T, D), _full(8, D), _wfull(D, HID), _wfull(HID, D)],
                 [_rows(T, D), _rows(T, HID), _rows(T, D), _rows(T, D), _full(8, D)],
                 [_sds((n_rows, D)), _sds((n_rows, HID), ACT), _sds((n_rows, D), ACT), _sds((n_rows, D), ACT),
                  _sds((8, D))])(dh, h, a, y, mp, w1, w2)


def _cf1_fwd(h, mp, w1, b1):
    n_rows = h.shape[0]

    def body(h_ref, mp_ref, w1_ref, b1_ref, glu_ref, a_ref):
        u = _mod(h_ref[...], mp_ref[0:1], mp_ref[1:2], mp_ref[2:3]).astype(MXU)
        a = jnp.dot(u, w1_ref[...], preferred_element_type=F32) + b1_ref[...]
        a_ref[...] = a.astype(ACT)
        glu_ref[...] = a[:, :D] * jax.nn.sigmoid(a[:, D:])

    return _call(body, "cf1_fwd", (n_rows // T,),
                 [_rows(T, D), _full(8, D), _wfull(D, 2 * D), _full(1, 2 * D)],
                 [_rows(T, D), _rows(T, 2 * D)],
                 [_sds((n_rows, D)), _sds((n_rows, 2 * D), ACT)])(h, mp, w1, b1)


def _cf2_fwd(h, glu, mp, wdw, bdw, lng, lnb, w2, b2):
    n_rows = h.shape[0]
    nt = n_rows // T
    hb = 16

    def body(h_ref, gp_ref, gc_ref, gn_ref, mp_ref, wdw_ref, bdw_ref, lng_ref, lnb_ref, w2_ref, b2_ref,
             hn_ref, cv_ref, sb_ref, y_ref, ext):
        i = pl.program_id(0)
        _fill_ext(ext, gp_ref, gc_ref, gn_ref, hb, i == 0, i == nt - 1)
        for c in range(D // LANE):
            lanes = slice(c * LANE, (c + 1) * LANE)
            cv_ref[:, lanes] = _conv(ext, wdw_ref, CK, hb, lanes) + bdw_ref[:, lanes]
        s = _silu(_ln(cv_ref[...], lng_ref[...], lnb_ref[...])).astype(MXU)
        sb_ref[...] = s.astype(ACT)
        y = jnp.dot(s, w2_ref[...], preferred_element_type=F32) + b2_ref[...]
        y_ref[...] = y
        hn_ref[...] = h_ref[...] + mp_ref[3:4] * y

    gp, gn = _halo(D, hb, n_rows)
    return _call(body, "cf2_fwd", (nt,),
                 [_rows(T, D), gp, _rows(T, D), gn, _full(8, D), _full(32, D), _full(1, D), _full(1, D), _full(1, D),
                  _wfull(D, D), _full(1, D)],
                 [_rows(T, D), _rows(T, D), _rows(T, D), _rows(T, D)],
                 [_sds((n_rows, D)), _sds((n_rows, D)), _sds((n_rows, D), ACT), _sds((n_rows, D))],
                 scratch=[pltpu.VMEM((T + 2 * hb, D), F32)])(h, glu, glu, glu, mp, wdw, bdw, lng, lnb, w2, b2)


def _cf2_bwd(dh, y, cv, mp, lng, lnb, w2):
    n_rows = dh.shape[0]

    def body(dh_ref, y_ref, cv_ref, mp_ref, lng_ref, lnb_ref, w2_ref, dcv_ref, dyb_ref, pg_ref):
        dhp = dh_ref[...]
        dy = mp_ref[3:4] * dhp
        dyb = dy.astype(MXU)
        dyb_ref[...] = dyb.astype(ACT)
        ds = _nt(dyb, w2_ref[...])
        _, vjp = jax.vjp(lambda cv_, g_, b_: _silu(_ln(cv_, g_, b_)), cv_ref[...], lng_ref[...], lnb_ref[...])
        dcv, dlng, dlnb = vjp(ds)
        dcv_ref[...] = dcv

        @pl.when(pl.program_id(0) == 0)
        def _():
            pg_ref[...] = jnp.zeros_like(pg_ref)
        pg_ref[0:1] += _sum0(dhp * y_ref[...])
        pg_ref[1:2] += _sum0(dy)
        pg_ref[2:3] += dlng
        pg_ref[3:4] += dlnb
        pg_ref[4:5] += _sum0(dcv)

    return _call(body, "cf2_bwd", (n_rows // T,),
                 [_rows(T, D), _rows(T, D), _rows(T, D), _full(8, D), _full(1, D), _full(1, D), _wfull(D, D)],
                 [_rows(T, D), _rows(T, D), _full(8, D)],
                 [_sds((n_rows, D)), _sds((n_rows, D), ACT), _sds((8, D))])(dh, y, cv, mp, lng, lnb, w2)


def _cf1_bwd(dh, h, a, dcv, glu, mp, wdw, w1):
    n_rows = h.shape[0]
    nt = n_rows // T
    hb = 16

    def body(dh_ref, h_ref, a_ref, dp_ref, dc_ref, dn_ref, gp_ref, gc_ref, gn_ref, mp_ref, wdw_ref, w1_ref,
             dho_ref, da_ref, ub_ref, pg_ref, pb_ref, dw_ref, dext, gext, dglu):
        i = pl.program_id(0)

        @pl.when(i == 0)
        def _():
            pg_ref[...] = jnp.zeros_like(pg_ref)
            pb_ref[...] = jnp.zeros_like(pb_ref)
            dw_ref[...] = jnp.zeros_like(dw_ref)
        _fill_ext(dext, dp_ref, dc_ref, dn_ref, hb, i == 0, i == nt - 1)
        _fill_ext(gext, gp_ref, gc_ref, gn_ref, hb, i == 0, i == nt - 1)
        for c in range(D // LANE):
            lanes = slice(c * LANE, (c + 1) * LANE)
            dglu[:, lanes] = _conv_tr(dext, wdw_ref, CK, hb, lanes)
            _conv_dw(dw_ref, dc_ref, gext, CK, hb, lanes)
        av = a_ref[...].astype(F32)
        _, vjp_glu = jax.vjp(lambda a1, a2: a1 * jax.nn.sigmoid(a2), av[:, :D], av[:, D:])
        da1, da2 = vjp_glu(dglu[...])
        da_ref[:, :D] = da1.astype(ACT)
        da_ref[:, D:] = da2.astype(ACT)
        pb_ref[0:1, :D] += _sum0(da1)
        pb_ref[0:1, D:] += _sum0(da2)
        du = _nt(da1, w1_ref[:, :D]) + _nt(da2, w1_ref[:, D:])
        u, vjp = jax.vjp(_mod, h_ref[...], mp_ref[0:1], mp_ref[1:2], mp_ref[2:3])
        ub_ref[...] = u.astype(ACT)
        dhn, dg, dsh, dsc = vjp(du)
        dho_ref[...] = dh_ref[...] + dhn
        pg_ref[0:1] += dg
        pg_ref[1:2] += dsh
        pg_ref[2:3] += dsc

    hp, hn = _halo(D, hb, n_rows)
    return _call(body, "cf1_bwd", (nt,),
                 [_rows(T, D), _rows(T, D), _rows(T, 2 * D), hp, _rows(T, D), hn, hp, _rows(T, D), hn,
                  _full(8, D), _full(32, D), _wfull(D, 2 * D)],
                 [_rows(T, D), _rows(T, 2 * D), _rows(T, D), _full(8, D), _full(8, 2 * D), _full(32, D)],
                 [_sds((n_rows, D)), _sds((n_rows, 2 * D), ACT), _sds((n_rows, D), ACT), _sds((8, D)),
                  _sds((8, 2 * D)), _sds((32, D))],
                 scratch=[pltpu.VMEM((T + 2 * hb, D), F32), pltpu.VMEM((T + 2 * hb, D), F32), pltpu.VMEM((T, D), F32)],
                 )(dh, h, a, dcv, dcv, dcv, glu, glu, glu, mp, wdw, w1)


def _sg_blocks():
    return [(c, g, slice(c * Q, (c + 1) * Q), slice(g * LANE, (g + 1) * LANE)) for c in range(T // Q) for g in range(SGG)]


def _hy1_fwd(hcat, mp2, wz, wuv, wxbc, wdt, lng, lnb, sgw, sgbt, nct):
    n_rows = hcat.shape[0]

    def body(h_ref, mp_ref, wz_ref, wuv_ref, wxbc_ref, wdt_ref, lng_ref, lnb_ref, sgw_ref, sgbt_ref,
             z_ref, uv_ref, xbcp_ref, dtr_ref, ysg_ref):
        u = _mod(h_ref[...], mp_ref[0:1], mp_ref[1:2], mp_ref[2:3]).astype(MXU)
        z_ref[...] = jnp.dot(u, wz_ref[...], preferred_element_type=F32)
        xbcp_ref[...] = jnp.dot(u, wxbc_ref[...], preferred_element_type=F32)
        dtr_ref[...] = jnp.dot(u, wdt_ref[...], preferred_element_type=F32)
        uv = jnp.dot(u, wuv_ref[...], preferred_element_type=F32)
        uv_ref[...] = uv
        gate = _gelu(uv[:, :D])
        vln = _ln(_gelu(uv[:, D:]), lng_ref[...], lnb_ref[...]).astype(MXU)
        for _, g, rs, ls in _sg_blocks():
            s = jnp.dot(sgw_ref[g], vln[rs, ls], preferred_element_type=F32) + sgbt_ref[:, g:g + 1]
            ysg_ref[rs, ls] = (gate[rs, ls] * s).astype(ACT)

    mspec = pl.BlockSpec((None, 8, D), lambda i: (jnp.where(i < nct, 0, 1), 0, 0))
    return _call(body, "hy1_fwd", (n_rows // T,),
                 [_rows(T, D), mspec, _wfull(D, D), _wfull(D, 2 * D), _wfull(D, XBC), _wfull(D, LANE),
                  _full(1, D), _full(1, D), _full(SGG, Q, Q), _full(Q, LANE)],
                 [_rows(T, D), _rows(T, 2 * D), _rows(T, XBC), _rows(T, LANE), _rows(T, D)],
                 [_sds((n_rows, D)), _sds((n_rows, 2 * D)), _sds((n_rows, XBC)), _sds((n_rows, LANE)),
                  _sds((n_rows, D), ACT)])(hcat, mp2, wz, wuv, wxbc, wdt, lng, lnb, sgw, sgbt)


def _hy1_bwd(hcat, uv, dz, dxbcp, ddf, ddb, dysg, dres, mp2, wz, wuv, wxbc, wdt, lng, lnb, sgw, sgbt, nct):
    n_rows = hcat.shape[0]
    n_lat = dres.shape[0]

    def body(h_ref, uv_ref, dz_ref, dxbcp_ref, ddf_ref, ddb_ref, dysg_ref, dres_ref, mp_ref, wz_ref, wuv_ref,
             wxbc_ref, wdt_ref, lng_ref, lnb_ref, sgw_ref, sgbt_ref,
             dho_ref, ub_ref, duv_ref, ddt_ref, pg2_ref, pl_ref, dsgw_ref, dsgb_ref, dgate_s, dvln_s):
        i = pl.program_id(0)

        @pl.when(i == 0)
        def _():
            pg2_ref[...] = jnp.zeros_like(pg2_ref)
            pl_ref[...] = jnp.zeros_like(pl_ref)
            dsgw_ref[...] = jnp.zeros_like(dsgw_ref)
            dsgb_ref[...] = jnp.zeros_like(dsgb_ref)
        uv = uv_ref[...]

        def f_sg(ug, uvv, g_, b_):
            return _gelu(ug), _ln(_gelu(uvv), g_, b_)
        (gate, vln), vjp_sg = jax.vjp(f_sg, uv[:, :D], uv[:, D:], lng_ref[...], lnb_ref[...])
        vlnb = vln.astype(MXU)
        lane = lax.broadcasted_iota(jnp.int32, (Q, LANE), 1)
        dsgb = jnp.zeros((Q, LANE), F32)
        for _, g, rs, ls in _sg_blocks():
            s = jnp.dot(sgw_ref[g], vlnb[rs, ls], preferred_element_type=F32) + sgbt_ref[:, g:g + 1]
            dyb = dysg_ref[rs, ls]
            dgate_s[rs, ls] = dyb * s
            ds = dyb * gate[rs, ls]
            dvln_s[rs, ls] = _tn_dot(sgw_ref[g], ds)
            dsgw_ref[g] += _nt(ds, vlnb[rs, ls])
            dsgb = dsgb + jnp.where(lane == g, jnp.sum(ds, axis=1, keepdims=True), 0.0)
        dsgb_ref[...] += dsgb
        dug, duvv, dlng, dlnb = vjp_sg((dgate_s[...], dvln_s[...]))
        pl_ref[0:1] += dlng
        pl_ref[1:2] += dlnb
        duv_ref[:, :D] = dug.astype(ACT)
        duv_ref[:, D:] = duvv.astype(ACT)
        ddt = (ddf_ref[...] + ddb_ref[...]).astype(MXU)
        ddt_ref[...] = ddt.astype(ACT)
        du = (_nt(dz_ref[...], wz_ref[...]) + _nt(dug, wuv_ref[:, :D]) + _nt(duvv, wuv_ref[:, D:])
              + _nt(dxbcp_ref[...], wxbc_ref[...]) + _nt(ddt, wdt_ref[...]))
        u, vjp = jax.vjp(_mod, h_ref[...], mp_ref[0:1], mp_ref[1:2], mp_ref[2:3])
        ub_ref[...] = u.astype(ACT)
        dhn, dg, dsh, dsc = vjp(du)
        dho_ref[...] = dres_ref[...] + dhn
        is_ctx = i < nct
        for k, val in enumerate((dg, dsh, dsc)):
            pg2_ref[0, k:k + 1] += jnp.where(is_ctx, val, 0.0)
            pg2_ref[1, k:k + 1] += jnp.where(is_ctx, 0.0, val)

    mspec = pl.BlockSpec((None, 8, D), lambda i: (jnp.where(i < nct, 0, 1), 0, 0))
    return _call(body, "hy1_bwd", (n_rows // T,),
                 [_rows(T, D), _rows(T, 2 * D), _rows(T, D), _rows(T, XBC), _rows(T, LANE), _rows(T, LANE), _rows(T, D),
                  _rows_lat(T, D, nct), mspec, _wfull(D, D), _wfull(D, 2 * D), _wfull(D, XBC), _wfull(D, LANE),
                  _full(1, D), _full(1, D), _full(SGG, Q, Q), _full(Q, LANE)],
                 [_rows_lat(T, D, nct), _rows(T, D), _rows(T, 2 * D), _rows(T, LANE), _full(2, 8, D), _full(8, D),
                  _full(SGG, Q, Q), _full(Q, LANE)],
                 [_sds((n_lat, D)), _sds((n_rows, D), ACT), _sds((n_rows, 2 * D), ACT), _sds((n_rows, LANE), ACT),
                  _sds((2, 8, D)), _sds((8, D)), _sds((SGG, Q, Q)), _sds((Q, LANE))],
                 scratch=[pltpu.VMEM((T, D), F32), pltpu.VMEM((T, D), F32)],
                 )(hcat, uv, dz, dxbcp, ddf, ddb, dysg, dres, mp2, wz, wuv, wxbc, wdt, lng, lnb, sgw, sgbt)


def _seq_edges(i, nct, nt):
    return (i == 0) | (i == nct), (i == nct - 1) | (i == nt - 1)


def _cv5_fwd(xbcp, w, b, nct):
    n_rows = xbcp.shape[0]
    nt = n_rows // T
    hb = 8

    def body(p_ref, c_ref, n_ref, w_ref, b_ref, o_ref, ext):
        first, last = _seq_edges(pl.program_id(0), nct, nt)
        _fill_ext(ext, p_ref, c_ref, n_ref, hb, first, last)
        for c in range(XBC // LANE):
            lanes = slice(c * LANE, (c + 1) * LANE)
            o_ref[:, lanes] = _silu(_conv(ext, w_ref, SK, hb, lanes) + b_ref[:, lanes])

    hp, hn = _halo(XBC, hb, n_rows)
    return _call(body, "cv5_fwd", (nt,), [hp, _rows(T, XBC), hn, _full(8, XBC), _full(1, XBC)],
                 _rows(T, XBC), _sds((n_rows, XBC)), scratch=[pltpu.VMEM((T + 2 * hb, XBC), F32)])(xbcp, xbcp, xbcp, w, b)


def _cv5_bwd1(xbcp, dxf, dxb, w, b, nct):
    n_rows = xbcp.shape[0]
    nt = n_rows // T
    hb = 8

    def body(p_ref, c_ref, n_ref, dxf_ref, dxb_ref, w_ref, b_ref, o_ref, pg_ref, ext):
        i = pl.program_id(0)
        first, last = _seq_edges(i, nct, nt)
        _fill_ext(ext, p_ref, c_ref, n_ref, hb, first, last)

        @pl.when(i == 0)
        def _():
            pg_ref[...] = jnp.zeros_like(pg_ref)
        for c in range(XBC // LANE):
            lanes = slice(c * LANE, (c + 1) * LANE)
            cv = _conv(ext, w_ref, SK, hb, lanes) + b_ref[:, lanes]
            sg = jax.nn.sigmoid(cv)
            dcv = (dxf_ref[:, lanes] + dxb_ref[:, lanes]) * (sg * (1.0 + cv * (1.0 - sg)))
            o_ref[:, lanes] = dcv
            pg_ref[0:1, lanes] += _sum0(dcv)

    hp, hn = _halo(XBC, hb, n_rows)
    return _call(body, "cv5_bwd1", (nt,),
                 [hp, _rows(T, XBC), hn, _rows(T, XBC), _rows(T, XBC), _full(8, XBC), _full(1, XBC)],
                 [_rows(T, XBC), _full(8, XBC)], [_sds((n_rows, XBC)), _sds((8, XBC))],
                 scratch=[pltpu.VMEM((T + 2 * hb, XBC), F32)])(xbcp, xbcp, xbcp, dxf, dxb, w, b)


def _cv5_bwd2(dcv, xbcp, w, nct):
    n_rows = xbcp.shape[0]
    nt = n_rows // T
    hb = 8

    def body(dp_ref, dc_ref, dn_ref, xp_ref, xc_ref, xn_ref, w_ref, o_ref, dw_ref, dext, xext):
        i = pl.program_id(0)
        first, last = _seq_edges(i, nct, nt)
        _fill_ext(dext, dp_ref, dc_ref, dn_ref, hb, first, last)
        _fill_ext(xext, xp_ref, xc_ref, xn_ref, hb, first, last)

        @pl.when(i == 0)
        def _():
            dw_ref[...] = jnp.zeros_like(dw_ref)
        for c in range(XBC // LANE):
            lanes = slice(c * LANE, (c + 1) * LANE)
            o_ref[:, lanes] = _conv_tr(dext, w_ref, SK, hb, lanes).astype(ACT)
            _conv_dw(dw_ref, dc_ref, xext, SK, hb, lanes)

    hp, hn = _halo(XBC, hb, n_rows)
    return _call(body, "cv5_bwd2", (nt,),
                 [hp, _rows(T, XBC), hn, hp, _rows(T, XBC), hn, _full(8, XBC)],
                 [_rows(T, XBC), _full(8, XBC)], [_sds((n_rows, XBC), ACT), _sds((8, XBC))],
                 scratch=[pltpu.VMEM((T + 2 * hb, XBC), F32), pltpu.VMEM((T + 2 * hb, XBC), F32)],
                 )(dcv, dcv, dcv, xbcp, xbcp, xbcp, w)


def _scan_order(nc, ncc, rev):
    if not rev:
        return lambda s: s
    return lambda s: jnp.where(s < ncc, ncc - 1 - s, nc - 1 - (s - ncc))


def _ssd_prep(dtr, sp, rev):
    dt = jax.nn.softplus(dtr + sp[0:1])
    a_neg = -jnp.exp(sp[1:2])
    r = lax.broadcasted_iota(jnp.int32, (Q, Q), 0)
    c = lax.broadcasted_iota(jnp.int32, (Q, Q), 1)
    msk = (c >= r) if rev else (c <= r)
    tri = msk.astype(F32)
    acs = jnp.dot(tri, dt * a_neg, precision=HI, preferred_element_type=F32)
    last = 0 if rev else Q - 1
    return dt, a_neg, acs, msk, tri, last


def _pair_sel(arr, lo, m, lane_lt):
    h0 = lo + 2 * m
    return jnp.where(lane_lt, arr[:, h0:h0 + 1], arr[:, h0 + 1:h0 + 2])


def _ssd_fwd(xbc, dtr, sp, ncc, rev):
    n_rows = xbc.shape[0]
    nc = n_rows // Q
    lo = 16 if rev else 0
    order = _scan_order(nc, ncc, rev)

    def body(x_ref, dtr_ref, sp_ref, y_ref, hin_ref, st):
        @pl.when(pl.program_id(0) == 0)
        def _():
            st[...] = jnp.zeros_like(st)
        dt, _, acs, msk, _, last = _ssd_prep(dtr_ref[...], sp_ref[...], rev)
        acs_t, dt_t = acs.T, dt.T
        eacs = jnp.exp(acs)
        eal = jnp.exp(acs[last:last + 1, :])
        tew = jnp.exp(acs[last:last + 1, :] - acs) * dt
        lane_lt = lax.broadcasted_iota(jnp.int32, (Q, LANE), 1) < 64
        row_lt = lax.broadcasted_iota(jnp.int32, (LANE, 1), 0) < 64
        s_g = [_nt(x_ref[:, 1280 + g * 128:1408 + g * 128], x_ref[:, 1024 + g * 128:1152 + g * 128]) for g in range(2)]
        for m in range(NPAIR):
            g = m // 4
            ls = slice(m * LANE, (m + 1) * LANE)
            x2 = x_ref[:, ls]
            bg = x_ref[:, 1024 + g * 128:1152 + g * 128]
            cg = x_ref[:, 1280 + g * 128:1408 + g * 128]
            y2 = jnp.zeros((Q, LANE), F32)
            for hh in range(2):
                h = lo + 2 * m + hh
                lm = jnp.exp(jnp.where(msk, acs[:, h:h + 1] - acs_t[h:h + 1, :], -jnp.inf))
                w = s_g[g] * lm * dt_t[h:h + 1, :]
                y2 = y2 + _nn(w, jnp.where(lane_lt == (hh == 0), x2, 0.0))
            hp = st[ls, :]
            hin_ref[ls, :] = hp
            y_ref[:, ls] = y2 + _nt(cg, hp) * _pair_sel(eacs, lo, m, lane_lt)
            snew = _tn_dot(x2 * _pair_sel(tew, lo, m, lane_lt), bg)
            h0 = lo + 2 * m
            st[ls, :] = jnp.where(row_lt, eal[:, h0:h0 + 1], eal[:, h0 + 1:h0 + 2]) * hp + snew

    return _call(body, "ssd_fwd_r" if rev else "ssd_fwd_f", (nc,),
                 [pl.BlockSpec((Q, XBC), lambda s: (order(s), 0)), pl.BlockSpec((Q, LANE), lambda s: (order(s), 0)),
                  _full(8, LANE)],
                 [pl.BlockSpec((Q, D), lambda s: (order(s), 0)), pl.BlockSpec((None, D, LANE), lambda s: (order(s), 0, 0))],
                 [_sds((n_rows, D)), _sds((nc, D, LANE))], scratch=[pltpu.VMEM((D, LANE), F32)])(xbc, dtr, sp)


def _ssd_bwd(xbc, dtr, dy, hin, sp, dl, ncc, rev):
    n_rows = xbc.shape[0]
    nc = n_rows // Q
    lo = 16 if rev else 0
    fwd_order = _scan_order(nc, ncc, rev)
    order = lambda s: fwd_order(nc - 1 - s)
    with_skip = not rev

    def body(x_ref, dtr_ref, dy_ref, hin_ref, sp_ref, dl_ref, dx_ref, ddtr_ref, pg_ref, dst):
        @pl.when(pl.program_id(0) == 0)
        def _():
            dst[...] = jnp.zeros_like(dst)
            pg_ref[...] = jnp.zeros_like(pg_ref)
        dtr_v = dtr_ref[...]
        dt, a_neg, acs, msk, tri, last = _ssd_prep(dtr_v, sp_ref[...], rev)
        acs_t, dt_t = acs.T, dt.T
        eacs = jnp.exp(acs)
        eal = jnp.exp(acs[last:last + 1, :])
        te = jnp.exp(acs[last:last + 1, :] - acs)
        lane = lax.broadcasted_iota(jnp.int32, (Q, LANE), 1)
        sub = lax.broadcasted_iota(jnp.int32, (LANE, Q), 0)
        lane1 = lax.broadcasted_iota(jnp.int32, (1, LANE), 1)
        lane_lt = lane < 64
        row_lt = lax.broadcasted_iota(jnp.int32, (LANE, 1), 0) < 64
        bgs = [x_ref[:, 1024 + g * 128:1152 + g * 128] for g in range(2)]
        cgs = [x_ref[:, 1280 + g * 128:1408 + g * 128] for g in range(2)]
        s_g = [_nt(cgs[g], bgs[g]) for g in range(2)]
        d_s = [jnp.zeros((Q, Q), F32), jnp.zeros((Q, Q), F32)]
        dc_x = [jnp.zeros((Q, LANE), F32), jnp.zeros((Q, LANE), F32)]
        db_x = [jnp.zeros((Q, LANE), F32), jnp.zeros((Q, LANE), F32)]
        dacs = jnp.zeros((Q, LANE), F32)
        colsum_t = jnp.zeros((LANE, Q), F32)
        ddt_x = jnp.zeros((Q, LANE), F32)
        dlast = jnp.zeros((1, LANE), F32)
        for m in range(NPAIR):
            g = m // 4
            ls = slice(m * LANE, (m + 1) * LANE)
            x2, dy2 = x_ref[:, ls], dy_ref[:, ls]
            hp, dhp = hin_ref[ls, :], dst[ls, :]
            dtm, em, eam = (_pair_sel(v, lo, m, lane_lt) for v in (dt, te, eacs))
            xd2 = x2 * dtm
            bh = _nt(bgs[g], dhp)
            ch = _nt(cgs[g], hp)
            dxd2 = em * bh
            for hh in range(2):
                h = lo + 2 * m + hh
                half = lane_lt == (hh == 0)
                lm = jnp.exp(jnp.where(msk, acs[:, h:h + 1] - acs_t[h:h + 1, :], -jnp.inf))
                mh = s_g[g] * lm
                dyh = jnp.where(half, dy2, 0.0)
                dxd2 = dxd2 + _tn_dot(mh, dyh)
                dm = _nt(dyh, jnp.where(half, xd2, 0.0))
                d_s[g] = d_s[g] + dm * lm
                gh = dm * mh
                dacs = dacs + jnp.where(lane == h, jnp.sum(gh, axis=1, keepdims=True), 0.0)
                colsum_t = colsum_t + jnp.where(sub == h, jnp.sum(gh, axis=0, keepdims=True), 0.0)
                t1 = jnp.sum(jnp.where(half, dy2 * ch * eam, 0.0), axis=1, keepdims=True)
                rj = jnp.sum(jnp.where(half, xd2 * bh * em, 0.0), axis=1, keepdims=True)
                dacs = dacs + jnp.where(lane == h, t1 - rj, 0.0)
                hs = hp * dhp
                hsum = jnp.sum(jnp.sum(jnp.where(row_lt == (hh == 0), hs, 0.0), axis=1, keepdims=True), axis=0, keepdims=True)
                dlast = dlast + jnp.where(lane1 == h, jnp.sum(rj, axis=0, keepdims=True) + eal[:, h:h + 1] * hsum, 0.0)
            for hh in range(2):
                h = lo + 2 * m + hh
                half = lane_lt == (hh == 0)
                ddt_x = ddt_x + jnp.where(lane == h, jnp.sum(jnp.where(half, dxd2 * x2, 0.0), axis=1, keepdims=True), 0.0)
            dx2 = dxd2 * dtm
            if with_skip:
                dx2 = dx2 + dl_ref[:, ls] * dy2
            dx_ref[:, ls] = dx2
            edy = eam * dy2
            dc_x[g] = dc_x[g] + _nn(edy, hp)
            db_x[g] = db_x[g] + _nn(em * xd2, dhp)
            h0 = lo + 2 * m
            dst[ls, :] = jnp.where(row_lt, eal[:, h0:h0 + 1], eal[:, h0 + 1:h0 + 2]) * dhp + _tn_dot(edy, cgs[g])
        dacs = dacs - colsum_t.T
        rowi = lax.broadcasted_iota(jnp.int32, (Q, LANE), 0)
        dacs = dacs + jnp.where(rowi == last, dlast, 0.0)
        da = lax.dot_general(tri, dacs, (((0,), (0,)), ((), ())), precision=HI, preferred_element_type=F32)
        ddt = ddt_x + da * a_neg
        mine = (lane >= lo) & (lane < lo + 16)
        ddtr = jnp.where(mine, ddt * jax.nn.sigmoid(dtr_v + sp_ref[0:1]), 0.0)
        ddtr_ref[...] = ddtr
        pg_ref[0:1] += _sum0(ddtr)
        pg_ref[1:2] += jnp.where(mine[0:1], _sum0(da * dt) * a_neg, 0.0)
        for g in range(2):
            dx_ref[:, 1024 + g * 128:1152 + g * 128] = _tn_dot(d_s[g], cgs[g]) + db_x[g]
            dx_ref[:, 1280 + g * 128:1408 + g * 128] = _nn(d_s[g], bgs[g]) + dc_x[g]

    return _call(body, "ssd_bwd_r" if rev else "ssd_bwd_f", (nc,),
                 [pl.BlockSpec((Q, XBC), lambda s: (order(s), 0)), pl.BlockSpec((Q, LANE), lambda s: (order(s), 0)),
                  pl.BlockSpec((Q, D), lambda s: (order(s), 0)), pl.BlockSpec((None, D, LANE), lambda s: (order(s), 0, 0)),
                  _full(8, LANE), _full(1, D)],
                 [pl.BlockSpec((Q, XBC), lambda s: (order(s), 0)), pl.BlockSpec((Q, LANE), lambda s: (order(s), 0)),
                  _full(8, LANE)],
                 [_sds((n_rows, XBC)), _sds((n_rows, LANE)), _sds((8, LANE))],
                 scratch=[pltpu.VMEM((D, LANE), F32)])(xbc, dtr, dy, hin, sp, dl)


def _hy4_fwd(h, yf, yb, xbc, z, ysg, mp, dl, ng, wout, nct):
    n_rows = h.shape[0]

    def body(h_ref, yf_ref, yb_ref, xs_ref, z_ref, ysg_ref, mp_ref, dl_ref, ng_ref, wout_ref, hn_ref, yssd_ref, out_ref):
        ytot = yf_ref[...] + yb_ref[...] + dl_ref[...] * xs_ref[...]
        yssd = _gate_norm(ytot, z_ref[...], ng_ref[...]).astype(MXU)
        yssd_ref[...] = yssd.astype(ACT)
        out = (jnp.dot(yssd, wout_ref[0:D, :], preferred_element_type=F32)
               + jnp.dot(ysg_ref[...].astype(MXU), wout_ref[D:2 * D, :], preferred_element_type=F32))
        out_ref[...] = out
        hn_ref[...] = h_ref[...] + mp_ref[3:4] * out

    return _call(body, "hy4_fwd", (n_rows // T,),
                 [_rows(T, D), _rows(T, D, nct), _rows(T, D, nct), _rows(T, D, nct), _rows(T, D, nct), _rows(T, D, nct),
                  _full(8, D), _full(1, D), _full(1, D), _wfull(2 * D, D)],
                 [_rows(T, D), _rows(T, D), _rows(T, D)],
                 [_sds((n_rows, D)), _sds((n_rows, D), ACT), _sds((n_rows, D))])(h, yf, yb, xbc, z, ysg, mp, dl, ng, wout)


def _hy4_bwd(dh, out, yf, yb, xbc, z, mp, dl, ng, wout, nct):
    n_lat = dh.shape[0]
    n_rows = yf.shape[0]

    def body(dh_ref, out_ref, yf_ref, yb_ref, xs_ref, z_ref, mp_ref, dl_ref, ng_ref, wout_ref,
             dy_ref, dz_ref, dysg_ref, doutb_ref, pg_ref):
        i = pl.program_id(0)

        @pl.when(i == 0)
        def _():
            pg_ref[...] = jnp.zeros_like(pg_ref)

        @pl.when(i < nct)
        def _():
            dy_ref[...] = jnp.zeros_like(dy_ref)
            dz_ref[...] = jnp.zeros_like(dz_ref)
            dysg_ref[...] = jnp.zeros_like(dysg_ref)
            doutb_ref[...] = jnp.zeros_like(doutb_ref)

        @pl.when(i >= nct)
        def _():
            dhp = dh_ref[...]
            doutb = (mp_ref[3:4] * dhp).astype(MXU)
            doutb_ref[...] = doutb.astype(ACT)
            dysg_ref[...] = _nt(doutb, wout_ref[D:2 * D, :])
            dyssd = _nt(doutb, wout_ref[0:D, :])
            xs = xs_ref[...]
            ytot = yf_ref[...] + yb_ref[...] + dl_ref[...] * xs
            _, vjp = jax.vjp(_gate_norm, ytot, z_ref[...], ng_ref[...])
            dytot, dz, dng = vjp(dyssd)
            dy_ref[...] = dytot
            dz_ref[...] = dz.astype(ACT)
            pg_ref[0:1] += _sum0(dhp * out_ref[...])
            pg_ref[1:2] += dng
            pg_ref[2:3] += _sum0(dytot * xs)

    return _call(body, "hy4_bwd", (n_rows // T,),
                 [_rows_lat(T, D, nct), _rows_lat(T, D, nct), _rows(T, D), _rows(T, D), _rows(T, D), _rows(T, D),
                  _full(8, D), _full(1, D), _full(1, D), _wfull(2 * D, D)],
                 [_rows(T, D), _rows(T, D), _rows(T, D), _rows_lat(T, D, nct), _full(8, D)],
                 [_sds((n_rows, D)), _sds((n_rows, D), ACT), _sds((n_rows, D)), _sds((n_lat, D), ACT), _sds((8, D))],
                 )(dh, out, yf, yb, xbc, z, mp, dl, ng, wout)


def _loss_bwd(h, tgt, fng):
    n_rows = h.shape[0]

    def body(h_ref, t_ref, g_ref, dh_ref, pg_ref, ls_ref):
        @pl.when(pl.program_id(0) == 0)
        def _():
            pg_ref[...] = jnp.zeros_like(pg_ref)
            ls_ref[...] = jnp.zeros_like(ls_ref)
        hv = h_ref[...]
        g = g_ref[...]
        r = lax.rsqrt(jnp.mean(hv * hv, axis=-1, keepdims=True) + EPS)
        n = hv * r
        e = n * g - t_ref[...]
        ls_ref[...] += 0.5 * jnp.sum(jnp.sum(e * e, axis=1, keepdims=True), axis=0, keepdims=True) * (1.0 / D)
        dyv = e * (1.0 / D)
        pg_ref[0:1] += _sum0(dyv * n)
        dn = dyv * g
        dh_ref[...] = r * (dn - n * jnp.mean(dn * n, axis=-1, keepdims=True))

    return _call(body, "loss_bwd", (n_rows // T,), [_rows(T, D), _rows(T, D), _full(1, D)],
                 [_rows(T, D), _full(8, D), _full(8, LANE)],
                 [_sds((n_rows, D)), _sds((8, D)), _sds((8, LANE))])(h, tgt, fng)


def _pad_rows(a, rows):
    return jnp.concatenate([a, jnp.zeros((rows - a.shape[0],) + a.shape[1:], a.dtype)], axis=0)


def _mp(*rows):
    return _pad_rows(jnp.stack(rows, axis=0), 8)


def _local_step(x, ctx, tgt, ada, cada0, w):
    n_lat, n_ctx = x.shape[0], ctx.shape[0]
    nct, ncc = n_ctx // T, n_ctx // Q
    a0 = [ada[0, k * D:(k + 1) * D] for k in range(6)]
    a1 = [ada[1, k * D:(k + 1) * D] for k in range(6)]
    c0 = [cada0[k * D:(k + 1) * D] for k in range(6)]
    g = {}

    hcat = jnp.concatenate([ctx, x], axis=0)
    mp2 = jnp.stack([_mp(w["norm_mix_g"][0], c0[0], c0[1]), _mp(w["norm_mix_g"][0], a0[0], a0[1], a0[2])], axis=0)
    mp_l0 = mp2[1]
    sgbt = _pad_cols(w["sg_b"][0].T, LANE)
    lng, lnb = w["sg_ln_g"][0][None], w["sg_ln_b"][0][None]
    z, uv, xbcp, dtr, ysg = _hy1_fwd(hcat, mp2, w["wz"], w["wuv"], w["wxbc"], w["wdt"], lng, lnb, w["sg_w"], sgbt, nct)
    cw = _pad_rows(w["ssd_conv_w"][0], 8)
    cb = w["ssd_conv_b"][0][None]
    xbc = _cv5_fwd(xbcp, cw, cb, nct)
    sp = _pad_rows(jnp.stack([_pad_cols(w["ssd_dt_bias"][0].reshape(1, 32), LANE)[0],
                              _pad_cols(w["ssd_a_log"][0].reshape(1, 32), LANE)[0]], axis=0), 8)
    dl = jnp.repeat(w["ssd_d"][0], 64)[None]
    ng = w["ssd_norm_g"][0][None]
    yf, hin_f = _ssd_fwd(xbc, dtr, sp, ncc, False)
    yb, hin_b = _ssd_fwd(xbc, dtr, sp, ncc, True)
    h1, yssd, out0 = _hy4_fwd(x, yf, yb, xbc, z, ysg, mp_l0, dl, ng, w["hy_w_out"], nct)

    mpm0 = _mp(w["norm_mlp_g"][0], a0[3], a0[4], a0[5])
    h2, am0, ym0 = _mlp_fwd(h1, mpm0, w["mlp_w1"][0], w["mlp_w2"][0], "mlp0_fwd")

    mpc = _mp(w["norm_mix_g"][1], a1[0], a1[1], a1[2])
    wdw = _pad_rows(w["cf_w_dw"][0], 32)
    glu, acf = _cf1_fwd(h2, mpc, w["cf_w_pw1"], w["cf_b_pw1"])
    h3, cv, scf, ycf = _cf2_fwd(h2, glu, mpc, wdw, w["cf_b_dw"], w["cf_ln_g"], w["cf_ln_b"], w["cf_w_pw2"], w["cf_b_pw2"])

    mpm1 = _mp(w["norm_mlp_g"][1], a1[3], a1[4], a1[5])
    h4, am1, ym1 = _mlp_fwd(h3, mpm1, w["mlp_w1"][1], w["mlp_w2"][1], "mlp1_fwd")

    dh4, pg_f, ls = _loss_bwd(h4, tgt, w["final_norm_g"][None])
    loss = ls[0, 0]
    g["final_norm_g"] = pg_f[0]

    dh3, da1, dy1, u1, pgm1 = _mlp_bwd(dh4, h3, am1, ym1, mpm1, w["mlp_w1"][1], w["mlp_w2"][1], "mlp1_bwd")
    gw1_1 = _tn(u1, da1, "tn_mlp1_w1")
    gw2_1 = _tn(am1, dy1, "tn_mlp1_w2", relu2=True)

    dcv, dycf, pgc2 = _cf2_bwd(dh3, ycf, cv, mpc, w["cf_ln_g"], w["cf_ln_b"], w["cf_w_pw2"])
    g["cf_w_pw2"] = _tn(scf, dycf, "tn_cf_pw2")
    dh2, dacf, ucf, pgc1, pbc1, dwdw = _cf1_bwd(dh3, h2, acf, dcv, glu, mpc, wdw, w["cf_w_pw1"])
    g["cf_w_pw1"] = _tn(ucf, dacf, "tn_cf_pw1")
    g["cf_b_pw2"], g["cf_ln_g"], g["cf_ln_b"], g["cf_b_dw"] = pgc2[1], pgc2[2], pgc2[3], pgc2[4]
    g["cf_b_pw1"] = pbc1[0]
    g["cf_w_dw"] = dwdw[:CK]

    dh1, da0, dy0, u0, pgm0 = _mlp_bwd(dh2, h1, am0, ym0, mpm0, w["mlp_w1"][0], w["mlp_w2"][0], "mlp0_bwd")
    g["mlp_w1"] = jnp.stack([_tn(u0, da0, "tn_mlp0_w1"), gw1_1])
    g["mlp_w2"] = jnp.stack([_tn(am0, dy0, "tn_mlp0_w2", relu2=True), gw2_1])
    g["norm_mlp_g"] = jnp.stack([pgm0[0], pgm1[0]])

    dyt, dz, dysg, doutb, pg4 = _hy4_bwd(dh1, out0, yf, yb, xbc, z, mp_l0, dl, ng, w["hy_w_out"], nct)
    ysg_lat = lax.slice_in_dim(ysg, n_ctx, n_ctx + n_lat, axis=0)
    g["hy_w_out"] = jnp.concatenate([_tn(yssd, doutb, "tn_out_ssd"), _tn(ysg_lat, doutb, "tn_out_sg")], axis=0)
    dxf, ddf, pgsf = _ssd_bwd(xbc, dtr, dyt, hin_f, sp, dl, ncc, False)
    dxb, ddb, pgsb = _ssd_bwd(xbc, dtr, dyt, hin_b, sp, dl, ncc, True)
    dcv5, pgcb = _cv5_bwd1(xbcp, dxf, dxb, cw, cb, nct)
    dxbcp, dcw = _cv5_bwd2(dcv5, xbcp, cw, nct)
    dx, ucat, duv, ddt, pg2, pln, dsgw, dsgbt = _hy1_bwd(
        hcat, uv, dz, dxbcp, ddf, ddb, dysg, dh1, mp2, w["wz"], w["wuv"], w["wxbc"], w["wdt"], lng, lnb, w["sg_w"], sgbt, nct)
    g["hy_w_in"] = jnp.concatenate([_tn(ucat, dz, "tn_in_z"), _tn(ucat, dxbcp, "tn_in_xbc"),
                                    _tn(ucat, ddt, "tn_in_dt")[:, :32], _tn(ucat, duv, "tn_in_uv")], axis=1)
    g["ssd_conv_w"], g["ssd_conv_b"] = dcw[:SK], pgcb[0]
    pgs = pgsf + pgsb
    g["ssd_dt_bias"], g["ssd_a_log"] = pgs[0, :32].reshape(2, 16), pgs[1, :32].reshape(2, 16)
    g["ssd_d"] = jnp.sum(pg4[2].reshape(16, 64), axis=1)
    g["ssd_norm_g"] = pg4[1]
    g["sg_ln_g"], g["sg_ln_b"] = pln[0], pln[1]
    g["sg_w"], g["sg_b"] = dsgw, dsgbt[:, :SGG].T
    g["norm_mix_g"] = jnp.stack([pg2[0, 0] + pg2[1, 0], pgc1[0]])

    zero = jnp.zeros((D,), F32)
    d_ada = jnp.stack([jnp.concatenate([pg2[1, 1], pg2[1, 2], pg4[0], pgm0[1], pgm0[2], pgm0[3]]),
                       jnp.concatenate([pgc1[1], pgc1[2], pgc2[0], pgm1[1], pgm1[2], pgm1[3]])])
    d_cada0 = jnp.concatenate([pg2[0, 1], pg2[0, 2], zero, zero, zero, zero])
    return loss, dx, g, d_ada, d_cada0


def _pad_cols(a, cols):
    return jnp.concatenate([a, jnp.zeros(a.shape[:-1] + (cols - a.shape[-1],), a.dtype)], axis=-1)


MESH = pl.DeviceIdType.MESH
ANY = pl.BlockSpec(memory_space=pl.ANY)
IN_VMEM = pl.BlockSpec(memory_space=pltpu.VMEM)


def _coords():
    return lax.axis_index("x"), lax.axis_index("y"), lax.axis_index("c")


def _ag8(x, name):
    r, wd = x.shape

    def body(x_ref, o_ref, send, recv, lsem):
        mx, my, mc = _coords()
        me = 4 * mx + 2 * my + mc
        mine = pltpu.make_async_copy(x_ref, o_ref.at[me], lsem)
        mine.start()
        sent, peers = [], []
        for k in range(1, 8):
            px = 1 - mx if k & 4 else mx
            py = 1 - my if k & 2 else my
            pc = 1 - mc if k & 1 else mc
            cp = pltpu.make_async_remote_copy(src_ref=x_ref, dst_ref=o_ref.at[me], send_sem=send.at[k - 1],
                                              recv_sem=recv.at[k - 1], device_id=(px, py, pc), device_id_type=MESH)
            cp.start()
            sent.append(cp)
            peers.append((4 * px + 2 * py + pc, (px, py, pc)))
        for k in range(1, 8):
            slot, peer = peers[k - 1]
            pltpu.make_async_remote_copy(src_ref=x_ref, dst_ref=o_ref.at[slot], send_sem=send.at[k - 1],
                                         recv_sem=recv.at[k - 1], device_id=peer, device_id_type=MESH).wait_recv()
        for cp in sent:
            cp.wait_send()
        mine.wait()

    return pl.pallas_call(
        body, name=name, out_shape=_sds((8, r, wd), x.dtype), in_specs=[IN_VMEM], out_specs=IN_VMEM,
        scratch_shapes=[pltpu.SemaphoreType.DMA((7,)), pltpu.SemaphoreType.DMA((7,)), pltpu.SemaphoreType.DMA(())],
        compiler_params=pltpu.CompilerParams(vmem_limit_bytes=VMEM_LIMIT))(x)


def _xchg4(buf, name, a2a):
    r, wd = buf.shape[-2:]

    def body(in_ref, o_ref, send, recv, lsem):
        mx, my, mc = _coords()
        me = 2 * mx + my
        mine = pltpu.make_async_copy(in_ref.at[me] if a2a else in_ref, o_ref.at[me], lsem)
        mine.start()
        sent, peers = [], []
        for k in range(1, 4):
            px = 1 - mx if k & 2 else mx
            py = 1 - my if k & 1 else my
            pj = 2 * px + py
            cp = pltpu.make_async_remote_copy(src_ref=in_ref.at[pj] if a2a else in_ref, dst_ref=o_ref.at[me],
                                              send_sem=send.at[k - 1], recv_sem=recv.at[k - 1],
                                              device_id=(px, py, mc), device_id_type=MESH)
            cp.start()
            sent.append(cp)
            peers.append((pj, (px, py, mc)))
        for k in range(1, 4):
            pj, peer = peers[k - 1]
            pltpu.make_async_remote_copy(src_ref=in_ref.at[pj] if a2a else in_ref, dst_ref=o_ref.at[pj],
                                         send_sem=send.at[k - 1], recv_sem=recv.at[k - 1],
                                         device_id=peer, device_id_type=MESH).wait_recv()
        for cp in sent:
            cp.wait_send()
        mine.wait()

    return pl.pallas_call(
        body, name=name, out_shape=_sds((4, r, wd), buf.dtype), in_specs=[ANY], out_specs=ANY,
        scratch_shapes=[pltpu.SemaphoreType.DMA((3,)), pltpu.SemaphoreType.DMA((3,)), pltpu.SemaphoreType.DMA(())],
        )(buf)


def _xchg_sib(x, name):
    def body(in_ref, o_ref, send, recv):
        mx, my, mc = _coords()
        cp = pltpu.make_async_remote_copy(src_ref=in_ref, dst_ref=o_ref, send_sem=send, recv_sem=recv,
                                          device_id=(mx, my, 1 - mc), device_id_type=MESH)
        cp.start()
        cp.wait_recv()
        cp.wait_send()

    return pl.pallas_call(
        body, name=name, out_shape=_sds(x.shape, x.dtype), in_specs=[ANY], out_specs=ANY,
        scratch_shapes=[pltpu.SemaphoreType.DMA(()), pltpu.SemaphoreType.DMA(())])(x)


def _sum_slots(gat, slots, name, tr=None):
    n, r, wd = gat.shape
    tr = r if tr is None else tr

    def body(g_ref, o_ref):
        acc = g_ref[slots[0]].astype(F32)
        for s in slots[1:]:
            acc = acc + g_ref[s].astype(F32)
        o_ref[...] = acc

    return _call(body, name, (r // tr,), [pl.BlockSpec((n, tr, wd), lambda i: (0, i, 0))], _rows(tr, wd), _sds((r, wd)))(gat)


def _add(a, b, name, tr):
    def body(a_ref, b_ref, o_ref):
        o_ref[...] = a_ref[...] + b_ref[...]

    r, wd = a.shape
    return _call(body, name, (r // tr,), [_rows(tr, wd), _rows(tr, wd)], _rows(tr, wd), _sds((r, wd)))(a, b)


def _ada_fwd(x16, ada_w_loc, ada_b_loc):
    nloc = ada_w_loc.shape[-1]

    def body(x_ref, w_ref, b_ref, s_ref, o_ref):
        s = _silu(x_ref[...])
        s_ref[...] = s
        o_ref[...] = jnp.dot(s, w_ref[...], precision=HI, preferred_element_type=F32) + b_ref[...]

    return _call(body, "ada_fwd", (2,),
                 [_full(16, D), pl.BlockSpec((None, D, nloc), lambda l: (l, 0, 0)), pl.BlockSpec((None, 1, nloc), lambda l: (l, 0, 0))],
                 [_full(16, D), pl.BlockSpec((None, 16, nloc), lambda l: (l, 0, 0))],
                 [_sds((16, D)), _sds((2, 16, nloc))])(x16, ada_w_loc, ada_b_loc[:, None, :])


def _ada_bwd(s16, d_loc, ada_w_loc):
    nloc = ada_w_loc.shape[-1]

    def body(s_ref, d_ref, w_ref, gw_ref, cp_ref):
        gw_ref[...] = lax.dot_general(s_ref[...], d_ref[...], (((0,), (0,)), ((), ())), precision=HI,
                                      preferred_element_type=F32)

        @pl.when(pl.program_id(0) == 0)
        def _():
            cp_ref[...] = lax.dot_general(d_ref[8:16, :], w_ref[...], (((1,), (1,)), ((), ())), precision=HI,
                                          preferred_element_type=F32)

    return _call(body, "ada_bwd", (2,),
                 [_full(16, D), pl.BlockSpec((None, 16, nloc), lambda l: (l, 0, 0)), pl.BlockSpec((None, D, nloc), lambda l: (l, 0, 0))],
                 [pl.BlockSpec((None, D, nloc), lambda l: (l, 0, 0)), _full(8, D)],
                 [_sds((2, D, nloc)), _sds((8, D))])(s16, d_loc, ada_w_loc)


def _cctx_grad(dscc, c_ctx):
    def body(d_ref, c_ref, o_ref):
        _, vjp = jax.vjp(_silu, c_ref[...])
        o_ref[...] = vjp(d_ref[...])[0]

    return _call(body, "cctx_grad", (1,), [_full(8, D), _full(8, D)], _full(8, D), _sds((8, D)))(dscc, c_ctx)


def _adamw(w, g, m, v, name):
    shape = w.shape
    wd = shape[-1]
    r = w.size // wd
    tr = 256 if r % 256 == 0 else r
    c1 = 1.0 - ADAM_B1 ** ADAM_STEP
    c2 = 1.0 - ADAM_B2 ** ADAM_STEP

    def body(w_ref, g_ref, m_ref, v_ref, d_ref, mo_ref, vo_ref):
        gv = g_ref[...]
        mn = ADAM_B1 * m_ref[...] + (1.0 - ADAM_B1) * gv
        vn = ADAM_B2 * v_ref[...] + (1.0 - ADAM_B2) * jnp.square(gv)
        mo_ref[...] = mn
        vo_ref[...] = vn
        d_ref[...] = -ADAM_LR * ((mn / c1) / (jnp.sqrt(vn / c2) + ADAM_EPS) + ADAM_WD * w_ref[...])

    spec = _rows(tr, wd)
    outs = _call(body, name, (r // tr,), [spec] * 4, [spec] * 3, [_sds((r, wd))] * 3)(
        *(a.reshape(r, wd) for a in (w, g, m, v)))
    return tuple(o.reshape(shape) for o in outs)


ROW = 1024


def _nrows(size):
    return -(-size // ROW)


def _pack(arrs, rows_total, dtype=F32):
    parts = []
    for a in arrs:
        flat = a.reshape(-1).astype(dtype)
        pad = _nrows(flat.shape[0]) * ROW - flat.shape[0]
        parts.append(flat if pad == 0 else jnp.concatenate([flat, jnp.zeros((pad,), dtype)]))
    flat = jnp.concatenate(parts)
    out = flat.reshape(-1, ROW)
    return _pad_rows(out, rows_total)


def _unpack(buf, shapes):
    lead = buf.shape[:-2]
    out, r0 = [], 0
    for shp in shapes:
        size = 1
        for s in shp:
            size *= s
        nr = _nrows(size)
        piece = lax.slice_in_dim(buf, r0, r0 + nr, axis=len(lead))
        out.append(piece.reshape(lead + (nr * ROW,))[..., :size].reshape(lead + tuple(shp)))
        r0 += nr
    return out


def _round_up(n, k):
    return -(-n // k) * k


WEIGHTS = ['c_ctx', 'ada_w', 'ada_b', 'norm_mix_g', 'norm_mlp_g', 'mlp_w1', 'mlp_w2', 'hy_w_in', 'ssd_conv_w', 'ssd_conv_b',
           'ssd_dt_bias', 'ssd_a_log', 'ssd_d', 'ssd_norm_g', 'sg_ln_g', 'sg_ln_b', 'sg_w', 'sg_b', 'hy_w_out', 'cf_w_pw1',
           'cf_b_pw1', 'cf_w_dw', 'cf_b_dw', 'cf_ln_g', 'cf_ln_b', 'cf_w_pw2', 'cf_b_pw2', 'final_norm_g']
BIG = {'mlp_w1': 2, 'mlp_w2': 1, 'hy_w_in': 2, 'hy_w_out': 1, 'cf_w_pw1': 2, 'cf_w_pw2': 1}
SMALL_SHARD = ['ssd_conv_w', 'cf_b_pw1', 'cf_w_dw', 'cf_b_dw', 'cf_ln_g', 'cf_ln_b', 'cf_b_pw2']
REP = ['norm_mix_g', 'norm_mlp_g', 'ssd_conv_b', 'ssd_dt_bias', 'ssd_a_log', 'ssd_d', 'ssd_norm_g', 'sg_ln_g', 'sg_ln_b',
       'sg_w', 'sg_b', 'final_norm_g']


def _gather_shards(stacked, axis):
    return jnp.concatenate([stacked[j] for j in range(4)], axis=axis)


def _split_shards(full, axis):
    n = full.shape[axis] // 4
    return [lax.slice_in_dim(full, j * n, (j + 1) * n, axis=axis) for j in range(4)]


def kernel(x, c, ctx, c_ctx, ada_w, ada_b, norm_mix_g, norm_mlp_g, mlp_w1, mlp_w2, hy_w_in, ssd_conv_w, ssd_conv_b, ssd_dt_bias, ssd_a_log, ssd_d, ssd_norm_g, sg_ln_g, sg_ln_b, sg_w, sg_b, hy_w_out, cf_w_pw1, cf_b_pw1, cf_w_dw, cf_b_dw, cf_ln_g, cf_ln_b, cf_w_pw2, cf_b_pw2, final_norm_g, loss_target, m_c_ctx, m_ada_w, m_ada_b, m_norm_mix_g, m_norm_mlp_g, m_mlp_w1, m_mlp_w2, m_hy_w_in, m_ssd_conv_w, m_ssd_conv_b, m_ssd_dt_bias, m_ssd_a_log, m_ssd_d, m_ssd_norm_g, m_sg_ln_g, m_sg_ln_b, m_sg_w, m_sg_b, m_hy_w_out, m_cf_w_pw1, m_cf_b_pw1, m_cf_w_dw, m_cf_b_dw, m_cf_ln_g, m_cf_ln_b, m_cf_w_pw2, m_cf_b_pw2, m_final_norm_g, v_c_ctx, v_ada_w, v_ada_b, v_norm_mix_g, v_norm_mlp_g, v_mlp_w1, v_mlp_w2, v_hy_w_in, v_ssd_conv_w, v_ssd_conv_b, v_ssd_dt_bias, v_ssd_a_log, v_ssd_d, v_ssd_norm_g, v_sg_ln_g, v_sg_ln_b, v_sg_w, v_sg_b, v_hy_w_out, v_cf_w_pw1, v_cf_b_pw1, v_cf_w_dw, v_cf_b_dw, v_cf_ln_g, v_cf_ln_b, v_cf_w_pw2, v_cf_b_pw2, v_final_norm_g):
    args = locals()
    wl = {n: args[n] for n in WEIGHTS}
    ml = {n: args["m_" + n] for n in WEIGHTS}
    vl = {n: args["v_" + n] for n in WEIGHTS}
    mx, my, mc = _coords()
    me = 4 * mx + 2 * my + mc
    shard = 2 * mx + my
    even = (0, 2, 4, 6)

    small_shapes = [wl[n].shape for n in SMALL_SHARD]
    blk1 = _pack([c] + [wl[n] for n in SMALL_SHARD], 24)
    got1 = _ag8(blk1, "ag_cond")
    x16 = _pad_rows(jnp.concatenate([got1[:, 0, :], c_ctx[None]], axis=0), 16)
    small_full = {}
    for n, parts in zip(SMALL_SHARD, _unpack(got1[:, 1:, :], small_shapes)):
        small_full[n] = jnp.concatenate([parts[s] for s in even], axis=-1)

    nloc = ada_w.shape[-1]
    ada_b_loc = lax.dynamic_slice_in_dim(ada_b, shard * nloc, nloc, axis=1)
    s16, ada_loc = _ada_fwd(x16, ada_w, ada_b_loc)
    got2 = _ag8(ada_loc.reshape(32, nloc), "ag_ada").reshape(8, 2, 16, nloc)
    ada_full = jnp.concatenate([got2[s] for s in even], axis=-1)
    ada_me = lax.dynamic_slice_in_dim(ada_full, me, 1, axis=1)[:, 0, :]
    cada0 = ada_full[0, 8, :]

    big_names = list(BIG)
    big_shapes = [wl[n].shape for n in big_names]
    wrows = _round_up(sum(_nrows(wl[n].size) for n in big_names), 16)
    wg = _xchg4(_pack([wl[n] for n in big_names], wrows, MXU), "ag_weights", a2a=False)
    wfull = {n: _gather_shards(st, BIG[n]) for n, st in zip(big_names, _unpack(wg, big_shapes))}

    w = {n: wl[n] for n in WEIGHTS if n not in BIG and n not in SMALL_SHARD}
    w.update(small_full)
    win = wfull["hy_w_in"][0]
    w["wz"], w["wxbc"], w["wuv"] = win[:, :D], win[:, D:D + XBC], win[:, D + XBC + 32:]
    w["wdt"] = _pad_cols(win[:, D + XBC:D + XBC + 32], LANE)
    w["hy_w_out"] = wfull["hy_w_out"][0]
    w["mlp_w1"], w["mlp_w2"] = wfull["mlp_w1"], wfull["mlp_w2"]
    w["cf_w_pw1"], w["cf_w_pw2"] = wfull["cf_w_pw1"][0], wfull["cf_w_pw2"][0]
    w["sg_w"] = sg_w[0].astype(MXU)

    loss_part, dx, g, d_ada, d_cada0 = _local_step(x[0], ctx[0], loss_target[0], ada_me, cada0, w)
    loss = lax.psum(loss_part, ("x", "y", "c"))
    full_shape = {n: wl[n].shape for n in WEIGHTS}
    for n in BIG:
        full_shape[n] = tuple(s * 4 if a == BIG[n] else s for a, s in enumerate(wl[n].shape))
    for n in SMALL_SHARD:
        full_shape[n] = wl[n].shape[:-1] + (wl[n].shape[-1] * 4,)
    g = {n: a.reshape(full_shape[n]) for n, a in g.items()}

    sm_names = REP + SMALL_SHARD
    srows = _round_up(sum(_nrows(g[n].size) for n in sm_names) + 18, 8)
    got3 = _ag8(_pack([g[n] for n in sm_names] + [d_ada, d_cada0], srows), "ag_small")
    tot3 = _sum_slots(got3, tuple(range(8)), "sum_small")
    sm_tot = _unpack(tot3, [full_shape[n] for n in sm_names] + [(2, 6 * D), (6 * D,)])
    grads = dict(zip(sm_names, sm_tot[:-2]))
    for n in SMALL_SHARD:
        k = wl[n].shape[-1]
        grads[n] = lax.dynamic_slice_in_dim(grads[n], shard * k, k, axis=grads[n].ndim - 1)
    dada_tot, dcada_tot = sm_tot[-2], sm_tot[-1]
    grads["ada_b"] = dada_tot.at[0].add(dcada_tot)
    r_ada = sum(_nrows(g[n].size) for n in sm_names)
    dada_all = got3[:, r_ada:r_ada + 12, :].reshape(8, 2, 6 * D)
    d16 = jnp.concatenate([jnp.transpose(dada_all, (1, 0, 2)),
                           jnp.stack([dcada_tot, jnp.zeros_like(dcada_tot)])[:, None, :],
                           jnp.zeros((2, 7, 6 * D), F32)], axis=1)
    d_loc = lax.dynamic_slice_in_dim(d16, shard * nloc, nloc, axis=2)
    grads["ada_w"], cpart = _ada_bwd(s16, d_loc, ada_w)
    got4 = _ag8(cpart, "ag_cctx")
    dscc = _sum_slots(got4, even, "sum_cctx")
    grads["c_ctx"] = _cctx_grad(dscc, _pad_rows(c_ctx[None], 8))[0]

    grows = _round_up(sum(_nrows(wl[n].size) for n in big_names), 512)
    pieces = jnp.stack([_pack([_split_shards(g[n], BIG[n])[j] for n in big_names], grows, jnp.bfloat16) for j in range(4)])
    got5 = _xchg4(pieces, "a2a_grads", a2a=True)
    part = _sum_slots(got5, (0, 1, 2, 3), "sum_grads", tr=512)
    tot = _add(part, _xchg_sib(part, "swap_grads"), "add_grads", 512)
    grads.update(zip(big_names, _unpack(tot, big_shapes)))

    delta, new_m, new_v = {}, {}, {}
    for n in list(BIG) + ["ada_w"]:
        delta[n], new_m[n], new_v[n] = _adamw(wl[n], grads[n], ml[n], vl[n], "adamw_" + n)
    for group, tag in ((["c_ctx", "ada_b"] + REP, "rep"), (SMALL_SHARD, "shard")):
        shapes = [wl[n].shape for n in group]
        rows = _round_up(sum(_nrows(wl[n].size) for n in group), 8)
        outs = _adamw(*(_pack([src[n] for n in group], rows) for src in (wl, grads, ml, vl)), "adamw_small_" + tag)
        for dst, buf in zip((delta, new_m, new_v), outs):
            dst.update(zip(group, _unpack(buf, shapes)))

    return (loss, dx[None], *[grads[n].reshape(wl[n].shape) for n in WEIGHTS], *[delta[n] for n in WEIGHTS],
            *[new_m[n] for n in WEIGHTS], *[new_v[n] for n in WEIGHTS])
```

```python
import functools

import jax
import jax.numpy as jnp
from jax import lax
from jax.experimental import pallas as pl
from jax.experimental.pallas import tpu as pltpu

F32 = jnp.float32
MXU = jnp.bfloat16
ACT = jnp.bfloat16
HI = lax.Precision.HIGHEST
EPS = 1e-6

D = 1024
HID = 4096
XBC = 1536
NPAIR = 8
Q = 128
SGG = 8
CK = 31
SK = 5
T = 256
LANE = 128
VMEM_LIMIT = 56 * 1024 * 1024

ADAM_LR, ADAM_B1, ADAM_B2, ADAM_EPS, ADAM_WD, ADAM_STEP = 0.001, 0.9, 0.999, 1e-08, 0.01, 10


def _call(body, name, grid, in_specs, out_specs, out_shape, scratch=()):
    return pl.pallas_call(
        body, name=name, grid=grid, in_specs=in_specs, out_specs=out_specs, out_shape=out_shape,
        scratch_shapes=list(scratch),
        compiler_params=pltpu.CompilerParams(dimension_semantics=("arbitrary",) * len(grid),
                                             vmem_limit_bytes=VMEM_LIMIT))


def _sds(shape, dt=F32):
    return jax.ShapeDtypeStruct(tuple(shape), dt)


def _rows(t, w, off=0, lane_blk=0):
    return pl.BlockSpec((t, w), lambda i: (i + off, lane_blk))


def _rows_lat(t, w, nct):
    return pl.BlockSpec((t, w), lambda i: (jnp.maximum(i - nct, 0), 0))


def _full(*shape):
    return pl.BlockSpec(shape, lambda *_: (0,) * len(shape))


def _wfull(*shape):
    return pl.BlockSpec(shape, lambda *_: (0,) * len(shape), pipeline_mode=pl.Buffered(1))


def _halo(w, hb, nrows):
    r, nb = T // hb, nrows // hb
    prev = pl.BlockSpec((hb, w), lambda i: (jnp.maximum(i * r - 1, 0), 0))
    nxt = pl.BlockSpec((hb, w), lambda i: (jnp.minimum((i + 1) * r, nb - 1), 0))
    return prev, nxt


def _nn(a, b):
    return jnp.dot(a.astype(MXU), b.astype(MXU), preferred_element_type=F32)


def _nt(a, b):
    return lax.dot_general(a.astype(MXU), b.astype(MXU), (((1,), (1,)), ((), ())), preferred_element_type=F32)


def _tn_dot(a, b):
    return lax.dot_general(a.astype(MXU), b.astype(MXU), (((0,), (0,)), ((), ())), preferred_element_type=F32)


def _sum0(x):
    return jnp.sum(x, axis=0, keepdims=True)


def _silu(x):
    return x * jax.nn.sigmoid(x)


def _gelu(x):
    return jax.nn.gelu(x, approximate=True)


def _mod(h, g, sh, sc):
    n = h * lax.rsqrt(jnp.mean(h * h, axis=-1, keepdims=True) + EPS)
    return n * g * (1.0 + sc) + sh


def _ln(x, g, b):
    xc = x - jnp.mean(x, axis=-1, keepdims=True)
    return xc * lax.rsqrt(jnp.mean(xc * xc, axis=-1, keepdims=True) + EPS) * g + b


def _gate_norm(ytot, z, ng):
    yg = ytot * _silu(z)
    halves = []
    for k in range(2):
        seg = yg[:, k * 512:(k + 1) * 512]
        halves.append(seg * lax.rsqrt(jnp.mean(seg * seg, axis=-1, keepdims=True) + EPS) * ng[:, k * 512:(k + 1) * 512])
    return jnp.concatenate(halves, axis=-1)


def _fill_ext(ext_ref, prev_ref, cur_ref, next_ref, hb, first, last):
    ext_ref[0:hb, :] = jnp.where(first, 0.0, prev_ref[...])
    ext_ref[hb:hb + T, :] = cur_ref[...]
    ext_ref[hb + T:hb + T + hb, :] = jnp.where(last, 0.0, next_ref[...])


def _conv(ext_ref, w_ref, k_taps, hb, lanes):
    off = hb - k_taps // 2
    acc = ext_ref[pl.ds(off, T), lanes] * w_ref[0:1, lanes]
    for k in range(1, k_taps):
        acc = acc + ext_ref[pl.ds(off + k, T), lanes] * w_ref[k:k + 1, lanes]
    return acc


def _conv_tr(ext_ref, w_ref, k_taps, hb, lanes):
    off = hb + k_taps // 2
    acc = ext_ref[pl.ds(off, T), lanes] * w_ref[0:1, lanes]
    for k in range(1, k_taps):
        acc = acc + ext_ref[pl.ds(off - k, T), lanes] * w_ref[k:k + 1, lanes]
    return acc


def _conv_wide(ext_ref, w_ref, k_taps, hb, lanes, flip=False):
    base = hb - k_taps // 2
    acc = None
    for b in range(8):
        taps = [k for k in range(k_taps) if (base + k) % 8 == b]
        if not taps:
            continue
        p = None
        for k in taps:
            wi = (k_taps - 1 - k) if flip else k
            term = ext_ref[pl.ds(base + k - b, T + 8), lanes] * w_ref[wi:wi + 1, lanes]
            p = term if p is None else p + term
        acc = p[b:b + T] if acc is None else acc + p[b:b + T]
    return acc


def _conv_dw_wide(dw_ref, d_ref, xext_ref, k_taps, hb, lanes):
    base = hb - k_taps // 2
    d = d_ref[:, lanes]
    for b in range(8):
        taps = [k for k in range(k_taps) if (base + k) % 8 == b]
        if not taps:
            continue
        lo_off = base + taps[0] - b
        span = base + taps[-1] - b - lo_off
        xs = xext_ref[pl.ds(lo_off + b, T + span), lanes]
        for k in taps:
            a = base + k - b - lo_off
            dw_ref[k:k + 1, lanes] += _sum0(d * xs[a:a + T])


def _conv_dw(dw_ref, d_ref, xext_ref, k_taps, hb, lanes):
    off = hb - k_taps // 2
    d = d_ref[:, lanes]
    for k in range(k_taps):
        dw_ref[k:k + 1, lanes] += _sum0(d * xext_ref[pl.ds(off + k, T), lanes])


def _tn(a, b, name, relu2=False):
    m_rows, ka = a.shape
    n = b.shape[1]
    tm = next(t for t in (1024, 768, 512, 256) if m_rows % t == 0)
    tk = min(ka, 1024)
    tn = n if n <= 1024 else next(t for t in (1024, 768, 512, 384, 256, 128) if n % t == 0)

    def body(a_ref, b_ref, o_ref):
        @pl.when(pl.program_id(2) == 0)
        def _():
            o_ref[...] = jnp.zeros_like(o_ref)
        av = a_ref[...]
        if relu2:
            av = jnp.square(jnp.maximum(av.astype(F32), 0.0))
        o_ref[...] += _tn_dot(av, b_ref[...])

    return _call(body, name, (ka // tk, n // tn, m_rows // tm),
                 [pl.BlockSpec((tm, tk), lambda k, j, m: (m, k)), pl.BlockSpec((tm, tn), lambda k, j, m: (m, j))],
                 pl.BlockSpec((tk, tn), lambda k, j, m: (k, j)), _sds((ka, n)))(a, b)


def _mlp_fwd(h, mp, w1, w2, name):
    n_rows = h.shape[0]

    def body(h_ref, mp_ref, w1_ref, w2_ref, hn_ref, a_ref, y_ref):
        hv = h_ref[...]
        u = _mod(hv, mp_ref[0:1], mp_ref[1:2], mp_ref[2:3]).astype(MXU)
        acc = jnp.zeros((T, D), F32)
        for j in range(HID // 1024):
            cs = slice(j * 1024, (j + 1) * 1024)
            a = jnp.dot(u, w1_ref[:, cs], preferred_element_type=F32)
            a_ref[:, cs] = a.astype(ACT)
            acc = acc + jnp.dot(jnp.square(jnp.maximum(a, 0.0)).astype(MXU), w2_ref[cs, :], preferred_element_type=F32)
        y_ref[...] = acc
        hn_ref[...] = hv + mp_ref[3:4] * acc

    return _call(body, name, (n_rows // T,),
                 [_rows(T, D), _full(8, D), _wfull(D, HID), _wfull(HID, D)],
                 [_rows(T, D), _rows(T, HID), _rows(T, D)],
                 [_sds((n_rows, D)), _sds((n_rows, HID), ACT), _sds((n_rows, D))])(h, mp, w1, w2)


def _mlp_bwd(dh, h, a, y, mp, w1, w2, name):
    n_rows = h.shape[0]

    def body(dh_ref, h_ref, a_ref, y_ref, mp_ref, w1_ref, w2_ref, dho_ref, da_ref, dyb_ref, ub_ref, pg_ref):
        dhp = dh_ref[...]
        u, vjp = jax.vjp(_mod, h_ref[...], mp_ref[0:1], mp_ref[1:2], mp_ref[2:3])
        ub_ref[...] = u.astype(ACT)
        dyb = (mp_ref[3:4] * dhp).astype(MXU)
        dyb_ref[...] = dyb.astype(ACT)
        du = jnp.zeros((T, D), F32)
        for j in range(HID // 1024):
            cs = slice(j * 1024, (j + 1) * 1024)
            dp = _nt(dyb, w2_ref[cs, :])
            da = dp * 2.0 * jnp.maximum(a_ref[:, cs].astype(F32), 0.0)
            da_ref[:, cs] = da.astype(ACT)
            du = du + _nt(da, w1_ref[:, cs])
        dhn, dg, dsh, dsc = vjp(du)
        dho_ref[...] = dhp + dhn

        @pl.when(pl.program_id(0) == 0)
        def _():
            pg_ref[...] = jnp.zeros_like(pg_ref)
        pg_ref[0:1] += dg
        pg_ref[1:2] += dsh
        pg_ref[2:3] += dsc
        pg_ref[3:4] += _sum0(dhp * y_ref[...])

    return _call(body, name, (n_rows // T,),
                 [_rows(T, D), _rows(T, D), _rows(T, HID), _rows(T, D), _full(8, D), _wfull(D, HID), _wfull(HID, D)],
                 [_rows(T, D), _rows(T, HID), _rows(T, D), _rows(T, D), _full(8, D)],
                 [_sds((n_rows, D)), _sds((n_rows, HID), ACT), _sds((n_rows, D), ACT), _sds((n_rows, D), ACT),
                  _sds((8, D))])(dh, h, a, y, mp, w1, w2)


def _cf1_fwd(h, mp, w1, b1):
    n_rows = h.shape[0]

    def body(h_ref, mp_ref, w1_ref, b1_ref, glu_ref, a_ref):
        u = _mod(h_ref[...], mp_ref[0:1], mp_ref[1:2], mp_ref[2:3]).astype(MXU)
        a = jnp.dot(u, w1_ref[...], preferred_element_type=F32) + b1_ref[...]
        a_ref[...] = a.astype(ACT)
        glu_ref[...] = a[:, :D] * jax.nn.sigmoid(a[:, D:])

    return _call(body, "cf1_fwd", (n_rows // T,),
                 [_rows(T, D), _full(8, D), _wfull(D, 2 * D), _full(1, 2 * D)],
                 [_rows(T, D), _rows(T, 2 * D)],
                 [_sds((n_rows, D)), _sds((n_rows, 2 * D), ACT)])(h, mp, w1, b1)


def _cf2_fwd(h, glu, mp, wdw, bdw, lng, lnb, w2, b2):
    n_rows = h.shape[0]
    nt = n_rows // T
    hb = 16

    def body(h_ref, gp_ref, gc_ref, gn_ref, mp_ref, wdw_ref, bdw_ref, lng_ref, lnb_ref, w2_ref, b2_ref,
             hn_ref, cv_ref, sb_ref, y_ref, ext):
        i = pl.program_id(0)
        _fill_ext(ext, gp_ref, gc_ref, gn_ref, hb, i == 0, i == nt - 1)
        for c in range(D // LANE):
            lanes = slice(c * LANE, (c + 1) * LANE)
            cv_ref[:, lanes] = _conv_wide(ext, wdw_ref, CK, hb, lanes) + bdw_ref[:, lanes]
        s = _silu(_ln(cv_ref[...], lng_ref[...], lnb_ref[...])).astype(MXU)
        sb_ref[...] = s.astype(ACT)
        y = jnp.dot(s, w2_ref[...], preferred_element_type=F32) + b2_ref[...]
        y_ref[...] = y
        hn_ref[...] = h_ref[...] + mp_ref[3:4] * y

    gp, gn = _halo(D, hb, n_rows)
    return _call(body, "cf2_fwd", (nt,),
                 [_rows(T, D), gp, _rows(T, D), gn, _full(8, D), _full(32, D), _full(1, D), _full(1, D), _full(1, D),
                  _wfull(D, D), _full(1, D)],
                 [_rows(T, D), _rows(T, D), _rows(T, D), _rows(T, D)],
                 [_sds((n_rows, D)), _sds((n_rows, D)), _sds((n_rows, D), ACT), _sds((n_rows, D))],
                 scratch=[pltpu.VMEM((T + 2 * hb, D), F32)])(h, glu, glu, glu, mp, wdw, bdw, lng, lnb, w2, b2)


def _cf2_bwd(dh, y, cv, mp, lng, lnb, w2):
    n_rows = dh.shape[0]

    def body(dh_ref, y_ref, cv_ref, mp_ref, lng_ref, lnb_ref, w2_ref, dcv_ref, dyb_ref, pg_ref):
        dhp = dh_ref[...]
        dy = mp_ref[3:4] * dhp
        dyb = dy.astype(MXU)
        dyb_ref[...] = dyb.astype(ACT)
        ds = _nt(dyb, w2_ref[...])
        _, vjp = jax.vjp(lambda cv_, g_, b_: _silu(_ln(cv_, g_, b_)), cv_ref[...], lng_ref[...], lnb_ref[...])
        dcv, dlng, dlnb = vjp(ds)
        dcv_ref[...] = dcv

        @pl.when(pl.program_id(0) == 0)
        def _():
            pg_ref[...] = jnp.zeros_like(pg_ref)
        pg_ref[0:1] += _sum0(dhp * y_ref[...])
        pg_ref[1:2] += _sum0(dy)
        pg_ref[2:3] += dlng
        pg_ref[3:4] += dlnb
        pg_ref[4:5] += _sum0(dcv)

    return _call(body, "cf2_bwd", (n_rows // T,),
                 [_rows(T, D), _rows(T, D), _rows(T, D), _full(8, D), _full(1, D), _full(1, D), _wfull(D, D)],
                 [_rows(T, D), _rows(T, D), _full(8, D)],
                 [_sds((n_rows, D)), _sds((n_rows, D), ACT), _sds((8, D))])(dh, y, cv, mp, lng, lnb, w2)


def _cf1_bwd(dh, h, a, dcv, glu, mp, wdw, w1):
    n_rows = h.shape[0]
    nt = n_rows // T
    hb = 16

    def body(dh_ref, h_ref, a_ref, dp_ref, dc_ref, dn_ref, gp_ref, gc_ref, gn_ref, mp_ref, wdw_ref, w1_ref,
             dho_ref, da_ref, ub_ref, pg_ref, pb_ref, dw_ref, dext, gext, dglu):
        i = pl.program_id(0)

        @pl.when(i == 0)
        def _():
            pg_ref[...] = jnp.zeros_like(pg_ref)
            pb_ref[...] = jnp.zeros_like(pb_ref)
            dw_ref[...] = jnp.zeros_like(dw_ref)
        _fill_ext(dext, dp_ref, dc_ref, dn_ref, hb, i == 0, i == nt - 1)
        _fill_ext(gext, gp_ref, gc_ref, gn_ref, hb, i == 0, i == nt - 1)
        for c in range(D // LANE):
            lanes = slice(c * LANE, (c + 1) * LANE)
            dglu[:, lanes] = _conv_wide(dext, wdw_ref, CK, hb, lanes, flip=True)
            _conv_dw_wide(dw_ref, dc_ref, gext, CK, hb, lanes)
        av = a_ref[...].astype(F32)
        _, vjp_glu = jax.vjp(lambda a1, a2: a1 * jax.nn.sigmoid(a2), av[:, :D], av[:, D:])
        da1, da2 = vjp_glu(dglu[...])
        da_ref[:, :D] = da1.astype(ACT)
        da_ref[:, D:] = da2.astype(ACT)
        pb_ref[0:1, :D] += _sum0(da1)
        pb_ref[0:1, D:] += _sum0(da2)
        du = _nt(da1, w1_ref[:, :D]) + _nt(da2, w1_ref[:, D:])
        u, vjp = jax.vjp(_mod, h_ref[...], mp_ref[0:1], mp_ref[1:2], mp_ref[2:3])
        ub_ref[...] = u.astype(ACT)
        dhn, dg, dsh, dsc = vjp(du)
        dho_ref[...] = dh_ref[...] + dhn
        pg_ref[0:1] += dg
        pg_ref[1:2] += dsh
        pg_ref[2:3] += dsc

    hp, hn = _halo(D, hb, n_rows)
    return _call(body, "cf1_bwd", (nt,),
                 [_rows(T, D), _rows(T, D), _rows(T, 2 * D), hp, _rows(T, D), hn, hp, _rows(T, D), hn,
                  _full(8, D), _full(32, D), _wfull(D, 2 * D)],
                 [_rows(T, D), _rows(T, 2 * D), _rows(T, D), _full(8, D), _full(8, 2 * D), _full(32, D)],
                 [_sds((n_rows, D)), _sds((n_rows, 2 * D), ACT), _sds((n_rows, D), ACT), _sds((8, D)),
                  _sds((8, 2 * D)), _sds((32, D))],
                 scratch=[pltpu.VMEM((T + 2 * hb, D), F32), pltpu.VMEM((T + 2 * hb, D), F32), pltpu.VMEM((T, D), F32)],
                 )(dh, h, a, dcv, dcv, dcv, glu, glu, glu, mp, wdw, w1)


def _sg_blocks():
    return [(c, g, slice(c * Q, (c + 1) * Q), slice(g * LANE, (g + 1) * LANE)) for c in range(T // Q) for g in range(SGG)]


def _hy1_fwd(hcat, mp2, wz, wuv, wxbc, wdt, lng, lnb, sgw, sgbt, nct):
    n_rows = hcat.shape[0]

    def body(h_ref, mp_ref, wz_ref, wuv_ref, wxbc_ref, wdt_ref, lng_ref, lnb_ref, sgw_ref, sgbt_ref,
             z_ref, uv_ref, xbcp_ref, dtr_ref, ysg_ref):
        u = _mod(h_ref[...], mp_ref[0:1], mp_ref[1:2], mp_ref[2:3]).astype(MXU)
        z_ref[...] = jnp.dot(u, wz_ref[...], preferred_element_type=F32)
        xbcp_ref[...] = jnp.dot(u, wxbc_ref[...], preferred_element_type=F32)
        dtr_ref[...] = jnp.dot(u, wdt_ref[...], preferred_element_type=F32)
        uv = jnp.dot(u, wuv_ref[...], preferred_element_type=F32)
        uv_ref[...] = uv
        gate = _gelu(uv[:, :D])
        vln = _ln(_gelu(uv[:, D:]), lng_ref[...], lnb_ref[...]).astype(MXU)
        for _, g, rs, ls in _sg_blocks():
            s = jnp.dot(sgw_ref[g], vln[rs, ls], preferred_element_type=F32) + sgbt_ref[:, g:g + 1]
            ysg_ref[rs, ls] = (gate[rs, ls] * s).astype(ACT)

    mspec = pl.BlockSpec((None, 8, D), lambda i: (jnp.where(i < nct, 0, 1), 0, 0))
    return _call(body, "hy1_fwd", (n_rows // T,),
                 [_rows(T, D), mspec, _wfull(D, D), _wfull(D, 2 * D), _wfull(D, XBC), _wfull(D, LANE),
                  _full(1, D), _full(1, D), _full(SGG, Q, Q), _full(Q, LANE)],
                 [_rows(T, D), _rows(T, 2 * D), _rows(T, XBC), _rows(T, LANE), _rows(T, D)],
                 [_sds((n_rows, D)), _sds((n_rows, 2 * D)), _sds((n_rows, XBC)), _sds((n_rows, LANE)),
                  _sds((n_rows, D), ACT)])(hcat, mp2, wz, wuv, wxbc, wdt, lng, lnb, sgw, sgbt)


def _hy1_bwd(hcat, uv, dz, dxbcp, ddf, ddb, dysg, dres, mp2, wz, wuv, wxbc, wdt, lng, lnb, sgw, sgbt, nct):
    n_rows = hcat.shape[0]
    n_lat = dres.shape[0]

    def body(h_ref, uv_ref, dz_ref, dxbcp_ref, ddf_ref, ddb_ref, dysg_ref, dres_ref, mp_ref, wz_ref, wuv_ref,
             wxbc_ref, wdt_ref, lng_ref, lnb_ref, sgw_ref, sgbt_ref,
             dho_ref, ub_ref, duv_ref, ddt_ref, pg2_ref, pl_ref, dsgw_ref, dsgb_ref, dgate_s, dvln_s):
        i = pl.program_id(0)

        @pl.when(i == 0)
        def _():
            pg2_ref[...] = jnp.zeros_like(pg2_ref)
            pl_ref[...] = jnp.zeros_like(pl_ref)
            dsgw_ref[...] = jnp.zeros_like(dsgw_ref)
            dsgb_ref[...] = jnp.zeros_like(dsgb_ref)
        uv = uv_ref[...]

        def f_sg(ug, uvv, g_, b_):
            return _gelu(ug), _ln(_gelu(uvv), g_, b_)
        (gate, vln), vjp_sg = jax.vjp(f_sg, uv[:, :D], uv[:, D:], lng_ref[...], lnb_ref[...])
        vlnb = vln.astype(MXU)
        lane = lax.broadcasted_iota(jnp.int32, (Q, LANE), 1)
        dsgb = jnp.zeros((Q, LANE), F32)
        for _, g, rs, ls in _sg_blocks():
            s = jnp.dot(sgw_ref[g], vlnb[rs, ls], preferred_element_type=F32) + sgbt_ref[:, g:g + 1]
            dyb = dysg_ref[rs, ls]
            dgate_s[rs, ls] = dyb * s
            ds = dyb * gate[rs, ls]
            dvln_s[rs, ls] = _tn_dot(sgw_ref[g], ds)
            dsgw_ref[g] += _nt(ds, vlnb[rs, ls])
            dsgb = dsgb + jnp.where(lane == g, jnp.sum(ds, axis=1, keepdims=True), 0.0)
        dsgb_ref[...] += dsgb
        dug, duvv, dlng, dlnb = vjp_sg((dgate_s[...], dvln_s[...]))
        pl_ref[0:1] += dlng
        pl_ref[1:2] += dlnb
        duv_ref[:, :D] = dug.astype(ACT)
        duv_ref[:, D:] = duvv.astype(ACT)
        ddt = (ddf_ref[...] + ddb_ref[...]).astype(MXU)
        ddt_ref[...] = ddt.astype(ACT)
        du = (_nt(dz_ref[...], wz_ref[...]) + _nt(dug, wuv_ref[:, :D]) + _nt(duvv, wuv_ref[:, D:])
              + _nt(dxbcp_ref[...], wxbc_ref[...]) + _nt(ddt, wdt_ref[...]))
        u, vjp = jax.vjp(_mod, h_ref[...], mp_ref[0:1], mp_ref[1:2], mp_ref[2:3])
        ub_ref[...] = u.astype(ACT)
        dhn, dg, dsh, dsc = vjp(du)
        dho_ref[...] = dres_ref[...] + dhn
        is_ctx = i < nct
        for k, val in enumerate((dg, dsh, dsc)):
            pg2_ref[0, k:k + 1] += jnp.where(is_ctx, val, 0.0)
            pg2_ref[1, k:k + 1] += jnp.where(is_ctx, 0.0, val)

    mspec = pl.BlockSpec((None, 8, D), lambda i: (jnp.where(i < nct, 0, 1), 0, 0))
    return _call(body, "hy1_bwd", (n_rows // T,),
                 [_rows(T, D), _rows(T, 2 * D), _rows(T, D), _rows(T, XBC), _rows(T, LANE), _rows(T, LANE), _rows(T, D),
                  _rows_lat(T, D, nct), mspec, _wfull(D, D), _wfull(D, 2 * D), _wfull(D, XBC), _wfull(D, LANE),
                  _full(1, D), _full(1, D), _full(SGG, Q, Q), _full(Q, LANE)],
                 [_rows_lat(T, D, nct), _rows(T, D), _rows(T, 2 * D), _rows(T, LANE), _full(2, 8, D), _full(8, D),
                  _full(SGG, Q, Q), _full(Q, LANE)],
                 [_sds((n_lat, D)), _sds((n_rows, D), ACT), _sds((n_rows, 2 * D), ACT), _sds((n_rows, LANE), ACT),
                  _sds((2, 8, D)), _sds((8, D)), _sds((SGG, Q, Q)), _sds((Q, LANE))],
                 scratch=[pltpu.VMEM((T, D), F32), pltpu.VMEM((T, D), F32)],
                 )(hcat, uv, dz, dxbcp, ddf, ddb, dysg, dres, mp2, wz, wuv, wxbc, wdt, lng, lnb, sgw, sgbt)


def _seq_edges(i, nct, nt):
    return (i == 0) | (i == nct), (i == nct - 1) | (i == nt - 1)


def _cv5_fwd(xbcp, w, b, nct):
    n_rows = xbcp.shape[0]
    nt = n_rows // T
    hb = 8

    def body(p_ref, c_ref, n_ref, w_ref, b_ref, o_ref, ext):
        first, last = _seq_edges(pl.program_id(0), nct, nt)
        _fill_ext(ext, p_ref, c_ref, n_ref, hb, first, last)
        for c in range(XBC // LANE):
            lanes = slice(c * LANE, (c + 1) * LANE)
            o_ref[:, lanes] = _silu(_conv(ext, w_ref, SK, hb, lanes) + b_ref[:, lanes])

    hp, hn = _halo(XBC, hb, n_rows)
    return _call(body, "cv5_fwd", (nt,), [hp, _rows(T, XBC), hn, _full(8, XBC), _full(1, XBC)],
                 _rows(T, XBC), _sds((n_rows, XBC)), scratch=[pltpu.VMEM((T + 2 * hb, XBC), F32)])(xbcp, xbcp, xbcp, w, b)


def _cv5_bwd1(xbcp, dxf, dxb, w, b, nct):
    n_rows = xbcp.shape[0]
    nt = n_rows // T
    hb = 8

    def body(p_ref, c_ref, n_ref, dxf_ref, dxb_ref, w_ref, b_ref, o_ref, pg_ref, ext):
        i = pl.program_id(0)
        first, last = _seq_edges(i, nct, nt)
        _fill_ext(ext, p_ref, c_ref, n_ref, hb, first, last)

        @pl.when(i == 0)
        def _():
            pg_ref[...] = jnp.zeros_like(pg_ref)
        for c in range(XBC // LANE):
            lanes = slice(c * LANE, (c + 1) * LANE)
            cv = _conv(ext, w_ref, SK, hb, lanes) + b_ref[:, lanes]
            sg = jax.nn.sigmoid(cv)
            dcv = (dxf_ref[:, lanes] + dxb_ref[:, lanes]) * (sg * (1.0 + cv * (1.0 - sg)))
            o_ref[:, lanes] = dcv
            pg_ref[0:1, lanes] += _sum0(dcv)

    hp, hn = _halo(XBC, hb, n_rows)
    return _call(body, "cv5_bwd1", (nt,),
                 [hp, _rows(T, XBC), hn, _rows(T, XBC), _rows(T, XBC), _full(8, XBC), _full(1, XBC)],
                 [_rows(T, XBC), _full(8, XBC)], [_sds((n_rows, XBC)), _sds((8, XBC))],
                 scratch=[pltpu.VMEM((T + 2 * hb, XBC), F32)])(xbcp, xbcp, xbcp, dxf, dxb, w, b)


def _cv5_bwd2(dcv, xbcp, w, nct):
    n_rows = xbcp.shape[0]
    nt = n_rows // T
    hb = 8

    def body(dp_ref, dc_ref, dn_ref, xp_ref, xc_ref, xn_ref, w_ref, o_ref, dw_ref, dext, xext):
        i = pl.program_id(0)
        first, last = _seq_edges(i, nct, nt)
        _fill_ext(dext, dp_ref, dc_ref, dn_ref, hb, first, last)
        _fill_ext(xext, xp_ref, xc_ref, xn_ref, hb, first, last)

        @pl.when(i == 0)
        def _():
            dw_ref[...] = jnp.zeros_like(dw_ref)
        for c in range(XBC // LANE):
            lanes = slice(c * LANE, (c + 1) * LANE)
            o_ref[:, lanes] = _conv_tr(dext, w_ref, SK, hb, lanes).astype(ACT)
            _conv_dw(dw_ref, dc_ref, xext, SK, hb, lanes)

    hp, hn = _halo(XBC, hb, n_rows)
    return _call(body, "cv5_bwd2", (nt,),
                 [hp, _rows(T, XBC), hn, hp, _rows(T, XBC), hn, _full(8, XBC)],
                 [_rows(T, XBC), _full(8, XBC)], [_sds((n_rows, XBC), ACT), _sds((8, XBC))],
                 scratch=[pltpu.VMEM((T + 2 * hb, XBC), F32), pltpu.VMEM((T + 2 * hb, XBC), F32)],
                 )(dcv, dcv, dcv, xbcp, xbcp, xbcp, w)


def _scan_order(nc, ncc, rev):
    if not rev:
        return lambda s: s
    return lambda s: jnp.where(s < ncc, ncc - 1 - s, nc - 1 - (s - ncc))


def _ssd_prep(dtr, sp, rev):
    dt = jax.nn.softplus(dtr + sp[0:1])
    a_neg = -jnp.exp(sp[1:2])
    r = lax.broadcasted_iota(jnp.int32, (Q, Q), 0)
    c = lax.broadcasted_iota(jnp.int32, (Q, Q), 1)
    msk = (c >= r) if rev else (c <= r)
    tri = msk.astype(F32)
    acs = jnp.dot(tri, dt * a_neg, precision=HI, preferred_element_type=F32)
    last = 0 if rev else Q - 1
    return dt, a_neg, acs, msk, tri, last


def _pair_sel(arr, lo, m, lane_lt):
    h0 = lo + 2 * m
    return jnp.where(lane_lt, arr[:, h0:h0 + 1], arr[:, h0 + 1:h0 + 2])


def _head_lanes(row, lo, g):
    lane = lax.broadcasted_iota(jnp.int32, (1, 512), 1)
    out = jnp.zeros((1, 512), F32)
    for k in range(8):
        h = lo + 8 * g + k
        out = jnp.where((lane >= 64 * k) & (lane < 64 * (k + 1)), row[:, h:h + 1], out)
    return out


def _halves(v, lane_lt):
    return jnp.concatenate([jnp.where(lane_lt, v, 0.0), jnp.where(lane_lt, 0.0, v)], axis=0)


def _ssd_fwd(xbc, dtr, sp, ncc, rev):
    n_rows = xbc.shape[0]
    nc = n_rows // Q
    lo = 16 if rev else 0
    order = _scan_order(nc, ncc, rev)

    def body(x_ref, dtr_ref, sp_ref, y_ref, hin_ref, st):
        @pl.when(pl.program_id(0) == 0)
        def _():
            st[...] = jnp.zeros_like(st)
        dt, _, acs, msk, _, last = _ssd_prep(dtr_ref[...], sp_ref[...], rev)
        acs_t, dt_t = acs.T, dt.T
        eacs = jnp.exp(acs)
        eal = jnp.exp(acs[last:last + 1, :])
        tew = jnp.exp(acs[last:last + 1, :] - acs) * dt
        lane_lt = lax.broadcasted_iota(jnp.int32, (Q, LANE), 1) < 64
        for g in range(2):
            gl = slice(g * 512, (g + 1) * 512)
            bg = x_ref[:, 1024 + g * 128:1152 + g * 128]
            cg = x_ref[:, 1280 + g * 128:1408 + g * 128]
            s_g = _nt(cg, bg)
            h_t = st[:, gl]
            hin_ref[:, gl] = h_t
            yoff = _nn(cg, h_t)
            xw = []
            for mm in range(4):
                m = 4 * g + mm
                ls = slice(m * LANE, (m + 1) * LANE)
                x2 = x_ref[:, ls]
                ws = []
                for hh in range(2):
                    h = lo + 2 * m + hh
                    lm = jnp.exp(jnp.where(msk, acs[:, h:h + 1] - acs_t[h:h + 1, :], -jnp.inf))
                    ws.append(s_g * lm * dt_t[h:h + 1, :])
                y2 = _nn(jnp.concatenate(ws, axis=1), _halves(x2, lane_lt))
                y_ref[:, ls] = y2 + yoff[:, mm * LANE:(mm + 1) * LANE] * _pair_sel(eacs, lo, m, lane_lt)
                xw.append(x2 * _pair_sel(tew, lo, m, lane_lt))
            st[:, gl] = _head_lanes(eal, lo, g) * h_t + _tn_dot(bg, jnp.concatenate(xw, axis=1))

    return _call(body, "ssd_fwd_r" if rev else "ssd_fwd_f", (nc,),
                 [pl.BlockSpec((Q, XBC), lambda s: (order(s), 0)), pl.BlockSpec((Q, LANE), lambda s: (order(s), 0)),
                  _full(8, LANE)],
                 [pl.BlockSpec((Q, D), lambda s: (order(s), 0)), pl.BlockSpec((None, LANE, D), lambda s: (order(s), 0, 0))],
                 [_sds((n_rows, D)), _sds((nc, LANE, D))], scratch=[pltpu.VMEM((LANE, D), F32)])(xbc, dtr, sp)


def _ssd_bwd(xbc, dtr, dy, hin, sp, dl, eh, ncc, rev):
    n_rows = xbc.shape[0]
    nc = n_rows // Q
    lo = 16 if rev else 0
    fwd_order = _scan_order(nc, ncc, rev)
    order = lambda s: fwd_order(nc - 1 - s)
    with_skip = not rev

    def body(x_ref, dtr_ref, dy_ref, hin_ref, sp_ref, dl_ref, eh_ref, dx_ref, ddtr_ref, pg_ref, dst):
        @pl.when(pl.program_id(0) == 0)
        def _():
            dst[...] = jnp.zeros_like(dst)
            pg_ref[...] = jnp.zeros_like(pg_ref)
        dtr_v = dtr_ref[...]
        dt, a_neg, acs, msk, tri, last = _ssd_prep(dtr_v, sp_ref[...], rev)
        acs_t = acs.T
        r = lax.broadcasted_iota(jnp.int32, (Q, Q), 0)
        c = lax.broadcasted_iota(jnp.int32, (Q, Q), 1)
        msk_t = (c <= r) if rev else (c >= r)
        eacs = jnp.exp(acs)
        eal = jnp.exp(acs[last:last + 1, :])
        te = jnp.exp(acs[last:last + 1, :] - acs)
        lane = lax.broadcasted_iota(jnp.int32, (Q, LANE), 1)
        lane1 = lax.broadcasted_iota(jnp.int32, (1, LANE), 1)
        lane_lt = lane < 64
        dacs = jnp.zeros((Q, LANE), F32)
        ddt_x = jnp.zeros((Q, LANE), F32)
        dlast = jnp.zeros((1, LANE), F32)
        hs_rows = []
        sub16 = lax.broadcasted_iota(jnp.int32, (16, Q), 0)
        dacs_t = jnp.zeros((16, Q), F32)
        for g in range(2):
            gl = slice(g * 512, (g + 1) * 512)
            bg = x_ref[:, 1024 + g * 128:1152 + g * 128]
            cg = x_ref[:, 1280 + g * 128:1408 + g * 128]
            s_g = _nt(cg, bg)
            s_gt = _nt(bg, cg)
            h_t, dh_t = hin_ref[:, gl], dst[:, gl]
            bh = _nn(bg, dh_t)
            yoff = _nn(cg, h_t)
            d_s = jnp.zeros((Q, Q), F32)
            edy, exd = [], []
            for mm in range(4):
                m = 4 * g + mm
                ls = slice(m * LANE, (m + 1) * LANE)
                x2, dy2 = x_ref[:, ls], dy_ref[:, ls]
                bh2 = bh[:, mm * LANE:(mm + 1) * LANE]
                dtm, em, eam = (_pair_sel(v, lo, m, lane_lt) for v in (dt, te, eacs))
                xd2 = x2 * dtm
                lms, mts = [], []
                for hh in range(2):
                    h = lo + 2 * m + hh
                    col, row = acs[:, h:h + 1], acs_t[h:h + 1, :]
                    lms.append(jnp.exp(jnp.where(msk, col - row, -jnp.inf)))
                    mts.append(s_gt * jnp.exp(jnp.where(msk_t, row - col, -jnp.inf)))
                dy_st = _halves(dy2, lane_lt)
                dxd2 = em * bh2 + _nn(jnp.concatenate(mts, axis=1), dy_st)
                dm_st = _nt(dy_st, xd2)
                dmt_st = _nt(_halves(xd2, lane_lt), dy2)
                d_s = d_s + dm_st[:Q] * lms[0] + dm_st[Q:] * lms[1]
                v1, v2, v3 = dy2 * yoff[:, mm * LANE:(mm + 1) * LANE] * eam, dxd2 * x2, xd2 * bh2 * em
                for hh in range(2):
                    h = lo + 2 * m + hh
                    half = lane_lt == (hh == 0)
                    g_rows = _sum0(dmt_st[hh * Q:(hh + 1) * Q] * mts[hh]) - _sum0(dm_st[hh * Q:(hh + 1) * Q] * s_g * lms[hh])
                    dacs_t = jnp.where(sub16 == 2 * m + hh, g_rows, dacs_t)
                    r1 = jnp.sum(jnp.where(half, v1, 0.0), axis=1, keepdims=True)
                    r2 = jnp.sum(jnp.where(half, v2, 0.0), axis=1, keepdims=True)
                    r3 = jnp.sum(jnp.where(half, v3, 0.0), axis=1, keepdims=True)
                    dacs = dacs + jnp.where(lane == h, r1 - r3, 0.0)
                    ddt_x = ddt_x + jnp.where(lane == h, r2, 0.0)
                    dlast = dlast + jnp.where(lane1 == h, _sum0(r3), 0.0)
                dx2 = dxd2 * dtm
                if with_skip:
                    dx2 = dx2 + dl_ref[:, ls] * dy2
                dx_ref[:, ls] = dx2
                edy.append(eam * dy2)
                exd.append(em * xd2)
            edy, exd = jnp.concatenate(edy, axis=1), jnp.concatenate(exd, axis=1)
            hs_rows.append(_sum0(h_t * dh_t))
            dst[:, gl] = _head_lanes(eal, lo, g) * dh_t + _tn_dot(cg, edy)
            dx_ref[:, 1024 + g * 128:1152 + g * 128] = _tn_dot(d_s, cg) + _nt(exd, dh_t)
            dx_ref[:, 1280 + g * 128:1408 + g * 128] = _nn(d_s, bg) + _nt(edy, h_t)
        hs = jnp.broadcast_to(jnp.concatenate(hs_rows, axis=1), (8, D))
        hsum = jnp.dot(hs, eh_ref[...], precision=HI, preferred_element_type=F32)[0:1]
        dlast = dlast + eal * hsum
        dacs = dacs + jnp.concatenate([jnp.zeros((lo, Q), F32)] * (lo > 0) + [dacs_t, jnp.zeros((LANE - 16 - lo, Q), F32)],
                                      axis=0).T
        rowi = lax.broadcasted_iota(jnp.int32, (Q, LANE), 0)
        dacs = dacs + jnp.where(rowi == last, dlast, 0.0)
        da = lax.dot_general(tri, dacs, (((0,), (0,)), ((), ())), precision=HI, preferred_element_type=F32)
        ddt = ddt_x + da * a_neg
        mine = (lane >= lo) & (lane < lo + 16)
        ddtr = jnp.where(mine, ddt * jax.nn.sigmoid(dtr_v + sp_ref[0:1]), 0.0)
        ddtr_ref[...] = ddtr
        pg_ref[0:1] += _sum0(ddtr)
        pg_ref[1:2] += jnp.where(mine[0:1], _sum0(da * dt) * a_neg, 0.0)

    blk = lambda w_: pl.BlockSpec((Q, w_), lambda s: (order(s), 0))
    return _call(body, "ssd_bwd_r" if rev else "ssd_bwd_f", (nc,),
                 [blk(XBC), blk(LANE), blk(D), pl.BlockSpec((None, LANE, D), lambda s: (order(s), 0, 0)),
                  _full(8, LANE), _full(1, D), _full(D, LANE)],
                 [blk(XBC), blk(LANE), _full(8, LANE)],
                 [_sds((n_rows, XBC)), _sds((n_rows, LANE)), _sds((8, LANE))],
                 scratch=[pltpu.VMEM((LANE, D), F32)])(xbc, dtr, dy, hin, sp, dl, eh)


def _ssd_fwd_old(xbc, dtr, sp, ncc, rev):
    n_rows = xbc.shape[0]
    nc = n_rows // Q
    lo = 16 if rev else 0
    order = _scan_order(nc, ncc, rev)

    def body(x_ref, dtr_ref, sp_ref, y_ref, hin_ref, st):
        @pl.when(pl.program_id(0) == 0)
        def _():
            st[...] = jnp.zeros_like(st)
        dt, _, acs, msk, _, last = _ssd_prep(dtr_ref[...], sp_ref[...], rev)
        acs_t, dt_t = acs.T, dt.T
        eacs = jnp.exp(acs)
        eal = jnp.exp(acs[last:last + 1, :])
        tew = jnp.exp(acs[last:last + 1, :] - acs) * dt
        lane_lt = lax.broadcasted_iota(jnp.int32, (Q, LANE), 1) < 64
        row_lt = lax.broadcasted_iota(jnp.int32, (LANE, 1), 0) < 64
        s_g = [_nt(x_ref[:, 1280 + g * 128:1408 + g * 128], x_ref[:, 1024 + g * 128:1152 + g * 128]) for g in range(2)]
        for m in range(NPAIR):
            g = m // 4
            ls = slice(m * LANE, (m + 1) * LANE)
            x2 = x_ref[:, ls]
            bg = x_ref[:, 1024 + g * 128:1152 + g * 128]
            cg = x_ref[:, 1280 + g * 128:1408 + g * 128]
            y2 = jnp.zeros((Q, LANE), F32)
            for hh in range(2):
                h = lo + 2 * m + hh
                lm = jnp.exp(jnp.where(msk, acs[:, h:h + 1] - acs_t[h:h + 1, :], -jnp.inf))
                w = s_g[g] * lm * dt_t[h:h + 1, :]
                y2 = y2 + _nn(w, jnp.where(lane_lt == (hh == 0), x2, 0.0))
            hp = st[ls, :]
            hin_ref[ls, :] = hp
            y_ref[:, ls] = y2 + _nt(cg, hp) * _pair_sel(eacs, lo, m, lane_lt)
            snew = _tn_dot(x2 * _pair_sel(tew, lo, m, lane_lt), bg)
            h0 = lo + 2 * m
            st[ls, :] = jnp.where(row_lt, eal[:, h0:h0 + 1], eal[:, h0 + 1:h0 + 2]) * hp + snew

    return _call(body, "ssd_fwd_r" if rev else "ssd_fwd_f", (nc,),
                 [pl.BlockSpec((Q, XBC), lambda s: (order(s), 0)), pl.BlockSpec((Q, LANE), lambda s: (order(s), 0)),
                  _full(8, LANE)],
                 [pl.BlockSpec((Q, D), lambda s: (order(s), 0)), pl.BlockSpec((None, D, LANE), lambda s: (order(s), 0, 0))],
                 [_sds((n_rows, D)), _sds((nc, D, LANE))], scratch=[pltpu.VMEM((D, LANE), F32)])(xbc, dtr, sp)


def _ssd_bwd_old(xbc, dtr, dy, hin, sp, dl, ncc, rev):
    n_rows = xbc.shape[0]
    nc = n_rows // Q
    lo = 16 if rev else 0
    fwd_order = _scan_order(nc, ncc, rev)
    order = lambda s: fwd_order(nc - 1 - s)
    with_skip = not rev

    def body(x_ref, dtr_ref, dy_ref, hin_ref, sp_ref, dl_ref, dx_ref, ddtr_ref, pg_ref, dst):
        @pl.when(pl.program_id(0) == 0)
        def _():
            dst[...] = jnp.zeros_like(dst)
            pg_ref[...] = jnp.zeros_like(pg_ref)
        dtr_v = dtr_ref[...]
        dt, a_neg, acs, msk, tri, last = _ssd_prep(dtr_v, sp_ref[...], rev)
        acs_t, dt_t = acs.T, dt.T
        eacs = jnp.exp(acs)
        eal = jnp.exp(acs[last:last + 1, :])
        te = jnp.exp(acs[last:last + 1, :] - acs)
        lane = lax.broadcasted_iota(jnp.int32, (Q, LANE), 1)
        sub = lax.broadcasted_iota(jnp.int32, (LANE, Q), 0)
        lane1 = lax.broadcasted_iota(jnp.int32, (1, LANE), 1)
        lane_lt = lane < 64
        row_lt = lax.broadcasted_iota(jnp.int32, (LANE, 1), 0) < 64
        bgs = [x_ref[:, 1024 + g * 128:1152 + g * 128] for g in range(2)]
        cgs = [x_ref[:, 1280 + g * 128:1408 + g * 128] for g in range(2)]
        s_g = [_nt(cgs[g], bgs[g]) for g in range(2)]
        d_s = [jnp.zeros((Q, Q), F32), jnp.zeros((Q, Q), F32)]
        dc_x = [jnp.zeros((Q, LANE), F32), jnp.zeros((Q, LANE), F32)]
        db_x = [jnp.zeros((Q, LANE), F32), jnp.zeros((Q, LANE), F32)]
        dacs = jnp.zeros((Q, LANE), F32)
        colsum_t = jnp.zeros((LANE, Q), F32)
        ddt_x = jnp.zeros((Q, LANE), F32)
        dlast = jnp.zeros((1, LANE), F32)
        for m in range(NPAIR):
            g = m // 4
            ls = slice(m * LANE, (m + 1) * LANE)
            x2, dy2 = x_ref[:, ls], dy_ref[:, ls]
            hp, dhp = hin_ref[ls, :], dst[ls, :]
            dtm, em, eam = (_pair_sel(v, lo, m, lane_lt) for v in (dt, te, eacs))
            xd2 = x2 * dtm
            bh = _nt(bgs[g], dhp)
            ch = _nt(cgs[g], hp)
            dxd2 = em * bh
            for hh in range(2):
                h = lo + 2 * m + hh
                half = lane_lt == (hh == 0)
                lm = jnp.exp(jnp.where(msk, acs[:, h:h + 1] - acs_t[h:h + 1, :], -jnp.inf))
                mh = s_g[g] * lm
                dyh = jnp.where(half, dy2, 0.0)
                dxd2 = dxd2 + _tn_dot(mh, dyh)
                dm = _nt(dyh, jnp.where(half, xd2, 0.0))
                d_s[g] = d_s[g] + dm * lm
                gh = dm * mh
                dacs = dacs + jnp.where(lane == h, jnp.sum(gh, axis=1, keepdims=True), 0.0)
                colsum_t = colsum_t + jnp.where(sub == h, jnp.sum(gh, axis=0, keepdims=True), 0.0)
                t1 = jnp.sum(jnp.where(half, dy2 * ch * eam, 0.0), axis=1, keepdims=True)
                rj = jnp.sum(jnp.where(half, xd2 * bh * em, 0.0), axis=1, keepdims=True)
                dacs = dacs + jnp.where(lane == h, t1 - rj, 0.0)
                hs = hp * dhp
                hsum = jnp.sum(jnp.sum(jnp.where(row_lt == (hh == 0), hs, 0.0), axis=1, keepdims=True), axis=0, keepdims=True)
                dlast = dlast + jnp.where(lane1 == h, jnp.sum(rj, axis=0, keepdims=True) + eal[:, h:h + 1] * hsum, 0.0)
            for hh in range(2):
                h = lo + 2 * m + hh
                half = lane_lt == (hh == 0)
                ddt_x = ddt_x + jnp.where(lane == h, jnp.sum(jnp.where(half, dxd2 * x2, 0.0), axis=1, keepdims=True), 0.0)
            dx2 = dxd2 * dtm
            if with_skip:
                dx2 = dx2 + dl_ref[:, ls] * dy2
            dx_ref[:, ls] = dx2
            edy = eam * dy2
            dc_x[g] = dc_x[g] + _nn(edy, hp)
            db_x[g] = db_x[g] + _nn(em * xd2, dhp)
            h0 = lo + 2 * m
            dst[ls, :] = jnp.where(row_lt, eal[:, h0:h0 + 1], eal[:, h0 + 1:h0 + 2]) * dhp + _tn_dot(edy, cgs[g])
        dacs = dacs - colsum_t.T
        rowi = lax.broadcasted_iota(jnp.int32, (Q, LANE), 0)
        dacs = dacs + jnp.where(rowi == last, dlast, 0.0)
        da = lax.dot_general(tri, dacs, (((0,), (0,)), ((), ())), precision=HI, preferred_element_type=F32)
        ddt = ddt_x + da * a_neg
        mine = (lane >= lo) & (lane < lo + 16)
        ddtr = jnp.where(mine, ddt * jax.nn.sigmoid(dtr_v + sp_ref[0:1]), 0.0)
        ddtr_ref[...] = ddtr
        pg_ref[0:1] += _sum0(ddtr)
        pg_ref[1:2] += jnp.where(mine[0:1], _sum0(da * dt) * a_neg, 0.0)
        for g in range(2):
            dx_ref[:, 1024 + g * 128:1152 + g * 128] = _tn_dot(d_s[g], cgs[g]) + db_x[g]
            dx_ref[:, 1280 + g * 128:1408 + g * 128] = _nn(d_s[g], bgs[g]) + dc_x[g]

    return _call(body, "ssd_bwd_r" if rev else "ssd_bwd_f", (nc,),
                 [pl.BlockSpec((Q, XBC), lambda s: (order(s), 0)), pl.BlockSpec((Q, LANE), lambda s: (order(s), 0)),
                  pl.BlockSpec((Q, D), lambda s: (order(s), 0)), pl.BlockSpec((None, D, LANE), lambda s: (order(s), 0, 0)),
                  _full(8, LANE), _full(1, D)],
                 [pl.BlockSpec((Q, XBC), lambda s: (order(s), 0)), pl.BlockSpec((Q, LANE), lambda s: (order(s), 0)),
                  _full(8, LANE)],
                 [_sds((n_rows, XBC)), _sds((n_rows, LANE)), _sds((8, LANE))],
                 scratch=[pltpu.VMEM((D, LANE), F32)])(xbc, dtr, dy, hin, sp, dl)


def _hy4_fwd(h, yf, yb, xbc, z, ysg, mp, dl, ng, wout, nct):
    n_rows = h.shape[0]

    def body(h_ref, yf_ref, yb_ref, xs_ref, z_ref, ysg_ref, mp_ref, dl_ref, ng_ref, wout_ref, hn_ref, yssd_ref, out_ref):
        ytot = yf_ref[...] + yb_ref[...] + dl_ref[...] * xs_ref[...]
        yssd = _gate_norm(ytot, z_ref[...], ng_ref[...]).astype(MXU)
        yssd_ref[...] = yssd.astype(ACT)
        out = (jnp.dot(yssd, wout_ref[0:D, :], preferred_element_type=F32)
               + jnp.dot(ysg_ref[...].astype(MXU), wout_ref[D:2 * D, :], preferred_element_type=F32))
        out_ref[...] = out
        hn_ref[...] = h_ref[...] + mp_ref[3:4] * out

    return _call(body, "hy4_fwd", (n_rows // T,),
                 [_rows(T, D), _rows(T, D, nct), _rows(T, D, nct), _rows(T, D, nct), _rows(T, D, nct), _rows(T, D, nct),
                  _full(8, D), _full(1, D), _full(1, D), _wfull(2 * D, D)],
                 [_rows(T, D), _rows(T, D), _rows(T, D)],
                 [_sds((n_rows, D)), _sds((n_rows, D), ACT), _sds((n_rows, D))])(h, yf, yb, xbc, z, ysg, mp, dl, ng, wout)


def _hy4_bwd(dh, out, yf, yb, xbc, z, mp, dl, ng, wout, nct):
    n_lat = dh.shape[0]
    n_rows = yf.shape[0]

    def body(dh_ref, out_ref, yf_ref, yb_ref, xs_ref, z_ref, mp_ref, dl_ref, ng_ref, wout_ref,
             dy_ref, dz_ref, dysg_ref, doutb_ref, pg_ref):
        i = pl.program_id(0)

        @pl.when(i == 0)
        def _():
            pg_ref[...] = jnp.zeros_like(pg_ref)

        @pl.when(i < nct)
        def _():
            dy_ref[...] = jnp.zeros_like(dy_ref)
            dz_ref[...] = jnp.zeros_like(dz_ref)
            dysg_ref[...] = jnp.zeros_like(dysg_ref)
            doutb_ref[...] = jnp.zeros_like(doutb_ref)

        @pl.when(i >= nct)
        def _():
            dhp = dh_ref[...]
            doutb = (mp_ref[3:4] * dhp).astype(MXU)
            doutb_ref[...] = doutb.astype(ACT)
            dysg_ref[...] = _nt(doutb, wout_ref[D:2 * D, :])
            dyssd = _nt(doutb, wout_ref[0:D, :])
            xs = xs_ref[...]
            ytot = yf_ref[...] + yb_ref[...] + dl_ref[...] * xs
            _, vjp = jax.vjp(_gate_norm, ytot, z_ref[...], ng_ref[...])
            dytot, dz, dng = vjp(dyssd)
            dy_ref[...] = dytot
            dz_ref[...] = dz.astype(ACT)
            pg_ref[0:1] += _sum0(dhp * out_ref[...])
            pg_ref[1:2] += dng
            pg_ref[2:3] += _sum0(dytot * xs)

    return _call(body, "hy4_bwd", (n_rows // T,),
                 [_rows_lat(T, D, nct), _rows_lat(T, D, nct), _rows(T, D), _rows(T, D), _rows(T, D), _rows(T, D),
                  _full(8, D), _full(1, D), _full(1, D), _wfull(2 * D, D)],
                 [_rows(T, D), _rows(T, D), _rows(T, D), _rows_lat(T, D, nct), _full(8, D)],
                 [_sds((n_rows, D)), _sds((n_rows, D), ACT), _sds((n_rows, D)), _sds((n_lat, D), ACT), _sds((8, D))],
                 )(dh, out, yf, yb, xbc, z, mp, dl, ng, wout)


def _loss_bwd(h, tgt, fng):
    n_rows = h.shape[0]

    def body(h_ref, t_ref, g_ref, dh_ref, pg_ref, ls_ref):
        @pl.when(pl.program_id(0) == 0)
        def _():
            pg_ref[...] = jnp.zeros_like(pg_ref)
            ls_ref[...] = jnp.zeros_like(ls_ref)
        hv = h_ref[...]
        g = g_ref[...]
        r = lax.rsqrt(jnp.mean(hv * hv, axis=-1, keepdims=True) + EPS)
        n = hv * r
        e = n * g - t_ref[...]
        ls_ref[...] += 0.5 * jnp.sum(jnp.sum(e * e, axis=1, keepdims=True), axis=0, keepdims=True) * (1.0 / D)
        dyv = e * (1.0 / D)
        pg_ref[0:1] += _sum0(dyv * n)
        dn = dyv * g
        dh_ref[...] = r * (dn - n * jnp.mean(dn * n, axis=-1, keepdims=True))

    return _call(body, "loss_bwd", (n_rows // T,), [_rows(T, D), _rows(T, D), _full(1, D)],
                 [_rows(T, D), _full(8, D), _full(8, LANE)],
                 [_sds((n_rows, D)), _sds((8, D)), _sds((8, LANE))])(h, tgt, fng)


def _pad_rows(a, rows):
    return jnp.concatenate([a, jnp.zeros((rows - a.shape[0],) + a.shape[1:], a.dtype)], axis=0)


def _mp(*rows):
    return _pad_rows(jnp.stack(rows, axis=0), 8)


def _local_step(x, ctx, tgt, ada, cada0, w, late_w=None, early_grads=None):
    n_lat, n_ctx = x.shape[0], ctx.shape[0]
    nct, ncc = n_ctx // T, n_ctx // Q
    a0 = [ada[0, k * D:(k + 1) * D] for k in range(6)]
    a1 = [ada[1, k * D:(k + 1) * D] for k in range(6)]
    c0 = [cada0[k * D:(k + 1) * D] for k in range(6)]
    g = {}

    hcat = jnp.concatenate([ctx, x], axis=0)
    mp2 = jnp.stack([_mp(w["norm_mix_g"][0], c0[0], c0[1]), _mp(w["norm_mix_g"][0], a0[0], a0[1], a0[2])], axis=0)
    mp_l0 = mp2[1]
    sgbt = _pad_cols(w["sg_b"][0].T, LANE)
    lng, lnb = w["sg_ln_g"][0][None], w["sg_ln_b"][0][None]
    z, uv, xbcp, dtr, ysg = _hy1_fwd(hcat, mp2, w["wz"], w["wuv"], w["wxbc"], w["wdt"], lng, lnb, w["sg_w"], sgbt, nct)
    cw = _pad_rows(w["ssd_conv_w"][0], 8)
    cb = w["ssd_conv_b"][0][None]
    xbc = _cv5_fwd(xbcp, cw, cb, nct)
    sp = _pad_rows(jnp.stack([_pad_cols(w["ssd_dt_bias"][0].reshape(1, 32), LANE)[0],
                              _pad_cols(w["ssd_a_log"][0].reshape(1, 32), LANE)[0]], axis=0), 8)
    dl = jnp.repeat(w["ssd_d"][0], 64)[None]
    ng = w["ssd_norm_g"][0][None]
    yf, hin_f = _ssd_fwd(xbc, dtr, sp, ncc, False)
    yb, hin_b = _ssd_fwd(xbc, dtr, sp, ncc, True)
    if late_w is not None:
        w = {**w, **late_w(yb)}
    h1, yssd, out0 =_hy4_fwd(x, yf, yb, xbc, z, ysg, mp_l0, dl, ng, w["hy_w_out"], nct)

    mpm0 = _mp(w["norm_mlp_g"][0], a0[3], a0[4], a0[5])
    h2, am0, ym0 = _mlp_fwd(h1, mpm0, w["mlp_w1"][0], w["mlp_w2"][0], "mlp0_fwd")

    mpc = _mp(w["norm_mix_g"][1], a1[0], a1[1], a1[2])
    wdw = _pad_rows(w["cf_w_dw"][0], 32)
    glu, acf = _cf1_fwd(h2, mpc, w["cf_w_pw1"], w["cf_b_pw1"])
    h3, cv, scf, ycf = _cf2_fwd(h2, glu, mpc, wdw, w["cf_b_dw"], w["cf_ln_g"], w["cf_ln_b"], w["cf_w_pw2"], w["cf_b_pw2"])

    mpm1 = _mp(w["norm_mlp_g"][1], a1[3], a1[4], a1[5])
    h4, am1, ym1 = _mlp_fwd(h3, mpm1, w["mlp_w1"][1], w["mlp_w2"][1], "mlp1_fwd")

    dh4, pg_f, ls = _loss_bwd(h4, tgt, w["final_norm_g"][None])
    loss = ls[0, 0]
    g["final_norm_g"] = pg_f[0]

    dh3, da1, dy1, u1, pgm1 = _mlp_bwd(dh4, h3, am1, ym1, mpm1, w["mlp_w1"][1], w["mlp_w2"][1], "mlp1_bwd")
    gw1_1 = _tn(u1, da1, "tn_mlp1_w1")
    gw2_1 = _tn(am1, dy1, "tn_mlp1_w2", relu2=True)

    dcv, dycf, pgc2 = _cf2_bwd(dh3, ycf, cv, mpc, w["cf_ln_g"], w["cf_ln_b"], w["cf_w_pw2"])
    g["cf_w_pw2"] = _tn(scf, dycf, "tn_cf_pw2")
    dh2, dacf, ucf, pgc1, pbc1, dwdw = _cf1_bwd(dh3, h2, acf, dcv, glu, mpc, wdw, w["cf_w_pw1"])
    g["cf_w_pw1"] = _tn(ucf, dacf, "tn_cf_pw1")
    g["cf_b_pw2"], g["cf_ln_g"], g["cf_ln_b"], g["cf_b_dw"] = pgc2[1], pgc2[2], pgc2[3], pgc2[4]
    g["cf_b_pw1"] = pbc1[0]
    g["cf_w_dw"] = dwdw[:CK]

    dh1, da0, dy0, u0, pgm0 = _mlp_bwd(dh2, h1, am0, ym0, mpm0, w["mlp_w1"][0], w["mlp_w2"][0], "mlp0_bwd")
    g["mlp_w1"] = jnp.stack([_tn(u0, da0, "tn_mlp0_w1"), gw1_1])
    g["mlp_w2"] = jnp.stack([_tn(am0, dy0, "tn_mlp0_w2", relu2=True), gw2_1])
    g["norm_mlp_g"] = jnp.stack([pgm0[0], pgm1[0]])
    if early_grads is not None:
        mp_l0 = mp_l0 + early_grads(g)

    dyt, dz, dysg, doutb, pg4 = _hy4_bwd(dh1, out0, yf, yb, xbc, z, mp_l0, dl, ng, w["hy_w_out"], nct)
    ysg_lat = lax.slice_in_dim(ysg, n_ctx, n_ctx + n_lat, axis=0)
    g["hy_w_out"] = jnp.concatenate([_tn(yssd, doutb, "tn_out_ssd"), _tn(ysg_lat, doutb, "tn_out_sg")], axis=0)
    head_of_lane = jnp.arange(D, dtype=jnp.int32)[:, None] // 64
    col = jnp.arange(LANE, dtype=jnp.int32)[None, :]
    dxf, ddf, pgsf = _ssd_bwd(xbc, dtr, dyt, hin_f, sp, dl, (col == head_of_lane).astype(F32), ncc, False)
    dxb, ddb, pgsb = _ssd_bwd(xbc, dtr, dyt, hin_b, sp, dl, (col == head_of_lane + 16).astype(F32), ncc, True)
    dcv5, pgcb = _cv5_bwd1(xbcp, dxf, dxb, cw, cb, nct)
    dxbcp, dcw = _cv5_bwd2(dcv5, xbcp, cw, nct)
    dx, ucat, duv, ddt, pg2, pln, dsgw, dsgbt = _hy1_bwd(
        hcat, uv, dz, dxbcp, ddf, ddb, dysg, dh1, mp2, w["wz"], w["wuv"], w["wxbc"], w["wdt"], lng, lnb, w["sg_w"], sgbt, nct)
    g["hy_w_in"] = jnp.concatenate([_tn(ucat, dz, "tn_in_z"), _tn(ucat, dxbcp, "tn_in_xbc"),
                                    _tn(ucat, ddt, "tn_in_dt")[:, :32], _tn(ucat, duv, "tn_in_uv")], axis=1)
    g["ssd_conv_w"], g["ssd_conv_b"] = dcw[:SK], pgcb[0]
    pgs = pgsf + pgsb
    g["ssd_dt_bias"], g["ssd_a_log"] = pgs[0, :32].reshape(2, 16), pgs[1, :32].reshape(2, 16)
    g["ssd_d"] = jnp.sum(pg4[2].reshape(16, 64), axis=1)
    g["ssd_norm_g"] = pg4[1]
    g["sg_ln_g"], g["sg_ln_b"] = pln[0], pln[1]
    g["sg_w"], g["sg_b"] = dsgw, dsgbt[:, :SGG].T
    g["norm_mix_g"] = jnp.stack([pg2[0, 0] + pg2[1, 0], pgc1[0]])

    zero = jnp.zeros((D,), F32)
    d_ada = jnp.stack([jnp.concatenate([pg2[1, 1], pg2[1, 2], pg4[0], pgm0[1], pgm0[2], pgm0[3]]),
                       jnp.concatenate([pgc1[1], pgc1[2], pgc2[0], pgm1[1], pgm1[2], pgm1[3]])])
    d_cada0 = jnp.concatenate([pg2[0, 1], pg2[0, 2], zero, zero, zero, zero])
    return loss, dx, g, d_ada, d_cada0


def _pad_cols(a, cols):
    return jnp.concatenate([a, jnp.zeros(a.shape[:-1] + (cols - a.shape[-1],), a.dtype)], axis=-1)


MESH = pl.DeviceIdType.MESH
ANY = pl.BlockSpec(memory_space=pl.ANY)
IN_VMEM = pl.BlockSpec(memory_space=pltpu.VMEM)


def _coords():
    return lax.axis_index("x"), lax.axis_index("y"), lax.axis_index("c")


def _ag8(x, name):
    r, wd = x.shape

    def body(x_ref, o_ref, send, recv, lsem):
        mx, my, mc = _coords()
        me = 4 * mx + 2 * my + mc
        mine = pltpu.make_async_copy(x_ref, o_ref.at[me], lsem)
        mine.start()
        sent, peers = [], []
        for k in range(1, 8):
            px = 1 - mx if k & 4 else mx
            py = 1 - my if k & 2 else my
            pc = 1 - mc if k & 1 else mc
            cp = pltpu.make_async_remote_copy(src_ref=x_ref, dst_ref=o_ref.at[me], send_sem=send.at[k - 1],
                                              recv_sem=recv.at[k - 1], device_id=(px, py, pc), device_id_type=MESH)
            cp.start()
            sent.append(cp)
            peers.append((4 * px + 2 * py + pc, (px, py, pc)))
        for k in range(1, 8):
            slot, peer = peers[k - 1]
            pltpu.make_async_remote_copy(src_ref=x_ref, dst_ref=o_ref.at[slot], send_sem=send.at[k - 1],
                                         recv_sem=recv.at[k - 1], device_id=peer, device_id_type=MESH).wait_recv()
        for cp in sent:
            cp.wait_send()
        mine.wait()

    return pl.pallas_call(
        body, name=name, out_shape=_sds((8, r, wd), x.dtype), in_specs=[IN_VMEM], out_specs=IN_VMEM,
        scratch_shapes=[pltpu.SemaphoreType.DMA((7,)), pltpu.SemaphoreType.DMA((7,)), pltpu.SemaphoreType.DMA(())],
        compiler_params=pltpu.CompilerParams(vmem_limit_bytes=VMEM_LIMIT))(x)


def _xchg4(buf, name, a2a):
    r, wd = buf.shape[-2:]

    def body(in_ref, o_ref, send, recv, lsem):
        mx, my, mc = _coords()
        me = 2 * mx + my
        mine = pltpu.make_async_copy(in_ref.at[me] if a2a else in_ref, o_ref.at[me], lsem)
        mine.start()
        sent, peers = [], []
        for k in range(1, 4):
            px = 1 - mx if k & 2 else mx
            py = 1 - my if k & 1 else my
            pj = 2 * px + py
            cp = pltpu.make_async_remote_copy(src_ref=in_ref.at[pj] if a2a else in_ref, dst_ref=o_ref.at[me],
                                              send_sem=send.at[k - 1], recv_sem=recv.at[k - 1],
                                              device_id=(px, py, mc), device_id_type=MESH)
            cp.start()
            sent.append(cp)
            peers.append((pj, (px, py, mc)))
        for k in range(1, 4):
            pj, peer = peers[k - 1]
            pltpu.make_async_remote_copy(src_ref=in_ref.at[pj] if a2a else in_ref, dst_ref=o_ref.at[pj],
                                         send_sem=send.at[k - 1], recv_sem=recv.at[k - 1],
                                         device_id=peer, device_id_type=MESH).wait_recv()
        for cp in sent:
            cp.wait_send()
        mine.wait()

    return pl.pallas_call(
        body, name=name, out_shape=_sds((4, r, wd), buf.dtype), in_specs=[ANY], out_specs=ANY,
        scratch_shapes=[pltpu.SemaphoreType.DMA((3,)), pltpu.SemaphoreType.DMA((3,)), pltpu.SemaphoreType.DMA(())],
        )(buf)


HBM = pl.BlockSpec(memory_space=pltpu.HBM)
SEM = pl.BlockSpec(memory_space=pltpu.SEMAPHORE)
EFFECT = pltpu.SideEffectType.DATAFLOW_SIDE_EFFECTING


def _x4_peers(in_ref, land_ref, send, recv, a2a):
    mx, my, mc = _coords()
    me = 2 * mx + my
    out = []
    for k in range(1, 4):
        px = 1 - mx if k & 2 else mx
        py = 1 - my if k & 1 else my
        pj = 2 * px + py
        mk = functools.partial(pltpu.make_async_remote_copy, src_ref=in_ref.at[pj] if a2a else in_ref,
                               send_sem=send.at[k - 1], recv_sem=recv.at[k - 1], device_id=(px, py, mc), device_id_type=MESH)
        out.append((mk(dst_ref=land_ref.at[me]), mk(dst_ref=land_ref.at[pj])))
    return out


def _x4_start(buf, name, a2a):
    r, wd = buf.shape[-2:]

    def body(in_ref, land_ref, send, recv, in_thru, land_thru, token):
        for start, _ in _x4_peers(in_ref, land_ref, send, recv, a2a):
            start.start()
        token[...] = jnp.zeros_like(token)

    land = lax.empty((4, r, wd), buf.dtype)
    return pl.pallas_call(
        body, name=name,
        out_shape=(pltpu.SemaphoreType.DMA((3,)), pltpu.SemaphoreType.DMA((3,)), pltpu.HBM(buf.shape, buf.dtype),
                   pltpu.HBM(land.shape, land.dtype), _sds((8, LANE))),
        in_specs=(HBM, HBM), out_specs=(SEM, SEM, HBM, HBM, IN_VMEM), input_output_aliases={0: 2, 1: 3},
        compiler_params=pltpu.CompilerParams(has_side_effects=EFFECT),
    )(pltpu.with_memory_space_constraint(buf, pltpu.HBM), pltpu.with_memory_space_constraint(land, pltpu.HBM))


def _x4_wait(send, recv, buf_thru, land_thru, after, name, a2a):
    def body(in_ref, land_ref, send_ref, recv_ref, after_ref, in_dead, got_ref):
        for _, arrive in _x4_peers(in_ref, land_ref, send_ref, recv_ref, a2a):
            arrive.wait_send()
            arrive.wait_recv()

    return pl.pallas_call(
        body, name=name, out_shape=(pltpu.HBM(buf_thru.shape, buf_thru.dtype), pltpu.HBM(land_thru.shape, land_thru.dtype)),
        in_specs=(HBM, HBM, SEM, SEM, ANY), out_specs=(HBM, HBM), input_output_aliases={0: 0, 1: 1},
        compiler_params=pltpu.CompilerParams(has_side_effects=EFFECT),
    )(buf_thru, land_thru, send, recv, after)[1]


def _xchg_sib(x, name):
    def body(in_ref, o_ref, send, recv):
        mx, my, mc = _coords()
        cp = pltpu.make_async_remote_copy(src_ref=in_ref, dst_ref=o_ref, send_sem=send, recv_sem=recv,
                                          device_id=(mx, my, 1 - mc), device_id_type=MESH)
        cp.start()
        cp.wait_recv()
        cp.wait_send()

    return pl.pallas_call(
        body, name=name, out_shape=_sds(x.shape, x.dtype), in_specs=[ANY], out_specs=ANY,
        scratch_shapes=[pltpu.SemaphoreType.DMA(()), pltpu.SemaphoreType.DMA(())])(x)


def _sum_slots(gat, slots, name, tr=None):
    n, r, wd = gat.shape
    tr = r if tr is None else tr

    def body(g_ref, o_ref):
        acc = g_ref[slots[0]].astype(F32)
        for s in slots[1:]:
            acc = acc + g_ref[s].astype(F32)
        o_ref[...] = acc

    return _call(body, name, (r // tr,), [pl.BlockSpec((n, tr, wd), lambda i: (0, i, 0))], _rows(tr, wd), _sds((r, wd)))(gat)


def _add(a, b, name, tr):
    def body(a_ref, b_ref, o_ref):
        o_ref[...] = a_ref[...] + b_ref[...]

    r, wd = a.shape
    return _call(body, name, (r // tr,), [_rows(tr, wd), _rows(tr, wd)], _rows(tr, wd), _sds((r, wd)))(a, b)


def _ada_fwd(x16, ada_w_loc, ada_b_loc):
    nloc = ada_w_loc.shape[-1]

    def body(x_ref, w_ref, b_ref, s_ref, o_ref):
        s = _silu(x_ref[...])
        s_ref[...] = s
        o_ref[...] = jnp.dot(s, w_ref[...], precision=HI, preferred_element_type=F32) + b_ref[...]

    return _call(body, "ada_fwd", (2,),
                 [_full(16, D), pl.BlockSpec((None, D, nloc), lambda l: (l, 0, 0)), pl.BlockSpec((None, 1, nloc), lambda l: (l, 0, 0))],
                 [_full(16, D), pl.BlockSpec((None, 16, nloc), lambda l: (l, 0, 0))],
                 [_sds((16, D)), _sds((2, 16, nloc))])(x16, ada_w_loc, ada_b_loc[:, None, :])


def _ada_bwd(s16, d_loc, ada_w_loc):
    nloc = ada_w_loc.shape[-1]

    def body(s_ref, d_ref, w_ref, gw_ref, cp_ref):
        gw_ref[...] = lax.dot_general(s_ref[...], d_ref[...], (((0,), (0,)), ((), ())), precision=HI,
                                      preferred_element_type=F32)

        @pl.when(pl.program_id(0) == 0)
        def _():
            cp_ref[...] = lax.dot_general(d_ref[8:16, :], w_ref[...], (((1,), (1,)), ((), ())), precision=HI,
                                          preferred_element_type=F32)

    return _call(body, "ada_bwd", (2,),
                 [_full(16, D), pl.BlockSpec((None, 16, nloc), lambda l: (l, 0, 0)), pl.BlockSpec((None, D, nloc), lambda l: (l, 0, 0))],
                 [pl.BlockSpec((None, D, nloc), lambda l: (l, 0, 0)), _full(8, D)],
                 [_sds((2, D, nloc)), _sds((8, D))])(s16, d_loc, ada_w_loc)


def _cctx_grad(dscc, c_ctx):
    def body(d_ref, c_ref, o_ref):
        _, vjp = jax.vjp(_silu, c_ref[...])
        o_ref[...] = vjp(d_ref[...])[0]

    return _call(body, "cctx_grad", (1,), [_full(8, D), _full(8, D)], _full(8, D), _sds((8, D)))(dscc, c_ctx)


def _adamw(w, g, m, v, name):
    shape = w.shape
    wd = shape[-1]
    r = w.size // wd
    tr = 256 if r % 256 == 0 else r
    c1 = 1.0 - ADAM_B1 ** ADAM_STEP
    c2 = 1.0 - ADAM_B2 ** ADAM_STEP

    def body(w_ref, g_ref, m_ref, v_ref, d_ref, mo_ref, vo_ref):
        gv = g_ref[...]
        mn = ADAM_B1 * m_ref[...] + (1.0 - ADAM_B1) * gv
        vn = ADAM_B2 * v_ref[...] + (1.0 - ADAM_B2) * jnp.square(gv)
        mo_ref[...] = mn
        vo_ref[...] = vn
        d_ref[...] = -ADAM_LR * ((mn / c1) / (jnp.sqrt(vn / c2) + ADAM_EPS) + ADAM_WD * w_ref[...])

    spec = _rows(tr, wd)
    outs = _call(body, name, (r // tr,), [spec] * 4, [spec] * 3, [_sds((r, wd))] * 3)(
        *(a.reshape(r, wd) for a in (w, g, m, v)))
    return tuple(o.reshape(shape) for o in outs)


ROW = 1024


def _nrows(size):
    return -(-size // ROW)


def _pack(arrs, rows_total, dtype=F32):
    parts = []
    for a in arrs:
        flat = a.reshape(-1).astype(dtype)
        pad = _nrows(flat.shape[0]) * ROW - flat.shape[0]
        parts.append(flat if pad == 0 else jnp.concatenate([flat, jnp.zeros((pad,), dtype)]))
    flat = jnp.concatenate(parts)
    out = flat.reshape(-1, ROW)
    return _pad_rows(out, rows_total)


def _unpack(buf, shapes):
    lead = buf.shape[:-2]
    out, r0 = [], 0
    for shp in shapes:
        size = 1
        for s in shp:
            size *= s
        nr = _nrows(size)
        piece = lax.slice_in_dim(buf, r0, r0 + nr, axis=len(lead))
        out.append(piece.reshape(lead + (nr * ROW,))[..., :size].reshape(lead + tuple(shp)))
        r0 += nr
    return out


def _round_up(n, k):
    return -(-n // k) * k


WEIGHTS = ['c_ctx', 'ada_w', 'ada_b', 'norm_mix_g', 'norm_mlp_g', 'mlp_w1', 'mlp_w2', 'hy_w_in', 'ssd_conv_w', 'ssd_conv_b',
           'ssd_dt_bias', 'ssd_a_log', 'ssd_d', 'ssd_norm_g', 'sg_ln_g', 'sg_ln_b', 'sg_w', 'sg_b', 'hy_w_out', 'cf_w_pw1',
           'cf_b_pw1', 'cf_w_dw', 'cf_b_dw', 'cf_ln_g', 'cf_ln_b', 'cf_w_pw2', 'cf_b_pw2', 'final_norm_g']
BIG = {'mlp_w1': 2, 'mlp_w2': 1, 'hy_w_in': 2, 'hy_w_out': 1, 'cf_w_pw1': 2, 'cf_w_pw2': 1}
SMALL_SHARD = ['ssd_conv_w', 'cf_b_pw1', 'cf_w_dw', 'cf_b_dw', 'cf_ln_g', 'cf_ln_b', 'cf_b_pw2']
REP = ['norm_mix_g', 'norm_mlp_g', 'ssd_conv_b', 'ssd_dt_bias', 'ssd_a_log', 'ssd_d', 'ssd_norm_g', 'sg_ln_g', 'sg_ln_b',
       'sg_w', 'sg_b', 'final_norm_g']


def _gather_shards(stacked, axis):
    return jnp.concatenate([stacked[j] for j in range(4)], axis=axis)


def _split_shards(full, axis):
    n = full.shape[axis] // 4
    return [lax.slice_in_dim(full, j * n, (j + 1) * n, axis=axis) for j in range(4)]


def kernel(x, c, ctx, c_ctx, ada_w, ada_b, norm_mix_g, norm_mlp_g, mlp_w1, mlp_w2, hy_w_in, ssd_conv_w, ssd_conv_b, ssd_dt_bias, ssd_a_log, ssd_d, ssd_norm_g, sg_ln_g, sg_ln_b, sg_w, sg_b, hy_w_out, cf_w_pw1, cf_b_pw1, cf_w_dw, cf_b_dw, cf_ln_g, cf_ln_b, cf_w_pw2, cf_b_pw2, final_norm_g, loss_target, m_c_ctx, m_ada_w, m_ada_b, m_norm_mix_g, m_norm_mlp_g, m_mlp_w1, m_mlp_w2, m_hy_w_in, m_ssd_conv_w, m_ssd_conv_b, m_ssd_dt_bias, m_ssd_a_log, m_ssd_d, m_ssd_norm_g, m_sg_ln_g, m_sg_ln_b, m_sg_w, m_sg_b, m_hy_w_out, m_cf_w_pw1, m_cf_b_pw1, m_cf_w_dw, m_cf_b_dw, m_cf_ln_g, m_cf_ln_b, m_cf_w_pw2, m_cf_b_pw2, m_final_norm_g, v_c_ctx, v_ada_w, v_ada_b, v_norm_mix_g, v_norm_mlp_g, v_mlp_w1, v_mlp_w2, v_hy_w_in, v_ssd_conv_w, v_ssd_conv_b, v_ssd_dt_bias, v_ssd_a_log, v_ssd_d, v_ssd_norm_g, v_sg_ln_g, v_sg_ln_b, v_sg_w, v_sg_b, v_hy_w_out, v_cf_w_pw1, v_cf_b_pw1, v_cf_w_dw, v_cf_b_dw, v_cf_ln_g, v_cf_ln_b, v_cf_w_pw2, v_cf_b_pw2, v_final_norm_g):
    args = locals()
    wl = {n: args[n] for n in WEIGHTS}
    ml = {n: args["m_" + n] for n in WEIGHTS}
    vl = {n: args["v_" + n] for n in WEIGHTS}
    mx, my, mc = _coords()
    me = 4 * mx + 2 * my + mc
    shard = 2 * mx + my
    even = (0, 2, 4, 6)

    def start_gather(names, name, tie):
        rows = _round_up(sum(_nrows(wl[n].size) for n in names), 16)
        return _x4_start(_pack([wl[n] for n in names], rows, MXU) + tie.astype(MXU), name, a2a=False)

    def finish_gather(handle, names, after, name):
        send, recv, own, land, _ = handle
        land = _x4_wait(send, recv, own, land, after, name, a2a=False)
        got = lax.dynamic_update_slice(land, own[None], (shard, 0, 0))
        shapes = [wl[n].shape for n in names]
        return {n: _gather_shards(st, BIG[n]) for n, st in zip(names, _unpack(got, shapes))}

    rest_names = [n for n in BIG if n != "hy_w_in"]
    h_in = start_gather(["hy_w_in"], "agw_in_start", jnp.zeros((), F32))
    h_rest = start_gather(rest_names, "agw_rest_start", h_in[4][0, 0])
    c = c + h_rest[4][0, 0]

    small_shapes = [wl[n].shape for n in SMALL_SHARD]
    blk1 = _pack([c] + [wl[n] for n in SMALL_SHARD], 24)
    got1 = _ag8(blk1, "ag_cond")
    x16 = _pad_rows(jnp.concatenate([got1[:, 0, :], c_ctx[None]], axis=0), 16)
    small_full = {}
    for n, parts in zip(SMALL_SHARD, _unpack(got1[:, 1:, :], small_shapes)):
        small_full[n] = jnp.concatenate([parts[s] for s in even], axis=-1)

    nloc = ada_w.shape[-1]
    ada_b_loc = lax.dynamic_slice_in_dim(ada_b, shard * nloc, nloc, axis=1)
    s16, ada_loc = _ada_fwd(x16, ada_w, ada_b_loc)
    got2 = _ag8(ada_loc.reshape(32, nloc), "ag_ada").reshape(8, 2, 16, nloc)
    ada_full = jnp.concatenate([got2[s] for s in even], axis=-1)
    ada_me = lax.dynamic_slice_in_dim(ada_full, me, 1, axis=1)[:, 0, :]
    cada0 = ada_full[0, 8, :]

    w = {n: wl[n] for n in WEIGHTS if n not in BIG and n not in SMALL_SHARD}
    w.update(small_full)
    win = finish_gather(h_in, ["hy_w_in"], ada_me, "agw_in_wait")["hy_w_in"][0]
    w["wz"], w["wxbc"], w["wuv"] = win[:, :D], win[:, D:D + XBC], win[:, D + XBC + 32:]
    w["wdt"] = _pad_cols(win[:, D + XBC:D + XBC + 32], LANE)
    w["sg_w"] = sg_w[0].astype(MXU)

    def late_w(after):
        wfull = finish_gather(h_rest, rest_names, after, "agw_rest_wait")
        return {"hy_w_out": wfull["hy_w_out"][0], "mlp_w1": wfull["mlp_w1"], "mlp_w2": wfull["mlp_w2"],
                "cf_w_pw1": wfull["cf_w_pw1"][0], "cf_w_pw2": wfull["cf_w_pw2"][0]}

    full_shape = {n: wl[n].shape for n in WEIGHTS}
    for n in BIG:
        full_shape[n] = tuple(s * 4 if a == BIG[n] else s for a, s in enumerate(wl[n].shape))
    for n in SMALL_SHARD:
        full_shape[n] = wl[n].shape[:-1] + (wl[n].shape[-1] * 4,)

    def grad_pieces(g_, names):
        rows = _round_up(sum(_nrows(wl[n].size) for n in names), 512)
        return jnp.stack([_pack([_split_shards(g_[n].reshape(full_shape[n]), BIG[n])[j] for n in names], rows, jnp.bfloat16)
                          for j in range(4)])

    early_names = ["mlp_w1", "mlp_w2", "cf_w_pw1", "cf_w_pw2"]
    last_names = ["hy_w_in", "hy_w_out"]
    early = {}

    def early_grads(g_):
        early["h"] = _x4_start(grad_pieces(g_, early_names), "a2a_early_start", a2a=True)
        return early["h"][4][0, 0]

    loss_part, dx, g, d_ada, d_cada0 = _local_step(x[0], ctx[0], loss_target[0], ada_me, cada0, w, late_w, early_grads)
    loss = lax.psum(loss_part, ("x", "y", "c"))
    g = {n: a.reshape(full_shape[n]) for n, a in g.items()}

    sm_names = REP + SMALL_SHARD
    srows = _round_up(sum(_nrows(g[n].size) for n in sm_names) + 18, 8)
    got3 = _ag8(_pack([g[n] for n in sm_names] + [d_ada, d_cada0], srows), "ag_small")
    tot3 = _sum_slots(got3, tuple(range(8)), "sum_small")
    sm_tot = _unpack(tot3, [full_shape[n] for n in sm_names] + [(2, 6 * D), (6 * D,)])
    grads = dict(zip(sm_names, sm_tot[:-2]))
    for n in SMALL_SHARD:
        k = wl[n].shape[-1]
        grads[n] = lax.dynamic_slice_in_dim(grads[n], shard * k, k, axis=grads[n].ndim - 1)
    dada_tot, dcada_tot = sm_tot[-2], sm_tot[-1]
    grads["ada_b"] = dada_tot.at[0].add(dcada_tot)
    r_ada = sum(_nrows(g[n].size) for n in sm_names)
    dada_all = got3[:, r_ada:r_ada + 12, :].reshape(8, 2, 6 * D)
    d16 = jnp.concatenate([jnp.transpose(dada_all, (1, 0, 2)),
                           jnp.stack([dcada_tot, jnp.zeros_like(dcada_tot)])[:, None, :],
                           jnp.zeros((2, 7, 6 * D), F32)], axis=1)
    d_loc = lax.dynamic_slice_in_dim(d16, shard * nloc, nloc, axis=2)
    grads["ada_w"], cpart = _ada_bwd(s16, d_loc, ada_w)
    got4 = _ag8(cpart, "ag_cctx")
    dscc = _sum_slots(got4, even, "sum_cctx")
    grads["c_ctx"] = _cctx_grad(dscc, _pad_rows(c_ctx[None], 8))[0]

    got_last = _xchg4(grad_pieces(g, last_names), "a2a_last", a2a=True)
    send, recv, own, land, _ = early["h"]
    land = _x4_wait(send, recv, own, land, got_last, "a2a_early_wait", a2a=True)
    got_early = lax.dynamic_update_slice(land, lax.dynamic_slice_in_dim(own, shard, 1, axis=0), (shard, 0, 0))
    for got, names, tag in ((got_early, early_names, "early"), (got_last, last_names, "last")):
        part = _sum_slots(got, (0, 1, 2, 3), "sum_grads_" + tag, tr=512)
        tot = _add(part, _xchg_sib(part, "swap_grads_" + tag), "add_grads_" + tag, 512)
        grads.update(zip(names, _unpack(tot, [wl[n].shape for n in names])))

    delta, new_m, new_v = {}, {}, {}
    for n in list(BIG) + ["ada_w"]:
        delta[n], new_m[n], new_v[n] = _adamw(wl[n], grads[n], ml[n], vl[n], "adamw_" + n)
    for group, tag in ((["c_ctx", "ada_b"] + REP, "rep"), (SMALL_SHARD, "shard")):
        shapes = [wl[n].shape for n in group]
        rows = _round_up(sum(_nrows(wl[n].size) for n in group), 8)
        outs = _adamw(*(_pack([src[n] for n in group], rows) for src in (wl, grads, ml, vl)), "adamw_small_" + tag)
        for dst, buf in zip((delta, new_m, new_v), outs):
            dst.update(zip(group, _unpack(buf, shapes)))

    return (loss, dx[None], *[grads[n].reshape(wl[n].shape) for n in WEIGHTS], *[delta[n] for n in WEIGHTS],
            *[new_m[n] for n in WEIGHTS], *[new_v[n] for n in WEIGHTS])
```

```python
import functools

import jax
import jax.numpy as jnp
from jax import lax
from jax.experimental import pallas as pl
from jax.experimental.pallas import tpu as pltpu

F32 = jnp.float32
MXU = jnp.bfloat16
ACT = jnp.bfloat16
HI = lax.Precision.HIGHEST
EPS = 1e-6

D = 1024
HID = 4096
XBC = 1536
NPAIR = 8
Q = 128
SGG = 8
CK = 31
SK = 5
T = 256
LANE = 128
VMEM_LIMIT = 56 * 1024 * 1024

ADAM_LR, ADAM_B1, ADAM_B2, ADAM_EPS, ADAM_WD, ADAM_STEP = 0.001, 0.9, 0.999, 1e-08, 0.01, 10


def _call(body, name, grid, in_specs, out_specs, out_shape, scratch=()):
    return pl.pallas_call(
        body, name=name, grid=grid, in_specs=in_specs, out_specs=out_specs, out_shape=out_shape,
        scratch_shapes=list(scratch),
        compiler_params=pltpu.CompilerParams(dimension_semantics=("arbitrary",) * len(grid),
                                             vmem_limit_bytes=VMEM_LIMIT))


def _sds(shape, dt=F32):
    return jax.ShapeDtypeStruct(tuple(shape), dt)


def _rows(t, w, off=0, lane_blk=0):
    return pl.BlockSpec((t, w), lambda i: (i + off, lane_blk))


def _rows_lat(t, w, nct):
    return pl.BlockSpec((t, w), lambda i: (jnp.maximum(i - nct, 0), 0))


def _full(*shape):
    return pl.BlockSpec(shape, lambda *_: (0,) * len(shape))


def _wfull(*shape):
    return pl.BlockSpec(shape, lambda *_: (0,) * len(shape), pipeline_mode=pl.Buffered(1))


def _halo(w, hb, nrows):
    r, nb = T // hb, nrows // hb
    prev = pl.BlockSpec((hb, w), lambda i: (jnp.maximum(i * r - 1, 0), 0))
    nxt = pl.BlockSpec((hb, w), lambda i: (jnp.minimum((i + 1) * r, nb - 1), 0))
    return prev, nxt


def _nn(a, b):
    return jnp.dot(a.astype(MXU), b.astype(MXU), preferred_element_type=F32)


def _nt(a, b):
    return lax.dot_general(a.astype(MXU), b.astype(MXU), (((1,), (1,)), ((), ())), preferred_element_type=F32)


def _tn_dot(a, b):
    return lax.dot_general(a.astype(MXU), b.astype(MXU), (((0,), (0,)), ((), ())), preferred_element_type=F32)


def _sum0(x):
    return jnp.sum(x, axis=0, keepdims=True)


def _silu(x):
    return x * jax.nn.sigmoid(x)


def _gelu(x):
    return jax.nn.gelu(x, approximate=True)


def _mod(h, g, sh, sc):
    n = h * lax.rsqrt(jnp.mean(h * h, axis=-1, keepdims=True) + EPS)
    return n * g * (1.0 + sc) + sh


def _ln(x, g, b):
    xc = x - jnp.mean(x, axis=-1, keepdims=True)
    return xc * lax.rsqrt(jnp.mean(xc * xc, axis=-1, keepdims=True) + EPS) * g + b


def _gate_norm(ytot, z, ng):
    yg = ytot * _silu(z)
    halves = []
    for k in range(2):
        seg = yg[:, k * 512:(k + 1) * 512]
        halves.append(seg * lax.rsqrt(jnp.mean(seg * seg, axis=-1, keepdims=True) + EPS) * ng[:, k * 512:(k + 1) * 512])
    return jnp.concatenate(halves, axis=-1)


def _fill_ext(ext_ref, prev_ref, cur_ref, next_ref, hb, first, last):
    ext_ref[0:hb, :] = jnp.where(first, 0.0, prev_ref[...])
    ext_ref[hb:hb + T, :] = cur_ref[...]
    ext_ref[hb + T:hb + T + hb, :] = jnp.where(last, 0.0, next_ref[...])


def _conv(ext_ref, w_ref, k_taps, hb, lanes):
    off = hb - k_taps // 2
    acc = ext_ref[pl.ds(off, T), lanes] * w_ref[0:1, lanes]
    for k in range(1, k_taps):
        acc = acc + ext_ref[pl.ds(off + k, T), lanes] * w_ref[k:k + 1, lanes]
    return acc


def _conv_tr(ext_ref, w_ref, k_taps, hb, lanes):
    off = hb + k_taps // 2
    acc = ext_ref[pl.ds(off, T), lanes] * w_ref[0:1, lanes]
    for k in range(1, k_taps):
        acc = acc + ext_ref[pl.ds(off - k, T), lanes] * w_ref[k:k + 1, lanes]
    return acc


def _conv_wide(ext_ref, w_ref, k_taps, hb, lanes, flip=False):
    base = hb - k_taps // 2
    acc = None
    for b in range(8):
        taps = [k for k in range(k_taps) if (base + k) % 8 == b]
        if not taps:
            continue
        p = None
        for k in taps:
            wi = (k_taps - 1 - k) if flip else k
            term = ext_ref[pl.ds(base + k - b, T + 8), lanes] * w_ref[wi:wi + 1, lanes]
            p = term if p is None else p + term
        acc = p[b:b + T] if acc is None else acc + p[b:b + T]
    return acc


def _conv_dw_wide(dw_ref, d_ref, xext_ref, k_taps, hb, lanes):
    base = hb - k_taps // 2
    d = d_ref[:, lanes]
    for b in range(8):
        taps = [k for k in range(k_taps) if (base + k) % 8 == b]
        if not taps:
            continue
        lo_off = base + taps[0] - b
        span = base + taps[-1] - b - lo_off
        xs = xext_ref[pl.ds(lo_off + b, T + span), lanes]
        for k in taps:
            a = base + k - b - lo_off
            dw_ref[k:k + 1, lanes] += _sum0(d * xs[a:a + T])


def _conv_dw(dw_ref, d_ref, xext_ref, k_taps, hb, lanes):
    off = hb - k_taps // 2
    d = d_ref[:, lanes]
    for k in range(k_taps):
        dw_ref[k:k + 1, lanes] += _sum0(d * xext_ref[pl.ds(off + k, T), lanes])


def _tn(a, b, name, relu2=False):
    m_rows, ka = a.shape
    n = b.shape[1]
    tm = next(t for t in (1024, 768, 512, 256) if m_rows % t == 0)
    tk = min(ka, 1024)
    tn = n if n <= 1024 else next(t for t in (1024, 768, 512, 384, 256, 128) if n % t == 0)

    def body(a_ref, b_ref, o_ref):
        @pl.when(pl.program_id(2) == 0)
        def _():
            o_ref[...] = jnp.zeros_like(o_ref)
        av = a_ref[...]
        if relu2:
            av = jnp.square(jnp.maximum(av.astype(F32), 0.0))
        o_ref[...] += _tn_dot(av, b_ref[...])

    return _call(body, name, (ka // tk, n // tn, m_rows // tm),
                 [pl.BlockSpec((tm, tk), lambda k, j, m: (m, k)), pl.BlockSpec((tm, tn), lambda k, j, m: (m, j))],
                 pl.BlockSpec((tk, tn), lambda k, j, m: (k, j)), _sds((ka, n)))(a, b)


def _mlp_fwd(h, mp, w1, w2, name):
    n_rows = h.shape[0]

    def body(h_ref, mp_ref, w1_ref, w2_ref, hn_ref, a_ref, y_ref):
        hv = h_ref[...]
        u = _mod(hv, mp_ref[0:1], mp_ref[1:2], mp_ref[2:3]).astype(MXU)
        acc = jnp.zeros((T, D), F32)
        for j in range(HID // 1024):
            cs = slice(j * 1024, (j + 1) * 1024)
            a = jnp.dot(u, w1_ref[:, cs], preferred_element_type=F32)
            a_ref[:, cs] = a.astype(ACT)
            acc = acc + jnp.dot(jnp.square(jnp.maximum(a, 0.0)).astype(MXU), w2_ref[cs, :], preferred_element_type=F32)
        y_ref[...] = acc
        hn_ref[...] = hv + mp_ref[3:4] * acc

    return _call(body, name, (n_rows // T,),
                 [_rows(T, D), _full(8, D), _wfull(D, HID), _wfull(HID, D)],
                 [_rows(T, D), _rows(T, HID), _rows(T, D)],
                 [_sds((n_rows, D)), _sds((n_rows, HID), ACT), _sds((n_rows, D))])(h, mp, w1, w2)


def _mlp_bwd(dh, h, a, y, mp, w1, w2, name):
    n_rows = h.shape[0]

    def body(dh_ref, h_ref, a_ref, y_ref, mp_ref, w1_ref, w2_ref, dho_ref, da_ref, dyb_ref, ub_ref, pg_ref):
        dhp = dh_ref[...]
        u, vjp = jax.vjp(_mod, h_ref[...], mp_ref[0:1], mp_ref[1:2], mp_ref[2:3])
        ub_ref[...] = u.astype(ACT)
        dyb = (mp_ref[3:4] * dhp).astype(MXU)
        dyb_ref[...] = dyb.astype(ACT)
        du = jnp.zeros((T, D), F32)
        for j in range(HID // 1024):
            cs = slice(j * 1024, (j + 1) * 1024)
            dp = _nt(dyb, w2_ref[cs, :])
            da = dp * 2.0 * jnp.maximum(a_ref[:, cs].astype(F32), 0.0)
            da_ref[:, cs] = da.astype(ACT)
            du = du + _nt(da, w1_ref[:, cs])
        dhn, dg, dsh, dsc = vjp(du)
        dho_ref[...] = dhp + dhn

        @pl.when(pl.program_id(0) == 0)
        def _():
            pg_ref[...] = jnp.zeros_like(pg_ref)
        pg_ref[0:1] += dg
        pg_ref[1:2] += dsh
        pg_ref[2:3] += dsc
        pg_ref[3:4] += _sum0(dhp * y_ref[...])

    return _call(body, name, (n_rows // T,),
                 [_rows(T, D), _rows(T, D), _rows(T, HID), _rows(T, D), _full(8, D), _wfull(D, HID), _wfull(HID, D)],
                 [_rows(T, D), _rows(T, HID), _rows(T, D), _rows(T, D), _full(8, D)],
                 [_sds((n_rows, D)), _sds((n_rows, HID), ACT), _sds((n_rows, D), ACT), _sds((n_rows, D), ACT),
                  _sds((8, D))])(dh, h, a, y, mp, w1, w2)


def _cf1_fwd(h, mp, w1, b1):
    n_rows = h.shape[0]

    def body(h_ref, mp_ref, w1_ref, b1_ref, glu_ref, a_ref):
        u = _mod(h_ref[...], mp_ref[0:1], mp_ref[1:2], mp_ref[2:3]).astype(MXU)
        a = jnp.dot(u, w1_ref[...], preferred_element_type=F32) + b1_ref[...]
        a_ref[...] = a.astype(ACT)
        glu_ref[...] = a[:, :D] * jax.nn.sigmoid(a[:, D:])

    return _call(body, "cf1_fwd", (n_rows // T,),
                 [_rows(T, D), _full(8, D), _wfull(D, 2 * D), _full(1, 2 * D)],
                 [_rows(T, D), _rows(T, 2 * D)],
                 [_sds((n_rows, D)), _sds((n_rows, 2 * D), ACT)])(h, mp, w1, b1)


def _cf2_fwd(h, glu, mp, wdw, bdw, lng, lnb, w2, b2):
    n_rows = h.shape[0]
    nt = n_rows // T
    hb = 16

    def body(h_ref, gp_ref, gc_ref, gn_ref, mp_ref, wdw_ref, bdw_ref, lng_ref, lnb_ref, w2_ref, b2_ref,
             hn_ref, cv_ref, sb_ref, y_ref, ext):
        i = pl.program_id(0)
        _fill_ext(ext, gp_ref, gc_ref, gn_ref, hb, i == 0, i == nt - 1)
        for c in range(D // LANE):
            lanes = slice(c * LANE, (c + 1) * LANE)
            cv_ref[:, lanes] = _conv_wide(ext, wdw_ref, CK, hb, lanes) + bdw_ref[:, lanes]
        s = _silu(_ln(cv_ref[...], lng_ref[...], lnb_ref[...])).astype(MXU)
        sb_ref[...] = s.astype(ACT)
        y = jnp.dot(s, w2_ref[...], preferred_element_type=F32) + b2_ref[...]
        y_ref[...] = y
        hn_ref[...] = h_ref[...] + mp_ref[3:4] * y

    gp, gn = _halo(D, hb, n_rows)
    return _call(body, "cf2_fwd", (nt,),
                 [_rows(T, D), gp, _rows(T, D), gn, _full(8, D), _full(32, D), _full(1, D), _full(1, D), _full(1, D),
                  _wfull(D, D), _full(1, D)],
                 [_rows(T, D), _rows(T, D), _rows(T, D), _rows(T, D)],
                 [_sds((n_rows, D)), _sds((n_rows, D)), _sds((n_rows, D), ACT), _sds((n_rows, D))],
                 scratch=[pltpu.VMEM((T + 2 * hb, D), F32)])(h, glu, glu, glu, mp, wdw, bdw, lng, lnb, w2, b2)


def _cf2_bwd(dh, y, cv, mp, lng, lnb, w2):
    n_rows = dh.shape[0]

    def body(dh_ref, y_ref, cv_ref, mp_ref, lng_ref, lnb_ref, w2_ref, dcv_ref, dyb_ref, pg_ref):
        dhp = dh_ref[...]
        dy = mp_ref[3:4] * dhp
        dyb = dy.astype(MXU)
        dyb_ref[...] = dyb.astype(ACT)
        ds = _nt(dyb, w2_ref[...])
        _, vjp = jax.vjp(lambda cv_, g_, b_: _silu(_ln(cv_, g_, b_)), cv_ref[...], lng_ref[...], lnb_ref[...])
        dcv, dlng, dlnb = vjp(ds)
        dcv_ref[...] = dcv

        @pl.when(pl.program_id(0) == 0)
        def _():
            pg_ref[...] = jnp.zeros_like(pg_ref)
        pg_ref[0:1] += _sum0(dhp * y_ref[...])
        pg_ref[1:2] += _sum0(dy)
        pg_ref[2:3] += dlng
        pg_ref[3:4] += dlnb
        pg_ref[4:5] += _sum0(dcv)

    return _call(body, "cf2_bwd", (n_rows // T,),
                 [_rows(T, D), _rows(T, D), _rows(T, D), _full(8, D), _full(1, D), _full(1, D), _wfull(D, D)],
                 [_rows(T, D), _rows(T, D), _full(8, D)],
                 [_sds((n_rows, D)), _sds((n_rows, D), ACT), _sds((8, D))])(dh, y, cv, mp, lng, lnb, w2)


def _cf1_bwd(dh, h, a, dcv, glu, mp, wdw, w1):
    n_rows = h.shape[0]
    nt = n_rows // T
    hb = 16

    def body(dh_ref, h_ref, a_ref, dp_ref, dc_ref, dn_ref, gp_ref, gc_ref, gn_ref, mp_ref, wdw_ref, w1_ref,
             dho_ref, da_ref, ub_ref, pg_ref, pb_ref, dw_ref, dext, gext, dglu):
        i = pl.program_id(0)

        @pl.when(i == 0)
        def _():
            pg_ref[...] = jnp.zeros_like(pg_ref)
            pb_ref[...] = jnp.zeros_like(pb_ref)
            dw_ref[...] = jnp.zeros_like(dw_ref)
        _fill_ext(dext, dp_ref, dc_ref, dn_ref, hb, i == 0, i == nt - 1)
        _fill_ext(gext, gp_ref, gc_ref, gn_ref, hb, i == 0, i == nt - 1)
        for c in range(D // LANE):
            lanes = slice(c * LANE, (c + 1) * LANE)
            dglu[:, lanes] = _conv_wide(dext, wdw_ref, CK, hb, lanes, flip=True)
            _conv_dw_wide(dw_ref, dc_ref, gext, CK, hb, lanes)
        av = a_ref[...].astype(F32)
        _, vjp_glu = jax.vjp(lambda a1, a2: a1 * jax.nn.sigmoid(a2), av[:, :D], av[:, D:])
        da1, da2 = vjp_glu(dglu[...])
        da_ref[:, :D] = da1.astype(ACT)
        da_ref[:, D:] = da2.astype(ACT)
        pb_ref[0:1, :D] += _sum0(da1)
        pb_ref[0:1, D:] += _sum0(da2)
        du = _nt(da1, w1_ref[:, :D]) + _nt(da2, w1_ref[:, D:])
        u, vjp = jax.vjp(_mod, h_ref[...], mp_ref[0:1], mp_ref[1:2], mp_ref[2:3])
        ub_ref[...] = u.astype(ACT)
        dhn, dg, dsh, dsc = vjp(du)
        dho_ref[...] = dh_ref[...] + dhn
        pg_ref[0:1] += dg
        pg_ref[1:2] += dsh
        pg_ref[2:3] += dsc

    hp, hn = _halo(D, hb, n_rows)
    return _call(body, "cf1_bwd", (nt,),
                 [_rows(T, D), _rows(T, D), _rows(T, 2 * D), hp, _rows(T, D), hn, hp, _rows(T, D), hn,
                  _full(8, D), _full(32, D), _wfull(D, 2 * D)],
                 [_rows(T, D), _rows(T, 2 * D), _rows(T, D), _full(8, D), _full(8, 2 * D), _full(32, D)],
                 [_sds((n_rows, D)), _sds((n_rows, 2 * D), ACT), _sds((n_rows, D), ACT), _sds((8, D)),
                  _sds((8, 2 * D)), _sds((32, D))],
                 scratch=[pltpu.VMEM((T + 2 * hb, D), F32), pltpu.VMEM((T + 2 * hb, D), F32), pltpu.VMEM((T, D), F32)],
                 )(dh, h, a, dcv, dcv, dcv, glu, glu, glu, mp, wdw, w1)


def _sg_blocks():
    return [(c, g, slice(c * Q, (c + 1) * Q), slice(g * LANE, (g + 1) * LANE)) for c in range(T // Q) for g in range(SGG)]


def _hy1_fwd(hcat, mp2, wz, wuv, wxbc, wdt, lng, lnb, sgw, sgbt, nct):
    n_rows = hcat.shape[0]

    def body(h_ref, mp_ref, wz_ref, wuv_ref, wxbc_ref, wdt_ref, lng_ref, lnb_ref, sgw_ref, sgbt_ref,
             z_ref, uv_ref, xbcp_ref, dtr_ref, ysg_ref):
        u = _mod(h_ref[...], mp_ref[0:1], mp_ref[1:2], mp_ref[2:3]).astype(MXU)
        z_ref[...] = jnp.dot(u, wz_ref[...], preferred_element_type=F32)
        xbcp_ref[...] = jnp.dot(u, wxbc_ref[...], preferred_element_type=F32)
        dtr_ref[...] = jnp.dot(u, wdt_ref[...], preferred_element_type=F32)
        uv = jnp.dot(u, wuv_ref[...], preferred_element_type=F32)
        uv_ref[...] = uv
        gate = _gelu(uv[:, :D])
        vln = _ln(_gelu(uv[:, D:]), lng_ref[...], lnb_ref[...]).astype(MXU)
        for _, g, rs, ls in _sg_blocks():
            s = jnp.dot(sgw_ref[g], vln[rs, ls], preferred_element_type=F32) + sgbt_ref[:, g:g + 1]
            ysg_ref[rs, ls] = (gate[rs, ls] * s).astype(ACT)

    mspec = pl.BlockSpec((None, 8, D), lambda i: (jnp.where(i < nct, 0, 1), 0, 0))
    return _call(body, "hy1_fwd", (n_rows // T,),
                 [_rows(T, D), mspec, _wfull(D, D), _wfull(D, 2 * D), _wfull(D, XBC), _wfull(D, LANE),
                  _full(1, D), _full(1, D), _full(SGG, Q, Q), _full(Q, LANE)],
                 [_rows(T, D), _rows(T, 2 * D), _rows(T, XBC), _rows(T, LANE), _rows(T, D)],
                 [_sds((n_rows, D)), _sds((n_rows, 2 * D)), _sds((n_rows, XBC)), _sds((n_rows, LANE)),
                  _sds((n_rows, D), ACT)])(hcat, mp2, wz, wuv, wxbc, wdt, lng, lnb, sgw, sgbt)


def _hy1_bwd(hcat, uv, dz, dxbcp, ddf, ddb, dysg, dres, mp2, wz, wuv, wxbc, wdt, lng, lnb, sgw, sgbt, nct):
    n_rows = hcat.shape[0]
    n_lat = dres.shape[0]

    def body(h_ref, uv_ref, dz_ref, dxbcp_ref, ddf_ref, ddb_ref, dysg_ref, dres_ref, mp_ref, wz_ref, wuv_ref,
             wxbc_ref, wdt_ref, lng_ref, lnb_ref, sgw_ref, sgbt_ref,
             dho_ref, ub_ref, duv_ref, ddt_ref, pg2_ref, pl_ref, dsgw_ref, dsgb_ref, dgate_s, dvln_s):
        i = pl.program_id(0)

        @pl.when(i == 0)
        def _():
            pg2_ref[...] = jnp.zeros_like(pg2_ref)
            pl_ref[...] = jnp.zeros_like(pl_ref)
            dsgw_ref[...] = jnp.zeros_like(dsgw_ref)
            dsgb_ref[...] = jnp.zeros_like(dsgb_ref)
        uv = uv_ref[...]

        def f_sg(ug, uvv, g_, b_):
            return _gelu(ug), _ln(_gelu(uvv), g_, b_)
        (gate, vln), vjp_sg = jax.vjp(f_sg, uv[:, :D], uv[:, D:], lng_ref[...], lnb_ref[...])
        vlnb = vln.astype(MXU)
        lane = lax.broadcasted_iota(jnp.int32, (Q, LANE), 1)
        dsgb = jnp.zeros((Q, LANE), F32)
        for _, g, rs, ls in _sg_blocks():
            s = jnp.dot(sgw_ref[g], vlnb[rs, ls], preferred_element_type=F32) + sgbt_ref[:, g:g + 1]
            dyb = dysg_ref[rs, ls]
            dgate_s[rs, ls] = dyb * s
            ds = dyb * gate[rs, ls]
            dvln_s[rs, ls] = _tn_dot(sgw_ref[g], ds)
            dsgw_ref[g] += _nt(ds, vlnb[rs, ls])
            dsgb = dsgb + jnp.where(lane == g, jnp.sum(ds, axis=1, keepdims=True), 0.0)
        dsgb_ref[...] += dsgb
        dug, duvv, dlng, dlnb = vjp_sg((dgate_s[...], dvln_s[...]))
        pl_ref[0:1] += dlng
        pl_ref[1:2] += dlnb
        duv_ref[:, :D] = dug.astype(ACT)
        duv_ref[:, D:] = duvv.astype(ACT)
        ddt = (ddf_ref[...] + ddb_ref[...]).astype(MXU)
        ddt_ref[...] = ddt.astype(ACT)
        du = (_nt(dz_ref[...], wz_ref[...]) + _nt(dug, wuv_ref[:, :D]) + _nt(duvv, wuv_ref[:, D:])
              + _nt(dxbcp_ref[...], wxbc_ref[...]) + _nt(ddt, wdt_ref[...]))
        u, vjp = jax.vjp(_mod, h_ref[...], mp_ref[0:1], mp_ref[1:2], mp_ref[2:3])
        ub_ref[...] = u.astype(ACT)
        dhn, dg, dsh, dsc = vjp(du)
        dho_ref[...] = dres_ref[...] + dhn
        is_ctx = i < nct
        for k, val in enumerate((dg, dsh, dsc)):
            pg2_ref[0, k:k + 1] += jnp.where(is_ctx, val, 0.0)
            pg2_ref[1, k:k + 1] += jnp.where(is_ctx, 0.0, val)

    mspec = pl.BlockSpec((None, 8, D), lambda i: (jnp.where(i < nct, 0, 1), 0, 0))
    return _call(body, "hy1_bwd", (n_rows // T,),
                 [_rows(T, D), _rows(T, 2 * D), _rows(T, D), _rows(T, XBC), _rows(T, LANE), _rows(T, LANE), _rows(T, D),
                  _rows_lat(T, D, nct), mspec, _wfull(D, D), _wfull(D, 2 * D), _wfull(D, XBC), _wfull(D, LANE),
                  _full(1, D), _full(1, D), _full(SGG, Q, Q), _full(Q, LANE)],
                 [_rows_lat(T, D, nct), _rows(T, D), _rows(T, 2 * D), _rows(T, LANE), _full(2, 8, D), _full(8, D),
                  _full(SGG, Q, Q), _full(Q, LANE)],
                 [_sds((n_lat, D)), _sds((n_rows, D), ACT), _sds((n_rows, 2 * D), ACT), _sds((n_rows, LANE), ACT),
                  _sds((2, 8, D)), _sds((8, D)), _sds((SGG, Q, Q)), _sds((Q, LANE))],
                 scratch=[pltpu.VMEM((T, D), F32), pltpu.VMEM((T, D), F32)],
                 )(hcat, uv, dz, dxbcp, ddf, ddb, dysg, dres, mp2, wz, wuv, wxbc, wdt, lng, lnb, sgw, sgbt)


def _seq_edges(i, nct, nt):
    return (i == 0) | (i == nct), (i == nct - 1) | (i == nt - 1)


def _cv5_fwd(xbcp, w, b, nct):
    n_rows = xbcp.shape[0]
    nt = n_rows // T
    hb = 8

    def body(p_ref, c_ref, n_ref, w_ref, b_ref, o_ref, ext):
        first, last = _seq_edges(pl.program_id(0), nct, nt)
        _fill_ext(ext, p_ref, c_ref, n_ref, hb, first, last)
        for c in range(XBC // LANE):
            lanes = slice(c * LANE, (c + 1) * LANE)
            o_ref[:, lanes] = _silu(_conv(ext, w_ref, SK, hb, lanes) + b_ref[:, lanes])

    hp, hn = _halo(XBC, hb, n_rows)
    return _call(body, "cv5_fwd", (nt,), [hp, _rows(T, XBC), hn, _full(8, XBC), _full(1, XBC)],
                 _rows(T, XBC), _sds((n_rows, XBC)), scratch=[pltpu.VMEM((T + 2 * hb, XBC), F32)])(xbcp, xbcp, xbcp, w, b)


def _cv5_bwd1(xbcp, dxf, dxb, w, b, nct):
    n_rows = xbcp.shape[0]
    nt = n_rows // T
    hb = 8

    def body(p_ref, c_ref, n_ref, dxf_ref, dxb_ref, w_ref, b_ref, o_ref, pg_ref, ext):
        i = pl.program_id(0)
        first, last = _seq_edges(i, nct, nt)
        _fill_ext(ext, p_ref, c_ref, n_ref, hb, first, last)

        @pl.when(i == 0)
        def _():
            pg_ref[...] = jnp.zeros_like(pg_ref)
        for c in range(XBC // LANE):
            lanes = slice(c * LANE, (c + 1) * LANE)
            cv = _conv(ext, w_ref, SK, hb, lanes) + b_ref[:, lanes]
            sg = jax.nn.sigmoid(cv)
            dcv = (dxf_ref[:, lanes] + dxb_ref[:, lanes]) * (sg * (1.0 + cv * (1.0 - sg)))
            o_ref[:, lanes] = dcv
            pg_ref[0:1, lanes] += _sum0(dcv)

    hp, hn = _halo(XBC, hb, n_rows)
    return _call(body, "cv5_bwd1", (nt,),
                 [hp, _rows(T, XBC), hn, _rows(T, XBC), _rows(T, XBC), _full(8, XBC), _full(1, XBC)],
                 [_rows(T, XBC), _full(8, XBC)], [_sds((n_rows, XBC)), _sds((8, XBC))],
                 scratch=[pltpu.VMEM((T + 2 * hb, XBC), F32)])(xbcp, xbcp, xbcp, dxf, dxb, w, b)


def _cv5_bwd2(dcv, xbcp, w, nct):
    n_rows = xbcp.shape[0]
    nt = n_rows // T
    hb = 8

    def body(dp_ref, dc_ref, dn_ref, xp_ref, xc_ref, xn_ref, w_ref, o_ref, dw_ref, dext, xext):
        i = pl.program_id(0)
        first, last = _seq_edges(i, nct, nt)
        _fill_ext(dext, dp_ref, dc_ref, dn_ref, hb, first, last)
        _fill_ext(xext, xp_ref, xc_ref, xn_ref, hb, first, last)

        @pl.when(i == 0)
        def _():
            dw_ref[...] = jnp.zeros_like(dw_ref)
        for c in range(XBC // LANE):
            lanes = slice(c * LANE, (c + 1) * LANE)
            o_ref[:, lanes] = _conv_tr(dext, w_ref, SK, hb, lanes).astype(ACT)
            _conv_dw(dw_ref, dc_ref, xext, SK, hb, lanes)

    hp, hn = _halo(XBC, hb, n_rows)
    return _call(body, "cv5_bwd2", (nt,),
                 [hp, _rows(T, XBC), hn, hp, _rows(T, XBC), hn, _full(8, XBC)],
                 [_rows(T, XBC), _full(8, XBC)], [_sds((n_rows, XBC), ACT), _sds((8, XBC))],
                 scratch=[pltpu.VMEM((T + 2 * hb, XBC), F32), pltpu.VMEM((T + 2 * hb, XBC), F32)],
                 )(dcv, dcv, dcv, xbcp, xbcp, xbcp, w)


def _scan_order(nc, ncc, rev):
    if not rev:
        return lambda s: s
    return lambda s: jnp.where(s < ncc, ncc - 1 - s, nc - 1 - (s - ncc))


def _ssd_prep(dtr, sp, rev):
    dt = jax.nn.softplus(dtr + sp[0:1])
    a_neg = -jnp.exp(sp[1:2])
    r = lax.broadcasted_iota(jnp.int32, (Q, Q), 0)
    c = lax.broadcasted_iota(jnp.int32, (Q, Q), 1)
    msk = (c >= r) if rev else (c <= r)
    tri = msk.astype(F32)
    acs = jnp.dot(tri, dt * a_neg, precision=HI, preferred_element_type=F32)
    last = 0 if rev else Q - 1
    return dt, a_neg, acs, msk, tri, last


def _pair_sel(arr, lo, m, lane_lt):
    h0 = lo + 2 * m
    return jnp.where(lane_lt, arr[:, h0:h0 + 1], arr[:, h0 + 1:h0 + 2])


def _head_lanes(row, lo, g):
    lane = lax.broadcasted_iota(jnp.int32, (1, 512), 1)
    out = jnp.zeros((1, 512), F32)
    for k in range(8):
        h = lo + 8 * g + k
        out = jnp.where((lane >= 64 * k) & (lane < 64 * (k + 1)), row[:, h:h + 1], out)
    return out


def _halves(v, lane_lt):
    return jnp.concatenate([jnp.where(lane_lt, v, 0.0), jnp.where(lane_lt, 0.0, v)], axis=0)


def _ssd_fwd(xbc, dtr, sp, ncc, rev):
    n_rows = xbc.shape[0]
    nc = n_rows // Q
    lo = 16 if rev else 0
    order = _scan_order(nc, ncc, rev)

    def body(x_ref, dtr_ref, sp_ref, y_ref, hin_ref, st):
        @pl.when(pl.program_id(0) == 0)
        def _():
            st[...] = jnp.zeros_like(st)
        dt, _, acs, msk, _, last = _ssd_prep(dtr_ref[...], sp_ref[...], rev)
        acs_t, dt_t = acs.T, dt.T
        eacs = jnp.exp(acs)
        eal = jnp.exp(acs[last:last + 1, :])
        tew = jnp.exp(acs[last:last + 1, :] - acs) * dt
        lane_lt = lax.broadcasted_iota(jnp.int32, (Q, LANE), 1) < 64
        for g in range(2):
            gl = slice(g * 512, (g + 1) * 512)
            bg = x_ref[:, 1024 + g * 128:1152 + g * 128]
            cg = x_ref[:, 1280 + g * 128:1408 + g * 128]
            s_g = _nt(cg, bg)
            h_t = st[:, gl]
            hin_ref[:, gl] = h_t
            yoff = _nn(cg, h_t)
            xw = []
            for mm in range(4):
                m = 4 * g + mm
                ls = slice(m * LANE, (m + 1) * LANE)
                x2 = x_ref[:, ls]
                ws = []
                for hh in range(2):
                    h = lo + 2 * m + hh
                    lm = jnp.exp(jnp.where(msk, acs[:, h:h + 1] - acs_t[h:h + 1, :], -jnp.inf))
                    ws.append(s_g * lm * dt_t[h:h + 1, :])
                y2 = _nn(jnp.concatenate(ws, axis=1), _halves(x2, lane_lt))
                y_ref[:, ls] = y2 + yoff[:, mm * LANE:(mm + 1) * LANE] * _pair_sel(eacs, lo, m, lane_lt)
                xw.append(x2 * _pair_sel(tew, lo, m, lane_lt))
            st[:, gl] = _head_lanes(eal, lo, g) * h_t + _tn_dot(bg, jnp.concatenate(xw, axis=1))

    return _call(body, "ssd_fwd_r" if rev else "ssd_fwd_f", (nc,),
                 [pl.BlockSpec((Q, XBC), lambda s: (order(s), 0)), pl.BlockSpec((Q, LANE), lambda s: (order(s), 0)),
                  _full(8, LANE)],
                 [pl.BlockSpec((Q, D), lambda s: (order(s), 0)), pl.BlockSpec((None, LANE, D), lambda s: (order(s), 0, 0))],
                 [_sds((n_rows, D)), _sds((nc, LANE, D))], scratch=[pltpu.VMEM((LANE, D), F32)])(xbc, dtr, sp)


def _ssd_bwd(xbc, dtr, dy, hin, sp, dl, eh, ncc, rev):
    n_rows = xbc.shape[0]
    nc = n_rows // Q
    lo = 16 if rev else 0
    fwd_order = _scan_order(nc, ncc, rev)
    order = lambda s: fwd_order(nc - 1 - s)
    with_skip = not rev

    def body(x_ref, dtr_ref, dy_ref, hin_ref, sp_ref, dl_ref, eh_ref, dx_ref, ddtr_ref, pg_ref, dst):
        @pl.when(pl.program_id(0) == 0)
        def _():
            dst[...] = jnp.zeros_like(dst)
            pg_ref[...] = jnp.zeros_like(pg_ref)
        dtr_v = dtr_ref[...]
        dt, a_neg, acs, msk, tri, last = _ssd_prep(dtr_v, sp_ref[...], rev)
        acs_t = acs.T
        r = lax.broadcasted_iota(jnp.int32, (Q, Q), 0)
        c = lax.broadcasted_iota(jnp.int32, (Q, Q), 1)
        msk_t = (c <= r) if rev else (c >= r)
        eacs = jnp.exp(acs)
        eal = jnp.exp(acs[last:last + 1, :])
        te = jnp.exp(acs[last:last + 1, :] - acs)
        lane = lax.broadcasted_iota(jnp.int32, (Q, LANE), 1)
        lane1 = lax.broadcasted_iota(jnp.int32, (1, LANE), 1)
        lane_lt = lane < 64
        dacs = jnp.zeros((Q, LANE), F32)
        ddt_x = jnp.zeros((Q, LANE), F32)
        dlast = jnp.zeros((1, LANE), F32)
        hs_rows = []
        sub16 = lax.broadcasted_iota(jnp.int32, (16, Q), 0)
        dacs_t = jnp.zeros((16, Q), F32)
        for g in range(2):
            gl = slice(g * 512, (g + 1) * 512)
            bg = x_ref[:, 1024 + g * 128:1152 + g * 128]
            cg = x_ref[:, 1280 + g * 128:1408 + g * 128]
            s_g = _nt(cg, bg)
            s_gt = _nt(bg, cg)
            h_t, dh_t = hin_ref[:, gl], dst[:, gl]
            bh = _nn(bg, dh_t)
            yoff = _nn(cg, h_t)
            d_s = jnp.zeros((Q, Q), F32)
            edy, exd = [], []
            for mm in range(4):
                m = 4 * g + mm
                ls = slice(m * LANE, (m + 1) * LANE)
                x2, dy2 = x_ref[:, ls], dy_ref[:, ls]
                bh2 = bh[:, mm * LANE:(mm + 1) * LANE]
                dtm, em, eam = (_pair_sel(v, lo, m, lane_lt) for v in (dt, te, eacs))
                xd2 = x2 * dtm
                lms, mts = [], []
                for hh in range(2):
                    h = lo + 2 * m + hh
                    col, row = acs[:, h:h + 1], acs_t[h:h + 1, :]
                    lms.append(jnp.exp(jnp.where(msk, col - row, -jnp.inf)))
                    mts.append(s_gt * jnp.exp(jnp.where(msk_t, row - col, -jnp.inf)))
                dy_st = _halves(dy2, lane_lt)
                dxd2 = em * bh2 + _nn(jnp.concatenate(mts, axis=1), dy_st)
                dm_st = _nt(dy_st, xd2)
                dmt_st = _nt(_halves(xd2, lane_lt), dy2)
                d_s = d_s + dm_st[:Q] * lms[0] + dm_st[Q:] * lms[1]
                v1, v2, v3 = dy2 * yoff[:, mm * LANE:(mm + 1) * LANE] * eam, dxd2 * x2, xd2 * bh2 * em
                for hh in range(2):
                    h = lo + 2 * m + hh
                    half = lane_lt == (hh == 0)
                    g_rows = _sum0(dmt_st[hh * Q:(hh + 1) * Q] * mts[hh]) - _sum0(dm_st[hh * Q:(hh + 1) * Q] * s_g * lms[hh])
                    dacs_t = jnp.where(sub16 == 2 * m + hh, g_rows, dacs_t)
                    r1 = jnp.sum(jnp.where(half, v1, 0.0), axis=1, keepdims=True)
                    r2 = jnp.sum(jnp.where(half, v2, 0.0), axis=1, keepdims=True)
                    r3 = jnp.sum(jnp.where(half, v3, 0.0), axis=1, keepdims=True)
                    dacs = dacs + jnp.where(lane == h, r1 - r3, 0.0)
                    ddt_x = ddt_x + jnp.where(lane == h, r2, 0.0)
                    dlast = dlast + jnp.where(lane1 == h, _sum0(r3), 0.0)
                dx2 = dxd2 * dtm
                if with_skip:
                    dx2 = dx2 + dl_ref[:, ls] * dy2
                dx_ref[:, ls] = dx2
                edy.append(eam * dy2)
                exd.append(em * xd2)
            edy, exd = jnp.concatenate(edy, axis=1), jnp.concatenate(exd, axis=1)
            hs_rows.append(_sum0(h_t * dh_t))
            dst[:, gl] = _head_lanes(eal, lo, g) * dh_t + _tn_dot(cg, edy)
            dx_ref[:, 1024 + g * 128:1152 + g * 128] = _tn_dot(d_s, cg) + _nt(exd, dh_t)
            dx_ref[:, 1280 + g * 128:1408 + g * 128] = _nn(d_s, bg) + _nt(edy, h_t)
        hs = jnp.broadcast_to(jnp.concatenate(hs_rows, axis=1), (8, D))
        hsum = jnp.dot(hs, eh_ref[...], precision=HI, preferred_element_type=F32)[0:1]
        dlast = dlast + eal * hsum
        dacs = dacs + jnp.concatenate([jnp.zeros((lo, Q), F32)] * (lo > 0) + [dacs_t, jnp.zeros((LANE - 16 - lo, Q), F32)],
                                      axis=0).T
        rowi = lax.broadcasted_iota(jnp.int32, (Q, LANE), 0)
        dacs = dacs + jnp.where(rowi == last, dlast, 0.0)
        da = lax.dot_general(tri, dacs, (((0,), (0,)), ((), ())), precision=HI, preferred_element_type=F32)
        ddt = ddt_x + da * a_neg
        mine = (lane >= lo) & (lane < lo + 16)
        ddtr = jnp.where(mine, ddt * jax.nn.sigmoid(dtr_v + sp_ref[0:1]), 0.0)
        ddtr_ref[...] = ddtr
        pg_ref[0:1] += _sum0(ddtr)
        pg_ref[1:2] += jnp.where(mine[0:1], _sum0(da * dt) * a_neg, 0.0)

    blk = lambda w_: pl.BlockSpec((Q, w_), lambda s: (order(s), 0))
    return _call(body, "ssd_bwd_r" if rev else "ssd_bwd_f", (nc,),
                 [blk(XBC), blk(LANE), blk(D), pl.BlockSpec((None, LANE, D), lambda s: (order(s), 0, 0)),
                  _full(8, LANE), _full(1, D), _full(D, LANE)],
                 [blk(XBC), blk(LANE), _full(8, LANE)],
                 [_sds((n_rows, XBC)), _sds((n_rows, LANE)), _sds((8, LANE))],
                 scratch=[pltpu.VMEM((LANE, D), F32)])(xbc, dtr, dy, hin, sp, dl, eh)


def _ssd_fwd_old(xbc, dtr, sp, ncc, rev):
    n_rows = xbc.shape[0]
    nc = n_rows // Q
    lo = 16 if rev else 0
    order = _scan_order(nc, ncc, rev)

    def body(x_ref, dtr_ref, sp_ref, y_ref, hin_ref, st):
        @pl.when(pl.program_id(0) == 0)
        def _():
            st[...] = jnp.zeros_like(st)
        dt, _, acs, msk, _, last = _ssd_prep(dtr_ref[...], sp_ref[...], rev)
        acs_t, dt_t = acs.T, dt.T
        eacs = jnp.exp(acs)
        eal = jnp.exp(acs[last:last + 1, :])
        tew = jnp.exp(acs[last:last + 1, :] - acs) * dt
        lane_lt = lax.broadcasted_iota(jnp.int32, (Q, LANE), 1) < 64
        row_lt = lax.broadcasted_iota(jnp.int32, (LANE, 1), 0) < 64
        s_g = [_nt(x_ref[:, 1280 + g * 128:1408 + g * 128], x_ref[:, 1024 + g * 128:1152 + g * 128]) for g in range(2)]
        for m in range(NPAIR):
            g = m // 4
            ls = slice(m * LANE, (m + 1) * LANE)
            x2 = x_ref[:, ls]
            bg = x_ref[:, 1024 + g * 128:1152 + g * 128]
            cg = x_ref[:, 1280 + g * 128:1408 + g * 128]
            y2 = jnp.zeros((Q, LANE), F32)
            for hh in range(2):
                h = lo + 2 * m + hh
                lm = jnp.exp(jnp.where(msk, acs[:, h:h + 1] - acs_t[h:h + 1, :], -jnp.inf))
                w = s_g[g] * lm * dt_t[h:h + 1, :]
                y2 = y2 + _nn(w, jnp.where(lane_lt == (hh == 0), x2, 0.0))
            hp = st[ls, :]
            hin_ref[ls, :] = hp
            y_ref[:, ls] = y2 + _nt(cg, hp) * _pair_sel(eacs, lo, m, lane_lt)
            snew = _tn_dot(x2 * _pair_sel(tew, lo, m, lane_lt), bg)
            h0 = lo + 2 * m
            st[ls, :] = jnp.where(row_lt, eal[:, h0:h0 + 1], eal[:, h0 + 1:h0 + 2]) * hp + snew

    return _call(body, "ssd_fwd_r" if rev else "ssd_fwd_f", (nc,),
                 [pl.BlockSpec((Q, XBC), lambda s: (order(s), 0)), pl.BlockSpec((Q, LANE), lambda s: (order(s), 0)),
                  _full(8, LANE)],
                 [pl.BlockSpec((Q, D), lambda s: (order(s), 0)), pl.BlockSpec((None, D, LANE), lambda s: (order(s), 0, 0))],
                 [_sds((n_rows, D)), _sds((nc, D, LANE))], scratch=[pltpu.VMEM((D, LANE), F32)])(xbc, dtr, sp)


def _ssd_bwd_old(xbc, dtr, dy, hin, sp, dl, ncc, rev):
    n_rows = xbc.shape[0]
    nc = n_rows // Q
    lo = 16 if rev else 0
    fwd_order = _scan_order(nc, ncc, rev)
    order = lambda s: fwd_order(nc - 1 - s)
    with_skip = not rev

    def body(x_ref, dtr_ref, dy_ref, hin_ref, sp_ref, dl_ref, dx_ref, ddtr_ref, pg_ref, dst):
        @pl.when(pl.program_id(0) == 0)
        def _():
            dst[...] = jnp.zeros_like(dst)
            pg_ref[...] = jnp.zeros_like(pg_ref)
        dtr_v = dtr_ref[...]
        dt, a_neg, acs, msk, tri, last = _ssd_prep(dtr_v, sp_ref[...], rev)
        acs_t, dt_t = acs.T, dt.T
        eacs = jnp.exp(acs)
        eal = jnp.exp(acs[last:last + 1, :])
        te = jnp.exp(acs[last:last + 1, :] - acs)
        lane = lax.broadcasted_iota(jnp.int32, (Q, LANE), 1)
        sub = lax.broadcasted_iota(jnp.int32, (LANE, Q), 0)
        lane1 = lax.broadcasted_iota(jnp.int32, (1, LANE), 1)
        lane_lt = lane < 64
        row_lt = lax.broadcasted_iota(jnp.int32, (LANE, 1), 0) < 64
        bgs = [x_ref[:, 1024 + g * 128:1152 + g * 128] for g in range(2)]
        cgs = [x_ref[:, 1280 + g * 128:1408 + g * 128] for g in range(2)]
        s_g = [_nt(cgs[g], bgs[g]) for g in range(2)]
        d_s = [jnp.zeros((Q, Q), F32), jnp.zeros((Q, Q), F32)]
        dc_x = [jnp.zeros((Q, LANE), F32), jnp.zeros((Q, LANE), F32)]
        db_x = [jnp.zeros((Q, LANE), F32), jnp.zeros((Q, LANE), F32)]
        dacs = jnp.zeros((Q, LANE), F32)
        colsum_t = jnp.zeros((LANE, Q), F32)
        ddt_x = jnp.zeros((Q, LANE), F32)
        dlast = jnp.zeros((1, LANE), F32)
        for m in range(NPAIR):
            g = m // 4
            ls = slice(m * LANE, (m + 1) * LANE)
            x2, dy2 = x_ref[:, ls], dy_ref[:, ls]
            hp, dhp = hin_ref[ls, :], dst[ls, :]
            dtm, em, eam = (_pair_sel(v, lo, m, lane_lt) for v in (dt, te, eacs))
            xd2 = x2 * dtm
            bh = _nt(bgs[g], dhp)
            ch = _nt(cgs[g], hp)
            dxd2 = em * bh
            for hh in range(2):
                h = lo + 2 * m + hh
                half = lane_lt == (hh == 0)
                lm = jnp.exp(jnp.where(msk, acs[:, h:h + 1] - acs_t[h:h + 1, :], -jnp.inf))
                mh = s_g[g] * lm
                dyh = jnp.where(half, dy2, 0.0)
                dxd2 = dxd2 + _tn_dot(mh, dyh)
                dm = _nt(dyh, jnp.where(half, xd2, 0.0))
                d_s[g] = d_s[g] + dm * lm
                gh = dm * mh
                dacs = dacs + jnp.where(lane == h, jnp.sum(gh, axis=1, keepdims=True), 0.0)
                colsum_t = colsum_t + jnp.where(sub == h, jnp.sum(gh, axis=0, keepdims=True), 0.0)
                t1 = jnp.sum(jnp.where(half, dy2 * ch * eam, 0.0), axis=1, keepdims=True)
                rj = jnp.sum(jnp.where(half, xd2 * bh * em, 0.0), axis=1, keepdims=True)
                dacs = dacs + jnp.where(lane == h, t1 - rj, 0.0)
                hs = hp * dhp
                hsum = jnp.sum(jnp.sum(jnp.where(row_lt == (hh == 0), hs, 0.0), axis=1, keepdims=True), axis=0, keepdims=True)
                dlast = dlast + jnp.where(lane1 == h, jnp.sum(rj, axis=0, keepdims=True) + eal[:, h:h + 1] * hsum, 0.0)
            for hh in range(2):
                h = lo + 2 * m + hh
                half = lane_lt == (hh == 0)
                ddt_x = ddt_x + jnp.where(lane == h, jnp.sum(jnp.where(half, dxd2 * x2, 0.0), axis=1, keepdims=True), 0.0)
            dx2 = dxd2 * dtm
            if with_skip:
                dx2 = dx2 + dl_ref[:, ls] * dy2
            dx_ref[:, ls] = dx2
            edy = eam * dy2
            dc_x[g] = dc_x[g] + _nn(edy, hp)
            db_x[g] = db_x[g] + _nn(em * xd2, dhp)
            h0 = lo + 2 * m
            dst[ls, :] = jnp.where(row_lt, eal[:, h0:h0 + 1], eal[:, h0 + 1:h0 + 2]) * dhp + _tn_dot(edy, cgs[g])
        dacs = dacs - colsum_t.T
        rowi = lax.broadcasted_iota(jnp.int32, (Q, LANE), 0)
        dacs = dacs + jnp.where(rowi == last, dlast, 0.0)
        da = lax.dot_general(tri, dacs, (((0,), (0,)), ((), ())), precision=HI, preferred_element_type=F32)
        ddt = ddt_x + da * a_neg
        mine = (lane >= lo) & (lane < lo + 16)
        ddtr = jnp.where(mine, ddt * jax.nn.sigmoid(dtr_v + sp_ref[0:1]), 0.0)
        ddtr_ref[...] = ddtr
        pg_ref[0:1] += _sum0(ddtr)
        pg_ref[1:2] += jnp.where(mine[0:1], _sum0(da * dt) * a_neg, 0.0)
        for g in range(2):
            dx_ref[:, 1024 + g * 128:1152 + g * 128] = _tn_dot(d_s[g], cgs[g]) + db_x[g]
            dx_ref[:, 1280 + g * 128:1408 + g * 128] = _nn(d_s[g], bgs[g]) + dc_x[g]

    return _call(body, "ssd_bwd_r" if rev else "ssd_bwd_f", (nc,),
                 [pl.BlockSpec((Q, XBC), lambda s: (order(s), 0)), pl.BlockSpec((Q, LANE), lambda s: (order(s), 0)),
                  pl.BlockSpec((Q, D), lambda s: (order(s), 0)), pl.BlockSpec((None, D, LANE), lambda s: (order(s), 0, 0)),
                  _full(8, LANE), _full(1, D)],
                 [pl.BlockSpec((Q, XBC), lambda s: (order(s), 0)), pl.BlockSpec((Q, LANE), lambda s: (order(s), 0)),
                  _full(8, LANE)],
                 [_sds((n_rows, XBC)), _sds((n_rows, LANE)), _sds((8, LANE))],
                 scratch=[pltpu.VMEM((D, LANE), F32)])(xbc, dtr, dy, hin, sp, dl)


def _hy4_fwd(h, yf, yb, xbc, z, ysg, mp, dl, ng, wout, nct):
    n_rows = h.shape[0]

    def body(h_ref, yf_ref, yb_ref, xs_ref, z_ref, ysg_ref, mp_ref, dl_ref, ng_ref, wout_ref, hn_ref, yssd_ref, out_ref):
        ytot = yf_ref[...] + yb_ref[...] + dl_ref[...] * xs_ref[...]
        yssd = _gate_norm(ytot, z_ref[...], ng_ref[...]).astype(MXU)
        yssd_ref[...] = yssd.astype(ACT)
        out = (jnp.dot(yssd, wout_ref[0:D, :], preferred_element_type=F32)
               + jnp.dot(ysg_ref[...].astype(MXU), wout_ref[D:2 * D, :], preferred_element_type=F32))
        out_ref[...] = out
        hn_ref[...] = h_ref[...] + mp_ref[3:4] * out

    return _call(body, "hy4_fwd", (n_rows // T,),
                 [_rows(T, D), _rows(T, D, nct), _rows(T, D, nct), _rows(T, D, nct), _rows(T, D, nct), _rows(T, D, nct),
                  _full(8, D), _full(1, D), _full(1, D), _wfull(2 * D, D)],
                 [_rows(T, D), _rows(T, D), _rows(T, D)],
                 [_sds((n_rows, D)), _sds((n_rows, D), ACT), _sds((n_rows, D))])(h, yf, yb, xbc, z, ysg, mp, dl, ng, wout)


def _hy4_bwd(dh, out, yf, yb, xbc, z, mp, dl, ng, wout, nct):
    n_lat = dh.shape[0]
    n_rows = yf.shape[0]

    def body(dh_ref, out_ref, yf_ref, yb_ref, xs_ref, z_ref, mp_ref, dl_ref, ng_ref, wout_ref,
             dy_ref, dz_ref, dysg_ref, doutb_ref, pg_ref):
        i = pl.program_id(0)

        @pl.when(i == 0)
        def _():
            pg_ref[...] = jnp.zeros_like(pg_ref)

        @pl.when(i < nct)
        def _():
            dy_ref[...] = jnp.zeros_like(dy_ref)
            dz_ref[...] = jnp.zeros_like(dz_ref)
            dysg_ref[...] = jnp.zeros_like(dysg_ref)
            doutb_ref[...] = jnp.zeros_like(doutb_ref)

        @pl.when(i >= nct)
        def _():
            dhp = dh_ref[...]
            doutb = (mp_ref[3:4] * dhp).astype(MXU)
            doutb_ref[...] = doutb.astype(ACT)
            dysg_ref[...] = _nt(doutb, wout_ref[D:2 * D, :])
            dyssd = _nt(doutb, wout_ref[0:D, :])
            xs = xs_ref[...]
            ytot = yf_ref[...] + yb_ref[...] + dl_ref[...] * xs
            _, vjp = jax.vjp(_gate_norm, ytot, z_ref[...], ng_ref[...])
            dytot, dz, dng = vjp(dyssd)
            dy_ref[...] = dytot
            dz_ref[...] = dz.astype(ACT)
            pg_ref[0:1] += _sum0(dhp * out_ref[...])
            pg_ref[1:2] += dng
            pg_ref[2:3] += _sum0(dytot * xs)

    return _call(body, "hy4_bwd", (n_rows // T,),
                 [_rows_lat(T, D, nct), _rows_lat(T, D, nct), _rows(T, D), _rows(T, D), _rows(T, D), _rows(T, D),
                  _full(8, D), _full(1, D), _full(1, D), _wfull(2 * D, D)],
                 [_rows(T, D), _rows(T, D), _rows(T, D), _rows_lat(T, D, nct), _full(8, D)],
                 [_sds((n_rows, D)), _sds((n_rows, D), ACT), _sds((n_rows, D)), _sds((n_lat, D), ACT), _sds((8, D))],
                 )(dh, out, yf, yb, xbc, z, mp, dl, ng, wout)


def _loss_bwd(h, tgt, fng):
    n_rows = h.shape[0]

    def body(h_ref, t_ref, g_ref, dh_ref, pg_ref, ls_ref):
        @pl.when(pl.program_id(0) == 0)
        def _():
            pg_ref[...] = jnp.zeros_like(pg_ref)
            ls_ref[...] = jnp.zeros_like(ls_ref)
        hv = h_ref[...]
        g = g_ref[...]
        r = lax.rsqrt(jnp.mean(hv * hv, axis=-1, keepdims=True) + EPS)
        n = hv * r
        e = n * g - t_ref[...]
        ls_ref[...] += 0.5 * jnp.sum(jnp.sum(e * e, axis=1, keepdims=True), axis=0, keepdims=True) * (1.0 / D)
        dyv = e * (1.0 / D)
        pg_ref[0:1] += _sum0(dyv * n)
        dn = dyv * g
        dh_ref[...] = r * (dn - n * jnp.mean(dn * n, axis=-1, keepdims=True))

    return _call(body, "loss_bwd", (n_rows // T,), [_rows(T, D), _rows(T, D), _full(1, D)],
                 [_rows(T, D), _full(8, D), _full(8, LANE)],
                 [_sds((n_rows, D)), _sds((8, D)), _sds((8, LANE))])(h, tgt, fng)


def _pad_rows(a, rows):
    return jnp.concatenate([a, jnp.zeros((rows - a.shape[0],) + a.shape[1:], a.dtype)], axis=0)


def _mp(*rows):
    return _pad_rows(jnp.stack(rows, axis=0), 8)


def _local_step(x, ctx, tgt, ada, cada0, w, late_w=None, early_grads=None):
    n_lat, n_ctx = x.shape[0], ctx.shape[0]
    nct, ncc = n_ctx // T, n_ctx // Q
    a0 = [ada[0, k * D:(k + 1) * D] for k in range(6)]
    a1 = [ada[1, k * D:(k + 1) * D] for k in range(6)]
    c0 = [cada0[k * D:(k + 1) * D] for k in range(6)]
    g = {}

    hcat = jnp.concatenate([ctx, x], axis=0)
    mp2 = jnp.stack([_mp(w["norm_mix_g"][0], c0[0], c0[1]), _mp(w["norm_mix_g"][0], a0[0], a0[1], a0[2])], axis=0)
    mp_l0 = mp2[1]
    sgbt = _pad_cols(w["sg_b"][0].T, LANE)
    lng, lnb = w["sg_ln_g"][0][None], w["sg_ln_b"][0][None]
    z, uv, xbcp, dtr, ysg = _hy1_fwd(hcat, mp2, w["wz"], w["wuv"], w["wxbc"], w["wdt"], lng, lnb, w["sg_w"], sgbt, nct)
    cw = _pad_rows(w["ssd_conv_w"][0], 8)
    cb = w["ssd_conv_b"][0][None]
    xbc = _cv5_fwd(xbcp, cw, cb, nct)
    sp = _pad_rows(jnp.stack([_pad_cols(w["ssd_dt_bias"][0].reshape(1, 32), LANE)[0],
                              _pad_cols(w["ssd_a_log"][0].reshape(1, 32), LANE)[0]], axis=0), 8)
    dl = jnp.repeat(w["ssd_d"][0], 64)[None]
    ng = w["ssd_norm_g"][0][None]
    yf, hin_f = _ssd_fwd(xbc, dtr, sp, ncc, False)
    yb, hin_b = _ssd_fwd(xbc, dtr, sp, ncc, True)
    if late_w is not None:
        w = {**w, **late_w(yb)}
    h1, yssd, out0 =_hy4_fwd(x, yf, yb, xbc, z, ysg, mp_l0, dl, ng, w["hy_w_out"], nct)

    mpm0 = _mp(w["norm_mlp_g"][0], a0[3], a0[4], a0[5])
    h2, am0, ym0 = _mlp_fwd(h1, mpm0, w["mlp_w1"][0], w["mlp_w2"][0], "mlp0_fwd")

    mpc = _mp(w["norm_mix_g"][1], a1[0], a1[1], a1[2])
    wdw = _pad_rows(w["cf_w_dw"][0], 32)
    glu, acf = _cf1_fwd(h2, mpc, w["cf_w_pw1"], w["cf_b_pw1"])
    h3, cv, scf, ycf = _cf2_fwd(h2, glu, mpc, wdw, w["cf_b_dw"], w["cf_ln_g"], w["cf_ln_b"], w["cf_w_pw2"], w["cf_b_pw2"])

    mpm1 = _mp(w["norm_mlp_g"][1], a1[3], a1[4], a1[5])
    h4, am1, ym1 = _mlp_fwd(h3, mpm1, w["mlp_w1"][1], w["mlp_w2"][1], "mlp1_fwd")

    dh4, pg_f, ls = _loss_bwd(h4, tgt, w["final_norm_g"][None])
    loss = ls[0, 0]
    g["final_norm_g"] = pg_f[0]

    dh3, da1, dy1, u1, pgm1 = _mlp_bwd(dh4, h3, am1, ym1, mpm1, w["mlp_w1"][1], w["mlp_w2"][1], "mlp1_bwd")
    gw1_1 = _tn(u1, da1, "tn_mlp1_w1")
    gw2_1 = _tn(am1, dy1, "tn_mlp1_w2", relu2=True)

    dcv, dycf, pgc2 = _cf2_bwd(dh3, ycf, cv, mpc, w["cf_ln_g"], w["cf_ln_b"], w["cf_w_pw2"])
    g["cf_w_pw2"] = _tn(scf, dycf, "tn_cf_pw2")
    dh2, dacf, ucf, pgc1, pbc1, dwdw = _cf1_bwd(dh3, h2, acf, dcv, glu, mpc, wdw, w["cf_w_pw1"])
    g["cf_w_pw1"] = _tn(ucf, dacf, "tn_cf_pw1")
    g["cf_b_pw2"], g["cf_ln_g"], g["cf_ln_b"], g["cf_b_dw"] = pgc2[1], pgc2[2], pgc2[3], pgc2[4]
    g["cf_b_pw1"] = pbc1[0]
    g["cf_w_dw"] = dwdw[:CK]

    dh1, da0, dy0, u0, pgm0 = _mlp_bwd(dh2, h1, am0, ym0, mpm0, w["mlp_w1"][0], w["mlp_w2"][0], "mlp0_bwd")
    g["mlp_w1"] = jnp.stack([_tn(u0, da0, "tn_mlp0_w1"), gw1_1])
    g["mlp_w2"] = jnp.stack([_tn(am0, dy0, "tn_mlp0_w2", relu2=True), gw2_1])
    g["norm_mlp_g"] = jnp.stack([pgm0[0], pgm1[0]])
    if early_grads is not None:
        mp_l0 = mp_l0 + early_grads(g)

    dyt, dz, dysg, doutb, pg4 = _hy4_bwd(dh1, out0, yf, yb, xbc, z, mp_l0, dl, ng, w["hy_w_out"], nct)
    ysg_lat = lax.slice_in_dim(ysg, n_ctx, n_ctx + n_lat, axis=0)
    g["hy_w_out"] = jnp.concatenate([_tn(yssd, doutb, "tn_out_ssd"), _tn(ysg_lat, doutb, "tn_out_sg")], axis=0)
    head_of_lane = jnp.arange(D, dtype=jnp.int32)[:, None] // 64
    col = jnp.arange(LANE, dtype=jnp.int32)[None, :]
    dxf, ddf, pgsf = _ssd_bwd(xbc, dtr, dyt, hin_f, sp, dl, (col == head_of_lane).astype(F32), ncc, False)
    dxb, ddb, pgsb = _ssd_bwd(xbc, dtr, dyt, hin_b, sp, dl, (col == head_of_lane + 16).astype(F32), ncc, True)
    dcv5, pgcb = _cv5_bwd1(xbcp, dxf, dxb, cw, cb, nct)
    dxbcp, dcw = _cv5_bwd2(dcv5, xbcp, cw, nct)
    dx, ucat, duv, ddt, pg2, pln, dsgw, dsgbt = _hy1_bwd(
        hcat, uv, dz, dxbcp, ddf, ddb, dysg, dh1, mp2, w["wz"], w["wuv"], w["wxbc"], w["wdt"], lng, lnb, w["sg_w"], sgbt, nct)
    g["hy_w_in"] = jnp.concatenate([_tn(ucat, dz, "tn_in_z"), _tn(ucat, dxbcp, "tn_in_xbc"),
                                    _tn(ucat, ddt, "tn_in_dt")[:, :32], _tn(ucat, duv, "tn_in_uv")], axis=1)
    g["ssd_conv_w"], g["ssd_conv_b"] = dcw[:SK], pgcb[0]
    pgs = pgsf + pgsb
    g["ssd_dt_bias"], g["ssd_a_log"] = pgs[0, :32].reshape(2, 16), pgs[1, :32].reshape(2, 16)
    g["ssd_d"] = jnp.sum(pg4[2].reshape(16, 64), axis=1)
    g["ssd_norm_g"] = pg4[1]
    g["sg_ln_g"], g["sg_ln_b"] = pln[0], pln[1]
    g["sg_w"], g["sg_b"] = dsgw, dsgbt[:, :SGG].T
    g["norm_mix_g"] = jnp.stack([pg2[0, 0] + pg2[1, 0], pgc1[0]])

    zero = jnp.zeros((D,), F32)
    d_ada = jnp.stack([jnp.concatenate([pg2[1, 1], pg2[1, 2], pg4[0], pgm0[1], pgm0[2], pgm0[3]]),
                       jnp.concatenate([pgc1[1], pgc1[2], pgc2[0], pgm1[1], pgm1[2], pgm1[3]])])
    d_cada0 = jnp.concatenate([pg2[0, 1], pg2[0, 2], zero, zero, zero, zero])
    return loss, dx, g, d_ada, d_cada0


def _pad_cols(a, cols):
    return jnp.concatenate([a, jnp.zeros(a.shape[:-1] + (cols - a.shape[-1],), a.dtype)], axis=-1)


MESH = pl.DeviceIdType.MESH
ANY = pl.BlockSpec(memory_space=pl.ANY)
IN_VMEM = pl.BlockSpec(memory_space=pltpu.VMEM)


def _coords():
    return lax.axis_index("x"), lax.axis_index("y"), lax.axis_index("c")


def _ag8(x, name):
    r, wd = x.shape

    def body(x_ref, o_ref, send, recv, lsem):
        mx, my, mc = _coords()
        me = 4 * mx + 2 * my + mc
        mine = pltpu.make_async_copy(x_ref, o_ref.at[me], lsem)
        mine.start()
        sent, peers = [], []
        for k in range(1, 8):
            px = 1 - mx if k & 4 else mx
            py = 1 - my if k & 2 else my
            pc = 1 - mc if k & 1 else mc
            cp = pltpu.make_async_remote_copy(src_ref=x_ref, dst_ref=o_ref.at[me], send_sem=send.at[k - 1],
                                              recv_sem=recv.at[k - 1], device_id=(px, py, pc), device_id_type=MESH)
            cp.start()
            sent.append(cp)
            peers.append((4 * px + 2 * py + pc, (px, py, pc)))
        for k in range(1, 8):
            slot, peer = peers[k - 1]
            pltpu.make_async_remote_copy(src_ref=x_ref, dst_ref=o_ref.at[slot], send_sem=send.at[k - 1],
                                         recv_sem=recv.at[k - 1], device_id=peer, device_id_type=MESH).wait_recv()
        for cp in sent:
            cp.wait_send()
        mine.wait()

    return pl.pallas_call(
        body, name=name, out_shape=_sds((8, r, wd), x.dtype), in_specs=[IN_VMEM], out_specs=IN_VMEM,
        scratch_shapes=[pltpu.SemaphoreType.DMA((7,)), pltpu.SemaphoreType.DMA((7,)), pltpu.SemaphoreType.DMA(())],
        compiler_params=pltpu.CompilerParams(vmem_limit_bytes=VMEM_LIMIT))(x)


def _xchg4(buf, name, a2a):
    r, wd = buf.shape[-2:]

    def body(in_ref, o_ref, send, recv, lsem):
        mx, my, mc = _coords()
        me = 2 * mx + my
        mine = pltpu.make_async_copy(in_ref.at[me] if a2a else in_ref, o_ref.at[me], lsem)
        mine.start()
        sent, peers = [], []
        for k in range(1, 4):
            px = 1 - mx if k & 2 else mx
            py = 1 - my if k & 1 else my
            pj = 2 * px + py
            cp = pltpu.make_async_remote_copy(src_ref=in_ref.at[pj] if a2a else in_ref, dst_ref=o_ref.at[me],
                                              send_sem=send.at[k - 1], recv_sem=recv.at[k - 1],
                                              device_id=(px, py, mc), device_id_type=MESH)
            cp.start()
            sent.append(cp)
            peers.append((pj, (px, py, mc)))
        for k in range(1, 4):
            pj, peer = peers[k - 1]
            pltpu.make_async_remote_copy(src_ref=in_ref.at[pj] if a2a else in_ref, dst_ref=o_ref.at[pj],
                                         send_sem=send.at[k - 1], recv_sem=recv.at[k - 1],
                                         device_id=peer, device_id_type=MESH).wait_recv()
        for cp in sent:
            cp.wait_send()
        mine.wait()

    return pl.pallas_call(
        body, name=name, out_shape=_sds((4, r, wd), buf.dtype), in_specs=[ANY], out_specs=ANY,
        scratch_shapes=[pltpu.SemaphoreType.DMA((3,)), pltpu.SemaphoreType.DMA((3,)), pltpu.SemaphoreType.DMA(())],
        )(buf)


HBM = pl.BlockSpec(memory_space=pltpu.HBM)
SEM = pl.BlockSpec(memory_space=pltpu.SEMAPHORE)
EFFECT = pltpu.SideEffectType.DATAFLOW_SIDE_EFFECTING


def _x4_peers(in_ref, land_ref, send, recv, a2a):
    mx, my, mc = _coords()
    me = 2 * mx + my
    out = []
    for k in range(1, 4):
        px = 1 - mx if k & 2 else mx
        py = 1 - my if k & 1 else my
        pj = 2 * px + py
        mk = functools.partial(pltpu.make_async_remote_copy, src_ref=in_ref.at[pj] if a2a else in_ref,
                               send_sem=send.at[k - 1], recv_sem=recv.at[k - 1], device_id=(px, py, mc), device_id_type=MESH)
        out.append((mk(dst_ref=land_ref.at[me]), mk(dst_ref=land_ref.at[pj])))
    return out


def _x4_start(buf, name, a2a):
    r, wd = buf.shape[-2:]

    def body(in_ref, land_ref, send, recv, in_thru, land_thru, token):
        for start, _ in _x4_peers(in_ref, land_ref, send, recv, a2a):
            start.start()
        token[...] = jnp.zeros_like(token)

    land = lax.empty((4, r, wd), buf.dtype)
    return pl.pallas_call(
        body, name=name,
        out_shape=(pltpu.SemaphoreType.DMA((3,)), pltpu.SemaphoreType.DMA((3,)), pltpu.HBM(buf.shape, buf.dtype),
                   pltpu.HBM(land.shape, land.dtype), _sds((8, LANE))),
        in_specs=(HBM, HBM), out_specs=(SEM, SEM, HBM, HBM, IN_VMEM), input_output_aliases={0: 2, 1: 3},
        compiler_params=pltpu.CompilerParams(has_side_effects=EFFECT),
    )(pltpu.with_memory_space_constraint(buf, pltpu.HBM), pltpu.with_memory_space_constraint(land, pltpu.HBM))


def _x4_wait(send, recv, buf_thru, land_thru, after, name, a2a):
    def body(in_ref, land_ref, send_ref, recv_ref, after_ref, in_dead, got_ref):
        for _, arrive in _x4_peers(in_ref, land_ref, send_ref, recv_ref, a2a):
            arrive.wait_send()
            arrive.wait_recv()

    return pl.pallas_call(
        body, name=name, out_shape=(pltpu.HBM(buf_thru.shape, buf_thru.dtype), pltpu.HBM(land_thru.shape, land_thru.dtype)),
        in_specs=(HBM, HBM, SEM, SEM, ANY), out_specs=(HBM, HBM), input_output_aliases={0: 0, 1: 1},
        compiler_params=pltpu.CompilerParams(has_side_effects=EFFECT),
    )(buf_thru, land_thru, send, recv, after)[1]


def _xchg_sib(x, name):
    def body(in_ref, o_ref, send, recv):
        mx, my, mc = _coords()
        cp = pltpu.make_async_remote_copy(src_ref=in_ref, dst_ref=o_ref, send_sem=send, recv_sem=recv,
                                          device_id=(mx, my, 1 - mc), device_id_type=MESH)
        cp.start()
        cp.wait_recv()
        cp.wait_send()

    return pl.pallas_call(
        body, name=name, out_shape=_sds(x.shape, x.dtype), in_specs=[ANY], out_specs=ANY,
        scratch_shapes=[pltpu.SemaphoreType.DMA(()), pltpu.SemaphoreType.DMA(())])(x)


def _sum_slots(gat, slots, name, tr=None):
    n, r, wd = gat.shape
    tr = r if tr is None else tr

    def body(g_ref, o_ref):
        acc = g_ref[slots[0]].astype(F32)
        for s in slots[1:]:
            acc = acc + g_ref[s].astype(F32)
        o_ref[...] = acc

    return _call(body, name, (r // tr,), [pl.BlockSpec((n, tr, wd), lambda i: (0, i, 0))], _rows(tr, wd), _sds((r, wd)))(gat)


def _add(a, b, name, tr):
    def body(a_ref, b_ref, o_ref):
        o_ref[...] = a_ref[...] + b_ref[...]

    r, wd = a.shape
    return _call(body, name, (r // tr,), [_rows(tr, wd), _rows(tr, wd)], _rows(tr, wd), _sds((r, wd)))(a, b)


def _ada_fwd(x16, ada_w_loc, ada_b_loc):
    nloc = ada_w_loc.shape[-1]

    def body(x_ref, w_ref, b_ref, s_ref, o_ref):
        s = _silu(x_ref[...])
        s_ref[...] = s
        o_ref[...] = jnp.dot(s, w_ref[...], precision=HI, preferred_element_type=F32) + b_ref[...]

    return _call(body, "ada_fwd", (2,),
                 [_full(16, D), pl.BlockSpec((None, D, nloc), lambda l: (l, 0, 0)), pl.BlockSpec((None, 1, nloc), lambda l: (l, 0, 0))],
                 [_full(16, D), pl.BlockSpec((None, 16, nloc), lambda l: (l, 0, 0))],
                 [_sds((16, D)), _sds((2, 16, nloc))])(x16, ada_w_loc, ada_b_loc[:, None, :])


def _ada_bwd(s16, d_loc, ada_w_loc):
    nloc = ada_w_loc.shape[-1]

    def body(s_ref, d_ref, w_ref, gw_ref, cp_ref):
        gw_ref[...] = lax.dot_general(s_ref[...], d_ref[...], (((0,), (0,)), ((), ())), precision=HI,
                                      preferred_element_type=F32)

        @pl.when(pl.program_id(0) == 0)
        def _():
            cp_ref[...] = lax.dot_general(d_ref[8:16, :], w_ref[...], (((1,), (1,)), ((), ())), precision=HI,
                                          preferred_element_type=F32)

    return _call(body, "ada_bwd", (2,),
                 [_full(16, D), pl.BlockSpec((None, 16, nloc), lambda l: (l, 0, 0)), pl.BlockSpec((None, D, nloc), lambda l: (l, 0, 0))],
                 [pl.BlockSpec((None, D, nloc), lambda l: (l, 0, 0)), _full(8, D)],
                 [_sds((2, D, nloc)), _sds((8, D))])(s16, d_loc, ada_w_loc)


def _cctx_grad(dscc, c_ctx):
    def body(d_ref, c_ref, o_ref):
        _, vjp = jax.vjp(_silu, c_ref[...])
        o_ref[...] = vjp(d_ref[...])[0]

    return _call(body, "cctx_grad", (1,), [_full(8, D), _full(8, D)], _full(8, D), _sds((8, D)))(dscc, c_ctx)


def _adamw(w, g, m, v, name):
    shape = w.shape
    wd = shape[-1]
    r = w.size // wd
    tr = 256 if r % 256 == 0 else r
    c1 = 1.0 - ADAM_B1 ** ADAM_STEP
    c2 = 1.0 - ADAM_B2 ** ADAM_STEP

    def body(w_ref, g_ref, m_ref, v_ref, d_ref, mo_ref, vo_ref):
        gv = g_ref[...]
        mn = ADAM_B1 * m_ref[...] + (1.0 - ADAM_B1) * gv
        vn = ADAM_B2 * v_ref[...] + (1.0 - ADAM_B2) * jnp.square(gv)
        mo_ref[...] = mn
        vo_ref[...] = vn
        d_ref[...] = -ADAM_LR * ((mn / c1) / (jnp.sqrt(vn / c2) + ADAM_EPS) + ADAM_WD * w_ref[...])

    spec = _rows(tr, wd)
    outs = _call(body, name, (r // tr,), [spec] * 4, [spec] * 3, [_sds((r, wd))] * 3)(
        *(a.reshape(r, wd) for a in (w, g, m, v)))
    return tuple(o.reshape(shape) for o in outs)


ROW = 1024


def _nrows(size):
    return -(-size // ROW)


def _pack(arrs, rows_total, dtype=F32):
    parts = []
    for a in arrs:
        flat = a.reshape(-1).astype(dtype)
        pad = _nrows(flat.shape[0]) * ROW - flat.shape[0]
        parts.append(flat if pad == 0 else jnp.concatenate([flat, jnp.zeros((pad,), dtype)]))
    flat = jnp.concatenate(parts)
    out = flat.reshape(-1, ROW)
    return _pad_rows(out, rows_total)


def _unpack(buf, shapes):
    lead = buf.shape[:-2]
    out, r0 = [], 0
    for shp in shapes:
        size = 1
        for s in shp:
            size *= s
        nr = _nrows(size)
        piece = lax.slice_in_dim(buf, r0, r0 + nr, axis=len(lead))
        out.append(piece.reshape(lead + (nr * ROW,))[..., :size].reshape(lead + tuple(shp)))
        r0 += nr
    return out


SLOT = 16


def _slot_rows(size):
    return _round_up(size // ROW, SLOT)


def _pack_rows(arrs, rows_total, dtype):
    parts, used = [], 0
    for a in arrs:
        part = a.astype(dtype).reshape(-1, ROW)
        extra = _slot_rows(a.size) - part.shape[0]
        parts.append(part if extra == 0 else jnp.pad(part, ((0, extra), (0, 0))))
        used += _slot_rows(a.size)
    if rows_total > used:
        parts.append(jnp.zeros((rows_total - used, ROW), dtype))
    return jnp.concatenate(parts, axis=0)


def _unpack_rows(buf, shapes):
    lead = buf.shape[:-2]
    out, r0 = [], 0
    for shp in shapes:
        size = 1
        for s in shp:
            size *= s
        piece = lax.slice_in_dim(buf, r0, r0 + size // ROW, axis=len(lead))
        out.append(piece.reshape(lead + tuple(shp)))
        r0 += _slot_rows(size)
    return out


def _round_up(n, k):
    return -(-n // k) * k


WEIGHTS = ['c_ctx', 'ada_w', 'ada_b', 'norm_mix_g', 'norm_mlp_g', 'mlp_w1', 'mlp_w2', 'hy_w_in', 'ssd_conv_w', 'ssd_conv_b',
           'ssd_dt_bias', 'ssd_a_log', 'ssd_d', 'ssd_norm_g', 'sg_ln_g', 'sg_ln_b', 'sg_w', 'sg_b', 'hy_w_out', 'cf_w_pw1',
           'cf_b_pw1', 'cf_w_dw', 'cf_b_dw', 'cf_ln_g', 'cf_ln_b', 'cf_w_pw2', 'cf_b_pw2', 'final_norm_g']
BIG = {'mlp_w1': 2, 'mlp_w2': 1, 'hy_w_in': 2, 'hy_w_out': 1, 'cf_w_pw1': 2, 'cf_w_pw2': 1}
SMALL_SHARD = ['ssd_conv_w', 'cf_b_pw1', 'cf_w_dw', 'cf_b_dw', 'cf_ln_g', 'cf_ln_b', 'cf_b_pw2']
REP = ['norm_mix_g', 'norm_mlp_g', 'ssd_conv_b', 'ssd_dt_bias', 'ssd_a_log', 'ssd_d', 'ssd_norm_g', 'sg_ln_g', 'sg_ln_b',
       'sg_w', 'sg_b', 'final_norm_g']


def _gather_shards(stacked, axis):
    return jnp.concatenate([stacked[j] for j in range(4)], axis=axis)


def _split_shards(full, axis):
    n = full.shape[axis] // 4
    return [lax.slice_in_dim(full, j * n, (j + 1) * n, axis=axis) for j in range(4)]


def kernel(x, c, ctx, c_ctx, ada_w, ada_b, norm_mix_g, norm_mlp_g, mlp_w1, mlp_w2, hy_w_in, ssd_conv_w, ssd_conv_b, ssd_dt_bias, ssd_a_log, ssd_d, ssd_norm_g, sg_ln_g, sg_ln_b, sg_w, sg_b, hy_w_out, cf_w_pw1, cf_b_pw1, cf_w_dw, cf_b_dw, cf_ln_g, cf_ln_b, cf_w_pw2, cf_b_pw2, final_norm_g, loss_target, m_c_ctx, m_ada_w, m_ada_b, m_norm_mix_g, m_norm_mlp_g, m_mlp_w1, m_mlp_w2, m_hy_w_in, m_ssd_conv_w, m_ssd_conv_b, m_ssd_dt_bias, m_ssd_a_log, m_ssd_d, m_ssd_norm_g, m_sg_ln_g, m_sg_ln_b, m_sg_w, m_sg_b, m_hy_w_out, m_cf_w_pw1, m_cf_b_pw1, m_cf_w_dw, m_cf_b_dw, m_cf_ln_g, m_cf_ln_b, m_cf_w_pw2, m_cf_b_pw2, m_final_norm_g, v_c_ctx, v_ada_w, v_ada_b, v_norm_mix_g, v_norm_mlp_g, v_mlp_w1, v_mlp_w2, v_hy_w_in, v_ssd_conv_w, v_ssd_conv_b, v_ssd_dt_bias, v_ssd_a_log, v_ssd_d, v_ssd_norm_g, v_sg_ln_g, v_sg_ln_b, v_sg_w, v_sg_b, v_hy_w_out, v_cf_w_pw1, v_cf_b_pw1, v_cf_w_dw, v_cf_b_dw, v_cf_ln_g, v_cf_ln_b, v_cf_w_pw2, v_cf_b_pw2, v_final_norm_g):
    args = locals()
    wl = {n: args[n] for n in WEIGHTS}
    ml = {n: args["m_" + n] for n in WEIGHTS}
    vl = {n: args["v_" + n] for n in WEIGHTS}
    mx, my, mc = _coords()
    me = 4 * mx + 2 * my + mc
    shard = 2 * mx + my
    even = (0, 2, 4, 6)

    def start_gather(names, name, tie=None):
        rows = sum(_slot_rows(wl[n].size) for n in names)
        buf = _pack_rows([wl[n] for n in names], rows, MXU)
        if tie is not None:
            buf, _ = lax.optimization_barrier((buf, tie))
        return _x4_start(buf, name, a2a=False)

    def finish_gather(handle, names, after, name):
        send, recv, own, land, _ = handle
        land = _x4_wait(send, recv, own, land, after, name, a2a=False)
        got = lax.dynamic_update_slice(land, own[None], (shard, 0, 0))
        shapes = [wl[n].shape for n in names]
        return {n: _gather_shards(st, BIG[n]) for n, st in zip(names, _unpack_rows(got, shapes))}

    rest_names = [n for n in BIG if n != "hy_w_in"]
    h_in = start_gather(["hy_w_in"], "agw_in_start")
    c = c + h_in[4][0, 0]

    small_shapes = [wl[n].shape for n in SMALL_SHARD]
    blk1 = _pack([c] + [wl[n] for n in SMALL_SHARD], 24)
    got1 = _ag8(blk1, "ag_cond")
    x16 = _pad_rows(jnp.concatenate([got1[:, 0, :], c_ctx[None]], axis=0), 16)
    small_full = {}
    for n, parts in zip(SMALL_SHARD, _unpack(got1[:, 1:, :], small_shapes)):
        small_full[n] = jnp.concatenate([parts[s] for s in even], axis=-1)

    nloc = ada_w.shape[-1]
    ada_b_loc = lax.dynamic_slice_in_dim(ada_b, shard * nloc, nloc, axis=1)
    s16, ada_loc = _ada_fwd(x16, ada_w, ada_b_loc)
    got2 = _ag8(ada_loc.reshape(32, nloc), "ag_ada").reshape(8, 2, 16, nloc)
    ada_full = jnp.concatenate([got2[s] for s in even], axis=-1)
    ada_me = lax.dynamic_slice_in_dim(ada_full, me, 1, axis=1)[:, 0, :]
    cada0 = ada_full[0, 8, :]

    w = {n: wl[n] for n in WEIGHTS if n not in BIG and n not in SMALL_SHARD}
    w.update(small_full)
    h_rest = start_gather(rest_names, "agw_rest_start", tie=ada_me)
    win = finish_gather(h_in, ["hy_w_in"], h_rest[4], "agw_in_wait")["hy_w_in"][0]
    w["wz"], w["wxbc"], w["wuv"] = win[:, :D], win[:, D:D + XBC], win[:, D + XBC + 32:]
    w["wdt"] = _pad_cols(win[:, D + XBC:D + XBC + 32], LANE)
    w["sg_w"] = sg_w[0].astype(MXU)

    def late_w(after):
        wfull = finish_gather(h_rest, rest_names, after, "agw_rest_wait")
        return {"hy_w_out": wfull["hy_w_out"][0], "mlp_w1": wfull["mlp_w1"], "mlp_w2": wfull["mlp_w2"],
                "cf_w_pw1": wfull["cf_w_pw1"][0], "cf_w_pw2": wfull["cf_w_pw2"][0]}

    full_shape = {n: wl[n].shape for n in WEIGHTS}
    for n in BIG:
        full_shape[n] = tuple(s * 4 if a == BIG[n] else s for a, s in enumerate(wl[n].shape))
    for n in SMALL_SHARD:
        full_shape[n] = wl[n].shape[:-1] + (wl[n].shape[-1] * 4,)

    def grad_pieces(g_, names):
        rows = _round_up(sum(_slot_rows(wl[n].size) for n in names), 512)
        return jnp.stack([_pack_rows([_split_shards(g_[n].reshape(full_shape[n]), BIG[n])[j] for n in names], rows, jnp.bfloat16)
                          for j in range(4)])

    early_names = ["mlp_w1", "mlp_w2", "cf_w_pw1", "cf_w_pw2"]
    last_names = ["hy_w_in", "hy_w_out"]
    early = {}

    def early_grads(g_):
        early["h"] = _x4_start(grad_pieces(g_, early_names), "a2a_early_start", a2a=True)
        return early["h"][4][0, 0]

    loss_part, dx, g, d_ada, d_cada0 = _local_step(x[0], ctx[0], loss_target[0], ada_me, cada0, w, late_w, early_grads)
    loss = lax.psum(loss_part, ("x", "y", "c"))
    g = {n: a.reshape(full_shape[n]) for n, a in g.items()}

    sm_names = REP + SMALL_SHARD
    srows = _round_up(sum(_nrows(g[n].size) for n in sm_names) + 18, 8)
    got3 = _ag8(_pack([g[n] for n in sm_names] + [d_ada, d_cada0], srows), "ag_small")
    tot3 = _sum_slots(got3, tuple(range(8)), "sum_small")
    sm_tot = _unpack(tot3, [full_shape[n] for n in sm_names] + [(2, 6 * D), (6 * D,)])
    grads = dict(zip(sm_names, sm_tot[:-2]))
    for n in SMALL_SHARD:
        k = wl[n].shape[-1]
        grads[n] = lax.dynamic_slice_in_dim(grads[n], shard * k, k, axis=grads[n].ndim - 1)
    dada_tot, dcada_tot = sm_tot[-2], sm_tot[-1]
    grads["ada_b"] = dada_tot.at[0].add(dcada_tot)
    r_ada = sum(_nrows(g[n].size) for n in sm_names)
    dada_all = got3[:, r_ada:r_ada + 12, :].reshape(8, 2, 6 * D)
    d16 = jnp.concatenate([jnp.transpose(dada_all, (1, 0, 2)),
                           jnp.stack([dcada_tot, jnp.zeros_like(dcada_tot)])[:, None, :],
                           jnp.zeros((2, 7, 6 * D), F32)], axis=1)
    d_loc = lax.dynamic_slice_in_dim(d16, shard * nloc, nloc, axis=2)
    grads["ada_w"], cpart = _ada_bwd(s16, d_loc, ada_w)
    got4 = _ag8(cpart, "ag_cctx")
    dscc = _sum_slots(got4, even, "sum_cctx")
    grads["c_ctx"] = _cctx_grad(dscc, _pad_rows(c_ctx[None], 8))[0]

    got_last = _xchg4(grad_pieces(g, last_names), "a2a_last", a2a=True)
    send, recv, own, land, _ = early["h"]
    land = _x4_wait(send, recv, own, land, got_last, "a2a_early_wait", a2a=True)
    got_early = lax.dynamic_update_slice(land, lax.dynamic_slice_in_dim(own, shard, 1, axis=0), (shard, 0, 0))
    for got, names, tag in ((got_early, early_names, "early"), (got_last, last_names, "last")):
        part = _sum_slots(got, (0, 1, 2, 3), "sum_grads_" + tag, tr=512)
        tot = _add(part, _xchg_sib(part, "swap_grads_" + tag), "add_grads_" + tag, 512)
        grads.update(zip(names, _unpack_rows(tot, [wl[n].shape for n in names])))

    delta, new_m, new_v = {}, {}, {}
    for n in list(BIG) + ["ada_w"]:
        delta[n], new_m[n], new_v[n] = _adamw(wl[n], grads[n], ml[n], vl[n], "adamw_" + n)
    for group, tag in ((["c_ctx", "ada_b"] + REP, "rep"), (SMALL_SHARD, "shard")):
        shapes = [wl[n].shape for n in group]
        rows = _round_up(sum(_nrows(wl[n].size) for n in group), 8)
        outs = _adamw(*(_pack([src[n] for n in group], rows) for src in (wl, grads, ml, vl)), "adamw_small_" + tag)
        for dst, buf in zip((delta, new_m, new_v), outs):
            dst.update(zip(group, _unpack(buf, shapes)))

    return (loss, dx[None], *[grads[n].reshape(wl[n].shape) for n in WEIGHTS], *[delta[n] for n in WEIGHTS],
            *[new_m[n] for n in WEIGHTS], *[new_v[n] for n in WEIGHTS])
```

```python
import functools

import jax
import jax.numpy as jnp
from jax import lax
from jax.experimental import pallas as pl
from jax.experimental.pallas import tpu as pltpu

F32 = jnp.float32
MXU = jnp.bfloat16
ACT = jnp.bfloat16
HI = lax.Precision.HIGHEST
EPS = 1e-6

D = 1024
HID = 4096
XBC = 1536
NPAIR = 8
Q = 128
SGG = 8
CK = 31
SK = 5
T = 256
LANE = 128
VMEM_LIMIT = 56 * 1024 * 1024

ADAM_LR, ADAM_B1, ADAM_B2, ADAM_EPS, ADAM_WD, ADAM_STEP = 0.001, 0.9, 0.999, 1e-08, 0.01, 10


def _call(body, name, grid, in_specs, out_specs, out_shape, scratch=()):
    return pl.pallas_call(
        body, name=name, grid=grid, in_specs=in_specs, out_specs=out_specs, out_shape=out_shape,
        scratch_shapes=list(scratch),
        compiler_params=pltpu.CompilerParams(dimension_semantics=("arbitrary",) * len(grid),
                                             vmem_limit_bytes=VMEM_LIMIT))


def _sds(shape, dt=F32):
    return jax.ShapeDtypeStruct(tuple(shape), dt)


def _rows(t, w, off=0, lane_blk=0):
    return pl.BlockSpec((t, w), lambda i: (i + off, lane_blk))


def _rows_lat(t, w, nct):
    return pl.BlockSpec((t, w), lambda i: (jnp.maximum(i - nct, 0), 0))


def _full(*shape):
    return pl.BlockSpec(shape, lambda *_: (0,) * len(shape))


def _wfull(*shape):
    return pl.BlockSpec(shape, lambda *_: (0,) * len(shape), pipeline_mode=pl.Buffered(1))


def _halo(w, hb, nrows):
    r, nb = T // hb, nrows // hb
    prev = pl.BlockSpec((hb, w), lambda i: (jnp.maximum(i * r - 1, 0), 0))
    nxt = pl.BlockSpec((hb, w), lambda i: (jnp.minimum((i + 1) * r, nb - 1), 0))
    return prev, nxt


def _nn(a, b):
    return jnp.dot(a.astype(MXU), b.astype(MXU), preferred_element_type=F32)


def _nt(a, b):
    return lax.dot_general(a.astype(MXU), b.astype(MXU), (((1,), (1,)), ((), ())), preferred_element_type=F32)


def _tn_dot(a, b):
    return lax.dot_general(a.astype(MXU), b.astype(MXU), (((0,), (0,)), ((), ())), preferred_element_type=F32)


def _sum0(x):
    return jnp.sum(x, axis=0, keepdims=True)


def _silu(x):
    return x * jax.nn.sigmoid(x)


def _gelu(x):
    return jax.nn.gelu(x, approximate=True)


def _mod(h, g, sh, sc):
    n = h * lax.rsqrt(jnp.mean(h * h, axis=-1, keepdims=True) + EPS)
    return n * g * (1.0 + sc) + sh


def _ln(x, g, b):
    xc = x - jnp.mean(x, axis=-1, keepdims=True)
    return xc * lax.rsqrt(jnp.mean(xc * xc, axis=-1, keepdims=True) + EPS) * g + b


def _gate_norm(ytot, z, ng):
    yg = ytot * _silu(z)
    halves = []
    for k in range(2):
        seg = yg[:, k * 512:(k + 1) * 512]
        halves.append(seg * lax.rsqrt(jnp.mean(seg * seg, axis=-1, keepdims=True) + EPS) * ng[:, k * 512:(k + 1) * 512])
    return jnp.concatenate(halves, axis=-1)


def _fill_ext(ext_ref, prev_ref, cur_ref, next_ref, hb, first, last):
    ext_ref[0:hb, :] = jnp.where(first, 0.0, prev_ref[...])
    ext_ref[hb:hb + T, :] = cur_ref[...]
    ext_ref[hb + T:hb + T + hb, :] = jnp.where(last, 0.0, next_ref[...])


def _conv(ext_ref, w_ref, k_taps, hb, lanes):
    off = hb - k_taps // 2
    acc = ext_ref[pl.ds(off, T), lanes] * w_ref[0:1, lanes]
    for k in range(1, k_taps):
        acc = acc + ext_ref[pl.ds(off + k, T), lanes] * w_ref[k:k + 1, lanes]
    return acc


def _conv_tr(ext_ref, w_ref, k_taps, hb, lanes):
    off = hb + k_taps // 2
    acc = ext_ref[pl.ds(off, T), lanes] * w_ref[0:1, lanes]
    for k in range(1, k_taps):
        acc = acc + ext_ref[pl.ds(off - k, T), lanes] * w_ref[k:k + 1, lanes]
    return acc


def _conv_wide(ext_ref, w_ref, k_taps, hb, lanes, flip=False):
    base = hb - k_taps // 2
    acc = None
    for b in range(8):
        taps = [k for k in range(k_taps) if (base + k) % 8 == b]
        if not taps:
            continue
        p = None
        for k in taps:
            wi = (k_taps - 1 - k) if flip else k
            term = ext_ref[pl.ds(base + k - b, T + 8), lanes] * w_ref[wi:wi + 1, lanes]
            p = term if p is None else p + term
        acc = p[b:b + T] if acc is None else acc + p[b:b + T]
    return acc


def _conv_dw_wide(dw_ref, d_ref, xext_ref, k_taps, hb, lanes):
    base = hb - k_taps // 2
    d = d_ref[:, lanes]
    for b in range(8):
        taps = [k for k in range(k_taps) if (base + k) % 8 == b]
        if not taps:
            continue
        lo_off = base + taps[0] - b
        span = base + taps[-1] - b - lo_off
        xs = xext_ref[pl.ds(lo_off + b, T + span), lanes]
        for k in taps:
            a = base + k - b - lo_off
            dw_ref[k:k + 1, lanes] += _sum0(d * xs[a:a + T])


def _conv_dw(dw_ref, d_ref, xext_ref, k_taps, hb, lanes):
    off = hb - k_taps // 2
    d = d_ref[:, lanes]
    for k in range(k_taps):
        dw_ref[k:k + 1, lanes] += _sum0(d * xext_ref[pl.ds(off + k, T), lanes])


def _tn(a, b, name, relu2=False, shard=None):
    m_rows, ka = a.shape
    n = b.shape[1]
    tm = next(t for t in (1024, 768, 512, 256) if m_rows % t == 0)
    tk = min(ka, 1024)
    tn = n if n <= 1024 else next(t for t in (1024, 768, 512, 384, 256, 128) if n % t == 0)
    if shard is not None and shard[0] == "col":
        tn = shard[1]
    if shard is not None and shard[0] == "row":
        tk = shard[1]
    n_m = m_rows // tm

    def body(a_ref, b_ref, o_ref, *acc):
        acc_ref = acc[0] if acc else o_ref

        @pl.when(pl.program_id(2) == 0)
        def _():
            acc_ref[...] = jnp.zeros_like(acc_ref)
        av = a_ref[...]
        if relu2:
            av = jnp.square(jnp.maximum(av.astype(F32), 0.0))
        acc_ref[...] += _tn_dot(av, b_ref[...])
        if acc:
            @pl.when(pl.program_id(2) == n_m - 1)
            def _():
                o_ref[...] = acc_ref[...].astype(o_ref.dtype)

    in_specs = [pl.BlockSpec((tm, tk), lambda k, j, m: (m, k)), pl.BlockSpec((tm, tn), lambda k, j, m: (m, j))]
    if shard is None:
        return _call(body, name, (ka // tk, n // tn, n_m), in_specs,
                     pl.BlockSpec((tk, tn), lambda k, j, m: (k, j)), _sds((ka, n)))(a, b)
    if shard[0] == "col":
        out_spec, out_shape = pl.BlockSpec((None, tk, tn), lambda k, j, m: (j, k, 0)), _sds((n // tn, ka, tn), jnp.bfloat16)
    else:
        out_spec, out_shape = pl.BlockSpec((None, tk, tn), lambda k, j, m: (k, 0, j)), _sds((ka // tk, tk, n), jnp.bfloat16)
    return _call(body, name, (ka // tk, n // tn, n_m), in_specs, out_spec, out_shape,
                 scratch=[pltpu.VMEM((tk, tn), F32)])(a, b)


def _mlp_fwd(h, mp, wpack, layer, name):
    n_rows = h.shape[0]

    def body(h_ref, mp_ref, w1_ref, w2_ref, hn_ref, a_ref, y_ref):
        hv = h_ref[...]
        u = _mod(hv, mp_ref[0:1], mp_ref[1:2], mp_ref[2:3]).astype(MXU)
        acc = jnp.zeros((T, D), F32)
        for j in range(HID // 1024):
            cs = slice(j * 1024, (j + 1) * 1024)
            a = jnp.dot(u, w1_ref[j], preferred_element_type=F32)
            a_ref[:, cs] = a.astype(ACT)
            acc = acc + jnp.dot(jnp.square(jnp.maximum(a, 0.0)).astype(MXU), w2_ref[j], preferred_element_type=F32)
        y_ref[...] = acc
        hn_ref[...] = hv + mp_ref[3:4] * acc

    return _call(body, name, (n_rows // T,),
                 [_rows(T, D), _full(8, D), _mlp_wspec(layer), _mlp_wspec(2 + layer)],
                 [_rows(T, D), _rows(T, HID), _rows(T, D)],
                 [_sds((n_rows, D)), _sds((n_rows, HID), ACT), _sds((n_rows, D))])(h, mp, wpack, wpack)


def _mlp_wspec(row_block):
    return pl.BlockSpec((4, 1024, 1024), lambda i: (0, row_block, 0), pipeline_mode=pl.Buffered(1))


def _mlp_bwd(dh, h, a, y, mp, wpack, layer, name):
    n_rows = h.shape[0]

    def body(dh_ref, h_ref, a_ref, y_ref, mp_ref, w1_ref, w2_ref, dho_ref, da_ref, dyb_ref, ub_ref, pg_ref):
        dhp = dh_ref[...]
        u, vjp = jax.vjp(_mod, h_ref[...], mp_ref[0:1], mp_ref[1:2], mp_ref[2:3])
        ub_ref[...] = u.astype(ACT)
        dyb = (mp_ref[3:4] * dhp).astype(MXU)
        dyb_ref[...] = dyb.astype(ACT)
        du = jnp.zeros((T, D), F32)
        for j in range(HID // 1024):
            cs = slice(j * 1024, (j + 1) * 1024)
            dp = _nt(dyb, w2_ref[j])
            da = dp * 2.0 * jnp.maximum(a_ref[:, cs].astype(F32), 0.0)
            da_ref[:, cs] = da.astype(ACT)
            du = du + _nt(da, w1_ref[j])
        dhn, dg, dsh, dsc = vjp(du)
        dho_ref[...] = dhp + dhn

        @pl.when(pl.program_id(0) == 0)
        def _():
            pg_ref[...] = jnp.zeros_like(pg_ref)
        pg_ref[0:1] += dg
        pg_ref[1:2] += dsh
        pg_ref[2:3] += dsc
        pg_ref[3:4] += _sum0(dhp * y_ref[...])

    return _call(body, name, (n_rows // T,),
                 [_rows(T, D), _rows(T, D), _rows(T, HID), _rows(T, D), _full(8, D), _mlp_wspec(layer), _mlp_wspec(2 + layer)],
                 [_rows(T, D), _rows(T, HID), _rows(T, D), _rows(T, D), _full(8, D)],
                 [_sds((n_rows, D)), _sds((n_rows, HID), ACT), _sds((n_rows, D), ACT), _sds((n_rows, D), ACT),
                  _sds((8, D))])(dh, h, a, y, mp, wpack, wpack)


def _cf1_fwd(h, mp, w1, b1):
    n_rows = h.shape[0]

    def body(h_ref, mp_ref, w1_ref, b1_ref, glu_ref, a_ref):
        u = _mod(h_ref[...], mp_ref[0:1], mp_ref[1:2], mp_ref[2:3]).astype(MXU)
        a = jnp.dot(u, w1_ref[...], preferred_element_type=F32) + b1_ref[...]
        a_ref[...] = a.astype(ACT)
        glu_ref[...] = a[:, :D] * jax.nn.sigmoid(a[:, D:])

    return _call(body, "cf1_fwd", (n_rows // T,),
                 [_rows(T, D), _full(8, D), _wfull(D, 2 * D), _full(1, 2 * D)],
                 [_rows(T, D), _rows(T, 2 * D)],
                 [_sds((n_rows, D)), _sds((n_rows, 2 * D), ACT)])(h, mp, w1, b1)


def _cf2_fwd(h, glu, mp, wdw, bdw, lng, lnb, w2, b2):
    n_rows = h.shape[0]
    nt = n_rows // T
    hb = 16

    def body(h_ref, gp_ref, gc_ref, gn_ref, mp_ref, wdw_ref, bdw_ref, lng_ref, lnb_ref, w2_ref, b2_ref,
             hn_ref, cv_ref, sb_ref, y_ref, ext):
        i = pl.program_id(0)
        _fill_ext(ext, gp_ref, gc_ref, gn_ref, hb, i == 0, i == nt - 1)
        for c in range(D // LANE):
            lanes = slice(c * LANE, (c + 1) * LANE)
            cv_ref[:, lanes] = _conv_wide(ext, wdw_ref, CK, hb, lanes) + bdw_ref[:, lanes]
        s = _silu(_ln(cv_ref[...], lng_ref[...], lnb_ref[...])).astype(MXU)
        sb_ref[...] = s.astype(ACT)
        y = jnp.dot(s, w2_ref[...], preferred_element_type=F32) + b2_ref[...]
        y_ref[...] = y
        hn_ref[...] = h_ref[...] + mp_ref[3:4] * y

    gp, gn = _halo(D, hb, n_rows)
    return _call(body, "cf2_fwd", (nt,),
                 [_rows(T, D), gp, _rows(T, D), gn, _full(8, D), _full(32, D), _full(1, D), _full(1, D), _full(1, D),
                  _wfull(D, D), _full(1, D)],
                 [_rows(T, D), _rows(T, D), _rows(T, D), _rows(T, D)],
                 [_sds((n_rows, D)), _sds((n_rows, D)), _sds((n_rows, D), ACT), _sds((n_rows, D))],
                 scratch=[pltpu.VMEM((T + 2 * hb, D), F32)])(h, glu, glu, glu, mp, wdw, bdw, lng, lnb, w2, b2)


def _cf2_bwd(dh, y, cv, mp, lng, lnb, w2):
    n_rows = dh.shape[0]

    def body(dh_ref, y_ref, cv_ref, mp_ref, lng_ref, lnb_ref, w2_ref, dcv_ref, dyb_ref, pg_ref):
        dhp = dh_ref[...]
        dy = mp_ref[3:4] * dhp
        dyb = dy.astype(MXU)
        dyb_ref[...] = dyb.astype(ACT)
        ds = _nt(dyb, w2_ref[...])
        _, vjp = jax.vjp(lambda cv_, g_, b_: _silu(_ln(cv_, g_, b_)), cv_ref[...], lng_ref[...], lnb_ref[...])
        dcv, dlng, dlnb = vjp(ds)
        dcv_ref[...] = dcv

        @pl.when(pl.program_id(0) == 0)
        def _():
            pg_ref[...] = jnp.zeros_like(pg_ref)
        pg_ref[0:1] += _sum0(dhp * y_ref[...])
        pg_ref[1:2] += _sum0(dy)
        pg_ref[2:3] += dlng
        pg_ref[3:4] += dlnb
        pg_ref[4:5] += _sum0(dcv)

    return _call(body, "cf2_bwd", (n_rows // T,),
                 [_rows(T, D), _rows(T, D), _rows(T, D), _full(8, D), _full(1, D), _full(1, D), _wfull(D, D)],
                 [_rows(T, D), _rows(T, D), _full(8, D)],
                 [_sds((n_rows, D)), _sds((n_rows, D), ACT), _sds((8, D))])(dh, y, cv, mp, lng, lnb, w2)


def _cf1_bwd(dh, h, a, dcv, glu, mp, wdw, w1):
    n_rows = h.shape[0]
    nt = n_rows // T
    hb = 16

    def body(dh_ref, h_ref, a_ref, dp_ref, dc_ref, dn_ref, gp_ref, gc_ref, gn_ref, mp_ref, wdw_ref, w1_ref,
             dho_ref, da_ref, ub_ref, pg_ref, pb_ref, dw_ref, dext, gext, dglu):
        i = pl.program_id(0)

        @pl.when(i == 0)
        def _():
            pg_ref[...] = jnp.zeros_like(pg_ref)
            pb_ref[...] = jnp.zeros_like(pb_ref)
            dw_ref[...] = jnp.zeros_like(dw_ref)
        _fill_ext(dext, dp_ref, dc_ref, dn_ref, hb, i == 0, i == nt - 1)
        _fill_ext(gext, gp_ref, gc_ref, gn_ref, hb, i == 0, i == nt - 1)
        for c in range(D // LANE):
            lanes = slice(c * LANE, (c + 1) * LANE)
            dglu[:, lanes] = _conv_wide(dext, wdw_ref, CK, hb, lanes, flip=True)
            _conv_dw_wide(dw_ref, dc_ref, gext, CK, hb, lanes)
        av = a_ref[...].astype(F32)
        _, vjp_glu = jax.vjp(lambda a1, a2: a1 * jax.nn.sigmoid(a2), av[:, :D], av[:, D:])
        da1, da2 = vjp_glu(dglu[...])
        da_ref[:, :D] = da1.astype(ACT)
        da_ref[:, D:] = da2.astype(ACT)
        pb_ref[0:1, :D] += _sum0(da1)
        pb_ref[0:1, D:] += _sum0(da2)
        du = _nt(da1, w1_ref[:, :D]) + _nt(da2, w1_ref[:, D:])
        u, vjp = jax.vjp(_mod, h_ref[...], mp_ref[0:1], mp_ref[1:2], mp_ref[2:3])
        ub_ref[...] = u.astype(ACT)
        dhn, dg, dsh, dsc = vjp(du)
        dho_ref[...] = dh_ref[...] + dhn
        pg_ref[0:1] += dg
        pg_ref[1:2] += dsh
        pg_ref[2:3] += dsc

    hp, hn = _halo(D, hb, n_rows)
    return _call(body, "cf1_bwd", (nt,),
                 [_rows(T, D), _rows(T, D), _rows(T, 2 * D), hp, _rows(T, D), hn, hp, _rows(T, D), hn,
                  _full(8, D), _full(32, D), _wfull(D, 2 * D)],
                 [_rows(T, D), _rows(T, 2 * D), _rows(T, D), _full(8, D), _full(8, 2 * D), _full(32, D)],
                 [_sds((n_rows, D)), _sds((n_rows, 2 * D), ACT), _sds((n_rows, D), ACT), _sds((8, D)),
                  _sds((8, 2 * D)), _sds((32, D))],
                 scratch=[pltpu.VMEM((T + 2 * hb, D), F32), pltpu.VMEM((T + 2 * hb, D), F32), pltpu.VMEM((T, D), F32)],
                 )(dh, h, a, dcv, dcv, dcv, glu, glu, glu, mp, wdw, w1)


def _sg_blocks():
    return [(c, g, slice(c * Q, (c + 1) * Q), slice(g * LANE, (g + 1) * LANE)) for c in range(T // Q) for g in range(SGG)]


def _ctx_spec(nct):
    return pl.BlockSpec((T, D), lambda i: (jnp.minimum(i, nct - 1), 0))


def _hy1_fwd(ctx, x, mp2, wz, wuv, wxbc, wdt, lng, lnb, sgw, sgbt, nct):
    n_lat = x.shape[0]
    n_rows = ctx.shape[0] + n_lat

    def body(c_ref, x_ref, mp_ref, wz_ref, wuv_ref, wxbc_ref, wdt_ref, lng_ref, lnb_ref, sgw_ref, sgbt_ref,
             z_ref, uv_ref, xbcp_ref, dtr_ref, ysg_ref):
        hv = jnp.where(pl.program_id(0) < nct, c_ref[...], x_ref[...])
        u = _mod(hv, mp_ref[0:1], mp_ref[1:2], mp_ref[2:3]).astype(MXU)
        z_ref[...] = jnp.dot(u, wz_ref[...], preferred_element_type=F32)
        xbcp_ref[...] = jnp.dot(u, wxbc_ref[...], preferred_element_type=F32)
        dtr_ref[...] = jnp.dot(u, wdt_ref[...], preferred_element_type=F32)
        uv = jnp.dot(u, wuv_ref[...], preferred_element_type=F32)
        uv_ref[...] = uv
        gate = _gelu(uv[:, :D])
        vln = _ln(_gelu(uv[:, D:]), lng_ref[...], lnb_ref[...]).astype(MXU)
        for _, g, rs, ls in _sg_blocks():
            s = jnp.dot(sgw_ref[g], vln[rs, ls], preferred_element_type=F32) + sgbt_ref[:, g:g + 1]
            ysg_ref[rs, ls] = (gate[rs, ls] * s).astype(ACT)

    mspec = pl.BlockSpec((None, 8, D), lambda i: (jnp.where(i < nct, 0, 1), 0, 0))
    return _call(body, "hy1_fwd", (n_rows // T,),
                 [_ctx_spec(nct), _rows_lat(T, D, nct), mspec, _wfull(D, D), _wfull(D, 2 * D), _wfull(D, XBC), _wfull(D, LANE),
                  _full(1, D), _full(1, D), _full(SGG, Q, Q), _full(Q, LANE)],
                 [_rows(T, D), _rows(T, 2 * D), _rows(T, XBC), _rows(T, LANE), _rows_lat(T, D, nct)],
                 [_sds((n_rows, D)), _sds((n_rows, 2 * D)), _sds((n_rows, XBC)), _sds((n_rows, LANE)),
                  _sds((n_lat, D), ACT)])(ctx, x, mp2, wz, wuv, wxbc, wdt, lng, lnb, sgw, sgbt)


def _hy1_bwd(ctx, x, uv, dz, dxbcp, ddf, ddb, dysg, dres, mp2, wz, wuv, wxbc, wdt, lng, lnb, sgw, sgbt, nct):
    n_lat = dres.shape[0]
    n_rows = ctx.shape[0] + n_lat

    def body(c_ref, x_ref, uv_ref, dz_ref, dxbcp_ref, ddf_ref, ddb_ref, dysg_ref, dres_ref, mp_ref, wz_ref, wuv_ref,
             wxbc_ref, wdt_ref, lng_ref, lnb_ref, sgw_ref, sgbt_ref,
             dho_ref, ub_ref, duv_ref, ddt_ref, pg2_ref, pl_ref, dsgw_ref, dsgb_ref, dgate_s, dvln_s):
        i = pl.program_id(0)

        @pl.when(i == 0)
        def _():
            pg2_ref[...] = jnp.zeros_like(pg2_ref)
            pl_ref[...] = jnp.zeros_like(pl_ref)
            dsgw_ref[...] = jnp.zeros_like(dsgw_ref)
            dsgb_ref[...] = jnp.zeros_like(dsgb_ref)
        uv = uv_ref[...]

        def f_sg(ug, uvv, g_, b_):
            return _gelu(ug), _ln(_gelu(uvv), g_, b_)
        (gate, vln), vjp_sg = jax.vjp(f_sg, uv[:, :D], uv[:, D:], lng_ref[...], lnb_ref[...])
        vlnb = vln.astype(MXU)
        lane = lax.broadcasted_iota(jnp.int32, (Q, LANE), 1)
        dsgb = jnp.zeros((Q, LANE), F32)
        for _, g, rs, ls in _sg_blocks():
            s = jnp.dot(sgw_ref[g], vlnb[rs, ls], preferred_element_type=F32) + sgbt_ref[:, g:g + 1]
            dyb = dysg_ref[rs, ls]
            dgate_s[rs, ls] = dyb * s
            ds = dyb * gate[rs, ls]
            dvln_s[rs, ls] = _tn_dot(sgw_ref[g], ds)
            dsgw_ref[g] += _nt(ds, vlnb[rs, ls])
            dsgb = dsgb + jnp.where(lane == g, jnp.sum(ds, axis=1, keepdims=True), 0.0)
        dsgb_ref[...] += dsgb
        dug, duvv, dlng, dlnb = vjp_sg((dgate_s[...], dvln_s[...]))
        pl_ref[0:1] += dlng
        pl_ref[1:2] += dlnb
        duv_ref[:, :D] = dug.astype(ACT)
        duv_ref[:, D:] = duvv.astype(ACT)
        ddt = (ddf_ref[...] + ddb_ref[...]).astype(MXU)
        ddt_ref[...] = ddt.astype(ACT)
        du = (_nt(dz_ref[...], wz_ref[...]) + _nt(dug, wuv_ref[:, :D]) + _nt(duvv, wuv_ref[:, D:])
              + _nt(dxbcp_ref[...], wxbc_ref[...]) + _nt(ddt, wdt_ref[...]))
        hv = jnp.where(i < nct, c_ref[...], x_ref[...])
        u, vjp = jax.vjp(_mod, hv, mp_ref[0:1], mp_ref[1:2], mp_ref[2:3])
        ub_ref[...] = u.astype(ACT)
        dhn, dg, dsh, dsc = vjp(du)
        dho_ref[...] = dres_ref[...] + dhn
        is_ctx = i < nct
        for k, val in enumerate((dg, dsh, dsc)):
            pg2_ref[0, k:k + 1] += jnp.where(is_ctx, val, 0.0)
            pg2_ref[1, k:k + 1] += jnp.where(is_ctx, 0.0, val)

    mspec = pl.BlockSpec((None, 8, D), lambda i: (jnp.where(i < nct, 0, 1), 0, 0))
    return _call(body, "hy1_bwd", (n_rows // T,),
                 [_ctx_spec(nct), _rows_lat(T, D, nct), _rows(T, 2 * D), _rows(T, D), _rows(T, XBC), _rows(T, LANE),
                  _rows(T, LANE), _rows(T, D),
                  _rows_lat(T, D, nct), mspec, _wfull(D, D), _wfull(D, 2 * D), _wfull(D, XBC), _wfull(D, LANE),
                  _full(1, D), _full(1, D), _full(SGG, Q, Q), _full(Q, LANE)],
                 [_rows_lat(T, D, nct), _rows(T, D), _rows(T, 2 * D), _rows(T, LANE), _full(2, 8, D), _full(8, D),
                  _full(SGG, Q, Q), _full(Q, LANE)],
                 [_sds((n_lat, D)), _sds((n_rows, D), ACT), _sds((n_rows, 2 * D), ACT), _sds((n_rows, LANE), ACT),
                  _sds((2, 8, D)), _sds((8, D)), _sds((SGG, Q, Q)), _sds((Q, LANE))],
                 scratch=[pltpu.VMEM((T, D), F32), pltpu.VMEM((T, D), F32)],
                 )(ctx, x, uv, dz, dxbcp, ddf, ddb, dysg, dres, mp2, wz, wuv, wxbc, wdt, lng, lnb, sgw, sgbt)


def _seq_edges(i, nct, nt):
    return (i == 0) | (i == nct), (i == nct - 1) | (i == nt - 1)


def _cv5_fwd(xbcp, w, b, nct):
    n_rows = xbcp.shape[0]
    nt = n_rows // T
    hb = 8

    def body(p_ref, c_ref, n_ref, w_ref, b_ref, o_ref, ext):
        first, last = _seq_edges(pl.program_id(0), nct, nt)
        _fill_ext(ext, p_ref, c_ref, n_ref, hb, first, last)
        for c in range(XBC // LANE):
            lanes = slice(c * LANE, (c + 1) * LANE)
            o_ref[:, lanes] = _silu(_conv(ext, w_ref, SK, hb, lanes) + b_ref[:, lanes])

    hp, hn = _halo(XBC, hb, n_rows)
    return _call(body, "cv5_fwd", (nt,), [hp, _rows(T, XBC), hn, _full(8, XBC), _full(1, XBC)],
                 _rows(T, XBC), _sds((n_rows, XBC)), scratch=[pltpu.VMEM((T + 2 * hb, XBC), F32)])(xbcp, xbcp, xbcp, w, b)


def _cv5_bwd1(xbcp, dxf, dxb, w, b, nct):
    n_rows = xbcp.shape[0]
    nt = n_rows // T
    hb = 8

    def body(p_ref, c_ref, n_ref, dxf_ref, dxb_ref, w_ref, b_ref, o_ref, pg_ref, ext):
        i = pl.program_id(0)
        first, last = _seq_edges(i, nct, nt)
        _fill_ext(ext, p_ref, c_ref, n_ref, hb, first, last)

        @pl.when(i == 0)
        def _():
            pg_ref[...] = jnp.zeros_like(pg_ref)
        for c in range(XBC // LANE):
            lanes = slice(c * LANE, (c + 1) * LANE)
            cv = _conv(ext, w_ref, SK, hb, lanes) + b_ref[:, lanes]
            sg = jax.nn.sigmoid(cv)
            dcv = (dxf_ref[:, lanes] + dxb_ref[:, lanes]) * (sg * (1.0 + cv * (1.0 - sg)))
            o_ref[:, lanes] = dcv
            pg_ref[0:1, lanes] += _sum0(dcv)

    hp, hn = _halo(XBC, hb, n_rows)
    return _call(body, "cv5_bwd1", (nt,),
                 [hp, _rows(T, XBC), hn, _rows(T, XBC), _rows(T, XBC), _full(8, XBC), _full(1, XBC)],
                 [_rows(T, XBC), _full(8, XBC)], [_sds((n_rows, XBC)), _sds((8, XBC))],
                 scratch=[pltpu.VMEM((T + 2 * hb, XBC), F32)])(xbcp, xbcp, xbcp, dxf, dxb, w, b)


def _cv5_bwd2(dcv, xbcp, w, nct):
    n_rows = xbcp.shape[0]
    nt = n_rows // T
    hb = 8

    def body(dp_ref, dc_ref, dn_ref, xp_ref, xc_ref, xn_ref, w_ref, o_ref, dw_ref, dext, xext):
        i = pl.program_id(0)
        first, last = _seq_edges(i, nct, nt)
        _fill_ext(dext, dp_ref, dc_ref, dn_ref, hb, first, last)
        _fill_ext(xext, xp_ref, xc_ref, xn_ref, hb, first, last)

        @pl.when(i == 0)
        def _():
            dw_ref[...] = jnp.zeros_like(dw_ref)
        for c in range(XBC // LANE):
            lanes = slice(c * LANE, (c + 1) * LANE)
            o_ref[:, lanes] = _conv_tr(dext, w_ref, SK, hb, lanes).astype(ACT)
            _conv_dw(dw_ref, dc_ref, xext, SK, hb, lanes)

    hp, hn = _halo(XBC, hb, n_rows)
    return _call(body, "cv5_bwd2", (nt,),
                 [hp, _rows(T, XBC), hn, hp, _rows(T, XBC), hn, _full(8, XBC)],
                 [_rows(T, XBC), _full(8, XBC)], [_sds((n_rows, XBC), ACT), _sds((8, XBC))],
                 scratch=[pltpu.VMEM((T + 2 * hb, XBC), F32), pltpu.VMEM((T + 2 * hb, XBC), F32)],
                 )(dcv, dcv, dcv, xbcp, xbcp, xbcp, w)


def _scan_order(nc, ncc, rev):
    if not rev:
        return lambda s: s
    return lambda s: jnp.where(s < ncc, ncc - 1 - s, nc - 1 - (s - ncc))


def _ssd_prep(dtr, sp, rev):
    dt = jax.nn.softplus(dtr + sp[0:1])
    a_neg = -jnp.exp(sp[1:2])
    r = lax.broadcasted_iota(jnp.int32, (Q, Q), 0)
    c = lax.broadcasted_iota(jnp.int32, (Q, Q), 1)
    msk = (c >= r) if rev else (c <= r)
    tri = msk.astype(F32)
    acs = jnp.dot(tri, dt * a_neg, precision=HI, preferred_element_type=F32)
    last = 0 if rev else Q - 1
    return dt, a_neg, acs, msk, tri, last


def _pair_sel(arr, lo, m, lane_lt):
    h0 = lo + 2 * m
    return jnp.where(lane_lt, arr[:, h0:h0 + 1], arr[:, h0 + 1:h0 + 2])


def _head_lanes(row, lo, g):
    lane = lax.broadcasted_iota(jnp.int32, (1, 512), 1)
    out = jnp.zeros((1, 512), F32)
    for k in range(8):
        h = lo + 8 * g + k
        out = jnp.where((lane >= 64 * k) & (lane < 64 * (k + 1)), row[:, h:h + 1], out)
    return out


def _halves(v, lane_lt):
    return jnp.concatenate([jnp.where(lane_lt, v, 0.0), jnp.where(lane_lt, 0.0, v)], axis=0)


def _ssd_fwd(xbc, dtr, sp, ncc, rev):
    n_rows = xbc.shape[0]
    nc = n_rows // Q
    lo = 16 if rev else 0
    order = _scan_order(nc, ncc, rev)

    def body(x_ref, dtr_ref, sp_ref, y_ref, hin_ref, st):
        @pl.when(pl.program_id(0) == 0)
        def _():
            st[...] = jnp.zeros_like(st)
        dt, _, acs, msk, _, last = _ssd_prep(dtr_ref[...], sp_ref[...], rev)
        acs_t, dt_t = acs.T, dt.T
        eacs = jnp.exp(acs)
        eal = jnp.exp(acs[last:last + 1, :])
        tew = jnp.exp(acs[last:last + 1, :] - acs) * dt
        lane_lt = lax.broadcasted_iota(jnp.int32, (Q, LANE), 1) < 64
        for g in range(2):
            gl = slice(g * 512, (g + 1) * 512)
            bg = x_ref[:, 1024 + g * 128:1152 + g * 128]
            cg = x_ref[:, 1280 + g * 128:1408 + g * 128]
            s_g = _nt(cg, bg)
            h_t = st[:, gl]
            hin_ref[:, gl] = h_t
            yoff = _nn(cg, h_t)
            xw = []
            for mm in range(4):
                m = 4 * g + mm
                ls = slice(m * LANE, (m + 1) * LANE)
                x2 = x_ref[:, ls]
                ws = []
                for hh in range(2):
                    h = lo + 2 * m + hh
                    lm = jnp.exp(jnp.where(msk, acs[:, h:h + 1] - acs_t[h:h + 1, :], -jnp.inf))
                    ws.append(s_g * lm * dt_t[h:h + 1, :])
                y2 = _nn(jnp.concatenate(ws, axis=1), _halves(x2, lane_lt))
                y_ref[:, ls] = y2 + yoff[:, mm * LANE:(mm + 1) * LANE] * _pair_sel(eacs, lo, m, lane_lt)
                xw.append(x2 * _pair_sel(tew, lo, m, lane_lt))
            st[:, gl] = _head_lanes(eal, lo, g) * h_t + _tn_dot(bg, jnp.concatenate(xw, axis=1))

    return _call(body, "ssd_fwd_r" if rev else "ssd_fwd_f", (nc,),
                 [pl.BlockSpec((Q, XBC), lambda s: (order(s), 0)), pl.BlockSpec((Q, LANE), lambda s: (order(s), 0)),
                  _full(8, LANE)],
                 [pl.BlockSpec((Q, D), lambda s: (order(s), 0)), pl.BlockSpec((None, LANE, D), lambda s: (order(s), 0, 0))],
                 [_sds((n_rows, D)), _sds((nc, LANE, D))], scratch=[pltpu.VMEM((LANE, D), F32)])(xbc, dtr, sp)


def _ssd_bwd(xbc, dtr, dy, hin, sp, dl, eh, ncc, rev):
    n_rows = xbc.shape[0]
    nc = n_rows // Q
    lo = 16 if rev else 0
    fwd_order = _scan_order(nc, ncc, rev)
    order = lambda s: fwd_order(nc - 1 - s)
    with_skip = not rev

    def body(x_ref, dtr_ref, dy_ref, hin_ref, sp_ref, dl_ref, eh_ref, dx_ref, ddtr_ref, pg_ref, dst):
        @pl.when(pl.program_id(0) == 0)
        def _():
            dst[...] = jnp.zeros_like(dst)
            pg_ref[...] = jnp.zeros_like(pg_ref)
        dtr_v = dtr_ref[...]
        dt, a_neg, acs, msk, tri, last = _ssd_prep(dtr_v, sp_ref[...], rev)
        acs_t = acs.T
        r = lax.broadcasted_iota(jnp.int32, (Q, Q), 0)
        c = lax.broadcasted_iota(jnp.int32, (Q, Q), 1)
        msk_t = (c <= r) if rev else (c >= r)
        eacs = jnp.exp(acs)
        eal = jnp.exp(acs[last:last + 1, :])
        te = jnp.exp(acs[last:last + 1, :] - acs)
        lane = lax.broadcasted_iota(jnp.int32, (Q, LANE), 1)
        lane1 = lax.broadcasted_iota(jnp.int32, (1, LANE), 1)
        lane_lt = lane < 64
        dacs = jnp.zeros((Q, LANE), F32)
        ddt_x = jnp.zeros((Q, LANE), F32)
        dlast = jnp.zeros((1, LANE), F32)
        hs_rows = []
        sub16 = lax.broadcasted_iota(jnp.int32, (16, Q), 0)
        dacs_t = jnp.zeros((16, Q), F32)
        for g in range(2):
            gl = slice(g * 512, (g + 1) * 512)
            bg = x_ref[:, 1024 + g * 128:1152 + g * 128]
            cg = x_ref[:, 1280 + g * 128:1408 + g * 128]
            s_g = _nt(cg, bg)
            s_gt = _nt(bg, cg)
            h_t, dh_t = hin_ref[:, gl], dst[:, gl]
            bh = _nn(bg, dh_t)
            yoff = _nn(cg, h_t)
            d_s = jnp.zeros((Q, Q), F32)
            edy, exd = [], []
            for mm in range(4):
                m = 4 * g + mm
                ls = slice(m * LANE, (m + 1) * LANE)
                x2, dy2 = x_ref[:, ls], dy_ref[:, ls]
                bh2 = bh[:, mm * LANE:(mm + 1) * LANE]
                dtm, em, eam = (_pair_sel(v, lo, m, lane_lt) for v in (dt, te, eacs))
                xd2 = x2 * dtm
                lms, mts = [], []
                for hh in range(2):
                    h = lo + 2 * m + hh
                    col, row = acs[:, h:h + 1], acs_t[h:h + 1, :]
                    lms.append(jnp.exp(jnp.where(msk, col - row, -jnp.inf)))
                    mts.append(s_gt * jnp.exp(jnp.where(msk_t, row - col, -jnp.inf)))
                dy_st = _halves(dy2, lane_lt)
                dxd2 = em * bh2 + _nn(jnp.concatenate(mts, axis=1), dy_st)
                dm_st = _nt(dy_st, xd2)
                dmt_st = _nt(_halves(xd2, lane_lt), dy2)
                d_s = d_s + dm_st[:Q] * lms[0] + dm_st[Q:] * lms[1]
                v1, v2, v3 = dy2 * yoff[:, mm * LANE:(mm + 1) * LANE] * eam, dxd2 * x2, xd2 * bh2 * em
                for hh in range(2):
                    h = lo + 2 * m + hh
                    half = lane_lt == (hh == 0)
                    g_rows = _sum0(dmt_st[hh * Q:(hh + 1) * Q] * mts[hh]) - _sum0(dm_st[hh * Q:(hh + 1) * Q] * s_g * lms[hh])
                    dacs_t = jnp.where(sub16 == 2 * m + hh, g_rows, dacs_t)
                    r1 = jnp.sum(jnp.where(half, v1, 0.0), axis=1, keepdims=True)
                    r2 = jnp.sum(jnp.where(half, v2, 0.0), axis=1, keepdims=True)
                    r3 = jnp.sum(jnp.where(half, v3, 0.0), axis=1, keepdims=True)
                    dacs = dacs + jnp.where(lane == h, r1 - r3, 0.0)
                    ddt_x = ddt_x + jnp.where(lane == h, r2, 0.0)
                    dlast = dlast + jnp.where(lane1 == h, _sum0(r3), 0.0)
                dx2 = dxd2 * dtm
                if with_skip:
                    dx2 = dx2 + dl_ref[:, ls] * dy2
                dx_ref[:, ls] = dx2
                edy.append(eam * dy2)
                exd.append(em * xd2)
            edy, exd = jnp.concatenate(edy, axis=1), jnp.concatenate(exd, axis=1)
            hs_rows.append(_sum0(h_t * dh_t))
            dst[:, gl] = _head_lanes(eal, lo, g) * dh_t + _tn_dot(cg, edy)
            dx_ref[:, 1024 + g * 128:1152 + g * 128] = _tn_dot(d_s, cg) + _nt(exd, dh_t)
            dx_ref[:, 1280 + g * 128:1408 + g * 128] = _nn(d_s, bg) + _nt(edy, h_t)
        hs = jnp.broadcast_to(jnp.concatenate(hs_rows, axis=1), (8, D))
        hsum = jnp.dot(hs, eh_ref[...], precision=HI, preferred_element_type=F32)[0:1]
        dlast = dlast + eal * hsum
        dacs = dacs + jnp.concatenate([jnp.zeros((lo, Q), F32)] * (lo > 0) + [dacs_t, jnp.zeros((LANE - 16 - lo, Q), F32)],
                                      axis=0).T
        rowi = lax.broadcasted_iota(jnp.int32, (Q, LANE), 0)
        dacs = dacs + jnp.where(rowi == last, dlast, 0.0)
        da = lax.dot_general(tri, dacs, (((0,), (0,)), ((), ())), precision=HI, preferred_element_type=F32)
        ddt = ddt_x + da * a_neg
        mine = (lane >= lo) & (lane < lo + 16)
        ddtr = jnp.where(mine, ddt * jax.nn.sigmoid(dtr_v + sp_ref[0:1]), 0.0)
        ddtr_ref[...] = ddtr
        pg_ref[0:1] += _sum0(ddtr)
        pg_ref[1:2] += jnp.where(mine[0:1], _sum0(da * dt) * a_neg, 0.0)

    blk = lambda w_: pl.BlockSpec((Q, w_), lambda s: (order(s), 0))
    return _call(body, "ssd_bwd_r" if rev else "ssd_bwd_f", (nc,),
                 [blk(XBC), blk(LANE), blk(D), pl.BlockSpec((None, LANE, D), lambda s: (order(s), 0, 0)),
                  _full(8, LANE), _full(1, D), _full(D, LANE)],
                 [blk(XBC), blk(LANE), _full(8, LANE)],
                 [_sds((n_rows, XBC)), _sds((n_rows, LANE)), _sds((8, LANE))],
                 scratch=[pltpu.VMEM((LANE, D), F32)])(xbc, dtr, dy, hin, sp, dl, eh)


def _ssd_fwd_old(xbc, dtr, sp, ncc, rev):
    n_rows = xbc.shape[0]
    nc = n_rows // Q
    lo = 16 if rev else 0
    order = _scan_order(nc, ncc, rev)

    def body(x_ref, dtr_ref, sp_ref, y_ref, hin_ref, st):
        @pl.when(pl.program_id(0) == 0)
        def _():
            st[...] = jnp.zeros_like(st)
        dt, _, acs, msk, _, last = _ssd_prep(dtr_ref[...], sp_ref[...], rev)
        acs_t, dt_t = acs.T, dt.T
        eacs = jnp.exp(acs)
        eal = jnp.exp(acs[last:last + 1, :])
        tew = jnp.exp(acs[last:last + 1, :] - acs) * dt
        lane_lt = lax.broadcasted_iota(jnp.int32, (Q, LANE), 1) < 64
        row_lt = lax.broadcasted_iota(jnp.int32, (LANE, 1), 0) < 64
        s_g = [_nt(x_ref[:, 1280 + g * 128:1408 + g * 128], x_ref[:, 1024 + g * 128:1152 + g * 128]) for g in range(2)]
        for m in range(NPAIR):
            g = m // 4
            ls = slice(m * LANE, (m + 1) * LANE)
            x2 = x_ref[:, ls]
            bg = x_ref[:, 1024 + g * 128:1152 + g * 128]
            cg = x_ref[:, 1280 + g * 128:1408 + g * 128]
            y2 = jnp.zeros((Q, LANE), F32)
            for hh in range(2):
                h = lo + 2 * m + hh
                lm = jnp.exp(jnp.where(msk, acs[:, h:h + 1] - acs_t[h:h + 1, :], -jnp.inf))
                w = s_g[g] * lm * dt_t[h:h + 1, :]
                y2 = y2 + _nn(w, jnp.where(lane_lt == (hh == 0), x2, 0.0))
            hp = st[ls, :]
            hin_ref[ls, :] = hp
            y_ref[:, ls] = y2 + _nt(cg, hp) * _pair_sel(eacs, lo, m, lane_lt)
            snew = _tn_dot(x2 * _pair_sel(tew, lo, m, lane_lt), bg)
            h0 = lo + 2 * m
            st[ls, :] = jnp.where(row_lt, eal[:, h0:h0 + 1], eal[:, h0 + 1:h0 + 2]) * hp + snew

    return _call(body, "ssd_fwd_r" if rev else "ssd_fwd_f", (nc,),
                 [pl.BlockSpec((Q, XBC), lambda s: (order(s), 0)), pl.BlockSpec((Q, LANE), lambda s: (order(s), 0)),
                  _full(8, LANE)],
                 [pl.BlockSpec((Q, D), lambda s: (order(s), 0)), pl.BlockSpec((None, D, LANE), lambda s: (order(s), 0, 0))],
                 [_sds((n_rows, D)), _sds((nc, D, LANE))], scratch=[pltpu.VMEM((D, LANE), F32)])(xbc, dtr, sp)


def _ssd_bwd_old(xbc, dtr, dy, hin, sp, dl, ncc, rev):
    n_rows = xbc.shape[0]
    nc = n_rows // Q
    lo = 16 if rev else 0
    fwd_order = _scan_order(nc, ncc, rev)
    order = lambda s: fwd_order(nc - 1 - s)
    with_skip = not rev

    def body(x_ref, dtr_ref, dy_ref, hin_ref, sp_ref, dl_ref, dx_ref, ddtr_ref, pg_ref, dst):
        @pl.when(pl.program_id(0) == 0)
        def _():
            dst[...] = jnp.zeros_like(dst)
            pg_ref[...] = jnp.zeros_like(pg_ref)
        dtr_v = dtr_ref[...]
        dt, a_neg, acs, msk, tri, last = _ssd_prep(dtr_v, sp_ref[...], rev)
        acs_t, dt_t = acs.T, dt.T
        eacs = jnp.exp(acs)
        eal = jnp.exp(acs[last:last + 1, :])
        te = jnp.exp(acs[last:last + 1, :] - acs)
        lane = lax.broadcasted_iota(jnp.int32, (Q, LANE), 1)
        sub = lax.broadcasted_iota(jnp.int32, (LANE, Q), 0)
        lane1 = lax.broadcasted_iota(jnp.int32, (1, LANE), 1)
        lane_lt = lane < 64
        row_lt = lax.broadcasted_iota(jnp.int32, (LANE, 1), 0) < 64
        bgs = [x_ref[:, 1024 + g * 128:1152 + g * 128] for g in range(2)]
        cgs = [x_ref[:, 1280 + g * 128:1408 + g * 128] for g in range(2)]
        s_g = [_nt(cgs[g], bgs[g]) for g in range(2)]
        d_s = [jnp.zeros((Q, Q), F32), jnp.zeros((Q, Q), F32)]
        dc_x = [jnp.zeros((Q, LANE), F32), jnp.zeros((Q, LANE), F32)]
        db_x = [jnp.zeros((Q, LANE), F32), jnp.zeros((Q, LANE), F32)]
        dacs = jnp.zeros((Q, LANE), F32)
        colsum_t = jnp.zeros((LANE, Q), F32)
        ddt_x = jnp.zeros((Q, LANE), F32)
        dlast = jnp.zeros((1, LANE), F32)
        for m in range(NPAIR):
            g = m // 4
            ls = slice(m * LANE, (m + 1) * LANE)
            x2, dy2 = x_ref[:, ls], dy_ref[:, ls]
            hp, dhp = hin_ref[ls, :], dst[ls, :]
            dtm, em, eam = (_pair_sel(v, lo, m, lane_lt) for v in (dt, te, eacs))
            xd2 = x2 * dtm
            bh = _nt(bgs[g], dhp)
            ch = _nt(cgs[g], hp)
            dxd2 = em * bh
            for hh in range(2):
                h = lo + 2 * m + hh
                half = lane_lt == (hh == 0)
                lm = jnp.exp(jnp.where(msk, acs[:, h:h + 1] - acs_t[h:h + 1, :], -jnp.inf))
                mh = s_g[g] * lm
                dyh = jnp.where(half, dy2, 0.0)
                dxd2 = dxd2 + _tn_dot(mh, dyh)
                dm = _nt(dyh, jnp.where(half, xd2, 0.0))
                d_s[g] = d_s[g] + dm * lm
                gh = dm * mh
                dacs = dacs + jnp.where(lane == h, jnp.sum(gh, axis=1, keepdims=True), 0.0)
                colsum_t = colsum_t + jnp.where(sub == h, jnp.sum(gh, axis=0, keepdims=True), 0.0)
                t1 = jnp.sum(jnp.where(half, dy2 * ch * eam, 0.0), axis=1, keepdims=True)
                rj = jnp.sum(jnp.where(half, xd2 * bh * em, 0.0), axis=1, keepdims=True)
                dacs = dacs + jnp.where(lane == h, t1 - rj, 0.0)
                hs = hp * dhp
                hsum = jnp.sum(jnp.sum(jnp.where(row_lt == (hh == 0), hs, 0.0), axis=1, keepdims=True), axis=0, keepdims=True)
                dlast = dlast + jnp.where(lane1 == h, jnp.sum(rj, axis=0, keepdims=True) + eal[:, h:h + 1] * hsum, 0.0)
            for hh in range(2):
                h = lo + 2 * m + hh
                half = lane_lt == (hh == 0)
                ddt_x = ddt_x + jnp.where(lane == h, jnp.sum(jnp.where(half, dxd2 * x2, 0.0), axis=1, keepdims=True), 0.0)
            dx2 = dxd2 * dtm
            if with_skip:
                dx2 = dx2 + dl_ref[:, ls] * dy2
            dx_ref[:, ls] = dx2
            edy = eam * dy2
            dc_x[g] = dc_x[g] + _nn(edy, hp)
            db_x[g] = db_x[g] + _nn(em * xd2, dhp)
            h0 = lo + 2 * m
            dst[ls, :] = jnp.where(row_lt, eal[:, h0:h0 + 1], eal[:, h0 + 1:h0 + 2]) * dhp + _tn_dot(edy, cgs[g])
        dacs = dacs - colsum_t.T
        rowi = lax.broadcasted_iota(jnp.int32, (Q, LANE), 0)
        dacs = dacs + jnp.where(rowi == last, dlast, 0.0)
        da = lax.dot_general(tri, dacs, (((0,), (0,)), ((), ())), precision=HI, preferred_element_type=F32)
        ddt = ddt_x + da * a_neg
        mine = (lane >= lo) & (lane < lo + 16)
        ddtr = jnp.where(mine, ddt * jax.nn.sigmoid(dtr_v + sp_ref[0:1]), 0.0)
        ddtr_ref[...] = ddtr
        pg_ref[0:1] += _sum0(ddtr)
        pg_ref[1:2] += jnp.where(mine[0:1], _sum0(da * dt) * a_neg, 0.0)
        for g in range(2):
            dx_ref[:, 1024 + g * 128:1152 + g * 128] = _tn_dot(d_s[g], cgs[g]) + db_x[g]
            dx_ref[:, 1280 + g * 128:1408 + g * 128] = _nn(d_s[g], bgs[g]) + dc_x[g]

    return _call(body, "ssd_bwd_r" if rev else "ssd_bwd_f", (nc,),
                 [pl.BlockSpec((Q, XBC), lambda s: (order(s), 0)), pl.BlockSpec((Q, LANE), lambda s: (order(s), 0)),
                  pl.BlockSpec((Q, D), lambda s: (order(s), 0)), pl.BlockSpec((None, D, LANE), lambda s: (order(s), 0, 0)),
                  _full(8, LANE), _full(1, D)],
                 [pl.BlockSpec((Q, XBC), lambda s: (order(s), 0)), pl.BlockSpec((Q, LANE), lambda s: (order(s), 0)),
                  _full(8, LANE)],
                 [_sds((n_rows, XBC)), _sds((n_rows, LANE)), _sds((8, LANE))],
                 scratch=[pltpu.VMEM((D, LANE), F32)])(xbc, dtr, dy, hin, sp, dl)


def _hy4_fwd(h, yf, yb, xbc, z, ysg, mp, dl, ng, wout, nct):
    n_rows = h.shape[0]

    def body(h_ref, yf_ref, yb_ref, xs_ref, z_ref, ysg_ref, mp_ref, dl_ref, ng_ref, wout_ref, hn_ref, yssd_ref, out_ref):
        ytot = yf_ref[...] + yb_ref[...] + dl_ref[...] * xs_ref[...]
        yssd = _gate_norm(ytot, z_ref[...], ng_ref[...]).astype(MXU)
        yssd_ref[...] = yssd.astype(ACT)
        out = (jnp.dot(yssd, wout_ref[0:D, :], preferred_element_type=F32)
               + jnp.dot(ysg_ref[...].astype(MXU), wout_ref[D:2 * D, :], preferred_element_type=F32))
        out_ref[...] = out
        hn_ref[...] = h_ref[...] + mp_ref[3:4] * out

    return _call(body, "hy4_fwd", (n_rows // T,),
                 [_rows(T, D), _rows(T, D, nct), _rows(T, D, nct), _rows(T, D, nct), _rows(T, D, nct), _rows(T, D),
                  _full(8, D), _full(1, D), _full(1, D), _wfull(2 * D, D)],
                 [_rows(T, D), _rows(T, D), _rows(T, D)],
                 [_sds((n_rows, D)), _sds((n_rows, D), ACT), _sds((n_rows, D))])(h, yf, yb, xbc, z, ysg, mp, dl, ng, wout)


def _hy4_bwd(dh, out, yf, yb, xbc, z, mp, dl, ng, wout, nct):
    n_lat = dh.shape[0]
    n_rows = yf.shape[0]

    def body(dh_ref, out_ref, yf_ref, yb_ref, xs_ref, z_ref, mp_ref, dl_ref, ng_ref, wout_ref,
             dy_ref, dz_ref, dysg_ref, doutb_ref, pg_ref):
        i = pl.program_id(0)

        @pl.when(i == 0)
        def _():
            pg_ref[...] = jnp.zeros_like(pg_ref)

        @pl.when(i < nct)
        def _():
            dy_ref[...] = jnp.zeros_like(dy_ref)
            dz_ref[...] = jnp.zeros_like(dz_ref)
            dysg_ref[...] = jnp.zeros_like(dysg_ref)
            doutb_ref[...] = jnp.zeros_like(doutb_ref)

        @pl.when(i >= nct)
        def _():
            dhp = dh_ref[...]
            doutb = (mp_ref[3:4] * dhp).astype(MXU)
            doutb_ref[...] = doutb.astype(ACT)
            dysg_ref[...] = _nt(doutb, wout_ref[D:2 * D, :])
            dyssd = _nt(doutb, wout_ref[0:D, :])
            xs = xs_ref[...]
            ytot = yf_ref[...] + yb_ref[...] + dl_ref[...] * xs
            _, vjp = jax.vjp(_gate_norm, ytot, z_ref[...], ng_ref[...])
            dytot, dz, dng = vjp(dyssd)
            dy_ref[...] = dytot
            dz_ref[...] = dz.astype(ACT)
            pg_ref[0:1] += _sum0(dhp * out_ref[...])
            pg_ref[1:2] += dng
            pg_ref[2:3] += _sum0(dytot * xs)

    return _call(body, "hy4_bwd", (n_rows // T,),
                 [_rows_lat(T, D, nct), _rows_lat(T, D, nct), _rows(T, D), _rows(T, D), _rows(T, D), _rows(T, D),
                  _full(8, D), _full(1, D), _full(1, D), _wfull(2 * D, D)],
                 [_rows(T, D), _rows(T, D), _rows(T, D), _rows_lat(T, D, nct), _full(8, D)],
                 [_sds((n_rows, D)), _sds((n_rows, D), ACT), _sds((n_rows, D)), _sds((n_lat, D), ACT), _sds((8, D))],
                 )(dh, out, yf, yb, xbc, z, mp, dl, ng, wout)


def _loss_bwd(h, tgt, fng):
    n_rows = h.shape[0]

    def body(h_ref, t_ref, g_ref, dh_ref, pg_ref, ls_ref):
        @pl.when(pl.program_id(0) == 0)
        def _():
            pg_ref[...] = jnp.zeros_like(pg_ref)
            ls_ref[...] = jnp.zeros_like(ls_ref)
        hv = h_ref[...]
        g = g_ref[...]
        r = lax.rsqrt(jnp.mean(hv * hv, axis=-1, keepdims=True) + EPS)
        n = hv * r
        e = n * g - t_ref[...]
        ls_ref[...] += 0.5 * jnp.sum(jnp.sum(e * e, axis=1, keepdims=True), axis=0, keepdims=True) * (1.0 / D)
        dyv = e * (1.0 / D)
        pg_ref[0:1] += _sum0(dyv * n)
        dn = dyv * g
        dh_ref[...] = r * (dn - n * jnp.mean(dn * n, axis=-1, keepdims=True))

    return _call(body, "loss_bwd", (n_rows // T,), [_rows(T, D), _rows(T, D), _full(1, D)],
                 [_rows(T, D), _full(8, D), _full(8, LANE)],
                 [_sds((n_rows, D)), _sds((8, D)), _sds((8, LANE))])(h, tgt, fng)


def _pad_rows(a, rows):
    return jnp.concatenate([a, jnp.zeros((rows - a.shape[0],) + a.shape[1:], a.dtype)], axis=0)


def _mp(*rows):
    return _pad_rows(jnp.stack(rows, axis=0), 8)


def _local_step(x, ctx, tgt, ada, cada0, w, late_w=None, early_grads=None):
    n_lat, n_ctx = x.shape[0], ctx.shape[0]
    nct, ncc = n_ctx // T, n_ctx // Q
    a0 = [ada[0, k * D:(k + 1) * D] for k in range(6)]
    a1 = [ada[1, k * D:(k + 1) * D] for k in range(6)]
    c0 = [cada0[k * D:(k + 1) * D] for k in range(6)]
    g = {}

    mp2 = jnp.stack([_mp(w["norm_mix_g"][0], c0[0], c0[1]), _mp(w["norm_mix_g"][0], a0[0], a0[1], a0[2])], axis=0)
    mp_l0 = mp2[1]
    sgbt = _pad_cols(w["sg_b"][0].T, LANE)
    lng, lnb = w["sg_ln_g"][0][None], w["sg_ln_b"][0][None]
    z, uv, xbcp, dtr, ysg = _hy1_fwd(ctx, x, mp2, w["wz"], w["wuv"], w["wxbc"], w["wdt"], lng, lnb, w["sg_w"], sgbt, nct)
    cw = _pad_rows(w["ssd_conv_w"][0], 8)
    cb = w["ssd_conv_b"][0][None]
    xbc = _cv5_fwd(xbcp, cw, cb, nct)
    sp = _pad_rows(jnp.stack([_pad_cols(w["ssd_dt_bias"][0].reshape(1, 32), LANE)[0],
                              _pad_cols(w["ssd_a_log"][0].reshape(1, 32), LANE)[0]], axis=0), 8)
    dl = jnp.repeat(w["ssd_d"][0], 64)[None]
    ng = w["ssd_norm_g"][0][None]
    yf, hin_f = _ssd_fwd(xbc, dtr, sp, ncc, False)
    yb, hin_b = _ssd_fwd(xbc, dtr, sp, ncc, True)
    if late_w is not None:
        w = {**w, **late_w(yb)}
    h1, yssd, out0 =_hy4_fwd(x, yf, yb, xbc, z, ysg, mp_l0, dl, ng, w["hy_w_out"], nct)

    mpm0 = _mp(w["norm_mlp_g"][0], a0[3], a0[4], a0[5])
    h2, am0, ym0 = _mlp_fwd(h1, mpm0, w["wpack"], 0, "mlp0_fwd")

    mpc = _mp(w["norm_mix_g"][1], a1[0], a1[1], a1[2])
    wdw = _pad_rows(w["cf_w_dw"][0], 32)
    glu, acf = _cf1_fwd(h2, mpc, w["cf_w_pw1"], w["cf_b_pw1"])
    h3, cv, scf, ycf = _cf2_fwd(h2, glu, mpc, wdw, w["cf_b_dw"], w["cf_ln_g"], w["cf_ln_b"], w["cf_w_pw2"], w["cf_b_pw2"])

    mpm1 = _mp(w["norm_mlp_g"][1], a1[3], a1[4], a1[5])
    h4, am1, ym1 = _mlp_fwd(h3, mpm1, w["wpack"], 1, "mlp1_fwd")

    dh4, pg_f, ls = _loss_bwd(h4, tgt, w["final_norm_g"][None])
    loss = ls[0, 0]
    g["final_norm_g"] = pg_f[0]

    gp = {}
    dh3, da1, dy1, u1, pgm1 = _mlp_bwd(dh4, h3, am1, ym1, mpm1, w["wpack"], 1, "mlp1_bwd")
    gw1_1 = _tn(u1, da1, "tn_mlp1_w1", shard=("col", 1024))
    gw2_1 = _tn(am1, dy1, "tn_mlp1_w2", relu2=True, shard=("row", 1024))

    dcv, dycf, pgc2 = _cf2_bwd(dh3, ycf, cv, mpc, w["cf_ln_g"], w["cf_ln_b"], w["cf_w_pw2"])
    gp["cf_w_pw2"] = _tn(scf, dycf, "tn_cf_pw2", shard=("row", 256))
    dh2, dacf, ucf, pgc1, pbc1, dwdw = _cf1_bwd(dh3, h2, acf, dcv, glu, mpc, wdw, w["cf_w_pw1"])
    gp["cf_w_pw1"] = _tn(ucf, dacf, "tn_cf_pw1", shard=("col", 512)).reshape(4, 512, 1024)
    g["cf_b_pw2"], g["cf_ln_g"], g["cf_ln_b"], g["cf_b_dw"] = pgc2[1], pgc2[2], pgc2[3], pgc2[4]
    g["cf_b_pw1"] = pbc1[0]
    g["cf_w_dw"] = dwdw[:CK]

    dh1, da0, dy0, u0, pgm0 = _mlp_bwd(dh2, h1, am0, ym0, mpm0, w["wpack"], 0, "mlp0_bwd")
    gp["mlp_w1"] = jnp.concatenate([_tn(u0, da0, "tn_mlp0_w1", shard=("col", 1024)), gw1_1], axis=1)
    gp["mlp_w2"] = jnp.concatenate([_tn(am0, dy0, "tn_mlp0_w2", relu2=True, shard=("row", 1024)), gw2_1], axis=1)
    g["norm_mlp_g"] = jnp.stack([pgm0[0], pgm1[0]])

    dyt, dz, dysg, doutb, pg4 = _hy4_bwd(dh1, out0, yf, yb, xbc, z, mp_l0, dl, ng, w["hy_w_out"], nct)
    gp["hy_w_out"] = jnp.concatenate([_tn(yssd, doutb, "tn_out_ssd", shard=("row", 512)),
                                      _tn(ysg, doutb, "tn_out_sg", shard=("row", 512))], axis=0)
    if early_grads is not None:
        sp = sp + early_grads(gp)
    head_of_lane = jnp.arange(D, dtype=jnp.int32)[:, None] // 64
    col = jnp.arange(LANE, dtype=jnp.int32)[None, :]
    dxf, ddf, pgsf = _ssd_bwd(xbc, dtr, dyt, hin_f, sp, dl, (col == head_of_lane).astype(F32), ncc, False)
    dxb, ddb, pgsb = _ssd_bwd(xbc, dtr, dyt, hin_b, sp, dl, (col == head_of_lane + 16).astype(F32), ncc, True)
    dcv5, pgcb = _cv5_bwd1(xbcp, dxf, dxb, cw, cb, nct)
    dxbcp, dcw = _cv5_bwd2(dcv5, xbcp, cw, nct)
    dx, ucat, duv, ddt, pg2, pln, dsgw, dsgbt = _hy1_bwd(
        ctx, x, uv, dz, dxbcp, ddf, ddb, dysg, dh1, mp2, w["wz"], w["wuv"], w["wxbc"], w["wdt"], lng, lnb, w["sg_w"], sgbt, nct)
    g["hy_w_in"] = jnp.concatenate([_tn(ucat, dz, "tn_in_z"), _tn(ucat, dxbcp, "tn_in_xbc"),
                                    _tn(ucat, ddt, "tn_in_dt")[:, :32], _tn(ucat, duv, "tn_in_uv")], axis=1)
    g["ssd_conv_w"], g["ssd_conv_b"] = dcw[:SK], pgcb[0]
    pgs = pgsf + pgsb
    g["ssd_dt_bias"], g["ssd_a_log"] = pgs[0, :32].reshape(2, 16), pgs[1, :32].reshape(2, 16)
    g["ssd_d"] = jnp.sum(pg4[2].reshape(16, 64), axis=1)
    g["ssd_norm_g"] = pg4[1]
    g["sg_ln_g"], g["sg_ln_b"] = pln[0], pln[1]
    g["sg_w"], g["sg_b"] = dsgw, dsgbt[:, :SGG].T
    g["norm_mix_g"] = jnp.stack([pg2[0, 0] + pg2[1, 0], pgc1[0]])

    zero = jnp.zeros((D,), F32)
    d_ada = jnp.stack([jnp.concatenate([pg2[1, 1], pg2[1, 2], pg4[0], pgm0[1], pgm0[2], pgm0[3]]),
                       jnp.concatenate([pgc1[1], pgc1[2], pgc2[0], pgm1[1], pgm1[2], pgm1[3]])])
    d_cada0 = jnp.concatenate([pg2[0, 1], pg2[0, 2], zero, zero, zero, zero])
    g["pieces"] = gp
    return loss, dx, g, d_ada, d_cada0


def _pad_cols(a, cols):
    return jnp.concatenate([a, jnp.zeros(a.shape[:-1] + (cols - a.shape[-1],), a.dtype)], axis=-1)


MESH = pl.DeviceIdType.MESH
ANY = pl.BlockSpec(memory_space=pl.ANY)
IN_VMEM = pl.BlockSpec(memory_space=pltpu.VMEM)


def _coords():
    return lax.axis_index("x"), lax.axis_index("y"), lax.axis_index("c")


def _ag8(x, name):
    r, wd = x.shape

    def body(x_ref, o_ref, send, recv, lsem):
        mx, my, mc = _coords()
        me = 4 * mx + 2 * my + mc
        mine = pltpu.make_async_copy(x_ref, o_ref.at[me], lsem)
        mine.start()
        sent, peers = [], []
        for k in range(1, 8):
            px = 1 - mx if k & 4 else mx
            py = 1 - my if k & 2 else my
            pc = 1 - mc if k & 1 else mc
            cp = pltpu.make_async_remote_copy(src_ref=x_ref, dst_ref=o_ref.at[me], send_sem=send.at[k - 1],
                                              recv_sem=recv.at[k - 1], device_id=(px, py, pc), device_id_type=MESH)
            cp.start()
            sent.append(cp)
            peers.append((4 * px + 2 * py + pc, (px, py, pc)))
        for k in range(1, 8):
            slot, peer = peers[k - 1]
            pltpu.make_async_remote_copy(src_ref=x_ref, dst_ref=o_ref.at[slot], send_sem=send.at[k - 1],
                                         recv_sem=recv.at[k - 1], device_id=peer, device_id_type=MESH).wait_recv()
        for cp in sent:
            cp.wait_send()
        mine.wait()

    return pl.pallas_call(
        body, name=name, out_shape=_sds((8, r, wd), x.dtype), in_specs=[IN_VMEM], out_specs=IN_VMEM,
        scratch_shapes=[pltpu.SemaphoreType.DMA((7,)), pltpu.SemaphoreType.DMA((7,)), pltpu.SemaphoreType.DMA(())],
        compiler_params=pltpu.CompilerParams(vmem_limit_bytes=VMEM_LIMIT))(x)


def _xchg4(buf, name, a2a):
    r, wd = buf.shape[-2:]

    def body(in_ref, o_ref, send, recv, lsem):
        mx, my, mc = _coords()
        me = 2 * mx + my
        mine = pltpu.make_async_copy(in_ref.at[me] if a2a else in_ref, o_ref.at[me], lsem)
        mine.start()
        sent, peers = [], []
        for k in range(1, 4):
            px = 1 - mx if k & 2 else mx
            py = 1 - my if k & 1 else my
            pj = 2 * px + py
            cp = pltpu.make_async_remote_copy(src_ref=in_ref.at[pj] if a2a else in_ref, dst_ref=o_ref.at[me],
                                              send_sem=send.at[k - 1], recv_sem=recv.at[k - 1],
                                              device_id=(px, py, mc), device_id_type=MESH)
            cp.start()
            sent.append(cp)
            peers.append((pj, (px, py, mc)))
        for k in range(1, 4):
            pj, peer = peers[k - 1]
            pltpu.make_async_remote_copy(src_ref=in_ref.at[pj] if a2a else in_ref, dst_ref=o_ref.at[pj],
                                         send_sem=send.at[k - 1], recv_sem=recv.at[k - 1],
                                         device_id=peer, device_id_type=MESH).wait_recv()
        for cp in sent:
            cp.wait_send()
        mine.wait()

    return pl.pallas_call(
        body, name=name, out_shape=_sds((4, r, wd), buf.dtype), in_specs=[ANY], out_specs=ANY,
        scratch_shapes=[pltpu.SemaphoreType.DMA((3,)), pltpu.SemaphoreType.DMA((3,)), pltpu.SemaphoreType.DMA(())],
        )(buf)


HBM = pl.BlockSpec(memory_space=pltpu.HBM)
SEM = pl.BlockSpec(memory_space=pltpu.SEMAPHORE)
EFFECT = pltpu.SideEffectType.DATAFLOW_SIDE_EFFECTING


def _x4_peers(in_ref, land_ref, send, recv, a2a):
    mx, my, mc = _coords()
    me = 2 * mx + my
    out = []
    for k in range(1, 4):
        px = 1 - mx if k & 2 else mx
        py = 1 - my if k & 1 else my
        pj = 2 * px + py
        mk = functools.partial(pltpu.make_async_remote_copy, src_ref=in_ref.at[pj] if a2a else in_ref,
                               send_sem=send.at[k - 1], recv_sem=recv.at[k - 1], device_id=(px, py, mc), device_id_type=MESH)
        out.append((mk(dst_ref=land_ref.at[me]), mk(dst_ref=land_ref.at[pj])))
    return out


def _x4_start(buf, name, a2a):
    r, wd = buf.shape[-2:]

    def body(in_ref, land_ref, send, recv, in_thru, land_thru, token):
        for start, _ in _x4_peers(in_ref, land_ref, send, recv, a2a):
            start.start()
        token[...] = jnp.zeros_like(token)

    land = lax.empty((4, r, wd), buf.dtype)
    return pl.pallas_call(
        body, name=name,
        out_shape=(pltpu.SemaphoreType.DMA((3,)), pltpu.SemaphoreType.DMA((3,)), pltpu.HBM(buf.shape, buf.dtype),
                   pltpu.HBM(land.shape, land.dtype), _sds((8, LANE))),
        in_specs=(HBM, HBM), out_specs=(SEM, SEM, HBM, HBM, IN_VMEM), input_output_aliases={0: 2, 1: 3},
        compiler_params=pltpu.CompilerParams(has_side_effects=EFFECT),
    )(pltpu.with_memory_space_constraint(buf, pltpu.HBM), pltpu.with_memory_space_constraint(land, pltpu.HBM))


def _x4_wait(send, recv, buf_thru, land_thru, after, name, a2a):
    def body(in_ref, land_ref, send_ref, recv_ref, after_ref, in_dead, got_ref):
        for _, arrive in _x4_peers(in_ref, land_ref, send_ref, recv_ref, a2a):
            arrive.wait_send()
            arrive.wait_recv()

    return pl.pallas_call(
        body, name=name, out_shape=(pltpu.HBM(buf_thru.shape, buf_thru.dtype), pltpu.HBM(land_thru.shape, land_thru.dtype)),
        in_specs=(HBM, HBM, SEM, SEM, ANY), out_specs=(HBM, HBM), input_output_aliases={0: 0, 1: 1},
        compiler_params=pltpu.CompilerParams(has_side_effects=EFFECT),
    )(buf_thru, land_thru, send, recv, after)[1]


def _xchg_sib(x, name):
    def body(in_ref, o_ref, send, recv):
        mx, my, mc = _coords()
        cp = pltpu.make_async_remote_copy(src_ref=in_ref, dst_ref=o_ref, send_sem=send, recv_sem=recv,
                                          device_id=(mx, my, 1 - mc), device_id_type=MESH)
        cp.start()
        cp.wait_recv()
        cp.wait_send()

    return pl.pallas_call(
        body, name=name, out_shape=_sds(x.shape, x.dtype), in_specs=[ANY], out_specs=ANY,
        scratch_shapes=[pltpu.SemaphoreType.DMA(()), pltpu.SemaphoreType.DMA(())])(x)


def _sum_slots(gat, slots, name, tr=None):
    n, r, wd = gat.shape
    tr = r if tr is None else tr

    def body(g_ref, o_ref):
        acc = g_ref[slots[0]].astype(F32)
        for s in slots[1:]:
            acc = acc + g_ref[s].astype(F32)
        o_ref[...] = acc

    return _call(body, name, (r // tr,), [pl.BlockSpec((n, tr, wd), lambda i: (0, i, 0))], _rows(tr, wd), _sds((r, wd)))(gat)


def _add(a, b, name, tr):
    def body(a_ref, b_ref, o_ref):
        o_ref[...] = a_ref[...] + b_ref[...]

    r, wd = a.shape
    return _call(body, name, (r // tr,), [_rows(tr, wd), _rows(tr, wd)], _rows(tr, wd), _sds((r, wd)))(a, b)


def _ada_fwd(x16, ada_w_loc, ada_b_loc):
    nloc = ada_w_loc.shape[-1]

    def body(x_ref, w_ref, b_ref, s_ref, o_ref):
        s = _silu(x_ref[...])
        s_ref[...] = s
        o_ref[...] = jnp.dot(s, w_ref[...], precision=HI, preferred_element_type=F32) + b_ref[...]

    return _call(body, "ada_fwd", (2,),
                 [_full(16, D), pl.BlockSpec((None, D, nloc), lambda l: (l, 0, 0)), pl.BlockSpec((None, 1, nloc), lambda l: (l, 0, 0))],
                 [_full(16, D), pl.BlockSpec((None, 16, nloc), lambda l: (l, 0, 0))],
                 [_sds((16, D)), _sds((2, 16, nloc))])(x16, ada_w_loc, ada_b_loc[:, None, :])


def _ada_bwd(s16, d_loc, ada_w_loc):
    nloc = ada_w_loc.shape[-1]

    def body(s_ref, d_ref, w_ref, gw_ref, cp_ref):
        gw_ref[...] = lax.dot_general(s_ref[...], d_ref[...], (((0,), (0,)), ((), ())), precision=HI,
                                      preferred_element_type=F32)

        @pl.when(pl.program_id(0) == 0)
        def _():
            cp_ref[...] = lax.dot_general(d_ref[8:16, :], w_ref[...], (((1,), (1,)), ((), ())), precision=HI,
                                          preferred_element_type=F32)

    return _call(body, "ada_bwd", (2,),
                 [_full(16, D), pl.BlockSpec((None, 16, nloc), lambda l: (l, 0, 0)), pl.BlockSpec((None, D, nloc), lambda l: (l, 0, 0))],
                 [pl.BlockSpec((None, D, nloc), lambda l: (l, 0, 0)), _full(8, D)],
                 [_sds((2, D, nloc)), _sds((8, D))])(s16, d_loc, ada_w_loc)


def _cctx_grad(dscc, c_ctx):
    def body(d_ref, c_ref, o_ref):
        _, vjp = jax.vjp(_silu, c_ref[...])
        o_ref[...] = vjp(d_ref[...])[0]

    return _call(body, "cctx_grad", (1,), [_full(8, D), _full(8, D)], _full(8, D), _sds((8, D)))(dscc, c_ctx)


def _adamw_math(w, g, m, v):
    mn = ADAM_B1 * m + (1.0 - ADAM_B1) * g
    vn = ADAM_B2 * v + (1.0 - ADAM_B2) * jnp.square(g)
    c1 = 1.0 - ADAM_B1 ** ADAM_STEP
    c2 = 1.0 - ADAM_B2 ** ADAM_STEP
    return -ADAM_LR * ((mn / c1) / (jnp.sqrt(vn / c2) + ADAM_EPS) + ADAM_WD * w), mn, vn


def _adamw(w, g, m, v, name):
    n_l, r, wd = w.shape
    tr = 256 if r % 256 == 0 else r

    def body(w_ref, g_ref, m_ref, v_ref, d_ref, mo_ref, vo_ref):
        d_ref[...], mo_ref[...], vo_ref[...] = _adamw_math(w_ref[...], g_ref[...], m_ref[...], v_ref[...])

    spec = pl.BlockSpec((None, tr, wd), lambda a, i: (a, i, 0))
    return tuple(_call(body, name, (n_l, r // tr), [spec] * 4, [spec] * 3, [_sds(w.shape)] * 3)(w, g, m, v))


def _adamw_small(ws, gs, ms, vs, name):
    n = len(ws)
    shapes = [a.shape for a in ws]
    as2d = lambda a: a.reshape(-1, a.shape[-1])

    def body(*refs):
        ins, outs = refs[:4 * n], refs[4 * n:]
        for k in range(n):
            res = _adamw_math(ins[k][...], ins[n + k][...], ins[2 * n + k][...], ins[3 * n + k][...])
            for j in range(3):
                outs[j * n + k][...] = res[j]

    flat = [as2d(a) for group in (ws, gs, ms, vs) for a in group]
    specs = [_full(*a.shape) for a in flat]
    outs = _call(body, name, (1,), specs, specs[:n] * 3, [_sds(a.shape) for a in flat[:n]] * 3)(*flat)
    return tuple([outs[j * n + k].reshape(shapes[k]) for k in range(n)] for j in range(3))


ROW = 1024


def _nrows(size):
    return -(-size // ROW)


def _pack(arrs, rows_total, dtype=F32):
    parts = []
    for a in arrs:
        flat = a.reshape(-1).astype(dtype)
        pad = _nrows(flat.shape[0]) * ROW - flat.shape[0]
        parts.append(flat if pad == 0 else jnp.concatenate([flat, jnp.zeros((pad,), dtype)]))
    flat = jnp.concatenate(parts)
    out = flat.reshape(-1, ROW)
    return _pad_rows(out, rows_total)


def _unpack(buf, shapes):
    lead = buf.shape[:-2]
    out, r0 = [], 0
    for shp in shapes:
        size = 1
        for s in shp:
            size *= s
        nr = _nrows(size)
        piece = lax.slice_in_dim(buf, r0, r0 + nr, axis=len(lead))
        out.append(piece.reshape(lead + (nr * ROW,))[..., :size].reshape(lead + tuple(shp)))
        r0 += nr
    return out


SLOT = 16


def _slot_rows(size):
    return _round_up(size // ROW, SLOT)


def _pack_rows(arrs, rows_total, dtype):
    parts, used = [], 0
    for a in arrs:
        part = a.astype(dtype).reshape(-1, ROW)
        extra = _slot_rows(a.size) - part.shape[0]
        parts.append(part if extra == 0 else jnp.pad(part, ((0, extra), (0, 0))))
        used += _slot_rows(a.size)
    if rows_total > used:
        parts.append(jnp.zeros((rows_total - used, ROW), dtype))
    return jnp.concatenate(parts, axis=0)


def _unpack_rows(buf, shapes):
    lead = buf.shape[:-2]
    out, r0 = [], 0
    for shp in shapes:
        size = 1
        for s in shp:
            size *= s
        piece = lax.slice_in_dim(buf, r0, r0 + size // ROW, axis=len(lead))
        out.append(piece.reshape(lead + tuple(shp)))
        r0 += _slot_rows(size)
    return out


def _round_up(n, k):
    return -(-n // k) * k


WEIGHTS = ['c_ctx', 'ada_w', 'ada_b', 'norm_mix_g', 'norm_mlp_g', 'mlp_w1', 'mlp_w2', 'hy_w_in', 'ssd_conv_w', 'ssd_conv_b',
           'ssd_dt_bias', 'ssd_a_log', 'ssd_d', 'ssd_norm_g', 'sg_ln_g', 'sg_ln_b', 'sg_w', 'sg_b', 'hy_w_out', 'cf_w_pw1',
           'cf_b_pw1', 'cf_w_dw', 'cf_b_dw', 'cf_ln_g', 'cf_ln_b', 'cf_w_pw2', 'cf_b_pw2', 'final_norm_g']
BIG = {'mlp_w1': 2, 'mlp_w2': 1, 'hy_w_in': 2, 'hy_w_out': 1, 'cf_w_pw1': 2, 'cf_w_pw2': 1}
SMALL_SHARD = ['ssd_conv_w', 'cf_b_pw1', 'cf_w_dw', 'cf_b_dw', 'cf_ln_g', 'cf_ln_b', 'cf_b_pw2']
REP = ['norm_mix_g', 'norm_mlp_g', 'ssd_conv_b', 'ssd_dt_bias', 'ssd_a_log', 'ssd_d', 'ssd_norm_g', 'sg_ln_g', 'sg_ln_b',
       'sg_w', 'sg_b', 'final_norm_g']


def _gather_shards(stacked, axis):
    return jnp.concatenate([stacked[j] for j in range(4)], axis=axis)


def _split_shards(full, axis):
    n = full.shape[axis] // 4
    return [lax.slice_in_dim(full, j * n, (j + 1) * n, axis=axis) for j in range(4)]


def kernel(x, c, ctx, c_ctx, ada_w, ada_b, norm_mix_g, norm_mlp_g, mlp_w1, mlp_w2, hy_w_in, ssd_conv_w, ssd_conv_b, ssd_dt_bias, ssd_a_log, ssd_d, ssd_norm_g, sg_ln_g, sg_ln_b, sg_w, sg_b, hy_w_out, cf_w_pw1, cf_b_pw1, cf_w_dw, cf_b_dw, cf_ln_g, cf_ln_b, cf_w_pw2, cf_b_pw2, final_norm_g, loss_target, m_c_ctx, m_ada_w, m_ada_b, m_norm_mix_g, m_norm_mlp_g, m_mlp_w1, m_mlp_w2, m_hy_w_in, m_ssd_conv_w, m_ssd_conv_b, m_ssd_dt_bias, m_ssd_a_log, m_ssd_d, m_ssd_norm_g, m_sg_ln_g, m_sg_ln_b, m_sg_w, m_sg_b, m_hy_w_out, m_cf_w_pw1, m_cf_b_pw1, m_cf_w_dw, m_cf_b_dw, m_cf_ln_g, m_cf_ln_b, m_cf_w_pw2, m_cf_b_pw2, m_final_norm_g, v_c_ctx, v_ada_w, v_ada_b, v_norm_mix_g, v_norm_mlp_g, v_mlp_w1, v_mlp_w2, v_hy_w_in, v_ssd_conv_w, v_ssd_conv_b, v_ssd_dt_bias, v_ssd_a_log, v_ssd_d, v_ssd_norm_g, v_sg_ln_g, v_sg_ln_b, v_sg_w, v_sg_b, v_hy_w_out, v_cf_w_pw1, v_cf_b_pw1, v_cf_w_dw, v_cf_b_dw, v_cf_ln_g, v_cf_ln_b, v_cf_w_pw2, v_cf_b_pw2, v_final_norm_g):
    args = locals()
    wl = {n: args[n] for n in WEIGHTS}
    ml = {n: args["m_" + n] for n in WEIGHTS}
    vl = {n: args["v_" + n] for n in WEIGHTS}
    mx, my, mc = _coords()
    me = 4 * mx + 2 * my + mc
    shard = 2 * mx + my
    even = (0, 2, 4, 6)

    def start_gather(names, name, tie=None):
        rows = sum(_slot_rows(wl[n].size) for n in names)
        buf = _pack_rows([wl[n] for n in names], rows, MXU)
        if tie is not None:
            buf, _ = lax.optimization_barrier((buf, tie))
        return _x4_start(buf, name, a2a=False)

    def finish_gather(handle, names, after, name):
        send, recv, own, land, _ = handle
        land = _x4_wait(send, recv, own, land, after, name, a2a=False)
        got = lax.dynamic_update_slice(land, own[None], (shard, 0, 0))
        shapes = [wl[n].shape for n in names]
        wfull = {n: _gather_shards(st, BIG[n]) for n, st in zip(names, _unpack_rows(got, shapes))}
        return wfull, got

    rest_names = [n for n in BIG if n != "hy_w_in"]
    h_in = start_gather(["hy_w_in"], "agw_in_start")
    c = c + h_in[4][0, 0]

    small_shapes = [wl[n].shape for n in SMALL_SHARD]
    blk1 = _pack([c] + [wl[n] for n in SMALL_SHARD], 24)
    got1 = _ag8(blk1, "ag_cond")
    x16 = _pad_rows(jnp.concatenate([got1[:, 0, :], c_ctx[None]], axis=0), 16)
    small_full = {}
    for n, parts in zip(SMALL_SHARD, _unpack(got1[:, 1:, :], small_shapes)):
        small_full[n] = jnp.concatenate([parts[s] for s in even], axis=-1)

    nloc = ada_w.shape[-1]
    ada_b_loc = lax.dynamic_slice_in_dim(ada_b, shard * nloc, nloc, axis=1)
    s16, ada_loc = _ada_fwd(x16, ada_w, ada_b_loc)
    got2 = _ag8(ada_loc.reshape(32, nloc), "ag_ada").reshape(8, 2, 16, nloc)
    ada_full = jnp.concatenate([got2[s] for s in even], axis=-1)
    ada_me = lax.dynamic_slice_in_dim(ada_full, me, 1, axis=1)[:, 0, :]
    cada0 = ada_full[0, 8, :]

    w = {n: wl[n] for n in WEIGHTS if n not in BIG and n not in SMALL_SHARD}
    w.update(small_full)
    h_rest = start_gather(rest_names, "agw_rest_start", tie=ada_me)
    win = finish_gather(h_in, ["hy_w_in"], h_rest[4], "agw_in_wait")[0]["hy_w_in"][0]
    w["wz"], w["wxbc"], w["wuv"] = win[:, :D], win[:, D:D + XBC], win[:, D + XBC + 32:]
    w["wdt"] = _pad_cols(win[:, D + XBC:D + XBC + 32], LANE)
    w["sg_w"] = sg_w[0].astype(MXU)

    def late_w(after):
        wfull, got = finish_gather(h_rest, rest_names, after, "agw_rest_wait")
        return {"hy_w_out": wfull["hy_w_out"][0], "wpack": got,
                "cf_w_pw1": wfull["cf_w_pw1"][0], "cf_w_pw2": wfull["cf_w_pw2"][0]}

    full_shape = {n: wl[n].shape for n in WEIGHTS}
    for n in BIG:
        full_shape[n] = tuple(s * 4 if a == BIG[n] else s for a, s in enumerate(wl[n].shape))
    for n in SMALL_SHARD:
        full_shape[n] = wl[n].shape[:-1] + (wl[n].shape[-1] * 4,)

    def grad_pieces(g_, names):
        rows = _round_up(sum(_slot_rows(wl[n].size) for n in names), 512)
        return jnp.stack([_pack_rows([_split_shards(g_[n].reshape(full_shape[n]), BIG[n])[j] for n in names], rows, jnp.bfloat16)
                          for j in range(4)])

    early_names = ["mlp_w1", "mlp_w2", "cf_w_pw1", "cf_w_pw2", "hy_w_out"]
    last_names = ["hy_w_in"]
    early = {}

    def early_grads(gp):
        used = sum(gp[n].shape[1] for n in early_names)
        parts = [gp[n] for n in early_names] + [jnp.zeros((4, _round_up(used, 512) - used, ROW), jnp.bfloat16)]
        early["h"] = _x4_start(jnp.concatenate(parts, axis=1), "a2a_early_start", a2a=True)
        return early["h"][4][0, 0]

    loss_part, dx, g, d_ada, d_cada0 = _local_step(x[0], ctx[0], loss_target[0], ada_me, cada0, w, late_w, early_grads)
    loss = lax.psum(loss_part, ("x", "y", "c"))
    g = {n: a.reshape(full_shape[n]) for n, a in g.items() if n != "pieces"}

    sm_names = REP + SMALL_SHARD
    srows = _round_up(sum(_nrows(g[n].size) for n in sm_names) + 18, 16)
    got3 = _ag8(_pack([g[n] for n in sm_names] + [d_ada, d_cada0], srows, jnp.bfloat16), "ag_small")
    tot3 = _sum_slots(got3, tuple(range(8)), "sum_small")
    sm_tot = _unpack(tot3, [full_shape[n] for n in sm_names] + [(2, 6 * D), (6 * D,)])
    grads = dict(zip(sm_names, sm_tot[:-2]))
    for n in SMALL_SHARD:
        k = wl[n].shape[-1]
        grads[n] = lax.dynamic_slice_in_dim(grads[n], shard * k, k, axis=grads[n].ndim - 1)
    dada_tot, dcada_tot = sm_tot[-2], sm_tot[-1]
    grads["ada_b"] = dada_tot.at[0].add(dcada_tot)
    r_ada = sum(_nrows(g[n].size) for n in sm_names)
    dada_all = got3[:, r_ada:r_ada + 12, :].astype(F32).reshape(8, 2, 6 * D)
    d16 = jnp.concatenate([jnp.transpose(dada_all, (1, 0, 2)),
                           jnp.stack([dcada_tot, jnp.zeros_like(dcada_tot)])[:, None, :],
                           jnp.zeros((2, 7, 6 * D), F32)], axis=1)
    d_loc = lax.dynamic_slice_in_dim(d16, shard * nloc, nloc, axis=2)
    grads["ada_w"], cpart = _ada_bwd(s16, d_loc, ada_w)
    got4 = _ag8(cpart, "ag_cctx")
    dscc = _sum_slots(got4, even, "sum_cctx")
    grads["c_ctx"] = _cctx_grad(dscc, _pad_rows(c_ctx[None], 8))[0]

    got_last = _xchg4(grad_pieces(g, last_names), "a2a_last", a2a=True)
    send, recv, own, land, _ = early["h"]
    land = _x4_wait(send, recv, own, land, got_last, "a2a_early_wait", a2a=True)
    got_early = lax.dynamic_update_slice(land, lax.dynamic_slice_in_dim(own, shard, 1, axis=0), (shard, 0, 0))
    for got, names, tag in ((got_early, early_names, "early"), (got_last, last_names, "last")):
        part = _sum_slots(got, (0, 1, 2, 3), "sum_grads_" + tag, tr=512)
        tot = _add(part, _xchg_sib(part, "swap_grads_" + tag), "add_grads_" + tag, 512)
        grads.update(zip(names, _unpack_rows(tot, [wl[n].shape for n in names])))

    delta, new_m, new_v = {}, {}, {}
    for n in list(BIG) + ["ada_w"]:
        delta[n], new_m[n], new_v[n] = _adamw(wl[n], grads[n], ml[n], vl[n], "adamw_" + n)
    small = ["c_ctx", "ada_b"] + REP + SMALL_SHARD
    outs = _adamw_small(*([src[n].reshape(wl[n].shape) for n in small] for src in (wl, grads, ml, vl)), "adamw_small")
    for dst, vals in zip((delta, new_m, new_v), outs):
        dst.update(zip(small, vals))

    return (loss, dx[None], *[grads[n].reshape(wl[n].shape) for n in WEIGHTS], *[delta[n] for n in WEIGHTS],
            *[new_m[n] for n in WEIGHTS], *[new_v[n] for n in WEIGHTS])
```

```python
import functools

import jax
import jax.numpy as jnp
from jax import lax
from jax.experimental import pallas as pl
from jax.experimental.pallas import tpu as pltpu

F32 = jnp.float32
MXU = jnp.bfloat16
ACT = jnp.bfloat16
HI = lax.Precision.HIGHEST
EPS = 1e-6

D = 1024
HID = 4096
XBC = 1536
NPAIR = 8
Q = 128
SGG = 8
CK = 31
SK = 5
T = 256
LANE = 128
VMEM_LIMIT = 56 * 1024 * 1024

ADAM_LR, ADAM_B1, ADAM_B2, ADAM_EPS, ADAM_WD, ADAM_STEP = 0.001, 0.9, 0.999, 1e-08, 0.01, 10


def _call(body, name, grid, in_specs, out_specs, out_shape, scratch=()):
    return pl.pallas_call(
        body, name=name, grid=grid, in_specs=in_specs, out_specs=out_specs, out_shape=out_shape,
        scratch_shapes=list(scratch),
        compiler_params=pltpu.CompilerParams(dimension_semantics=("arbitrary",) * len(grid),
                                             vmem_limit_bytes=VMEM_LIMIT))


def _sds(shape, dt=F32):
    return jax.ShapeDtypeStruct(tuple(shape), dt)


def _rows(t, w, off=0, lane_blk=0):
    return pl.BlockSpec((t, w), lambda i: (i + off, lane_blk))


def _rows_lat(t, w, nct):
    return pl.BlockSpec((t, w), lambda i: (jnp.maximum(i - nct, 0), 0))


def _full(*shape):
    return pl.BlockSpec(shape, lambda *_: (0,) * len(shape))


def _wfull(*shape):
    return pl.BlockSpec(shape, lambda *_: (0,) * len(shape), pipeline_mode=pl.Buffered(1))


def _halo(w, hb, nrows):
    r, nb = T // hb, nrows // hb
    prev = pl.BlockSpec((hb, w), lambda i: (jnp.maximum(i * r - 1, 0), 0))
    nxt = pl.BlockSpec((hb, w), lambda i: (jnp.minimum((i + 1) * r, nb - 1), 0))
    return prev, nxt


def _nn(a, b):
    return jnp.dot(a.astype(MXU), b.astype(MXU), preferred_element_type=F32)


def _nt(a, b):
    return lax.dot_general(a.astype(MXU), b.astype(MXU), (((1,), (1,)), ((), ())), preferred_element_type=F32)


def _tn_dot(a, b):
    return lax.dot_general(a.astype(MXU), b.astype(MXU), (((0,), (0,)), ((), ())), preferred_element_type=F32)


def _sum0(x):
    return jnp.sum(x, axis=0, keepdims=True)


def _silu(x):
    return x * jax.nn.sigmoid(x)


def _gelu(x):
    return jax.nn.gelu(x, approximate=True)


def _mod(h, g, sh, sc):
    n = h * lax.rsqrt(jnp.mean(h * h, axis=-1, keepdims=True) + EPS)
    return n * g * (1.0 + sc) + sh


def _ln(x, g, b):
    xc = x - jnp.mean(x, axis=-1, keepdims=True)
    return xc * lax.rsqrt(jnp.mean(xc * xc, axis=-1, keepdims=True) + EPS) * g + b


def _gate_norm(ytot, z, ng):
    yg = ytot * _silu(z)
    halves = []
    for k in range(2):
        seg = yg[:, k * 512:(k + 1) * 512]
        halves.append(seg * lax.rsqrt(jnp.mean(seg * seg, axis=-1, keepdims=True) + EPS) * ng[:, k * 512:(k + 1) * 512])
    return jnp.concatenate(halves, axis=-1)


def _fill_ext(ext_ref, prev_ref, cur_ref, next_ref, hb, first, last):
    ext_ref[0:hb, :] = jnp.where(first, 0.0, prev_ref[...])
    ext_ref[hb:hb + T, :] = cur_ref[...]
    ext_ref[hb + T:hb + T + hb, :] = jnp.where(last, 0.0, next_ref[...])


def _conv(ext_ref, w_ref, k_taps, hb, lanes):
    off = hb - k_taps // 2
    acc = ext_ref[pl.ds(off, T), lanes] * w_ref[0:1, lanes]
    for k in range(1, k_taps):
        acc = acc + ext_ref[pl.ds(off + k, T), lanes] * w_ref[k:k + 1, lanes]
    return acc


def _conv_tr(ext_ref, w_ref, k_taps, hb, lanes):
    off = hb + k_taps // 2
    acc = ext_ref[pl.ds(off, T), lanes] * w_ref[0:1, lanes]
    for k in range(1, k_taps):
        acc = acc + ext_ref[pl.ds(off - k, T), lanes] * w_ref[k:k + 1, lanes]
    return acc


def _conv_wide(ext_ref, w_ref, k_taps, hb, lanes, flip=False):
    base = hb - k_taps // 2
    acc = None
    for b in range(8):
        taps = [k for k in range(k_taps) if (base + k) % 8 == b]
        if not taps:
            continue
        p = None
        for k in taps:
            wi = (k_taps - 1 - k) if flip else k
            term = ext_ref[pl.ds(base + k - b, T + 8), lanes] * w_ref[wi:wi + 1, lanes]
            p = term if p is None else p + term
        acc = p[b:b + T] if acc is None else acc + p[b:b + T]
    return acc


def _conv_dw_wide(dw_ref, d_ref, xext_ref, k_taps, hb, lanes):
    base = hb - k_taps // 2
    d = d_ref[:, lanes]
    for b in range(8):
        taps = [k for k in range(k_taps) if (base + k) % 8 == b]
        if not taps:
            continue
        lo_off = base + taps[0] - b
        span = base + taps[-1] - b - lo_off
        xs = xext_ref[pl.ds(lo_off + b, T + span), lanes]
        for k in taps:
            a = base + k - b - lo_off
            dw_ref[k:k + 1, lanes] += _sum0(d * xs[a:a + T])


def _conv_dw(dw_ref, d_ref, xext_ref, k_taps, hb, lanes):
    off = hb - k_taps // 2
    d = d_ref[:, lanes]
    for k in range(k_taps):
        dw_ref[k:k + 1, lanes] += _sum0(d * xext_ref[pl.ds(off + k, T), lanes])


def _tn(a, b, name, relu2=False, shard=None):
    m_rows, ka = a.shape
    n = b.shape[1]
    tm = next(t for t in (1024, 768, 512, 256) if m_rows % t == 0)
    tk = min(ka, 1024)
    tn = n if n <= 1024 else next(t for t in (1024, 768, 512, 384, 256, 128) if n % t == 0)
    if shard is not None and shard[0] == "col":
        tn = shard[1]
    if shard is not None and shard[0] == "row":
        tk = shard[1]
    n_m = m_rows // tm

    def body(a_ref, b_ref, o_ref, *acc):
        acc_ref = acc[0] if acc else o_ref

        @pl.when(pl.program_id(2) == 0)
        def _():
            acc_ref[...] = jnp.zeros_like(acc_ref)
        av = a_ref[...]
        if relu2:
            av = jnp.square(jnp.maximum(av.astype(F32), 0.0))
        acc_ref[...] += _tn_dot(av, b_ref[...])
        if acc:
            @pl.when(pl.program_id(2) == n_m - 1)
            def _():
                o_ref[...] = acc_ref[...].astype(o_ref.dtype)

    in_specs = [pl.BlockSpec((tm, tk), lambda k, j, m: (m, k)), pl.BlockSpec((tm, tn), lambda k, j, m: (m, j))]
    if shard is None:
        return _call(body, name, (ka // tk, n // tn, n_m), in_specs,
                     pl.BlockSpec((tk, tn), lambda k, j, m: (k, j)), _sds((ka, n)))(a, b)
    if shard[0] == "col":
        out_spec, out_shape = pl.BlockSpec((None, tk, tn), lambda k, j, m: (j, k, 0)), _sds((n // tn, ka, tn), jnp.bfloat16)
    else:
        out_spec, out_shape = pl.BlockSpec((None, tk, tn), lambda k, j, m: (k, 0, j)), _sds((ka // tk, tk, n), jnp.bfloat16)
    return _call(body, name, (ka // tk, n // tn, n_m), in_specs, out_spec, out_shape,
                 scratch=[pltpu.VMEM((tk, tn), F32)])(a, b)


def _mlp_fwd(h, mp, wpack, layer, name):
    n_rows = h.shape[0]

    def body(h_ref, mp_ref, w1_ref, w2_ref, hn_ref, a_ref, y_ref):
        hv = h_ref[...]
        u = _mod(hv, mp_ref[0:1], mp_ref[1:2], mp_ref[2:3]).astype(MXU)
        acc = jnp.zeros((T, D), F32)
        for j in range(HID // 1024):
            cs = slice(j * 1024, (j + 1) * 1024)
            a = jnp.dot(u, w1_ref[j], preferred_element_type=F32)
            a_ref[:, cs] = a.astype(ACT)
            acc = acc + jnp.dot(jnp.square(jnp.maximum(a, 0.0)).astype(MXU), w2_ref[j], preferred_element_type=F32)
        y_ref[...] = acc
        hn_ref[...] = hv + mp_ref[3:4] * acc

    return _call(body, name, (n_rows // T,),
                 [_rows(T, D), _full(8, D), _mlp_wspec(layer), _mlp_wspec(2 + layer)],
                 [_rows(T, D), _rows(T, HID), _rows(T, D)],
                 [_sds((n_rows, D)), _sds((n_rows, HID), ACT), _sds((n_rows, D))])(h, mp, wpack, wpack)


def _mlp_wspec(row_block):
    return pl.BlockSpec((4, 1024, 1024), lambda i: (0, row_block, 0), pipeline_mode=pl.Buffered(1))


def _mlp_bwd(dh, h, a, y, mp, wpack, layer, name):
    n_rows = h.shape[0]

    def body(dh_ref, h_ref, a_ref, y_ref, mp_ref, w1_ref, w2_ref, dho_ref, da_ref, dyb_ref, ub_ref, pg_ref):
        dhp = dh_ref[...]
        u, vjp = jax.vjp(_mod, h_ref[...], mp_ref[0:1], mp_ref[1:2], mp_ref[2:3])
        ub_ref[...] = u.astype(ACT)
        dyb = (mp_ref[3:4] * dhp).astype(MXU)
        dyb_ref[...] = dyb.astype(ACT)
        du = jnp.zeros((T, D), F32)
        for j in range(HID // 1024):
            cs = slice(j * 1024, (j + 1) * 1024)
            dp = _nt(dyb, w2_ref[j])
            da = dp * 2.0 * jnp.maximum(a_ref[:, cs].astype(F32), 0.0)
            da_ref[:, cs] = da.astype(ACT)
            du = du + _nt(da, w1_ref[j])
        dhn, dg, dsh, dsc = vjp(du)
        dho_ref[...] = dhp + dhn

        @pl.when(pl.program_id(0) == 0)
        def _():
            pg_ref[...] = jnp.zeros_like(pg_ref)
        pg_ref[0:1] += dg
        pg_ref[1:2] += dsh
        pg_ref[2:3] += dsc
        pg_ref[3:4] += _sum0(dhp * y_ref[...])

    return _call(body, name, (n_rows // T,),
                 [_rows(T, D), _rows(T, D), _rows(T, HID), _rows(T, D), _full(8, D), _mlp_wspec(layer), _mlp_wspec(2 + layer)],
                 [_rows(T, D), _rows(T, HID), _rows(T, D), _rows(T, D), _full(8, D)],
                 [_sds((n_rows, D)), _sds((n_rows, HID), ACT), _sds((n_rows, D), ACT), _sds((n_rows, D), ACT),
                  _sds((8, D))])(dh, h, a, y, mp, wpack, wpack)


def _cf1_fwd(h, mp, w1, b1):
    n_rows = h.shape[0]

    def body(h_ref, mp_ref, w1_ref, b1_ref, glu_ref, a_ref):
        u = _mod(h_ref[...], mp_ref[0:1], mp_ref[1:2], mp_ref[2:3]).astype(MXU)
        a = jnp.dot(u, w1_ref[...], preferred_element_type=F32) + b1_ref[...]
        a_ref[...] = a.astype(ACT)
        glu_ref[...] = a[:, :D] * jax.nn.sigmoid(a[:, D:])

    return _call(body, "cf1_fwd", (n_rows // T,),
                 [_rows(T, D), _full(8, D), _wfull(D, 2 * D), _full(1, 2 * D)],
                 [_rows(T, D), _rows(T, 2 * D)],
                 [_sds((n_rows, D)), _sds((n_rows, 2 * D), ACT)])(h, mp, w1, b1)


def _cf2_fwd(h, glu, mp, wdw, bdw, lng, lnb, w2, b2):
    n_rows = h.shape[0]
    nt = n_rows // T
    hb = 16

    def body(h_ref, gp_ref, gc_ref, gn_ref, mp_ref, wdw_ref, bdw_ref, lng_ref, lnb_ref, w2_ref, b2_ref,
             hn_ref, cv_ref, sb_ref, y_ref, ext):
        i = pl.program_id(0)
        _fill_ext(ext, gp_ref, gc_ref, gn_ref, hb, i == 0, i == nt - 1)
        for c in range(D // LANE):
            lanes = slice(c * LANE, (c + 1) * LANE)
            cv_ref[:, lanes] = _conv_wide(ext, wdw_ref, CK, hb, lanes) + bdw_ref[:, lanes]
        s = _silu(_ln(cv_ref[...], lng_ref[...], lnb_ref[...])).astype(MXU)
        sb_ref[...] = s.astype(ACT)
        y = jnp.dot(s, w2_ref[...], preferred_element_type=F32) + b2_ref[...]
        y_ref[...] = y
        hn_ref[...] = h_ref[...] + mp_ref[3:4] * y

    gp, gn = _halo(D, hb, n_rows)
    return _call(body, "cf2_fwd", (nt,),
                 [_rows(T, D), gp, _rows(T, D), gn, _full(8, D), _full(32, D), _full(1, D), _full(1, D), _full(1, D),
                  _wfull(D, D), _full(1, D)],
                 [_rows(T, D), _rows(T, D), _rows(T, D), _rows(T, D)],
                 [_sds((n_rows, D)), _sds((n_rows, D)), _sds((n_rows, D), ACT), _sds((n_rows, D))],
                 scratch=[pltpu.VMEM((T + 2 * hb, D), F32)])(h, glu, glu, glu, mp, wdw, bdw, lng, lnb, w2, b2)


def _cf2_bwd(dh, y, cv, mp, lng, lnb, w2):
    n_rows = dh.shape[0]

    def body(dh_ref, y_ref, cv_ref, mp_ref, lng_ref, lnb_ref, w2_ref, dcv_ref, dyb_ref, pg_ref):
        dhp = dh_ref[...]
        dy = mp_ref[3:4] * dhp
        dyb = dy.astype(MXU)
        dyb_ref[...] = dyb.astype(ACT)
        ds = _nt(dyb, w2_ref[...])
        _, vjp = jax.vjp(lambda cv_, g_, b_: _silu(_ln(cv_, g_, b_)), cv_ref[...], lng_ref[...], lnb_ref[...])
        dcv, dlng, dlnb = vjp(ds)
        dcv_ref[...] = dcv

        @pl.when(pl.program_id(0) == 0)
        def _():
            pg_ref[...] = jnp.zeros_like(pg_ref)
        pg_ref[0:1] += _sum0(dhp * y_ref[...])
        pg_ref[1:2] += _sum0(dy)
        pg_ref[2:3] += dlng
        pg_ref[3:4] += dlnb
        pg_ref[4:5] += _sum0(dcv)

    return _call(body, "cf2_bwd", (n_rows // T,),
                 [_rows(T, D), _rows(T, D), _rows(T, D), _full(8, D), _full(1, D), _full(1, D), _wfull(D, D)],
                 [_rows(T, D), _rows(T, D), _full(8, D)],
                 [_sds((n_rows, D)), _sds((n_rows, D), ACT), _sds((8, D))])(dh, y, cv, mp, lng, lnb, w2)


def _cf1_bwd(dh, h, a, dcv, glu, mp, wdw, w1):
    n_rows = h.shape[0]
    nt = n_rows // T
    hb = 16

    def body(dh_ref, h_ref, a_ref, dp_ref, dc_ref, dn_ref, gp_ref, gc_ref, gn_ref, mp_ref, wdw_ref, w1_ref,
             dho_ref, da_ref, ub_ref, pg_ref, pb_ref, dw_ref, dext, gext, dglu):
        i = pl.program_id(0)

        @pl.when(i == 0)
        def _():
            pg_ref[...] = jnp.zeros_like(pg_ref)
            pb_ref[...] = jnp.zeros_like(pb_ref)
            dw_ref[...] = jnp.zeros_like(dw_ref)
        _fill_ext(dext, dp_ref, dc_ref, dn_ref, hb, i == 0, i == nt - 1)
        _fill_ext(gext, gp_ref, gc_ref, gn_ref, hb, i == 0, i == nt - 1)
        for c in range(D // LANE):
            lanes = slice(c * LANE, (c + 1) * LANE)
            dglu[:, lanes] = _conv_wide(dext, wdw_ref, CK, hb, lanes, flip=True)
            _conv_dw_wide(dw_ref, dc_ref, gext, CK, hb, lanes)
        av = a_ref[...].astype(F32)
        _, vjp_glu = jax.vjp(lambda a1, a2: a1 * jax.nn.sigmoid(a2), av[:, :D], av[:, D:])
        da1, da2 = vjp_glu(dglu[...])
        da_ref[:, :D] = da1.astype(ACT)
        da_ref[:, D:] = da2.astype(ACT)
        pb_ref[0:1, :D] += _sum0(da1)
        pb_ref[0:1, D:] += _sum0(da2)
        du = _nt(da1, w1_ref[:, :D]) + _nt(da2, w1_ref[:, D:])
        u, vjp = jax.vjp(_mod, h_ref[...], mp_ref[0:1], mp_ref[1:2], mp_ref[2:3])
        ub_ref[...] = u.astype(ACT)
        dhn, dg, dsh, dsc = vjp(du)
        dho_ref[...] = dh_ref[...] + dhn
        pg_ref[0:1] += dg
        pg_ref[1:2] += dsh
        pg_ref[2:3] += dsc

    hp, hn = _halo(D, hb, n_rows)
    return _call(body, "cf1_bwd", (nt,),
                 [_rows(T, D), _rows(T, D), _rows(T, 2 * D), hp, _rows(T, D), hn, hp, _rows(T, D), hn,
                  _full(8, D), _full(32, D), _wfull(D, 2 * D)],
                 [_rows(T, D), _rows(T, 2 * D), _rows(T, D), _full(8, D), _full(8, 2 * D), _full(32, D)],
                 [_sds((n_rows, D)), _sds((n_rows, 2 * D), ACT), _sds((n_rows, D), ACT), _sds((8, D)),
                  _sds((8, 2 * D)), _sds((32, D))],
                 scratch=[pltpu.VMEM((T + 2 * hb, D), F32), pltpu.VMEM((T + 2 * hb, D), F32), pltpu.VMEM((T, D), F32)],
                 )(dh, h, a, dcv, dcv, dcv, glu, glu, glu, mp, wdw, w1)


def _sg_blocks():
    return [(c, g, slice(c * Q, (c + 1) * Q), slice(g * LANE, (g + 1) * LANE)) for c in range(T // Q) for g in range(SGG)]


IN_W = D + XBC + 32 + 2 * D
IN_LOC = IN_W // 4


def _win_split(shards):
    o1, o2, o3 = D, D + XBC, D + XBC + 32
    tr = 256

    def cols(s_ref, lo, hi):
        parts = []
        for j in range(4):
            a, b = max(lo, j * IN_LOC), min(hi, (j + 1) * IN_LOC)
            if a < b:
                parts.append(s_ref[j][:, a - j * IN_LOC:b - j * IN_LOC])
        return parts[0] if len(parts) == 1 else jnp.concatenate(parts, axis=1)

    def body(s_ref, wz_ref, wxbc_ref, wdt_ref, wuv_ref):
        wz_ref[...] = cols(s_ref, 0, o1)
        wxbc_ref[...] = cols(s_ref, o1, o2)
        dt = cols(s_ref, o2, o3)
        wdt_ref[...] = jnp.concatenate([dt, jnp.zeros((tr, LANE - 32), dt.dtype)], axis=1)
        wuv_ref[...] = cols(s_ref, o3, IN_W)

    dt_ = shards.dtype
    return _call(body, "win_split", (D // tr,), [pl.BlockSpec((4, tr, IN_LOC), lambda i: (0, i, 0))],
                 [_rows(tr, D), _rows(tr, XBC), _rows(tr, LANE), _rows(tr, 2 * D)],
                 [_sds((D, D), dt_), _sds((D, XBC), dt_), _sds((D, LANE), dt_), _sds((D, 2 * D), dt_)])(shards)


def _win_join(gz, gxbc, gdt, guv):
    tr = 256
    bounds = (0, D, D + XBC, D + XBC + 32, IN_W)

    def body(gz_ref, gx_ref, gd_ref, gu_ref, o_ref):
        segs = (gz_ref, gx_ref, gd_ref, gu_ref)
        for j in range(4):
            parts = []
            for k in range(4):
                a, b = max(bounds[k], j * IN_LOC), min(bounds[k + 1], (j + 1) * IN_LOC)
                if a < b:
                    parts.append(segs[k][:, a - bounds[k]:b - bounds[k]])
            full = parts[0] if len(parts) == 1 else jnp.concatenate(parts, axis=1)
            o_ref[j] = full.astype(jnp.bfloat16)

    return _call(body, "win_join", (D // tr,), [_rows(tr, D), _rows(tr, XBC), _rows(tr, LANE), _rows(tr, 2 * D)],
                 pl.BlockSpec((4, tr, IN_LOC), lambda i: (0, i, 0)), _sds((4, D, IN_LOC), jnp.bfloat16))(gz, gxbc, gdt, guv)


def _ctx_spec(nct):
    return pl.BlockSpec((T, D), lambda i: (jnp.minimum(i, nct - 1), 0))


def _hy1_fwd(ctx, x, mp2, wz, wuv, wxbc, wdt, lng, lnb, sgw, sgbt, nct):
    n_lat = x.shape[0]
    n_rows = ctx.shape[0] + n_lat

    def body(c_ref, x_ref, mp_ref, wz_ref, wuv_ref, wxbc_ref, wdt_ref, lng_ref, lnb_ref, sgw_ref, sgbt_ref,
             z_ref, uv_ref, xbcp_ref, dtr_ref, ysg_ref):
        hv = jnp.where(pl.program_id(0) < nct, c_ref[...], x_ref[...])
        u = _mod(hv, mp_ref[0:1], mp_ref[1:2], mp_ref[2:3]).astype(MXU)
        z_ref[...] = jnp.dot(u, wz_ref[...], preferred_element_type=F32)
        xbcp_ref[...] = jnp.dot(u, wxbc_ref[...], preferred_element_type=F32)
        dtr_ref[...] = jnp.dot(u, wdt_ref[...], preferred_element_type=F32)
        uv = jnp.dot(u, wuv_ref[...], preferred_element_type=F32)
        uv_ref[...] = uv
        gate = _gelu(uv[:, :D])
        vln = _ln(_gelu(uv[:, D:]), lng_ref[...], lnb_ref[...]).astype(MXU)
        for _, g, rs, ls in _sg_blocks():
            s = jnp.dot(sgw_ref[g], vln[rs, ls], preferred_element_type=F32) + sgbt_ref[:, g:g + 1]
            ysg_ref[rs, ls] = (gate[rs, ls] * s).astype(ACT)

    mspec = pl.BlockSpec((None, 8, D), lambda i: (jnp.where(i < nct, 0, 1), 0, 0))
    return _call(body, "hy1_fwd", (n_rows // T,),
                 [_ctx_spec(nct), _rows_lat(T, D, nct), mspec, _wfull(D, D), _wfull(D, 2 * D), _wfull(D, XBC), _wfull(D, LANE),
                  _full(1, D), _full(1, D), _full(SGG, Q, Q), _full(Q, LANE)],
                 [_rows(T, D), _rows(T, 2 * D), _rows(T, XBC), _rows(T, LANE), _rows_lat(T, D, nct)],
                 [_sds((n_rows, D)), _sds((n_rows, 2 * D)), _sds((n_rows, XBC)), _sds((n_rows, LANE)),
                  _sds((n_lat, D), ACT)])(ctx, x, mp2, wz, wuv, wxbc, wdt, lng, lnb, sgw, sgbt)


def _hy1_bwd(ctx, x, uv, dz, dxbcp, ddf, ddb, dysg, dres, mp2, wz, wuv, wxbc, wdt, lng, lnb, sgw, sgbt, nct):
    n_lat = dres.shape[0]
    n_rows = ctx.shape[0] + n_lat

    def body(c_ref, x_ref, uv_ref, dz_ref, dxbcp_ref, ddf_ref, ddb_ref, dysg_ref, dres_ref, mp_ref, wz_ref, wuv_ref,
             wxbc_ref, wdt_ref, lng_ref, lnb_ref, sgw_ref, sgbt_ref,
             dho_ref, ub_ref, duv_ref, ddt_ref, pg2_ref, pl_ref, dsgw_ref, dsgb_ref, dgate_s, dvln_s):
        i = pl.program_id(0)

        @pl.when(i == 0)
        def _():
            pg2_ref[...] = jnp.zeros_like(pg2_ref)
            pl_ref[...] = jnp.zeros_like(pl_ref)
            dsgw_ref[...] = jnp.zeros_like(dsgw_ref)
            dsgb_ref[...] = jnp.zeros_like(dsgb_ref)
        uv = uv_ref[...]

        def f_sg(ug, uvv, g_, b_):
            return _gelu(ug), _ln(_gelu(uvv), g_, b_)
        (gate, vln), vjp_sg = jax.vjp(f_sg, uv[:, :D], uv[:, D:], lng_ref[...], lnb_ref[...])
        vlnb = vln.astype(MXU)
        lane = lax.broadcasted_iota(jnp.int32, (Q, LANE), 1)
        dsgb = jnp.zeros((Q, LANE), F32)
        for _, g, rs, ls in _sg_blocks():
            s = jnp.dot(sgw_ref[g], vlnb[rs, ls], preferred_element_type=F32) + sgbt_ref[:, g:g + 1]
            dyb = dysg_ref[rs, ls]
            dgate_s[rs, ls] = dyb * s
            ds = dyb * gate[rs, ls]
            dvln_s[rs, ls] = _tn_dot(sgw_ref[g], ds)
            dsgw_ref[g] += _nt(ds, vlnb[rs, ls])
            dsgb = dsgb + jnp.where(lane == g, jnp.sum(ds, axis=1, keepdims=True), 0.0)
        dsgb_ref[...] += dsgb
        dug, duvv, dlng, dlnb = vjp_sg((dgate_s[...], dvln_s[...]))
        pl_ref[0:1] += dlng
        pl_ref[1:2] += dlnb
        duv_ref[:, :D] = dug.astype(ACT)
        duv_ref[:, D:] = duvv.astype(ACT)
        ddt = (ddf_ref[...] + ddb_ref[...]).astype(MXU)
        ddt_ref[...] = ddt.astype(ACT)
        du = (_nt(dz_ref[...], wz_ref[...]) + _nt(dug, wuv_ref[:, :D]) + _nt(duvv, wuv_ref[:, D:])
              + _nt(dxbcp_ref[...], wxbc_ref[...]) + _nt(ddt, wdt_ref[...]))
        hv = jnp.where(i < nct, c_ref[...], x_ref[...])
        u, vjp = jax.vjp(_mod, hv, mp_ref[0:1], mp_ref[1:2], mp_ref[2:3])
        ub_ref[...] = u.astype(ACT)
        dhn, dg, dsh, dsc = vjp(du)
        dho_ref[...] = dres_ref[...] + dhn
        is_ctx = i < nct
        for k, val in enumerate((dg, dsh, dsc)):
            pg2_ref[0, k:k + 1] += jnp.where(is_ctx, val, 0.0)
            pg2_ref[1, k:k + 1] += jnp.where(is_ctx, 0.0, val)

    mspec = pl.BlockSpec((None, 8, D), lambda i: (jnp.where(i < nct, 0, 1), 0, 0))
    return _call(body, "hy1_bwd", (n_rows // T,),
                 [_ctx_spec(nct), _rows_lat(T, D, nct), _rows(T, 2 * D), _rows(T, D), _rows(T, XBC), _rows(T, LANE),
                  _rows(T, LANE), _rows(T, D),
                  _rows_lat(T, D, nct), mspec, _wfull(D, D), _wfull(D, 2 * D), _wfull(D, XBC), _wfull(D, LANE),
                  _full(1, D), _full(1, D), _full(SGG, Q, Q), _full(Q, LANE)],
                 [_rows_lat(T, D, nct), _rows(T, D), _rows(T, 2 * D), _rows(T, LANE), _full(2, 8, D), _full(8, D),
                  _full(SGG, Q, Q), _full(Q, LANE)],
                 [_sds((n_lat, D)), _sds((n_rows, D), ACT), _sds((n_rows, 2 * D), ACT), _sds((n_rows, LANE), ACT),
                  _sds((2, 8, D)), _sds((8, D)), _sds((SGG, Q, Q)), _sds((Q, LANE))],
                 scratch=[pltpu.VMEM((T, D), F32), pltpu.VMEM((T, D), F32)],
                 )(ctx, x, uv, dz, dxbcp, ddf, ddb, dysg, dres, mp2, wz, wuv, wxbc, wdt, lng, lnb, sgw, sgbt)


def _seq_edges(i, nct, nt):
    return (i == 0) | (i == nct), (i == nct - 1) | (i == nt - 1)


def _cv5_fwd(xbcp, w, b, nct):
    n_rows = xbcp.shape[0]
    nt = n_rows // T
    hb = 8

    def body(p_ref, c_ref, n_ref, w_ref, b_ref, o_ref, ext):
        first, last = _seq_edges(pl.program_id(0), nct, nt)
        _fill_ext(ext, p_ref, c_ref, n_ref, hb, first, last)
        for c in range(XBC // LANE):
            lanes = slice(c * LANE, (c + 1) * LANE)
            o_ref[:, lanes] = _silu(_conv(ext, w_ref, SK, hb, lanes) + b_ref[:, lanes])

    hp, hn = _halo(XBC, hb, n_rows)
    return _call(body, "cv5_fwd", (nt,), [hp, _rows(T, XBC), hn, _full(8, XBC), _full(1, XBC)],
                 _rows(T, XBC), _sds((n_rows, XBC)), scratch=[pltpu.VMEM((T + 2 * hb, XBC), F32)])(xbcp, xbcp, xbcp, w, b)


def _cv5_bwd1(xbcp, dxf, dxb, w, b, nct):
    n_rows = xbcp.shape[0]
    nt = n_rows // T
    hb = 8

    def body(p_ref, c_ref, n_ref, dxf_ref, dxb_ref, w_ref, b_ref, o_ref, pg_ref, ext):
        i = pl.program_id(0)
        first, last = _seq_edges(i, nct, nt)
        _fill_ext(ext, p_ref, c_ref, n_ref, hb, first, last)

        @pl.when(i == 0)
        def _():
            pg_ref[...] = jnp.zeros_like(pg_ref)
        for c in range(XBC // LANE):
            lanes = slice(c * LANE, (c + 1) * LANE)
            cv = _conv(ext, w_ref, SK, hb, lanes) + b_ref[:, lanes]
            sg = jax.nn.sigmoid(cv)
            dcv = (dxf_ref[:, lanes] + dxb_ref[:, lanes]) * (sg * (1.0 + cv * (1.0 - sg)))
            o_ref[:, lanes] = dcv
            pg_ref[0:1, lanes] += _sum0(dcv)

    hp, hn = _halo(XBC, hb, n_rows)
    return _call(body, "cv5_bwd1", (nt,),
                 [hp, _rows(T, XBC), hn, _rows(T, XBC), _rows(T, XBC), _full(8, XBC), _full(1, XBC)],
                 [_rows(T, XBC), _full(8, XBC)], [_sds((n_rows, XBC)), _sds((8, XBC))],
                 scratch=[pltpu.VMEM((T + 2 * hb, XBC), F32)])(xbcp, xbcp, xbcp, dxf, dxb, w, b)


def _cv5_bwd2(dcv, xbcp, w, nct):
    n_rows = xbcp.shape[0]
    nt = n_rows // T
    hb = 8

    def body(dp_ref, dc_ref, dn_ref, xp_ref, xc_ref, xn_ref, w_ref, o_ref, dw_ref, dext, xext):
        i = pl.program_id(0)
        first, last = _seq_edges(i, nct, nt)
        _fill_ext(dext, dp_ref, dc_ref, dn_ref, hb, first, last)
        _fill_ext(xext, xp_ref, xc_ref, xn_ref, hb, first, last)

        @pl.when(i == 0)
        def _():
            dw_ref[...] = jnp.zeros_like(dw_ref)
        for c in range(XBC // LANE):
            lanes = slice(c * LANE, (c + 1) * LANE)
            o_ref[:, lanes] = _conv_tr(dext, w_ref, SK, hb, lanes).astype(ACT)
            _conv_dw(dw_ref, dc_ref, xext, SK, hb, lanes)

    hp, hn = _halo(XBC, hb, n_rows)
    return _call(body, "cv5_bwd2", (nt,),
                 [hp, _rows(T, XBC), hn, hp, _rows(T, XBC), hn, _full(8, XBC)],
                 [_rows(T, XBC), _full(8, XBC)], [_sds((n_rows, XBC), ACT), _sds((8, XBC))],
                 scratch=[pltpu.VMEM((T + 2 * hb, XBC), F32), pltpu.VMEM((T + 2 * hb, XBC), F32)],
                 )(dcv, dcv, dcv, xbcp, xbcp, xbcp, w)


def _scan_order(nc, ncc, rev):
    if not rev:
        return lambda s: s
    return lambda s: jnp.where(s < ncc, ncc - 1 - s, nc - 1 - (s - ncc))


def _ssd_prep(dtr, sp, rev):
    dt = jax.nn.softplus(dtr + sp[0:1])
    a_neg = -jnp.exp(sp[1:2])
    r = lax.broadcasted_iota(jnp.int32, (Q, Q), 0)
    c = lax.broadcasted_iota(jnp.int32, (Q, Q), 1)
    msk = (c >= r) if rev else (c <= r)
    tri = msk.astype(F32)
    acs = jnp.dot(tri, dt * a_neg, precision=HI, preferred_element_type=F32)
    last = 0 if rev else Q - 1
    return dt, a_neg, acs, msk, tri, last


def _pair_sel(arr, lo, m, lane_lt):
    h0 = lo + 2 * m
    return jnp.where(lane_lt, arr[:, h0:h0 + 1], arr[:, h0 + 1:h0 + 2])


def _head_lanes(row, lo, g):
    lane = lax.broadcasted_iota(jnp.int32, (1, 512), 1)
    out = jnp.zeros((1, 512), F32)
    for k in range(8):
        h = lo + 8 * g + k
        out = jnp.where((lane >= 64 * k) & (lane < 64 * (k + 1)), row[:, h:h + 1], out)
    return out


def _halves(v, lane_lt):
    return jnp.concatenate([jnp.where(lane_lt, v, 0.0), jnp.where(lane_lt, 0.0, v)], axis=0)


def _ssd_fwd(xbc, dtr, sp, ncc, rev):
    n_rows = xbc.shape[0]
    nc = n_rows // Q
    lo = 16 if rev else 0
    order = _scan_order(nc, ncc, rev)

    def body(x_ref, dtr_ref, sp_ref, y_ref, hin_ref, st):
        @pl.when(pl.program_id(0) == 0)
        def _():
            st[...] = jnp.zeros_like(st)
        dt, _, acs, msk, _, last = _ssd_prep(dtr_ref[...], sp_ref[...], rev)
        acs_t, dt_t = acs.T, dt.T
        eacs = jnp.exp(acs)
        eal = jnp.exp(acs[last:last + 1, :])
        tew = jnp.exp(acs[last:last + 1, :] - acs) * dt
        lane_lt = lax.broadcasted_iota(jnp.int32, (Q, LANE), 1) < 64
        for g in range(2):
            gl = slice(g * 512, (g + 1) * 512)
            bg = x_ref[:, 1024 + g * 128:1152 + g * 128]
            cg = x_ref[:, 1280 + g * 128:1408 + g * 128]
            s_g = _nt(cg, bg)
            h_t = st[:, gl]
            hin_ref[:, gl] = h_t
            yoff = _nn(cg, h_t)
            xw = []
            for mm in range(4):
                m = 4 * g + mm
                ls = slice(m * LANE, (m + 1) * LANE)
                x2 = x_ref[:, ls]
                ws = []
                for hh in range(2):
                    h = lo + 2 * m + hh
                    lm = jnp.exp(jnp.where(msk, acs[:, h:h + 1] - acs_t[h:h + 1, :], -jnp.inf))
                    ws.append(s_g * lm * dt_t[h:h + 1, :])
                y2 = _nn(jnp.concatenate(ws, axis=1), _halves(x2, lane_lt))
                y_ref[:, ls] = y2 + yoff[:, mm * LANE:(mm + 1) * LANE] * _pair_sel(eacs, lo, m, lane_lt)
                xw.append(x2 * _pair_sel(tew, lo, m, lane_lt))
            st[:, gl] = _head_lanes(eal, lo, g) * h_t + _tn_dot(bg, jnp.concatenate(xw, axis=1))

    return _call(body, "ssd_fwd_r" if rev else "ssd_fwd_f", (nc,),
                 [pl.BlockSpec((Q, XBC), lambda s: (order(s), 0)), pl.BlockSpec((Q, LANE), lambda s: (order(s), 0)),
                  _full(8, LANE)],
                 [pl.BlockSpec((Q, D), lambda s: (order(s), 0)), pl.BlockSpec((None, LANE, D), lambda s: (order(s), 0, 0))],
                 [_sds((n_rows, D)), _sds((nc, LANE, D))], scratch=[pltpu.VMEM((LANE, D), F32)])(xbc, dtr, sp)


def _ssd_bwd(xbc, dtr, dy, hin, sp, dl, eh, ncc, rev):
    n_rows = xbc.shape[0]
    nc = n_rows // Q
    lo = 16 if rev else 0
    fwd_order = _scan_order(nc, ncc, rev)
    order = lambda s: fwd_order(nc - 1 - s)
    with_skip = not rev

    def body(x_ref, dtr_ref, dy_ref, hin_ref, sp_ref, dl_ref, eh_ref, dx_ref, ddtr_ref, pg_ref, dst):
        @pl.when(pl.program_id(0) == 0)
        def _():
            dst[...] = jnp.zeros_like(dst)
            pg_ref[...] = jnp.zeros_like(pg_ref)
        dtr_v = dtr_ref[...]
        dt, a_neg, acs, msk, tri, last = _ssd_prep(dtr_v, sp_ref[...], rev)
        acs_t = acs.T
        r = lax.broadcasted_iota(jnp.int32, (Q, Q), 0)
        c = lax.broadcasted_iota(jnp.int32, (Q, Q), 1)
        msk_t = (c <= r) if rev else (c >= r)
        eacs = jnp.exp(acs)
        eal = jnp.exp(acs[last:last + 1, :])
        te = jnp.exp(acs[last:last + 1, :] - acs)
        lane = lax.broadcasted_iota(jnp.int32, (Q, LANE), 1)
        lane1 = lax.broadcasted_iota(jnp.int32, (1, LANE), 1)
        lane_lt = lane < 64
        dacs = jnp.zeros((Q, LANE), F32)
        ddt_x = jnp.zeros((Q, LANE), F32)
        dlast = jnp.zeros((1, LANE), F32)
        hs_rows = []
        sub16 = lax.broadcasted_iota(jnp.int32, (16, Q), 0)
        dacs_t = jnp.zeros((16, Q), F32)
        for g in range(2):
            gl = slice(g * 512, (g + 1) * 512)
            bg = x_ref[:, 1024 + g * 128:1152 + g * 128]
            cg = x_ref[:, 1280 + g * 128:1408 + g * 128]
            s_g = _nt(cg, bg)
            s_gt = _nt(bg, cg)
            h_t, dh_t = hin_ref[:, gl], dst[:, gl]
            bh = _nn(bg, dh_t)
            yoff = _nn(cg, h_t)
            d_s = jnp.zeros((Q, Q), F32)
            edy, exd = [], []
            for mm in range(4):
                m = 4 * g + mm
                ls = slice(m * LANE, (m + 1) * LANE)
                x2, dy2 = x_ref[:, ls], dy_ref[:, ls]
                bh2 = bh[:, mm * LANE:(mm + 1) * LANE]
                dtm, em, eam = (_pair_sel(v, lo, m, lane_lt) for v in (dt, te, eacs))
                xd2 = x2 * dtm
                lms, mts = [], []
                for hh in range(2):
                    h = lo + 2 * m + hh
                    col, row = acs[:, h:h + 1], acs_t[h:h + 1, :]
                    lms.append(jnp.exp(jnp.where(msk, col - row, -jnp.inf)))
                    mts.append(s_gt * jnp.exp(jnp.where(msk_t, row - col, -jnp.inf)))
                dy_st = _halves(dy2, lane_lt)
                dxd2 = em * bh2 + _nn(jnp.concatenate(mts, axis=1), dy_st)
                dm_st = _nt(dy_st, xd2)
                dmt_st = _nt(_halves(xd2, lane_lt), dy2)
                d_s = d_s + dm_st[:Q] * lms[0] + dm_st[Q:] * lms[1]
                v1, v2, v3 = dy2 * yoff[:, mm * LANE:(mm + 1) * LANE] * eam, dxd2 * x2, xd2 * bh2 * em
                for hh in range(2):
                    h = lo + 2 * m + hh
                    half = lane_lt == (hh == 0)
                    g_rows = _sum0(dmt_st[hh * Q:(hh + 1) * Q] * mts[hh]) - _sum0(dm_st[hh * Q:(hh + 1) * Q] * s_g * lms[hh])
                    dacs_t = jnp.where(sub16 == 2 * m + hh, g_rows, dacs_t)
                    r1 = jnp.sum(jnp.where(half, v1, 0.0), axis=1, keepdims=True)
                    r2 = jnp.sum(jnp.where(half, v2, 0.0), axis=1, keepdims=True)
                    r3 = jnp.sum(jnp.where(half, v3, 0.0), axis=1, keepdims=True)
                    dacs = dacs + jnp.where(lane == h, r1 - r3, 0.0)
                    ddt_x = ddt_x + jnp.where(lane == h, r2, 0.0)
                    dlast = dlast + jnp.where(lane1 == h, _sum0(r3), 0.0)
                dx2 = dxd2 * dtm
                if with_skip:
                    dx2 = dx2 + dl_ref[:, ls] * dy2
                dx_ref[:, ls] = dx2
                edy.append(eam * dy2)
                exd.append(em * xd2)
            edy, exd = jnp.concatenate(edy, axis=1), jnp.concatenate(exd, axis=1)
            hs_rows.append(_sum0(h_t * dh_t))
            dst[:, gl] = _head_lanes(eal, lo, g) * dh_t + _tn_dot(cg, edy)
            dx_ref[:, 1024 + g * 128:1152 + g * 128] = _tn_dot(d_s, cg) + _nt(exd, dh_t)
            dx_ref[:, 1280 + g * 128:1408 + g * 128] = _nn(d_s, bg) + _nt(edy, h_t)
        hs = jnp.broadcast_to(jnp.concatenate(hs_rows, axis=1), (8, D))
        hsum = jnp.dot(hs, eh_ref[...], precision=HI, preferred_element_type=F32)[0:1]
        dlast = dlast + eal * hsum
        dacs = dacs + jnp.concatenate([jnp.zeros((lo, Q), F32)] * (lo > 0) + [dacs_t, jnp.zeros((LANE - 16 - lo, Q), F32)],
                                      axis=0).T
        rowi = lax.broadcasted_iota(jnp.int32, (Q, LANE), 0)
        dacs = dacs + jnp.where(rowi == last, dlast, 0.0)
        da = lax.dot_general(tri, dacs, (((0,), (0,)), ((), ())), precision=HI, preferred_element_type=F32)
        ddt = ddt_x + da * a_neg
        mine = (lane >= lo) & (lane < lo + 16)
        ddtr = jnp.where(mine, ddt * jax.nn.sigmoid(dtr_v + sp_ref[0:1]), 0.0)
        ddtr_ref[...] = ddtr
        pg_ref[0:1] += _sum0(ddtr)
        pg_ref[1:2] += jnp.where(mine[0:1], _sum0(da * dt) * a_neg, 0.0)

    blk = lambda w_: pl.BlockSpec((Q, w_), lambda s: (order(s), 0))
    return _call(body, "ssd_bwd_r" if rev else "ssd_bwd_f", (nc,),
                 [blk(XBC), blk(LANE), blk(D), pl.BlockSpec((None, LANE, D), lambda s: (order(s), 0, 0)),
                  _full(8, LANE), _full(1, D), _full(D, LANE)],
                 [blk(XBC), blk(LANE), _full(8, LANE)],
                 [_sds((n_rows, XBC)), _sds((n_rows, LANE)), _sds((8, LANE))],
                 scratch=[pltpu.VMEM((LANE, D), F32)])(xbc, dtr, dy, hin, sp, dl, eh)


def _ssd_fwd_old(xbc, dtr, sp, ncc, rev):
    n_rows = xbc.shape[0]
    nc = n_rows // Q
    lo = 16 if rev else 0
    order = _scan_order(nc, ncc, rev)

    def body(x_ref, dtr_ref, sp_ref, y_ref, hin_ref, st):
        @pl.when(pl.program_id(0) == 0)
        def _():
            st[...] = jnp.zeros_like(st)
        dt, _, acs, msk, _, last = _ssd_prep(dtr_ref[...], sp_ref[...], rev)
        acs_t, dt_t = acs.T, dt.T
        eacs = jnp.exp(acs)
        eal = jnp.exp(acs[last:last + 1, :])
        tew = jnp.exp(acs[last:last + 1, :] - acs) * dt
        lane_lt = lax.broadcasted_iota(jnp.int32, (Q, LANE), 1) < 64
        row_lt = lax.broadcasted_iota(jnp.int32, (LANE, 1), 0) < 64
        s_g = [_nt(x_ref[:, 1280 + g * 128:1408 + g * 128], x_ref[:, 1024 + g * 128:1152 + g * 128]) for g in range(2)]
        for m in range(NPAIR):
            g = m // 4
            ls = slice(m * LANE, (m + 1) * LANE)
            x2 = x_ref[:, ls]
            bg = x_ref[:, 1024 + g * 128:1152 + g * 128]
            cg = x_ref[:, 1280 + g * 128:1408 + g * 128]
            y2 = jnp.zeros((Q, LANE), F32)
            for hh in range(2):
                h = lo + 2 * m + hh
                lm = jnp.exp(jnp.where(msk, acs[:, h:h + 1] - acs_t[h:h + 1, :], -jnp.inf))
                w = s_g[g] * lm * dt_t[h:h + 1, :]
                y2 = y2 + _nn(w, jnp.where(lane_lt == (hh == 0), x2, 0.0))
            hp = st[ls, :]
            hin_ref[ls, :] = hp
            y_ref[:, ls] = y2 + _nt(cg, hp) * _pair_sel(eacs, lo, m, lane_lt)
            snew = _tn_dot(x2 * _pair_sel(tew, lo, m, lane_lt), bg)
            h0 = lo + 2 * m
            st[ls, :] = jnp.where(row_lt, eal[:, h0:h0 + 1], eal[:, h0 + 1:h0 + 2]) * hp + snew

    return _call(body, "ssd_fwd_r" if rev else "ssd_fwd_f", (nc,),
                 [pl.BlockSpec((Q, XBC), lambda s: (order(s), 0)), pl.BlockSpec((Q, LANE), lambda s: (order(s), 0)),
                  _full(8, LANE)],
                 [pl.BlockSpec((Q, D), lambda s: (order(s), 0)), pl.BlockSpec((None, D, LANE), lambda s: (order(s), 0, 0))],
                 [_sds((n_rows, D)), _sds((nc, D, LANE))], scratch=[pltpu.VMEM((D, LANE), F32)])(xbc, dtr, sp)


def _ssd_bwd_old(xbc, dtr, dy, hin, sp, dl, ncc, rev):
    n_rows = xbc.shape[0]
    nc = n_rows // Q
    lo = 16 if rev else 0
    fwd_order = _scan_order(nc, ncc, rev)
    order = lambda s: fwd_order(nc - 1 - s)
    with_skip = not rev

    def body(x_ref, dtr_ref, dy_ref, hin_ref, sp_ref, dl_ref, dx_ref, ddtr_ref, pg_ref, dst):
        @pl.when(pl.program_id(0) == 0)
        def _():
            dst[...] = jnp.zeros_like(dst)
            pg_ref[...] = jnp.zeros_like(pg_ref)
        dtr_v = dtr_ref[...]
        dt, a_neg, acs, msk, tri, last = _ssd_prep(dtr_v, sp_ref[...], rev)
        acs_t, dt_t = acs.T, dt.T
        eacs = jnp.exp(acs)
        eal = jnp.exp(acs[last:last + 1, :])
        te = jnp.exp(acs[last:last + 1, :] - acs)
        lane = lax.broadcasted_iota(jnp.int32, (Q, LANE), 1)
        sub = lax.broadcasted_iota(jnp.int32, (LANE, Q), 0)
        lane1 = lax.broadcasted_iota(jnp.int32, (1, LANE), 1)
        lane_lt = lane < 64
        row_lt = lax.broadcasted_iota(jnp.int32, (LANE, 1), 0) < 64
        bgs = [x_ref[:, 1024 + g * 128:1152 + g * 128] for g in range(2)]
        cgs = [x_ref[:, 1280 + g * 128:1408 + g * 128] for g in range(2)]
        s_g = [_nt(cgs[g], bgs[g]) for g in range(2)]
        d_s = [jnp.zeros((Q, Q), F32), jnp.zeros((Q, Q), F32)]
        dc_x = [jnp.zeros((Q, LANE), F32), jnp.zeros((Q, LANE), F32)]
        db_x = [jnp.zeros((Q, LANE), F32), jnp.zeros((Q, LANE), F32)]
        dacs = jnp.zeros((Q, LANE), F32)
        colsum_t = jnp.zeros((LANE, Q), F32)
        ddt_x = jnp.zeros((Q, LANE), F32)
        dlast = jnp.zeros((1, LANE), F32)
        for m in range(NPAIR):
            g = m // 4
            ls = slice(m * LANE, (m + 1) * LANE)
            x2, dy2 = x_ref[:, ls], dy_ref[:, ls]
            hp, dhp = hin_ref[ls, :], dst[ls, :]
            dtm, em, eam = (_pair_sel(v, lo, m, lane_lt) for v in (dt, te, eacs))
            xd2 = x2 * dtm
            bh = _nt(bgs[g], dhp)
            ch = _nt(cgs[g], hp)
            dxd2 = em * bh
            for hh in range(2):
                h = lo + 2 * m + hh
                half = lane_lt == (hh == 0)
                lm = jnp.exp(jnp.where(msk, acs[:, h:h + 1] - acs_t[h:h + 1, :], -jnp.inf))
                mh = s_g[g] * lm
                dyh = jnp.where(half, dy2, 0.0)
                dxd2 = dxd2 + _tn_dot(mh, dyh)
                dm = _nt(dyh, jnp.where(half, xd2, 0.0))
                d_s[g] = d_s[g] + dm * lm
                gh = dm * mh
                dacs = dacs + jnp.where(lane == h, jnp.sum(gh, axis=1, keepdims=True), 0.0)
                colsum_t = colsum_t + jnp.where(sub == h, jnp.sum(gh, axis=0, keepdims=True), 0.0)
                t1 = jnp.sum(jnp.where(half, dy2 * ch * eam, 0.0), axis=1, keepdims=True)
                rj = jnp.sum(jnp.where(half, xd2 * bh * em, 0.0), axis=1, keepdims=True)
                dacs = dacs + jnp.where(lane == h, t1 - rj, 0.0)
                hs = hp * dhp
                hsum = jnp.sum(jnp.sum(jnp.where(row_lt == (hh == 0), hs, 0.0), axis=1, keepdims=True), axis=0, keepdims=True)
                dlast = dlast + jnp.where(lane1 == h, jnp.sum(rj, axis=0, keepdims=True) + eal[:, h:h + 1] * hsum, 0.0)
            for hh in range(2):
                h = lo + 2 * m + hh
                half = lane_lt == (hh == 0)
                ddt_x = ddt_x + jnp.where(lane == h, jnp.sum(jnp.where(half, dxd2 * x2, 0.0), axis=1, keepdims=True), 0.0)
            dx2 = dxd2 * dtm
            if with_skip:
                dx2 = dx2 + dl_ref[:, ls] * dy2
            dx_ref[:, ls] = dx2
            edy = eam * dy2
            dc_x[g] = dc_x[g] + _nn(edy, hp)
            db_x[g] = db_x[g] + _nn(em * xd2, dhp)
            h0 = lo + 2 * m
            dst[ls, :] = jnp.where(row_lt, eal[:, h0:h0 + 1], eal[:, h0 + 1:h0 + 2]) * dhp + _tn_dot(edy, cgs[g])
        dacs = dacs - colsum_t.T
        rowi = lax.broadcasted_iota(jnp.int32, (Q, LANE), 0)
        dacs = dacs + jnp.where(rowi == last, dlast, 0.0)
        da = lax.dot_general(tri, dacs, (((0,), (0,)), ((), ())), precision=HI, preferred_element_type=F32)
        ddt = ddt_x + da * a_neg
        mine = (lane >= lo) & (lane < lo + 16)
        ddtr = jnp.where(mine, ddt * jax.nn.sigmoid(dtr_v + sp_ref[0:1]), 0.0)
        ddtr_ref[...] = ddtr
        pg_ref[0:1] += _sum0(ddtr)
        pg_ref[1:2] += jnp.where(mine[0:1], _sum0(da * dt) * a_neg, 0.0)
        for g in range(2):
            dx_ref[:, 1024 + g * 128:1152 + g * 128] = _tn_dot(d_s[g], cgs[g]) + db_x[g]
            dx_ref[:, 1280 + g * 128:1408 + g * 128] = _nn(d_s[g], bgs[g]) + dc_x[g]

    return _call(body, "ssd_bwd_r" if rev else "ssd_bwd_f", (nc,),
                 [pl.BlockSpec((Q, XBC), lambda s: (order(s), 0)), pl.BlockSpec((Q, LANE), lambda s: (order(s), 0)),
                  pl.BlockSpec((Q, D), lambda s: (order(s), 0)), pl.BlockSpec((None, D, LANE), lambda s: (order(s), 0, 0)),
                  _full(8, LANE), _full(1, D)],
                 [pl.BlockSpec((Q, XBC), lambda s: (order(s), 0)), pl.BlockSpec((Q, LANE), lambda s: (order(s), 0)),
                  _full(8, LANE)],
                 [_sds((n_rows, XBC)), _sds((n_rows, LANE)), _sds((8, LANE))],
                 scratch=[pltpu.VMEM((D, LANE), F32)])(xbc, dtr, dy, hin, sp, dl)


def _hy4_fwd(h, yf, yb, xbc, z, ysg, mp, dl, ng, wout, nct):
    n_rows = h.shape[0]

    def body(h_ref, yf_ref, yb_ref, xs_ref, z_ref, ysg_ref, mp_ref, dl_ref, ng_ref, wout_ref, hn_ref, yssd_ref, out_ref):
        ytot = yf_ref[...] + yb_ref[...] + dl_ref[...] * xs_ref[...]
        yssd = _gate_norm(ytot, z_ref[...], ng_ref[...]).astype(MXU)
        yssd_ref[...] = yssd.astype(ACT)
        out = (jnp.dot(yssd, wout_ref[0:D, :], preferred_element_type=F32)
               + jnp.dot(ysg_ref[...].astype(MXU), wout_ref[D:2 * D, :], preferred_element_type=F32))
        out_ref[...] = out
        hn_ref[...] = h_ref[...] + mp_ref[3:4] * out

    return _call(body, "hy4_fwd", (n_rows // T,),
                 [_rows(T, D), _rows(T, D, nct), _rows(T, D, nct), _rows(T, D, nct), _rows(T, D, nct), _rows(T, D),
                  _full(8, D), _full(1, D), _full(1, D), _wfull(2 * D, D)],
                 [_rows(T, D), _rows(T, D), _rows(T, D)],
                 [_sds((n_rows, D)), _sds((n_rows, D), ACT), _sds((n_rows, D))])(h, yf, yb, xbc, z, ysg, mp, dl, ng, wout)


def _hy4_bwd(dh, out, yf, yb, xbc, z, mp, dl, ng, wout, nct):
    n_lat = dh.shape[0]
    n_rows = yf.shape[0]

    def body(dh_ref, out_ref, yf_ref, yb_ref, xs_ref, z_ref, mp_ref, dl_ref, ng_ref, wout_ref,
             dy_ref, dz_ref, dysg_ref, doutb_ref, pg_ref):
        i = pl.program_id(0)

        @pl.when(i == 0)
        def _():
            pg_ref[...] = jnp.zeros_like(pg_ref)

        @pl.when(i < nct)
        def _():
            dy_ref[...] = jnp.zeros_like(dy_ref)
            dz_ref[...] = jnp.zeros_like(dz_ref)
            dysg_ref[...] = jnp.zeros_like(dysg_ref)
            doutb_ref[...] = jnp.zeros_like(doutb_ref)

        @pl.when(i >= nct)
        def _():
            dhp = dh_ref[...]
            doutb = (mp_ref[3:4] * dhp).astype(MXU)
            doutb_ref[...] = doutb.astype(ACT)
            dysg_ref[...] = _nt(doutb, wout_ref[D:2 * D, :])
            dyssd = _nt(doutb, wout_ref[0:D, :])
            xs = xs_ref[...]
            ytot = yf_ref[...] + yb_ref[...] + dl_ref[...] * xs
            _, vjp = jax.vjp(_gate_norm, ytot, z_ref[...], ng_ref[...])
            dytot, dz, dng = vjp(dyssd)
            dy_ref[...] = dytot
            dz_ref[...] = dz.astype(ACT)
            pg_ref[0:1] += _sum0(dhp * out_ref[...])
            pg_ref[1:2] += dng
            pg_ref[2:3] += _sum0(dytot * xs)

    return _call(body, "hy4_bwd", (n_rows // T,),
                 [_rows_lat(T, D, nct), _rows_lat(T, D, nct), _rows(T, D), _rows(T, D), _rows(T, D), _rows(T, D),
                  _full(8, D), _full(1, D), _full(1, D), _wfull(2 * D, D)],
                 [_rows(T, D), _rows(T, D), _rows(T, D), _rows_lat(T, D, nct), _full(8, D)],
                 [_sds((n_rows, D)), _sds((n_rows, D), ACT), _sds((n_rows, D)), _sds((n_lat, D), ACT), _sds((8, D))],
                 )(dh, out, yf, yb, xbc, z, mp, dl, ng, wout)


def _loss_bwd(h, tgt, fng):
    n_rows = h.shape[0]

    def body(h_ref, t_ref, g_ref, dh_ref, pg_ref, ls_ref):
        @pl.when(pl.program_id(0) == 0)
        def _():
            pg_ref[...] = jnp.zeros_like(pg_ref)
            ls_ref[...] = jnp.zeros_like(ls_ref)
        hv = h_ref[...]
        g = g_ref[...]
        r = lax.rsqrt(jnp.mean(hv * hv, axis=-1, keepdims=True) + EPS)
        n = hv * r
        e = n * g - t_ref[...]
        ls_ref[...] += 0.5 * jnp.sum(jnp.sum(e * e, axis=1, keepdims=True), axis=0, keepdims=True) * (1.0 / D)
        dyv = e * (1.0 / D)
        pg_ref[0:1] += _sum0(dyv * n)
        dn = dyv * g
        dh_ref[...] = r * (dn - n * jnp.mean(dn * n, axis=-1, keepdims=True))

    return _call(body, "loss_bwd", (n_rows // T,), [_rows(T, D), _rows(T, D), _full(1, D)],
                 [_rows(T, D), _full(8, D), _full(8, LANE)],
                 [_sds((n_rows, D)), _sds((8, D)), _sds((8, LANE))])(h, tgt, fng)


def _pad_rows(a, rows):
    return jnp.concatenate([a, jnp.zeros((rows - a.shape[0],) + a.shape[1:], a.dtype)], axis=0)


def _mp(*rows):
    return _pad_rows(jnp.stack(rows, axis=0), 8)


def _local_step(x, ctx, tgt, ada, cada0, w, late_w=None, early_grads=None):
    n_lat, n_ctx = x.shape[0], ctx.shape[0]
    nct, ncc = n_ctx // T, n_ctx // Q
    a0 = [ada[0, k * D:(k + 1) * D] for k in range(6)]
    a1 = [ada[1, k * D:(k + 1) * D] for k in range(6)]
    c0 = [cada0[k * D:(k + 1) * D] for k in range(6)]
    g = {}

    mp2 = jnp.stack([_mp(w["norm_mix_g"][0], c0[0], c0[1]), _mp(w["norm_mix_g"][0], a0[0], a0[1], a0[2])], axis=0)
    mp_l0 = mp2[1]
    sgbt = _pad_cols(w["sg_b"][0].T, LANE)
    lng, lnb = w["sg_ln_g"][0][None], w["sg_ln_b"][0][None]
    z, uv, xbcp, dtr, ysg = _hy1_fwd(ctx, x, mp2, w["wz"], w["wuv"], w["wxbc"], w["wdt"], lng, lnb, w["sg_w"], sgbt, nct)
    cw = _pad_rows(w["ssd_conv_w"][0], 8)
    cb = w["ssd_conv_b"][0][None]
    xbc = _cv5_fwd(xbcp, cw, cb, nct)
    sp = _pad_rows(jnp.stack([_pad_cols(w["ssd_dt_bias"][0].reshape(1, 32), LANE)[0],
                              _pad_cols(w["ssd_a_log"][0].reshape(1, 32), LANE)[0]], axis=0), 8)
    dl = jnp.repeat(w["ssd_d"][0], 64)[None]
    ng = w["ssd_norm_g"][0][None]
    yf, hin_f = _ssd_fwd(xbc, dtr, sp, ncc, False)
    yb, hin_b = _ssd_fwd(xbc, dtr, sp, ncc, True)
    if late_w is not None:
        w = {**w, **late_w(yb)}
    h1, yssd, out0 =_hy4_fwd(x, yf, yb, xbc, z, ysg, mp_l0, dl, ng, w["hy_w_out"], nct)

    mpm0 = _mp(w["norm_mlp_g"][0], a0[3], a0[4], a0[5])
    h2, am0, ym0 = _mlp_fwd(h1, mpm0, w["wpack"], 0, "mlp0_fwd")

    mpc = _mp(w["norm_mix_g"][1], a1[0], a1[1], a1[2])
    wdw = _pad_rows(w["cf_w_dw"][0], 32)
    glu, acf = _cf1_fwd(h2, mpc, w["cf_w_pw1"], w["cf_b_pw1"])
    h3, cv, scf, ycf = _cf2_fwd(h2, glu, mpc, wdw, w["cf_b_dw"], w["cf_ln_g"], w["cf_ln_b"], w["cf_w_pw2"], w["cf_b_pw2"])

    mpm1 = _mp(w["norm_mlp_g"][1], a1[3], a1[4], a1[5])
    h4, am1, ym1 = _mlp_fwd(h3, mpm1, w["wpack"], 1, "mlp1_fwd")

    dh4, pg_f, ls = _loss_bwd(h4, tgt, w["final_norm_g"][None])
    loss = ls[0, 0]
    g["final_norm_g"] = pg_f[0]

    gp = {}
    dh3, da1, dy1, u1, pgm1 = _mlp_bwd(dh4, h3, am1, ym1, mpm1, w["wpack"], 1, "mlp1_bwd")
    gw1_1 = _tn(u1, da1, "tn_mlp1_w1", shard=("col", 1024))
    gw2_1 = _tn(am1, dy1, "tn_mlp1_w2", relu2=True, shard=("row", 1024))

    dcv, dycf, pgc2 = _cf2_bwd(dh3, ycf, cv, mpc, w["cf_ln_g"], w["cf_ln_b"], w["cf_w_pw2"])
    gp["cf_w_pw2"] = _tn(scf, dycf, "tn_cf_pw2", shard=("row", 256))
    dh2, dacf, ucf, pgc1, pbc1, dwdw = _cf1_bwd(dh3, h2, acf, dcv, glu, mpc, wdw, w["cf_w_pw1"])
    gp["cf_w_pw1"] = _tn(ucf, dacf, "tn_cf_pw1", shard=("col", 512)).reshape(4, 512, 1024)
    g["cf_b_pw2"], g["cf_ln_g"], g["cf_ln_b"], g["cf_b_dw"] = pgc2[1], pgc2[2], pgc2[3], pgc2[4]
    g["cf_b_pw1"] = pbc1[0]
    g["cf_w_dw"] = dwdw[:CK]

    dh1, da0, dy0, u0, pgm0 = _mlp_bwd(dh2, h1, am0, ym0, mpm0, w["wpack"], 0, "mlp0_bwd")
    gp["mlp_w1"] = jnp.concatenate([_tn(u0, da0, "tn_mlp0_w1", shard=("col", 1024)), gw1_1], axis=1)
    gp["mlp_w2"] = jnp.concatenate([_tn(am0, dy0, "tn_mlp0_w2", relu2=True, shard=("row", 1024)), gw2_1], axis=1)
    g["norm_mlp_g"] = jnp.stack([pgm0[0], pgm1[0]])

    dyt, dz, dysg, doutb, pg4 = _hy4_bwd(dh1, out0, yf, yb, xbc, z, mp_l0, dl, ng, w["hy_w_out"], nct)
    gp["hy_w_out"] = jnp.concatenate([_tn(yssd, doutb, "tn_out_ssd", shard=("row", 512)),
                                      _tn(ysg, doutb, "tn_out_sg", shard=("row", 512))], axis=0)
    if early_grads is not None:
        sp = sp + early_grads(gp)
    head_of_lane = jnp.arange(D, dtype=jnp.int32)[:, None] // 64
    col = jnp.arange(LANE, dtype=jnp.int32)[None, :]
    dxf, ddf, pgsf = _ssd_bwd(xbc, dtr, dyt, hin_f, sp, dl, (col == head_of_lane).astype(F32), ncc, False)
    dxb, ddb, pgsb = _ssd_bwd(xbc, dtr, dyt, hin_b, sp, dl, (col == head_of_lane + 16).astype(F32), ncc, True)
    dcv5, pgcb = _cv5_bwd1(xbcp, dxf, dxb, cw, cb, nct)
    dxbcp, dcw = _cv5_bwd2(dcv5, xbcp, cw, nct)
    dx, ucat, duv, ddt, pg2, pln, dsgw, dsgbt = _hy1_bwd(
        ctx, x, uv, dz, dxbcp, ddf, ddb, dysg, dh1, mp2, w["wz"], w["wuv"], w["wxbc"], w["wdt"], lng, lnb, w["sg_w"], sgbt, nct)
    gp["hy_w_in"] = _win_join(_tn(ucat, dz, "tn_in_z"), _tn(ucat, dxbcp, "tn_in_xbc"), _tn(ucat, ddt, "tn_in_dt"),
                              _tn(ucat, duv, "tn_in_uv"))
    g["ssd_conv_w"], g["ssd_conv_b"] = dcw[:SK], pgcb[0]
    pgs = pgsf + pgsb
    g["ssd_dt_bias"], g["ssd_a_log"] = pgs[0, :32].reshape(2, 16), pgs[1, :32].reshape(2, 16)
    g["ssd_d"] = jnp.sum(pg4[2].reshape(16, 64), axis=1)
    g["ssd_norm_g"] = pg4[1]
    g["sg_ln_g"], g["sg_ln_b"] = pln[0], pln[1]
    g["sg_w"], g["sg_b"] = dsgw, dsgbt[:, :SGG].T
    g["norm_mix_g"] = jnp.stack([pg2[0, 0] + pg2[1, 0], pgc1[0]])

    zero = jnp.zeros((D,), F32)
    d_ada = jnp.stack([jnp.concatenate([pg2[1, 1], pg2[1, 2], pg4[0], pgm0[1], pgm0[2], pgm0[3]]),
                       jnp.concatenate([pgc1[1], pgc1[2], pgc2[0], pgm1[1], pgm1[2], pgm1[3]])])
    d_cada0 = jnp.concatenate([pg2[0, 1], pg2[0, 2], zero, zero, zero, zero])
    g["pieces"] = gp
    return loss, dx, g, d_ada, d_cada0


def _pad_cols(a, cols):
    return jnp.concatenate([a, jnp.zeros(a.shape[:-1] + (cols - a.shape[-1],), a.dtype)], axis=-1)


MESH = pl.DeviceIdType.MESH
ANY = pl.BlockSpec(memory_space=pl.ANY)
IN_VMEM = pl.BlockSpec(memory_space=pltpu.VMEM)


def _coords():
    return lax.axis_index("x"), lax.axis_index("y"), lax.axis_index("c")


def _ag8(x, name):
    r, wd = x.shape

    def body(x_ref, o_ref, send, recv, lsem):
        mx, my, mc = _coords()
        me = 4 * mx + 2 * my + mc
        mine = pltpu.make_async_copy(x_ref, o_ref.at[me], lsem)
        mine.start()
        sent, peers = [], []
        for k in range(1, 8):
            px = 1 - mx if k & 4 else mx
            py = 1 - my if k & 2 else my
            pc = 1 - mc if k & 1 else mc
            cp = pltpu.make_async_remote_copy(src_ref=x_ref, dst_ref=o_ref.at[me], send_sem=send.at[k - 1],
                                              recv_sem=recv.at[k - 1], device_id=(px, py, pc), device_id_type=MESH)
            cp.start()
            sent.append(cp)
            peers.append((4 * px + 2 * py + pc, (px, py, pc)))
        for k in range(1, 8):
            slot, peer = peers[k - 1]
            pltpu.make_async_remote_copy(src_ref=x_ref, dst_ref=o_ref.at[slot], send_sem=send.at[k - 1],
                                         recv_sem=recv.at[k - 1], device_id=peer, device_id_type=MESH).wait_recv()
        for cp in sent:
            cp.wait_send()
        mine.wait()

    return pl.pallas_call(
        body, name=name, out_shape=_sds((8, r, wd), x.dtype), in_specs=[IN_VMEM], out_specs=IN_VMEM,
        scratch_shapes=[pltpu.SemaphoreType.DMA((7,)), pltpu.SemaphoreType.DMA((7,)), pltpu.SemaphoreType.DMA(())],
        compiler_params=pltpu.CompilerParams(vmem_limit_bytes=VMEM_LIMIT))(x)


def _xchg4(buf, name, a2a):
    r, wd = buf.shape[-2:]

    def body(in_ref, o_ref, send, recv, lsem):
        mx, my, mc = _coords()
        me = 2 * mx + my
        mine = pltpu.make_async_copy(in_ref.at[me] if a2a else in_ref, o_ref.at[me], lsem)
        mine.start()
        sent, peers = [], []
        for k in range(1, 4):
            px = 1 - mx if k & 2 else mx
            py = 1 - my if k & 1 else my
            pj = 2 * px + py
            cp = pltpu.make_async_remote_copy(src_ref=in_ref.at[pj] if a2a else in_ref, dst_ref=o_ref.at[me],
                                              send_sem=send.at[k - 1], recv_sem=recv.at[k - 1],
                                              device_id=(px, py, mc), device_id_type=MESH)
            cp.start()
            sent.append(cp)
            peers.append((pj, (px, py, mc)))
        for k in range(1, 4):
            pj, peer = peers[k - 1]
            pltpu.make_async_remote_copy(src_ref=in_ref.at[pj] if a2a else in_ref, dst_ref=o_ref.at[pj],
                                         send_sem=send.at[k - 1], recv_sem=recv.at[k - 1],
                                         device_id=peer, device_id_type=MESH).wait_recv()
        for cp in sent:
            cp.wait_send()
        mine.wait()

    return pl.pallas_call(
        body, name=name, out_shape=_sds((4, r, wd), buf.dtype), in_specs=[ANY], out_specs=ANY,
        scratch_shapes=[pltpu.SemaphoreType.DMA((3,)), pltpu.SemaphoreType.DMA((3,)), pltpu.SemaphoreType.DMA(())],
        )(buf)


HBM = pl.BlockSpec(memory_space=pltpu.HBM)
SEM = pl.BlockSpec(memory_space=pltpu.SEMAPHORE)
EFFECT = pltpu.SideEffectType.DATAFLOW_SIDE_EFFECTING


def _x4_peers(in_ref, land_ref, send, recv, a2a):
    mx, my, mc = _coords()
    me = 2 * mx + my
    out = []
    for k in range(1, 4):
        px = 1 - mx if k & 2 else mx
        py = 1 - my if k & 1 else my
        pj = 2 * px + py
        mk = functools.partial(pltpu.make_async_remote_copy, src_ref=in_ref.at[pj] if a2a else in_ref,
                               send_sem=send.at[k - 1], recv_sem=recv.at[k - 1], device_id=(px, py, mc), device_id_type=MESH)
        out.append((mk(dst_ref=land_ref.at[me]), mk(dst_ref=land_ref.at[pj])))
    return out


def _x4_start(buf, name, a2a):
    r, wd = buf.shape[-2:]

    def body(in_ref, land_ref, send, recv, in_thru, land_thru, token):
        for start, _ in _x4_peers(in_ref, land_ref, send, recv, a2a):
            start.start()
        token[...] = jnp.zeros_like(token)

    land = lax.empty((4, r, wd), buf.dtype)
    return pl.pallas_call(
        body, name=name,
        out_shape=(pltpu.SemaphoreType.DMA((3,)), pltpu.SemaphoreType.DMA((3,)), pltpu.HBM(buf.shape, buf.dtype),
                   pltpu.HBM(land.shape, land.dtype), _sds((8, LANE))),
        in_specs=(HBM, HBM), out_specs=(SEM, SEM, HBM, HBM, IN_VMEM), input_output_aliases={0: 2, 1: 3},
        compiler_params=pltpu.CompilerParams(has_side_effects=EFFECT),
    )(pltpu.with_memory_space_constraint(buf, pltpu.HBM), pltpu.with_memory_space_constraint(land, pltpu.HBM))


def _x4_wait(send, recv, buf_thru, land_thru, after, name, a2a):
    def body(in_ref, land_ref, send_ref, recv_ref, after_ref, in_dead, got_ref):
        for _, arrive in _x4_peers(in_ref, land_ref, send_ref, recv_ref, a2a):
            arrive.wait_send()
            arrive.wait_recv()

    return pl.pallas_call(
        body, name=name, out_shape=(pltpu.HBM(buf_thru.shape, buf_thru.dtype), pltpu.HBM(land_thru.shape, land_thru.dtype)),
        in_specs=(HBM, HBM, SEM, SEM, ANY), out_specs=(HBM, HBM), input_output_aliases={0: 0, 1: 1},
        compiler_params=pltpu.CompilerParams(has_side_effects=EFFECT),
    )(buf_thru, land_thru, send, recv, after)[1]


def _xchg_sib(x, name):
    def body(in_ref, o_ref, send, recv):
        mx, my, mc = _coords()
        cp = pltpu.make_async_remote_copy(src_ref=in_ref, dst_ref=o_ref, send_sem=send, recv_sem=recv,
                                          device_id=(mx, my, 1 - mc), device_id_type=MESH)
        cp.start()
        cp.wait_recv()
        cp.wait_send()

    return pl.pallas_call(
        body, name=name, out_shape=_sds(x.shape, x.dtype), in_specs=[ANY], out_specs=ANY,
        scratch_shapes=[pltpu.SemaphoreType.DMA(()), pltpu.SemaphoreType.DMA(())])(x)


def _sum_slots(gat, slots, name, tr=None):
    n, r, wd = gat.shape
    tr = r if tr is None else tr

    def body(g_ref, o_ref):
        acc = g_ref[slots[0]].astype(F32)
        for s in slots[1:]:
            acc = acc + g_ref[s].astype(F32)
        o_ref[...] = acc

    return _call(body, name, (r // tr,), [pl.BlockSpec((n, tr, wd), lambda i: (0, i, 0))], _rows(tr, wd), _sds((r, wd)))(gat)


def _add(a, b, name, tr):
    def body(a_ref, b_ref, o_ref):
        o_ref[...] = a_ref[...] + b_ref[...]

    r, wd = a.shape
    return _call(body, name, (r // tr,), [_rows(tr, wd), _rows(tr, wd)], _rows(tr, wd), _sds((r, wd)))(a, b)


def _ada_fwd(x16, ada_w_loc, ada_b_loc):
    nloc = ada_w_loc.shape[-1]

    def body(x_ref, w_ref, b_ref, s_ref, o_ref):
        s = _silu(x_ref[...])
        s_ref[...] = s
        o_ref[...] = jnp.dot(s, w_ref[...], precision=HI, preferred_element_type=F32) + b_ref[...]

    return _call(body, "ada_fwd", (2,),
                 [_full(16, D), pl.BlockSpec((None, D, nloc), lambda l: (l, 0, 0)), pl.BlockSpec((None, 1, nloc), lambda l: (l, 0, 0))],
                 [_full(16, D), pl.BlockSpec((None, 16, nloc), lambda l: (l, 0, 0))],
                 [_sds((16, D)), _sds((2, 16, nloc))])(x16, ada_w_loc, ada_b_loc[:, None, :])


def _ada_bwd(s16, d_loc, ada_w_loc):
    nloc = ada_w_loc.shape[-1]

    def body(s_ref, d_ref, w_ref, gw_ref, cp_ref):
        gw_ref[...] = lax.dot_general(s_ref[...], d_ref[...], (((0,), (0,)), ((), ())), precision=HI,
                                      preferred_element_type=F32)

        @pl.when(pl.program_id(0) == 0)
        def _():
            cp_ref[...] = lax.dot_general(d_ref[8:16, :], w_ref[...], (((1,), (1,)), ((), ())), precision=HI,
                                          preferred_element_type=F32)

    return _call(body, "ada_bwd", (2,),
                 [_full(16, D), pl.BlockSpec((None, 16, nloc), lambda l: (l, 0, 0)), pl.BlockSpec((None, D, nloc), lambda l: (l, 0, 0))],
                 [pl.BlockSpec((None, D, nloc), lambda l: (l, 0, 0)), _full(8, D)],
                 [_sds((2, D, nloc)), _sds((8, D))])(s16, d_loc, ada_w_loc)


def _cctx_grad(dscc, c_ctx):
    def body(d_ref, c_ref, o_ref):
        _, vjp = jax.vjp(_silu, c_ref[...])
        o_ref[...] = vjp(d_ref[...])[0]

    return _call(body, "cctx_grad", (1,), [_full(8, D), _full(8, D)], _full(8, D), _sds((8, D)))(dscc, c_ctx)


def _adamw_math(w, g, m, v):
    mn = ADAM_B1 * m + (1.0 - ADAM_B1) * g
    vn = ADAM_B2 * v + (1.0 - ADAM_B2) * jnp.square(g)
    c1 = 1.0 - ADAM_B1 ** ADAM_STEP
    c2 = 1.0 - ADAM_B2 ** ADAM_STEP
    return -ADAM_LR * ((mn / c1) / (jnp.sqrt(vn / c2) + ADAM_EPS) + ADAM_WD * w), mn, vn


def _adamw(w, g, m, v, name):
    n_l, r, wd = w.shape
    tr = 256 if r % 256 == 0 else r

    def body(w_ref, g_ref, m_ref, v_ref, d_ref, mo_ref, vo_ref):
        d_ref[...], mo_ref[...], vo_ref[...] = _adamw_math(w_ref[...], g_ref[...], m_ref[...], v_ref[...])

    spec = pl.BlockSpec((None, tr, wd), lambda a, i: (a, i, 0))
    return tuple(_call(body, name, (n_l, r // tr), [spec] * 4, [spec] * 3, [_sds(w.shape)] * 3)(w, g, m, v))


def _adamw_small(ws, gs, ms, vs, name):
    n = len(ws)
    shapes = [a.shape for a in ws]
    as2d = lambda a: a.reshape(-1, a.shape[-1])

    def body(*refs):
        ins, outs = refs[:4 * n], refs[4 * n:]
        for k in range(n):
            res = _adamw_math(ins[k][...], ins[n + k][...], ins[2 * n + k][...], ins[3 * n + k][...])
            for j in range(3):
                outs[j * n + k][...] = res[j]

    flat = [as2d(a) for group in (ws, gs, ms, vs) for a in group]
    specs = [_full(*a.shape) for a in flat]
    outs = _call(body, name, (1,), specs, specs[:n] * 3, [_sds(a.shape) for a in flat[:n]] * 3)(*flat)
    return tuple([outs[j * n + k].reshape(shapes[k]) for k in range(n)] for j in range(3))


ROW = 1024


def _nrows(size):
    return -(-size // ROW)


def _pack(arrs, rows_total, dtype=F32):
    parts = []
    for a in arrs:
        flat = a.reshape(-1).astype(dtype)
        pad = _nrows(flat.shape[0]) * ROW - flat.shape[0]
        parts.append(flat if pad == 0 else jnp.concatenate([flat, jnp.zeros((pad,), dtype)]))
    flat = jnp.concatenate(parts)
    out = flat.reshape(-1, ROW)
    return _pad_rows(out, rows_total)


def _unpack(buf, shapes):
    lead = buf.shape[:-2]
    out, r0 = [], 0
    for shp in shapes:
        size = 1
        for s in shp:
            size *= s
        nr = _nrows(size)
        piece = lax.slice_in_dim(buf, r0, r0 + nr, axis=len(lead))
        out.append(piece.reshape(lead + (nr * ROW,))[..., :size].reshape(lead + tuple(shp)))
        r0 += nr
    return out


SLOT = 16


def _slot_rows(size):
    return _round_up(size // ROW, SLOT)


def _pack_rows(arrs, rows_total, dtype):
    parts, used = [], 0
    for a in arrs:
        part = a.astype(dtype).reshape(-1, ROW)
        extra = _slot_rows(a.size) - part.shape[0]
        parts.append(part if extra == 0 else jnp.pad(part, ((0, extra), (0, 0))))
        used += _slot_rows(a.size)
    if rows_total > used:
        parts.append(jnp.zeros((rows_total - used, ROW), dtype))
    return jnp.concatenate(parts, axis=0)


def _unpack_rows(buf, shapes):
    lead = buf.shape[:-2]
    out, r0 = [], 0
    for shp in shapes:
        size = 1
        for s in shp:
            size *= s
        piece = lax.slice_in_dim(buf, r0, r0 + size // ROW, axis=len(lead))
        out.append(piece.reshape(lead + tuple(shp)))
        r0 += _slot_rows(size)
    return out


def _round_up(n, k):
    return -(-n // k) * k


WEIGHTS = ['c_ctx', 'ada_w', 'ada_b', 'norm_mix_g', 'norm_mlp_g', 'mlp_w1', 'mlp_w2', 'hy_w_in', 'ssd_conv_w', 'ssd_conv_b',
           'ssd_dt_bias', 'ssd_a_log', 'ssd_d', 'ssd_norm_g', 'sg_ln_g', 'sg_ln_b', 'sg_w', 'sg_b', 'hy_w_out', 'cf_w_pw1',
           'cf_b_pw1', 'cf_w_dw', 'cf_b_dw', 'cf_ln_g', 'cf_ln_b', 'cf_w_pw2', 'cf_b_pw2', 'final_norm_g']
BIG = {'mlp_w1': 2, 'mlp_w2': 1, 'hy_w_in': 2, 'hy_w_out': 1, 'cf_w_pw1': 2, 'cf_w_pw2': 1}
SMALL_SHARD = ['ssd_conv_w', 'cf_b_pw1', 'cf_w_dw', 'cf_b_dw', 'cf_ln_g', 'cf_ln_b', 'cf_b_pw2']
REP = ['norm_mix_g', 'norm_mlp_g', 'ssd_conv_b', 'ssd_dt_bias', 'ssd_a_log', 'ssd_d', 'ssd_norm_g', 'sg_ln_g', 'sg_ln_b',
       'sg_w', 'sg_b', 'final_norm_g']


def _gather_shards(stacked, axis):
    return jnp.concatenate([stacked[j] for j in range(4)], axis=axis)


def _split_shards(full, axis):
    n = full.shape[axis] // 4
    return [lax.slice_in_dim(full, j * n, (j + 1) * n, axis=axis) for j in range(4)]


def kernel(x, c, ctx, c_ctx, ada_w, ada_b, norm_mix_g, norm_mlp_g, mlp_w1, mlp_w2, hy_w_in, ssd_conv_w, ssd_conv_b, ssd_dt_bias, ssd_a_log, ssd_d, ssd_norm_g, sg_ln_g, sg_ln_b, sg_w, sg_b, hy_w_out, cf_w_pw1, cf_b_pw1, cf_w_dw, cf_b_dw, cf_ln_g, cf_ln_b, cf_w_pw2, cf_b_pw2, final_norm_g, loss_target, m_c_ctx, m_ada_w, m_ada_b, m_norm_mix_g, m_norm_mlp_g, m_mlp_w1, m_mlp_w2, m_hy_w_in, m_ssd_conv_w, m_ssd_conv_b, m_ssd_dt_bias, m_ssd_a_log, m_ssd_d, m_ssd_norm_g, m_sg_ln_g, m_sg_ln_b, m_sg_w, m_sg_b, m_hy_w_out, m_cf_w_pw1, m_cf_b_pw1, m_cf_w_dw, m_cf_b_dw, m_cf_ln_g, m_cf_ln_b, m_cf_w_pw2, m_cf_b_pw2, m_final_norm_g, v_c_ctx, v_ada_w, v_ada_b, v_norm_mix_g, v_norm_mlp_g, v_mlp_w1, v_mlp_w2, v_hy_w_in, v_ssd_conv_w, v_ssd_conv_b, v_ssd_dt_bias, v_ssd_a_log, v_ssd_d, v_ssd_norm_g, v_sg_ln_g, v_sg_ln_b, v_sg_w, v_sg_b, v_hy_w_out, v_cf_w_pw1, v_cf_b_pw1, v_cf_w_dw, v_cf_b_dw, v_cf_ln_g, v_cf_ln_b, v_cf_w_pw2, v_cf_b_pw2, v_final_norm_g):
    args = locals()
    wl = {n: args[n] for n in WEIGHTS}
    ml = {n: args["m_" + n] for n in WEIGHTS}
    vl = {n: args["v_" + n] for n in WEIGHTS}
    mx, my, mc = _coords()
    me = 4 * mx + 2 * my + mc
    shard = 2 * mx + my
    even = (0, 2, 4, 6)

    def start_gather(names, name, tie=None):
        rows = sum(_slot_rows(wl[n].size) for n in names)
        buf = _pack_rows([wl[n] for n in names], rows, MXU)
        if tie is not None:
            buf, _ = lax.optimization_barrier((buf, tie))
        return _x4_start(buf, name, a2a=False)

    def finish_gather(handle, names, after, name):
        send, recv, own, land, _ = handle
        land = _x4_wait(send, recv, own, land, after, name, a2a=False)
        got = lax.dynamic_update_slice(land, own[None], (shard, 0, 0))
        shapes = [wl[n].shape for n in names]
        wfull = {n: _gather_shards(st, BIG[n]) for n, st in zip(names, _unpack_rows(got, shapes))}
        return wfull, got

    rest_names = [n for n in BIG if n != "hy_w_in"]
    h_in = _x4_start(hy_w_in[0].astype(MXU), "agw_in_start", a2a=False)
    c = c + h_in[4][0, 0]

    small_shapes = [wl[n].shape for n in SMALL_SHARD]
    blk1 = _pack([c] + [wl[n] for n in SMALL_SHARD], 24)
    got1 = _ag8(blk1, "ag_cond")
    x16 = _pad_rows(jnp.concatenate([got1[:, 0, :], c_ctx[None]], axis=0), 16)
    small_full = {}
    for n, parts in zip(SMALL_SHARD, _unpack(got1[:, 1:, :], small_shapes)):
        small_full[n] = jnp.concatenate([parts[s] for s in even], axis=-1)

    nloc = ada_w.shape[-1]
    ada_b_loc = lax.dynamic_slice_in_dim(ada_b, shard * nloc, nloc, axis=1)
    s16, ada_loc = _ada_fwd(x16, ada_w, ada_b_loc)
    got2 = _ag8(ada_loc.reshape(32, nloc), "ag_ada").reshape(8, 2, 16, nloc)
    ada_full = jnp.concatenate([got2[s] for s in even], axis=-1)
    ada_me = lax.dynamic_slice_in_dim(ada_full, me, 1, axis=1)[:, 0, :]
    cada0 = ada_full[0, 8, :]

    w = {n: wl[n] for n in WEIGHTS if n not in BIG and n not in SMALL_SHARD}
    w.update(small_full)
    h_rest = start_gather(rest_names, "agw_rest_start", tie=ada_me)
    send, recv, own, land, _ = h_in
    land = _x4_wait(send, recv, own, land, h_rest[4], "agw_in_wait", a2a=False)
    w["wz"], w["wxbc"], w["wdt"], w["wuv"] = _win_split(lax.dynamic_update_slice(land, own[None], (shard, 0, 0)))
    w["sg_w"] = sg_w[0].astype(MXU)

    def late_w(after):
        wfull, got = finish_gather(h_rest, rest_names, after, "agw_rest_wait")
        return {"hy_w_out": wfull["hy_w_out"][0], "wpack": got,
                "cf_w_pw1": wfull["cf_w_pw1"][0], "cf_w_pw2": wfull["cf_w_pw2"][0]}

    full_shape = {n: wl[n].shape for n in WEIGHTS}
    for n in BIG:
        full_shape[n] = tuple(s * 4 if a == BIG[n] else s for a, s in enumerate(wl[n].shape))
    for n in SMALL_SHARD:
        full_shape[n] = wl[n].shape[:-1] + (wl[n].shape[-1] * 4,)

    def grad_pieces(g_, names):
        rows = sum(_slot_rows(wl[n].size) for n in names)
        return jnp.stack([_pack_rows([_split_shards(g_[n].reshape(full_shape[n]), BIG[n])[j] for n in names], rows, jnp.bfloat16)
                          for j in range(4)])

    early_names = ["mlp_w1", "mlp_w2", "cf_w_pw1", "cf_w_pw2", "hy_w_out"]
    last_names = ["hy_w_in"]
    early = {}

    def early_grads(gp):
        used = sum(gp[n].shape[1] for n in early_names)
        parts = [gp[n] for n in early_names] + [jnp.zeros((4, _round_up(used, 512) - used, ROW), jnp.bfloat16)]
        early["h"] = _x4_start(jnp.concatenate(parts, axis=1), "a2a_early_start", a2a=True)
        return early["h"][4][0, 0]

    loss_part, dx, g, d_ada, d_cada0 = _local_step(x[0], ctx[0], loss_target[0], ada_me, cada0, w, late_w, early_grads)
    loss = lax.psum(loss_part, ("x", "y", "c"))
    gp_last = g["pieces"]["hy_w_in"]
    g = {n: a.reshape(full_shape[n]) for n, a in g.items() if n != "pieces"}

    sm_names = REP + SMALL_SHARD
    srows = _round_up(sum(_nrows(g[n].size) for n in sm_names) + 18, 16)
    got3 = _ag8(_pack([g[n] for n in sm_names] + [d_ada, d_cada0], srows, jnp.bfloat16), "ag_small")
    tot3 = _sum_slots(got3, tuple(range(8)), "sum_small")
    sm_tot = _unpack(tot3, [full_shape[n] for n in sm_names] + [(2, 6 * D), (6 * D,)])
    grads = dict(zip(sm_names, sm_tot[:-2]))
    for n in SMALL_SHARD:
        k = wl[n].shape[-1]
        grads[n] = lax.dynamic_slice_in_dim(grads[n], shard * k, k, axis=grads[n].ndim - 1)
    dada_tot, dcada_tot = sm_tot[-2], sm_tot[-1]
    grads["ada_b"] = dada_tot.at[0].add(dcada_tot)
    r_ada = sum(_nrows(g[n].size) for n in sm_names)
    dada_all = got3[:, r_ada:r_ada + 12, :].astype(F32).reshape(8, 2, 6 * D)
    d16 = jnp.concatenate([jnp.transpose(dada_all, (1, 0, 2)),
                           jnp.stack([dcada_tot, jnp.zeros_like(dcada_tot)])[:, None, :],
                           jnp.zeros((2, 7, 6 * D), F32)], axis=1)
    d_loc = lax.dynamic_slice_in_dim(d16, shard * nloc, nloc, axis=2)
    grads["ada_w"], cpart = _ada_bwd(s16, d_loc, ada_w)
    got4 = _ag8(cpart, "ag_cctx")
    dscc = _sum_slots(got4, even, "sum_cctx")
    grads["c_ctx"] = _cctx_grad(dscc, _pad_rows(c_ctx[None], 8))[0]

    delta, new_m, new_v = {}, {}, {}

    def finish(handle, after, names, tag):
        send, recv, own, land, _ = handle
        land = _x4_wait(send, recv, own, land, after, "a2a_" + tag + "_wait", a2a=True)
        got = lax.dynamic_update_slice(land, lax.dynamic_slice_in_dim(own, shard, 1, axis=0), (shard, 0, 0))
        tr = 512 if got.shape[1] % 512 == 0 else got.shape[1]
        part = _sum_slots(got, (0, 1, 2, 3), "sum_grads_" + tag, tr=tr)
        tot = _add(part, _xchg_sib(part, "swap_grads_" + tag), "add_grads_" + tag, tr)
        if len(names) == 1:
            grads[names[0]] = tot.reshape(wl[names[0]].shape)
        else:
            grads.update(zip(names, _unpack_rows(tot, [wl[n].shape for n in names])))
        for n in names:
            delta[n], new_m[n], new_v[n] = _adamw(wl[n], grads[n], ml[n], vl[n], "adamw_" + n)

    last_pieces, _ = lax.optimization_barrier((gp_last, grads["c_ctx"]))
    h_last = _x4_start(last_pieces, "a2a_last_start", a2a=True)
    finish(early["h"], h_last[4], early_names, "early")
    delta["ada_w"], new_m["ada_w"], new_v["ada_w"] = _adamw(ada_w, grads["ada_w"], ml["ada_w"], vl["ada_w"], "adamw_ada_w")
    small = ["c_ctx", "ada_b"] + REP + SMALL_SHARD
    outs = _adamw_small(*([src[n].reshape(wl[n].shape) for n in small] for src in (wl, grads, ml, vl)), "adamw_small")
    for dst, vals in zip((delta, new_m, new_v), outs):
        dst.update(zip(small, vals))
    finish(h_last, delta["mlp_w1"], last_names, "last")

    return (loss, dx[None], *[grads[n].reshape(wl[n].shape) for n in WEIGHTS], *[delta[n] for n in WEIGHTS],
            *[new_m[n] for n in WEIGHTS], *[new_v[n] for n in WEIGHTS])
```

```python
import functools

import jax
import jax.numpy as jnp
from jax import lax
from jax.experimental import pallas as pl
from jax.experimental.pallas import tpu as pltpu

F32 = jnp.float32
MXU = jnp.bfloat16
ACT = jnp.bfloat16
HI = lax.Precision.HIGHEST
EPS = 1e-6

D = 1024
HID = 4096
XBC = 1536
Q = 128
SGG = 8
CK = 31
SK = 5
T = 256
LANE = 128
VMEM_LIMIT = 56 * 1024 * 1024

ADAM_LR, ADAM_B1, ADAM_B2, ADAM_EPS, ADAM_WD, ADAM_STEP = 0.001, 0.9, 0.999, 1e-08, 0.01, 10


def _call(body, name, grid, in_specs, out_specs, out_shape, scratch=()):
    return pl.pallas_call(
        body, name=name, grid=grid, in_specs=in_specs, out_specs=out_specs, out_shape=out_shape,
        scratch_shapes=list(scratch),
        compiler_params=pltpu.CompilerParams(dimension_semantics=("arbitrary",) * len(grid),
                                             vmem_limit_bytes=VMEM_LIMIT))


def _sds(shape, dt=F32):
    return jax.ShapeDtypeStruct(tuple(shape), dt)


def _rows(t, w, off=0, lane_blk=0):
    return pl.BlockSpec((t, w), lambda i: (i + off, lane_blk))


def _rows_lat(t, w, nct):
    return pl.BlockSpec((t, w), lambda i: (jnp.maximum(i - nct, 0), 0))


def _full(*shape):
    return pl.BlockSpec(shape, lambda *_: (0,) * len(shape))


def _wfull(*shape):
    return pl.BlockSpec(shape, lambda *_: (0,) * len(shape), pipeline_mode=pl.Buffered(1))


def _halo(w, hb, nrows):
    r, nb = T // hb, nrows // hb
    prev = pl.BlockSpec((hb, w), lambda i: (jnp.maximum(i * r - 1, 0), 0))
    nxt = pl.BlockSpec((hb, w), lambda i: (jnp.minimum((i + 1) * r, nb - 1), 0))
    return prev, nxt


def _nn(a, b):
    return jnp.dot(a.astype(MXU), b.astype(MXU), preferred_element_type=F32)


def _nt(a, b):
    return lax.dot_general(a.astype(MXU), b.astype(MXU), (((1,), (1,)), ((), ())), preferred_element_type=F32)


def _tn_dot(a, b):
    return lax.dot_general(a.astype(MXU), b.astype(MXU), (((0,), (0,)), ((), ())), preferred_element_type=F32)


def _sum0(x):
    return jnp.sum(x, axis=0, keepdims=True)


def _silu(x):
    return x * jax.nn.sigmoid(x)


def _gelu(x):
    return jax.nn.gelu(x, approximate=True)


def _mod(h, g, sh, sc):
    n = h * lax.rsqrt(jnp.mean(h * h, axis=-1, keepdims=True) + EPS)
    return n * g * (1.0 + sc) + sh


def _ln(x, g, b):
    xc = x - jnp.mean(x, axis=-1, keepdims=True)
    return xc * lax.rsqrt(jnp.mean(xc * xc, axis=-1, keepdims=True) + EPS) * g + b


def _gate_norm(ytot, z, ng):
    yg = ytot * _silu(z)
    halves = []
    for k in range(2):
        seg = yg[:, k * 512:(k + 1) * 512]
        halves.append(seg * lax.rsqrt(jnp.mean(seg * seg, axis=-1, keepdims=True) + EPS) * ng[:, k * 512:(k + 1) * 512])
    return jnp.concatenate(halves, axis=-1)


def _fill_ext(ext_ref, prev_ref, cur_ref, next_ref, hb, first, last):
    ext_ref[0:hb, :] = jnp.where(first, 0.0, prev_ref[...])
    ext_ref[hb:hb + T, :] = cur_ref[...]
    ext_ref[hb + T:hb + T + hb, :] = jnp.where(last, 0.0, next_ref[...])


def _conv(ext_ref, w_ref, k_taps, hb, lanes):
    off = hb - k_taps // 2
    acc = ext_ref[pl.ds(off, T), lanes] * w_ref[0:1, lanes]
    for k in range(1, k_taps):
        acc = acc + ext_ref[pl.ds(off + k, T), lanes] * w_ref[k:k + 1, lanes]
    return acc


def _conv_tr(ext_ref, w_ref, k_taps, hb, lanes):
    off = hb + k_taps // 2
    acc = ext_ref[pl.ds(off, T), lanes] * w_ref[0:1, lanes]
    for k in range(1, k_taps):
        acc = acc + ext_ref[pl.ds(off - k, T), lanes] * w_ref[k:k + 1, lanes]
    return acc


def _conv_wide(ext_ref, w_ref, k_taps, hb, lanes, flip=False):
    base = hb - k_taps // 2
    acc = None
    for b in range(8):
        taps = [k for k in range(k_taps) if (base + k) % 8 == b]
        if not taps:
            continue
        p = None
        for k in taps:
            wi = (k_taps - 1 - k) if flip else k
            term = ext_ref[pl.ds(base + k - b, T + 8), lanes] * w_ref[wi:wi + 1, lanes]
            p = term if p is None else p + term
        acc = p[b:b + T] if acc is None else acc + p[b:b + T]
    return acc


def _conv_dw_wide(dw_ref, d_ref, xext_ref, k_taps, hb, lanes):
    base = hb - k_taps // 2
    d = d_ref[:, lanes]
    for b in range(8):
        taps = [k for k in range(k_taps) if (base + k) % 8 == b]
        if not taps:
            continue
        lo_off = base + taps[0] - b
        span = base + taps[-1] - b - lo_off
        xs = xext_ref[pl.ds(lo_off + b, T + span), lanes]
        for k in taps:
            a = base + k - b - lo_off
            dw_ref[k:k + 1, lanes] += _sum0(d * xs[a:a + T])


def _conv_dw(dw_ref, d_ref, xext_ref, k_taps, hb, lanes):
    off = hb - k_taps // 2
    d = d_ref[:, lanes]
    for k in range(k_taps):
        dw_ref[k:k + 1, lanes] += _sum0(d * xext_ref[pl.ds(off + k, T), lanes])


def _tn(a, b, name, relu2=False, shard=None):
    m_rows, ka = a.shape
    n = b.shape[1]
    tm = next(t for t in (1024, 768, 512, 256) if m_rows % t == 0)
    tk = min(ka, 1024)
    tn = n if n <= 1024 else next(t for t in (1024, 768, 512, 384, 256, 128) if n % t == 0)
    if shard is not None and shard[0] == "col":
        tn = shard[1]
    if shard is not None and shard[0] == "row":
        tk = shard[1]
    n_m = m_rows // tm

    def body(a_ref, b_ref, o_ref, *acc):
        acc_ref = acc[0] if acc else o_ref

        @pl.when(pl.program_id(2) == 0)
        def _():
            acc_ref[...] = jnp.zeros_like(acc_ref)
        av = a_ref[...]
        if relu2:
            av = jnp.square(jnp.maximum(av.astype(F32), 0.0))
        acc_ref[...] += _tn_dot(av, b_ref[...])
        if acc:
            @pl.when(pl.program_id(2) == n_m - 1)
            def _():
                o_ref[...] = acc_ref[...].astype(o_ref.dtype)

    in_specs = [pl.BlockSpec((tm, tk), lambda k, j, m: (m, k)), pl.BlockSpec((tm, tn), lambda k, j, m: (m, j))]
    if shard is None:
        return _call(body, name, (ka // tk, n // tn, n_m), in_specs,
                     pl.BlockSpec((tk, tn), lambda k, j, m: (k, j)), _sds((ka, n)))(a, b)
    if shard[0] == "col":
        out_spec, out_shape = pl.BlockSpec((None, tk, tn), lambda k, j, m: (j, k, 0)), _sds((n // tn, ka, tn), jnp.bfloat16)
    else:
        out_spec, out_shape = pl.BlockSpec((None, tk, tn), lambda k, j, m: (k, 0, j)), _sds((ka // tk, tk, n), jnp.bfloat16)
    return _call(body, name, (ka // tk, n // tn, n_m), in_specs, out_spec, out_shape,
                 scratch=[pltpu.VMEM((tk, tn), F32)])(a, b)


def _mlp_fwd(h, mp, wpack, layer, name):
    n_rows = h.shape[0]

    def body(h_ref, mp_ref, w1_ref, w2_ref, hn_ref, a_ref, y_ref):
        hv = h_ref[...]
        u = _mod(hv, mp_ref[0:1], mp_ref[1:2], mp_ref[2:3]).astype(MXU)
        acc = jnp.zeros((T, D), F32)
        for j in range(HID // 1024):
            cs = slice(j * 1024, (j + 1) * 1024)
            a = jnp.dot(u, w1_ref[j], preferred_element_type=F32)
            a_ref[:, cs] = a.astype(ACT)
            acc = acc + jnp.dot(jnp.square(jnp.maximum(a, 0.0)).astype(MXU), w2_ref[j], preferred_element_type=F32)
        y_ref[...] = acc
        hn_ref[...] = hv + mp_ref[3:4] * acc

    return _call(body, name, (n_rows // T,),
                 [_rows(T, D), _full(8, D), _mlp_wspec(layer), _mlp_wspec(2 + layer)],
                 [_rows(T, D), _rows(T, HID), _rows(T, D)],
                 [_sds((n_rows, D)), _sds((n_rows, HID), ACT), _sds((n_rows, D))])(h, mp, wpack, wpack)


def _mlp_wspec(row_block):
    return pl.BlockSpec((4, 1024, 1024), lambda i: (0, row_block, 0), pipeline_mode=pl.Buffered(1))


def _mlp_bwd(dh, h, a, y, mp, wpack, layer, name):
    n_rows = h.shape[0]

    def body(dh_ref, h_ref, a_ref, y_ref, mp_ref, w1_ref, w2_ref, dho_ref, da_ref, dyb_ref, ub_ref, pg_ref):
        dhp = dh_ref[...]
        u, vjp = jax.vjp(_mod, h_ref[...], mp_ref[0:1], mp_ref[1:2], mp_ref[2:3])
        ub_ref[...] = u.astype(ACT)
        dyb = (mp_ref[3:4] * dhp).astype(MXU)
        dyb_ref[...] = dyb.astype(ACT)
        du = jnp.zeros((T, D), F32)
        for j in range(HID // 1024):
            cs = slice(j * 1024, (j + 1) * 1024)
            dp = _nt(dyb, w2_ref[j])
            da = dp * 2.0 * jnp.maximum(a_ref[:, cs].astype(F32), 0.0)
            da_ref[:, cs] = da.astype(ACT)
            du = du + _nt(da, w1_ref[j])
        dhn, dg, dsh, dsc = vjp(du)
        dho_ref[...] = dhp + dhn

        @pl.when(pl.program_id(0) == 0)
        def _():
            pg_ref[...] = jnp.zeros_like(pg_ref)
        pg_ref[0:1] += dg
        pg_ref[1:2] += dsh
        pg_ref[2:3] += dsc
        pg_ref[3:4] += _sum0(dhp * y_ref[...])

    return _call(body, name, (n_rows // T,),
                 [_rows(T, D), _rows(T, D), _rows(T, HID), _rows(T, D), _full(8, D), _mlp_wspec(layer), _mlp_wspec(2 + layer)],
                 [_rows(T, D), _rows(T, HID), _rows(T, D), _rows(T, D), _full(8, D)],
                 [_sds((n_rows, D)), _sds((n_rows, HID), ACT), _sds((n_rows, D), ACT), _sds((n_rows, D), ACT),
                  _sds((8, D))])(dh, h, a, y, mp, wpack, wpack)


def _cf1_fwd(h, mp, w1, b1):
    n_rows = h.shape[0]

    def body(h_ref, mp_ref, w1_ref, b1_ref, glu_ref, a_ref):
        u = _mod(h_ref[...], mp_ref[0:1], mp_ref[1:2], mp_ref[2:3]).astype(MXU)
        a = jnp.dot(u, w1_ref[...], preferred_element_type=F32) + b1_ref[...]
        a_ref[...] = a.astype(ACT)
        glu_ref[...] = a[:, :D] * jax.nn.sigmoid(a[:, D:])

    return _call(body, "cf1_fwd", (n_rows // T,),
                 [_rows(T, D), _full(8, D), _wfull(D, 2 * D), _full(1, 2 * D)],
                 [_rows(T, D), _rows(T, 2 * D)],
                 [_sds((n_rows, D)), _sds((n_rows, 2 * D), ACT)])(h, mp, w1, b1)


def _cf2_fwd(h, glu, mp, wdw, bdw, lng, lnb, w2, b2):
    n_rows = h.shape[0]
    nt = n_rows // T
    hb = 16

    def body(h_ref, gp_ref, gc_ref, gn_ref, mp_ref, wdw_ref, bdw_ref, lng_ref, lnb_ref, w2_ref, b2_ref,
             hn_ref, cv_ref, sb_ref, y_ref, ext):
        i = pl.program_id(0)
        _fill_ext(ext, gp_ref, gc_ref, gn_ref, hb, i == 0, i == nt - 1)
        for c in range(D // LANE):
            lanes = slice(c * LANE, (c + 1) * LANE)
            cv_ref[:, lanes] = _conv_wide(ext, wdw_ref, CK, hb, lanes) + bdw_ref[:, lanes]
        s = _silu(_ln(cv_ref[...], lng_ref[...], lnb_ref[...])).astype(MXU)
        sb_ref[...] = s.astype(ACT)
        y = jnp.dot(s, w2_ref[...], preferred_element_type=F32) + b2_ref[...]
        y_ref[...] = y
        hn_ref[...] = h_ref[...] + mp_ref[3:4] * y

    gp, gn = _halo(D, hb, n_rows)
    return _call(body, "cf2_fwd", (nt,),
                 [_rows(T, D), gp, _rows(T, D), gn, _full(8, D), _full(32, D), _full(1, D), _full(1, D), _full(1, D),
                  _wfull(D, D), _full(1, D)],
                 [_rows(T, D), _rows(T, D), _rows(T, D), _rows(T, D)],
                 [_sds((n_rows, D)), _sds((n_rows, D)), _sds((n_rows, D), ACT), _sds((n_rows, D))],
                 scratch=[pltpu.VMEM((T + 2 * hb, D), F32)])(h, glu, glu, glu, mp, wdw, bdw, lng, lnb, w2, b2)


def _cf2_bwd(dh, y, cv, mp, lng, lnb, w2):
    n_rows = dh.shape[0]

    def body(dh_ref, y_ref, cv_ref, mp_ref, lng_ref, lnb_ref, w2_ref, dcv_ref, dyb_ref, pg_ref):
        dhp = dh_ref[...]
        dy = mp_ref[3:4] * dhp
        dyb = dy.astype(MXU)
        dyb_ref[...] = dyb.astype(ACT)
        ds = _nt(dyb, w2_ref[...])
        _, vjp = jax.vjp(lambda cv_, g_, b_: _silu(_ln(cv_, g_, b_)), cv_ref[...], lng_ref[...], lnb_ref[...])
        dcv, dlng, dlnb = vjp(ds)
        dcv_ref[...] = dcv

        @pl.when(pl.program_id(0) == 0)
        def _():
            pg_ref[...] = jnp.zeros_like(pg_ref)
        pg_ref[0:1] += _sum0(dhp * y_ref[...])
        pg_ref[1:2] += _sum0(dy)
        pg_ref[2:3] += dlng
        pg_ref[3:4] += dlnb
        pg_ref[4:5] += _sum0(dcv)

    return _call(body, "cf2_bwd", (n_rows // T,),
                 [_rows(T, D), _rows(T, D), _rows(T, D), _full(8, D), _full(1, D), _full(1, D), _wfull(D, D)],
                 [_rows(T, D), _rows(T, D), _full(8, D)],
                 [_sds((n_rows, D)), _sds((n_rows, D), ACT), _sds((8, D))])(dh, y, cv, mp, lng, lnb, w2)


def _cf1_bwd(dh, h, a, dcv, glu, mp, wdw, w1):
    n_rows = h.shape[0]
    nt = n_rows // T
    hb = 16

    def body(dh_ref, h_ref, a_ref, dp_ref, dc_ref, dn_ref, gp_ref, gc_ref, gn_ref, mp_ref, wdw_ref, w1_ref,
             dho_ref, da_ref, ub_ref, pg_ref, pb_ref, dw_ref, dext, gext, dglu):
        i = pl.program_id(0)

        @pl.when(i == 0)
        def _():
            pg_ref[...] = jnp.zeros_like(pg_ref)
            pb_ref[...] = jnp.zeros_like(pb_ref)
            dw_ref[...] = jnp.zeros_like(dw_ref)
        _fill_ext(dext, dp_ref, dc_ref, dn_ref, hb, i == 0, i == nt - 1)
        _fill_ext(gext, gp_ref, gc_ref, gn_ref, hb, i == 0, i == nt - 1)
        for c in range(D // LANE):
            lanes = slice(c * LANE, (c + 1) * LANE)
            dglu[:, lanes] = _conv_wide(dext, wdw_ref, CK, hb, lanes, flip=True)
            _conv_dw_wide(dw_ref, dc_ref, gext, CK, hb, lanes)
        av = a_ref[...].astype(F32)
        _, vjp_glu = jax.vjp(lambda a1, a2: a1 * jax.nn.sigmoid(a2), av[:, :D], av[:, D:])
        da1, da2 = vjp_glu(dglu[...])
        da_ref[:, :D] = da1.astype(ACT)
        da_ref[:, D:] = da2.astype(ACT)
        pb_ref[0:1, :D] += _sum0(da1)
        pb_ref[0:1, D:] += _sum0(da2)
        du = _nt(da1, w1_ref[:, :D]) + _nt(da2, w1_ref[:, D:])
        u, vjp = jax.vjp(_mod, h_ref[...], mp_ref[0:1], mp_ref[1:2], mp_ref[2:3])
        ub_ref[...] = u.astype(ACT)
        dhn, dg, dsh, dsc = vjp(du)
        dho_ref[...] = dh_ref[...] + dhn
        pg_ref[0:1] += dg
        pg_ref[1:2] += dsh
        pg_ref[2:3] += dsc

    hp, hn = _halo(D, hb, n_rows)
    return _call(body, "cf1_bwd", (nt,),
                 [_rows(T, D), _rows(T, D), _rows(T, 2 * D), hp, _rows(T, D), hn, hp, _rows(T, D), hn,
                  _full(8, D), _full(32, D), _wfull(D, 2 * D)],
                 [_rows(T, D), _rows(T, 2 * D), _rows(T, D), _full(8, D), _full(8, 2 * D), _full(32, D)],
                 [_sds((n_rows, D)), _sds((n_rows, 2 * D), ACT), _sds((n_rows, D), ACT), _sds((8, D)),
                  _sds((8, 2 * D)), _sds((32, D))],
                 scratch=[pltpu.VMEM((T + 2 * hb, D), F32), pltpu.VMEM((T + 2 * hb, D), F32), pltpu.VMEM((T, D), F32)],
                 )(dh, h, a, dcv, dcv, dcv, glu, glu, glu, mp, wdw, w1)


def _sg_blocks():
    return [(c, g, slice(c * Q, (c + 1) * Q), slice(g * LANE, (g + 1) * LANE)) for c in range(T // Q) for g in range(SGG)]


IN_W = D + XBC + 32 + 2 * D
IN_LOC = IN_W // 4


def _win_split(shards):
    o1, o2, o3 = D, D + XBC, D + XBC + 32
    tr = 256

    def cols(s_ref, lo, hi):
        parts = []
        for j in range(4):
            a, b = max(lo, j * IN_LOC), min(hi, (j + 1) * IN_LOC)
            if a < b:
                parts.append(s_ref[j][:, a - j * IN_LOC:b - j * IN_LOC])
        return parts[0] if len(parts) == 1 else jnp.concatenate(parts, axis=1)

    def body(s_ref, wz_ref, wxbc_ref, wdt_ref, wuv_ref):
        wz_ref[...] = cols(s_ref, 0, o1)
        wxbc_ref[...] = cols(s_ref, o1, o2)
        dt = cols(s_ref, o2, o3)
        wdt_ref[...] = jnp.concatenate([dt, jnp.zeros((tr, LANE - 32), dt.dtype)], axis=1)
        wuv_ref[...] = cols(s_ref, o3, IN_W)

    dt_ = shards.dtype
    return _call(body, "win_split", (D // tr,), [pl.BlockSpec((4, tr, IN_LOC), lambda i: (0, i, 0))],
                 [_rows(tr, D), _rows(tr, XBC), _rows(tr, LANE), _rows(tr, 2 * D)],
                 [_sds((D, D), dt_), _sds((D, XBC), dt_), _sds((D, LANE), dt_), _sds((D, 2 * D), dt_)])(shards)


def _win_join(gz, gxbc, gdt, guv):
    tr = 256
    bounds = (0, D, D + XBC, D + XBC + 32, IN_W)

    def body(gz_ref, gx_ref, gd_ref, gu_ref, o_ref):
        segs = (gz_ref, gx_ref, gd_ref, gu_ref)
        for j in range(4):
            parts = []
            for k in range(4):
                a, b = max(bounds[k], j * IN_LOC), min(bounds[k + 1], (j + 1) * IN_LOC)
                if a < b:
                    parts.append(segs[k][:, a - bounds[k]:b - bounds[k]])
            full = parts[0] if len(parts) == 1 else jnp.concatenate(parts, axis=1)
            o_ref[j] = full.astype(jnp.bfloat16)

    return _call(body, "win_join", (D // tr,), [_rows(tr, D), _rows(tr, XBC), _rows(tr, LANE), _rows(tr, 2 * D)],
                 pl.BlockSpec((4, tr, IN_LOC), lambda i: (0, i, 0)), _sds((4, D, IN_LOC), jnp.bfloat16))(gz, gxbc, gdt, guv)


def _ctx_spec(nct):
    return pl.BlockSpec((T, D), lambda i: (jnp.minimum(i, nct - 1), 0))


def _hy1_fwd(ctx, x, mp2, wz, wuv, wxbc, wdt, lng, lnb, sgw, sgbt, nct):
    n_lat = x.shape[0]
    n_rows = ctx.shape[0] + n_lat

    def body(c_ref, x_ref, mp_ref, wz_ref, wuv_ref, wxbc_ref, wdt_ref, lng_ref, lnb_ref, sgw_ref, sgbt_ref,
             z_ref, uv_ref, xbcp_ref, dtr_ref, ysg_ref):
        hv = jnp.where(pl.program_id(0) < nct, c_ref[...], x_ref[...])
        u = _mod(hv, mp_ref[0:1], mp_ref[1:2], mp_ref[2:3]).astype(MXU)
        z_ref[...] = jnp.dot(u, wz_ref[...], preferred_element_type=F32)
        xbcp_ref[...] = jnp.dot(u, wxbc_ref[...], preferred_element_type=F32)
        dtr_ref[...] = jnp.dot(u, wdt_ref[...], preferred_element_type=F32)
        uv = jnp.dot(u, wuv_ref[...], preferred_element_type=F32)
        uv_ref[...] = uv
        gate = _gelu(uv[:, :D])
        vln = _ln(_gelu(uv[:, D:]), lng_ref[...], lnb_ref[...]).astype(MXU)
        for _, g, rs, ls in _sg_blocks():
            s = jnp.dot(sgw_ref[g], vln[rs, ls], preferred_element_type=F32) + sgbt_ref[:, g:g + 1]
            ysg_ref[rs, ls] = (gate[rs, ls] * s).astype(ACT)

    mspec = pl.BlockSpec((None, 8, D), lambda i: (jnp.where(i < nct, 0, 1), 0, 0))
    return _call(body, "hy1_fwd", (n_rows // T,),
                 [_ctx_spec(nct), _rows_lat(T, D, nct), mspec, _wfull(D, D), _wfull(D, 2 * D), _wfull(D, XBC), _wfull(D, LANE),
                  _full(1, D), _full(1, D), _full(SGG, Q, Q), _full(Q, LANE)],
                 [_rows(T, D), _rows(T, 2 * D), _rows(T, XBC), _rows(T, LANE), _rows_lat(T, D, nct)],
                 [_sds((n_rows, D)), _sds((n_rows, 2 * D)), _sds((n_rows, XBC)), _sds((n_rows, LANE)),
                  _sds((n_lat, D), ACT)])(ctx, x, mp2, wz, wuv, wxbc, wdt, lng, lnb, sgw, sgbt)


def _hy1_bwd(ctx, x, uv, dz, dxbcp, ddf, ddb, dysg, dres, mp2, wz, wuv, wxbc, wdt, lng, lnb, sgw, sgbt, nct):
    n_lat = dres.shape[0]
    n_rows = ctx.shape[0] + n_lat

    def body(c_ref, x_ref, uv_ref, dz_ref, dxbcp_ref, ddf_ref, ddb_ref, dysg_ref, dres_ref, mp_ref, wz_ref, wuv_ref,
             wxbc_ref, wdt_ref, lng_ref, lnb_ref, sgw_ref, sgbt_ref,
             dho_ref, ub_ref, duv_ref, ddt_ref, pg2_ref, pl_ref, dsgw_ref, dsgb_ref, dgate_s, dvln_s):
        i = pl.program_id(0)

        @pl.when(i == 0)
        def _():
            pg2_ref[...] = jnp.zeros_like(pg2_ref)
            pl_ref[...] = jnp.zeros_like(pl_ref)
            dsgw_ref[...] = jnp.zeros_like(dsgw_ref)
            dsgb_ref[...] = jnp.zeros_like(dsgb_ref)
        uv = uv_ref[...]

        def f_sg(ug, uvv, g_, b_):
            return _gelu(ug), _ln(_gelu(uvv), g_, b_)
        (gate, vln), vjp_sg = jax.vjp(f_sg, uv[:, :D], uv[:, D:], lng_ref[...], lnb_ref[...])
        vlnb = vln.astype(MXU)
        lane = lax.broadcasted_iota(jnp.int32, (Q, LANE), 1)
        dsgb = jnp.zeros((Q, LANE), F32)
        for _, g, rs, ls in _sg_blocks():
            s = jnp.dot(sgw_ref[g], vlnb[rs, ls], preferred_element_type=F32) + sgbt_ref[:, g:g + 1]
            dyb = dysg_ref[rs, ls]
            dgate_s[rs, ls] = dyb * s
            ds = dyb * gate[rs, ls]
            dvln_s[rs, ls] = _tn_dot(sgw_ref[g], ds)
            dsgw_ref[g] += _nt(ds, vlnb[rs, ls])
            dsgb = dsgb + jnp.where(lane == g, jnp.sum(ds, axis=1, keepdims=True), 0.0)
        dsgb_ref[...] += dsgb
        dug, duvv, dlng, dlnb = vjp_sg((dgate_s[...], dvln_s[...]))
        pl_ref[0:1] += dlng
        pl_ref[1:2] += dlnb
        duv_ref[:, :D] = dug.astype(ACT)
        duv_ref[:, D:] = duvv.astype(ACT)
        ddt = (ddf_ref[...] + ddb_ref[...]).astype(MXU)
        ddt_ref[...] = ddt.astype(ACT)
        du = (_nt(dz_ref[...], wz_ref[...]) + _nt(dug, wuv_ref[:, :D]) + _nt(duvv, wuv_ref[:, D:])
              + _nt(dxbcp_ref[...], wxbc_ref[...]) + _nt(ddt, wdt_ref[...]))
        hv = jnp.where(i < nct, c_ref[...], x_ref[...])
        u, vjp = jax.vjp(_mod, hv, mp_ref[0:1], mp_ref[1:2], mp_ref[2:3])
        ub_ref[...] = u.astype(ACT)
        dhn, dg, dsh, dsc = vjp(du)
        dho_ref[...] = dres_ref[...] + dhn
        is_ctx = i < nct
        for k, val in enumerate((dg, dsh, dsc)):
            pg2_ref[0, k:k + 1] += jnp.where(is_ctx, val, 0.0)
            pg2_ref[1, k:k + 1] += jnp.where(is_ctx, 0.0, val)

    mspec = pl.BlockSpec((None, 8, D), lambda i: (jnp.where(i < nct, 0, 1), 0, 0))
    return _call(body, "hy1_bwd", (n_rows // T,),
                 [_ctx_spec(nct), _rows_lat(T, D, nct), _rows(T, 2 * D), _rows(T, D), _rows(T, XBC), _rows(T, LANE),
                  _rows(T, LANE), _rows(T, D),
                  _rows_lat(T, D, nct), mspec, _wfull(D, D), _wfull(D, 2 * D), _wfull(D, XBC), _wfull(D, LANE),
                  _full(1, D), _full(1, D), _full(SGG, Q, Q), _full(Q, LANE)],
                 [_rows_lat(T, D, nct), _rows(T, D), _rows(T, 2 * D), _rows(T, LANE), _full(2, 8, D), _full(8, D),
                  _full(SGG, Q, Q), _full(Q, LANE)],
                 [_sds((n_lat, D)), _sds((n_rows, D), ACT), _sds((n_rows, 2 * D), ACT), _sds((n_rows, LANE), ACT),
                  _sds((2, 8, D)), _sds((8, D)), _sds((SGG, Q, Q)), _sds((Q, LANE))],
                 scratch=[pltpu.VMEM((T, D), F32), pltpu.VMEM((T, D), F32)],
                 )(ctx, x, uv, dz, dxbcp, ddf, ddb, dysg, dres, mp2, wz, wuv, wxbc, wdt, lng, lnb, sgw, sgbt)


def _seq_edges(i, nct, nt):
    return (i == 0) | (i == nct), (i == nct - 1) | (i == nt - 1)


def _cv5_fwd(xbcp, w, b, nct):
    n_rows = xbcp.shape[0]
    nt = n_rows // T
    hb = 8

    def body(p_ref, c_ref, n_ref, w_ref, b_ref, o_ref, ext):
        first, last = _seq_edges(pl.program_id(0), nct, nt)
        _fill_ext(ext, p_ref, c_ref, n_ref, hb, first, last)
        for c in range(XBC // LANE):
            lanes = slice(c * LANE, (c + 1) * LANE)
            o_ref[:, lanes] = _silu(_conv(ext, w_ref, SK, hb, lanes) + b_ref[:, lanes])

    hp, hn = _halo(XBC, hb, n_rows)
    return _call(body, "cv5_fwd", (nt,), [hp, _rows(T, XBC), hn, _full(8, XBC), _full(1, XBC)],
                 _rows(T, XBC), _sds((n_rows, XBC)), scratch=[pltpu.VMEM((T + 2 * hb, XBC), F32)])(xbcp, xbcp, xbcp, w, b)


def _cv5_bwd1(xbcp, dxf, dxb, w, b, nct):
    n_rows = xbcp.shape[0]
    nt = n_rows // T
    hb = 8

    def body(p_ref, c_ref, n_ref, dxf_ref, dxb_ref, w_ref, b_ref, o_ref, pg_ref, ext):
        i = pl.program_id(0)
        first, last = _seq_edges(i, nct, nt)
        _fill_ext(ext, p_ref, c_ref, n_ref, hb, first, last)

        @pl.when(i == 0)
        def _():
            pg_ref[...] = jnp.zeros_like(pg_ref)
        for c in range(XBC // LANE):
            lanes = slice(c * LANE, (c + 1) * LANE)
            cv = _conv(ext, w_ref, SK, hb, lanes) + b_ref[:, lanes]
            sg = jax.nn.sigmoid(cv)
            dcv = (dxf_ref[:, lanes] + dxb_ref[:, lanes]) * (sg * (1.0 + cv * (1.0 - sg)))
            o_ref[:, lanes] = dcv
            pg_ref[0:1, lanes] += _sum0(dcv)

    hp, hn = _halo(XBC, hb, n_rows)
    return _call(body, "cv5_bwd1", (nt,),
                 [hp, _rows(T, XBC), hn, _rows(T, XBC), _rows(T, XBC), _full(8, XBC), _full(1, XBC)],
                 [_rows(T, XBC), _full(8, XBC)], [_sds((n_rows, XBC)), _sds((8, XBC))],
                 scratch=[pltpu.VMEM((T + 2 * hb, XBC), F32)])(xbcp, xbcp, xbcp, dxf, dxb, w, b)


def _cv5_bwd2(dcv, xbcp, w, nct):
    n_rows = xbcp.shape[0]
    nt = n_rows // T
    hb = 8

    def body(dp_ref, dc_ref, dn_ref, xp_ref, xc_ref, xn_ref, w_ref, o_ref, dw_ref, dext, xext):
        i = pl.program_id(0)
        first, last = _seq_edges(i, nct, nt)
        _fill_ext(dext, dp_ref, dc_ref, dn_ref, hb, first, last)
        _fill_ext(xext, xp_ref, xc_ref, xn_ref, hb, first, last)

        @pl.when(i == 0)
        def _():
            dw_ref[...] = jnp.zeros_like(dw_ref)
        for c in range(XBC // LANE):
            lanes = slice(c * LANE, (c + 1) * LANE)
            o_ref[:, lanes] = _conv_tr(dext, w_ref, SK, hb, lanes).astype(ACT)
            _conv_dw(dw_ref, dc_ref, xext, SK, hb, lanes)

    hp, hn = _halo(XBC, hb, n_rows)
    return _call(body, "cv5_bwd2", (nt,),
                 [hp, _rows(T, XBC), hn, hp, _rows(T, XBC), hn, _full(8, XBC)],
                 [_rows(T, XBC), _full(8, XBC)], [_sds((n_rows, XBC), ACT), _sds((8, XBC))],
                 scratch=[pltpu.VMEM((T + 2 * hb, XBC), F32), pltpu.VMEM((T + 2 * hb, XBC), F32)],
                 )(dcv, dcv, dcv, xbcp, xbcp, xbcp, w)


def _scan_order(nc, ncc, rev):
    if not rev:
        return lambda s: s
    return lambda s: jnp.where(s < ncc, ncc - 1 - s, nc - 1 - (s - ncc))


def _ssd_prep(dtr, sp, rev):
    dt = jax.nn.softplus(dtr + sp[0:1])
    a_neg = -jnp.exp(sp[1:2])
    r = lax.broadcasted_iota(jnp.int32, (Q, Q), 0)
    c = lax.broadcasted_iota(jnp.int32, (Q, Q), 1)
    msk = (c >= r) if rev else (c <= r)
    tri = msk.astype(F32)
    acs = jnp.dot(tri, dt * a_neg, precision=HI, preferred_element_type=F32)
    last = 0 if rev else Q - 1
    return dt, a_neg, acs, msk, tri, last


def _pair_sel(arr, lo, m, lane_lt):
    h0 = lo + 2 * m
    return jnp.where(lane_lt, arr[:, h0:h0 + 1], arr[:, h0 + 1:h0 + 2])


def _head_lanes(row, lo, g):
    lane = lax.broadcasted_iota(jnp.int32, (1, 512), 1)
    out = jnp.zeros((1, 512), F32)
    for k in range(8):
        h = lo + 8 * g + k
        out = jnp.where((lane >= 64 * k) & (lane < 64 * (k + 1)), row[:, h:h + 1], out)
    return out


def _halves(v, lane_lt):
    return jnp.concatenate([jnp.where(lane_lt, v, 0.0), jnp.where(lane_lt, 0.0, v)], axis=0)


def _ssd_fwd(xbc, dtr, sp, ncc, rev):
    n_rows = xbc.shape[0]
    nc = n_rows // Q
    lo = 16 if rev else 0
    order = _scan_order(nc, ncc, rev)

    def body(x_ref, dtr_ref, sp_ref, y_ref, hin_ref, st):
        @pl.when(pl.program_id(0) == 0)
        def _():
            st[...] = jnp.zeros_like(st)
        dt, _, acs, msk, _, last = _ssd_prep(dtr_ref[...], sp_ref[...], rev)
        acs_t, dt_t = acs.T, dt.T
        eacs = jnp.exp(acs)
        eal = jnp.exp(acs[last:last + 1, :])
        tew = jnp.exp(acs[last:last + 1, :] - acs) * dt
        lane_lt = lax.broadcasted_iota(jnp.int32, (Q, LANE), 1) < 64
        for g in range(2):
            gl = slice(g * 512, (g + 1) * 512)
            bg = x_ref[:, 1024 + g * 128:1152 + g * 128]
            cg = x_ref[:, 1280 + g * 128:1408 + g * 128]
            s_g = _nt(cg, bg)
            h_t = st[:, gl]
            hin_ref[:, gl] = h_t
            yoff = _nn(cg, h_t)
            xw = []
            for mm in range(4):
                m = 4 * g + mm
                ls = slice(m * LANE, (m + 1) * LANE)
                x2 = x_ref[:, ls]
                ws = []
                for hh in range(2):
                    h = lo + 2 * m + hh
                    lm = jnp.exp(jnp.where(msk, acs[:, h:h + 1] - acs_t[h:h + 1, :], -jnp.inf))
                    ws.append(s_g * lm * dt_t[h:h + 1, :])
                y2 = _nn(jnp.concatenate(ws, axis=1), _halves(x2, lane_lt))
                y_ref[:, ls] = y2 + yoff[:, mm * LANE:(mm + 1) * LANE] * _pair_sel(eacs, lo, m, lane_lt)
                xw.append(x2 * _pair_sel(tew, lo, m, lane_lt))
            st[:, gl] = _head_lanes(eal, lo, g) * h_t + _tn_dot(bg, jnp.concatenate(xw, axis=1))

    return _call(body, "ssd_fwd_r" if rev else "ssd_fwd_f", (nc,),
                 [pl.BlockSpec((Q, XBC), lambda s: (order(s), 0)), pl.BlockSpec((Q, LANE), lambda s: (order(s), 0)),
                  _full(8, LANE)],
                 [pl.BlockSpec((Q, D), lambda s: (order(s), 0)), pl.BlockSpec((None, LANE, D), lambda s: (order(s), 0, 0))],
                 [_sds((n_rows, D)), _sds((nc, LANE, D))], scratch=[pltpu.VMEM((LANE, D), F32)])(xbc, dtr, sp)


def _ssd_bwd(xbc, dtr, dy, hin, sp, dl, eh, ncc, rev):
    n_rows = xbc.shape[0]
    nc = n_rows // Q
    lo = 16 if rev else 0
    fwd_order = _scan_order(nc, ncc, rev)
    order = lambda s: fwd_order(nc - 1 - s)
    with_skip = not rev

    def body(x_ref, dtr_ref, dy_ref, hin_ref, sp_ref, dl_ref, eh_ref, dx_ref, ddtr_ref, pg_ref, dst):
        @pl.when(pl.program_id(0) == 0)
        def _():
            dst[...] = jnp.zeros_like(dst)
            pg_ref[...] = jnp.zeros_like(pg_ref)
        dtr_v = dtr_ref[...]
        dt, a_neg, acs, msk, tri, last = _ssd_prep(dtr_v, sp_ref[...], rev)
        acs_t = acs.T
        r = lax.broadcasted_iota(jnp.int32, (Q, Q), 0)
        c = lax.broadcasted_iota(jnp.int32, (Q, Q), 1)
        msk_t = (c <= r) if rev else (c >= r)
        eacs = jnp.exp(acs)
        eal = jnp.exp(acs[last:last + 1, :])
        te = jnp.exp(acs[last:last + 1, :] - acs)
        lane = lax.broadcasted_iota(jnp.int32, (Q, LANE), 1)
        lane1 = lax.broadcasted_iota(jnp.int32, (1, LANE), 1)
        lane_lt = lane < 64
        dacs = jnp.zeros((Q, LANE), F32)
        ddt_x = jnp.zeros((Q, LANE), F32)
        dlast = jnp.zeros((1, LANE), F32)
        hs_rows = []
        sub16 = lax.broadcasted_iota(jnp.int32, (16, Q), 0)
        dacs_t = jnp.zeros((16, Q), F32)
        for g in range(2):
            gl = slice(g * 512, (g + 1) * 512)
            bg = x_ref[:, 1024 + g * 128:1152 + g * 128]
            cg = x_ref[:, 1280 + g * 128:1408 + g * 128]
            s_g = _nt(cg, bg)
            s_gt = _nt(bg, cg)
            h_t, dh_t = hin_ref[:, gl], dst[:, gl]
            bh = _nn(bg, dh_t)
            yoff = _nn(cg, h_t)
            d_s = jnp.zeros((Q, Q), F32)
            edy, exd = [], []
            for mm in range(4):
                m = 4 * g + mm
                ls = slice(m * LANE, (m + 1) * LANE)
                x2, dy2 = x_ref[:, ls], dy_ref[:, ls]
                bh2 = bh[:, mm * LANE:(mm + 1) * LANE]
                dtm, em, eam = (_pair_sel(v, lo, m, lane_lt) for v in (dt, te, eacs))
                xd2 = x2 * dtm
                lms, mts = [], []
                for hh in range(2):
                    h = lo + 2 * m + hh
                    col, row = acs[:, h:h + 1], acs_t[h:h + 1, :]
                    lms.append(jnp.exp(jnp.where(msk, col - row, -jnp.inf)))
                    mts.append(s_gt * jnp.exp(jnp.where(msk_t, row - col, -jnp.inf)))
                dy_st = _halves(dy2, lane_lt)
                dxd2 = em * bh2 + _nn(jnp.concatenate(mts, axis=1), dy_st)
                dm_st = _nt(dy_st, xd2)
                dmt_st = _nt(_halves(xd2, lane_lt), dy2)
                d_s = d_s + dm_st[:Q] * lms[0] + dm_st[Q:] * lms[1]
                v1, v2, v3 = dy2 * yoff[:, mm * LANE:(mm + 1) * LANE] * eam, dxd2 * x2, xd2 * bh2 * em
                for hh in range(2):
                    h = lo + 2 * m + hh
                    half = lane_lt == (hh == 0)
                    g_rows = _sum0(dmt_st[hh * Q:(hh + 1) * Q] * mts[hh]) - _sum0(dm_st[hh * Q:(hh + 1) * Q] * s_g * lms[hh])
                    dacs_t = jnp.where(sub16 == 2 * m + hh, g_rows, dacs_t)
                    r1 = jnp.sum(jnp.where(half, v1, 0.0), axis=1, keepdims=True)
                    r2 = jnp.sum(jnp.where(half, v2, 0.0), axis=1, keepdims=True)
                    r3 = jnp.sum(jnp.where(half, v3, 0.0), axis=1, keepdims=True)
                    dacs = dacs + jnp.where(lane == h, r1 - r3, 0.0)
                    ddt_x = ddt_x + jnp.where(lane == h, r2, 0.0)
                    dlast = dlast + jnp.where(lane1 == h, _sum0(r3), 0.0)
                dx2 = dxd2 * dtm
                if with_skip:
                    dx2 = dx2 + dl_ref[:, ls] * dy2
                dx_ref[:, ls] = dx2
                edy.append(eam * dy2)
                exd.append(em * xd2)
            edy, exd = jnp.concatenate(edy, axis=1), jnp.concatenate(exd, axis=1)
            hs_rows.append(_sum0(h_t * dh_t))
            dst[:, gl] = _head_lanes(eal, lo, g) * dh_t + _tn_dot(cg, edy)
            dx_ref[:, 1024 + g * 128:1152 + g * 128] = _tn_dot(d_s, cg) + _nt(exd, dh_t)
            dx_ref[:, 1280 + g * 128:1408 + g * 128] = _nn(d_s, bg) + _nt(edy, h_t)
        hs = jnp.broadcast_to(jnp.concatenate(hs_rows, axis=1), (8, D))
        hsum = jnp.dot(hs, eh_ref[...], precision=HI, preferred_element_type=F32)[0:1]
        dlast = dlast + eal * hsum
        dacs = dacs + jnp.concatenate([jnp.zeros((lo, Q), F32)] * (lo > 0) + [dacs_t, jnp.zeros((LANE - 16 - lo, Q), F32)],
                                      axis=0).T
        rowi = lax.broadcasted_iota(jnp.int32, (Q, LANE), 0)
        dacs = dacs + jnp.where(rowi == last, dlast, 0.0)
        da = lax.dot_general(tri, dacs, (((0,), (0,)), ((), ())), precision=HI, preferred_element_type=F32)
        ddt = ddt_x + da * a_neg
        mine = (lane >= lo) & (lane < lo + 16)
        ddtr = jnp.where(mine, ddt * jax.nn.sigmoid(dtr_v + sp_ref[0:1]), 0.0)
        ddtr_ref[...] = ddtr
        pg_ref[0:1] += _sum0(ddtr)
        pg_ref[1:2] += jnp.where(mine[0:1], _sum0(da * dt) * a_neg, 0.0)

    blk = lambda w_: pl.BlockSpec((Q, w_), lambda s: (order(s), 0))
    return _call(body, "ssd_bwd_r" if rev else "ssd_bwd_f", (nc,),
                 [blk(XBC), blk(LANE), blk(D), pl.BlockSpec((None, LANE, D), lambda s: (order(s), 0, 0)),
                  _full(8, LANE), _full(1, D), _full(D, LANE)],
                 [blk(XBC), blk(LANE), _full(8, LANE)],
                 [_sds((n_rows, XBC)), _sds((n_rows, LANE)), _sds((8, LANE))],
                 scratch=[pltpu.VMEM((LANE, D), F32)])(xbc, dtr, dy, hin, sp, dl, eh)


def _hy4_fwd(h, yf, yb, xbc, z, ysg, mp, dl, ng, wout, nct):
    n_rows = h.shape[0]

    def body(h_ref, yf_ref, yb_ref, xs_ref, z_ref, ysg_ref, mp_ref, dl_ref, ng_ref, wout_ref, hn_ref, yssd_ref, out_ref):
        ytot = yf_ref[...] + yb_ref[...] + dl_ref[...] * xs_ref[...]
        yssd = _gate_norm(ytot, z_ref[...], ng_ref[...]).astype(MXU)
        yssd_ref[...] = yssd.astype(ACT)
        out = (jnp.dot(yssd, wout_ref[0:D, :], preferred_element_type=F32)
               + jnp.dot(ysg_ref[...].astype(MXU), wout_ref[D:2 * D, :], preferred_element_type=F32))
        out_ref[...] = out
        hn_ref[...] = h_ref[...] + mp_ref[3:4] * out

    return _call(body, "hy4_fwd", (n_rows // T,),
                 [_rows(T, D), _rows(T, D, nct), _rows(T, D, nct), _rows(T, D, nct), _rows(T, D, nct), _rows(T, D),
                  _full(8, D), _full(1, D), _full(1, D), _wfull(2 * D, D)],
                 [_rows(T, D), _rows(T, D), _rows(T, D)],
                 [_sds((n_rows, D)), _sds((n_rows, D), ACT), _sds((n_rows, D))])(h, yf, yb, xbc, z, ysg, mp, dl, ng, wout)


def _hy4_bwd(dh, out, yf, yb, xbc, z, mp, dl, ng, wout, nct):
    n_lat = dh.shape[0]
    n_rows = yf.shape[0]

    def body(dh_ref, out_ref, yf_ref, yb_ref, xs_ref, z_ref, mp_ref, dl_ref, ng_ref, wout_ref,
             dy_ref, dz_ref, dysg_ref, doutb_ref, pg_ref):
        i = pl.program_id(0)

        @pl.when(i == 0)
        def _():
            pg_ref[...] = jnp.zeros_like(pg_ref)

        @pl.when(i < nct)
        def _():
            dy_ref[...] = jnp.zeros_like(dy_ref)
            dz_ref[...] = jnp.zeros_like(dz_ref)
            dysg_ref[...] = jnp.zeros_like(dysg_ref)
            doutb_ref[...] = jnp.zeros_like(doutb_ref)

        @pl.when(i >= nct)
        def _():
            dhp = dh_ref[...]
            doutb = (mp_ref[3:4] * dhp).astype(MXU)
            doutb_ref[...] = doutb.astype(ACT)
            dysg_ref[...] = _nt(doutb, wout_ref[D:2 * D, :])
            dyssd = _nt(doutb, wout_ref[0:D, :])
            xs = xs_ref[...]
            ytot = yf_ref[...] + yb_ref[...] + dl_ref[...] * xs
            _, vjp = jax.vjp(_gate_norm, ytot, z_ref[...], ng_ref[...])
            dytot, dz, dng = vjp(dyssd)
            dy_ref[...] = dytot
            dz_ref[...] = dz.astype(ACT)
            pg_ref[0:1] += _sum0(dhp * out_ref[...])
            pg_ref[1:2] += dng
            pg_ref[2:3] += _sum0(dytot * xs)

    return _call(body, "hy4_bwd", (n_rows // T,),
                 [_rows_lat(T, D, nct), _rows_lat(T, D, nct), _rows(T, D), _rows(T, D), _rows(T, D), _rows(T, D),
                  _full(8, D), _full(1, D), _full(1, D), _wfull(2 * D, D)],
                 [_rows(T, D), _rows(T, D), _rows(T, D), _rows_lat(T, D, nct), _full(8, D)],
                 [_sds((n_rows, D)), _sds((n_rows, D), ACT), _sds((n_rows, D)), _sds((n_lat, D), ACT), _sds((8, D))],
                 )(dh, out, yf, yb, xbc, z, mp, dl, ng, wout)


def _loss_bwd(h, tgt, fng):
    n_rows = h.shape[0]

    def body(h_ref, t_ref, g_ref, dh_ref, pg_ref, ls_ref):
        @pl.when(pl.program_id(0) == 0)
        def _():
            pg_ref[...] = jnp.zeros_like(pg_ref)
            ls_ref[...] = jnp.zeros_like(ls_ref)
        hv = h_ref[...]
        g = g_ref[...]
        r = lax.rsqrt(jnp.mean(hv * hv, axis=-1, keepdims=True) + EPS)
        n = hv * r
        e = n * g - t_ref[...]
        ls_ref[...] += 0.5 * jnp.sum(jnp.sum(e * e, axis=1, keepdims=True), axis=0, keepdims=True) * (1.0 / D)
        dyv = e * (1.0 / D)
        pg_ref[0:1] += _sum0(dyv * n)
        dn = dyv * g
        dh_ref[...] = r * (dn - n * jnp.mean(dn * n, axis=-1, keepdims=True))

    return _call(body, "loss_bwd", (n_rows // T,), [_rows(T, D), _rows(T, D), _full(1, D)],
                 [_rows(T, D), _full(8, D), _full(8, LANE)],
                 [_sds((n_rows, D)), _sds((8, D)), _sds((8, LANE))])(h, tgt, fng)


def _pad_rows(a, rows):
    return jnp.concatenate([a, jnp.zeros((rows - a.shape[0],) + a.shape[1:], a.dtype)], axis=0)


def _mp(*rows):
    return _pad_rows(jnp.stack(rows, axis=0), 8)


def _local_step(x, ctx, tgt, ada, cada0, w, late_w=None, early_grads=None):
    n_lat, n_ctx = x.shape[0], ctx.shape[0]
    nct, ncc = n_ctx // T, n_ctx // Q
    a0 = [ada[0, k * D:(k + 1) * D] for k in range(6)]
    a1 = [ada[1, k * D:(k + 1) * D] for k in range(6)]
    c0 = [cada0[k * D:(k + 1) * D] for k in range(6)]
    g = {}

    mp2 = jnp.stack([_mp(w["norm_mix_g"][0], c0[0], c0[1]), _mp(w["norm_mix_g"][0], a0[0], a0[1], a0[2])], axis=0)
    mp_l0 = mp2[1]
    sgbt = _pad_cols(w["sg_b"][0].T, LANE)
    lng, lnb = w["sg_ln_g"][0][None], w["sg_ln_b"][0][None]
    z, uv, xbcp, dtr, ysg = _hy1_fwd(ctx, x, mp2, w["wz"], w["wuv"], w["wxbc"], w["wdt"], lng, lnb, w["sg_w"], sgbt, nct)
    cw = _pad_rows(w["ssd_conv_w"][0], 8)
    cb = w["ssd_conv_b"][0][None]
    xbc = _cv5_fwd(xbcp, cw, cb, nct)
    sp = _pad_rows(jnp.stack([_pad_cols(w["ssd_dt_bias"][0].reshape(1, 32), LANE)[0],
                              _pad_cols(w["ssd_a_log"][0].reshape(1, 32), LANE)[0]], axis=0), 8)
    dl = jnp.repeat(w["ssd_d"][0], 64)[None]
    ng = w["ssd_norm_g"][0][None]
    yf, hin_f = _ssd_fwd(xbc, dtr, sp, ncc, False)
    yb, hin_b = _ssd_fwd(xbc, dtr, sp, ncc, True)
    if late_w is not None:
        w = {**w, **late_w(yb)}
    h1, yssd, out0 =_hy4_fwd(x, yf, yb, xbc, z, ysg, mp_l0, dl, ng, w["hy_w_out"], nct)

    mpm0 = _mp(w["norm_mlp_g"][0], a0[3], a0[4], a0[5])
    h2, am0, ym0 = _mlp_fwd(h1, mpm0, w["wpack"], 0, "mlp0_fwd")

    mpc = _mp(w["norm_mix_g"][1], a1[0], a1[1], a1[2])
    wdw = _pad_rows(w["cf_w_dw"][0], 32)
    glu, acf = _cf1_fwd(h2, mpc, w["cf_w_pw1"], w["cf_b_pw1"])
    h3, cv, scf, ycf = _cf2_fwd(h2, glu, mpc, wdw, w["cf_b_dw"], w["cf_ln_g"], w["cf_ln_b"], w["cf_w_pw2"], w["cf_b_pw2"])

    mpm1 = _mp(w["norm_mlp_g"][1], a1[3], a1[4], a1[5])
    h4, am1, ym1 = _mlp_fwd(h3, mpm1, w["wpack"], 1, "mlp1_fwd")

    dh4, pg_f, ls = _loss_bwd(h4, tgt, w["final_norm_g"][None])
    loss = ls[0, 0]
    g["final_norm_g"] = pg_f[0]

    gp = {}
    dh3, da1, dy1, u1, pgm1 = _mlp_bwd(dh4, h3, am1, ym1, mpm1, w["wpack"], 1, "mlp1_bwd")
    gw1_1 = _tn(u1, da1, "tn_mlp1_w1", shard=("col", 1024))
    gw2_1 = _tn(am1, dy1, "tn_mlp1_w2", relu2=True, shard=("row", 1024))

    dcv, dycf, pgc2 = _cf2_bwd(dh3, ycf, cv, mpc, w["cf_ln_g"], w["cf_ln_b"], w["cf_w_pw2"])
    gp["cf_w_pw2"] = _tn(scf, dycf, "tn_cf_pw2", shard=("row", 256))
    dh2, dacf, ucf, pgc1, pbc1, dwdw = _cf1_bwd(dh3, h2, acf, dcv, glu, mpc, wdw, w["cf_w_pw1"])
    gp["cf_w_pw1"] = _tn(ucf, dacf, "tn_cf_pw1", shard=("col", 512)).reshape(4, 512, 1024)
    g["cf_b_pw2"], g["cf_ln_g"], g["cf_ln_b"], g["cf_b_dw"] = pgc2[1], pgc2[2], pgc2[3], pgc2[4]
    g["cf_b_pw1"] = pbc1[0]
    g["cf_w_dw"] = dwdw[:CK]

    dh1, da0, dy0, u0, pgm0 = _mlp_bwd(dh2, h1, am0, ym0, mpm0, w["wpack"], 0, "mlp0_bwd")
    gp["mlp_w1"] = jnp.concatenate([_tn(u0, da0, "tn_mlp0_w1", shard=("col", 1024)), gw1_1], axis=1)
    gp["mlp_w2"] = jnp.concatenate([_tn(am0, dy0, "tn_mlp0_w2", relu2=True, shard=("row", 1024)), gw2_1], axis=1)
    g["norm_mlp_g"] = jnp.stack([pgm0[0], pgm1[0]])

    dyt, dz, dysg, doutb, pg4 = _hy4_bwd(dh1, out0, yf, yb, xbc, z, mp_l0, dl, ng, w["hy_w_out"], nct)
    gp["hy_w_out"] = jnp.concatenate([_tn(yssd, doutb, "tn_out_ssd", shard=("row", 512)),
                                      _tn(ysg, doutb, "tn_out_sg", shard=("row", 512))], axis=0)
    if early_grads is not None:
        sp = sp + early_grads(gp)
    head_of_lane = jnp.arange(D, dtype=jnp.int32)[:, None] // 64
    col = jnp.arange(LANE, dtype=jnp.int32)[None, :]
    dxf, ddf, pgsf = _ssd_bwd(xbc, dtr, dyt, hin_f, sp, dl, (col == head_of_lane).astype(F32), ncc, False)
    dxb, ddb, pgsb = _ssd_bwd(xbc, dtr, dyt, hin_b, sp, dl, (col == head_of_lane + 16).astype(F32), ncc, True)
    dcv5, pgcb = _cv5_bwd1(xbcp, dxf, dxb, cw, cb, nct)
    dxbcp, dcw = _cv5_bwd2(dcv5, xbcp, cw, nct)
    dx, ucat, duv, ddt, pg2, pln, dsgw, dsgbt = _hy1_bwd(
        ctx, x, uv, dz, dxbcp, ddf, ddb, dysg, dh1, mp2, w["wz"], w["wuv"], w["wxbc"], w["wdt"], lng, lnb, w["sg_w"], sgbt, nct)
    gp["hy_w_in"] = _win_join(_tn(ucat, dz, "tn_in_z"), _tn(ucat, dxbcp, "tn_in_xbc"), _tn(ucat, ddt, "tn_in_dt"),
                              _tn(ucat, duv, "tn_in_uv"))
    g["ssd_conv_w"], g["ssd_conv_b"] = dcw[:SK], pgcb[0]
    pgs = pgsf + pgsb
    g["ssd_dt_bias"], g["ssd_a_log"] = pgs[0, :32].reshape(2, 16), pgs[1, :32].reshape(2, 16)
    g["ssd_d"] = jnp.sum(pg4[2].reshape(16, 64), axis=1)
    g["ssd_norm_g"] = pg4[1]
    g["sg_ln_g"], g["sg_ln_b"] = pln[0], pln[1]
    g["sg_w"], g["sg_b"] = dsgw, dsgbt[:, :SGG].T
    g["norm_mix_g"] = jnp.stack([pg2[0, 0] + pg2[1, 0], pgc1[0]])

    zero = jnp.zeros((D,), F32)
    d_ada = jnp.stack([jnp.concatenate([pg2[1, 1], pg2[1, 2], pg4[0], pgm0[1], pgm0[2], pgm0[3]]),
                       jnp.concatenate([pgc1[1], pgc1[2], pgc2[0], pgm1[1], pgm1[2], pgm1[3]])])
    d_cada0 = jnp.concatenate([pg2[0, 1], pg2[0, 2], zero, zero, zero, zero])
    g["pieces"] = gp
    return loss, dx, g, d_ada, d_cada0


def _pad_cols(a, cols):
    return jnp.concatenate([a, jnp.zeros(a.shape[:-1] + (cols - a.shape[-1],), a.dtype)], axis=-1)


MESH = pl.DeviceIdType.MESH
ANY = pl.BlockSpec(memory_space=pl.ANY)
IN_VMEM = pl.BlockSpec(memory_space=pltpu.VMEM)


def _coords():
    return lax.axis_index("x"), lax.axis_index("y"), lax.axis_index("c")


def _ag8(x, name):
    r, wd = x.shape

    def body(x_ref, o_ref, send, recv, lsem):
        mx, my, mc = _coords()
        me = 4 * mx + 2 * my + mc
        mine = pltpu.make_async_copy(x_ref, o_ref.at[me], lsem)
        mine.start()
        sent, peers = [], []
        for k in range(1, 8):
            px = 1 - mx if k & 4 else mx
            py = 1 - my if k & 2 else my
            pc = 1 - mc if k & 1 else mc
            cp = pltpu.make_async_remote_copy(src_ref=x_ref, dst_ref=o_ref.at[me], send_sem=send.at[k - 1],
                                              recv_sem=recv.at[k - 1], device_id=(px, py, pc), device_id_type=MESH)
            cp.start()
            sent.append(cp)
            peers.append((4 * px + 2 * py + pc, (px, py, pc)))
        for k in range(1, 8):
            slot, peer = peers[k - 1]
            pltpu.make_async_remote_copy(src_ref=x_ref, dst_ref=o_ref.at[slot], send_sem=send.at[k - 1],
                                         recv_sem=recv.at[k - 1], device_id=peer, device_id_type=MESH).wait_recv()
        for cp in sent:
            cp.wait_send()
        mine.wait()

    return pl.pallas_call(
        body, name=name, out_shape=_sds((8, r, wd), x.dtype), in_specs=[IN_VMEM], out_specs=IN_VMEM,
        scratch_shapes=[pltpu.SemaphoreType.DMA((7,)), pltpu.SemaphoreType.DMA((7,)), pltpu.SemaphoreType.DMA(())],
        compiler_params=pltpu.CompilerParams(vmem_limit_bytes=VMEM_LIMIT))(x)


HBM = pl.BlockSpec(memory_space=pltpu.HBM)
SEM = pl.BlockSpec(memory_space=pltpu.SEMAPHORE)
EFFECT = pltpu.SideEffectType.DATAFLOW_SIDE_EFFECTING


def _x4_peers(in_ref, land_ref, send, recv, a2a):
    mx, my, mc = _coords()
    me = 2 * mx + my
    out = []
    for k in range(1, 4):
        px = 1 - mx if k & 2 else mx
        py = 1 - my if k & 1 else my
        pj = 2 * px + py
        mk = functools.partial(pltpu.make_async_remote_copy, src_ref=in_ref.at[pj] if a2a else in_ref,
                               send_sem=send.at[k - 1], recv_sem=recv.at[k - 1], device_id=(px, py, mc), device_id_type=MESH)
        out.append((mk(dst_ref=land_ref.at[me]), mk(dst_ref=land_ref.at[pj])))
    return out


def _x4_start(buf, name, a2a):
    r, wd = buf.shape[-2:]

    def body(in_ref, land_ref, send, recv, in_thru, land_thru, token):
        for start, _ in _x4_peers(in_ref, land_ref, send, recv, a2a):
            start.start()
        token[...] = jnp.zeros_like(token)

    land = lax.empty((4, r, wd), buf.dtype)
    return pl.pallas_call(
        body, name=name,
        out_shape=(pltpu.SemaphoreType.DMA((3,)), pltpu.SemaphoreType.DMA((3,)), pltpu.HBM(buf.shape, buf.dtype),
                   pltpu.HBM(land.shape, land.dtype), _sds((8, LANE))),
        in_specs=(HBM, HBM), out_specs=(SEM, SEM, HBM, HBM, IN_VMEM), input_output_aliases={0: 2, 1: 3},
        compiler_params=pltpu.CompilerParams(has_side_effects=EFFECT),
    )(pltpu.with_memory_space_constraint(buf, pltpu.HBM), pltpu.with_memory_space_constraint(land, pltpu.HBM))


def _x4_wait(send, recv, buf_thru, land_thru, after, name, a2a):
    def body(in_ref, land_ref, send_ref, recv_ref, after_ref, in_dead, got_ref):
        for _, arrive in _x4_peers(in_ref, land_ref, send_ref, recv_ref, a2a):
            arrive.wait_send()
            arrive.wait_recv()

    return pl.pallas_call(
        body, name=name, out_shape=(pltpu.HBM(buf_thru.shape, buf_thru.dtype), pltpu.HBM(land_thru.shape, land_thru.dtype)),
        in_specs=(HBM, HBM, SEM, SEM, ANY), out_specs=(HBM, HBM), input_output_aliases={0: 0, 1: 1},
        compiler_params=pltpu.CompilerParams(has_side_effects=EFFECT),
    )(buf_thru, land_thru, send, recv, after)


def _xchg_sib(x, name):
    def body(in_ref, o_ref, send, recv):
        mx, my, mc = _coords()
        cp = pltpu.make_async_remote_copy(src_ref=in_ref, dst_ref=o_ref, send_sem=send, recv_sem=recv,
                                          device_id=(mx, my, 1 - mc), device_id_type=MESH)
        cp.start()
        cp.wait_recv()
        cp.wait_send()

    return pl.pallas_call(
        body, name=name, out_shape=_sds(x.shape, x.dtype), in_specs=[ANY], out_specs=ANY,
        scratch_shapes=[pltpu.SemaphoreType.DMA(()), pltpu.SemaphoreType.DMA(())])(x)


def _sum_slots(gat, slots, name, tr=None):
    n, r, wd = gat.shape
    tr = r if tr is None else tr

    def body(g_ref, o_ref):
        acc = g_ref[slots[0]].astype(F32)
        for s in slots[1:]:
            acc = acc + g_ref[s].astype(F32)
        o_ref[...] = acc

    return _call(body, name, (r // tr,), [pl.BlockSpec((n, tr, wd), lambda i: (0, i, 0))], _rows(tr, wd), _sds((r, wd)))(gat)


def _add(a, b, name, tr):
    def body(a_ref, b_ref, o_ref):
        o_ref[...] = a_ref[...] + b_ref[...]

    r, wd = a.shape
    return _call(body, name, (r // tr,), [_rows(tr, wd), _rows(tr, wd)], _rows(tr, wd), _sds((r, wd)))(a, b)


def _ada_fwd(x16, ada_w_loc, ada_b_loc):
    nloc = ada_w_loc.shape[-1]

    def body(x_ref, w_ref, b_ref, s_ref, o_ref):
        s = _silu(x_ref[...])
        s_ref[...] = s
        o_ref[...] = jnp.dot(s, w_ref[...], precision=HI, preferred_element_type=F32) + b_ref[...]

    return _call(body, "ada_fwd", (2,),
                 [_full(16, D), pl.BlockSpec((None, D, nloc), lambda l: (l, 0, 0)), pl.BlockSpec((None, 1, nloc), lambda l: (l, 0, 0))],
                 [_full(16, D), pl.BlockSpec((None, 16, nloc), lambda l: (l, 0, 0))],
                 [_sds((16, D)), _sds((2, 16, nloc))])(x16, ada_w_loc, ada_b_loc[:, None, :])


def _ada_bwd(s16, d_loc, ada_w_loc):
    nloc = ada_w_loc.shape[-1]

    def body(s_ref, d_ref, w_ref, gw_ref, cp_ref):
        gw_ref[...] = lax.dot_general(s_ref[...], d_ref[...], (((0,), (0,)), ((), ())), precision=HI,
                                      preferred_element_type=F32)

        @pl.when(pl.program_id(0) == 0)
        def _():
            cp_ref[...] = lax.dot_general(d_ref[8:16, :], w_ref[...], (((1,), (1,)), ((), ())), precision=HI,
                                          preferred_element_type=F32)

    return _call(body, "ada_bwd", (2,),
                 [_full(16, D), pl.BlockSpec((None, 16, nloc), lambda l: (l, 0, 0)), pl.BlockSpec((None, D, nloc), lambda l: (l, 0, 0))],
                 [pl.BlockSpec((None, D, nloc), lambda l: (l, 0, 0)), _full(8, D)],
                 [_sds((2, D, nloc)), _sds((8, D))])(s16, d_loc, ada_w_loc)


def _cctx_grad(dscc, c_ctx):
    def body(d_ref, c_ref, o_ref):
        _, vjp = jax.vjp(_silu, c_ref[...])
        o_ref[...] = vjp(d_ref[...])[0]

    return _call(body, "cctx_grad", (1,), [_full(8, D), _full(8, D)], _full(8, D), _sds((8, D)))(dscc, c_ctx)


def _adamw_math(w, g, m, v):
    mn = ADAM_B1 * m + (1.0 - ADAM_B1) * g
    vn = ADAM_B2 * v + (1.0 - ADAM_B2) * jnp.square(g)
    c1 = 1.0 - ADAM_B1 ** ADAM_STEP
    c2 = 1.0 - ADAM_B2 ** ADAM_STEP
    return -ADAM_LR * ((mn / c1) / (jnp.sqrt(vn / c2) + ADAM_EPS) + ADAM_WD * w), mn, vn


def _adamw(w, g, m, v, name):
    n_l, r, wd = w.shape
    tr = 256 if r % 256 == 0 else r

    def body(w_ref, g_ref, m_ref, v_ref, d_ref, mo_ref, vo_ref):
        d_ref[...], mo_ref[...], vo_ref[...] = _adamw_math(w_ref[...], g_ref[...], m_ref[...], v_ref[...])

    spec = pl.BlockSpec((None, tr, wd), lambda a, i: (a, i, 0))
    return tuple(_call(body, name, (n_l, r // tr), [spec] * 4, [spec] * 3, [_sds(w.shape)] * 3)(w, g, m, v))


def _adamw_small(ws, gs, ms, vs, name):
    n = len(ws)
    shapes = [a.shape for a in ws]
    as2d = lambda a: a.reshape(-1, a.shape[-1])

    def body(*refs):
        ins, outs = refs[:4 * n], refs[4 * n:]
        for k in range(n):
            res = _adamw_math(ins[k][...], ins[n + k][...], ins[2 * n + k][...], ins[3 * n + k][...])
            for j in range(3):
                outs[j * n + k][...] = res[j]

    flat = [as2d(a) for group in (ws, gs, ms, vs) for a in group]
    specs = [_full(*a.shape) for a in flat]
    outs = _call(body, name, (1,), specs, specs[:n] * 3, [_sds(a.shape) for a in flat[:n]] * 3)(*flat)
    return tuple([outs[j * n + k].reshape(shapes[k]) for k in range(n)] for j in range(3))


ROW = 1024


def _nrows(size):
    return -(-size // ROW)


def _pack(arrs, rows_total, dtype=F32):
    parts = []
    for a in arrs:
        flat = a.reshape(-1).astype(dtype)
        pad = _nrows(flat.shape[0]) * ROW - flat.shape[0]
        parts.append(flat if pad == 0 else jnp.concatenate([flat, jnp.zeros((pad,), dtype)]))
    flat = jnp.concatenate(parts)
    out = flat.reshape(-1, ROW)
    return _pad_rows(out, rows_total)


def _unpack(buf, shapes):
    lead = buf.shape[:-2]
    out, r0 = [], 0
    for shp in shapes:
        size = 1
        for s in shp:
            size *= s
        nr = _nrows(size)
        piece = lax.slice_in_dim(buf, r0, r0 + nr, axis=len(lead))
        out.append(piece.reshape(lead + (nr * ROW,))[..., :size].reshape(lead + tuple(shp)))
        r0 += nr
    return out


SLOT = 16


def _slot_rows(size):
    return _round_up(size // ROW, SLOT)


def _pack_rows(arrs, rows_total, dtype):
    parts, used = [], 0
    for a in arrs:
        part = a.astype(dtype).reshape(-1, ROW)
        extra = _slot_rows(a.size) - part.shape[0]
        parts.append(part if extra == 0 else jnp.pad(part, ((0, extra), (0, 0))))
        used += _slot_rows(a.size)
    if rows_total > used:
        parts.append(jnp.zeros((rows_total - used, ROW), dtype))
    return jnp.concatenate(parts, axis=0)


def _unpack_rows(buf, shapes):
    lead = buf.shape[:-2]
    out, r0 = [], 0
    for shp in shapes:
        size = 1
        for s in shp:
            size *= s
        piece = lax.slice_in_dim(buf, r0, r0 + size // ROW, axis=len(lead))
        out.append(piece.reshape(lead + tuple(shp)))
        r0 += _slot_rows(size)
    return out


def _round_up(n, k):
    return -(-n // k) * k


WEIGHTS = ['c_ctx', 'ada_w', 'ada_b', 'norm_mix_g', 'norm_mlp_g', 'mlp_w1', 'mlp_w2', 'hy_w_in', 'ssd_conv_w', 'ssd_conv_b',
           'ssd_dt_bias', 'ssd_a_log', 'ssd_d', 'ssd_norm_g', 'sg_ln_g', 'sg_ln_b', 'sg_w', 'sg_b', 'hy_w_out', 'cf_w_pw1',
           'cf_b_pw1', 'cf_w_dw', 'cf_b_dw', 'cf_ln_g', 'cf_ln_b', 'cf_w_pw2', 'cf_b_pw2', 'final_norm_g']
BIG = {'mlp_w1': 2, 'mlp_w2': 1, 'hy_w_in': 2, 'hy_w_out': 1, 'cf_w_pw1': 2, 'cf_w_pw2': 1}
SMALL_SHARD = ['ssd_conv_w', 'cf_b_pw1', 'cf_w_dw', 'cf_b_dw', 'cf_ln_g', 'cf_ln_b', 'cf_b_pw2']
REP = ['norm_mix_g', 'norm_mlp_g', 'ssd_conv_b', 'ssd_dt_bias', 'ssd_a_log', 'ssd_d', 'ssd_norm_g', 'sg_ln_g', 'sg_ln_b',
       'sg_w', 'sg_b', 'final_norm_g']


def _gather_shards(stacked, axis):
    return jnp.concatenate([stacked[j] for j in range(4)], axis=axis)


def kernel(x, c, ctx, c_ctx, ada_w, ada_b, norm_mix_g, norm_mlp_g, mlp_w1, mlp_w2, hy_w_in, ssd_conv_w, ssd_conv_b, ssd_dt_bias, ssd_a_log, ssd_d, ssd_norm_g, sg_ln_g, sg_ln_b, sg_w, sg_b, hy_w_out, cf_w_pw1, cf_b_pw1, cf_w_dw, cf_b_dw, cf_ln_g, cf_ln_b, cf_w_pw2, cf_b_pw2, final_norm_g, loss_target, m_c_ctx, m_ada_w, m_ada_b, m_norm_mix_g, m_norm_mlp_g, m_mlp_w1, m_mlp_w2, m_hy_w_in, m_ssd_conv_w, m_ssd_conv_b, m_ssd_dt_bias, m_ssd_a_log, m_ssd_d, m_ssd_norm_g, m_sg_ln_g, m_sg_ln_b, m_sg_w, m_sg_b, m_hy_w_out, m_cf_w_pw1, m_cf_b_pw1, m_cf_w_dw, m_cf_b_dw, m_cf_ln_g, m_cf_ln_b, m_cf_w_pw2, m_cf_b_pw2, m_final_norm_g, v_c_ctx, v_ada_w, v_ada_b, v_norm_mix_g, v_norm_mlp_g, v_mlp_w1, v_mlp_w2, v_hy_w_in, v_ssd_conv_w, v_ssd_conv_b, v_ssd_dt_bias, v_ssd_a_log, v_ssd_d, v_ssd_norm_g, v_sg_ln_g, v_sg_ln_b, v_sg_w, v_sg_b, v_hy_w_out, v_cf_w_pw1, v_cf_b_pw1, v_cf_w_dw, v_cf_b_dw, v_cf_ln_g, v_cf_ln_b, v_cf_w_pw2, v_cf_b_pw2, v_final_norm_g):
    args = locals()
    wl = {n: args[n] for n in WEIGHTS}
    ml = {n: args["m_" + n] for n in WEIGHTS}
    vl = {n: args["v_" + n] for n in WEIGHTS}
    mx, my, mc = _coords()
    me = 4 * mx + 2 * my + mc
    shard = 2 * mx + my
    even = (0, 2, 4, 6)

    def start_gather(names, name, tie=None):
        rows = sum(_slot_rows(wl[n].size) for n in names)
        buf = _pack_rows([wl[n] for n in names], rows, MXU)
        if tie is not None:
            buf, _ = lax.optimization_barrier((buf, tie))
        return _x4_start(buf, name, a2a=False)

    def finish_gather(handle, names, after, name):
        send, recv, own, land, _ = handle
        own, land = _x4_wait(send, recv, own, land, after, name, a2a=False)
        got = lax.dynamic_update_slice(land, own[None], (shard, 0, 0))
        shapes = [wl[n].shape for n in names]
        wfull = {n: _gather_shards(st, BIG[n]) for n, st in zip(names, _unpack_rows(got, shapes))}
        return wfull, got

    rest_names = [n for n in BIG if n != "hy_w_in"]
    h_in = _x4_start(hy_w_in[0].astype(MXU), "agw_in_start", a2a=False)
    c = c + h_in[4][0, 0]

    small_shapes = [wl[n].shape for n in SMALL_SHARD]
    blk1 = _pack([c] + [wl[n] for n in SMALL_SHARD], 24)
    got1 = _ag8(blk1, "ag_cond")
    x16 = _pad_rows(jnp.concatenate([got1[:, 0, :], c_ctx[None]], axis=0), 16)
    small_full = {}
    for n, parts in zip(SMALL_SHARD, _unpack(got1[:, 1:, :], small_shapes)):
        small_full[n] = jnp.concatenate([parts[s] for s in even], axis=-1)

    nloc = ada_w.shape[-1]
    ada_b_loc = lax.dynamic_slice_in_dim(ada_b, shard * nloc, nloc, axis=1)
    s16, ada_loc = _ada_fwd(x16, ada_w, ada_b_loc)
    got2 = _ag8(ada_loc.reshape(32, nloc), "ag_ada").reshape(8, 2, 16, nloc)
    ada_full = jnp.concatenate([got2[s] for s in even], axis=-1)
    ada_me = lax.dynamic_slice_in_dim(ada_full, me, 1, axis=1)[:, 0, :]
    cada0 = ada_full[0, 8, :]

    w = {n: wl[n] for n in WEIGHTS if n not in BIG and n not in SMALL_SHARD}
    w.update(small_full)
    h_rest = start_gather(rest_names, "agw_rest_start", tie=ada_me)
    send, recv, own, land, _ = h_in
    own, land = _x4_wait(send, recv, own, land, h_rest[4], "agw_in_wait", a2a=False)
    w["wz"], w["wxbc"], w["wdt"], w["wuv"] = _win_split(lax.dynamic_update_slice(land, own[None], (shard, 0, 0)))
    w["sg_w"] = sg_w[0].astype(MXU)

    def late_w(after):
        wfull, got = finish_gather(h_rest, rest_names, after, "agw_rest_wait")
        return {"hy_w_out": wfull["hy_w_out"][0], "wpack": got,
                "cf_w_pw1": wfull["cf_w_pw1"][0], "cf_w_pw2": wfull["cf_w_pw2"][0]}

    full_shape = {n: wl[n].shape for n in WEIGHTS}
    for n in BIG:
        full_shape[n] = tuple(s * 4 if a == BIG[n] else s for a, s in enumerate(wl[n].shape))
    for n in SMALL_SHARD:
        full_shape[n] = wl[n].shape[:-1] + (wl[n].shape[-1] * 4,)

    early_names = ["mlp_w1", "mlp_w2", "cf_w_pw1", "cf_w_pw2", "hy_w_out"]
    early = {}

    def early_grads(gp):
        used = sum(gp[n].shape[1] for n in early_names)
        parts = [gp[n] for n in early_names] + [jnp.zeros((4, _round_up(used, 512) - used, ROW), jnp.bfloat16)]
        early["h"] = _x4_start(jnp.concatenate(parts, axis=1), "a2a_early_start", a2a=True)
        return early["h"][4][0, 0]

    loss_part, dx, g, d_ada, d_cada0 = _local_step(x[0], ctx[0], loss_target[0], ada_me, cada0, w, late_w, early_grads)
    loss = lax.psum(loss_part, ("x", "y", "c"))
    gp_last = g["pieces"]["hy_w_in"]
    g = {n: a.reshape(full_shape[n]) for n, a in g.items() if n != "pieces"}

    sm_names = REP + SMALL_SHARD
    srows = _round_up(sum(_nrows(g[n].size) for n in sm_names) + 18, 16)
    got3 = _ag8(_pack([g[n] for n in sm_names] + [d_ada, d_cada0], srows, jnp.bfloat16), "ag_small")
    tot3 = _sum_slots(got3, tuple(range(8)), "sum_small")
    sm_tot = _unpack(tot3, [full_shape[n] for n in sm_names] + [(2, 6 * D), (6 * D,)])
    grads = dict(zip(sm_names, sm_tot[:-2]))
    for n in SMALL_SHARD:
        k = wl[n].shape[-1]
        grads[n] = lax.dynamic_slice_in_dim(grads[n], shard * k, k, axis=grads[n].ndim - 1)
    dada_tot, dcada_tot = sm_tot[-2], sm_tot[-1]
    grads["ada_b"] = dada_tot.at[0].add(dcada_tot)
    r_ada = sum(_nrows(g[n].size) for n in sm_names)
    dada_all = got3[:, r_ada:r_ada + 12, :].astype(F32).reshape(8, 2, 6 * D)
    d16 = jnp.concatenate([jnp.transpose(dada_all, (1, 0, 2)),
                           jnp.stack([dcada_tot, jnp.zeros_like(dcada_tot)])[:, None, :],
                           jnp.zeros((2, 7, 6 * D), F32)], axis=1)
    d_loc = lax.dynamic_slice_in_dim(d16, shard * nloc, nloc, axis=2)
    grads["ada_w"], cpart = _ada_bwd(s16, d_loc, ada_w)
    got4 = _ag8(cpart, "ag_cctx")
    dscc = _sum_slots(got4, even, "sum_cctx")
    grads["c_ctx"] = _cctx_grad(dscc, _pad_rows(c_ctx[None], 8))[0]

    delta, new_m, new_v = {}, {}, {}

    def finish(handle, after, names, tag):
        send, recv, own, land, _ = handle
        own, land = _x4_wait(send, recv, own, land, after, "a2a_" + tag + "_wait", a2a=True)
        got = lax.dynamic_update_slice(land, lax.dynamic_slice_in_dim(own, shard, 1, axis=0), (shard, 0, 0))
        tr = 512 if got.shape[1] % 512 == 0 else got.shape[1]
        part = _sum_slots(got, (0, 1, 2, 3), "sum_grads_" + tag, tr=tr)
        tot = _add(part, _xchg_sib(part, "swap_grads_" + tag), "add_grads_" + tag, tr)
        if len(names) == 1:
            grads[names[0]] = tot.reshape(wl[names[0]].shape)
        else:
            grads.update(zip(names, _unpack_rows(tot, [wl[n].shape for n in names])))
        for n in names:
            delta[n], new_m[n], new_v[n] = _adamw(wl[n], grads[n], ml[n], vl[n], "adamw_" + n)

    last_pieces, _ = lax.optimization_barrier((gp_last, grads["c_ctx"]))
    h_last = _x4_start(last_pieces, "a2a_last_start", a2a=True)
    finish(early["h"], h_last[4], early_names, "early")
    delta["ada_w"], new_m["ada_w"], new_v["ada_w"] = _adamw(ada_w, grads["ada_w"], ml["ada_w"], vl["ada_w"], "adamw_ada_w")
    small = ["c_ctx", "ada_b"] + REP + SMALL_SHARD
    outs = _adamw_small(*([src[n].reshape(wl[n].shape) for n in small] for src in (wl, grads, ml, vl)), "adamw_small")
    for dst, vals in zip((delta, new_m, new_v), outs):
        dst.update(zip(small, vals))
    finish(h_last, delta["mlp_w1"], ["hy_w_in"], "last")

    return (loss, dx[None], *[grads[n].reshape(wl[n].shape) for n in WEIGHTS], *[delta[n] for n in WEIGHTS],
            *[new_m[n] for n in WEIGHTS], *[new_v[n] for n in WEIGHTS])
```

```python
import functools

import jax
import jax.numpy as jnp
from jax import lax
from jax.experimental import pallas as pl
from jax.experimental.pallas import tpu as pltpu

F32 = jnp.float32
MXU = jnp.bfloat16
ACT = jnp.bfloat16
HI = lax.Precision.HIGHEST
EPS = 1e-6

D = 1024
HID = 4096
XBC = 1536
Q = 128
SGG = 8
CK = 31
SK = 5
T = 256
LANE = 128
VMEM_LIMIT = 56 * 1024 * 1024

ADAM_LR, ADAM_B1, ADAM_B2, ADAM_EPS, ADAM_WD, ADAM_STEP = 0.001, 0.9, 0.999, 1e-08, 0.01, 10


def _call(body, name, grid, in_specs, out_specs, out_shape, scratch=()):
    return pl.pallas_call(
        body, name=name, grid=grid, in_specs=in_specs, out_specs=out_specs, out_shape=out_shape,
        scratch_shapes=list(scratch),
        compiler_params=pltpu.CompilerParams(dimension_semantics=("arbitrary",) * len(grid),
                                             vmem_limit_bytes=VMEM_LIMIT))


def _sds(shape, dt=F32):
    return jax.ShapeDtypeStruct(tuple(shape), dt)


def _rows(t, w, off=0, lane_blk=0):
    return pl.BlockSpec((t, w), lambda i: (i + off, lane_blk))


def _rows_lat(t, w, nct):
    return pl.BlockSpec((t, w), lambda i: (jnp.maximum(i - nct, 0), 0))


def _full(*shape):
    return pl.BlockSpec(shape, lambda *_: (0,) * len(shape))


def _wfull(*shape):
    return pl.BlockSpec(shape, lambda *_: (0,) * len(shape), pipeline_mode=pl.Buffered(1))


def _halo(w, hb, nrows):
    r, nb = T // hb, nrows // hb
    prev = pl.BlockSpec((hb, w), lambda i: (jnp.maximum(i * r - 1, 0), 0))
    nxt = pl.BlockSpec((hb, w), lambda i: (jnp.minimum((i + 1) * r, nb - 1), 0))
    return prev, nxt


def _nn(a, b):
    return jnp.dot(a.astype(MXU), b.astype(MXU), preferred_element_type=F32)


def _nt(a, b):
    return lax.dot_general(a.astype(MXU), b.astype(MXU), (((1,), (1,)), ((), ())), preferred_element_type=F32)


def _tn_dot(a, b):
    return lax.dot_general(a.astype(MXU), b.astype(MXU), (((0,), (0,)), ((), ())), preferred_element_type=F32)


def _sum0(x):
    return jnp.sum(x, axis=0, keepdims=True)


def _silu(x):
    return x * jax.nn.sigmoid(x)


def _gelu(x):
    return jax.nn.gelu(x, approximate=True)


def _mod(h, g, sh, sc):
    n = h * lax.rsqrt(jnp.mean(h * h, axis=-1, keepdims=True) + EPS)
    return n * g * (1.0 + sc) + sh


def _ln(x, g, b):
    xc = x - jnp.mean(x, axis=-1, keepdims=True)
    return xc * lax.rsqrt(jnp.mean(xc * xc, axis=-1, keepdims=True) + EPS) * g + b


def _gate_norm(ytot, z, ng):
    yg = ytot * _silu(z)
    halves = []
    for k in range(2):
        seg = yg[:, k * 512:(k + 1) * 512]
        halves.append(seg * lax.rsqrt(jnp.mean(seg * seg, axis=-1, keepdims=True) + EPS) * ng[:, k * 512:(k + 1) * 512])
    return jnp.concatenate(halves, axis=-1)


def _fill_ext(ext_ref, prev_ref, cur_ref, next_ref, hb, first, last):
    ext_ref[0:hb, :] = jnp.where(first, 0.0, prev_ref[...])
    ext_ref[hb:hb + T, :] = cur_ref[...]
    ext_ref[hb + T:hb + T + hb, :] = jnp.where(last, 0.0, next_ref[...])


def _conv(ext_ref, w_ref, k_taps, hb, lanes):
    off = hb - k_taps // 2
    acc = ext_ref[pl.ds(off, T), lanes] * w_ref[0:1, lanes]
    for k in range(1, k_taps):
        acc = acc + ext_ref[pl.ds(off + k, T), lanes] * w_ref[k:k + 1, lanes]
    return acc


def _conv_tr(ext_ref, w_ref, k_taps, hb, lanes):
    off = hb + k_taps // 2
    acc = ext_ref[pl.ds(off, T), lanes] * w_ref[0:1, lanes]
    for k in range(1, k_taps):
        acc = acc + ext_ref[pl.ds(off - k, T), lanes] * w_ref[k:k + 1, lanes]
    return acc


def _conv_wide(ext_ref, w_ref, k_taps, hb, lanes, flip=False):
    base = hb - k_taps // 2
    acc = None
    for b in range(8):
        taps = [k for k in range(k_taps) if (base + k) % 8 == b]
        if not taps:
            continue
        p = None
        for k in taps:
            wi = (k_taps - 1 - k) if flip else k
            term = ext_ref[pl.ds(base + k - b, T + 8), lanes] * w_ref[wi:wi + 1, lanes]
            p = term if p is None else p + term
        acc = p[b:b + T] if acc is None else acc + p[b:b + T]
    return acc


def _conv_dw_wide(dw_ref, d_ref, xext_ref, k_taps, hb, lanes):
    base = hb - k_taps // 2
    d = d_ref[:, lanes]
    for b in range(8):
        taps = [k for k in range(k_taps) if (base + k) % 8 == b]
        if not taps:
            continue
        lo_off = base + taps[0] - b
        span = base + taps[-1] - b - lo_off
        xs = xext_ref[pl.ds(lo_off + b, T + span), lanes]
        for k in taps:
            a = base + k - b - lo_off
            dw_ref[k:k + 1, lanes] += _sum0(d * xs[a:a + T])


def _conv_dw(dw_ref, d_ref, xext_ref, k_taps, hb, lanes):
    off = hb - k_taps // 2
    d = d_ref[:, lanes]
    for k in range(k_taps):
        dw_ref[k:k + 1, lanes] += _sum0(d * xext_ref[pl.ds(off + k, T), lanes])


def _tn(a, b, name, relu2=False, shard=None):
    m_rows, ka = a.shape
    n = b.shape[1]
    tm = next(t for t in (1024, 768, 512, 256) if m_rows % t == 0)
    tk = min(ka, 1024)
    tn = n if n <= 1024 else next(t for t in (1024, 768, 512, 384, 256, 128) if n % t == 0)
    if shard is not None and shard[0] == "col":
        tn = shard[1]
    if shard is not None and shard[0] == "row":
        tk = shard[1]
    n_m = m_rows // tm

    def body(a_ref, b_ref, o_ref, *acc):
        acc_ref = acc[0] if acc else o_ref

        @pl.when(pl.program_id(2) == 0)
        def _():
            acc_ref[...] = jnp.zeros_like(acc_ref)
        av = a_ref[...]
        if relu2:
            av = jnp.square(jnp.maximum(av.astype(F32), 0.0))
        acc_ref[...] += _tn_dot(av, b_ref[...])
        if acc:
            @pl.when(pl.program_id(2) == n_m - 1)
            def _():
                o_ref[...] = acc_ref[...].astype(o_ref.dtype)

    in_specs = [pl.BlockSpec((tm, tk), lambda k, j, m: (m, k)), pl.BlockSpec((tm, tn), lambda k, j, m: (m, j))]
    if shard is None:
        return _call(body, name, (ka // tk, n // tn, n_m), in_specs,
                     pl.BlockSpec((tk, tn), lambda k, j, m: (k, j)), _sds((ka, n)))(a, b)
    if shard[0] == "col":
        out_spec, out_shape = pl.BlockSpec((None, tk, tn), lambda k, j, m: (j, k, 0)), _sds((n // tn, ka, tn), jnp.bfloat16)
    else:
        out_spec, out_shape = pl.BlockSpec((None, tk, tn), lambda k, j, m: (k, 0, j)), _sds((ka // tk, tk, n), jnp.bfloat16)
    return _call(body, name, (ka // tk, n // tn, n_m), in_specs, out_spec, out_shape,
                 scratch=[pltpu.VMEM((tk, tn), F32)])(a, b)


def _mlp_fwd(h, mp, wpack, layer, name):
    n_rows = h.shape[0]

    def body(h_ref, mp_ref, w1_ref, w2_ref, hn_ref, a_ref, y_ref):
        hv = h_ref[...]
        u = _mod(hv, mp_ref[0:1], mp_ref[1:2], mp_ref[2:3]).astype(MXU)
        acc = jnp.zeros((T, D), F32)
        for j in range(HID // 1024):
            cs = slice(j * 1024, (j + 1) * 1024)
            a = jnp.dot(u, w1_ref[j], preferred_element_type=F32)
            a_ref[:, cs] = a.astype(ACT)
            acc = acc + jnp.dot(jnp.square(jnp.maximum(a, 0.0)).astype(MXU), w2_ref[j], preferred_element_type=F32)
        y_ref[...] = acc
        hn_ref[...] = hv + mp_ref[3:4] * acc

    return _call(body, name, (n_rows // T,),
                 [_rows(T, D), _full(8, D), _mlp_wspec(layer), _mlp_wspec(2 + layer)],
                 [_rows(T, D), _rows(T, HID), _rows(T, D)],
                 [_sds((n_rows, D)), _sds((n_rows, HID), ACT), _sds((n_rows, D))])(h, mp, wpack, wpack)


def _mlp_wspec(row_block):
    return pl.BlockSpec((4, 1024, 1024), lambda i: (0, row_block, 0), pipeline_mode=pl.Buffered(1))


def _mlp_bwd(dh, h, a, y, mp, wpack, layer, name):
    n_rows = h.shape[0]

    def body(dh_ref, h_ref, a_ref, y_ref, mp_ref, w1_ref, w2_ref, dho_ref, da_ref, dyb_ref, ub_ref, pg_ref):
        dhp = dh_ref[...]
        u, vjp = jax.vjp(_mod, h_ref[...], mp_ref[0:1], mp_ref[1:2], mp_ref[2:3])
        ub_ref[...] = u.astype(ACT)
        dyb = (mp_ref[3:4] * dhp).astype(MXU)
        dyb_ref[...] = dyb.astype(ACT)
        du = jnp.zeros((T, D), F32)
        for j in range(HID // 1024):
            cs = slice(j * 1024, (j + 1) * 1024)
            dp = _nt(dyb, w2_ref[j])
            da = dp * 2.0 * jnp.maximum(a_ref[:, cs].astype(F32), 0.0)
            da_ref[:, cs] = da.astype(ACT)
            du = du + _nt(da, w1_ref[j])
        dhn, dg, dsh, dsc = vjp(du)
        dho_ref[...] = dhp + dhn

        @pl.when(pl.program_id(0) == 0)
        def _():
            pg_ref[...] = jnp.zeros_like(pg_ref)
        pg_ref[0:1] += dg
        pg_ref[1:2] += dsh
        pg_ref[2:3] += dsc
        pg_ref[3:4] += _sum0(dhp * y_ref[...])

    return _call(body, name, (n_rows // T,),
                 [_rows(T, D), _rows(T, D), _rows(T, HID), _rows(T, D), _full(8, D), _mlp_wspec(layer), _mlp_wspec(2 + layer)],
                 [_rows(T, D), _rows(T, HID), _rows(T, D), _rows(T, D), _full(8, D)],
                 [_sds((n_rows, D)), _sds((n_rows, HID), ACT), _sds((n_rows, D), ACT), _sds((n_rows, D), ACT),
                  _sds((8, D))])(dh, h, a, y, mp, wpack, wpack)


def _cf1_fwd(h, mp, w1, b1):
    n_rows = h.shape[0]

    def body(h_ref, mp_ref, w1_ref, b1_ref, glu_ref, a_ref):
        u = _mod(h_ref[...], mp_ref[0:1], mp_ref[1:2], mp_ref[2:3]).astype(MXU)
        a = jnp.dot(u, w1_ref[...], preferred_element_type=F32) + b1_ref[...]
        a_ref[...] = a.astype(ACT)
        glu_ref[...] = a[:, :D] * jax.nn.sigmoid(a[:, D:])

    return _call(body, "cf1_fwd", (n_rows // T,),
                 [_rows(T, D), _full(8, D), _wfull(D, 2 * D), _full(1, 2 * D)],
                 [_rows(T, D), _rows(T, 2 * D)],
                 [_sds((n_rows, D)), _sds((n_rows, 2 * D), ACT)])(h, mp, w1, b1)


def _cf2_fwd(h, glu, mp, wdw, bdw, lng, lnb, w2, b2):
    n_rows = h.shape[0]
    nt = n_rows // T
    hb = 16

    def body(h_ref, gp_ref, gc_ref, gn_ref, mp_ref, wdw_ref, bdw_ref, lng_ref, lnb_ref, w2_ref, b2_ref,
             hn_ref, cv_ref, sb_ref, y_ref, ext):
        i = pl.program_id(0)
        _fill_ext(ext, gp_ref, gc_ref, gn_ref, hb, i == 0, i == nt - 1)
        for c in range(D // LANE):
            lanes = slice(c * LANE, (c + 1) * LANE)
            cv_ref[:, lanes] = _conv_wide(ext, wdw_ref, CK, hb, lanes) + bdw_ref[:, lanes]
        s = _silu(_ln(cv_ref[...], lng_ref[...], lnb_ref[...])).astype(MXU)
        sb_ref[...] = s.astype(ACT)
        y = jnp.dot(s, w2_ref[...], preferred_element_type=F32) + b2_ref[...]
        y_ref[...] = y
        hn_ref[...] = h_ref[...] + mp_ref[3:4] * y

    gp, gn = _halo(D, hb, n_rows)
    return _call(body, "cf2_fwd", (nt,),
                 [_rows(T, D), gp, _rows(T, D), gn, _full(8, D), _full(32, D), _full(1, D), _full(1, D), _full(1, D),
                  _wfull(D, D), _full(1, D)],
                 [_rows(T, D), _rows(T, D), _rows(T, D), _rows(T, D)],
                 [_sds((n_rows, D)), _sds((n_rows, D)), _sds((n_rows, D), ACT), _sds((n_rows, D))],
                 scratch=[pltpu.VMEM((T + 2 * hb, D), F32)])(h, glu, glu, glu, mp, wdw, bdw, lng, lnb, w2, b2)


def _cf2_bwd(dh, y, cv, mp, lng, lnb, w2):
    n_rows = dh.shape[0]

    def body(dh_ref, y_ref, cv_ref, mp_ref, lng_ref, lnb_ref, w2_ref, dcv_ref, dyb_ref, pg_ref):
        dhp = dh_ref[...]
        dy = mp_ref[3:4] * dhp
        dyb = dy.astype(MXU)
        dyb_ref[...] = dyb.astype(ACT)
        ds = _nt(dyb, w2_ref[...])
        _, vjp = jax.vjp(lambda cv_, g_, b_: _silu(_ln(cv_, g_, b_)), cv_ref[...], lng_ref[...], lnb_ref[...])
        dcv, dlng, dlnb = vjp(ds)
        dcv_ref[...] = dcv

        @pl.when(pl.program_id(0) == 0)
        def _():
            pg_ref[...] = jnp.zeros_like(pg_ref)
        pg_ref[0:1] += _sum0(dhp * y_ref[...])
        pg_ref[1:2] += _sum0(dy)
        pg_ref[2:3] += dlng
        pg_ref[3:4] += dlnb
        pg_ref[4:5] += _sum0(dcv)

    return _call(body, "cf2_bwd", (n_rows // T,),
                 [_rows(T, D), _rows(T, D), _rows(T, D), _full(8, D), _full(1, D), _full(1, D), _wfull(D, D)],
                 [_rows(T, D), _rows(T, D), _full(8, D)],
                 [_sds((n_rows, D)), _sds((n_rows, D), ACT), _sds((8, D))])(dh, y, cv, mp, lng, lnb, w2)


def _cf1_bwd(dh, h, a, dcv, glu, mp, wdw, w1):
    n_rows = h.shape[0]
    nt = n_rows // T
    hb = 16

    def body(dh_ref, h_ref, a_ref, dp_ref, dc_ref, dn_ref, gp_ref, gc_ref, gn_ref, mp_ref, wdw_ref, w1_ref,
             dho_ref, da_ref, ub_ref, pg_ref, pb_ref, dw_ref, dext, gext, dglu):
        i = pl.program_id(0)

        @pl.when(i == 0)
        def _():
            pg_ref[...] = jnp.zeros_like(pg_ref)
            pb_ref[...] = jnp.zeros_like(pb_ref)
            dw_ref[...] = jnp.zeros_like(dw_ref)
        _fill_ext(dext, dp_ref, dc_ref, dn_ref, hb, i == 0, i == nt - 1)
        _fill_ext(gext, gp_ref, gc_ref, gn_ref, hb, i == 0, i == nt - 1)
        for c in range(D // LANE):
            lanes = slice(c * LANE, (c + 1) * LANE)
            dglu[:, lanes] = _conv_wide(dext, wdw_ref, CK, hb, lanes, flip=True)
            _conv_dw_wide(dw_ref, dc_ref, gext, CK, hb, lanes)
        av = a_ref[...].astype(F32)
        _, vjp_glu = jax.vjp(lambda a1, a2: a1 * jax.nn.sigmoid(a2), av[:, :D], av[:, D:])
        da1, da2 = vjp_glu(dglu[...])
        da_ref[:, :D] = da1.astype(ACT)
        da_ref[:, D:] = da2.astype(ACT)
        pb_ref[0:1, :D] += _sum0(da1)
        pb_ref[0:1, D:] += _sum0(da2)
        du = _nt(da1, w1_ref[:, :D]) + _nt(da2, w1_ref[:, D:])
        u, vjp = jax.vjp(_mod, h_ref[...], mp_ref[0:1], mp_ref[1:2], mp_ref[2:3])
        ub_ref[...] = u.astype(ACT)
        dhn, dg, dsh, dsc = vjp(du)
        dho_ref[...] = dh_ref[...] + dhn
        pg_ref[0:1] += dg
        pg_ref[1:2] += dsh
        pg_ref[2:3] += dsc

    hp, hn = _halo(D, hb, n_rows)
    return _call(body, "cf1_bwd", (nt,),
                 [_rows(T, D), _rows(T, D), _rows(T, 2 * D), hp, _rows(T, D), hn, hp, _rows(T, D), hn,
                  _full(8, D), _full(32, D), _wfull(D, 2 * D)],
                 [_rows(T, D), _rows(T, 2 * D), _rows(T, D), _full(8, D), _full(8, 2 * D), _full(32, D)],
                 [_sds((n_rows, D)), _sds((n_rows, 2 * D), ACT), _sds((n_rows, D), ACT), _sds((8, D)),
                  _sds((8, 2 * D)), _sds((32, D))],
                 scratch=[pltpu.VMEM((T + 2 * hb, D), F32), pltpu.VMEM((T + 2 * hb, D), F32), pltpu.VMEM((T, D), F32)],
                 )(dh, h, a, dcv, dcv, dcv, glu, glu, glu, mp, wdw, w1)


def _sg_blocks():
    return [(c, g, slice(c * Q, (c + 1) * Q), slice(g * LANE, (g + 1) * LANE)) for c in range(T // Q) for g in range(SGG)]


IN_W = D + XBC + 32 + 2 * D
IN_LOC = IN_W // 4


def _win_split(shards):
    o1, o2, o3 = D, D + XBC, D + XBC + 32
    tr = 256

    def cols(s_ref, lo, hi):
        parts = []
        for j in range(4):
            a, b = max(lo, j * IN_LOC), min(hi, (j + 1) * IN_LOC)
            if a < b:
                parts.append(s_ref[j][:, a - j * IN_LOC:b - j * IN_LOC])
        return parts[0] if len(parts) == 1 else jnp.concatenate(parts, axis=1)

    def body(s_ref, wz_ref, wxbc_ref, wdt_ref, wuv_ref):
        wz_ref[...] = cols(s_ref, 0, o1)
        wxbc_ref[...] = cols(s_ref, o1, o2)
        dt = cols(s_ref, o2, o3)
        wdt_ref[...] = jnp.concatenate([dt, jnp.zeros((tr, LANE - 32), dt.dtype)], axis=1)
        wuv_ref[...] = cols(s_ref, o3, IN_W)

    dt_ = shards.dtype
    return _call(body, "win_split", (D // tr,), [pl.BlockSpec((4, tr, IN_LOC), lambda i: (0, i, 0))],
                 [_rows(tr, D), _rows(tr, XBC), _rows(tr, LANE), _rows(tr, 2 * D)],
                 [_sds((D, D), dt_), _sds((D, XBC), dt_), _sds((D, LANE), dt_), _sds((D, 2 * D), dt_)])(shards)


def _win_join(gz, gxbc, gdt, guv):
    tr = 256
    bounds = (0, D, D + XBC, D + XBC + 32, IN_W)

    def body(gz_ref, gx_ref, gd_ref, gu_ref, o_ref):
        segs = (gz_ref, gx_ref, gd_ref, gu_ref)
        for j in range(4):
            parts = []
            for k in range(4):
                a, b = max(bounds[k], j * IN_LOC), min(bounds[k + 1], (j + 1) * IN_LOC)
                if a < b:
                    parts.append(segs[k][:, a - bounds[k]:b - bounds[k]])
            full = parts[0] if len(parts) == 1 else jnp.concatenate(parts, axis=1)
            o_ref[j] = full.astype(jnp.bfloat16)

    return _call(body, "win_join", (D // tr,), [_rows(tr, D), _rows(tr, XBC), _rows(tr, LANE), _rows(tr, 2 * D)],
                 pl.BlockSpec((4, tr, IN_LOC), lambda i: (0, i, 0)), _sds((4, D, IN_LOC), jnp.bfloat16))(gz, gxbc, gdt, guv)


def _ctx_spec(nct):
    return pl.BlockSpec((T, D), lambda i: (jnp.minimum(i, nct - 1), 0))


def _hy1_fwd(ctx, x, mp2, wz, wuv, wxbc, wdt, lng, lnb, sgw, sgbt, nct):
    n_lat = x.shape[0]
    n_rows = ctx.shape[0] + n_lat

    def body(c_ref, x_ref, mp_ref, wz_ref, wuv_ref, wxbc_ref, wdt_ref, lng_ref, lnb_ref, sgw_ref, sgbt_ref,
             z_ref, uv_ref, xbcp_ref, dtr_ref, ysg_ref):
        hv = jnp.where(pl.program_id(0) < nct, c_ref[...], x_ref[...])
        u = _mod(hv, mp_ref[0:1], mp_ref[1:2], mp_ref[2:3]).astype(MXU)
        z_ref[...] = jnp.dot(u, wz_ref[...], preferred_element_type=F32)
        xbcp_ref[...] = jnp.dot(u, wxbc_ref[...], preferred_element_type=F32)
        dtr_ref[...] = jnp.dot(u, wdt_ref[...], preferred_element_type=F32)
        uv = jnp.dot(u, wuv_ref[...], preferred_element_type=F32)
        uv_ref[...] = uv
        gate = _gelu(uv[:, :D])
        vln = _ln(_gelu(uv[:, D:]), lng_ref[...], lnb_ref[...]).astype(MXU)
        for _, g, rs, ls in _sg_blocks():
            s = jnp.dot(sgw_ref[g], vln[rs, ls], preferred_element_type=F32) + sgbt_ref[:, g:g + 1]
            ysg_ref[rs, ls] = (gate[rs, ls] * s).astype(ACT)

    mspec = pl.BlockSpec((None, 8, D), lambda i: (jnp.where(i < nct, 0, 1), 0, 0))
    return _call(body, "hy1_fwd", (n_rows // T,),
                 [_ctx_spec(nct), _rows_lat(T, D, nct), mspec, _wfull(D, D), _wfull(D, 2 * D), _wfull(D, XBC), _wfull(D, LANE),
                  _full(1, D), _full(1, D), _full(SGG, Q, Q), _full(Q, LANE)],
                 [_rows(T, D), _rows(T, 2 * D), _rows(T, XBC), _rows(T, LANE), _rows_lat(T, D, nct)],
                 [_sds((n_rows, D)), _sds((n_rows, 2 * D)), _sds((n_rows, XBC)), _sds((n_rows, LANE)),
                  _sds((n_lat, D), ACT)])(ctx, x, mp2, wz, wuv, wxbc, wdt, lng, lnb, sgw, sgbt)


def _hy1_bwd(ctx, x, uv, dz, dxbcp, ddf, ddb, dysg, dres, mp2, wz, wuv, wxbc, wdt, lng, lnb, sgw, sgbt, nct):
    n_lat = dres.shape[0]
    n_rows = ctx.shape[0] + n_lat

    def body(c_ref, x_ref, uv_ref, dz_ref, dxbcp_ref, ddf_ref, ddb_ref, dysg_ref, dres_ref, mp_ref, wz_ref, wuv_ref,
             wxbc_ref, wdt_ref, lng_ref, lnb_ref, sgw_ref, sgbt_ref,
             dho_ref, ub_ref, duv_ref, ddt_ref, pg2_ref, pl_ref, dsgw_ref, dsgb_ref, dgate_s, dvln_s):
        i = pl.program_id(0)

        @pl.when(i == 0)
        def _():
            pg2_ref[...] = jnp.zeros_like(pg2_ref)
            pl_ref[...] = jnp.zeros_like(pl_ref)
            dsgw_ref[...] = jnp.zeros_like(dsgw_ref)
            dsgb_ref[...] = jnp.zeros_like(dsgb_ref)
        uv = uv_ref[...]

        def f_sg(ug, uvv, g_, b_):
            return _gelu(ug), _ln(_gelu(uvv), g_, b_)
        (gate, vln), vjp_sg = jax.vjp(f_sg, uv[:, :D], uv[:, D:], lng_ref[...], lnb_ref[...])
        vlnb = vln.astype(MXU)
        lane = lax.broadcasted_iota(jnp.int32, (Q, LANE), 1)
        dsgb = jnp.zeros((Q, LANE), F32)
        for _, g, rs, ls in _sg_blocks():
            s = jnp.dot(sgw_ref[g], vlnb[rs, ls], preferred_element_type=F32) + sgbt_ref[:, g:g + 1]
            dyb = dysg_ref[rs, ls]
            dgate_s[rs, ls] = dyb * s
            ds = dyb * gate[rs, ls]
            dvln_s[rs, ls] = _tn_dot(sgw_ref[g], ds)
            dsgw_ref[g] += _nt(ds, vlnb[rs, ls])
            dsgb = dsgb + jnp.where(lane == g, jnp.sum(ds, axis=1, keepdims=True), 0.0)
        dsgb_ref[...] += dsgb
        dug, duvv, dlng, dlnb = vjp_sg((dgate_s[...], dvln_s[...]))
        pl_ref[0:1] += dlng
        pl_ref[1:2] += dlnb
        duv_ref[:, :D] = dug.astype(ACT)
        duv_ref[:, D:] = duvv.astype(ACT)
        ddt = (ddf_ref[...] + ddb_ref[...]).astype(MXU)
        ddt_ref[...] = ddt.astype(ACT)
        du = (_nt(dz_ref[...], wz_ref[...]) + _nt(dug, wuv_ref[:, :D]) + _nt(duvv, wuv_ref[:, D:])
              + _nt(dxbcp_ref[...], wxbc_ref[...]) + _nt(ddt, wdt_ref[...]))
        hv = jnp.where(i < nct, c_ref[...], x_ref[...])
        u, vjp = jax.vjp(_mod, hv, mp_ref[0:1], mp_ref[1:2], mp_ref[2:3])
        ub_ref[...] = u.astype(ACT)
        dhn, dg, dsh, dsc = vjp(du)
        dho_ref[...] = dres_ref[...] + dhn
        is_ctx = i < nct
        for k, val in enumerate((dg, dsh, dsc)):
            pg2_ref[0, k:k + 1] += jnp.where(is_ctx, val, 0.0)
            pg2_ref[1, k:k + 1] += jnp.where(is_ctx, 0.0, val)

    mspec = pl.BlockSpec((None, 8, D), lambda i: (jnp.where(i < nct, 0, 1), 0, 0))
    return _call(body, "hy1_bwd", (n_rows // T,),
                 [_ctx_spec(nct), _rows_lat(T, D, nct), _rows(T, 2 * D), _rows(T, D), _rows(T, XBC), _rows(T, LANE),
                  _rows(T, LANE), _rows(T, D),
                  _rows_lat(T, D, nct), mspec, _wfull(D, D), _wfull(D, 2 * D), _wfull(D, XBC), _wfull(D, LANE),
                  _full(1, D), _full(1, D), _full(SGG, Q, Q), _full(Q, LANE)],
                 [_rows_lat(T, D, nct), _rows(T, D), _rows(T, 2 * D), _rows(T, LANE), _full(2, 8, D), _full(8, D),
                  _full(SGG, Q, Q), _full(Q, LANE)],
                 [_sds((n_lat, D)), _sds((n_rows, D), ACT), _sds((n_rows, 2 * D), ACT), _sds((n_rows, LANE), ACT),
                  _sds((2, 8, D)), _sds((8, D)), _sds((SGG, Q, Q)), _sds((Q, LANE))],
                 scratch=[pltpu.VMEM((T, D), F32), pltpu.VMEM((T, D), F32)],
                 )(ctx, x, uv, dz, dxbcp, ddf, ddb, dysg, dres, mp2, wz, wuv, wxbc, wdt, lng, lnb, sgw, sgbt)


def _seq_edges(i, nct, nt):
    return (i == 0) | (i == nct), (i == nct - 1) | (i == nt - 1)


def _cv5_fwd(xbcp, w, b, nct):
    n_rows = xbcp.shape[0]
    nt = n_rows // T
    hb = 8

    def body(p_ref, c_ref, n_ref, w_ref, b_ref, o_ref, ext):
        first, last = _seq_edges(pl.program_id(0), nct, nt)
        _fill_ext(ext, p_ref, c_ref, n_ref, hb, first, last)
        for c in range(XBC // LANE):
            lanes = slice(c * LANE, (c + 1) * LANE)
            o_ref[:, lanes] = _silu(_conv(ext, w_ref, SK, hb, lanes) + b_ref[:, lanes])

    hp, hn = _halo(XBC, hb, n_rows)
    return _call(body, "cv5_fwd", (nt,), [hp, _rows(T, XBC), hn, _full(8, XBC), _full(1, XBC)],
                 _rows(T, XBC), _sds((n_rows, XBC)), scratch=[pltpu.VMEM((T + 2 * hb, XBC), F32)])(xbcp, xbcp, xbcp, w, b)


def _cv5_bwd1(xbcp, dxf, dxb, w, b, nct):
    n_rows = xbcp.shape[0]
    nt = n_rows // T
    hb = 8

    def body(p_ref, c_ref, n_ref, dxf_ref, dxb_ref, w_ref, b_ref, o_ref, pg_ref, ext):
        i = pl.program_id(0)
        first, last = _seq_edges(i, nct, nt)
        _fill_ext(ext, p_ref, c_ref, n_ref, hb, first, last)

        @pl.when(i == 0)
        def _():
            pg_ref[...] = jnp.zeros_like(pg_ref)
        for c in range(XBC // LANE):
            lanes = slice(c * LANE, (c + 1) * LANE)
            cv = _conv(ext, w_ref, SK, hb, lanes) + b_ref[:, lanes]
            sg = jax.nn.sigmoid(cv)
            dcv = (dxf_ref[:, lanes] + dxb_ref[:, lanes]) * (sg * (1.0 + cv * (1.0 - sg)))
            o_ref[:, lanes] = dcv
            pg_ref[0:1, lanes] += _sum0(dcv)

    hp, hn = _halo(XBC, hb, n_rows)
    return _call(body, "cv5_bwd1", (nt,),
                 [hp, _rows(T, XBC), hn, _rows(T, XBC), _rows(T, XBC), _full(8, XBC), _full(1, XBC)],
                 [_rows(T, XBC), _full(8, XBC)], [_sds((n_rows, XBC)), _sds((8, XBC))],
                 scratch=[pltpu.VMEM((T + 2 * hb, XBC), F32)])(xbcp, xbcp, xbcp, dxf, dxb, w, b)


def _cv5_bwd2(dcv, xbcp, w, nct):
    n_rows = xbcp.shape[0]
    nt = n_rows // T
    hb = 8

    def body(dp_ref, dc_ref, dn_ref, xp_ref, xc_ref, xn_ref, w_ref, o_ref, dw_ref, dext, xext):
        i = pl.program_id(0)
        first, last = _seq_edges(i, nct, nt)
        _fill_ext(dext, dp_ref, dc_ref, dn_ref, hb, first, last)
        _fill_ext(xext, xp_ref, xc_ref, xn_ref, hb, first, last)

        @pl.when(i == 0)
        def _():
            dw_ref[...] = jnp.zeros_like(dw_ref)
        for c in range(XBC // LANE):
            lanes = slice(c * LANE, (c + 1) * LANE)
            o_ref[:, lanes] = _conv_tr(dext, w_ref, SK, hb, lanes).astype(ACT)
            _conv_dw(dw_ref, dc_ref, xext, SK, hb, lanes)

    hp, hn = _halo(XBC, hb, n_rows)
    return _call(body, "cv5_bwd2", (nt,),
                 [hp, _rows(T, XBC), hn, hp, _rows(T, XBC), hn, _full(8, XBC)],
                 [_rows(T, XBC), _full(8, XBC)], [_sds((n_rows, XBC), ACT), _sds((8, XBC))],
                 scratch=[pltpu.VMEM((T + 2 * hb, XBC), F32), pltpu.VMEM((T + 2 * hb, XBC), F32)],
                 )(dcv, dcv, dcv, xbcp, xbcp, xbcp, w)


def _scan_order(nc, ncc, rev):
    if not rev:
        return lambda s: s
    return lambda s: jnp.where(s < ncc, ncc - 1 - s, nc - 1 - (s - ncc))


def _ssd_prep(dtr, sp, rev):
    dt = jax.nn.softplus(dtr + sp[0:1])
    a_neg = -jnp.exp(sp[1:2])
    r = lax.broadcasted_iota(jnp.int32, (Q, Q), 0)
    c = lax.broadcasted_iota(jnp.int32, (Q, Q), 1)
    msk = (c >= r) if rev else (c <= r)
    tri = msk.astype(F32)
    acs = jnp.dot(tri, dt * a_neg, precision=HI, preferred_element_type=F32)
    last = 0 if rev else Q - 1
    return dt, a_neg, acs, msk, tri, last


def _pair_sel(arr, lo, m, lane_lt):
    h0 = lo + 2 * m
    return jnp.where(lane_lt, arr[:, h0:h0 + 1], arr[:, h0 + 1:h0 + 2])


def _head_lanes(row, lo, g):
    lane = lax.broadcasted_iota(jnp.int32, (1, 512), 1)
    out = jnp.zeros((1, 512), F32)
    for k in range(8):
        h = lo + 8 * g + k
        out = jnp.where((lane >= 64 * k) & (lane < 64 * (k + 1)), row[:, h:h + 1], out)
    return out


def _halves(v, lane_lt):
    return jnp.concatenate([jnp.where(lane_lt, v, 0.0), jnp.where(lane_lt, 0.0, v)], axis=0)


def _ssd_fwd(xbc, dtr, sp, ncc, rev):
    n_rows = xbc.shape[0]
    nc = n_rows // Q
    lo = 16 if rev else 0
    order = _scan_order(nc, ncc, rev)

    def body(x_ref, dtr_ref, sp_ref, y_ref, hin_ref, st):
        @pl.when(pl.program_id(0) == 0)
        def _():
            st[...] = jnp.zeros_like(st)
        dt, _, acs, msk, _, last = _ssd_prep(dtr_ref[...], sp_ref[...], rev)
        acs_t, dt_t = acs.T, dt.T
        eacs = jnp.exp(acs)
        eal = jnp.exp(acs[last:last + 1, :])
        tew = jnp.exp(acs[last:last + 1, :] - acs) * dt
        lane_lt = lax.broadcasted_iota(jnp.int32, (Q, LANE), 1) < 64
        for g in range(2):
            gl = slice(g * 512, (g + 1) * 512)
            bg = x_ref[:, 1024 + g * 128:1152 + g * 128]
            cg = x_ref[:, 1280 + g * 128:1408 + g * 128]
            s_g = _nt(cg, bg)
            h_t = st[:, gl]
            hin_ref[:, gl] = h_t
            yoff = _nn(cg, h_t)
            xw = []
            for mm in range(4):
                m = 4 * g + mm
                ls = slice(m * LANE, (m + 1) * LANE)
                x2 = x_ref[:, ls]
                ws = []
                for hh in range(2):
                    h = lo + 2 * m + hh
                    lm = jnp.exp(jnp.where(msk, acs[:, h:h + 1] - acs_t[h:h + 1, :], -jnp.inf))
                    ws.append(s_g * lm * dt_t[h:h + 1, :])
                y2 = _nn(jnp.concatenate(ws, axis=1), _halves(x2, lane_lt))
                y_ref[:, ls] = y2 + yoff[:, mm * LANE:(mm + 1) * LANE] * _pair_sel(eacs, lo, m, lane_lt)
                xw.append(x2 * _pair_sel(tew, lo, m, lane_lt))
            st[:, gl] = _head_lanes(eal, lo, g) * h_t + _tn_dot(bg, jnp.concatenate(xw, axis=1))

    return _call(body, "ssd_fwd_r" if rev else "ssd_fwd_f", (nc,),
                 [pl.BlockSpec((Q, XBC), lambda s: (order(s), 0)), pl.BlockSpec((Q, LANE), lambda s: (order(s), 0)),
                  _full(8, LANE)],
                 [pl.BlockSpec((Q, D), lambda s: (order(s), 0)), pl.BlockSpec((None, LANE, D), lambda s: (order(s), 0, 0))],
                 [_sds((n_rows, D)), _sds((nc, LANE, D))], scratch=[pltpu.VMEM((LANE, D), F32)])(xbc, dtr, sp)


def _ssd_bwd(xbc, dtr, dy, hin, sp, dl, eh, ncc, rev):
    n_rows = xbc.shape[0]
    nc = n_rows // Q
    lo = 16 if rev else 0
    fwd_order = _scan_order(nc, ncc, rev)
    order = lambda s: fwd_order(nc - 1 - s)
    with_skip = not rev

    def body(x_ref, dtr_ref, dy_ref, hin_ref, sp_ref, dl_ref, eh_ref, dx_ref, ddtr_ref, pg_ref, dst):
        @pl.when(pl.program_id(0) == 0)
        def _():
            dst[...] = jnp.zeros_like(dst)
            pg_ref[...] = jnp.zeros_like(pg_ref)
        dtr_v = dtr_ref[...]
        dt, a_neg, acs, msk, tri, last = _ssd_prep(dtr_v, sp_ref[...], rev)
        acs_t = acs.T
        r = lax.broadcasted_iota(jnp.int32, (Q, Q), 0)
        c = lax.broadcasted_iota(jnp.int32, (Q, Q), 1)
        msk_t = (c <= r) if rev else (c >= r)
        eacs = jnp.exp(acs)
        eal = jnp.exp(acs[last:last + 1, :])
        te = jnp.exp(acs[last:last + 1, :] - acs)
        lane = lax.broadcasted_iota(jnp.int32, (Q, LANE), 1)
        lane1 = lax.broadcasted_iota(jnp.int32, (1, LANE), 1)
        lane_lt = lane < 64
        dacs = jnp.zeros((Q, LANE), F32)
        ddt_x = jnp.zeros((Q, LANE), F32)
        dlast = jnp.zeros((1, LANE), F32)
        hs_rows = []
        sub16 = lax.broadcasted_iota(jnp.int32, (16, Q), 0)
        dacs_t = jnp.zeros((16, Q), F32)
        for g in range(2):
            gl = slice(g * 512, (g + 1) * 512)
            bg = x_ref[:, 1024 + g * 128:1152 + g * 128]
            cg = x_ref[:, 1280 + g * 128:1408 + g * 128]
            s_g = _nt(cg, bg)
            s_gt = _nt(bg, cg)
            h_t, dh_t = hin_ref[:, gl], dst[:, gl]
            bh = _nn(bg, dh_t)
            yoff = _nn(cg, h_t)
            d_s = jnp.zeros((Q, Q), F32)
            edy, exd = [], []
            for mm in range(4):
                m = 4 * g + mm
                ls = slice(m * LANE, (m + 1) * LANE)
                x2, dy2 = x_ref[:, ls], dy_ref[:, ls]
                bh2 = bh[:, mm * LANE:(mm + 1) * LANE]
                dtm, em, eam = (_pair_sel(v, lo, m, lane_lt) for v in (dt, te, eacs))
                xd2 = x2 * dtm
                lms, mts = [], []
                for hh in range(2):
                    h = lo + 2 * m + hh
                    col, row = acs[:, h:h + 1], acs_t[h:h + 1, :]
                    lms.append(jnp.exp(jnp.where(msk, col - row, -jnp.inf)))
                    mts.append(s_gt * jnp.exp(jnp.where(msk_t, row - col, -jnp.inf)))
                dy_st = _halves(dy2, lane_lt)
                dxd2 = em * bh2 + _nn(jnp.concatenate(mts, axis=1), dy_st)
                dm_st = _nt(dy_st, xd2)
                dmt_st = _nt(_halves(xd2, lane_lt), dy2)
                d_s = d_s + dm_st[:Q] * lms[0] + dm_st[Q:] * lms[1]
                v1, v2, v3 = dy2 * yoff[:, mm * LANE:(mm + 1) * LANE] * eam, dxd2 * x2, xd2 * bh2 * em
                for hh in range(2):
                    h = lo + 2 * m + hh
                    half = lane_lt == (hh == 0)
                    g_rows = _sum0(dmt_st[hh * Q:(hh + 1) * Q] * mts[hh]) - _sum0(dm_st[hh * Q:(hh + 1) * Q] * s_g * lms[hh])
                    dacs_t = jnp.where(sub16 == 2 * m + hh, g_rows, dacs_t)
                    r1 = jnp.sum(jnp.where(half, v1, 0.0), axis=1, keepdims=True)
                    r2 = jnp.sum(jnp.where(half, v2, 0.0), axis=1, keepdims=True)
                    r3 = jnp.sum(jnp.where(half, v3, 0.0), axis=1, keepdims=True)
                    dacs = dacs + jnp.where(lane == h, r1 - r3, 0.0)
                    ddt_x = ddt_x + jnp.where(lane == h, r2, 0.0)
                    dlast = dlast + jnp.where(lane1 == h, _sum0(r3), 0.0)
                dx2 = dxd2 * dtm
                if with_skip:
                    dx2 = dx2 + dl_ref[:, ls] * dy2
                dx_ref[:, ls] = dx2
                edy.append(eam * dy2)
                exd.append(em * xd2)
            edy, exd = jnp.concatenate(edy, axis=1), jnp.concatenate(exd, axis=1)
            hs_rows.append(_sum0(h_t * dh_t))
            dst[:, gl] = _head_lanes(eal, lo, g) * dh_t + _tn_dot(cg, edy)
            dx_ref[:, 1024 + g * 128:1152 + g * 128] = _tn_dot(d_s, cg) + _nt(exd, dh_t)
            dx_ref[:, 1280 + g * 128:1408 + g * 128] = _nn(d_s, bg) + _nt(edy, h_t)
        hs = jnp.broadcast_to(jnp.concatenate(hs_rows, axis=1), (8, D))
        hsum = jnp.dot(hs, eh_ref[...], precision=HI, preferred_element_type=F32)[0:1]
        dlast = dlast + eal * hsum
        dacs = dacs + jnp.concatenate([jnp.zeros((lo, Q), F32)] * (lo > 0) + [dacs_t, jnp.zeros((LANE - 16 - lo, Q), F32)],
                                      axis=0).T
        rowi = lax.broadcasted_iota(jnp.int32, (Q, LANE), 0)
        dacs = dacs + jnp.where(rowi == last, dlast, 0.0)
        da = lax.dot_general(tri, dacs, (((0,), (0,)), ((), ())), precision=HI, preferred_element_type=F32)
        ddt = ddt_x + da * a_neg
        mine = (lane >= lo) & (lane < lo + 16)
        ddtr = jnp.where(mine, ddt * jax.nn.sigmoid(dtr_v + sp_ref[0:1]), 0.0)
        ddtr_ref[...] = ddtr
        pg_ref[0:1] += _sum0(ddtr)
        pg_ref[1:2] += jnp.where(mine[0:1], _sum0(da * dt) * a_neg, 0.0)

    blk = lambda w_: pl.BlockSpec((Q, w_), lambda s: (order(s), 0))
    return _call(body, "ssd_bwd_r" if rev else "ssd_bwd_f", (nc,),
                 [blk(XBC), blk(LANE), blk(D), pl.BlockSpec((None, LANE, D), lambda s: (order(s), 0, 0)),
                  _full(8, LANE), _full(1, D), _full(D, LANE)],
                 [blk(XBC), blk(LANE), _full(8, LANE)],
                 [_sds((n_rows, XBC)), _sds((n_rows, LANE)), _sds((8, LANE))],
                 scratch=[pltpu.VMEM((LANE, D), F32)])(xbc, dtr, dy, hin, sp, dl, eh)


def _hy4_fwd(h, yf, yb, xbc, z, ysg, mp, dl, ng, wout, nct):
    n_rows = h.shape[0]

    def body(h_ref, yf_ref, yb_ref, xs_ref, z_ref, ysg_ref, mp_ref, dl_ref, ng_ref, wout_ref, hn_ref, yssd_ref, out_ref):
        ytot = yf_ref[...] + yb_ref[...] + dl_ref[...] * xs_ref[...]
        yssd = _gate_norm(ytot, z_ref[...], ng_ref[...]).astype(MXU)
        yssd_ref[...] = yssd.astype(ACT)
        out = (jnp.dot(yssd, wout_ref[0:D, :], preferred_element_type=F32)
               + jnp.dot(ysg_ref[...].astype(MXU), wout_ref[D:2 * D, :], preferred_element_type=F32))
        out_ref[...] = out
        hn_ref[...] = h_ref[...] + mp_ref[3:4] * out

    return _call(body, "hy4_fwd", (n_rows // T,),
                 [_rows(T, D), _rows(T, D, nct), _rows(T, D, nct), _rows(T, D, nct), _rows(T, D, nct), _rows(T, D),
                  _full(8, D), _full(1, D), _full(1, D), _wfull(2 * D, D)],
                 [_rows(T, D), _rows(T, D), _rows(T, D)],
                 [_sds((n_rows, D)), _sds((n_rows, D), ACT), _sds((n_rows, D))])(h, yf, yb, xbc, z, ysg, mp, dl, ng, wout)


def _hy4_bwd(dh, out, yf, yb, xbc, z, mp, dl, ng, wout, nct):
    n_lat = dh.shape[0]
    n_rows = yf.shape[0]

    def body(dh_ref, out_ref, yf_ref, yb_ref, xs_ref, z_ref, mp_ref, dl_ref, ng_ref, wout_ref,
             dy_ref, dz_ref, dysg_ref, doutb_ref, pg_ref):
        i = pl.program_id(0)

        @pl.when(i == 0)
        def _():
            pg_ref[...] = jnp.zeros_like(pg_ref)

        @pl.when(i < nct)
        def _():
            dy_ref[...] = jnp.zeros_like(dy_ref)
            dz_ref[...] = jnp.zeros_like(dz_ref)
            dysg_ref[...] = jnp.zeros_like(dysg_ref)
            doutb_ref[...] = jnp.zeros_like(doutb_ref)

        @pl.when(i >= nct)
        def _():
            dhp = dh_ref[...]
            doutb = (mp_ref[3:4] * dhp).astype(MXU)
            doutb_ref[...] = doutb.astype(ACT)
            dysg_ref[...] = _nt(doutb, wout_ref[D:2 * D, :])
            dyssd = _nt(doutb, wout_ref[0:D, :])
            xs = xs_ref[...]
            ytot = yf_ref[...] + yb_ref[...] + dl_ref[...] * xs
            _, vjp = jax.vjp(_gate_norm, ytot, z_ref[...], ng_ref[...])
            dytot, dz, dng = vjp(dyssd)
            dy_ref[...] = dytot
            dz_ref[...] = dz.astype(ACT)
            pg_ref[0:1] += _sum0(dhp * out_ref[...])
            pg_ref[1:2] += dng
            pg_ref[2:3] += _sum0(dytot * xs)

    return _call(body, "hy4_bwd", (n_rows // T,),
                 [_rows_lat(T, D, nct), _rows_lat(T, D, nct), _rows(T, D), _rows(T, D), _rows(T, D), _rows(T, D),
                  _full(8, D), _full(1, D), _full(1, D), _wfull(2 * D, D)],
                 [_rows(T, D), _rows(T, D), _rows(T, D), _rows_lat(T, D, nct), _full(8, D)],
                 [_sds((n_rows, D)), _sds((n_rows, D), ACT), _sds((n_rows, D)), _sds((n_lat, D), ACT), _sds((8, D))],
                 )(dh, out, yf, yb, xbc, z, mp, dl, ng, wout)


def _loss_bwd(h, tgt, fng):
    n_rows = h.shape[0]

    def body(h_ref, t_ref, g_ref, dh_ref, pg_ref, ls_ref):
        @pl.when(pl.program_id(0) == 0)
        def _():
            pg_ref[...] = jnp.zeros_like(pg_ref)
            ls_ref[...] = jnp.zeros_like(ls_ref)
        hv = h_ref[...]
        g = g_ref[...]
        r = lax.rsqrt(jnp.mean(hv * hv, axis=-1, keepdims=True) + EPS)
        n = hv * r
        e = n * g - t_ref[...]
        ls_ref[...] += 0.5 * jnp.sum(jnp.sum(e * e, axis=1, keepdims=True), axis=0, keepdims=True) * (1.0 / D)
        dyv = e * (1.0 / D)
        pg_ref[0:1] += _sum0(dyv * n)
        dn = dyv * g
        dh_ref[...] = r * (dn - n * jnp.mean(dn * n, axis=-1, keepdims=True))

    return _call(body, "loss_bwd", (n_rows // T,), [_rows(T, D), _rows(T, D), _full(1, D)],
                 [_rows(T, D), _full(8, D), _full(8, LANE)],
                 [_sds((n_rows, D)), _sds((8, D)), _sds((8, LANE))])(h, tgt, fng)


def _pad_rows(a, rows):
    return jnp.concatenate([a, jnp.zeros((rows - a.shape[0],) + a.shape[1:], a.dtype)], axis=0)


def _mp(*rows):
    return _pad_rows(jnp.stack(rows, axis=0), 8)


def _local_step(x, ctx, tgt, ada, cada0, w, late_w=None, early_grads=None, small_grads=None):
    n_lat, n_ctx = x.shape[0], ctx.shape[0]
    nct, ncc = n_ctx // T, n_ctx // Q
    a0 = [ada[0, k * D:(k + 1) * D] for k in range(6)]
    a1 = [ada[1, k * D:(k + 1) * D] for k in range(6)]
    c0 = [cada0[k * D:(k + 1) * D] for k in range(6)]
    g = {}

    mp2 = jnp.stack([_mp(w["norm_mix_g"][0], c0[0], c0[1]), _mp(w["norm_mix_g"][0], a0[0], a0[1], a0[2])], axis=0)
    mp_l0 = mp2[1]
    sgbt = _pad_cols(w["sg_b"][0].T, LANE)
    lng, lnb = w["sg_ln_g"][0][None], w["sg_ln_b"][0][None]
    z, uv, xbcp, dtr, ysg = _hy1_fwd(ctx, x, mp2, w["wz"], w["wuv"], w["wxbc"], w["wdt"], lng, lnb, w["sg_w"], sgbt, nct)
    cw = _pad_rows(w["ssd_conv_w"][0], 8)
    cb = w["ssd_conv_b"][0][None]
    xbc = _cv5_fwd(xbcp, cw, cb, nct)
    sp = _pad_rows(jnp.stack([_pad_cols(w["ssd_dt_bias"][0].reshape(1, 32), LANE)[0],
                              _pad_cols(w["ssd_a_log"][0].reshape(1, 32), LANE)[0]], axis=0), 8)
    dl = jnp.repeat(w["ssd_d"][0], 64)[None]
    ng = w["ssd_norm_g"][0][None]
    yf, hin_f = _ssd_fwd(xbc, dtr, sp, ncc, False)
    yb, hin_b = _ssd_fwd(xbc, dtr, sp, ncc, True)
    if late_w is not None:
        w = {**w, **late_w(yb)}
    h1, yssd, out0 =_hy4_fwd(x, yf, yb, xbc, z, ysg, mp_l0, dl, ng, w["hy_w_out"], nct)

    mpm0 = _mp(w["norm_mlp_g"][0], a0[3], a0[4], a0[5])
    h2, am0, ym0 = _mlp_fwd(h1, mpm0, w["wpack"], 0, "mlp0_fwd")

    mpc = _mp(w["norm_mix_g"][1], a1[0], a1[1], a1[2])
    wdw = _pad_rows(w["cf_w_dw"][0], 32)
    glu, acf = _cf1_fwd(h2, mpc, w["cf_w_pw1"], w["cf_b_pw1"])
    h3, cv, scf, ycf = _cf2_fwd(h2, glu, mpc, wdw, w["cf_b_dw"], w["cf_ln_g"], w["cf_ln_b"], w["cf_w_pw2"], w["cf_b_pw2"])

    mpm1 = _mp(w["norm_mlp_g"][1], a1[3], a1[4], a1[5])
    h4, am1, ym1 = _mlp_fwd(h3, mpm1, w["wpack"], 1, "mlp1_fwd")

    dh4, pg_f, ls = _loss_bwd(h4, tgt, w["final_norm_g"][None])
    loss = ls[0, 0]
    g["final_norm_g"] = pg_f[0]

    gp = {}
    dh3, da1, dy1, u1, pgm1 = _mlp_bwd(dh4, h3, am1, ym1, mpm1, w["wpack"], 1, "mlp1_bwd")
    gw1_1 = _tn(u1, da1, "tn_mlp1_w1", shard=("col", 1024))
    gw2_1 = _tn(am1, dy1, "tn_mlp1_w2", relu2=True, shard=("row", 1024))

    dcv, dycf, pgc2 = _cf2_bwd(dh3, ycf, cv, mpc, w["cf_ln_g"], w["cf_ln_b"], w["cf_w_pw2"])
    gp["cf_w_pw2"] = _tn(scf, dycf, "tn_cf_pw2", shard=("row", 256))
    dh2, dacf, ucf, pgc1, pbc1, dwdw = _cf1_bwd(dh3, h2, acf, dcv, glu, mpc, wdw, w["cf_w_pw1"])
    gp["cf_w_pw1"] = _tn(ucf, dacf, "tn_cf_pw1", shard=("col", 512)).reshape(4, 512, 1024)
    g["cf_b_pw2"], g["cf_ln_g"], g["cf_ln_b"], g["cf_b_dw"] = pgc2[1], pgc2[2], pgc2[3], pgc2[4]
    g["cf_b_pw1"] = pbc1[0]
    g["cf_w_dw"] = dwdw[:CK]

    dh1, da0, dy0, u0, pgm0 = _mlp_bwd(dh2, h1, am0, ym0, mpm0, w["wpack"], 0, "mlp0_bwd")
    gp["mlp_w1"] = jnp.concatenate([_tn(u0, da0, "tn_mlp0_w1", shard=("col", 1024)), gw1_1], axis=1)
    gp["mlp_w2"] = jnp.concatenate([_tn(am0, dy0, "tn_mlp0_w2", relu2=True, shard=("row", 1024)), gw2_1], axis=1)
    g["norm_mlp_g"] = jnp.stack([pgm0[0], pgm1[0]])

    dyt, dz, dysg, doutb, pg4 = _hy4_bwd(dh1, out0, yf, yb, xbc, z, mp_l0, dl, ng, w["hy_w_out"], nct)
    gp["hy_w_out"] = jnp.concatenate([_tn(yssd, doutb, "tn_out_ssd", shard=("row", 512)),
                                      _tn(ysg, doutb, "tn_out_sg", shard=("row", 512))], axis=0)
    if early_grads is not None:
        sp = sp + early_grads(gp)
    head_of_lane = jnp.arange(D, dtype=jnp.int32)[:, None] // 64
    col = jnp.arange(LANE, dtype=jnp.int32)[None, :]
    dxf, ddf, pgsf = _ssd_bwd(xbc, dtr, dyt, hin_f, sp, dl, (col == head_of_lane).astype(F32), ncc, False)
    dxb, ddb, pgsb = _ssd_bwd(xbc, dtr, dyt, hin_b, sp, dl, (col == head_of_lane + 16).astype(F32), ncc, True)
    dcv5, pgcb = _cv5_bwd1(xbcp, dxf, dxb, cw, cb, nct)
    dxbcp, dcw = _cv5_bwd2(dcv5, xbcp, cw, nct)
    dx, ucat, duv, ddt, pg2, pln, dsgw, dsgbt = _hy1_bwd(
        ctx, x, uv, dz, dxbcp, ddf, ddb, dysg, dh1, mp2, w["wz"], w["wuv"], w["wxbc"], w["wdt"], lng, lnb, w["sg_w"], sgbt, nct)
    g["ssd_conv_w"], g["ssd_conv_b"] = dcw[:SK], pgcb[0]
    pgs = pgsf + pgsb
    g["ssd_dt_bias"], g["ssd_a_log"] = pgs[0, :32].reshape(2, 16), pgs[1, :32].reshape(2, 16)
    g["ssd_d"] = jnp.sum(pg4[2].reshape(16, 64), axis=1)
    g["ssd_norm_g"] = pg4[1]
    g["sg_ln_g"], g["sg_ln_b"] = pln[0], pln[1]
    g["sg_w"], g["sg_b"] = dsgw, dsgbt[:, :SGG].T
    g["norm_mix_g"] = jnp.stack([pg2[0, 0] + pg2[1, 0], pgc1[0]])

    zero = jnp.zeros((D,), F32)
    d_ada = jnp.stack([jnp.concatenate([pg2[1, 1], pg2[1, 2], pg4[0], pgm0[1], pgm0[2], pgm0[3]]),
                       jnp.concatenate([pgc1[1], pgc1[2], pgc2[0], pgm1[1], pgm1[2], pgm1[3]])])
    d_cada0 = jnp.concatenate([pg2[0, 1], pg2[0, 2], zero, zero, zero, zero])
    if small_grads is not None:
        ucat, _ = lax.optimization_barrier((ucat, small_grads(g, d_ada, d_cada0)))
    gp["hy_w_in"] = _win_join(_tn(ucat, dz, "tn_in_z"), _tn(ucat, dxbcp, "tn_in_xbc"), _tn(ucat, ddt, "tn_in_dt"),
                              _tn(ucat, duv, "tn_in_uv"))
    g["pieces"] = gp
    return loss, dx, g, d_ada, d_cada0


def _pad_cols(a, cols):
    return jnp.concatenate([a, jnp.zeros(a.shape[:-1] + (cols - a.shape[-1],), a.dtype)], axis=-1)


MESH = pl.DeviceIdType.MESH
ANY = pl.BlockSpec(memory_space=pl.ANY)
IN_VMEM = pl.BlockSpec(memory_space=pltpu.VMEM)


def _coords():
    return lax.axis_index("x"), lax.axis_index("y"), lax.axis_index("c")


def _ag8(x, name):
    r, wd = x.shape

    def body(x_ref, o_ref, send, recv, lsem):
        mx, my, mc = _coords()
        me = 4 * mx + 2 * my + mc
        mine = pltpu.make_async_copy(x_ref, o_ref.at[me], lsem)
        mine.start()
        sent, peers = [], []
        for k in range(1, 8):
            px = 1 - mx if k & 4 else mx
            py = 1 - my if k & 2 else my
            pc = 1 - mc if k & 1 else mc
            cp = pltpu.make_async_remote_copy(src_ref=x_ref, dst_ref=o_ref.at[me], send_sem=send.at[k - 1],
                                              recv_sem=recv.at[k - 1], device_id=(px, py, pc), device_id_type=MESH)
            cp.start()
            sent.append(cp)
            peers.append((4 * px + 2 * py + pc, (px, py, pc)))
        for k in range(1, 8):
            slot, peer = peers[k - 1]
            pltpu.make_async_remote_copy(src_ref=x_ref, dst_ref=o_ref.at[slot], send_sem=send.at[k - 1],
                                         recv_sem=recv.at[k - 1], device_id=peer, device_id_type=MESH).wait_recv()
        for cp in sent:
            cp.wait_send()
        mine.wait()

    return pl.pallas_call(
        body, name=name, out_shape=_sds((8, r, wd), x.dtype), in_specs=[IN_VMEM], out_specs=IN_VMEM,
        scratch_shapes=[pltpu.SemaphoreType.DMA((7,)), pltpu.SemaphoreType.DMA((7,)), pltpu.SemaphoreType.DMA(())],
        compiler_params=pltpu.CompilerParams(vmem_limit_bytes=VMEM_LIMIT))(x)


HBM = pl.BlockSpec(memory_space=pltpu.HBM)
SEM = pl.BlockSpec(memory_space=pltpu.SEMAPHORE)
EFFECT = pltpu.SideEffectType.DATAFLOW_SIDE_EFFECTING


def _x4_peers(in_ref, land_ref, send, recv, a2a):
    mx, my, mc = _coords()
    me = 2 * mx + my
    out = []
    for k in range(1, 4):
        px = 1 - mx if k & 2 else mx
        py = 1 - my if k & 1 else my
        pj = 2 * px + py
        mk = functools.partial(pltpu.make_async_remote_copy, src_ref=in_ref.at[pj] if a2a else in_ref,
                               send_sem=send.at[k - 1], recv_sem=recv.at[k - 1], device_id=(px, py, mc), device_id_type=MESH)
        out.append((mk(dst_ref=land_ref.at[me]), mk(dst_ref=land_ref.at[pj])))
    return out


def _x4_start(buf, name, a2a):
    r, wd = buf.shape[-2:]

    def body(in_ref, land_ref, send, recv, in_thru, land_thru, token):
        for start, _ in _x4_peers(in_ref, land_ref, send, recv, a2a):
            start.start()
        token[...] = jnp.zeros_like(token)

    land = lax.empty((4, r, wd), buf.dtype)
    return pl.pallas_call(
        body, name=name,
        out_shape=(pltpu.SemaphoreType.DMA((3,)), pltpu.SemaphoreType.DMA((3,)), pltpu.HBM(buf.shape, buf.dtype),
                   pltpu.HBM(land.shape, land.dtype), _sds((8, LANE))),
        in_specs=(HBM, HBM), out_specs=(SEM, SEM, HBM, HBM, IN_VMEM), input_output_aliases={0: 2, 1: 3},
        compiler_params=pltpu.CompilerParams(has_side_effects=EFFECT),
    )(pltpu.with_memory_space_constraint(buf, pltpu.HBM), pltpu.with_memory_space_constraint(land, pltpu.HBM))


def _x4_wait(send, recv, buf_thru, land_thru, after, name, a2a):
    def body(in_ref, land_ref, send_ref, recv_ref, after_ref, in_dead, got_ref):
        for _, arrive in _x4_peers(in_ref, land_ref, send_ref, recv_ref, a2a):
            arrive.wait_send()
            arrive.wait_recv()

    return pl.pallas_call(
        body, name=name, out_shape=(pltpu.HBM(buf_thru.shape, buf_thru.dtype), pltpu.HBM(land_thru.shape, land_thru.dtype)),
        in_specs=(HBM, HBM, SEM, SEM, ANY), out_specs=(HBM, HBM), input_output_aliases={0: 0, 1: 1},
        compiler_params=pltpu.CompilerParams(has_side_effects=EFFECT),
    )(buf_thru, land_thru, send, recv, after)


def _ag8_peers(x_ref, land_ref, send, recv):
    mx, my, mc = _coords()
    me = 4 * mx + 2 * my + mc
    out = []
    for k in range(1, 8):
        px = 1 - mx if k & 4 else mx
        py = 1 - my if k & 2 else my
        pc = 1 - mc if k & 1 else mc
        mk = functools.partial(pltpu.make_async_remote_copy, src_ref=x_ref, send_sem=send.at[k - 1], recv_sem=recv.at[k - 1],
                               device_id=(px, py, pc), device_id_type=MESH)
        out.append((mk(dst_ref=land_ref.at[me]), mk(dst_ref=land_ref.at[4 * px + 2 * py + pc])))
    return out


def _split_start(x, land_shape, peers, n_copies, name):
    def body(x_ref, land_ref, send, recv, x_thru, land_thru, token):
        for start, _ in peers(x_ref, land_ref, send, recv):
            start.start()
        token[...] = jnp.zeros_like(token)

    land = lax.empty(land_shape, x.dtype)
    return pl.pallas_call(
        body, name=name,
        out_shape=(pltpu.SemaphoreType.DMA((n_copies,)), pltpu.SemaphoreType.DMA((n_copies,)), pltpu.HBM(x.shape, x.dtype),
                   pltpu.HBM(land.shape, land.dtype), _sds((8, LANE))),
        in_specs=(HBM, HBM), out_specs=(SEM, SEM, HBM, HBM, IN_VMEM), input_output_aliases={0: 2, 1: 3},
        compiler_params=pltpu.CompilerParams(has_side_effects=EFFECT),
    )(pltpu.with_memory_space_constraint(x, pltpu.HBM), pltpu.with_memory_space_constraint(land, pltpu.HBM))


def _split_wait(handle, after, peers, name):
    send, recv, x_thru, land_thru, _ = handle

    def body(x_ref, land_ref, send_ref, recv_ref, after_ref, x_dead, got_ref):
        for _, arrive in peers(x_ref, land_ref, send_ref, recv_ref):
            arrive.wait_send()
            arrive.wait_recv()

    return pl.pallas_call(
        body, name=name, out_shape=(pltpu.HBM(x_thru.shape, x_thru.dtype), pltpu.HBM(land_thru.shape, land_thru.dtype)),
        in_specs=(HBM, HBM, SEM, SEM, ANY), out_specs=(HBM, HBM), input_output_aliases={0: 0, 1: 1},
        compiler_params=pltpu.CompilerParams(has_side_effects=EFFECT),
    )(x_thru, land_thru, send, recv, after)


def _sib_peers(x_ref, land_ref, send, recv):
    mx, my, mc = _coords()
    cp = pltpu.make_async_remote_copy(src_ref=x_ref, dst_ref=land_ref, send_sem=send.at[0], recv_sem=recv.at[0],
                                      device_id=(mx, my, 1 - mc), device_id_type=MESH)
    return [(cp, cp)]


def _xchg_sib(x, name):
    def body(in_ref, o_ref, send, recv):
        mx, my, mc = _coords()
        cp = pltpu.make_async_remote_copy(src_ref=in_ref, dst_ref=o_ref, send_sem=send, recv_sem=recv,
                                          device_id=(mx, my, 1 - mc), device_id_type=MESH)
        cp.start()
        cp.wait_recv()
        cp.wait_send()

    return pl.pallas_call(
        body, name=name, out_shape=_sds(x.shape, x.dtype), in_specs=[ANY], out_specs=ANY,
        scratch_shapes=[pltpu.SemaphoreType.DMA(()), pltpu.SemaphoreType.DMA(())])(x)


def _sum_slots(gat, slots, name, tr=None):
    n, r, wd = gat.shape
    tr = r if tr is None else tr

    def body(g_ref, o_ref):
        acc = g_ref[slots[0]].astype(F32)
        for s in slots[1:]:
            acc = acc + g_ref[s].astype(F32)
        o_ref[...] = acc

    return _call(body, name, (r // tr,), [pl.BlockSpec((n, tr, wd), lambda i: (0, i, 0))], _rows(tr, wd), _sds((r, wd)))(gat)


def _add(a, b, name, tr):
    def body(a_ref, b_ref, o_ref):
        o_ref[...] = a_ref[...] + b_ref[...]

    r, wd = a.shape
    return _call(body, name, (r // tr,), [_rows(tr, wd), _rows(tr, wd)], _rows(tr, wd), _sds((r, wd)))(a, b)


def _ada_fwd(x16, ada_w_loc, ada_b_loc):
    nloc = ada_w_loc.shape[-1]

    def body(x_ref, w_ref, b_ref, s_ref, o_ref):
        s = _silu(x_ref[...])
        s_ref[...] = s
        o_ref[...] = jnp.dot(s, w_ref[...], precision=HI, preferred_element_type=F32) + b_ref[...]

    return _call(body, "ada_fwd", (2,),
                 [_full(16, D), pl.BlockSpec((None, D, nloc), lambda l: (l, 0, 0)), pl.BlockSpec((None, 1, nloc), lambda l: (l, 0, 0))],
                 [_full(16, D), pl.BlockSpec((None, 16, nloc), lambda l: (l, 0, 0))],
                 [_sds((16, D)), _sds((2, 16, nloc))])(x16, ada_w_loc, ada_b_loc[:, None, :])


def _ada_bwd(s16, d_loc, ada_w_loc):
    nloc = ada_w_loc.shape[-1]

    def body(s_ref, d_ref, w_ref, gw_ref, cp_ref):
        gw_ref[...] = lax.dot_general(s_ref[...], d_ref[...], (((0,), (0,)), ((), ())), precision=HI,
                                      preferred_element_type=F32)

        @pl.when(pl.program_id(0) == 0)
        def _():
            cp_ref[...] = lax.dot_general(d_ref[8:16, :], w_ref[...], (((1,), (1,)), ((), ())), precision=HI,
                                          preferred_element_type=F32)

    return _call(body, "ada_bwd", (2,),
                 [_full(16, D), pl.BlockSpec((None, 16, nloc), lambda l: (l, 0, 0)), pl.BlockSpec((None, D, nloc), lambda l: (l, 0, 0))],
                 [pl.BlockSpec((None, D, nloc), lambda l: (l, 0, 0)), _full(8, D)],
                 [_sds((2, D, nloc)), _sds((8, D))])(s16, d_loc, ada_w_loc)


def _cctx_grad(dscc, c_ctx):
    def body(d_ref, c_ref, o_ref):
        _, vjp = jax.vjp(_silu, c_ref[...])
        o_ref[...] = vjp(d_ref[...])[0]

    return _call(body, "cctx_grad", (1,), [_full(8, D), _full(8, D)], _full(8, D), _sds((8, D)))(dscc, c_ctx)


def _adamw_math(w, g, m, v):
    mn = ADAM_B1 * m + (1.0 - ADAM_B1) * g
    vn = ADAM_B2 * v + (1.0 - ADAM_B2) * jnp.square(g)
    c1 = 1.0 - ADAM_B1 ** ADAM_STEP
    c2 = 1.0 - ADAM_B2 ** ADAM_STEP
    return -ADAM_LR * ((mn / c1) / (jnp.sqrt(vn / c2) + ADAM_EPS) + ADAM_WD * w), mn, vn


def _adamw(w, g, m, v, name):
    n_l, r, wd = w.shape
    tr = 256 if r % 256 == 0 else r

    def body(w_ref, g_ref, m_ref, v_ref, d_ref, mo_ref, vo_ref):
        d_ref[...], mo_ref[...], vo_ref[...] = _adamw_math(w_ref[...], g_ref[...], m_ref[...], v_ref[...])

    spec = pl.BlockSpec((None, tr, wd), lambda a, i: (a, i, 0))
    return tuple(_call(body, name, (n_l, r // tr), [spec] * 4, [spec] * 3, [_sds(w.shape)] * 3)(w, g, m, v))


def _adamw_small(ws, gs, ms, vs, name):
    n = len(ws)
    shapes = [a.shape for a in ws]
    as2d = lambda a: a.reshape(-1, a.shape[-1])

    def body(*refs):
        ins, outs = refs[:4 * n], refs[4 * n:]
        for k in range(n):
            res = _adamw_math(ins[k][...], ins[n + k][...], ins[2 * n + k][...], ins[3 * n + k][...])
            for j in range(3):
                outs[j * n + k][...] = res[j]

    flat = [as2d(a) for group in (ws, gs, ms, vs) for a in group]
    specs = [_full(*a.shape) for a in flat]
    outs = _call(body, name, (1,), specs, specs[:n] * 3, [_sds(a.shape) for a in flat[:n]] * 3)(*flat)
    return tuple([outs[j * n + k].reshape(shapes[k]) for k in range(n)] for j in range(3))


ROW = 1024


def _nrows(size):
    return -(-size // ROW)


def _pack(arrs, rows_total, dtype=F32):
    parts = []
    for a in arrs:
        flat = a.reshape(-1).astype(dtype)
        pad = _nrows(flat.shape[0]) * ROW - flat.shape[0]
        parts.append(flat if pad == 0 else jnp.concatenate([flat, jnp.zeros((pad,), dtype)]))
    flat = jnp.concatenate(parts)
    out = flat.reshape(-1, ROW)
    return _pad_rows(out, rows_total)


def _unpack(buf, shapes):
    lead = buf.shape[:-2]
    out, r0 = [], 0
    for shp in shapes:
        size = 1
        for s in shp:
            size *= s
        nr = _nrows(size)
        piece = lax.slice_in_dim(buf, r0, r0 + nr, axis=len(lead))
        out.append(piece.reshape(lead + (nr * ROW,))[..., :size].reshape(lead + tuple(shp)))
        r0 += nr
    return out


SLOT = 16


def _slot_rows(size):
    return _round_up(size // ROW, SLOT)


def _pack_rows(arrs, rows_total, dtype):
    parts, used = [], 0
    for a in arrs:
        part = a.astype(dtype).reshape(-1, ROW)
        extra = _slot_rows(a.size) - part.shape[0]
        parts.append(part if extra == 0 else jnp.pad(part, ((0, extra), (0, 0))))
        used += _slot_rows(a.size)
    if rows_total > used:
        parts.append(jnp.zeros((rows_total - used, ROW), dtype))
    return jnp.concatenate(parts, axis=0)


def _unpack_rows(buf, shapes):
    lead = buf.shape[:-2]
    out, r0 = [], 0
    for shp in shapes:
        size = 1
        for s in shp:
            size *= s
        piece = lax.slice_in_dim(buf, r0, r0 + size // ROW, axis=len(lead))
        out.append(piece.reshape(lead + tuple(shp)))
        r0 += _slot_rows(size)
    return out


def _round_up(n, k):
    return -(-n // k) * k


WEIGHTS = ['c_ctx', 'ada_w', 'ada_b', 'norm_mix_g', 'norm_mlp_g', 'mlp_w1', 'mlp_w2', 'hy_w_in', 'ssd_conv_w', 'ssd_conv_b',
           'ssd_dt_bias', 'ssd_a_log', 'ssd_d', 'ssd_norm_g', 'sg_ln_g', 'sg_ln_b', 'sg_w', 'sg_b', 'hy_w_out', 'cf_w_pw1',
           'cf_b_pw1', 'cf_w_dw', 'cf_b_dw', 'cf_ln_g', 'cf_ln_b', 'cf_w_pw2', 'cf_b_pw2', 'final_norm_g']
BIG = {'mlp_w1': 2, 'mlp_w2': 1, 'hy_w_in': 2, 'hy_w_out': 1, 'cf_w_pw1': 2, 'cf_w_pw2': 1}
SMALL_SHARD = ['ssd_conv_w', 'cf_b_pw1', 'cf_w_dw', 'cf_b_dw', 'cf_ln_g', 'cf_ln_b', 'cf_b_pw2']
REP = ['norm_mix_g', 'norm_mlp_g', 'ssd_conv_b', 'ssd_dt_bias', 'ssd_a_log', 'ssd_d', 'ssd_norm_g', 'sg_ln_g', 'sg_ln_b',
       'sg_w', 'sg_b', 'final_norm_g']


def _gather_shards(stacked, axis):
    return jnp.concatenate([stacked[j] for j in range(4)], axis=axis)


def kernel(x, c, ctx, c_ctx, ada_w, ada_b, norm_mix_g, norm_mlp_g, mlp_w1, mlp_w2, hy_w_in, ssd_conv_w, ssd_conv_b, ssd_dt_bias, ssd_a_log, ssd_d, ssd_norm_g, sg_ln_g, sg_ln_b, sg_w, sg_b, hy_w_out, cf_w_pw1, cf_b_pw1, cf_w_dw, cf_b_dw, cf_ln_g, cf_ln_b, cf_w_pw2, cf_b_pw2, final_norm_g, loss_target, m_c_ctx, m_ada_w, m_ada_b, m_norm_mix_g, m_norm_mlp_g, m_mlp_w1, m_mlp_w2, m_hy_w_in, m_ssd_conv_w, m_ssd_conv_b, m_ssd_dt_bias, m_ssd_a_log, m_ssd_d, m_ssd_norm_g, m_sg_ln_g, m_sg_ln_b, m_sg_w, m_sg_b, m_hy_w_out, m_cf_w_pw1, m_cf_b_pw1, m_cf_w_dw, m_cf_b_dw, m_cf_ln_g, m_cf_ln_b, m_cf_w_pw2, m_cf_b_pw2, m_final_norm_g, v_c_ctx, v_ada_w, v_ada_b, v_norm_mix_g, v_norm_mlp_g, v_mlp_w1, v_mlp_w2, v_hy_w_in, v_ssd_conv_w, v_ssd_conv_b, v_ssd_dt_bias, v_ssd_a_log, v_ssd_d, v_ssd_norm_g, v_sg_ln_g, v_sg_ln_b, v_sg_w, v_sg_b, v_hy_w_out, v_cf_w_pw1, v_cf_b_pw1, v_cf_w_dw, v_cf_b_dw, v_cf_ln_g, v_cf_ln_b, v_cf_w_pw2, v_cf_b_pw2, v_final_norm_g):
    args = locals()
    wl = {n: args[n] for n in WEIGHTS}
    ml = {n: args["m_" + n] for n in WEIGHTS}
    vl = {n: args["v_" + n] for n in WEIGHTS}
    mx, my, mc = _coords()
    me = 4 * mx + 2 * my + mc
    shard = 2 * mx + my
    even = (0, 2, 4, 6)

    def start_gather(names, name, tie=None):
        rows = sum(_slot_rows(wl[n].size) for n in names)
        buf = _pack_rows([wl[n] for n in names], rows, MXU)
        if tie is not None:
            buf, _ = lax.optimization_barrier((buf, tie))
        return _x4_start(buf, name, a2a=False)

    def finish_gather(handle, names, after, name):
        send, recv, own, land, _ = handle
        own, land = _x4_wait(send, recv, own, land, after, name, a2a=False)
        got = lax.dynamic_update_slice(land, own[None], (shard, 0, 0))
        shapes = [wl[n].shape for n in names]
        wfull = {n: _gather_shards(st, BIG[n]) for n, st in zip(names, _unpack_rows(got, shapes))}
        return wfull, got

    rest_names = [n for n in BIG if n != "hy_w_in"]
    h_in = _x4_start(hy_w_in[0].astype(MXU), "agw_in_start", a2a=False)
    c = c + h_in[4][0, 0]

    small_shapes = [wl[n].shape for n in SMALL_SHARD]
    blk1 = _pack([c] + [wl[n] for n in SMALL_SHARD], 24)
    got1 = _ag8(blk1, "ag_cond")
    x16 = _pad_rows(jnp.concatenate([got1[:, 0, :], c_ctx[None]], axis=0), 16)
    small_full = {}
    for n, parts in zip(SMALL_SHARD, _unpack(got1[:, 1:, :], small_shapes)):
        small_full[n] = jnp.concatenate([parts[s] for s in even], axis=-1)

    nloc = ada_w.shape[-1]
    ada_b_loc = lax.dynamic_slice_in_dim(ada_b, shard * nloc, nloc, axis=1)
    s16, ada_loc = _ada_fwd(x16, ada_w, ada_b_loc)
    got2 = _ag8(ada_loc.reshape(32, nloc), "ag_ada").reshape(8, 2, 16, nloc)
    ada_full = jnp.concatenate([got2[s] for s in even], axis=-1)
    ada_me = lax.dynamic_slice_in_dim(ada_full, me, 1, axis=1)[:, 0, :]
    cada0 = ada_full[0, 8, :]

    w = {n: wl[n] for n in WEIGHTS if n not in BIG and n not in SMALL_SHARD}
    w.update(small_full)
    h_rest = start_gather(rest_names, "agw_rest_start", tie=ada_me)
    send, recv, own, land, _ = h_in
    own, land = _x4_wait(send, recv, own, land, h_rest[4], "agw_in_wait", a2a=False)
    w["wz"], w["wxbc"], w["wdt"], w["wuv"] = _win_split(lax.dynamic_update_slice(land, own[None], (shard, 0, 0)))
    w["sg_w"] = sg_w[0].astype(MXU)

    def late_w(after):
        wfull, got = finish_gather(h_rest, rest_names, after, "agw_rest_wait")
        return {"hy_w_out": wfull["hy_w_out"][0], "wpack": got,
                "cf_w_pw1": wfull["cf_w_pw1"][0], "cf_w_pw2": wfull["cf_w_pw2"][0]}

    full_shape = {n: wl[n].shape for n in WEIGHTS}
    for n in BIG:
        full_shape[n] = tuple(s * 4 if a == BIG[n] else s for a, s in enumerate(wl[n].shape))
    for n in SMALL_SHARD:
        full_shape[n] = wl[n].shape[:-1] + (wl[n].shape[-1] * 4,)

    early_names = ["mlp_w1", "mlp_w2", "cf_w_pw1", "cf_w_pw2", "hy_w_out"]
    early = {}

    def early_grads(gp):
        used = sum(gp[n].shape[1] for n in early_names)
        parts = [gp[n] for n in early_names] + [jnp.zeros((4, _round_up(used, 512) - used, ROW), jnp.bfloat16)]
        early["h"] = _x4_start(jnp.concatenate(parts, axis=1), "a2a_early_start", a2a=True)
        return early["h"][4][0, 0]

    sm_names = REP + SMALL_SHARD
    r_ada = sum(_nrows(wl[n].size * (4 if n in SMALL_SHARD else 1)) for n in sm_names)
    small = {}

    def small_grads(g_, d_ada_, d_cada0_):
        buf = _pack([g_[n] for n in sm_names] + [d_ada_, d_cada0_], _round_up(r_ada + 18, 16), jnp.bfloat16)
        small["h"] = _split_start(buf, (8,) + buf.shape, _ag8_peers, 7, "ag_small_start")
        return small["h"][4][0, 0]

    loss_part, dx, g, d_ada, d_cada0 = _local_step(x[0], ctx[0], loss_target[0], ada_me, cada0, w, late_w, early_grads,
                                                   small_grads)
    loss = lax.psum(loss_part, ("x", "y", "c"))
    gp_last = g["pieces"]["hy_w_in"]

    delta, new_m, new_v, grads = {}, {}, {}, {}
    h_last = _x4_start(gp_last, "a2a_last_start", a2a=True)
    send, recv, own, land, _ = early["h"]
    own, land = _x4_wait(send, recv, own, land, h_last[4], "a2a_early_wait", a2a=True)
    got = lax.dynamic_update_slice(land, lax.dynamic_slice_in_dim(own, shard, 1, axis=0), (shard, 0, 0))
    part_early = _sum_slots(got, (0, 1, 2, 3), "sum_grads_early", tr=512)
    h_swap = _split_start(part_early, part_early.shape, _sib_peers, 1, "swap_early_start")

    small_in, land3 = _split_wait(small["h"], h_swap[4], _ag8_peers, "ag_small_wait")
    got3 = lax.dynamic_update_slice(land3, small_in[None], (me, 0, 0))
    tot3 = _sum_slots(got3, tuple(range(8)), "sum_small")
    sm_tot = _unpack(tot3, [full_shape[n] for n in sm_names] + [(2, 6 * D), (6 * D,)])
    grads.update(zip(sm_names, sm_tot[:-2]))
    for n in SMALL_SHARD:
        k = wl[n].shape[-1]
        grads[n] = lax.dynamic_slice_in_dim(grads[n], shard * k, k, axis=grads[n].ndim - 1)
    dada_tot, dcada_tot = sm_tot[-2], sm_tot[-1]
    grads["ada_b"] = dada_tot.at[0].add(dcada_tot)
    dada_all = got3[:, r_ada:r_ada + 12, :].astype(F32).reshape(8, 2, 6 * D)
    d16 = jnp.concatenate([jnp.transpose(dada_all, (1, 0, 2)),
                           jnp.stack([dcada_tot, jnp.zeros_like(dcada_tot)])[:, None, :],
                           jnp.zeros((2, 7, 6 * D), F32)], axis=1)
    d_loc = lax.dynamic_slice_in_dim(d16, shard * nloc, nloc, axis=2)
    grads["ada_w"], cpart = _ada_bwd(s16, d_loc, ada_w)
    got4 = _ag8(cpart, "ag_cctx")
    dscc = _sum_slots(got4, even, "sum_cctx")
    grads["c_ctx"] = _cctx_grad(dscc, _pad_rows(c_ctx[None], 8))[0]

    delta["ada_w"], new_m["ada_w"], new_v["ada_w"] = _adamw(ada_w, grads["ada_w"], ml["ada_w"], vl["ada_w"], "adamw_ada_w")
    small_names = ["c_ctx", "ada_b"] + REP + SMALL_SHARD
    outs = _adamw_small(*([src[n].reshape(wl[n].shape) for n in small_names] for src in (wl, grads, ml, vl)), "adamw_small")
    for dst, vals in zip((delta, new_m, new_v), outs):
        dst.update(zip(small_names, vals))

    def step(names, tot):
        if len(names) == 1:
            grads[names[0]] = tot.reshape(wl[names[0]].shape)
        else:
            grads.update(zip(names, _unpack_rows(tot, [wl[n].shape for n in names])))
        for n in names:
            delta[n], new_m[n], new_v[n] = _adamw(wl[n], grads[n], ml[n], vl[n], "adamw_" + n)

    part_early, sib_early = _split_wait(h_swap, delta["ada_w"], _sib_peers, "swap_early_wait")
    step(early_names, _add(part_early, sib_early, "add_grads_early", 512))
    send, recv, own, land, _ = h_last
    own, land = _x4_wait(send, recv, own, land, delta["mlp_w1"], "a2a_last_wait", a2a=True)
    got = lax.dynamic_update_slice(land, lax.dynamic_slice_in_dim(own, shard, 1, axis=0), (shard, 0, 0))
    part_last = _sum_slots(got, (0, 1, 2, 3), "sum_grads_last", tr=512)
    step(["hy_w_in"], _add(part_last, _xchg_sib(part_last, "swap_grads_last"), "add_grads_last", 512))

    return (loss, dx[None], *[grads[n].reshape(wl[n].shape) for n in WEIGHTS], *[delta[n] for n in WEIGHTS],
            *[new_m[n] for n in WEIGHTS], *[new_v[n] for n in WEIGHTS])
```

```python
import functools

import jax
import jax.numpy as jnp
from jax import lax
from jax.experimental import pallas as pl
from jax.experimental.pallas import tpu as pltpu

F32 = jnp.float32
MXU = jnp.bfloat16
ACT = jnp.bfloat16
HI = lax.Precision.HIGHEST
EPS = 1e-6

D = 1024
HID = 4096
XBC = 1536
Q = 128
SGG = 8
CK = 31
SK = 5
T = 256
LANE = 128
VMEM_LIMIT = 56 * 1024 * 1024

ADAM_LR, ADAM_B1, ADAM_B2, ADAM_EPS, ADAM_WD, ADAM_STEP = 0.001, 0.9, 0.999, 1e-08, 0.01, 10


def _call(body, name, grid, in_specs, out_specs, out_shape, scratch=()):
    return pl.pallas_call(
        body, name=name, grid=grid, in_specs=in_specs, out_specs=out_specs, out_shape=out_shape,
        scratch_shapes=list(scratch),
        compiler_params=pltpu.CompilerParams(dimension_semantics=("arbitrary",) * len(grid),
                                             vmem_limit_bytes=VMEM_LIMIT))


def _sds(shape, dt=F32):
    return jax.ShapeDtypeStruct(tuple(shape), dt)


def _rows(t, w, off=0, lane_blk=0):
    return pl.BlockSpec((t, w), lambda i: (i + off, lane_blk))


def _rows_lat(t, w, nct):
    return pl.BlockSpec((t, w), lambda i: (jnp.maximum(i - nct, 0), 0))


def _full(*shape):
    return pl.BlockSpec(shape, lambda *_: (0,) * len(shape))


def _wfull(*shape):
    return pl.BlockSpec(shape, lambda *_: (0,) * len(shape), pipeline_mode=pl.Buffered(1))


def _halo(w, hb, nrows):
    r, nb = T // hb, nrows // hb
    prev = pl.BlockSpec((hb, w), lambda i: (jnp.maximum(i * r - 1, 0), 0))
    nxt = pl.BlockSpec((hb, w), lambda i: (jnp.minimum((i + 1) * r, nb - 1), 0))
    return prev, nxt


def _nn(a, b):
    return jnp.dot(a.astype(MXU), b.astype(MXU), preferred_element_type=F32)


def _nt(a, b):
    return lax.dot_general(a.astype(MXU), b.astype(MXU), (((1,), (1,)), ((), ())), preferred_element_type=F32)


def _tn_dot(a, b):
    return lax.dot_general(a.astype(MXU), b.astype(MXU), (((0,), (0,)), ((), ())), preferred_element_type=F32)


def _sum0(x):
    return jnp.sum(x, axis=0, keepdims=True)


def _silu(x):
    return x * jax.nn.sigmoid(x)


def _gelu(x):
    return jax.nn.gelu(x, approximate=True)


def _mod(h, g, sh, sc):
    n = h * lax.rsqrt(jnp.mean(h * h, axis=-1, keepdims=True) + EPS)
    return n * g * (1.0 + sc) + sh


def _ln(x, g, b):
    xc = x - jnp.mean(x, axis=-1, keepdims=True)
    return xc * lax.rsqrt(jnp.mean(xc * xc, axis=-1, keepdims=True) + EPS) * g + b


def _gate_norm(ytot, z, ng):
    yg = ytot * _silu(z)
    halves = []
    for k in range(2):
        seg = yg[:, k * 512:(k + 1) * 512]
        halves.append(seg * lax.rsqrt(jnp.mean(seg * seg, axis=-1, keepdims=True) + EPS) * ng[:, k * 512:(k + 1) * 512])
    return jnp.concatenate(halves, axis=-1)


def _fill_ext(ext_ref, prev_ref, cur_ref, next_ref, hb, first, last):
    ext_ref[0:hb, :] = jnp.where(first, 0.0, prev_ref[...])
    ext_ref[hb:hb + T, :] = cur_ref[...]
    ext_ref[hb + T:hb + T + hb, :] = jnp.where(last, 0.0, next_ref[...])


def _conv(ext_ref, w_ref, k_taps, hb, lanes):
    off = hb - k_taps // 2
    acc = ext_ref[pl.ds(off, T), lanes] * w_ref[0:1, lanes]
    for k in range(1, k_taps):
        acc = acc + ext_ref[pl.ds(off + k, T), lanes] * w_ref[k:k + 1, lanes]
    return acc


def _conv_tr(ext_ref, w_ref, k_taps, hb, lanes):
    off = hb + k_taps // 2
    acc = ext_ref[pl.ds(off, T), lanes] * w_ref[0:1, lanes]
    for k in range(1, k_taps):
        acc = acc + ext_ref[pl.ds(off - k, T), lanes] * w_ref[k:k + 1, lanes]
    return acc


def _conv_wide(ext_ref, w_ref, k_taps, hb, lanes, flip=False):
    base = hb - k_taps // 2
    acc = None
    for b in range(8):
        taps = [k for k in range(k_taps) if (base + k) % 8 == b]
        if not taps:
            continue
        p = None
        for k in taps:
            wi = (k_taps - 1 - k) if flip else k
            term = ext_ref[pl.ds(base + k - b, T + 8), lanes] * w_ref[wi:wi + 1, lanes]
            p = term if p is None else p + term
        acc = p[b:b + T] if acc is None else acc + p[b:b + T]
    return acc


def _conv_dw_wide(dw_ref, d_ref, xext_ref, k_taps, hb, lanes):
    base = hb - k_taps // 2
    d = d_ref[:, lanes]
    for b in range(8):
        taps = [k for k in range(k_taps) if (base + k) % 8 == b]
        if not taps:
            continue
        lo_off = base + taps[0] - b
        span = base + taps[-1] - b - lo_off
        xs = xext_ref[pl.ds(lo_off + b, T + span), lanes]
        for k in taps:
            a = base + k - b - lo_off
            dw_ref[k:k + 1, lanes] += _sum0(d * xs[a:a + T])


def _conv_dw(dw_ref, d_ref, xext_ref, k_taps, hb, lanes):
    off = hb - k_taps // 2
    d = d_ref[:, lanes]
    for k in range(k_taps):
        dw_ref[k:k + 1, lanes] += _sum0(d * xext_ref[pl.ds(off + k, T), lanes])


def _tn(a, b, name, relu2=False, shard=None):
    m_rows, ka = a.shape
    n = b.shape[1]
    tm = next(t for t in (1024, 768, 512, 256) if m_rows % t == 0)
    tk = min(ka, 1024)
    tn = n if n <= 1024 else next(t for t in (1024, 768, 512, 384, 256, 128) if n % t == 0)
    if shard is not None and shard[0] == "col":
        tn = shard[1]
    if shard is not None and shard[0] == "row":
        tk = shard[1]
    n_m = m_rows // tm

    def body(a_ref, b_ref, o_ref, *acc):
        acc_ref = acc[0] if acc else o_ref

        @pl.when(pl.program_id(2) == 0)
        def _():
            acc_ref[...] = jnp.zeros_like(acc_ref)
        av = a_ref[...]
        if relu2:
            av = jnp.square(jnp.maximum(av.astype(F32), 0.0))
        acc_ref[...] += _tn_dot(av, b_ref[...])
        if acc:
            @pl.when(pl.program_id(2) == n_m - 1)
            def _():
                o_ref[...] = acc_ref[...].astype(o_ref.dtype)

    in_specs = [pl.BlockSpec((tm, tk), lambda k, j, m: (m, k)), pl.BlockSpec((tm, tn), lambda k, j, m: (m, j))]
    if shard is None:
        return _call(body, name, (ka // tk, n // tn, n_m), in_specs,
                     pl.BlockSpec((tk, tn), lambda k, j, m: (k, j)), _sds((ka, n)))(a, b)
    if shard[0] == "col":
        out_spec, out_shape = pl.BlockSpec((None, tk, tn), lambda k, j, m: (j, k, 0)), _sds((n // tn, ka, tn), jnp.bfloat16)
    else:
        out_spec, out_shape = pl.BlockSpec((None, tk, tn), lambda k, j, m: (k, 0, j)), _sds((ka // tk, tk, n), jnp.bfloat16)
    return _call(body, name, (ka // tk, n // tn, n_m), in_specs, out_spec, out_shape,
                 scratch=[pltpu.VMEM((tk, tn), F32)])(a, b)


def _mlp_fwd(h, mp, wpack, layer, name):
    n_rows = h.shape[0]

    def body(h_ref, mp_ref, w1_ref, w2_ref, hn_ref, a_ref, y_ref):
        hv = h_ref[...]
        u = _mod(hv, mp_ref[0:1], mp_ref[1:2], mp_ref[2:3]).astype(MXU)
        acc = jnp.zeros((T, D), F32)
        for j in range(HID // 1024):
            cs = slice(j * 1024, (j + 1) * 1024)
            a = jnp.dot(u, w1_ref[j], preferred_element_type=F32)
            a_ref[:, cs] = a.astype(ACT)
            acc = acc + jnp.dot(jnp.square(jnp.maximum(a, 0.0)).astype(MXU), w2_ref[j], preferred_element_type=F32)
        y_ref[...] = acc
        hn_ref[...] = hv + mp_ref[3:4] * acc

    return _call(body, name, (n_rows // T,),
                 [_rows(T, D), _full(8, D), _mlp_wspec(layer), _mlp_wspec(2 + layer)],
                 [_rows(T, D), _rows(T, HID), _rows(T, D)],
                 [_sds((n_rows, D)), _sds((n_rows, HID), ACT), _sds((n_rows, D))])(h, mp, wpack, wpack)


def _mlp_wspec(row_block):
    return pl.BlockSpec((4, 1024, 1024), lambda i: (0, row_block, 0), pipeline_mode=pl.Buffered(1))


def _mlp_bwd(dh, h, a, y, mp, wpack, layer, name):
    n_rows = h.shape[0]

    def body(dh_ref, h_ref, a_ref, y_ref, mp_ref, w1_ref, w2_ref, dho_ref, da_ref, dyb_ref, ub_ref, pg_ref):
        dhp = dh_ref[...]
        u, vjp = jax.vjp(_mod, h_ref[...], mp_ref[0:1], mp_ref[1:2], mp_ref[2:3])
        ub_ref[...] = u.astype(ACT)
        dyb = (mp_ref[3:4] * dhp).astype(MXU)
        dyb_ref[...] = dyb.astype(ACT)
        du = jnp.zeros((T, D), F32)
        for j in range(HID // 1024):
            cs = slice(j * 1024, (j + 1) * 1024)
            dp = _nt(dyb, w2_ref[j])
            da = dp * 2.0 * jnp.maximum(a_ref[:, cs].astype(F32), 0.0)
            da_ref[:, cs] = da.astype(ACT)
            du = du + _nt(da, w1_ref[j])
        dhn, dg, dsh, dsc = vjp(du)
        dho_ref[...] = dhp + dhn

        @pl.when(pl.program_id(0) == 0)
        def _():
            pg_ref[...] = jnp.zeros_like(pg_ref)
        pg_ref[0:1] += dg
        pg_ref[1:2] += dsh
        pg_ref[2:3] += dsc
        pg_ref[3:4] += _sum0(dhp * y_ref[...])

    return _call(body, name, (n_rows // T,),
                 [_rows(T, D), _rows(T, D), _rows(T, HID), _rows(T, D), _full(8, D), _mlp_wspec(layer), _mlp_wspec(2 + layer)],
                 [_rows(T, D), _rows(T, HID), _rows(T, D), _rows(T, D), _full(8, D)],
                 [_sds((n_rows, D)), _sds((n_rows, HID), ACT), _sds((n_rows, D), ACT), _sds((n_rows, D), ACT),
                  _sds((8, D))])(dh, h, a, y, mp, wpack, wpack)


def _cf1_fwd(h, mp, w1, b1):
    n_rows = h.shape[0]

    def body(h_ref, mp_ref, w1_ref, b1_ref, glu_ref, a_ref):
        u = _mod(h_ref[...], mp_ref[0:1], mp_ref[1:2], mp_ref[2:3]).astype(MXU)
        a = jnp.dot(u, w1_ref[...], preferred_element_type=F32) + b1_ref[...]
        a_ref[...] = a.astype(ACT)
        glu_ref[...] = a[:, :D] * jax.nn.sigmoid(a[:, D:])

    return _call(body, "cf1_fwd", (n_rows // T,),
                 [_rows(T, D), _full(8, D), _wfull(D, 2 * D), _full(1, 2 * D)],
                 [_rows(T, D), _rows(T, 2 * D)],
                 [_sds((n_rows, D)), _sds((n_rows, 2 * D), ACT)])(h, mp, w1, b1)


def _cf2_fwd(h, glu, mp, wdw, bdw, lng, lnb, w2, b2):
    n_rows = h.shape[0]
    nt = n_rows // T
    hb = 16

    def body(h_ref, gp_ref, gc_ref, gn_ref, mp_ref, wdw_ref, bdw_ref, lng_ref, lnb_ref, w2_ref, b2_ref,
             hn_ref, cv_ref, sb_ref, y_ref, ext):
        i = pl.program_id(0)
        _fill_ext(ext, gp_ref, gc_ref, gn_ref, hb, i == 0, i == nt - 1)
        for c in range(D // LANE):
            lanes = slice(c * LANE, (c + 1) * LANE)
            cv_ref[:, lanes] = _conv_wide(ext, wdw_ref, CK, hb, lanes) + bdw_ref[:, lanes]
        s = _silu(_ln(cv_ref[...], lng_ref[...], lnb_ref[...])).astype(MXU)
        sb_ref[...] = s.astype(ACT)
        y = jnp.dot(s, w2_ref[...], preferred_element_type=F32) + b2_ref[...]
        y_ref[...] = y
        hn_ref[...] = h_ref[...] + mp_ref[3:4] * y

    gp, gn = _halo(D, hb, n_rows)
    return _call(body, "cf2_fwd", (nt,),
                 [_rows(T, D), gp, _rows(T, D), gn, _full(8, D), _full(32, D), _full(1, D), _full(1, D), _full(1, D),
                  _wfull(D, D), _full(1, D)],
                 [_rows(T, D), _rows(T, D), _rows(T, D), _rows(T, D)],
                 [_sds((n_rows, D)), _sds((n_rows, D)), _sds((n_rows, D), ACT), _sds((n_rows, D))],
                 scratch=[pltpu.VMEM((T + 2 * hb, D), F32)])(h, glu, glu, glu, mp, wdw, bdw, lng, lnb, w2, b2)


def _cf2_bwd(dh, y, cv, mp, lng, lnb, w2):
    n_rows = dh.shape[0]

    def body(dh_ref, y_ref, cv_ref, mp_ref, lng_ref, lnb_ref, w2_ref, dcv_ref, dyb_ref, pg_ref):
        dhp = dh_ref[...]
        dy = mp_ref[3:4] * dhp
        dyb = dy.astype(MXU)
        dyb_ref[...] = dyb.astype(ACT)
        ds = _nt(dyb, w2_ref[...])
        _, vjp = jax.vjp(lambda cv_, g_, b_: _silu(_ln(cv_, g_, b_)), cv_ref[...], lng_ref[...], lnb_ref[...])
        dcv, dlng, dlnb = vjp(ds)
        dcv_ref[...] = dcv

        @pl.when(pl.program_id(0) == 0)
        def _():
            pg_ref[...] = jnp.zeros_like(pg_ref)
        pg_ref[0:1] += _sum0(dhp * y_ref[...])
        pg_ref[1:2] += _sum0(dy)
        pg_ref[2:3] += dlng
        pg_ref[3:4] += dlnb
        pg_ref[4:5] += _sum0(dcv)

    return _call(body, "cf2_bwd", (n_rows // T,),
                 [_rows(T, D), _rows(T, D), _rows(T, D), _full(8, D), _full(1, D), _full(1, D), _wfull(D, D)],
                 [_rows(T, D), _rows(T, D), _full(8, D)],
                 [_sds((n_rows, D)), _sds((n_rows, D), ACT), _sds((8, D))])(dh, y, cv, mp, lng, lnb, w2)


def _cf1_bwd(dh, h, a, dcv, glu, mp, wdw, w1):
    n_rows = h.shape[0]
    nt = n_rows // T
    hb = 16

    def body(dh_ref, h_ref, a_ref, dp_ref, dc_ref, dn_ref, gp_ref, gc_ref, gn_ref, mp_ref, wdw_ref, w1_ref,
             dho_ref, da_ref, ub_ref, pg_ref, pb_ref, dw_ref, dext, gext, dglu):
        i = pl.program_id(0)

        @pl.when(i == 0)
        def _():
            pg_ref[...] = jnp.zeros_like(pg_ref)
            pb_ref[...] = jnp.zeros_like(pb_ref)
            dw_ref[...] = jnp.zeros_like(dw_ref)
        _fill_ext(dext, dp_ref, dc_ref, dn_ref, hb, i == 0, i == nt - 1)
        _fill_ext(gext, gp_ref, gc_ref, gn_ref, hb, i == 0, i == nt - 1)
        for c in range(D // LANE):
            lanes = slice(c * LANE, (c + 1) * LANE)
            dglu[:, lanes] = _conv_wide(dext, wdw_ref, CK, hb, lanes, flip=True)
            _conv_dw_wide(dw_ref, dc_ref, gext, CK, hb, lanes)
        av = a_ref[...].astype(F32)
        _, vjp_glu = jax.vjp(lambda a1, a2: a1 * jax.nn.sigmoid(a2), av[:, :D], av[:, D:])
        da1, da2 = vjp_glu(dglu[...])
        da_ref[:, :D] = da1.astype(ACT)
        da_ref[:, D:] = da2.astype(ACT)
        pb_ref[0:1, :D] += _sum0(da1)
        pb_ref[0:1, D:] += _sum0(da2)
        du = _nt(da1, w1_ref[:, :D]) + _nt(da2, w1_ref[:, D:])
        u, vjp = jax.vjp(_mod, h_ref[...], mp_ref[0:1], mp_ref[1:2], mp_ref[2:3])
        ub_ref[...] = u.astype(ACT)
        dhn, dg, dsh, dsc = vjp(du)
        dho_ref[...] = dh_ref[...] + dhn
        pg_ref[0:1] += dg
        pg_ref[1:2] += dsh
        pg_ref[2:3] += dsc

    hp, hn = _halo(D, hb, n_rows)
    return _call(body, "cf1_bwd", (nt,),
                 [_rows(T, D), _rows(T, D), _rows(T, 2 * D), hp, _rows(T, D), hn, hp, _rows(T, D), hn,
                  _full(8, D), _full(32, D), _wfull(D, 2 * D)],
                 [_rows(T, D), _rows(T, 2 * D), _rows(T, D), _full(8, D), _full(8, 2 * D), _full(32, D)],
                 [_sds((n_rows, D)), _sds((n_rows, 2 * D), ACT), _sds((n_rows, D), ACT), _sds((8, D)),
                  _sds((8, 2 * D)), _sds((32, D))],
                 scratch=[pltpu.VMEM((T + 2 * hb, D), F32), pltpu.VMEM((T + 2 * hb, D), F32), pltpu.VMEM((T, D), F32)],
                 )(dh, h, a, dcv, dcv, dcv, glu, glu, glu, mp, wdw, w1)


def _sg_blocks():
    return [(c, g, slice(c * Q, (c + 1) * Q), slice(g * LANE, (g + 1) * LANE)) for c in range(T // Q) for g in range(SGG)]


IN_W = D + XBC + 32 + 2 * D
IN_LOC = IN_W // 4


def _win_split(shards):
    o1, o2, o3 = D, D + XBC, D + XBC + 32
    tr = 256

    def cols(s_ref, lo, hi):
        parts = []
        for j in range(4):
            a, b = max(lo, j * IN_LOC), min(hi, (j + 1) * IN_LOC)
            if a < b:
                parts.append(s_ref[j][:, a - j * IN_LOC:b - j * IN_LOC])
        return parts[0] if len(parts) == 1 else jnp.concatenate(parts, axis=1)

    def body(s_ref, wz_ref, wxbc_ref, wdt_ref, wuv_ref):
        wz_ref[...] = cols(s_ref, 0, o1)
        wxbc_ref[...] = cols(s_ref, o1, o2)
        dt = cols(s_ref, o2, o3)
        wdt_ref[...] = jnp.concatenate([dt, jnp.zeros((tr, LANE - 32), dt.dtype)], axis=1)
        wuv_ref[...] = cols(s_ref, o3, IN_W)

    dt_ = shards.dtype
    return _call(body, "win_split", (D // tr,), [pl.BlockSpec((4, tr, IN_LOC), lambda i: (0, i, 0))],
                 [_rows(tr, D), _rows(tr, XBC), _rows(tr, LANE), _rows(tr, 2 * D)],
                 [_sds((D, D), dt_), _sds((D, XBC), dt_), _sds((D, LANE), dt_), _sds((D, 2 * D), dt_)])(shards)


def _win_join(gz, gxbc, gdt, guv):
    tr = 256
    bounds = (0, D, D + XBC, D + XBC + 32, IN_W)

    def body(gz_ref, gx_ref, gd_ref, gu_ref, o_ref):
        segs = (gz_ref, gx_ref, gd_ref, gu_ref)
        for j in range(4):
            parts = []
            for k in range(4):
                a, b = max(bounds[k], j * IN_LOC), min(bounds[k + 1], (j + 1) * IN_LOC)
                if a < b:
                    parts.append(segs[k][:, a - bounds[k]:b - bounds[k]])
            full = parts[0] if len(parts) == 1 else jnp.concatenate(parts, axis=1)
            o_ref[j] = full.astype(jnp.bfloat16)

    return _call(body, "win_join", (D // tr,), [_rows(tr, D), _rows(tr, XBC), _rows(tr, LANE), _rows(tr, 2 * D)],
                 pl.BlockSpec((4, tr, IN_LOC), lambda i: (0, i, 0)), _sds((4, D, IN_LOC), jnp.bfloat16))(gz, gxbc, gdt, guv)


def _ctx_spec(nct):
    return pl.BlockSpec((T, D), lambda i: (jnp.minimum(i, nct - 1), 0))


def _hy1_fwd(ctx, x, mp2, wz, wuv, wxbc, wdt, lng, lnb, sgw, sgbt, nct):
    n_lat = x.shape[0]
    n_rows = ctx.shape[0] + n_lat

    def body(c_ref, x_ref, mp_ref, wz_ref, wuv_ref, wxbc_ref, wdt_ref, lng_ref, lnb_ref, sgw_ref, sgbt_ref,
             z_ref, uv_ref, xbcp_ref, dtr_ref, ysg_ref):
        hv = jnp.where(pl.program_id(0) < nct, c_ref[...], x_ref[...])
        u = _mod(hv, mp_ref[0:1], mp_ref[1:2], mp_ref[2:3]).astype(MXU)
        z_ref[...] = jnp.dot(u, wz_ref[...], preferred_element_type=F32)
        xbcp_ref[...] = jnp.dot(u, wxbc_ref[...], preferred_element_type=F32)
        dtr_ref[...] = jnp.dot(u, wdt_ref[...], preferred_element_type=F32)
        uv = jnp.dot(u, wuv_ref[...], preferred_element_type=F32)
        uv_ref[...] = uv
        gate = _gelu(uv[:, :D])
        vln = _ln(_gelu(uv[:, D:]), lng_ref[...], lnb_ref[...]).astype(MXU)
        for _, g, rs, ls in _sg_blocks():
            s = jnp.dot(sgw_ref[g], vln[rs, ls], preferred_element_type=F32) + sgbt_ref[:, g:g + 1]
            ysg_ref[rs, ls] = (gate[rs, ls] * s).astype(ACT)

    mspec = pl.BlockSpec((None, 8, D), lambda i: (jnp.where(i < nct, 0, 1), 0, 0))
    return _call(body, "hy1_fwd", (n_rows // T,),
                 [_ctx_spec(nct), _rows_lat(T, D, nct), mspec, _wfull(D, D), _wfull(D, 2 * D), _wfull(D, XBC), _wfull(D, LANE),
                  _full(1, D), _full(1, D), _full(SGG, Q, Q), _full(Q, LANE)],
                 [_rows(T, D), _rows(T, 2 * D), _rows(T, XBC), _rows(T, LANE), _rows_lat(T, D, nct)],
                 [_sds((n_rows, D)), _sds((n_rows, 2 * D)), _sds((n_rows, XBC)), _sds((n_rows, LANE)),
                  _sds((n_lat, D), ACT)])(ctx, x, mp2, wz, wuv, wxbc, wdt, lng, lnb, sgw, sgbt)


def _hy1_bwd(ctx, x, uv, dz, dxbcp, ddf, ddb, dysg, dres, mp2, wz, wuv, wxbc, wdt, lng, lnb, sgw, sgbt, nct):
    n_lat = dres.shape[0]
    n_rows = ctx.shape[0] + n_lat

    def body(c_ref, x_ref, uv_ref, dz_ref, dxbcp_ref, ddf_ref, ddb_ref, dysg_ref, dres_ref, mp_ref, wz_ref, wuv_ref,
             wxbc_ref, wdt_ref, lng_ref, lnb_ref, sgw_ref, sgbt_ref,
             dho_ref, ub_ref, duv_ref, ddt_ref, pg2_ref, pl_ref, dsgw_ref, dsgb_ref, dgate_s, dvln_s):
        i = pl.program_id(0)

        @pl.when(i == 0)
        def _():
            pg2_ref[...] = jnp.zeros_like(pg2_ref)
            pl_ref[...] = jnp.zeros_like(pl_ref)
            dsgw_ref[...] = jnp.zeros_like(dsgw_ref)
            dsgb_ref[...] = jnp.zeros_like(dsgb_ref)
        uv = uv_ref[...]

        def f_sg(ug, uvv, g_, b_):
            return _gelu(ug), _ln(_gelu(uvv), g_, b_)
        (gate, vln), vjp_sg = jax.vjp(f_sg, uv[:, :D], uv[:, D:], lng_ref[...], lnb_ref[...])
        vlnb = vln.astype(MXU)
        lane = lax.broadcasted_iota(jnp.int32, (Q, LANE), 1)
        dsgb = jnp.zeros((Q, LANE), F32)
        for _, g, rs, ls in _sg_blocks():
            s = jnp.dot(sgw_ref[g], vlnb[rs, ls], preferred_element_type=F32) + sgbt_ref[:, g:g + 1]
            dyb = dysg_ref[rs, ls]
            dgate_s[rs, ls] = dyb * s
            ds = dyb * gate[rs, ls]
            dvln_s[rs, ls] = _tn_dot(sgw_ref[g], ds)
            dsgw_ref[g] += _nt(ds, vlnb[rs, ls])
            dsgb = dsgb + jnp.where(lane == g, jnp.sum(ds, axis=1, keepdims=True), 0.0)
        dsgb_ref[...] += dsgb
        dug, duvv, dlng, dlnb = vjp_sg((dgate_s[...], dvln_s[...]))
        pl_ref[0:1] += dlng
        pl_ref[1:2] += dlnb
        duv_ref[:, :D] = dug.astype(ACT)
        duv_ref[:, D:] = duvv.astype(ACT)
        ddt = (ddf_ref[...] + ddb_ref[...]).astype(MXU)
        ddt_ref[...] = ddt.astype(ACT)
        du = (_nt(dz_ref[...], wz_ref[...]) + _nt(dug, wuv_ref[:, :D]) + _nt(duvv, wuv_ref[:, D:])
              + _nt(dxbcp_ref[...], wxbc_ref[...]) + _nt(ddt, wdt_ref[...]))
        hv = jnp.where(i < nct, c_ref[...], x_ref[...])
        u, vjp = jax.vjp(_mod, hv, mp_ref[0:1], mp_ref[1:2], mp_ref[2:3])
        ub_ref[...] = u.astype(ACT)
        dhn, dg, dsh, dsc = vjp(du)
        dho_ref[...] = dres_ref[...] + dhn
        is_ctx = i < nct
        for k, val in enumerate((dg, dsh, dsc)):
            pg2_ref[0, k:k + 1] += jnp.where(is_ctx, val, 0.0)
            pg2_ref[1, k:k + 1] += jnp.where(is_ctx, 0.0, val)

    mspec = pl.BlockSpec((None, 8, D), lambda i: (jnp.where(i < nct, 0, 1), 0, 0))
    return _call(body, "hy1_bwd", (n_rows // T,),
                 [_ctx_spec(nct), _rows_lat(T, D, nct), _rows(T, 2 * D), _rows(T, D), _rows(T, XBC), _rows(T, LANE),
                  _rows(T, LANE), _rows(T, D),
                  _rows_lat(T, D, nct), mspec, _wfull(D, D), _wfull(D, 2 * D), _wfull(D, XBC), _wfull(D, LANE),
                  _full(1, D), _full(1, D), _full(SGG, Q, Q), _full(Q, LANE)],
                 [_rows_lat(T, D, nct), _rows(T, D), _rows(T, 2 * D), _rows(T, LANE), _full(2, 8, D), _full(8, D),
                  _full(SGG, Q, Q), _full(Q, LANE)],
                 [_sds((n_lat, D)), _sds((n_rows, D), ACT), _sds((n_rows, 2 * D), ACT), _sds((n_rows, LANE), ACT),
                  _sds((2, 8, D)), _sds((8, D)), _sds((SGG, Q, Q)), _sds((Q, LANE))],
                 scratch=[pltpu.VMEM((T, D), F32), pltpu.VMEM((T, D), F32)],
                 )(ctx, x, uv, dz, dxbcp, ddf, ddb, dysg, dres, mp2, wz, wuv, wxbc, wdt, lng, lnb, sgw, sgbt)


def _seq_edges(i, nct, nt):
    return (i == 0) | (i == nct), (i == nct - 1) | (i == nt - 1)


def _cv5_fwd(xbcp, w, b, nct):
    n_rows = xbcp.shape[0]
    nt = n_rows // T
    hb = 8

    def body(p_ref, c_ref, n_ref, w_ref, b_ref, o_ref, ext):
        first, last = _seq_edges(pl.program_id(0), nct, nt)
        _fill_ext(ext, p_ref, c_ref, n_ref, hb, first, last)
        for c in range(XBC // LANE):
            lanes = slice(c * LANE, (c + 1) * LANE)
            o_ref[:, lanes] = _silu(_conv(ext, w_ref, SK, hb, lanes) + b_ref[:, lanes])

    hp, hn = _halo(XBC, hb, n_rows)
    return _call(body, "cv5_fwd", (nt,), [hp, _rows(T, XBC), hn, _full(8, XBC), _full(1, XBC)],
                 _rows(T, XBC), _sds((n_rows, XBC)), scratch=[pltpu.VMEM((T + 2 * hb, XBC), F32)])(xbcp, xbcp, xbcp, w, b)


def _cv5_bwd1(xbcp, dxf, dxb, w, b, nct):
    n_rows = xbcp.shape[0]
    nt = n_rows // T
    hb = 8

    def body(p_ref, c_ref, n_ref, dxf_ref, dxb_ref, w_ref, b_ref, o_ref, pg_ref, ext):
        i = pl.program_id(0)
        first, last = _seq_edges(i, nct, nt)
        _fill_ext(ext, p_ref, c_ref, n_ref, hb, first, last)

        @pl.when(i == 0)
        def _():
            pg_ref[...] = jnp.zeros_like(pg_ref)
        for c in range(XBC // LANE):
            lanes = slice(c * LANE, (c + 1) * LANE)
            cv = _conv(ext, w_ref, SK, hb, lanes) + b_ref[:, lanes]
            sg = jax.nn.sigmoid(cv)
            dcv = (dxf_ref[:, lanes] + dxb_ref[:, lanes]) * (sg * (1.0 + cv * (1.0 - sg)))
            o_ref[:, lanes] = dcv
            pg_ref[0:1, lanes] += _sum0(dcv)

    hp, hn = _halo(XBC, hb, n_rows)
    return _call(body, "cv5_bwd1", (nt,),
                 [hp, _rows(T, XBC), hn, _rows(T, XBC), _rows(T, XBC), _full(8, XBC), _full(1, XBC)],
                 [_rows(T, XBC), _full(8, XBC)], [_sds((n_rows, XBC)), _sds((8, XBC))],
                 scratch=[pltpu.VMEM((T + 2 * hb, XBC), F32)])(xbcp, xbcp, xbcp, dxf, dxb, w, b)


def _cv5_bwd2(dcv, xbcp, w, nct):
    n_rows = xbcp.shape[0]
    nt = n_rows // T
    hb = 8

    def body(dp_ref, dc_ref, dn_ref, xp_ref, xc_ref, xn_ref, w_ref, o_ref, dw_ref, dext, xext):
        i = pl.program_id(0)
        first, last = _seq_edges(i, nct, nt)
        _fill_ext(dext, dp_ref, dc_ref, dn_ref, hb, first, last)
        _fill_ext(xext, xp_ref, xc_ref, xn_ref, hb, first, last)

        @pl.when(i == 0)
        def _():
            dw_ref[...] = jnp.zeros_like(dw_ref)
        for c in range(XBC // LANE):
            lanes = slice(c * LANE, (c + 1) * LANE)
            o_ref[:, lanes] = _conv_tr(dext, w_ref, SK, hb, lanes).astype(ACT)
            _conv_dw(dw_ref, dc_ref, xext, SK, hb, lanes)

    hp, hn = _halo(XBC, hb, n_rows)
    return _call(body, "cv5_bwd2", (nt,),
                 [hp, _rows(T, XBC), hn, hp, _rows(T, XBC), hn, _full(8, XBC)],
                 [_rows(T, XBC), _full(8, XBC)], [_sds((n_rows, XBC), ACT), _sds((8, XBC))],
                 scratch=[pltpu.VMEM((T + 2 * hb, XBC), F32), pltpu.VMEM((T + 2 * hb, XBC), F32)],
                 )(dcv, dcv, dcv, xbcp, xbcp, xbcp, w)


def _scan_order(nc, ncc, rev):
    if not rev:
        return lambda s: s
    return lambda s: jnp.where(s < ncc, ncc - 1 - s, nc - 1 - (s - ncc))


def _ssd_prep(dtr, sp, rev):
    dt = jax.nn.softplus(dtr + sp[0:1])
    a_neg = -jnp.exp(sp[1:2])
    r = lax.broadcasted_iota(jnp.int32, (Q, Q), 0)
    c = lax.broadcasted_iota(jnp.int32, (Q, Q), 1)
    msk = (c >= r) if rev else (c <= r)
    tri = msk.astype(F32)
    acs = jnp.dot(tri, dt * a_neg, precision=HI, preferred_element_type=F32)
    last = 0 if rev else Q - 1
    return dt, a_neg, acs, msk, tri, last


def _pair_sel(arr, lo, m, lane_lt):
    h0 = lo + 2 * m
    return jnp.where(lane_lt, arr[:, h0:h0 + 1], arr[:, h0 + 1:h0 + 2])


def _head_lanes(row, lo, g):
    lane = lax.broadcasted_iota(jnp.int32, (1, 512), 1)
    out = jnp.zeros((1, 512), F32)
    for k in range(8):
        h = lo + 8 * g + k
        out = jnp.where((lane >= 64 * k) & (lane < 64 * (k + 1)), row[:, h:h + 1], out)
    return out


def _halves(v, lane_lt):
    return jnp.concatenate([jnp.where(lane_lt, v, 0.0), jnp.where(lane_lt, 0.0, v)], axis=0)


def _ssd_fwd(xbc, dtr, sp, ncc, rev):
    n_rows = xbc.shape[0]
    nc = n_rows // Q
    lo = 16 if rev else 0
    order = _scan_order(nc, ncc, rev)

    def body(x_ref, dtr_ref, sp_ref, y_ref, hin_ref, st):
        @pl.when(pl.program_id(0) == 0)
        def _():
            st[...] = jnp.zeros_like(st)
        dt, _, acs, msk, _, last = _ssd_prep(dtr_ref[...], sp_ref[...], rev)
        acs_t, dt_t = acs.T, dt.T
        eacs = jnp.exp(acs)
        eal = jnp.exp(acs[last:last + 1, :])
        tew = jnp.exp(acs[last:last + 1, :] - acs) * dt
        lane_lt = lax.broadcasted_iota(jnp.int32, (Q, LANE), 1) < 64
        for g in range(2):
            gl = slice(g * 512, (g + 1) * 512)
            bg = x_ref[:, 1024 + g * 128:1152 + g * 128]
            cg = x_ref[:, 1280 + g * 128:1408 + g * 128]
            s_g = _nt(cg, bg)
            h_t = st[:, gl]
            hin_ref[:, gl] = h_t
            yoff = _nn(cg, h_t)
            xw = []
            for mm in range(4):
                m = 4 * g + mm
                ls = slice(m * LANE, (m + 1) * LANE)
                x2 = x_ref[:, ls]
                ws = []
                for hh in range(2):
                    h = lo + 2 * m + hh
                    lm = jnp.exp(jnp.where(msk, acs[:, h:h + 1] - acs_t[h:h + 1, :], -jnp.inf))
                    ws.append(s_g * lm * dt_t[h:h + 1, :])
                y2 = _nn(jnp.concatenate(ws, axis=1), _halves(x2, lane_lt))
                y_ref[:, ls] = y2 + yoff[:, mm * LANE:(mm + 1) * LANE] * _pair_sel(eacs, lo, m, lane_lt)
                xw.append(x2 * _pair_sel(tew, lo, m, lane_lt))
            st[:, gl] = _head_lanes(eal, lo, g) * h_t + _tn_dot(bg, jnp.concatenate(xw, axis=1))

    return _call(body, "ssd_fwd_r" if rev else "ssd_fwd_f", (nc,),
                 [pl.BlockSpec((Q, XBC), lambda s: (order(s), 0)), pl.BlockSpec((Q, LANE), lambda s: (order(s), 0)),
                  _full(8, LANE)],
                 [pl.BlockSpec((Q, D), lambda s: (order(s), 0)), pl.BlockSpec((None, LANE, D), lambda s: (order(s), 0, 0))],
                 [_sds((n_rows, D)), _sds((nc, LANE, D))], scratch=[pltpu.VMEM((LANE, D), F32)])(xbc, dtr, sp)


def _ssd_bwd(xbc, dtr, dy, hin, sp, dl, eh, ncc, rev):
    n_rows = xbc.shape[0]
    nc = n_rows // Q
    lo = 16 if rev else 0
    fwd_order = _scan_order(nc, ncc, rev)
    order = lambda s: fwd_order(nc - 1 - s)
    with_skip = not rev

    def body(x_ref, dtr_ref, dy_ref, hin_ref, sp_ref, dl_ref, eh_ref, dx_ref, ddtr_ref, pg_ref, dst):
        @pl.when(pl.program_id(0) == 0)
        def _():
            dst[...] = jnp.zeros_like(dst)
            pg_ref[...] = jnp.zeros_like(pg_ref)
        dtr_v = dtr_ref[...]
        dt, a_neg, acs, msk, tri, last = _ssd_prep(dtr_v, sp_ref[...], rev)
        acs_t = acs.T
        r = lax.broadcasted_iota(jnp.int32, (Q, Q), 0)
        c = lax.broadcasted_iota(jnp.int32, (Q, Q), 1)
        msk_t = (c <= r) if rev else (c >= r)
        eacs = jnp.exp(acs)
        eal = jnp.exp(acs[last:last + 1, :])
        te = jnp.exp(acs[last:last + 1, :] - acs)
        lane = lax.broadcasted_iota(jnp.int32, (Q, LANE), 1)
        lane1 = lax.broadcasted_iota(jnp.int32, (1, LANE), 1)
        lane_lt = lane < 64
        dacs = jnp.zeros((Q, LANE), F32)
        ddt_x = jnp.zeros((Q, LANE), F32)
        dlast = jnp.zeros((1, LANE), F32)
        hs_rows = []
        sub16 = lax.broadcasted_iota(jnp.int32, (16, Q), 0)
        dacs_t = jnp.zeros((16, Q), F32)
        for g in range(2):
            gl = slice(g * 512, (g + 1) * 512)
            bg = x_ref[:, 1024 + g * 128:1152 + g * 128]
            cg = x_ref[:, 1280 + g * 128:1408 + g * 128]
            s_g = _nt(cg, bg)
            s_gt = _nt(bg, cg)
            h_t, dh_t = hin_ref[:, gl], dst[:, gl]
            bh = _nn(bg, dh_t)
            yoff = _nn(cg, h_t)
            d_s = jnp.zeros((Q, Q), F32)
            edy, exd = [], []
            for mm in range(4):
                m = 4 * g + mm
                ls = slice(m * LANE, (m + 1) * LANE)
                x2, dy2 = x_ref[:, ls], dy_ref[:, ls]
                bh2 = bh[:, mm * LANE:(mm + 1) * LANE]
                dtm, em, eam = (_pair_sel(v, lo, m, lane_lt) for v in (dt, te, eacs))
                xd2 = x2 * dtm
                lms, mts = [], []
                for hh in range(2):
                    h = lo + 2 * m + hh
                    col, row = acs[:, h:h + 1], acs_t[h:h + 1, :]
                    lms.append(jnp.exp(jnp.where(msk, col - row, -jnp.inf)))
                    mts.append(s_gt * jnp.exp(jnp.where(msk_t, row - col, -jnp.inf)))
                dy_st = _halves(dy2, lane_lt)
                dxd2 = em * bh2 + _nn(jnp.concatenate(mts, axis=1), dy_st)
                dm_st = _nt(dy_st, xd2)
                dmt_st = _nt(_halves(xd2, lane_lt), dy2)
                d_s = d_s + dm_st[:Q] * lms[0] + dm_st[Q:] * lms[1]
                v1, v2, v3 = dy2 * yoff[:, mm * LANE:(mm + 1) * LANE] * eam, dxd2 * x2, xd2 * bh2 * em
                for hh in range(2):
                    h = lo + 2 * m + hh
                    half = lane_lt == (hh == 0)
                    g_rows = _sum0(dmt_st[hh * Q:(hh + 1) * Q] * mts[hh]) - _sum0(dm_st[hh * Q:(hh + 1) * Q] * s_g * lms[hh])
                    dacs_t = jnp.where(sub16 == 2 * m + hh, g_rows, dacs_t)
                    r1 = jnp.sum(jnp.where(half, v1, 0.0), axis=1, keepdims=True)
                    r2 = jnp.sum(jnp.where(half, v2, 0.0), axis=1, keepdims=True)
                    r3 = jnp.sum(jnp.where(half, v3, 0.0), axis=1, keepdims=True)
                    dacs = dacs + jnp.where(lane == h, r1 - r3, 0.0)
                    ddt_x = ddt_x + jnp.where(lane == h, r2, 0.0)
                    dlast = dlast + jnp.where(lane1 == h, _sum0(r3), 0.0)
                dx2 = dxd2 * dtm
                if with_skip:
                    dx2 = dx2 + dl_ref[:, ls] * dy2
                dx_ref[:, ls] = dx2
                edy.append(eam * dy2)
                exd.append(em * xd2)
            edy, exd = jnp.concatenate(edy, axis=1), jnp.concatenate(exd, axis=1)
            hs_rows.append(_sum0(h_t * dh_t))
            dst[:, gl] = _head_lanes(eal, lo, g) * dh_t + _tn_dot(cg, edy)
            dx_ref[:, 1024 + g * 128:1152 + g * 128] = _tn_dot(d_s, cg) + _nt(exd, dh_t)
            dx_ref[:, 1280 + g * 128:1408 + g * 128] = _nn(d_s, bg) + _nt(edy, h_t)
        hs = jnp.broadcast_to(jnp.concatenate(hs_rows, axis=1), (8, D))
        hsum = jnp.dot(hs, eh_ref[...], precision=HI, preferred_element_type=F32)[0:1]
        dlast = dlast + eal * hsum
        dacs = dacs + jnp.concatenate([jnp.zeros((lo, Q), F32)] * (lo > 0) + [dacs_t, jnp.zeros((LANE - 16 - lo, Q), F32)],
                                      axis=0).T
        rowi = lax.broadcasted_iota(jnp.int32, (Q, LANE), 0)
        dacs = dacs + jnp.where(rowi == last, dlast, 0.0)
        da = lax.dot_general(tri, dacs, (((0,), (0,)), ((), ())), precision=HI, preferred_element_type=F32)
        ddt = ddt_x + da * a_neg
        mine = (lane >= lo) & (lane < lo + 16)
        ddtr = jnp.where(mine, ddt * jax.nn.sigmoid(dtr_v + sp_ref[0:1]), 0.0)
        ddtr_ref[...] = ddtr
        pg_ref[0:1] += _sum0(ddtr)
        pg_ref[1:2] += jnp.where(mine[0:1], _sum0(da * dt) * a_neg, 0.0)

    blk = lambda w_: pl.BlockSpec((Q, w_), lambda s: (order(s), 0))
    return _call(body, "ssd_bwd_r" if rev else "ssd_bwd_f", (nc,),
                 [blk(XBC), blk(LANE), blk(D), pl.BlockSpec((None, LANE, D), lambda s: (order(s), 0, 0)),
                  _full(8, LANE), _full(1, D), _full(D, LANE)],
                 [blk(XBC), blk(LANE), _full(8, LANE)],
                 [_sds((n_rows, XBC)), _sds((n_rows, LANE)), _sds((8, LANE))],
                 scratch=[pltpu.VMEM((LANE, D), F32)])(xbc, dtr, dy, hin, sp, dl, eh)


def _hy4_fwd(h, yf, yb, xbc, z, ysg, mp, dl, ng, wout, nct):
    n_rows = h.shape[0]

    def body(h_ref, yf_ref, yb_ref, xs_ref, z_ref, ysg_ref, mp_ref, dl_ref, ng_ref, wout_ref, hn_ref, yssd_ref, out_ref):
        ytot = yf_ref[...] + yb_ref[...] + dl_ref[...] * xs_ref[...]
        yssd = _gate_norm(ytot, z_ref[...], ng_ref[...]).astype(MXU)
        yssd_ref[...] = yssd.astype(ACT)
        out = (jnp.dot(yssd, wout_ref[0:D, :], preferred_element_type=F32)
               + jnp.dot(ysg_ref[...].astype(MXU), wout_ref[D:2 * D, :], preferred_element_type=F32))
        out_ref[...] = out
        hn_ref[...] = h_ref[...] + mp_ref[3:4] * out

    return _call(body, "hy4_fwd", (n_rows // T,),
                 [_rows(T, D), _rows(T, D, nct), _rows(T, D, nct), _rows(T, D, nct), _rows(T, D, nct), _rows(T, D),
                  _full(8, D), _full(1, D), _full(1, D), _wfull(2 * D, D)],
                 [_rows(T, D), _rows(T, D), _rows(T, D)],
                 [_sds((n_rows, D)), _sds((n_rows, D), ACT), _sds((n_rows, D))])(h, yf, yb, xbc, z, ysg, mp, dl, ng, wout)


def _hy4_bwd(dh, out, yf, yb, xbc, z, mp, dl, ng, wout, nct):
    n_lat = dh.shape[0]
    n_rows = yf.shape[0]

    def body(dh_ref, out_ref, yf_ref, yb_ref, xs_ref, z_ref, mp_ref, dl_ref, ng_ref, wout_ref,
             dy_ref, dz_ref, dysg_ref, doutb_ref, pg_ref):
        i = pl.program_id(0)

        @pl.when(i == 0)
        def _():
            pg_ref[...] = jnp.zeros_like(pg_ref)

        @pl.when(i < nct)
        def _():
            dy_ref[...] = jnp.zeros_like(dy_ref)
            dz_ref[...] = jnp.zeros_like(dz_ref)
            dysg_ref[...] = jnp.zeros_like(dysg_ref)
            doutb_ref[...] = jnp.zeros_like(doutb_ref)

        @pl.when(i >= nct)
        def _():
            dhp = dh_ref[...]
            doutb = (mp_ref[3:4] * dhp).astype(MXU)
            doutb_ref[...] = doutb.astype(ACT)
            dysg_ref[...] = _nt(doutb, wout_ref[D:2 * D, :])
            dyssd = _nt(doutb, wout_ref[0:D, :])
            xs = xs_ref[...]
            ytot = yf_ref[...] + yb_ref[...] + dl_ref[...] * xs
            _, vjp = jax.vjp(_gate_norm, ytot, z_ref[...], ng_ref[...])
            dytot, dz, dng = vjp(dyssd)
            dy_ref[...] = dytot
            dz_ref[...] = dz.astype(ACT)
            pg_ref[0:1] += _sum0(dhp * out_ref[...])
            pg_ref[1:2] += dng
            pg_ref[2:3] += _sum0(dytot * xs)

    return _call(body, "hy4_bwd", (n_rows // T,),
                 [_rows_lat(T, D, nct), _rows_lat(T, D, nct), _rows(T, D), _rows(T, D), _rows(T, D), _rows(T, D),
                  _full(8, D), _full(1, D), _full(1, D), _wfull(2 * D, D)],
                 [_rows(T, D), _rows(T, D), _rows(T, D), _rows_lat(T, D, nct), _full(8, D)],
                 [_sds((n_rows, D)), _sds((n_rows, D), ACT), _sds((n_rows, D)), _sds((n_lat, D), ACT), _sds((8, D))],
                 )(dh, out, yf, yb, xbc, z, mp, dl, ng, wout)


def _loss_bwd(h, tgt, fng):
    n_rows = h.shape[0]

    def body(h_ref, t_ref, g_ref, dh_ref, pg_ref, ls_ref):
        @pl.when(pl.program_id(0) == 0)
        def _():
            pg_ref[...] = jnp.zeros_like(pg_ref)
            ls_ref[...] = jnp.zeros_like(ls_ref)
        hv = h_ref[...]
        g = g_ref[...]
        r = lax.rsqrt(jnp.mean(hv * hv, axis=-1, keepdims=True) + EPS)
        n = hv * r
        e = n * g - t_ref[...]
        ls_ref[...] += 0.5 * jnp.sum(jnp.sum(e * e, axis=1, keepdims=True), axis=0, keepdims=True) * (1.0 / D)
        dyv = e * (1.0 / D)
        pg_ref[0:1] += _sum0(dyv * n)
        dn = dyv * g
        dh_ref[...] = r * (dn - n * jnp.mean(dn * n, axis=-1, keepdims=True))

    return _call(body, "loss_bwd", (n_rows // T,), [_rows(T, D), _rows(T, D), _full(1, D)],
                 [_rows(T, D), _full(8, D), _full(8, LANE)],
                 [_sds((n_rows, D)), _sds((8, D)), _sds((8, LANE))])(h, tgt, fng)


def _pad_rows(a, rows):
    return jnp.concatenate([a, jnp.zeros((rows - a.shape[0],) + a.shape[1:], a.dtype)], axis=0)


def _mp(*rows):
    return _pad_rows(jnp.stack(rows, axis=0), 8)


def _local_step(x, ctx, tgt, ada, cada0, w, late_w=None, early_grads=None, small_grads=None):
    n_lat, n_ctx = x.shape[0], ctx.shape[0]
    nct, ncc = n_ctx // T, n_ctx // Q
    a0 = [ada[0, k * D:(k + 1) * D] for k in range(6)]
    a1 = [ada[1, k * D:(k + 1) * D] for k in range(6)]
    c0 = [cada0[k * D:(k + 1) * D] for k in range(6)]
    g = {}

    mp2 = jnp.stack([_mp(w["norm_mix_g"][0], c0[0], c0[1]), _mp(w["norm_mix_g"][0], a0[0], a0[1], a0[2])], axis=0)
    mp_l0 = mp2[1]
    sgbt = _pad_cols(w["sg_b"][0].T, LANE)
    lng, lnb = w["sg_ln_g"][0][None], w["sg_ln_b"][0][None]
    z, uv, xbcp, dtr, ysg = _hy1_fwd(ctx, x, mp2, w["wz"], w["wuv"], w["wxbc"], w["wdt"], lng, lnb, w["sg_w"], sgbt, nct)
    cw = _pad_rows(w["ssd_conv_w"][0], 8)
    cb = w["ssd_conv_b"][0][None]
    xbc = _cv5_fwd(xbcp, cw, cb, nct)
    sp = _pad_rows(jnp.stack([_pad_cols(w["ssd_dt_bias"][0].reshape(1, 32), LANE)[0],
                              _pad_cols(w["ssd_a_log"][0].reshape(1, 32), LANE)[0]], axis=0), 8)
    dl = jnp.repeat(w["ssd_d"][0], 64)[None]
    ng = w["ssd_norm_g"][0][None]
    yf, hin_f = _ssd_fwd(xbc, dtr, sp, ncc, False)
    yb, hin_b = _ssd_fwd(xbc, dtr, sp, ncc, True)
    if late_w is not None:
        w = {**w, **late_w(yb)}
    h1, yssd, out0 = _hy4_fwd(x, yf, yb, xbc, z, ysg, mp_l0, dl, ng, w["hy_w_out"], nct)

    mpm0 = _mp(w["norm_mlp_g"][0], a0[3], a0[4], a0[5])
    h2, am0, ym0 = _mlp_fwd(h1, mpm0, w["wpack"], 0, "mlp0_fwd")

    mpc = _mp(w["norm_mix_g"][1], a1[0], a1[1], a1[2])
    wdw = _pad_rows(w["cf_w_dw"][0], 32)
    glu, acf = _cf1_fwd(h2, mpc, w["cf_w_pw1"], w["cf_b_pw1"])
    h3, cv, scf, ycf = _cf2_fwd(h2, glu, mpc, wdw, w["cf_b_dw"], w["cf_ln_g"], w["cf_ln_b"], w["cf_w_pw2"], w["cf_b_pw2"])

    mpm1 = _mp(w["norm_mlp_g"][1], a1[3], a1[4], a1[5])
    h4, am1, ym1 = _mlp_fwd(h3, mpm1, w["wpack"], 1, "mlp1_fwd")

    dh4, pg_f, ls = _loss_bwd(h4, tgt, w["final_norm_g"][None])
    loss = ls[0, 0]
    g["final_norm_g"] = pg_f[0]

    gp = {}
    dh3, da1, dy1, u1, pgm1 = _mlp_bwd(dh4, h3, am1, ym1, mpm1, w["wpack"], 1, "mlp1_bwd")
    gw1_1 = _tn(u1, da1, "tn_mlp1_w1", shard=("col", 1024))
    gw2_1 = _tn(am1, dy1, "tn_mlp1_w2", relu2=True, shard=("row", 1024))

    dcv, dycf, pgc2 = _cf2_bwd(dh3, ycf, cv, mpc, w["cf_ln_g"], w["cf_ln_b"], w["cf_w_pw2"])
    gp["cf_w_pw2"] = _tn(scf, dycf, "tn_cf_pw2", shard=("row", 256))
    dh2, dacf, ucf, pgc1, pbc1, dwdw = _cf1_bwd(dh3, h2, acf, dcv, glu, mpc, wdw, w["cf_w_pw1"])
    gp["cf_w_pw1"] = _tn(ucf, dacf, "tn_cf_pw1", shard=("col", 512)).reshape(4, 512, 1024)
    g["cf_b_pw2"], g["cf_ln_g"], g["cf_ln_b"], g["cf_b_dw"] = pgc2[1], pgc2[2], pgc2[3], pgc2[4]
    g["cf_b_pw1"] = pbc1[0]
    g["cf_w_dw"] = dwdw[:CK]

    dh1, da0, dy0, u0, pgm0 = _mlp_bwd(dh2, h1, am0, ym0, mpm0, w["wpack"], 0, "mlp0_bwd")
    gp["mlp_w1"] = jnp.concatenate([_tn(u0, da0, "tn_mlp0_w1", shard=("col", 1024)), gw1_1], axis=1)
    gp["mlp_w2"] = jnp.concatenate([_tn(am0, dy0, "tn_mlp0_w2", relu2=True, shard=("row", 1024)), gw2_1], axis=1)
    g["norm_mlp_g"] = jnp.stack([pgm0[0], pgm1[0]])

    dyt, dz, dysg, doutb, pg4 = _hy4_bwd(dh1, out0, yf, yb, xbc, z, mp_l0, dl, ng, w["hy_w_out"], nct)
    gp["hy_w_out"] = jnp.concatenate([_tn(yssd, doutb, "tn_out_ssd", shard=("row", 512)),
                                      _tn(ysg, doutb, "tn_out_sg", shard=("row", 512))], axis=0)
    if early_grads is not None:
        sp = sp + early_grads(gp)
    head_of_lane = jnp.arange(D, dtype=jnp.int32)[:, None] // 64
    col = jnp.arange(LANE, dtype=jnp.int32)[None, :]
    dxf, ddf, pgsf = _ssd_bwd(xbc, dtr, dyt, hin_f, sp, dl, (col == head_of_lane).astype(F32), ncc, False)
    dxb, ddb, pgsb = _ssd_bwd(xbc, dtr, dyt, hin_b, sp, dl, (col == head_of_lane + 16).astype(F32), ncc, True)
    dcv5, pgcb = _cv5_bwd1(xbcp, dxf, dxb, cw, cb, nct)
    dxbcp, dcw = _cv5_bwd2(dcv5, xbcp, cw, nct)
    dx, ucat, duv, ddt, pg2, pln, dsgw, dsgbt = _hy1_bwd(
        ctx, x, uv, dz, dxbcp, ddf, ddb, dysg, dh1, mp2, w["wz"], w["wuv"], w["wxbc"], w["wdt"], lng, lnb, w["sg_w"], sgbt, nct)
    g["ssd_conv_w"], g["ssd_conv_b"] = dcw[:SK], pgcb[0]
    pgs = pgsf + pgsb
    g["ssd_dt_bias"], g["ssd_a_log"] = pgs[0, :32].reshape(2, 16), pgs[1, :32].reshape(2, 16)
    g["ssd_d"] = jnp.sum(pg4[2].reshape(16, 64), axis=1)
    g["ssd_norm_g"] = pg4[1]
    g["sg_ln_g"], g["sg_ln_b"] = pln[0], pln[1]
    g["sg_w"], g["sg_b"] = dsgw, dsgbt[:, :SGG].T
    g["norm_mix_g"] = jnp.stack([pg2[0, 0] + pg2[1, 0], pgc1[0]])

    zero = jnp.zeros((D,), F32)
    d_ada = jnp.stack([jnp.concatenate([pg2[1, 1], pg2[1, 2], pg4[0], pgm0[1], pgm0[2], pgm0[3]]),
                       jnp.concatenate([pgc1[1], pgc1[2], pgc2[0], pgm1[1], pgm1[2], pgm1[3]])])
    d_cada0 = jnp.concatenate([pg2[0, 1], pg2[0, 2], zero, zero, zero, zero])
    if small_grads is not None:
        ucat, _ = lax.optimization_barrier((ucat, small_grads(g, d_ada, d_cada0)))
    gp["hy_w_in"] = _win_join(_tn(ucat, dz, "tn_in_z"), _tn(ucat, dxbcp, "tn_in_xbc"), _tn(ucat, ddt, "tn_in_dt"),
                              _tn(ucat, duv, "tn_in_uv"))
    g["pieces"] = gp
    return loss, dx, g, d_ada, d_cada0


def _pad_cols(a, cols):
    return jnp.concatenate([a, jnp.zeros(a.shape[:-1] + (cols - a.shape[-1],), a.dtype)], axis=-1)


MESH = pl.DeviceIdType.MESH
ANY = pl.BlockSpec(memory_space=pl.ANY)
IN_VMEM = pl.BlockSpec(memory_space=pltpu.VMEM)


def _coords():
    return lax.axis_index("x"), lax.axis_index("y"), lax.axis_index("c")


def _ag8(x, name):
    r, wd = x.shape

    def body(x_ref, o_ref, send, recv, lsem):
        mx, my, mc = _coords()
        me = 4 * mx + 2 * my + mc
        mine = pltpu.make_async_copy(x_ref, o_ref.at[me], lsem)
        mine.start()
        sent, peers = [], []
        for k in range(1, 8):
            px = 1 - mx if k & 4 else mx
            py = 1 - my if k & 2 else my
            pc = 1 - mc if k & 1 else mc
            cp = pltpu.make_async_remote_copy(src_ref=x_ref, dst_ref=o_ref.at[me], send_sem=send.at[k - 1],
                                              recv_sem=recv.at[k - 1], device_id=(px, py, pc), device_id_type=MESH)
            cp.start()
            sent.append(cp)
            peers.append((4 * px + 2 * py + pc, (px, py, pc)))
        for k in range(1, 8):
            slot, peer = peers[k - 1]
            pltpu.make_async_remote_copy(src_ref=x_ref, dst_ref=o_ref.at[slot], send_sem=send.at[k - 1],
                                         recv_sem=recv.at[k - 1], device_id=peer, device_id_type=MESH).wait_recv()
        for cp in sent:
            cp.wait_send()
        mine.wait()

    return pl.pallas_call(
        body, name=name, out_shape=_sds((8, r, wd), x.dtype), in_specs=[IN_VMEM], out_specs=IN_VMEM,
        scratch_shapes=[pltpu.SemaphoreType.DMA((7,)), pltpu.SemaphoreType.DMA((7,)), pltpu.SemaphoreType.DMA(())],
        compiler_params=pltpu.CompilerParams(vmem_limit_bytes=VMEM_LIMIT))(x)


HBM = pl.BlockSpec(memory_space=pltpu.HBM)
SEM = pl.BlockSpec(memory_space=pltpu.SEMAPHORE)
EFFECT = pltpu.SideEffectType.DATAFLOW_SIDE_EFFECTING


def _x4_peers(in_ref, land_ref, send, recv, a2a):
    mx, my, mc = _coords()
    me = 2 * mx + my
    out = []
    for k in range(1, 4):
        px = 1 - mx if k & 2 else mx
        py = 1 - my if k & 1 else my
        pj = 2 * px + py
        mk = functools.partial(pltpu.make_async_remote_copy, src_ref=in_ref.at[pj] if a2a else in_ref,
                               send_sem=send.at[k - 1], recv_sem=recv.at[k - 1], device_id=(px, py, mc), device_id_type=MESH)
        out.append((mk(dst_ref=land_ref.at[me]), mk(dst_ref=land_ref.at[pj])))
    return out


def _x4_start(buf, name, a2a):
    r, wd = buf.shape[-2:]

    def body(in_ref, land_ref, send, recv, in_thru, land_thru, token):
        for start, _ in _x4_peers(in_ref, land_ref, send, recv, a2a):
            start.start()
        token[...] = jnp.zeros_like(token)

    land = lax.empty((4, r, wd), buf.dtype)
    return pl.pallas_call(
        body, name=name,
        out_shape=(pltpu.SemaphoreType.DMA((3,)), pltpu.SemaphoreType.DMA((3,)), pltpu.HBM(buf.shape, buf.dtype),
                   pltpu.HBM(land.shape, land.dtype), _sds((8, LANE))),
        in_specs=(HBM, HBM), out_specs=(SEM, SEM, HBM, HBM, IN_VMEM), input_output_aliases={0: 2, 1: 3},
        compiler_params=pltpu.CompilerParams(has_side_effects=EFFECT),
    )(pltpu.with_memory_space_constraint(buf, pltpu.HBM), pltpu.with_memory_space_constraint(land, pltpu.HBM))


def _x4_wait(send, recv, buf_thru, land_thru, after, name, a2a):
    def body(in_ref, land_ref, send_ref, recv_ref, after_ref, in_dead, got_ref):
        for _, arrive in _x4_peers(in_ref, land_ref, send_ref, recv_ref, a2a):
            arrive.wait_send()
            arrive.wait_recv()

    return pl.pallas_call(
        body, name=name, out_shape=(pltpu.HBM(buf_thru.shape, buf_thru.dtype), pltpu.HBM(land_thru.shape, land_thru.dtype)),
        in_specs=(HBM, HBM, SEM, SEM, ANY), out_specs=(HBM, HBM), input_output_aliases={0: 0, 1: 1},
        compiler_params=pltpu.CompilerParams(has_side_effects=EFFECT),
    )(buf_thru, land_thru, send, recv, after)


def _ag8_peers(x_ref, land_ref, send, recv):
    mx, my, mc = _coords()
    me = 4 * mx + 2 * my + mc
    out = []
    for k in range(1, 8):
        px = 1 - mx if k & 4 else mx
        py = 1 - my if k & 2 else my
        pc = 1 - mc if k & 1 else mc
        mk = functools.partial(pltpu.make_async_remote_copy, src_ref=x_ref, send_sem=send.at[k - 1], recv_sem=recv.at[k - 1],
                               device_id=(px, py, pc), device_id_type=MESH)
        out.append((mk(dst_ref=land_ref.at[me]), mk(dst_ref=land_ref.at[4 * px + 2 * py + pc])))
    return out


def _split_start(x, land_shape, peers, n_copies, name):
    def body(x_ref, land_ref, send, recv, x_thru, land_thru, token):
        for start, _ in peers(x_ref, land_ref, send, recv):
            start.start()
        token[...] = jnp.zeros_like(token)

    land = lax.empty(land_shape, x.dtype)
    return pl.pallas_call(
        body, name=name,
        out_shape=(pltpu.SemaphoreType.DMA((n_copies,)), pltpu.SemaphoreType.DMA((n_copies,)), pltpu.HBM(x.shape, x.dtype),
                   pltpu.HBM(land.shape, land.dtype), _sds((8, LANE))),
        in_specs=(HBM, HBM), out_specs=(SEM, SEM, HBM, HBM, IN_VMEM), input_output_aliases={0: 2, 1: 3},
        compiler_params=pltpu.CompilerParams(has_side_effects=EFFECT),
    )(pltpu.with_memory_space_constraint(x, pltpu.HBM), pltpu.with_memory_space_constraint(land, pltpu.HBM))


def _split_wait(handle, after, peers, name):
    send, recv, x_thru, land_thru, _ = handle

    def body(x_ref, land_ref, send_ref, recv_ref, after_ref, x_dead, got_ref):
        for _, arrive in peers(x_ref, land_ref, send_ref, recv_ref):
            arrive.wait_send()
            arrive.wait_recv()

    return pl.pallas_call(
        body, name=name, out_shape=(pltpu.HBM(x_thru.shape, x_thru.dtype), pltpu.HBM(land_thru.shape, land_thru.dtype)),
        in_specs=(HBM, HBM, SEM, SEM, ANY), out_specs=(HBM, HBM), input_output_aliases={0: 0, 1: 1},
        compiler_params=pltpu.CompilerParams(has_side_effects=EFFECT),
    )(x_thru, land_thru, send, recv, after)


def _sib_peers(x_ref, land_ref, send, recv):
    mx, my, mc = _coords()
    cp = pltpu.make_async_remote_copy(src_ref=x_ref, dst_ref=land_ref, send_sem=send.at[0], recv_sem=recv.at[0],
                                      device_id=(mx, my, 1 - mc), device_id_type=MESH)
    return [(cp, cp)]


def _xchg_sib(x, name):
    def body(in_ref, o_ref, send, recv):
        mx, my, mc = _coords()
        cp = pltpu.make_async_remote_copy(src_ref=in_ref, dst_ref=o_ref, send_sem=send, recv_sem=recv,
                                          device_id=(mx, my, 1 - mc), device_id_type=MESH)
        cp.start()
        cp.wait_recv()
        cp.wait_send()

    return pl.pallas_call(
        body, name=name, out_shape=_sds(x.shape, x.dtype), in_specs=[ANY], out_specs=ANY,
        scratch_shapes=[pltpu.SemaphoreType.DMA(()), pltpu.SemaphoreType.DMA(())])(x)


def _sum_slots(gat, slots, name, tr=None):
    n, r, wd = gat.shape
    tr = r if tr is None else tr

    def body(g_ref, o_ref):
        acc = g_ref[slots[0]].astype(F32)
        for s in slots[1:]:
            acc = acc + g_ref[s].astype(F32)
        o_ref[...] = acc

    return _call(body, name, (r // tr,), [pl.BlockSpec((n, tr, wd), lambda i: (0, i, 0))], _rows(tr, wd), _sds((r, wd)))(gat)


def _add(a, b, name, tr):
    def body(a_ref, b_ref, o_ref):
        o_ref[...] = a_ref[...] + b_ref[...]

    r, wd = a.shape
    return _call(body, name, (r // tr,), [_rows(tr, wd), _rows(tr, wd)], _rows(tr, wd), _sds((r, wd)))(a, b)


def _ada_fwd(x16, ada_w_loc, ada_b_loc):
    nloc = ada_w_loc.shape[-1]

    def body(x_ref, w_ref, b_ref, s_ref, o_ref):
        s = _silu(x_ref[...])
        s_ref[...] = s
        o_ref[...] = jnp.dot(s, w_ref[...], precision=HI, preferred_element_type=F32) + b_ref[...]

    return _call(body, "ada_fwd", (2,),
                 [_full(16, D), pl.BlockSpec((None, D, nloc), lambda l: (l, 0, 0)), pl.BlockSpec((None, 1, nloc), lambda l: (l, 0, 0))],
                 [_full(16, D), pl.BlockSpec((None, 16, nloc), lambda l: (l, 0, 0))],
                 [_sds((16, D)), _sds((2, 16, nloc))])(x16, ada_w_loc, ada_b_loc[:, None, :])


def _ada_bwd(s16, d_loc, ada_w_loc):
    nloc = ada_w_loc.shape[-1]

    def body(s_ref, d_ref, w_ref, gw_ref, cp_ref):
        gw_ref[...] = lax.dot_general(s_ref[...], d_ref[...], (((0,), (0,)), ((), ())), precision=HI,
                                      preferred_element_type=F32)

        @pl.when(pl.program_id(0) == 0)
        def _():
            cp_ref[...] = lax.dot_general(d_ref[8:16, :], w_ref[...], (((1,), (1,)), ((), ())), precision=HI,
                                          preferred_element_type=F32)

    return _call(body, "ada_bwd", (2,),
                 [_full(16, D), pl.BlockSpec((None, 16, nloc), lambda l: (l, 0, 0)), pl.BlockSpec((None, D, nloc), lambda l: (l, 0, 0))],
                 [pl.BlockSpec((None, D, nloc), lambda l: (l, 0, 0)), _full(8, D)],
                 [_sds((2, D, nloc)), _sds((8, D))])(s16, d_loc, ada_w_loc)


def _cctx_grad(dscc, c_ctx):
    def body(d_ref, c_ref, o_ref):
        _, vjp = jax.vjp(_silu, c_ref[...])
        o_ref[...] = vjp(d_ref[...])[0]

    return _call(body, "cctx_grad", (1,), [_full(8, D), _full(8, D)], _full(8, D), _sds((8, D)))(dscc, c_ctx)


def _adamw_math(w, g, m, v):
    mn = ADAM_B1 * m + (1.0 - ADAM_B1) * g
    vn = ADAM_B2 * v + (1.0 - ADAM_B2) * jnp.square(g)
    c1 = 1.0 - ADAM_B1 ** ADAM_STEP
    c2 = 1.0 - ADAM_B2 ** ADAM_STEP
    return -ADAM_LR * ((mn / c1) / (jnp.sqrt(vn / c2) + ADAM_EPS) + ADAM_WD * w), mn, vn


def _adamw(w, g, m, v, name):
    n_l, r, wd = w.shape
    tr = 256 if r % 256 == 0 else r

    def body(w_ref, g_ref, m_ref, v_ref, d_ref, mo_ref, vo_ref):
        d_ref[...], mo_ref[...], vo_ref[...] = _adamw_math(w_ref[...], g_ref[...], m_ref[...], v_ref[...])

    spec = pl.BlockSpec((None, tr, wd), lambda a, i: (a, i, 0))
    return tuple(_call(body, name, (n_l, r // tr), [spec] * 4, [spec] * 3, [_sds(w.shape)] * 3)(w, g, m, v))


def _adamw_small(ws, gs, ms, vs, name):
    n = len(ws)
    shapes = [a.shape for a in ws]
    as2d = lambda a: a.reshape(-1, a.shape[-1])

    def body(*refs):
        ins, outs = refs[:4 * n], refs[4 * n:]
        for k in range(n):
            res = _adamw_math(ins[k][...], ins[n + k][...], ins[2 * n + k][...], ins[3 * n + k][...])
            for j in range(3):
                outs[j * n + k][...] = res[j]

    flat = [as2d(a) for group in (ws, gs, ms, vs) for a in group]
    specs = [_full(*a.shape) for a in flat]
    outs = _call(body, name, (1,), specs, specs[:n] * 3, [_sds(a.shape) for a in flat[:n]] * 3)(*flat)
    return tuple([outs[j * n + k].reshape(shapes[k]) for k in range(n)] for j in range(3))


ROW = 1024


def _nrows(size):
    return -(-size // ROW)


def _pack(arrs, rows_total, dtype=F32):
    parts = []
    for a in arrs:
        flat = a.reshape(-1).astype(dtype)
        pad = _nrows(flat.shape[0]) * ROW - flat.shape[0]
        parts.append(flat if pad == 0 else jnp.concatenate([flat, jnp.zeros((pad,), dtype)]))
    flat = jnp.concatenate(parts)
    out = flat.reshape(-1, ROW)
    return _pad_rows(out, rows_total)


def _unpack(buf, shapes):
    lead = buf.shape[:-2]
    out, r0 = [], 0
    for shp in shapes:
        size = 1
        for s in shp:
            size *= s
        nr = _nrows(size)
        piece = lax.slice_in_dim(buf, r0, r0 + nr, axis=len(lead))
        out.append(piece.reshape(lead + (nr * ROW,))[..., :size].reshape(lead + tuple(shp)))
        r0 += nr
    return out


SLOT = 16


def _slot_rows(size):
    return _round_up(size // ROW, SLOT)


def _pack_rows(arrs, rows_total, dtype):
    parts, used = [], 0
    for a in arrs:
        part = a.astype(dtype).reshape(-1, ROW)
        extra = _slot_rows(a.size) - part.shape[0]
        parts.append(part if extra == 0 else jnp.pad(part, ((0, extra), (0, 0))))
        used += _slot_rows(a.size)
    if rows_total > used:
        parts.append(jnp.zeros((rows_total - used, ROW), dtype))
    return jnp.concatenate(parts, axis=0)


def _unpack_rows(buf, shapes):
    lead = buf.shape[:-2]
    out, r0 = [], 0
    for shp in shapes:
        size = 1
        for s in shp:
            size *= s
        piece = lax.slice_in_dim(buf, r0, r0 + size // ROW, axis=len(lead))
        out.append(piece.reshape(lead + tuple(shp)))
        r0 += _slot_rows(size)
    return out


def _round_up(n, k):
    return -(-n // k) * k


WEIGHTS = ['c_ctx', 'ada_w', 'ada_b', 'norm_mix_g', 'norm_mlp_g', 'mlp_w1', 'mlp_w2', 'hy_w_in', 'ssd_conv_w', 'ssd_conv_b',
           'ssd_dt_bias', 'ssd_a_log', 'ssd_d', 'ssd_norm_g', 'sg_ln_g', 'sg_ln_b', 'sg_w', 'sg_b', 'hy_w_out', 'cf_w_pw1',
           'cf_b_pw1', 'cf_w_dw', 'cf_b_dw', 'cf_ln_g', 'cf_ln_b', 'cf_w_pw2', 'cf_b_pw2', 'final_norm_g']
BIG = {'mlp_w1': 2, 'mlp_w2': 1, 'hy_w_in': 2, 'hy_w_out': 1, 'cf_w_pw1': 2, 'cf_w_pw2': 1}
SMALL_SHARD = ['ssd_conv_w', 'cf_b_pw1', 'cf_w_dw', 'cf_b_dw', 'cf_ln_g', 'cf_ln_b', 'cf_b_pw2']
REP = ['norm_mix_g', 'norm_mlp_g', 'ssd_conv_b', 'ssd_dt_bias', 'ssd_a_log', 'ssd_d', 'ssd_norm_g', 'sg_ln_g', 'sg_ln_b',
       'sg_w', 'sg_b', 'final_norm_g']


def _gather_shards(stacked, axis):
    return jnp.concatenate([stacked[j] for j in range(4)], axis=axis)


def kernel(x, c, ctx, c_ctx, ada_w, ada_b, norm_mix_g, norm_mlp_g, mlp_w1, mlp_w2, hy_w_in, ssd_conv_w, ssd_conv_b, ssd_dt_bias, ssd_a_log, ssd_d, ssd_norm_g, sg_ln_g, sg_ln_b, sg_w, sg_b, hy_w_out, cf_w_pw1, cf_b_pw1, cf_w_dw, cf_b_dw, cf_ln_g, cf_ln_b, cf_w_pw2, cf_b_pw2, final_norm_g, loss_target, m_c_ctx, m_ada_w, m_ada_b, m_norm_mix_g, m_norm_mlp_g, m_mlp_w1, m_mlp_w2, m_hy_w_in, m_ssd_conv_w, m_ssd_conv_b, m_ssd_dt_bias, m_ssd_a_log, m_ssd_d, m_ssd_norm_g, m_sg_ln_g, m_sg_ln_b, m_sg_w, m_sg_b, m_hy_w_out, m_cf_w_pw1, m_cf_b_pw1, m_cf_w_dw, m_cf_b_dw, m_cf_ln_g, m_cf_ln_b, m_cf_w_pw2, m_cf_b_pw2, m_final_norm_g, v_c_ctx, v_ada_w, v_ada_b, v_norm_mix_g, v_norm_mlp_g, v_mlp_w1, v_mlp_w2, v_hy_w_in, v_ssd_conv_w, v_ssd_conv_b, v_ssd_dt_bias, v_ssd_a_log, v_ssd_d, v_ssd_norm_g, v_sg_ln_g, v_sg_ln_b, v_sg_w, v_sg_b, v_hy_w_out, v_cf_w_pw1, v_cf_b_pw1, v_cf_w_dw, v_cf_b_dw, v_cf_ln_g, v_cf_ln_b, v_cf_w_pw2, v_cf_b_pw2, v_final_norm_g):
    args = locals()
    wl = {n: args[n] for n in WEIGHTS}
    ml = {n: args["m_" + n] for n in WEIGHTS}
    vl = {n: args["v_" + n] for n in WEIGHTS}
    mx, my, mc = _coords()
    me = 4 * mx + 2 * my + mc
    shard = 2 * mx + my
    even = (0, 2, 4, 6)

    def start_gather(names, name, tie=None):
        rows = sum(_slot_rows(wl[n].size) for n in names)
        buf = _pack_rows([wl[n] for n in names], rows, MXU)
        if tie is not None:
            buf, _ = lax.optimization_barrier((buf, tie))
        return _x4_start(buf, name, a2a=False)

    def finish_gather(handle, names, after, name):
        send, recv, own, land, _ = handle
        own, land = _x4_wait(send, recv, own, land, after, name, a2a=False)
        got = lax.dynamic_update_slice(land, own[None], (shard, 0, 0))
        shapes = [wl[n].shape for n in names]
        wfull = {n: _gather_shards(st, BIG[n]) for n, st in zip(names, _unpack_rows(got, shapes))}
        return wfull, got

    rest_names = ["mlp_w1", "mlp_w2", "hy_w_out", "cf_w_pw1", "cf_w_pw2"]
    h_in = _x4_start(hy_w_in[0].astype(MXU), "agw_in_start", a2a=False)
    c = c + h_in[4][0, 0]

    small_shapes = [wl[n].shape for n in SMALL_SHARD]
    blk1 = _pack([c] + [wl[n] for n in SMALL_SHARD], 24)
    got1 = _ag8(blk1, "ag_cond")
    x16 = _pad_rows(jnp.concatenate([got1[:, 0, :], c_ctx[None]], axis=0), 16)
    small_full = {}
    for n, parts in zip(SMALL_SHARD, _unpack(got1[:, 1:, :], small_shapes)):
        small_full[n] = jnp.concatenate([parts[s] for s in even], axis=-1)

    nloc = ada_w.shape[-1]
    ada_b_loc = lax.dynamic_slice_in_dim(ada_b, shard * nloc, nloc, axis=1)
    s16, ada_loc = _ada_fwd(x16, ada_w, ada_b_loc)
    got2 = _ag8(ada_loc.reshape(32, nloc), "ag_ada").reshape(8, 2, 16, nloc)
    ada_full = jnp.concatenate([got2[s] for s in even], axis=-1)
    ada_me = lax.dynamic_slice_in_dim(ada_full, me, 1, axis=1)[:, 0, :]
    cada0 = ada_full[0, 8, :]

    w = {n: wl[n] for n in WEIGHTS if n not in BIG and n not in SMALL_SHARD}
    w.update(small_full)
    h_rest = start_gather(rest_names, "agw_rest_start", tie=ada_me)
    send, recv, own, land, _ = h_in
    own, land = _x4_wait(send, recv, own, land, h_rest[4], "agw_in_wait", a2a=False)
    w["wz"], w["wxbc"], w["wdt"], w["wuv"] = _win_split(lax.dynamic_update_slice(land, own[None], (shard, 0, 0)))
    w["sg_w"] = sg_w[0].astype(MXU)

    def late_w(after):
        wfull, got = finish_gather(h_rest, rest_names, after, "agw_rest_wait")
        return {"hy_w_out": wfull["hy_w_out"][0], "wpack": got,
                "cf_w_pw1": wfull["cf_w_pw1"][0], "cf_w_pw2": wfull["cf_w_pw2"][0]}

    full_shape = {n: wl[n].shape for n in WEIGHTS}
    for n in BIG:
        full_shape[n] = tuple(s * 4 if a == BIG[n] else s for a, s in enumerate(wl[n].shape))
    for n in SMALL_SHARD:
        full_shape[n] = wl[n].shape[:-1] + (wl[n].shape[-1] * 4,)

    early_names = ["mlp_w1", "mlp_w2", "cf_w_pw1", "cf_w_pw2", "hy_w_out"]
    early = {}

    def early_grads(gp):
        used = sum(gp[n].shape[1] for n in early_names)
        parts = [gp[n] for n in early_names] + [jnp.zeros((4, _round_up(used, 512) - used, ROW), jnp.bfloat16)]
        early["h"] = _x4_start(jnp.concatenate(parts, axis=1), "a2a_early_start", a2a=True)
        return early["h"][4][0, 0]

    sm_names = REP + SMALL_SHARD
    r_ada = sum(_nrows(wl[n].size * (4 if n in SMALL_SHARD else 1)) for n in sm_names)
    small = {}

    def small_grads(g_, d_ada_, d_cada0_):
        buf = _pack([g_[n] for n in sm_names] + [d_ada_, d_cada0_], _round_up(r_ada + 18, 16), jnp.bfloat16)
        small["h"] = _split_start(buf, (8,) + buf.shape, _ag8_peers, 7, "ag_small_start")
        return small["h"][4][0, 0]

    loss_part, dx, g, d_ada, d_cada0 = _local_step(x[0], ctx[0], loss_target[0], ada_me, cada0, w, late_w, early_grads,
                                                   small_grads)
    gp_last = g["pieces"]["hy_w_in"]

    delta, new_m, new_v, grads = {}, {}, {}, {}
    h_last = _x4_start(gp_last, "a2a_last_start", a2a=True)
    send, recv, own, land, _ = early["h"]
    own, land = _x4_wait(send, recv, own, land, h_last[4], "a2a_early_wait", a2a=True)
    got = lax.dynamic_update_slice(land, lax.dynamic_slice_in_dim(own, shard, 1, axis=0), (shard, 0, 0))
    part_early = _sum_slots(got, (0, 1, 2, 3), "sum_grads_early", tr=512)
    h_swap = _split_start(part_early, part_early.shape, _sib_peers, 1, "swap_early_start")

    small_in, land3 = _split_wait(small["h"], h_swap[4], _ag8_peers, "ag_small_wait")
    got3 = lax.dynamic_update_slice(land3, small_in[None], (me, 0, 0))
    tot3 = _sum_slots(got3, tuple(range(8)), "sum_small")
    sm_tot = _unpack(tot3, [full_shape[n] for n in sm_names] + [(2, 6 * D), (6 * D,)])
    grads.update(zip(sm_names, sm_tot[:-2]))
    for n in SMALL_SHARD:
        k = wl[n].shape[-1]
        grads[n] = lax.dynamic_slice_in_dim(grads[n], shard * k, k, axis=grads[n].ndim - 1)
    dada_tot, dcada_tot = sm_tot[-2], sm_tot[-1]
    grads["ada_b"] = dada_tot.at[0].add(dcada_tot)
    dada_all = got3[:, r_ada:r_ada + 12, :].astype(F32).reshape(8, 2, 6 * D)
    d16 = jnp.concatenate([jnp.transpose(dada_all, (1, 0, 2)),
                           jnp.stack([dcada_tot, jnp.zeros_like(dcada_tot)])[:, None, :],
                           jnp.zeros((2, 7, 6 * D), F32)], axis=1)
    d_loc = lax.dynamic_slice_in_dim(d16, shard * nloc, nloc, axis=2)
    grads["ada_w"], cpart = _ada_bwd(s16, d_loc, ada_w)
    cpart = lax.dynamic_update_slice(cpart, jnp.full((1, D), loss_part, F32), (1, 0))
    h_cctx = _split_start(cpart, (8,) + cpart.shape, _ag8_peers, 7, "ag_cctx_start")
    grads["ada_w"], _ = lax.optimization_barrier((grads["ada_w"], h_cctx[4]))

    delta["ada_w"], new_m["ada_w"], new_v["ada_w"] = _adamw(ada_w, grads["ada_w"], ml["ada_w"], vl["ada_w"], "adamw_ada_w")
    small_names = ["ada_b"] + REP + SMALL_SHARD
    outs = _adamw_small(*([src[n].reshape(wl[n].shape) for n in small_names] for src in (wl, grads, ml, vl)), "adamw_small")
    for dst, vals in zip((delta, new_m, new_v), outs):
        dst.update(zip(small_names, vals))

    def step(names, tot):
        if len(names) == 1:
            grads[names[0]] = tot.reshape(wl[names[0]].shape)
        else:
            grads.update(zip(names, _unpack_rows(tot, [wl[n].shape for n in names])))
        for n in names:
            delta[n], new_m[n], new_v[n] = _adamw(wl[n], grads[n], ml[n], vl[n], "adamw_" + n)

    part_early, sib_early = _split_wait(h_swap, delta["ada_w"], _sib_peers, "swap_early_wait")
    step(early_names, _add(part_early, sib_early, "add_grads_early", 512))
    send, recv, own, land, _ = h_last
    own, land = _x4_wait(send, recv, own, land, delta["mlp_w1"], "a2a_last_wait", a2a=True)
    got = lax.dynamic_update_slice(land, lax.dynamic_slice_in_dim(own, shard, 1, axis=0), (shard, 0, 0))
    part_last = _sum_slots(got, (0, 1, 2, 3), "sum_grads_last", tr=512)
    step(["hy_w_in"], _add(part_last, _xchg_sib(part_last, "swap_grads_last"), "add_grads_last", 512))

    c_in, land4 = _split_wait(h_cctx, delta["hy_w_in"], _ag8_peers, "ag_cctx_wait")
    got4 = lax.dynamic_update_slice(land4, c_in[None], (me, 0, 0))
    loss = _sum_slots(got4, tuple(range(8)), "sum_loss")[1, 0]
    grads["c_ctx"] = _cctx_grad(_sum_slots(got4, even, "sum_cctx"), _pad_rows(c_ctx[None], 8))[0]
    outs = _adamw_small(*([src["c_ctx"]] for src in (wl, grads, ml, vl)), "adamw_c_ctx")
    delta["c_ctx"], new_m["c_ctx"], new_v["c_ctx"] = (o[0] for o in outs)

    return (loss, dx[None], *[grads[n].reshape(wl[n].shape) for n in WEIGHTS], *[delta[n] for n in WEIGHTS],
            *[new_m[n] for n in WEIGHTS], *[new_v[n] for n in WEIGHTS])
```

```python
import functools

import jax
import jax.numpy as jnp
from jax import lax
from jax.experimental import pallas as pl
from jax.experimental.pallas import tpu as pltpu

F32 = jnp.float32
MXU = jnp.bfloat16
ACT = jnp.bfloat16
HI = lax.Precision.HIGHEST
EPS = 1e-6

D = 1024
HID = 4096
XBC = 1536
Q = 128
SGG = 8
CK = 31
SK = 5
T = 256
LANE = 128
VMEM_LIMIT = 56 * 1024 * 1024

ADAM_LR, ADAM_B1, ADAM_B2, ADAM_EPS, ADAM_WD, ADAM_STEP = 0.001, 0.9, 0.999, 1e-08, 0.01, 10


def _call(body, name, grid, in_specs, out_specs, out_shape, scratch=()):
    return pl.pallas_call(
        body, name=name, grid=grid, in_specs=in_specs, out_specs=out_specs, out_shape=out_shape,
        scratch_shapes=list(scratch),
        compiler_params=pltpu.CompilerParams(dimension_semantics=("arbitrary",) * len(grid),
                                             vmem_limit_bytes=VMEM_LIMIT))


def _sds(shape, dt=F32):
    return jax.ShapeDtypeStruct(tuple(shape), dt)


def _rows(t, w, off=0, lane_blk=0):
    return pl.BlockSpec((t, w), lambda i: (i + off, lane_blk))


def _rows_lat(t, w, nct):
    return pl.BlockSpec((t, w), lambda i: (jnp.maximum(i - nct, 0), 0))


def _full(*shape):
    return pl.BlockSpec(shape, lambda *_: (0,) * len(shape))


def _wfull(*shape):
    return pl.BlockSpec(shape, lambda *_: (0,) * len(shape), pipeline_mode=pl.Buffered(1))


def _halo(w, hb, nrows):
    r, nb = T // hb, nrows // hb
    prev = pl.BlockSpec((hb, w), lambda i: (jnp.maximum(i * r - 1, 0), 0))
    nxt = pl.BlockSpec((hb, w), lambda i: (jnp.minimum((i + 1) * r, nb - 1), 0))
    return prev, nxt


def _nn(a, b):
    return jnp.dot(a.astype(MXU), b.astype(MXU), preferred_element_type=F32)


def _nt(a, b):
    return lax.dot_general(a.astype(MXU), b.astype(MXU), (((1,), (1,)), ((), ())), preferred_element_type=F32)


def _tn_dot(a, b):
    return lax.dot_general(a.astype(MXU), b.astype(MXU), (((0,), (0,)), ((), ())), preferred_element_type=F32)


def _sum0(x):
    return jnp.sum(x, axis=0, keepdims=True)


def _silu(x):
    return x * jax.nn.sigmoid(x)


def _gelu(x):
    return jax.nn.gelu(x, approximate=True)


def _mod(h, g, sh, sc):
    n = h * lax.rsqrt(jnp.mean(h * h, axis=-1, keepdims=True) + EPS)
    return n * g * (1.0 + sc) + sh


def _ln(x, g, b):
    xc = x - jnp.mean(x, axis=-1, keepdims=True)
    return xc * lax.rsqrt(jnp.mean(xc * xc, axis=-1, keepdims=True) + EPS) * g + b


def _gate_norm(ytot, z, ng):
    yg = ytot * _silu(z)
    halves = []
    for k in range(2):
        seg = yg[:, k * 512:(k + 1) * 512]
        halves.append(seg * lax.rsqrt(jnp.mean(seg * seg, axis=-1, keepdims=True) + EPS) * ng[:, k * 512:(k + 1) * 512])
    return jnp.concatenate(halves, axis=-1)


def _fill_ext(ext_ref, prev_ref, cur_ref, next_ref, hb, first, last):
    ext_ref[0:hb, :] = jnp.where(first, 0.0, prev_ref[...])
    ext_ref[hb:hb + T, :] = cur_ref[...]
    ext_ref[hb + T:hb + T + hb, :] = jnp.where(last, 0.0, next_ref[...])


def _conv(ext_ref, w_ref, k_taps, hb, lanes):
    off = hb - k_taps // 2
    acc = ext_ref[pl.ds(off, T), lanes] * w_ref[0:1, lanes]
    for k in range(1, k_taps):
        acc = acc + ext_ref[pl.ds(off + k, T), lanes] * w_ref[k:k + 1, lanes]
    return acc


def _conv_tr(ext_ref, w_ref, k_taps, hb, lanes):
    off = hb + k_taps // 2
    acc = ext_ref[pl.ds(off, T), lanes] * w_ref[0:1, lanes]
    for k in range(1, k_taps):
        acc = acc + ext_ref[pl.ds(off - k, T), lanes] * w_ref[k:k + 1, lanes]
    return acc


def _conv_wide(ext_ref, w_ref, k_taps, hb, lanes, flip=False):
    base = hb - k_taps // 2
    acc = None
    for b in range(8):
        taps = [k for k in range(k_taps) if (base + k) % 8 == b]
        if not taps:
            continue
        p = None
        for k in taps:
            wi = (k_taps - 1 - k) if flip else k
            term = ext_ref[pl.ds(base + k - b, T + 8), lanes] * w_ref[wi:wi + 1, lanes]
            p = term if p is None else p + term
        acc = p[b:b + T] if acc is None else acc + p[b:b + T]
    return acc


def _conv_dw_wide(dw_ref, d_ref, xext_ref, k_taps, hb, lanes):
    base = hb - k_taps // 2
    d = d_ref[:, lanes]
    for b in range(8):
        taps = [k for k in range(k_taps) if (base + k) % 8 == b]
        if not taps:
            continue
        lo_off = base + taps[0] - b
        span = base + taps[-1] - b - lo_off
        xs = xext_ref[pl.ds(lo_off + b, T + span), lanes]
        for k in taps:
            a = base + k - b - lo_off
            dw_ref[k:k + 1, lanes] += _sum0(d * xs[a:a + T])


def _conv_dw(dw_ref, d_ref, xext_ref, k_taps, hb, lanes):
    off = hb - k_taps // 2
    d = d_ref[:, lanes]
    for k in range(k_taps):
        dw_ref[k:k + 1, lanes] += _sum0(d * xext_ref[pl.ds(off + k, T), lanes])


def _tn(a, b, name, relu2=False, shard=None):
    m_rows, ka = a.shape
    n = b.shape[1]
    tm = next(t for t in (1024, 768, 512, 256) if m_rows % t == 0)
    tk = min(ka, 1024)
    tn = n if n <= 1024 else next(t for t in (1024, 768, 512, 384, 256, 128) if n % t == 0)
    if shard is not None and shard[0] == "col":
        tn = shard[1]
    if shard is not None and shard[0] == "row":
        tk = shard[1]
    n_m = m_rows // tm

    def body(a_ref, b_ref, o_ref, *acc):
        acc_ref = acc[0] if acc else o_ref

        @pl.when(pl.program_id(2) == 0)
        def _():
            acc_ref[...] = jnp.zeros_like(acc_ref)
        av = a_ref[...]
        if relu2:
            av = jnp.square(jnp.maximum(av.astype(F32), 0.0))
        acc_ref[...] += _tn_dot(av, b_ref[...])
        if acc:
            @pl.when(pl.program_id(2) == n_m - 1)
            def _():
                o_ref[...] = acc_ref[...].astype(o_ref.dtype)

    in_specs = [pl.BlockSpec((tm, tk), lambda k, j, m: (m, k)), pl.BlockSpec((tm, tn), lambda k, j, m: (m, j))]
    if shard is None:
        return _call(body, name, (ka // tk, n // tn, n_m), in_specs,
                     pl.BlockSpec((tk, tn), lambda k, j, m: (k, j)), _sds((ka, n)))(a, b)
    if shard[0] == "col":
        out_spec, out_shape = pl.BlockSpec((None, tk, tn), lambda k, j, m: (j, k, 0)), _sds((n // tn, ka, tn), jnp.bfloat16)
    else:
        out_spec, out_shape = pl.BlockSpec((None, tk, tn), lambda k, j, m: (k, 0, j)), _sds((ka // tk, tk, n), jnp.bfloat16)
    return _call(body, name, (ka // tk, n // tn, n_m), in_specs, out_spec, out_shape,
                 scratch=[pltpu.VMEM((tk, tn), F32)])(a, b)


def _mlp_fwd(h, mp, wpack, layer, name):
    n_rows = h.shape[0]

    def body(h_ref, mp_ref, w1_ref, w2_ref, hn_ref, a_ref, y_ref):
        hv = h_ref[...]
        u = _mod(hv, mp_ref[0:1], mp_ref[1:2], mp_ref[2:3]).astype(MXU)
        acc = jnp.zeros((T, D), F32)
        for j in range(HID // 1024):
            cs = slice(j * 1024, (j + 1) * 1024)
            a = jnp.dot(u, w1_ref[j], preferred_element_type=F32)
            a_ref[:, cs] = a.astype(ACT)
            acc = acc + jnp.dot(jnp.square(jnp.maximum(a, 0.0)).astype(MXU), w2_ref[j], preferred_element_type=F32)
        y_ref[...] = acc
        hn_ref[...] = hv + mp_ref[3:4] * acc

    return _call(body, name, (n_rows // T,),
                 [_rows(T, D), _full(8, D), _mlp_wspec(layer), _mlp_wspec(2 + layer)],
                 [_rows(T, D), _rows(T, HID), _rows(T, D)],
                 [_sds((n_rows, D)), _sds((n_rows, HID), ACT), _sds((n_rows, D))])(h, mp, wpack, wpack)


def _mlp_wspec(row_block):
    return pl.BlockSpec((4, 1024, 1024), lambda i: (0, row_block, 0), pipeline_mode=pl.Buffered(1))


def _mlp_bwd(dh, h, a, y, mp, wpack, layer, name):
    n_rows = h.shape[0]

    def body(dh_ref, h_ref, a_ref, y_ref, mp_ref, w1_ref, w2_ref, dho_ref, da_ref, dyb_ref, ub_ref, pg_ref):
        dhp = dh_ref[...]
        u, vjp = jax.vjp(_mod, h_ref[...], mp_ref[0:1], mp_ref[1:2], mp_ref[2:3])
        ub_ref[...] = u.astype(ACT)
        dyb = (mp_ref[3:4] * dhp).astype(MXU)
        dyb_ref[...] = dyb.astype(ACT)
        du = jnp.zeros((T, D), F32)
        for j in range(HID // 1024):
            cs = slice(j * 1024, (j + 1) * 1024)
            dp = _nt(dyb, w2_ref[j])
            da = dp * 2.0 * jnp.maximum(a_ref[:, cs].astype(F32), 0.0)
            da_ref[:, cs] = da.astype(ACT)
            du = du + _nt(da, w1_ref[j])
        dhn, dg, dsh, dsc = vjp(du)
        dho_ref[...] = dhp + dhn

        @pl.when(pl.program_id(0) == 0)
        def _():
            pg_ref[...] = jnp.zeros_like(pg_ref)
        pg_ref[0:1] += dg
        pg_ref[1:2] += dsh
        pg_ref[2:3] += dsc
        pg_ref[3:4] += _sum0(dhp * y_ref[...])

    return _call(body, name, (n_rows // T,),
                 [_rows(T, D), _rows(T, D), _rows(T, HID), _rows(T, D), _full(8, D), _mlp_wspec(layer), _mlp_wspec(2 + layer)],
                 [_rows(T, D), _rows(T, HID), _rows(T, D), _rows(T, D), _full(8, D)],
                 [_sds((n_rows, D)), _sds((n_rows, HID), ACT), _sds((n_rows, D), ACT), _sds((n_rows, D), ACT),
                  _sds((8, D))])(dh, h, a, y, mp, wpack, wpack)


def _cf1_fwd(h, mp, w1, b1):
    n_rows = h.shape[0]

    def body(h_ref, mp_ref, w1_ref, b1_ref, glu_ref, a_ref):
        u = _mod(h_ref[...], mp_ref[0:1], mp_ref[1:2], mp_ref[2:3]).astype(MXU)
        a = jnp.dot(u, w1_ref[...], preferred_element_type=F32) + b1_ref[...]
        a_ref[...] = a.astype(ACT)
        glu_ref[...] = a[:, :D] * jax.nn.sigmoid(a[:, D:])

    return _call(body, "cf1_fwd", (n_rows // T,),
                 [_rows(T, D), _full(8, D), _wfull(D, 2 * D), _full(1, 2 * D)],
                 [_rows(T, D), _rows(T, 2 * D)],
                 [_sds((n_rows, D)), _sds((n_rows, 2 * D), ACT)])(h, mp, w1, b1)


def _cf2_fwd(h, glu, mp, wdw, bdw, lng, lnb, w2, b2):
    n_rows = h.shape[0]
    nt = n_rows // T
    hb = 16

    def body(h_ref, gp_ref, gc_ref, gn_ref, mp_ref, wdw_ref, bdw_ref, lng_ref, lnb_ref, w2_ref, b2_ref,
             hn_ref, cv_ref, sb_ref, y_ref, ext):
        i = pl.program_id(0)
        _fill_ext(ext, gp_ref, gc_ref, gn_ref, hb, i == 0, i == nt - 1)
        for c in range(D // LANE):
            lanes = slice(c * LANE, (c + 1) * LANE)
            cv_ref[:, lanes] = _conv_wide(ext, wdw_ref, CK, hb, lanes) + bdw_ref[:, lanes]
        s = _silu(_ln(cv_ref[...], lng_ref[...], lnb_ref[...])).astype(MXU)
        sb_ref[...] = s.astype(ACT)
        y = jnp.dot(s, w2_ref[...], preferred_element_type=F32) + b2_ref[...]
        y_ref[...] = y
        hn_ref[...] = h_ref[...] + mp_ref[3:4] * y

    gp, gn = _halo(D, hb, n_rows)
    return _call(body, "cf2_fwd", (nt,),
                 [_rows(T, D), gp, _rows(T, D), gn, _full(8, D), _full(32, D), _full(1, D), _full(1, D), _full(1, D),
                  _wfull(D, D), _full(1, D)],
                 [_rows(T, D), _rows(T, D), _rows(T, D), _rows(T, D)],
                 [_sds((n_rows, D)), _sds((n_rows, D)), _sds((n_rows, D), ACT), _sds((n_rows, D))],
                 scratch=[pltpu.VMEM((T + 2 * hb, D), F32)])(h, glu, glu, glu, mp, wdw, bdw, lng, lnb, w2, b2)


def _cf2_bwd(dh, y, cv, mp, lng, lnb, w2):
    n_rows = dh.shape[0]

    def body(dh_ref, y_ref, cv_ref, mp_ref, lng_ref, lnb_ref, w2_ref, dcv_ref, dyb_ref, pg_ref):
        dhp = dh_ref[...]
        dy = mp_ref[3:4] * dhp
        dyb = dy.astype(MXU)
        dyb_ref[...] = dyb.astype(ACT)
        ds = _nt(dyb, w2_ref[...])
        _, vjp = jax.vjp(lambda cv_, g_, b_: _silu(_ln(cv_, g_, b_)), cv_ref[...], lng_ref[...], lnb_ref[...])
        dcv, dlng, dlnb = vjp(ds)
        dcv_ref[...] = dcv

        @pl.when(pl.program_id(0) == 0)
        def _():
            pg_ref[...] = jnp.zeros_like(pg_ref)
        pg_ref[0:1] += _sum0(dhp * y_ref[...])
        pg_ref[1:2] += _sum0(dy)
        pg_ref[2:3] += dlng
        pg_ref[3:4] += dlnb
        pg_ref[4:5] += _sum0(dcv)

    return _call(body, "cf2_bwd", (n_rows // T,),
                 [_rows(T, D), _rows(T, D), _rows(T, D), _full(8, D), _full(1, D), _full(1, D), _wfull(D, D)],
                 [_rows(T, D), _rows(T, D), _full(8, D)],
                 [_sds((n_rows, D)), _sds((n_rows, D), ACT), _sds((8, D))])(dh, y, cv, mp, lng, lnb, w2)


def _cf1_bwd(dh, h, a, dcv, glu, mp, wdw, w1):
    n_rows = h.shape[0]
    nt = n_rows // T
    hb = 16

    def body(dh_ref, h_ref, a_ref, dp_ref, dc_ref, dn_ref, gp_ref, gc_ref, gn_ref, mp_ref, wdw_ref, w1_ref,
             dho_ref, da_ref, ub_ref, pg_ref, pb_ref, dw_ref, dext, gext, dglu):
        i = pl.program_id(0)

        @pl.when(i == 0)
        def _():
            pg_ref[...] = jnp.zeros_like(pg_ref)
            pb_ref[...] = jnp.zeros_like(pb_ref)
            dw_ref[...] = jnp.zeros_like(dw_ref)
        _fill_ext(dext, dp_ref, dc_ref, dn_ref, hb, i == 0, i == nt - 1)
        _fill_ext(gext, gp_ref, gc_ref, gn_ref, hb, i == 0, i == nt - 1)
        for c in range(D // LANE):
            lanes = slice(c * LANE, (c + 1) * LANE)
            dglu[:, lanes] = _conv_wide(dext, wdw_ref, CK, hb, lanes, flip=True)
            _conv_dw_wide(dw_ref, dc_ref, gext, CK, hb, lanes)
        av = a_ref[...].astype(F32)
        _, vjp_glu = jax.vjp(lambda a1, a2: a1 * jax.nn.sigmoid(a2), av[:, :D], av[:, D:])
        da1, da2 = vjp_glu(dglu[...])
        da_ref[:, :D] = da1.astype(ACT)
        da_ref[:, D:] = da2.astype(ACT)
        pb_ref[0:1, :D] += _sum0(da1)
        pb_ref[0:1, D:] += _sum0(da2)
        du = _nt(da1, w1_ref[:, :D]) + _nt(da2, w1_ref[:, D:])
        u, vjp = jax.vjp(_mod, h_ref[...], mp_ref[0:1], mp_ref[1:2], mp_ref[2:3])
        ub_ref[...] = u.astype(ACT)
        dhn, dg, dsh, dsc = vjp(du)
        dho_ref[...] = dh_ref[...] + dhn
        pg_ref[0:1] += dg
        pg_ref[1:2] += dsh
        pg_ref[2:3] += dsc

    hp, hn = _halo(D, hb, n_rows)
    return _call(body, "cf1_bwd", (nt,),
                 [_rows(T, D), _rows(T, D), _rows(T, 2 * D), hp, _rows(T, D), hn, hp, _rows(T, D), hn,
                  _full(8, D), _full(32, D), _wfull(D, 2 * D)],
                 [_rows(T, D), _rows(T, 2 * D), _rows(T, D), _full(8, D), _full(8, 2 * D), _full(32, D)],
                 [_sds((n_rows, D)), _sds((n_rows, 2 * D), ACT), _sds((n_rows, D), ACT), _sds((8, D)),
                  _sds((8, 2 * D)), _sds((32, D))],
                 scratch=[pltpu.VMEM((T + 2 * hb, D), F32), pltpu.VMEM((T + 2 * hb, D), F32), pltpu.VMEM((T, D), F32)],
                 )(dh, h, a, dcv, dcv, dcv, glu, glu, glu, mp, wdw, w1)


def _sg_blocks():
    return [(c, g, slice(c * Q, (c + 1) * Q), slice(g * LANE, (g + 1) * LANE)) for c in range(T // Q) for g in range(SGG)]


IN_W = D + XBC + 32 + 2 * D
IN_LOC = IN_W // 4


def _win_split(shards):
    o1, o2, o3 = D, D + XBC, D + XBC + 32
    tr = 256

    def cols(s_ref, lo, hi):
        parts = []
        for j in range(4):
            a, b = max(lo, j * IN_LOC), min(hi, (j + 1) * IN_LOC)
            if a < b:
                parts.append(s_ref[j][:, a - j * IN_LOC:b - j * IN_LOC])
        return parts[0] if len(parts) == 1 else jnp.concatenate(parts, axis=1)

    def body(s_ref, wz_ref, wxbc_ref, wdt_ref, wuv_ref):
        wz_ref[...] = cols(s_ref, 0, o1)
        wxbc_ref[...] = cols(s_ref, o1, o2)
        dt = cols(s_ref, o2, o3)
        wdt_ref[...] = jnp.concatenate([dt, jnp.zeros((tr, LANE - 32), dt.dtype)], axis=1)
        wuv_ref[...] = cols(s_ref, o3, IN_W)

    dt_ = shards.dtype
    return _call(body, "win_split", (D // tr,), [pl.BlockSpec((4, tr, IN_LOC), lambda i: (0, i, 0))],
                 [_rows(tr, D), _rows(tr, XBC), _rows(tr, LANE), _rows(tr, 2 * D)],
                 [_sds((D, D), dt_), _sds((D, XBC), dt_), _sds((D, LANE), dt_), _sds((D, 2 * D), dt_)])(shards)


def _win_join(gz, gxbc, gdt, guv):
    tr = 256
    bounds = (0, D, D + XBC, D + XBC + 32, IN_W)

    def body(gz_ref, gx_ref, gd_ref, gu_ref, o_ref):
        segs = (gz_ref, gx_ref, gd_ref, gu_ref)
        for j in range(4):
            parts = []
            for k in range(4):
                a, b = max(bounds[k], j * IN_LOC), min(bounds[k + 1], (j + 1) * IN_LOC)
                if a < b:
                    parts.append(segs[k][:, a - bounds[k]:b - bounds[k]])
            full = parts[0] if len(parts) == 1 else jnp.concatenate(parts, axis=1)
            o_ref[j] = full.astype(jnp.bfloat16)

    return _call(body, "win_join", (D // tr,), [_rows(tr, D), _rows(tr, XBC), _rows(tr, LANE), _rows(tr, 2 * D)],
                 pl.BlockSpec((4, tr, IN_LOC), lambda i: (0, i, 0)), _sds((4, D, IN_LOC), jnp.bfloat16))(gz, gxbc, gdt, guv)


def _ctx_spec(nct):
    return pl.BlockSpec((T, D), lambda i: (jnp.minimum(i, nct - 1), 0))


def _hy1_fwd(ctx, x, mp2, wz, wuv, wxbc, wdt, lng, lnb, sgw, sgbt, nct):
    n_lat = x.shape[0]
    n_rows = ctx.shape[0] + n_lat

    def body(c_ref, x_ref, mp_ref, wz_ref, wuv_ref, wxbc_ref, wdt_ref, lng_ref, lnb_ref, sgw_ref, sgbt_ref,
             z_ref, uv_ref, xbcp_ref, dtr_ref, ysg_ref):
        hv = jnp.where(pl.program_id(0) < nct, c_ref[...], x_ref[...])
        u = _mod(hv, mp_ref[0:1], mp_ref[1:2], mp_ref[2:3]).astype(MXU)
        z_ref[...] = jnp.dot(u, wz_ref[...], preferred_element_type=F32)
        xbcp_ref[...] = jnp.dot(u, wxbc_ref[...], preferred_element_type=F32)
        dtr_ref[...] = jnp.dot(u, wdt_ref[...], preferred_element_type=F32)
        uv = jnp.dot(u, wuv_ref[...], preferred_element_type=F32)
        uv_ref[...] = uv
        gate = _gelu(uv[:, :D])
        vln = _ln(_gelu(uv[:, D:]), lng_ref[...], lnb_ref[...]).astype(MXU)
        for _, g, rs, ls in _sg_blocks():
            s = jnp.dot(sgw_ref[g], vln[rs, ls], preferred_element_type=F32) + sgbt_ref[:, g:g + 1]
            ysg_ref[rs, ls] = (gate[rs, ls] * s).astype(ACT)

    mspec = pl.BlockSpec((None, 8, D), lambda i: (jnp.where(i < nct, 0, 1), 0, 0))
    return _call(body, "hy1_fwd", (n_rows // T,),
                 [_ctx_spec(nct), _rows_lat(T, D, nct), mspec, _wfull(D, D), _wfull(D, 2 * D), _wfull(D, XBC), _wfull(D, LANE),
                  _full(1, D), _full(1, D), _full(SGG, Q, Q), _full(Q, LANE)],
                 [_rows(T, D), _rows(T, 2 * D), _rows(T, XBC), _rows(T, LANE), _rows_lat(T, D, nct)],
                 [_sds((n_rows, D)), _sds((n_rows, 2 * D)), _sds((n_rows, XBC)), _sds((n_rows, LANE)),
                  _sds((n_lat, D), ACT)])(ctx, x, mp2, wz, wuv, wxbc, wdt, lng, lnb, sgw, sgbt)


def _hy1_bwd(ctx, x, uv, dz, dxbcp, ddf, ddb, dysg, dres, mp2, wz, wuv, wxbc, wdt, lng, lnb, sgw, sgbt, nct):
    n_lat = dres.shape[0]
    n_rows = ctx.shape[0] + n_lat

    def body(c_ref, x_ref, uv_ref, dz_ref, dxbcp_ref, ddf_ref, ddb_ref, dysg_ref, dres_ref, mp_ref, wz_ref, wuv_ref,
             wxbc_ref, wdt_ref, lng_ref, lnb_ref, sgw_ref, sgbt_ref,
             dho_ref, ub_ref, duv_ref, ddt_ref, pg2_ref, pl_ref, dsgw_ref, dsgb_ref, dgate_s, dvln_s):
        i = pl.program_id(0)

        @pl.when(i == 0)
        def _():
            pg2_ref[...] = jnp.zeros_like(pg2_ref)
            pl_ref[...] = jnp.zeros_like(pl_ref)
            dsgw_ref[...] = jnp.zeros_like(dsgw_ref)
            dsgb_ref[...] = jnp.zeros_like(dsgb_ref)
        uv = uv_ref[...]

        def f_sg(ug, uvv, g_, b_):
            return _gelu(ug), _ln(_gelu(uvv), g_, b_)
        (gate, vln), vjp_sg = jax.vjp(f_sg, uv[:, :D], uv[:, D:], lng_ref[...], lnb_ref[...])
        vlnb = vln.astype(MXU)
        lane = lax.broadcasted_iota(jnp.int32, (Q, LANE), 1)
        dsgb = jnp.zeros((Q, LANE), F32)
        for _, g, rs, ls in _sg_blocks():
            s = jnp.dot(sgw_ref[g], vlnb[rs, ls], preferred_element_type=F32) + sgbt_ref[:, g:g + 1]
            dyb = dysg_ref[rs, ls]
            dgate_s[rs, ls] = dyb * s
            ds = dyb * gate[rs, ls]
            dvln_s[rs, ls] = _tn_dot(sgw_ref[g], ds)
            dsgw_ref[g] += _nt(ds, vlnb[rs, ls])
            dsgb = dsgb + jnp.where(lane == g, jnp.sum(ds, axis=1, keepdims=True), 0.0)
        dsgb_ref[...] += dsgb
        dug, duvv, dlng, dlnb = vjp_sg((dgate_s[...], dvln_s[...]))
        pl_ref[0:1] += dlng
        pl_ref[1:2] += dlnb
        duv_ref[:, :D] = dug.astype(ACT)
        duv_ref[:, D:] = duvv.astype(ACT)
        ddt = (ddf_ref[...] + ddb_ref[...]).astype(MXU)
        ddt_ref[...] = ddt.astype(ACT)
        du = (_nt(dz_ref[...], wz_ref[...]) + _nt(dug, wuv_ref[:, :D]) + _nt(duvv, wuv_ref[:, D:])
              + _nt(dxbcp_ref[...], wxbc_ref[...]) + _nt(ddt, wdt_ref[...]))
        hv = jnp.where(i < nct, c_ref[...], x_ref[...])
        u, vjp = jax.vjp(_mod, hv, mp_ref[0:1], mp_ref[1:2], mp_ref[2:3])
        ub_ref[...] = u.astype(ACT)
        dhn, dg, dsh, dsc = vjp(du)
        dho_ref[...] = dres_ref[...] + dhn
        is_ctx = i < nct
        for k, val in enumerate((dg, dsh, dsc)):
            pg2_ref[0, k:k + 1] += jnp.where(is_ctx, val, 0.0)
            pg2_ref[1, k:k + 1] += jnp.where(is_ctx, 0.0, val)

    mspec = pl.BlockSpec((None, 8, D), lambda i: (jnp.where(i < nct, 0, 1), 0, 0))
    return _call(body, "hy1_bwd", (n_rows // T,),
                 [_ctx_spec(nct), _rows_lat(T, D, nct), _rows(T, 2 * D), _rows(T, D), _rows(T, XBC), _rows(T, LANE),
                  _rows(T, LANE), _rows(T, D),
                  _rows_lat(T, D, nct), mspec, _wfull(D, D), _wfull(D, 2 * D), _wfull(D, XBC), _wfull(D, LANE),
                  _full(1, D), _full(1, D), _full(SGG, Q, Q), _full(Q, LANE)],
                 [_rows_lat(T, D, nct), _rows(T, D), _rows(T, 2 * D), _rows(T, LANE), _full(2, 8, D), _full(8, D),
                  _full(SGG, Q, Q), _full(Q, LANE)],
                 [_sds((n_lat, D)), _sds((n_rows, D), ACT), _sds((n_rows, 2 * D), ACT), _sds((n_rows, LANE), ACT),
                  _sds((2, 8, D)), _sds((8, D)), _sds((SGG, Q, Q)), _sds((Q, LANE))],
                 scratch=[pltpu.VMEM((T, D), F32), pltpu.VMEM((T, D), F32)],
                 )(ctx, x, uv, dz, dxbcp, ddf, ddb, dysg, dres, mp2, wz, wuv, wxbc, wdt, lng, lnb, sgw, sgbt)


def _seq_edges(i, nct, nt):
    return (i == 0) | (i == nct), (i == nct - 1) | (i == nt - 1)


def _cv5_fwd(xbcp, w, b, nct):
    n_rows = xbcp.shape[0]
    nt = n_rows // T
    hb = 8

    def body(p_ref, c_ref, n_ref, w_ref, b_ref, o_ref, ext):
        first, last = _seq_edges(pl.program_id(0), nct, nt)
        _fill_ext(ext, p_ref, c_ref, n_ref, hb, first, last)
        for c in range(XBC // LANE):
            lanes = slice(c * LANE, (c + 1) * LANE)
            o_ref[:, lanes] = _silu(_conv(ext, w_ref, SK, hb, lanes) + b_ref[:, lanes])

    hp, hn = _halo(XBC, hb, n_rows)
    return _call(body, "cv5_fwd", (nt,), [hp, _rows(T, XBC), hn, _full(8, XBC), _full(1, XBC)],
                 _rows(T, XBC), _sds((n_rows, XBC)), scratch=[pltpu.VMEM((T + 2 * hb, XBC), F32)])(xbcp, xbcp, xbcp, w, b)


def _cv5_bwd1(xbcp, dxf, dxb, w, b, nct):
    n_rows = xbcp.shape[0]
    nt = n_rows // T
    hb = 8

    def body(p_ref, c_ref, n_ref, dxf_ref, dxb_ref, w_ref, b_ref, o_ref, pg_ref, ext):
        i = pl.program_id(0)
        first, last = _seq_edges(i, nct, nt)
        _fill_ext(ext, p_ref, c_ref, n_ref, hb, first, last)

        @pl.when(i == 0)
        def _():
            pg_ref[...] = jnp.zeros_like(pg_ref)
        for c in range(XBC // LANE):
            lanes = slice(c * LANE, (c + 1) * LANE)
            cv = _conv(ext, w_ref, SK, hb, lanes) + b_ref[:, lanes]
            sg = jax.nn.sigmoid(cv)
            dcv = (dxf_ref[:, lanes] + dxb_ref[:, lanes]) * (sg * (1.0 + cv * (1.0 - sg)))
            o_ref[:, lanes] = dcv
            pg_ref[0:1, lanes] += _sum0(dcv)

    hp, hn = _halo(XBC, hb, n_rows)
    return _call(body, "cv5_bwd1", (nt,),
                 [hp, _rows(T, XBC), hn, _rows(T, XBC), _rows(T, XBC), _full(8, XBC), _full(1, XBC)],
                 [_rows(T, XBC), _full(8, XBC)], [_sds((n_rows, XBC)), _sds((8, XBC))],
                 scratch=[pltpu.VMEM((T + 2 * hb, XBC), F32)])(xbcp, xbcp, xbcp, dxf, dxb, w, b)


def _cv5_bwd2(dcv, xbcp, w, nct):
    n_rows = xbcp.shape[0]
    nt = n_rows // T
    hb = 8

    def body(dp_ref, dc_ref, dn_ref, xp_ref, xc_ref, xn_ref, w_ref, o_ref, dw_ref, dext, xext):
        i = pl.program_id(0)
        first, last = _seq_edges(i, nct, nt)
        _fill_ext(dext, dp_ref, dc_ref, dn_ref, hb, first, last)
        _fill_ext(xext, xp_ref, xc_ref, xn_ref, hb, first, last)

        @pl.when(i == 0)
        def _():
            dw_ref[...] = jnp.zeros_like(dw_ref)
        for c in range(XBC // LANE):
            lanes = slice(c * LANE, (c + 1) * LANE)
            o_ref[:, lanes] = _conv_tr(dext, w_ref, SK, hb, lanes).astype(ACT)
            _conv_dw(dw_ref, dc_ref, xext, SK, hb, lanes)

    hp, hn = _halo(XBC, hb, n_rows)
    return _call(body, "cv5_bwd2", (nt,),
                 [hp, _rows(T, XBC), hn, hp, _rows(T, XBC), hn, _full(8, XBC)],
                 [_rows(T, XBC), _full(8, XBC)], [_sds((n_rows, XBC), ACT), _sds((8, XBC))],
                 scratch=[pltpu.VMEM((T + 2 * hb, XBC), F32), pltpu.VMEM((T + 2 * hb, XBC), F32)],
                 )(dcv, dcv, dcv, xbcp, xbcp, xbcp, w)


def _scan_order(nc, ncc, rev):
    if not rev:
        return lambda s: s
    return lambda s: jnp.where(s < ncc, ncc - 1 - s, nc - 1 - (s - ncc))


def _ssd_prep(dtr, sp, rev):
    dt = jax.nn.softplus(dtr + sp[0:1])
    a_neg = -jnp.exp(sp[1:2])
    r = lax.broadcasted_iota(jnp.int32, (Q, Q), 0)
    c = lax.broadcasted_iota(jnp.int32, (Q, Q), 1)
    msk = (c >= r) if rev else (c <= r)
    tri = msk.astype(F32)
    acs = jnp.dot(tri, dt * a_neg, precision=HI, preferred_element_type=F32)
    last = 0 if rev else Q - 1
    return dt, a_neg, acs, msk, tri, last


def _pair_sel(arr, lo, m, lane_lt):
    h0 = lo + 2 * m
    return jnp.where(lane_lt, arr[:, h0:h0 + 1], arr[:, h0 + 1:h0 + 2])


def _head_lanes(row, lo, g):
    lane = lax.broadcasted_iota(jnp.int32, (1, 512), 1)
    out = jnp.zeros((1, 512), F32)
    for k in range(8):
        h = lo + 8 * g + k
        out = jnp.where((lane >= 64 * k) & (lane < 64 * (k + 1)), row[:, h:h + 1], out)
    return out


def _halves(v, lane_lt):
    return jnp.concatenate([jnp.where(lane_lt, v, 0.0), jnp.where(lane_lt, 0.0, v)], axis=0)


def _ssd_fwd(xbc, dtr, sp, ncc, rev):
    n_rows = xbc.shape[0]
    nc = n_rows // Q
    lo = 16 if rev else 0
    order = _scan_order(nc, ncc, rev)

    def body(x_ref, dtr_ref, sp_ref, y_ref, hin_ref, st):
        @pl.when(pl.program_id(0) == 0)
        def _():
            st[...] = jnp.zeros_like(st)
        dt, _, acs, msk, _, last = _ssd_prep(dtr_ref[...], sp_ref[...], rev)
        acs_t, dt_t = acs.T, dt.T
        eacs = jnp.exp(acs)
        eal = jnp.exp(acs[last:last + 1, :])
        tew = jnp.exp(acs[last:last + 1, :] - acs) * dt
        lane_lt = lax.broadcasted_iota(jnp.int32, (Q, LANE), 1) < 64
        for g in range(2):
            gl = slice(g * 512, (g + 1) * 512)
            bg = x_ref[:, 1024 + g * 128:1152 + g * 128]
            cg = x_ref[:, 1280 + g * 128:1408 + g * 128]
            s_g = _nt(cg, bg)
            h_t = st[:, gl]
            hin_ref[:, gl] = h_t
            yoff = _nn(cg, h_t)
            xw = []
            for mm in range(4):
                m = 4 * g + mm
                ls = slice(m * LANE, (m + 1) * LANE)
                x2 = x_ref[:, ls]
                ws = []
                for hh in range(2):
                    h = lo + 2 * m + hh
                    lm = jnp.exp(jnp.where(msk, acs[:, h:h + 1] - acs_t[h:h + 1, :], -jnp.inf))
                    ws.append(s_g * lm * dt_t[h:h + 1, :])
                y2 = _nn(jnp.concatenate(ws, axis=1), _halves(x2, lane_lt))
                y_ref[:, ls] = y2 + yoff[:, mm * LANE:(mm + 1) * LANE] * _pair_sel(eacs, lo, m, lane_lt)
                xw.append(x2 * _pair_sel(tew, lo, m, lane_lt))
            st[:, gl] = _head_lanes(eal, lo, g) * h_t + _tn_dot(bg, jnp.concatenate(xw, axis=1))

    return _call(body, "ssd_fwd_r" if rev else "ssd_fwd_f", (nc,),
                 [pl.BlockSpec((Q, XBC), lambda s: (order(s), 0)), pl.BlockSpec((Q, LANE), lambda s: (order(s), 0)),
                  _full(8, LANE)],
                 [pl.BlockSpec((Q, D), lambda s: (order(s), 0)), pl.BlockSpec((None, LANE, D), lambda s: (order(s), 0, 0))],
                 [_sds((n_rows, D)), _sds((nc, LANE, D))], scratch=[pltpu.VMEM((LANE, D), F32)])(xbc, dtr, sp)


def _ssd_bwd(xbc, dtr, dy, hin, sp, dl, eh, ncc, rev):
    n_rows = xbc.shape[0]
    nc = n_rows // Q
    lo = 16 if rev else 0
    fwd_order = _scan_order(nc, ncc, rev)
    order = lambda s: fwd_order(nc - 1 - s)
    with_skip = not rev

    def body(x_ref, dtr_ref, dy_ref, hin_ref, sp_ref, dl_ref, eh_ref, dx_ref, ddtr_ref, pg_ref, dst):
        @pl.when(pl.program_id(0) == 0)
        def _():
            dst[...] = jnp.zeros_like(dst)
            pg_ref[...] = jnp.zeros_like(pg_ref)
        dtr_v = dtr_ref[...]
        dt, a_neg, acs, msk, tri, last = _ssd_prep(dtr_v, sp_ref[...], rev)
        acs_t = acs.T
        r = lax.broadcasted_iota(jnp.int32, (Q, Q), 0)
        c = lax.broadcasted_iota(jnp.int32, (Q, Q), 1)
        msk_t = (c <= r) if rev else (c >= r)
        eacs = jnp.exp(acs)
        eal = jnp.exp(acs[last:last + 1, :])
        te = jnp.exp(acs[last:last + 1, :] - acs)
        lane = lax.broadcasted_iota(jnp.int32, (Q, LANE), 1)
        lane1 = lax.broadcasted_iota(jnp.int32, (1, LANE), 1)
        lane_lt = lane < 64
        dacs = jnp.zeros((Q, LANE), F32)
        ddt_x = jnp.zeros((Q, LANE), F32)
        dlast = jnp.zeros((1, LANE), F32)
        hs_rows = []
        sub16 = lax.broadcasted_iota(jnp.int32, (16, Q), 0)
        dacs_t = jnp.zeros((16, Q), F32)
        for g in range(2):
            gl = slice(g * 512, (g + 1) * 512)
            bg = x_ref[:, 1024 + g * 128:1152 + g * 128]
            cg = x_ref[:, 1280 + g * 128:1408 + g * 128]
            s_g = _nt(cg, bg)
            s_gt = _nt(bg, cg)
            h_t, dh_t = hin_ref[:, gl], dst[:, gl]
            bh = _nn(bg, dh_t)
            yoff = _nn(cg, h_t)
            d_s = jnp.zeros((Q, Q), F32)
            edy, exd = [], []
            for mm in range(4):
                m = 4 * g + mm
                ls = slice(m * LANE, (m + 1) * LANE)
                x2, dy2 = x_ref[:, ls], dy_ref[:, ls]
                bh2 = bh[:, mm * LANE:(mm + 1) * LANE]
                dtm, em, eam = (_pair_sel(v, lo, m, lane_lt) for v in (dt, te, eacs))
                xd2 = x2 * dtm
                lms, mts = [], []
                for hh in range(2):
                    h = lo + 2 * m + hh
                    col, row = acs[:, h:h + 1], acs_t[h:h + 1, :]
                    lms.append(jnp.exp(jnp.where(msk, col - row, -jnp.inf)))
                    mts.append(s_gt * jnp.exp(jnp.where(msk_t, row - col, -jnp.inf)))
                dy_st = _halves(dy2, lane_lt)
                dxd2 = em * bh2 + _nn(jnp.concatenate(mts, axis=1), dy_st)
                dm_st = _nt(dy_st, xd2)
                dmt_st = _nt(_halves(xd2, lane_lt), dy2)
                d_s = d_s + dm_st[:Q] * lms[0] + dm_st[Q:] * lms[1]
                v1, v2, v3 = dy2 * yoff[:, mm * LANE:(mm + 1) * LANE] * eam, dxd2 * x2, xd2 * bh2 * em
                for hh in range(2):
                    h = lo + 2 * m + hh
                    half = lane_lt == (hh == 0)
                    g_rows = _sum0(dmt_st[hh * Q:(hh + 1) * Q] * mts[hh]) - _sum0(dm_st[hh * Q:(hh + 1) * Q] * s_g * lms[hh])
                    dacs_t = jnp.where(sub16 == 2 * m + hh, g_rows, dacs_t)
                    r1 = jnp.sum(jnp.where(half, v1, 0.0), axis=1, keepdims=True)
                    r2 = jnp.sum(jnp.where(half, v2, 0.0), axis=1, keepdims=True)
                    r3 = jnp.sum(jnp.where(half, v3, 0.0), axis=1, keepdims=True)
                    dacs = dacs + jnp.where(lane == h, r1 - r3, 0.0)
                    ddt_x = ddt_x + jnp.where(lane == h, r2, 0.0)
                    dlast = dlast + jnp.where(lane1 == h, _sum0(r3), 0.0)
                dx2 = dxd2 * dtm
                if with_skip:
                    dx2 = dx2 + dl_ref[:, ls] * dy2
                dx_ref[:, ls] = dx2
                edy.append(eam * dy2)
                exd.append(em * xd2)
            edy, exd = jnp.concatenate(edy, axis=1), jnp.concatenate(exd, axis=1)
            hs_rows.append(_sum0(h_t * dh_t))
            dst[:, gl] = _head_lanes(eal, lo, g) * dh_t + _tn_dot(cg, edy)
            dx_ref[:, 1024 + g * 128:1152 + g * 128] = _tn_dot(d_s, cg) + _nt(exd, dh_t)
            dx_ref[:, 1280 + g * 128:1408 + g * 128] = _nn(d_s, bg) + _nt(edy, h_t)
        hs = jnp.broadcast_to(jnp.concatenate(hs_rows, axis=1), (8, D))
        hsum = jnp.dot(hs, eh_ref[...], precision=HI, preferred_element_type=F32)[0:1]
        dlast = dlast + eal * hsum
        dacs = dacs + jnp.concatenate([jnp.zeros((lo, Q), F32)] * (lo > 0) + [dacs_t, jnp.zeros((LANE - 16 - lo, Q), F32)],
                                      axis=0).T
        rowi = lax.broadcasted_iota(jnp.int32, (Q, LANE), 0)
        dacs = dacs + jnp.where(rowi == last, dlast, 0.0)
        da = lax.dot_general(tri, dacs, (((0,), (0,)), ((), ())), precision=HI, preferred_element_type=F32)
        ddt = ddt_x + da * a_neg
        mine = (lane >= lo) & (lane < lo + 16)
        ddtr = jnp.where(mine, ddt * jax.nn.sigmoid(dtr_v + sp_ref[0:1]), 0.0)
        ddtr_ref[...] = ddtr
        pg_ref[0:1] += _sum0(ddtr)
        pg_ref[1:2] += jnp.where(mine[0:1], _sum0(da * dt) * a_neg, 0.0)

    blk = lambda w_: pl.BlockSpec((Q, w_), lambda s: (order(s), 0))
    return _call(body, "ssd_bwd_r" if rev else "ssd_bwd_f", (nc,),
                 [blk(XBC), blk(LANE), blk(D), pl.BlockSpec((None, LANE, D), lambda s: (order(s), 0, 0)),
                  _full(8, LANE), _full(1, D), _full(D, LANE)],
                 [blk(XBC), blk(LANE), _full(8, LANE)],
                 [_sds((n_rows, XBC)), _sds((n_rows, LANE)), _sds((8, LANE))],
                 scratch=[pltpu.VMEM((LANE, D), F32)])(xbc, dtr, dy, hin, sp, dl, eh)


def _hy4_fwd(h, yf, yb, xbc, z, ysg, mp, dl, ng, wout, nct):
    n_rows = h.shape[0]

    def body(h_ref, yf_ref, yb_ref, xs_ref, z_ref, ysg_ref, mp_ref, dl_ref, ng_ref, wout_ref, hn_ref, yssd_ref, out_ref):
        ytot = yf_ref[...] + yb_ref[...] + dl_ref[...] * xs_ref[...]
        yssd = _gate_norm(ytot, z_ref[...], ng_ref[...]).astype(MXU)
        yssd_ref[...] = yssd.astype(ACT)
        out = (jnp.dot(yssd, wout_ref[0:D, :], preferred_element_type=F32)
               + jnp.dot(ysg_ref[...].astype(MXU), wout_ref[D:2 * D, :], preferred_element_type=F32))
        out_ref[...] = out
        hn_ref[...] = h_ref[...] + mp_ref[3:4] * out

    return _call(body, "hy4_fwd", (n_rows // T,),
                 [_rows(T, D), _rows(T, D, nct), _rows(T, D, nct), _rows(T, D, nct), _rows(T, D, nct), _rows(T, D),
                  _full(8, D), _full(1, D), _full(1, D), _wfull(2 * D, D)],
                 [_rows(T, D), _rows(T, D), _rows(T, D)],
                 [_sds((n_rows, D)), _sds((n_rows, D), ACT), _sds((n_rows, D))])(h, yf, yb, xbc, z, ysg, mp, dl, ng, wout)


def _hy4_bwd(dh, out, yf, yb, xbc, z, mp, dl, ng, wout, nct):
    n_lat = dh.shape[0]
    n_rows = yf.shape[0]

    def body(dh_ref, out_ref, yf_ref, yb_ref, xs_ref, z_ref, mp_ref, dl_ref, ng_ref, wout_ref,
             dy_ref, dz_ref, dysg_ref, doutb_ref, pg_ref):
        i = pl.program_id(0)

        @pl.when(i == 0)
        def _():
            pg_ref[...] = jnp.zeros_like(pg_ref)

        @pl.when(i < nct)
        def _():
            dy_ref[...] = jnp.zeros_like(dy_ref)
            dz_ref[...] = jnp.zeros_like(dz_ref)
            dysg_ref[...] = jnp.zeros_like(dysg_ref)
            doutb_ref[...] = jnp.zeros_like(doutb_ref)

        @pl.when(i >= nct)
        def _():
            dhp = dh_ref[...]
            doutb = (mp_ref[3:4] * dhp).astype(MXU)
            doutb_ref[...] = doutb.astype(ACT)
            dysg_ref[...] = _nt(doutb, wout_ref[D:2 * D, :])
            dyssd = _nt(doutb, wout_ref[0:D, :])
            xs = xs_ref[...]
            ytot = yf_ref[...] + yb_ref[...] + dl_ref[...] * xs
            _, vjp = jax.vjp(_gate_norm, ytot, z_ref[...], ng_ref[...])
            dytot, dz, dng = vjp(dyssd)
            dy_ref[...] = dytot
            dz_ref[...] = dz.astype(ACT)
            pg_ref[0:1] += _sum0(dhp * out_ref[...])
            pg_ref[1:2] += dng
            pg_ref[2:3] += _sum0(dytot * xs)

    return _call(body, "hy4_bwd", (n_rows // T,),
                 [_rows_lat(T, D, nct), _rows_lat(T, D, nct), _rows(T, D), _rows(T, D), _rows(T, D), _rows(T, D),
                  _full(8, D), _full(1, D), _full(1, D), _wfull(2 * D, D)],
                 [_rows(T, D), _rows(T, D), _rows(T, D), _rows_lat(T, D, nct), _full(8, D)],
                 [_sds((n_rows, D)), _sds((n_rows, D), ACT), _sds((n_rows, D)), _sds((n_lat, D), ACT), _sds((8, D))],
                 )(dh, out, yf, yb, xbc, z, mp, dl, ng, wout)


def _loss_bwd(h, tgt, fng):
    n_rows = h.shape[0]

    def body(h_ref, t_ref, g_ref, dh_ref, pg_ref, ls_ref):
        @pl.when(pl.program_id(0) == 0)
        def _():
            pg_ref[...] = jnp.zeros_like(pg_ref)
            ls_ref[...] = jnp.zeros_like(ls_ref)
        hv = h_ref[...]
        g = g_ref[...]
        r = lax.rsqrt(jnp.mean(hv * hv, axis=-1, keepdims=True) + EPS)
        n = hv * r
        e = n * g - t_ref[...]
        ls_ref[...] += 0.5 * jnp.sum(jnp.sum(e * e, axis=1, keepdims=True), axis=0, keepdims=True) * (1.0 / D)
        dyv = e * (1.0 / D)
        pg_ref[0:1] += _sum0(dyv * n)
        dn = dyv * g
        dh_ref[...] = r * (dn - n * jnp.mean(dn * n, axis=-1, keepdims=True))

    return _call(body, "loss_bwd", (n_rows // T,), [_rows(T, D), _rows(T, D), _full(1, D)],
                 [_rows(T, D), _full(8, D), _full(8, LANE)],
                 [_sds((n_rows, D)), _sds((8, D)), _sds((8, LANE))])(h, tgt, fng)


def _pad_rows(a, rows):
    return jnp.concatenate([a, jnp.zeros((rows - a.shape[0],) + a.shape[1:], a.dtype)], axis=0)


def _mp(*rows):
    return _pad_rows(jnp.stack(rows, axis=0), 8)


def _local_step(x, ctx, tgt, ada, cada0, w, late_w=None, early_grads=None, small_grads=None):
    n_lat, n_ctx = x.shape[0], ctx.shape[0]
    nct, ncc = n_ctx // T, n_ctx // Q
    a0 = [ada[0, k * D:(k + 1) * D] for k in range(6)]
    a1 = [ada[1, k * D:(k + 1) * D] for k in range(6)]
    c0 = [cada0[k * D:(k + 1) * D] for k in range(6)]
    g = {}

    mp2 = jnp.stack([_mp(w["norm_mix_g"][0], c0[0], c0[1]), _mp(w["norm_mix_g"][0], a0[0], a0[1], a0[2])], axis=0)
    mp_l0 = mp2[1]
    sgbt = _pad_cols(w["sg_b"][0].T, LANE)
    lng, lnb = w["sg_ln_g"][0][None], w["sg_ln_b"][0][None]
    z, uv, xbcp, dtr, ysg = _hy1_fwd(ctx, x, mp2, w["wz"], w["wuv"], w["wxbc"], w["wdt"], lng, lnb, w["sg_w"], sgbt, nct)
    cw = _pad_rows(w["ssd_conv_w"][0], 8)
    cb = w["ssd_conv_b"][0][None]
    xbc = _cv5_fwd(xbcp, cw, cb, nct)
    sp = _pad_rows(jnp.stack([_pad_cols(w["ssd_dt_bias"][0].reshape(1, 32), LANE)[0],
                              _pad_cols(w["ssd_a_log"][0].reshape(1, 32), LANE)[0]], axis=0), 8)
    dl = jnp.repeat(w["ssd_d"][0], 64)[None]
    ng = w["ssd_norm_g"][0][None]
    yf, hin_f = _ssd_fwd(xbc, dtr, sp, ncc, False)
    yb, hin_b = _ssd_fwd(xbc, dtr, sp, ncc, True)
    if late_w is not None:
        w = {**w, **late_w(yb)}
    h1, yssd, out0 = _hy4_fwd(x, yf, yb, xbc, z, ysg, mp_l0, dl, ng, w["hy_w_out"], nct)

    mpm0 = _mp(w["norm_mlp_g"][0], a0[3], a0[4], a0[5])
    h2, am0, ym0 = _mlp_fwd(h1, mpm0, w["wpack"], 0, "mlp0_fwd")

    mpc = _mp(w["norm_mix_g"][1], a1[0], a1[1], a1[2])
    wdw = _pad_rows(w["cf_w_dw"][0], 32)
    glu, acf = _cf1_fwd(h2, mpc, w["cf_w_pw1"], w["cf_b_pw1"])
    h3, cv, scf, ycf = _cf2_fwd(h2, glu, mpc, wdw, w["cf_b_dw"], w["cf_ln_g"], w["cf_ln_b"], w["cf_w_pw2"], w["cf_b_pw2"])

    mpm1 = _mp(w["norm_mlp_g"][1], a1[3], a1[4], a1[5])
    h4, am1, ym1 = _mlp_fwd(h3, mpm1, w["wpack"], 1, "mlp1_fwd")

    dh4, pg_f, ls = _loss_bwd(h4, tgt, w["final_norm_g"][None])
    loss = ls[0, 0]
    g["final_norm_g"] = pg_f[0]

    gp = {}
    dh3, da1, dy1, u1, pgm1 = _mlp_bwd(dh4, h3, am1, ym1, mpm1, w["wpack"], 1, "mlp1_bwd")
    gw1_1 = _tn(u1, da1, "tn_mlp1_w1", shard=("col", 1024))
    gw2_1 = _tn(am1, dy1, "tn_mlp1_w2", relu2=True, shard=("row", 1024))

    dcv, dycf, pgc2 = _cf2_bwd(dh3, ycf, cv, mpc, w["cf_ln_g"], w["cf_ln_b"], w["cf_w_pw2"])
    gp["cf_w_pw2"] = _tn(scf, dycf, "tn_cf_pw2", shard=("row", 256))
    dh2, dacf, ucf, pgc1, pbc1, dwdw = _cf1_bwd(dh3, h2, acf, dcv, glu, mpc, wdw, w["cf_w_pw1"])
    gp["cf_w_pw1"] = _tn(ucf, dacf, "tn_cf_pw1", shard=("col", 512)).reshape(4, 512, 1024)
    g["cf_b_pw2"], g["cf_ln_g"], g["cf_ln_b"], g["cf_b_dw"] = pgc2[1], pgc2[2], pgc2[3], pgc2[4]
    g["cf_b_pw1"] = pbc1[0]
    g["cf_w_dw"] = dwdw[:CK]

    dh1, da0, dy0, u0, pgm0 = _mlp_bwd(dh2, h1, am0, ym0, mpm0, w["wpack"], 0, "mlp0_bwd")
    gp["mlp_w1"] = jnp.concatenate([_tn(u0, da0, "tn_mlp0_w1", shard=("col", 1024)), gw1_1], axis=1)
    gp["mlp_w2"] = jnp.concatenate([_tn(am0, dy0, "tn_mlp0_w2", relu2=True, shard=("row", 1024)), gw2_1], axis=1)
    g["norm_mlp_g"] = jnp.stack([pgm0[0], pgm1[0]])

    dyt, dz, dysg, doutb, pg4 = _hy4_bwd(dh1, out0, yf, yb, xbc, z, mp_l0, dl, ng, w["hy_w_out"], nct)
    gp["hy_w_out"] = jnp.concatenate([_tn(yssd, doutb, "tn_out_ssd", shard=("row", 512)),
                                      _tn(ysg, doutb, "tn_out_sg", shard=("row", 512))], axis=0)
    if early_grads is not None:
        sp = sp + early_grads(gp)
    head_of_lane = jnp.arange(D, dtype=jnp.int32)[:, None] // 64
    col = jnp.arange(LANE, dtype=jnp.int32)[None, :]
    dxf, ddf, pgsf = _ssd_bwd(xbc, dtr, dyt, hin_f, sp, dl, (col == head_of_lane).astype(F32), ncc, False)
    dxb, ddb, pgsb = _ssd_bwd(xbc, dtr, dyt, hin_b, sp, dl, (col == head_of_lane + 16).astype(F32), ncc, True)
    dcv5, pgcb = _cv5_bwd1(xbcp, dxf, dxb, cw, cb, nct)
    dxbcp, dcw = _cv5_bwd2(dcv5, xbcp, cw, nct)
    dx, ucat, duv, ddt, pg2, pln, dsgw, dsgbt = _hy1_bwd(
        ctx, x, uv, dz, dxbcp, ddf, ddb, dysg, dh1, mp2, w["wz"], w["wuv"], w["wxbc"], w["wdt"], lng, lnb, w["sg_w"], sgbt, nct)
    g["ssd_conv_w"], g["ssd_conv_b"] = dcw[:SK], pgcb[0]
    pgs = pgsf + pgsb
    g["ssd_dt_bias"], g["ssd_a_log"] = pgs[0, :32].reshape(2, 16), pgs[1, :32].reshape(2, 16)
    g["ssd_d"] = jnp.sum(pg4[2].reshape(16, 64), axis=1)
    g["ssd_norm_g"] = pg4[1]
    g["sg_ln_g"], g["sg_ln_b"] = pln[0], pln[1]
    g["sg_w"], g["sg_b"] = dsgw, dsgbt[:, :SGG].T
    g["norm_mix_g"] = jnp.stack([pg2[0, 0] + pg2[1, 0], pgc1[0]])

    zero = jnp.zeros((D,), F32)
    d_ada = jnp.stack([jnp.concatenate([pg2[1, 1], pg2[1, 2], pg4[0], pgm0[1], pgm0[2], pgm0[3]]),
                       jnp.concatenate([pgc1[1], pgc1[2], pgc2[0], pgm1[1], pgm1[2], pgm1[3]])])
    d_cada0 = jnp.concatenate([pg2[0, 1], pg2[0, 2], zero, zero, zero, zero])
    if small_grads is not None:
        ucat, _ = lax.optimization_barrier((ucat, small_grads(g, d_ada, d_cada0)))
    gp["hy_w_in"] = _win_join(_tn(ucat, dz, "tn_in_z"), _tn(ucat, dxbcp, "tn_in_xbc"), _tn(ucat, ddt, "tn_in_dt"),
                              _tn(ucat, duv, "tn_in_uv"))
    g["pieces"] = gp
    return loss, dx, g, d_ada, d_cada0


def _pad_cols(a, cols):
    return jnp.concatenate([a, jnp.zeros(a.shape[:-1] + (cols - a.shape[-1],), a.dtype)], axis=-1)


MESH = pl.DeviceIdType.MESH
ANY = pl.BlockSpec(memory_space=pl.ANY)
IN_VMEM = pl.BlockSpec(memory_space=pltpu.VMEM)


def _coords():
    return lax.axis_index("x"), lax.axis_index("y"), lax.axis_index("c")


def _ag8(x, name):
    r, wd = x.shape

    def body(x_ref, o_ref, send, recv, lsem):
        mx, my, mc = _coords()
        me = 4 * mx + 2 * my + mc
        mine = pltpu.make_async_copy(x_ref, o_ref.at[me], lsem)
        mine.start()
        sent, peers = [], []
        for k in range(1, 8):
            px = 1 - mx if k & 4 else mx
            py = 1 - my if k & 2 else my
            pc = 1 - mc if k & 1 else mc
            cp = pltpu.make_async_remote_copy(src_ref=x_ref, dst_ref=o_ref.at[me], send_sem=send.at[k - 1],
                                              recv_sem=recv.at[k - 1], device_id=(px, py, pc), device_id_type=MESH)
            cp.start()
            sent.append(cp)
            peers.append((4 * px + 2 * py + pc, (px, py, pc)))
        for k in range(1, 8):
            slot, peer = peers[k - 1]
            pltpu.make_async_remote_copy(src_ref=x_ref, dst_ref=o_ref.at[slot], send_sem=send.at[k - 1],
                                         recv_sem=recv.at[k - 1], device_id=peer, device_id_type=MESH).wait_recv()
        for cp in sent:
            cp.wait_send()
        mine.wait()

    return pl.pallas_call(
        body, name=name, out_shape=_sds((8, r, wd), x.dtype), in_specs=[IN_VMEM], out_specs=IN_VMEM,
        scratch_shapes=[pltpu.SemaphoreType.DMA((7,)), pltpu.SemaphoreType.DMA((7,)), pltpu.SemaphoreType.DMA(())],
        compiler_params=pltpu.CompilerParams(vmem_limit_bytes=VMEM_LIMIT))(x)


HBM = pl.BlockSpec(memory_space=pltpu.HBM)
SEM = pl.BlockSpec(memory_space=pltpu.SEMAPHORE)
EFFECT = pltpu.SideEffectType.DATAFLOW_SIDE_EFFECTING


def _x4_peers(in_ref, land_ref, send, recv, a2a):
    mx, my, mc = _coords()
    me = 2 * mx + my
    out = []
    for k in range(1, 4):
        px = 1 - mx if k & 2 else mx
        py = 1 - my if k & 1 else my
        pj = 2 * px + py
        mk = functools.partial(pltpu.make_async_remote_copy, src_ref=in_ref.at[pj] if a2a else in_ref,
                               send_sem=send.at[k - 1], recv_sem=recv.at[k - 1], device_id=(px, py, mc), device_id_type=MESH)
        out.append((mk(dst_ref=land_ref.at[me]), mk(dst_ref=land_ref.at[pj])))
    return out


def _x4_start(buf, name, a2a):
    r, wd = buf.shape[-2:]

    def body(in_ref, land_ref, send, recv, in_thru, land_thru, token):
        for start, _ in _x4_peers(in_ref, land_ref, send, recv, a2a):
            start.start()
        token[...] = jnp.zeros_like(token)

    land = lax.empty((4, r, wd), buf.dtype)
    return pl.pallas_call(
        body, name=name,
        out_shape=(pltpu.SemaphoreType.DMA((3,)), pltpu.SemaphoreType.DMA((3,)), pltpu.HBM(buf.shape, buf.dtype),
                   pltpu.HBM(land.shape, land.dtype), _sds((8, LANE))),
        in_specs=(HBM, HBM), out_specs=(SEM, SEM, HBM, HBM, IN_VMEM), input_output_aliases={0: 2, 1: 3},
        compiler_params=pltpu.CompilerParams(has_side_effects=EFFECT),
    )(pltpu.with_memory_space_constraint(buf, pltpu.HBM), pltpu.with_memory_space_constraint(land, pltpu.HBM))


def _x4_wait(send, recv, buf_thru, land_thru, after, name, a2a):
    def body(in_ref, land_ref, send_ref, recv_ref, after_ref, in_dead, got_ref):
        for _, arrive in _x4_peers(in_ref, land_ref, send_ref, recv_ref, a2a):
            arrive.wait_send()
            arrive.wait_recv()

    return pl.pallas_call(
        body, name=name, out_shape=(pltpu.HBM(buf_thru.shape, buf_thru.dtype), pltpu.HBM(land_thru.shape, land_thru.dtype)),
        in_specs=(HBM, HBM, SEM, SEM, ANY), out_specs=(HBM, HBM), input_output_aliases={0: 0, 1: 1},
        compiler_params=pltpu.CompilerParams(has_side_effects=EFFECT),
    )(buf_thru, land_thru, send, recv, after)


def _ag8_peers(x_ref, land_ref, send, recv):
    mx, my, mc = _coords()
    me = 4 * mx + 2 * my + mc
    out = []
    for k in range(1, 8):
        px = 1 - mx if k & 4 else mx
        py = 1 - my if k & 2 else my
        pc = 1 - mc if k & 1 else mc
        mk = functools.partial(pltpu.make_async_remote_copy, src_ref=x_ref, send_sem=send.at[k - 1], recv_sem=recv.at[k - 1],
                               device_id=(px, py, pc), device_id_type=MESH)
        out.append((mk(dst_ref=land_ref.at[me]), mk(dst_ref=land_ref.at[4 * px + 2 * py + pc])))
    return out


def _split_start(x, land_shape, peers, n_copies, name):
    def body(x_ref, land_ref, send, recv, x_thru, land_thru, token):
        for start, _ in peers(x_ref, land_ref, send, recv):
            start.start()
        token[...] = jnp.zeros_like(token)

    land = lax.empty(land_shape, x.dtype)
    return pl.pallas_call(
        body, name=name,
        out_shape=(pltpu.SemaphoreType.DMA((n_copies,)), pltpu.SemaphoreType.DMA((n_copies,)), pltpu.HBM(x.shape, x.dtype),
                   pltpu.HBM(land.shape, land.dtype), _sds((8, LANE))),
        in_specs=(HBM, HBM), out_specs=(SEM, SEM, HBM, HBM, IN_VMEM), input_output_aliases={0: 2, 1: 3},
        compiler_params=pltpu.CompilerParams(has_side_effects=EFFECT),
    )(pltpu.with_memory_space_constraint(x, pltpu.HBM), pltpu.with_memory_space_constraint(land, pltpu.HBM))


def _split_wait(handle, after, peers, name):
    send, recv, x_thru, land_thru, _ = handle

    def body(x_ref, land_ref, send_ref, recv_ref, after_ref, x_dead, got_ref):
        for _, arrive in peers(x_ref, land_ref, send_ref, recv_ref):
            arrive.wait_send()
            arrive.wait_recv()

    return pl.pallas_call(
        body, name=name, out_shape=(pltpu.HBM(x_thru.shape, x_thru.dtype), pltpu.HBM(land_thru.shape, land_thru.dtype)),
        in_specs=(HBM, HBM, SEM, SEM, ANY), out_specs=(HBM, HBM), input_output_aliases={0: 0, 1: 1},
        compiler_params=pltpu.CompilerParams(has_side_effects=EFFECT),
    )(x_thru, land_thru, send, recv, after)


def _sib_peers(x_ref, land_ref, send, recv):
    mx, my, mc = _coords()
    cp = pltpu.make_async_remote_copy(src_ref=x_ref, dst_ref=land_ref, send_sem=send.at[0], recv_sem=recv.at[0],
                                      device_id=(mx, my, 1 - mc), device_id_type=MESH)
    return [(cp, cp)]


def _xchg_sib(x, name):
    def body(in_ref, o_ref, send, recv):
        mx, my, mc = _coords()
        cp = pltpu.make_async_remote_copy(src_ref=in_ref, dst_ref=o_ref, send_sem=send, recv_sem=recv,
                                          device_id=(mx, my, 1 - mc), device_id_type=MESH)
        cp.start()
        cp.wait_recv()
        cp.wait_send()

    return pl.pallas_call(
        body, name=name, out_shape=_sds(x.shape, x.dtype), in_specs=[ANY], out_specs=ANY,
        scratch_shapes=[pltpu.SemaphoreType.DMA(()), pltpu.SemaphoreType.DMA(())])(x)


def _sum_slots(gat, slots, name, tr=None):
    n, r, wd = gat.shape
    tr = r if tr is None else tr

    def body(g_ref, o_ref):
        acc = g_ref[slots[0]].astype(F32)
        for s in slots[1:]:
            acc = acc + g_ref[s].astype(F32)
        o_ref[...] = acc

    return _call(body, name, (r // tr,), [pl.BlockSpec((n, tr, wd), lambda i: (0, i, 0))], _rows(tr, wd), _sds((r, wd)))(gat)


def _add(a, b, name, tr):
    def body(a_ref, b_ref, o_ref):
        o_ref[...] = a_ref[...] + b_ref[...]

    r, wd = a.shape
    return _call(body, name, (r // tr,), [_rows(tr, wd), _rows(tr, wd)], _rows(tr, wd), _sds((r, wd)))(a, b)


def _ada_fwd(x16, ada_w_loc, ada_b_loc):
    nloc = ada_w_loc.shape[-1]

    def body(x_ref, w_ref, b_ref, s_ref, o_ref):
        s = _silu(x_ref[...])
        s_ref[...] = s
        o_ref[...] = jnp.dot(s, w_ref[...], precision=HI, preferred_element_type=F32) + b_ref[...]

    return _call(body, "ada_fwd", (2,),
                 [_full(16, D), pl.BlockSpec((None, D, nloc), lambda l: (l, 0, 0)), pl.BlockSpec((None, 1, nloc), lambda l: (l, 0, 0))],
                 [_full(16, D), pl.BlockSpec((None, 16, nloc), lambda l: (l, 0, 0))],
                 [_sds((16, D)), _sds((2, 16, nloc))])(x16, ada_w_loc, ada_b_loc[:, None, :])


def _ada_bwd(s16, d_loc, ada_w_loc):
    nloc = ada_w_loc.shape[-1]

    def body(s_ref, d_ref, w_ref, gw_ref, cp_ref):
        gw_ref[...] = lax.dot_general(s_ref[...], d_ref[...], (((0,), (0,)), ((), ())), precision=HI,
                                      preferred_element_type=F32)

        @pl.when(pl.program_id(0) == 0)
        def _():
            cp_ref[...] = lax.dot_general(d_ref[8:16, :], w_ref[...], (((1,), (1,)), ((), ())), precision=HI,
                                          preferred_element_type=F32)

    return _call(body, "ada_bwd", (2,),
                 [_full(16, D), pl.BlockSpec((None, 16, nloc), lambda l: (l, 0, 0)), pl.BlockSpec((None, D, nloc), lambda l: (l, 0, 0))],
                 [pl.BlockSpec((None, D, nloc), lambda l: (l, 0, 0)), _full(8, D)],
                 [_sds((2, D, nloc)), _sds((8, D))])(s16, d_loc, ada_w_loc)


def _cctx_grad(dscc, c_ctx):
    def body(d_ref, c_ref, o_ref):
        _, vjp = jax.vjp(_silu, c_ref[...])
        o_ref[...] = vjp(d_ref[...])[0]

    return _call(body, "cctx_grad", (1,), [_full(8, D), _full(8, D)], _full(8, D), _sds((8, D)))(dscc, c_ctx)


def _adamw_math(w, g, m, v):
    mn = ADAM_B1 * m + (1.0 - ADAM_B1) * g
    vn = ADAM_B2 * v + (1.0 - ADAM_B2) * jnp.square(g)
    c1 = 1.0 - ADAM_B1 ** ADAM_STEP
    c2 = 1.0 - ADAM_B2 ** ADAM_STEP
    return -ADAM_LR * ((mn / c1) / (jnp.sqrt(vn / c2) + ADAM_EPS) + ADAM_WD * w), mn, vn


def _adamw(w, g, m, v, name):
    n_l, r, wd = w.shape
    tr = 256 if r % 256 == 0 else r

    def body(w_ref, g_ref, m_ref, v_ref, d_ref, mo_ref, vo_ref):
        d_ref[...], mo_ref[...], vo_ref[...] = _adamw_math(w_ref[...], g_ref[...], m_ref[...], v_ref[...])

    spec = pl.BlockSpec((None, tr, wd), lambda a, i: (a, i, 0))
    return tuple(_call(body, name, (n_l, r // tr), [spec] * 4, [spec] * 3, [_sds(w.shape)] * 3)(w, g, m, v))


def _adamw_rows(w, part, sib, m, v, r0, name):
    rows = w.shape[0]
    tr = 256

    def body(w_ref, p_ref, s_ref, m_ref, v_ref, g_ref, d_ref, mo_ref, vo_ref):
        g = p_ref[...] + s_ref[...]
        g_ref[...] = g
        d_ref[...], mo_ref[...], vo_ref[...] = _adamw_math(w_ref[...], g, m_ref[...], v_ref[...])

    here, there = _rows(tr, ROW), _rows(tr, ROW, r0 // tr)
    return tuple(_call(body, name, (rows // tr,), [here, there, there, here, here], [here] * 4, [_sds(w.shape)] * 4)(
        w, part, sib, m, v))


def _adamw_small(ws, gs, ms, vs, name):
    n = len(ws)
    shapes = [a.shape for a in ws]
    as2d = lambda a: a.reshape(-1, a.shape[-1])

    def body(*refs):
        ins, outs = refs[:4 * n], refs[4 * n:]
        for k in range(n):
            res = _adamw_math(ins[k][...], ins[n + k][...], ins[2 * n + k][...], ins[3 * n + k][...])
            for j in range(3):
                outs[j * n + k][...] = res[j]

    flat = [as2d(a) for group in (ws, gs, ms, vs) for a in group]
    specs = [_full(*a.shape) for a in flat]
    outs = _call(body, name, (1,), specs, specs[:n] * 3, [_sds(a.shape) for a in flat[:n]] * 3)(*flat)
    return tuple([outs[j * n + k].reshape(shapes[k]) for k in range(n)] for j in range(3))


ROW = 1024


def _nrows(size):
    return -(-size // ROW)


def _pack(arrs, rows_total, dtype=F32):
    parts = []
    for a in arrs:
        flat = a.reshape(-1).astype(dtype)
        pad = _nrows(flat.shape[0]) * ROW - flat.shape[0]
        parts.append(flat if pad == 0 else jnp.concatenate([flat, jnp.zeros((pad,), dtype)]))
    flat = jnp.concatenate(parts)
    out = flat.reshape(-1, ROW)
    return _pad_rows(out, rows_total)


def _unpack(buf, shapes):
    lead = buf.shape[:-2]
    out, r0 = [], 0
    for shp in shapes:
        size = 1
        for s in shp:
            size *= s
        nr = _nrows(size)
        piece = lax.slice_in_dim(buf, r0, r0 + nr, axis=len(lead))
        out.append(piece.reshape(lead + (nr * ROW,))[..., :size].reshape(lead + tuple(shp)))
        r0 += nr
    return out


SLOT = 16


def _slot_rows(size):
    return _round_up(size // ROW, SLOT)


def _pack_rows(arrs, rows_total, dtype):
    parts, used = [], 0
    for a in arrs:
        part = a.astype(dtype).reshape(-1, ROW)
        extra = _slot_rows(a.size) - part.shape[0]
        parts.append(part if extra == 0 else jnp.pad(part, ((0, extra), (0, 0))))
        used += _slot_rows(a.size)
    if rows_total > used:
        parts.append(jnp.zeros((rows_total - used, ROW), dtype))
    return jnp.concatenate(parts, axis=0)


def _unpack_rows(buf, shapes):
    lead = buf.shape[:-2]
    out, r0 = [], 0
    for shp in shapes:
        size = 1
        for s in shp:
            size *= s
        piece = lax.slice_in_dim(buf, r0, r0 + size // ROW, axis=len(lead))
        out.append(piece.reshape(lead + tuple(shp)))
        r0 += _slot_rows(size)
    return out


def _round_up(n, k):
    return -(-n // k) * k


WEIGHTS = ['c_ctx', 'ada_w', 'ada_b', 'norm_mix_g', 'norm_mlp_g', 'mlp_w1', 'mlp_w2', 'hy_w_in', 'ssd_conv_w', 'ssd_conv_b',
           'ssd_dt_bias', 'ssd_a_log', 'ssd_d', 'ssd_norm_g', 'sg_ln_g', 'sg_ln_b', 'sg_w', 'sg_b', 'hy_w_out', 'cf_w_pw1',
           'cf_b_pw1', 'cf_w_dw', 'cf_b_dw', 'cf_ln_g', 'cf_ln_b', 'cf_w_pw2', 'cf_b_pw2', 'final_norm_g']
BIG = {'mlp_w1': 2, 'mlp_w2': 1, 'hy_w_in': 2, 'hy_w_out': 1, 'cf_w_pw1': 2, 'cf_w_pw2': 1}
SMALL_SHARD = ['ssd_conv_w', 'cf_b_pw1', 'cf_w_dw', 'cf_b_dw', 'cf_ln_g', 'cf_ln_b', 'cf_b_pw2']
REP = ['norm_mix_g', 'norm_mlp_g', 'ssd_conv_b', 'ssd_dt_bias', 'ssd_a_log', 'ssd_d', 'ssd_norm_g', 'sg_ln_g', 'sg_ln_b',
       'sg_w', 'sg_b', 'final_norm_g']


def _gather_shards(stacked, axis):
    return jnp.concatenate([stacked[j] for j in range(4)], axis=axis)


def kernel(x, c, ctx, c_ctx, ada_w, ada_b, norm_mix_g, norm_mlp_g, mlp_w1, mlp_w2, hy_w_in, ssd_conv_w, ssd_conv_b, ssd_dt_bias, ssd_a_log, ssd_d, ssd_norm_g, sg_ln_g, sg_ln_b, sg_w, sg_b, hy_w_out, cf_w_pw1, cf_b_pw1, cf_w_dw, cf_b_dw, cf_ln_g, cf_ln_b, cf_w_pw2, cf_b_pw2, final_norm_g, loss_target, m_c_ctx, m_ada_w, m_ada_b, m_norm_mix_g, m_norm_mlp_g, m_mlp_w1, m_mlp_w2, m_hy_w_in, m_ssd_conv_w, m_ssd_conv_b, m_ssd_dt_bias, m_ssd_a_log, m_ssd_d, m_ssd_norm_g, m_sg_ln_g, m_sg_ln_b, m_sg_w, m_sg_b, m_hy_w_out, m_cf_w_pw1, m_cf_b_pw1, m_cf_w_dw, m_cf_b_dw, m_cf_ln_g, m_cf_ln_b, m_cf_w_pw2, m_cf_b_pw2, m_final_norm_g, v_c_ctx, v_ada_w, v_ada_b, v_norm_mix_g, v_norm_mlp_g, v_mlp_w1, v_mlp_w2, v_hy_w_in, v_ssd_conv_w, v_ssd_conv_b, v_ssd_dt_bias, v_ssd_a_log, v_ssd_d, v_ssd_norm_g, v_sg_ln_g, v_sg_ln_b, v_sg_w, v_sg_b, v_hy_w_out, v_cf_w_pw1, v_cf_b_pw1, v_cf_w_dw, v_cf_b_dw, v_cf_ln_g, v_cf_ln_b, v_cf_w_pw2, v_cf_b_pw2, v_final_norm_g):
    args = locals()
    wl = {n: args[n] for n in WEIGHTS}
    ml = {n: args["m_" + n] for n in WEIGHTS}
    vl = {n: args["v_" + n] for n in WEIGHTS}
    mx, my, mc = _coords()
    me = 4 * mx + 2 * my + mc
    shard = 2 * mx + my
    even = (0, 2, 4, 6)

    def start_gather(names, name, tie=None):
        rows = sum(_slot_rows(wl[n].size) for n in names)
        buf = _pack_rows([wl[n] for n in names], rows, MXU)
        if tie is not None:
            buf, _ = lax.optimization_barrier((buf, tie))
        return _x4_start(buf, name, a2a=False)

    def finish_gather(handle, names, after, name):
        send, recv, own, land, _ = handle
        own, land = _x4_wait(send, recv, own, land, after, name, a2a=False)
        got = lax.dynamic_update_slice(land, own[None], (shard, 0, 0))
        shapes = [wl[n].shape for n in names]
        wfull = {n: _gather_shards(st, BIG[n]) for n, st in zip(names, _unpack_rows(got, shapes))}
        return wfull, got

    rest_names = ["mlp_w1", "mlp_w2", "hy_w_out", "cf_w_pw1", "cf_w_pw2"]
    h_in = _x4_start(hy_w_in[0].astype(MXU), "agw_in_start", a2a=False)
    c = c + h_in[4][0, 0]

    small_shapes = [wl[n].shape for n in SMALL_SHARD]
    blk1 = _pack([c] + [wl[n] for n in SMALL_SHARD], 24)
    got1 = _ag8(blk1, "ag_cond")
    x16 = _pad_rows(jnp.concatenate([got1[:, 0, :], c_ctx[None]], axis=0), 16)
    small_full = {}
    for n, parts in zip(SMALL_SHARD, _unpack(got1[:, 1:, :], small_shapes)):
        small_full[n] = jnp.concatenate([parts[s] for s in even], axis=-1)

    nloc = ada_w.shape[-1]
    ada_b_loc = lax.dynamic_slice_in_dim(ada_b, shard * nloc, nloc, axis=1)
    s16, ada_loc = _ada_fwd(x16, ada_w, ada_b_loc)
    got2 = _ag8(ada_loc.reshape(32, nloc), "ag_ada").reshape(8, 2, 16, nloc)
    ada_full = jnp.concatenate([got2[s] for s in even], axis=-1)
    ada_me = lax.dynamic_slice_in_dim(ada_full, me, 1, axis=1)[:, 0, :]
    cada0 = ada_full[0, 8, :]

    w = {n: wl[n] for n in WEIGHTS if n not in BIG and n not in SMALL_SHARD}
    w.update(small_full)
    h_rest = start_gather(rest_names, "agw_rest_start", tie=ada_me)
    send, recv, own, land, _ = h_in
    own, land = _x4_wait(send, recv, own, land, h_rest[4], "agw_in_wait", a2a=False)
    w["wz"], w["wxbc"], w["wdt"], w["wuv"] = _win_split(lax.dynamic_update_slice(land, own[None], (shard, 0, 0)))
    w["sg_w"] = sg_w[0].astype(MXU)

    def late_w(after):
        wfull, got = finish_gather(h_rest, rest_names, after, "agw_rest_wait")
        return {"hy_w_out": wfull["hy_w_out"][0], "wpack": got,
                "cf_w_pw1": wfull["cf_w_pw1"][0], "cf_w_pw2": wfull["cf_w_pw2"][0]}

    full_shape = {n: wl[n].shape for n in WEIGHTS}
    for n in BIG:
        full_shape[n] = tuple(s * 4 if a == BIG[n] else s for a, s in enumerate(wl[n].shape))
    for n in SMALL_SHARD:
        full_shape[n] = wl[n].shape[:-1] + (wl[n].shape[-1] * 4,)

    early_names = ["mlp_w1", "mlp_w2", "cf_w_pw1", "cf_w_pw2", "hy_w_out"]
    early = {}

    def early_grads(gp):
        used = sum(gp[n].shape[1] for n in early_names)
        parts = [gp[n] for n in early_names] + [jnp.zeros((4, _round_up(used, 512) - used, ROW), jnp.bfloat16)]
        early["h"] = _x4_start(jnp.concatenate(parts, axis=1), "a2a_early_start", a2a=True)
        return early["h"][4][0, 0]

    sm_names = REP + SMALL_SHARD
    r_ada = sum(_nrows(wl[n].size * (4 if n in SMALL_SHARD else 1)) for n in sm_names)
    small = {}

    def small_grads(g_, d_ada_, d_cada0_):
        buf = _pack([g_[n] for n in sm_names] + [d_ada_, d_cada0_], _round_up(r_ada + 18, 16), jnp.bfloat16)
        small["h"] = _split_start(buf, (8,) + buf.shape, _ag8_peers, 7, "ag_small_start")
        return small["h"][4][0, 0]

    loss_part, dx, g, d_ada, d_cada0 = _local_step(x[0], ctx[0], loss_target[0], ada_me, cada0, w, late_w, early_grads,
                                                   small_grads)
    gp_last = g["pieces"]["hy_w_in"]

    delta, new_m, new_v, grads = {}, {}, {}, {}
    h_last = _x4_start(gp_last, "a2a_last_start", a2a=True)
    send, recv, own, land, _ = early["h"]
    own, land = _x4_wait(send, recv, own, land, h_last[4], "a2a_early_wait", a2a=True)
    got = lax.dynamic_update_slice(land, lax.dynamic_slice_in_dim(own, shard, 1, axis=0), (shard, 0, 0))
    part_early = _sum_slots(got, (0, 1, 2, 3), "sum_grads_early", tr=512)
    h_swap = _split_start(part_early, part_early.shape, _sib_peers, 1, "swap_early_start")

    small_in, land3 = _split_wait(small["h"], h_swap[4], _ag8_peers, "ag_small_wait")
    got3 = lax.dynamic_update_slice(land3, small_in[None], (me, 0, 0))
    tot3 = _sum_slots(got3, tuple(range(8)), "sum_small")
    sm_tot = _unpack(tot3, [full_shape[n] for n in sm_names] + [(2, 6 * D), (6 * D,)])
    grads.update(zip(sm_names, sm_tot[:-2]))
    for n in SMALL_SHARD:
        k = wl[n].shape[-1]
        grads[n] = lax.dynamic_slice_in_dim(grads[n], shard * k, k, axis=grads[n].ndim - 1)
    dada_tot, dcada_tot = sm_tot[-2], sm_tot[-1]
    grads["ada_b"] = dada_tot.at[0].add(dcada_tot)
    dada_all = got3[:, r_ada:r_ada + 12, :].astype(F32).reshape(8, 2, 6 * D)
    d16 = jnp.concatenate([jnp.transpose(dada_all, (1, 0, 2)),
                           jnp.stack([dcada_tot, jnp.zeros_like(dcada_tot)])[:, None, :],
                           jnp.zeros((2, 7, 6 * D), F32)], axis=1)
    d_loc = lax.dynamic_slice_in_dim(d16, shard * nloc, nloc, axis=2)
    grads["ada_w"], cpart = _ada_bwd(s16, d_loc, ada_w)
    cpart = lax.dynamic_update_slice(cpart, jnp.full((1, D), loss_part, F32), (1, 0))
    h_cctx = _split_start(cpart, (8,) + cpart.shape, _ag8_peers, 7, "ag_cctx_start")
    grads["ada_w"], _ = lax.optimization_barrier((grads["ada_w"], h_cctx[4]))

    delta["ada_w"], new_m["ada_w"], new_v["ada_w"] = _adamw(ada_w, grads["ada_w"], ml["ada_w"], vl["ada_w"], "adamw_ada_w")
    small_names = ["ada_b"] + REP + SMALL_SHARD
    outs = _adamw_small(*([src[n].reshape(wl[n].shape) for n in small_names] for src in (wl, grads, ml, vl)), "adamw_small")
    for dst, vals in zip((delta, new_m, new_v), outs):
        dst.update(zip(small_names, vals))

    def step(n, tot):
        grads[n] = tot.reshape(wl[n].shape)
        delta[n], new_m[n], new_v[n] = _adamw(wl[n], grads[n], ml[n], vl[n], "adamw_" + n)

    part_early, sib_early = _split_wait(h_swap, delta["ada_w"], _sib_peers, "swap_early_wait")
    r0 = 0
    for n in early_names:
        nr, shp = _slot_rows(wl[n].size), wl[n].shape
        if shp[-1] == ROW:
            outs = _adamw_rows(wl[n].reshape(-1, ROW), part_early, sib_early, ml[n].reshape(-1, ROW), vl[n].reshape(-1, ROW),
                               r0, "adamw_" + n)
            grads[n], delta[n], new_m[n], new_v[n] = (o.reshape(shp) for o in outs)
        else:
            step(n, _add(lax.slice_in_dim(part_early, r0, r0 + nr), lax.slice_in_dim(sib_early, r0, r0 + nr), "add_" + n, nr))
        r0 += nr
    send, recv, own, land, _ = h_last
    own, land = _x4_wait(send, recv, own, land, delta["mlp_w1"], "a2a_last_wait", a2a=True)
    got = lax.dynamic_update_slice(land, lax.dynamic_slice_in_dim(own, shard, 1, axis=0), (shard, 0, 0))
    part_last = _sum_slots(got, (0, 1, 2, 3), "sum_grads_last", tr=512)
    step("hy_w_in", _add(part_last, _xchg_sib(part_last, "swap_grads_last"), "add_grads_last", 512))

    c_in, land4 = _split_wait(h_cctx, delta["hy_w_in"], _ag8_peers, "ag_cctx_wait")
    got4 = lax.dynamic_update_slice(land4, c_in[None], (me, 0, 0))
    loss = _sum_slots(got4, tuple(range(8)), "sum_loss")[1, 0]
    grads["c_ctx"] = _cctx_grad(_sum_slots(got4, even, "sum_cctx"), _pad_rows(c_ctx[None], 8))[0]
    outs = _adamw_small(*([src["c_ctx"]] for src in (wl, grads, ml, vl)), "adamw_c_ctx")
    delta["c_ctx"], new_m["c_ctx"], new_v["c_ctx"] = (o[0] for o in outs)

    return (loss, dx[None], *[grads[n].reshape(wl[n].shape) for n in WEIGHTS], *[delta[n] for n in WEIGHTS],
            *[new_m[n] for n in WEIGHTS], *[new_v[n] for n in WEIGHTS])
```

```python
import functools

import jax
import jax.numpy as jnp
from jax import lax
from jax.experimental import pallas as pl
from jax.experimental.pallas import tpu as pltpu

F32 = jnp.float32
MXU = jnp.bfloat16
ACT = jnp.bfloat16
HI = lax.Precision.HIGHEST
EPS = 1e-6

D = 1024
HID = 4096
XBC = 1536
Q = 128
SGG = 8
CK = 31
SK = 5
T = 256
LANE = 128
VMEM_LIMIT = 56 * 1024 * 1024

ADAM_LR, ADAM_B1, ADAM_B2, ADAM_EPS, ADAM_WD, ADAM_STEP = 0.001, 0.9, 0.999, 1e-08, 0.01, 10


def _call(body, name, grid, in_specs, out_specs, out_shape, scratch=()):
    return pl.pallas_call(
        body, name=name, grid=grid, in_specs=in_specs, out_specs=out_specs, out_shape=out_shape,
        scratch_shapes=list(scratch),
        compiler_params=pltpu.CompilerParams(dimension_semantics=("arbitrary",) * len(grid),
                                             vmem_limit_bytes=VMEM_LIMIT))


def _sds(shape, dt=F32):
    return jax.ShapeDtypeStruct(tuple(shape), dt)


def _rows(t, w, off=0, lane_blk=0):
    return pl.BlockSpec((t, w), lambda i: (i + off, lane_blk))


def _rows_lat(t, w, nct):
    return pl.BlockSpec((t, w), lambda i: (jnp.maximum(i - nct, 0), 0))


def _full(*shape):
    return pl.BlockSpec(shape, lambda *_: (0,) * len(shape))


def _wfull(*shape):
    return pl.BlockSpec(shape, lambda *_: (0,) * len(shape), pipeline_mode=pl.Buffered(1))


def _halo(w, hb, nrows):
    r, nb = T // hb, nrows // hb
    prev = pl.BlockSpec((hb, w), lambda i: (jnp.maximum(i * r - 1, 0), 0))
    nxt = pl.BlockSpec((hb, w), lambda i: (jnp.minimum((i + 1) * r, nb - 1), 0))
    return prev, nxt


def _nn(a, b):
    return jnp.dot(a.astype(MXU), b.astype(MXU), preferred_element_type=F32)


def _nt(a, b):
    return lax.dot_general(a.astype(MXU), b.astype(MXU), (((1,), (1,)), ((), ())), preferred_element_type=F32)


def _tn_dot(a, b):
    return lax.dot_general(a.astype(MXU), b.astype(MXU), (((0,), (0,)), ((), ())), preferred_element_type=F32)


def _sum0(x):
    return jnp.sum(x, axis=0, keepdims=True)


def _silu(x):
    return x * jax.nn.sigmoid(x)


def _gelu(x):
    return jax.nn.gelu(x, approximate=True)


def _mod(h, g, sh, sc):
    n = h * lax.rsqrt(jnp.mean(h * h, axis=-1, keepdims=True) + EPS)
    return n * g * (1.0 + sc) + sh


def _ln(x, g, b):
    xc = x - jnp.mean(x, axis=-1, keepdims=True)
    return xc * lax.rsqrt(jnp.mean(xc * xc, axis=-1, keepdims=True) + EPS) * g + b


def _gate_norm(ytot, z, ng):
    yg = ytot * _silu(z)
    halves = []
    for k in range(2):
        seg = yg[:, k * 512:(k + 1) * 512]
        halves.append(seg * lax.rsqrt(jnp.mean(seg * seg, axis=-1, keepdims=True) + EPS) * ng[:, k * 512:(k + 1) * 512])
    return jnp.concatenate(halves, axis=-1)


def _fill_ext(ext_ref, prev_ref, cur_ref, next_ref, hb, first, last):
    ext_ref[0:hb, :] = jnp.where(first, 0.0, prev_ref[...])
    ext_ref[hb:hb + T, :] = cur_ref[...]
    ext_ref[hb + T:hb + T + hb, :] = jnp.where(last, 0.0, next_ref[...])


def _conv(ext_ref, w_ref, k_taps, hb, lanes):
    off = hb - k_taps // 2
    acc = ext_ref[pl.ds(off, T), lanes] * w_ref[0:1, lanes]
    for k in range(1, k_taps):
        acc = acc + ext_ref[pl.ds(off + k, T), lanes] * w_ref[k:k + 1, lanes]
    return acc


def _conv_tr(ext_ref, w_ref, k_taps, hb, lanes):
    off = hb + k_taps // 2
    acc = ext_ref[pl.ds(off, T), lanes] * w_ref[0:1, lanes]
    for k in range(1, k_taps):
        acc = acc + ext_ref[pl.ds(off - k, T), lanes] * w_ref[k:k + 1, lanes]
    return acc


def _conv_wide(ext_ref, w_ref, k_taps, hb, lanes, flip=False):
    base = hb - k_taps // 2
    acc = None
    for b in range(8):
        taps = [k for k in range(k_taps) if (base + k) % 8 == b]
        if not taps:
            continue
        p = None
        for k in taps:
            wi = (k_taps - 1 - k) if flip else k
            term = ext_ref[pl.ds(base + k - b, T + 8), lanes] * w_ref[wi:wi + 1, lanes]
            p = term if p is None else p + term
        acc = p[b:b + T] if acc is None else acc + p[b:b + T]
    return acc


def _conv_dw_wide(dw_ref, d_ref, xext_ref, k_taps, hb, lanes):
    base = hb - k_taps // 2
    d = d_ref[:, lanes]
    for b in range(8):
        taps = [k for k in range(k_taps) if (base + k) % 8 == b]
        if not taps:
            continue
        lo_off = base + taps[0] - b
        span = base + taps[-1] - b - lo_off
        xs = xext_ref[pl.ds(lo_off + b, T + span), lanes]
        for k in taps:
            a = base + k - b - lo_off
            dw_ref[k:k + 1, lanes] += _sum0(d * xs[a:a + T])


def _conv_dw(dw_ref, d_ref, xext_ref, k_taps, hb, lanes):
    off = hb - k_taps // 2
    d = d_ref[:, lanes]
    for k in range(k_taps):
        dw_ref[k:k + 1, lanes] += _sum0(d * xext_ref[pl.ds(off + k, T), lanes])


def _tn(a, b, name, relu2=False, shard=None):
    m_rows, ka = a.shape
    n = b.shape[1]
    tm = next(t for t in (1024, 768, 512, 256) if m_rows % t == 0)
    tk = min(ka, 1024)
    tn = n if n <= 1024 else next(t for t in (1024, 768, 512, 384, 256, 128) if n % t == 0)
    n_m = m_rows // tm
    per = 1 if shard is None else (tn if shard[0] == "col" else tk) // shard[1]

    def body(a_ref, b_ref, o_ref, *acc):
        acc_ref = acc[0] if acc else o_ref

        @pl.when(pl.program_id(2) == 0)
        def _():
            acc_ref[...] = jnp.zeros_like(acc_ref)
        av = a_ref[...]
        if relu2:
            av = jnp.square(jnp.maximum(av.astype(F32), 0.0))
        acc_ref[...] += _tn_dot(av, b_ref[...])
        if acc:
            @pl.when(pl.program_id(2) == n_m - 1)
            def _():
                for s in range(per):
                    if shard[0] == "col":
                        o_ref[s] = acc_ref[:, s * shard[1]:(s + 1) * shard[1]].astype(o_ref.dtype)
                    else:
                        o_ref[s] = acc_ref[s * shard[1]:(s + 1) * shard[1], :].astype(o_ref.dtype)

    in_specs = [pl.BlockSpec((tm, tk), lambda k, j, m: (m, k)), pl.BlockSpec((tm, tn), lambda k, j, m: (m, j))]
    if shard is None:
        return _call(body, name, (ka // tk, n // tn, n_m), in_specs,
                     pl.BlockSpec((tk, tn), lambda k, j, m: (k, j)), _sds((ka, n)))(a, b)
    if shard[0] == "col":
        out_spec = pl.BlockSpec((per, tk, shard[1]), lambda k, j, m: (j, k, 0))
        out_shape = _sds((n // shard[1], ka, shard[1]), jnp.bfloat16)
    else:
        out_spec = pl.BlockSpec((per, shard[1], tn), lambda k, j, m: (k, 0, j))
        out_shape = _sds((ka // shard[1], shard[1], n), jnp.bfloat16)
    return _call(body, name, (ka // tk, n // tn, n_m), in_specs, out_spec, out_shape,
                 scratch=[pltpu.VMEM((tk, tn), F32)])(a, b)


def _mlp_fwd(h, mp, wpack, layer, name):
    n_rows = h.shape[0]

    def body(h_ref, mp_ref, w1_ref, w2_ref, hn_ref, a_ref, y_ref):
        hv = h_ref[...]
        u = _mod(hv, mp_ref[0:1], mp_ref[1:2], mp_ref[2:3]).astype(MXU)
        acc = jnp.zeros((T, D), F32)
        for j in range(HID // 1024):
            cs = slice(j * 1024, (j + 1) * 1024)
            a = jnp.dot(u, w1_ref[j], preferred_element_type=F32)
            a_ref[:, cs] = a.astype(ACT)
            acc = acc + jnp.dot(jnp.square(jnp.maximum(a, 0.0)).astype(MXU), w2_ref[j], preferred_element_type=F32)
        y_ref[...] = acc
        hn_ref[...] = hv + mp_ref[3:4] * acc

    return _call(body, name, (n_rows // T,),
                 [_rows(T, D), _full(8, D), _mlp_wspec(layer), _mlp_wspec(2 + layer)],
                 [_rows(T, D), _rows(T, HID), _rows(T, D)],
                 [_sds((n_rows, D)), _sds((n_rows, HID), ACT), _sds((n_rows, D))])(h, mp, wpack, wpack)


def _mlp_wspec(row_block):
    return pl.BlockSpec((4, 1024, 1024), lambda i: (0, row_block, 0), pipeline_mode=pl.Buffered(1))


def _mlp_bwd(dh, h, a, y, mp, wpack, layer, name):
    n_rows = h.shape[0]

    def body(dh_ref, h_ref, a_ref, y_ref, mp_ref, w1_ref, w2_ref, dho_ref, da_ref, dyb_ref, ub_ref, pg_ref):
        dhp = dh_ref[...]
        u, vjp = jax.vjp(_mod, h_ref[...], mp_ref[0:1], mp_ref[1:2], mp_ref[2:3])
        ub_ref[...] = u.astype(ACT)
        dyb = (mp_ref[3:4] * dhp).astype(MXU)
        dyb_ref[...] = dyb.astype(ACT)
        du = jnp.zeros((T, D), F32)
        for j in range(HID // 1024):
            cs = slice(j * 1024, (j + 1) * 1024)
            dp = _nt(dyb, w2_ref[j])
            da = dp * 2.0 * jnp.maximum(a_ref[:, cs].astype(F32), 0.0)
            da_ref[:, cs] = da.astype(ACT)
            du = du + _nt(da, w1_ref[j])
        dhn, dg, dsh, dsc = vjp(du)
        dho_ref[...] = dhp + dhn

        @pl.when(pl.program_id(0) == 0)
        def _():
            pg_ref[...] = jnp.zeros_like(pg_ref)
        pg_ref[0:1] += dg
        pg_ref[1:2] += dsh
        pg_ref[2:3] += dsc
        pg_ref[3:4] += _sum0(dhp * y_ref[...])

    return _call(body, name, (n_rows // T,),
                 [_rows(T, D), _rows(T, D), _rows(T, HID), _rows(T, D), _full(8, D), _mlp_wspec(layer), _mlp_wspec(2 + layer)],
                 [_rows(T, D), _rows(T, HID), _rows(T, D), _rows(T, D), _full(8, D)],
                 [_sds((n_rows, D)), _sds((n_rows, HID), ACT), _sds((n_rows, D), ACT), _sds((n_rows, D), ACT),
                  _sds((8, D))])(dh, h, a, y, mp, wpack, wpack)


def _cf1_fwd(h, mp, w1, b1):
    n_rows = h.shape[0]

    def body(h_ref, mp_ref, w1_ref, b1_ref, glu_ref, a_ref):
        u = _mod(h_ref[...], mp_ref[0:1], mp_ref[1:2], mp_ref[2:3]).astype(MXU)
        a = jnp.dot(u, w1_ref[...], preferred_element_type=F32) + b1_ref[...]
        a_ref[...] = a.astype(ACT)
        glu_ref[...] = a[:, :D] * jax.nn.sigmoid(a[:, D:])

    return _call(body, "cf1_fwd", (n_rows // T,),
                 [_rows(T, D), _full(8, D), _wfull(D, 2 * D), _full(1, 2 * D)],
                 [_rows(T, D), _rows(T, 2 * D)],
                 [_sds((n_rows, D)), _sds((n_rows, 2 * D), ACT)])(h, mp, w1, b1)


def _cf2_fwd(h, glu, mp, wdw, bdw, lng, lnb, w2, b2):
    n_rows = h.shape[0]
    nt = n_rows // T
    hb = 16

    def body(h_ref, gp_ref, gc_ref, gn_ref, mp_ref, wdw_ref, bdw_ref, lng_ref, lnb_ref, w2_ref, b2_ref,
             hn_ref, cv_ref, sb_ref, y_ref, ext):
        i = pl.program_id(0)
        _fill_ext(ext, gp_ref, gc_ref, gn_ref, hb, i == 0, i == nt - 1)
        for c in range(D // LANE):
            lanes = slice(c * LANE, (c + 1) * LANE)
            cv_ref[:, lanes] = _conv_wide(ext, wdw_ref, CK, hb, lanes) + bdw_ref[:, lanes]
        s = _silu(_ln(cv_ref[...], lng_ref[...], lnb_ref[...])).astype(MXU)
        sb_ref[...] = s.astype(ACT)
        y = jnp.dot(s, w2_ref[...], preferred_element_type=F32) + b2_ref[...]
        y_ref[...] = y
        hn_ref[...] = h_ref[...] + mp_ref[3:4] * y

    gp, gn = _halo(D, hb, n_rows)
    return _call(body, "cf2_fwd", (nt,),
                 [_rows(T, D), gp, _rows(T, D), gn, _full(8, D), _full(32, D), _full(1, D), _full(1, D), _full(1, D),
                  _wfull(D, D), _full(1, D)],
                 [_rows(T, D), _rows(T, D), _rows(T, D), _rows(T, D)],
                 [_sds((n_rows, D)), _sds((n_rows, D)), _sds((n_rows, D), ACT), _sds((n_rows, D))],
                 scratch=[pltpu.VMEM((T + 2 * hb, D), F32)])(h, glu, glu, glu, mp, wdw, bdw, lng, lnb, w2, b2)


def _cf2_bwd(dh, y, cv, mp, lng, lnb, w2):
    n_rows = dh.shape[0]

    def body(dh_ref, y_ref, cv_ref, mp_ref, lng_ref, lnb_ref, w2_ref, dcv_ref, dyb_ref, pg_ref):
        dhp = dh_ref[...]
        dy = mp_ref[3:4] * dhp
        dyb = dy.astype(MXU)
        dyb_ref[...] = dyb.astype(ACT)
        ds = _nt(dyb, w2_ref[...])
        _, vjp = jax.vjp(lambda cv_, g_, b_: _silu(_ln(cv_, g_, b_)), cv_ref[...], lng_ref[...], lnb_ref[...])
        dcv, dlng, dlnb = vjp(ds)
        dcv_ref[...] = dcv

        @pl.when(pl.program_id(0) == 0)
        def _():
            pg_ref[...] = jnp.zeros_like(pg_ref)
        pg_ref[0:1] += _sum0(dhp * y_ref[...])
        pg_ref[1:2] += _sum0(dy)
        pg_ref[2:3] += dlng
        pg_ref[3:4] += dlnb
        pg_ref[4:5] += _sum0(dcv)

    return _call(body, "cf2_bwd", (n_rows // T,),
                 [_rows(T, D), _rows(T, D), _rows(T, D), _full(8, D), _full(1, D), _full(1, D), _wfull(D, D)],
                 [_rows(T, D), _rows(T, D), _full(8, D)],
                 [_sds((n_rows, D)), _sds((n_rows, D), ACT), _sds((8, D))])(dh, y, cv, mp, lng, lnb, w2)


def _cf1_bwd(dh, h, a, dcv, glu, mp, wdw, w1):
    n_rows = h.shape[0]
    nt = n_rows // T
    hb = 16

    def body(dh_ref, h_ref, a_ref, dp_ref, dc_ref, dn_ref, gp_ref, gc_ref, gn_ref, mp_ref, wdw_ref, w1_ref,
             dho_ref, da_ref, ub_ref, pg_ref, pb_ref, dw_ref, dext, gext, dglu):
        i = pl.program_id(0)

        @pl.when(i == 0)
        def _():
            pg_ref[...] = jnp.zeros_like(pg_ref)
            pb_ref[...] = jnp.zeros_like(pb_ref)
            dw_ref[...] = jnp.zeros_like(dw_ref)
        _fill_ext(dext, dp_ref, dc_ref, dn_ref, hb, i == 0, i == nt - 1)
        _fill_ext(gext, gp_ref, gc_ref, gn_ref, hb, i == 0, i == nt - 1)
        for c in range(D // LANE):
            lanes = slice(c * LANE, (c + 1) * LANE)
            dglu[:, lanes] = _conv_wide(dext, wdw_ref, CK, hb, lanes, flip=True)
            _conv_dw_wide(dw_ref, dc_ref, gext, CK, hb, lanes)
        av = a_ref[...].astype(F32)
        _, vjp_glu = jax.vjp(lambda a1, a2: a1 * jax.nn.sigmoid(a2), av[:, :D], av[:, D:])
        da1, da2 = vjp_glu(dglu[...])
        da_ref[:, :D] = da1.astype(ACT)
        da_ref[:, D:] = da2.astype(ACT)
        pb_ref[0:1, :D] += _sum0(da1)
        pb_ref[0:1, D:] += _sum0(da2)
        du = _nt(da1, w1_ref[:, :D]) + _nt(da2, w1_ref[:, D:])
        u, vjp = jax.vjp(_mod, h_ref[...], mp_ref[0:1], mp_ref[1:2], mp_ref[2:3])
        ub_ref[...] = u.astype(ACT)
        dhn, dg, dsh, dsc = vjp(du)
        dho_ref[...] = dh_ref[...] + dhn
        pg_ref[0:1] += dg
        pg_ref[1:2] += dsh
        pg_ref[2:3] += dsc

    hp, hn = _halo(D, hb, n_rows)
    return _call(body, "cf1_bwd", (nt,),
                 [_rows(T, D), _rows(T, D), _rows(T, 2 * D), hp, _rows(T, D), hn, hp, _rows(T, D), hn,
                  _full(8, D), _full(32, D), _wfull(D, 2 * D)],
                 [_rows(T, D), _rows(T, 2 * D), _rows(T, D), _full(8, D), _full(8, 2 * D), _full(32, D)],
                 [_sds((n_rows, D)), _sds((n_rows, 2 * D), ACT), _sds((n_rows, D), ACT), _sds((8, D)),
                  _sds((8, 2 * D)), _sds((32, D))],
                 scratch=[pltpu.VMEM((T + 2 * hb, D), F32), pltpu.VMEM((T + 2 * hb, D), F32), pltpu.VMEM((T, D), F32)],
                 )(dh, h, a, dcv, dcv, dcv, glu, glu, glu, mp, wdw, w1)


def _sg_blocks():
    return [(c, g, slice(c * Q, (c + 1) * Q), slice(g * LANE, (g + 1) * LANE)) for c in range(T // Q) for g in range(SGG)]


IN_W = D + XBC + 32 + 2 * D
IN_LOC = IN_W // 4


def _win_split(shards):
    o1, o2, o3 = D, D + XBC, D + XBC + 32
    tr = 256

    def cols(s_ref, lo, hi):
        parts = []
        for j in range(4):
            a, b = max(lo, j * IN_LOC), min(hi, (j + 1) * IN_LOC)
            if a < b:
                parts.append(s_ref[j][:, a - j * IN_LOC:b - j * IN_LOC])
        return parts[0] if len(parts) == 1 else jnp.concatenate(parts, axis=1)

    def body(s_ref, wz_ref, wxbc_ref, wdt_ref, wuv_ref):
        wz_ref[...] = cols(s_ref, 0, o1)
        wxbc_ref[...] = cols(s_ref, o1, o2)
        dt = cols(s_ref, o2, o3)
        wdt_ref[...] = jnp.concatenate([dt, jnp.zeros((tr, LANE - 32), dt.dtype)], axis=1)
        wuv_ref[...] = cols(s_ref, o3, IN_W)

    dt_ = shards.dtype
    return _call(body, "win_split", (D // tr,), [pl.BlockSpec((4, tr, IN_LOC), lambda i: (0, i, 0))],
                 [_rows(tr, D), _rows(tr, XBC), _rows(tr, LANE), _rows(tr, 2 * D)],
                 [_sds((D, D), dt_), _sds((D, XBC), dt_), _sds((D, LANE), dt_), _sds((D, 2 * D), dt_)])(shards)


def _win_join(gz, gxbc, gdt, guv):
    tr = 256
    bounds = (0, D, D + XBC, D + XBC + 32, IN_W)

    def body(gz_ref, gx_ref, gd_ref, gu_ref, o_ref):
        segs = (gz_ref, gx_ref, gd_ref, gu_ref)
        for j in range(4):
            parts = []
            for k in range(4):
                a, b = max(bounds[k], j * IN_LOC), min(bounds[k + 1], (j + 1) * IN_LOC)
                if a < b:
                    parts.append(segs[k][:, a - bounds[k]:b - bounds[k]])
            full = parts[0] if len(parts) == 1 else jnp.concatenate(parts, axis=1)
            o_ref[j] = full.astype(jnp.bfloat16)

    return _call(body, "win_join", (D // tr,), [_rows(tr, D), _rows(tr, XBC), _rows(tr, LANE), _rows(tr, 2 * D)],
                 pl.BlockSpec((4, tr, IN_LOC), lambda i: (0, i, 0)), _sds((4, D, IN_LOC), jnp.bfloat16))(gz, gxbc, gdt, guv)


def _ctx_spec(nct):
    return pl.BlockSpec((T, D), lambda i: (jnp.minimum(i, nct - 1), 0))


def _hy1_fwd(ctx, x, mp2, wz, wuv, wxbc, wdt, lng, lnb, sgw, sgbt, nct):
    n_lat = x.shape[0]
    n_rows = ctx.shape[0] + n_lat

    def body(c_ref, x_ref, mp_ref, wz_ref, wuv_ref, wxbc_ref, wdt_ref, lng_ref, lnb_ref, sgw_ref, sgbt_ref,
             z_ref, uv_ref, xbcp_ref, dtr_ref, ysg_ref):
        hv = jnp.where(pl.program_id(0) < nct, c_ref[...], x_ref[...])
        u = _mod(hv, mp_ref[0:1], mp_ref[1:2], mp_ref[2:3]).astype(MXU)
        z_ref[...] = jnp.dot(u, wz_ref[...], preferred_element_type=F32)
        xbcp_ref[...] = jnp.dot(u, wxbc_ref[...], preferred_element_type=F32)
        dtr_ref[...] = jnp.dot(u, wdt_ref[...], preferred_element_type=F32)
        uv = jnp.dot(u, wuv_ref[...], preferred_element_type=F32)
        uv_ref[...] = uv
        gate = _gelu(uv[:, :D])
        vln = _ln(_gelu(uv[:, D:]), lng_ref[...], lnb_ref[...]).astype(MXU)
        for _, g, rs, ls in _sg_blocks():
            s = jnp.dot(sgw_ref[g], vln[rs, ls], preferred_element_type=F32) + sgbt_ref[:, g:g + 1]
            ysg_ref[rs, ls] = (gate[rs, ls] * s).astype(ACT)

    mspec = pl.BlockSpec((None, 8, D), lambda i: (jnp.where(i < nct, 0, 1), 0, 0))
    return _call(body, "hy1_fwd", (n_rows // T,),
                 [_ctx_spec(nct), _rows_lat(T, D, nct), mspec, _wfull(D, D), _wfull(D, 2 * D), _wfull(D, XBC), _wfull(D, LANE),
                  _full(1, D), _full(1, D), _full(SGG, Q, Q), _full(Q, LANE)],
                 [_rows(T, D), _rows(T, 2 * D), _rows(T, XBC), _rows(T, LANE), _rows_lat(T, D, nct)],
                 [_sds((n_rows, D)), _sds((n_rows, 2 * D)), _sds((n_rows, XBC)), _sds((n_rows, LANE)),
                  _sds((n_lat, D), ACT)])(ctx, x, mp2, wz, wuv, wxbc, wdt, lng, lnb, sgw, sgbt)


def _hy1_bwd(ctx, x, uv, dz, dxbcp, ddf, ddb, dysg, dres, mp2, wz, wuv, wxbc, wdt, lng, lnb, sgw, sgbt, nct):
    n_lat = dres.shape[0]
    n_rows = ctx.shape[0] + n_lat

    def body(c_ref, x_ref, uv_ref, dz_ref, dxbcp_ref, ddf_ref, ddb_ref, dysg_ref, dres_ref, mp_ref, wz_ref, wuv_ref,
             wxbc_ref, wdt_ref, lng_ref, lnb_ref, sgw_ref, sgbt_ref,
             dho_ref, ub_ref, duv_ref, ddt_ref, pg2_ref, pl_ref, dsgw_ref, dsgb_ref, dgate_s, dvln_s):
        i = pl.program_id(0)

        @pl.when(i == 0)
        def _():
            pg2_ref[...] = jnp.zeros_like(pg2_ref)
            pl_ref[...] = jnp.zeros_like(pl_ref)
            dsgw_ref[...] = jnp.zeros_like(dsgw_ref)
            dsgb_ref[...] = jnp.zeros_like(dsgb_ref)
        uv = uv_ref[...]

        def f_sg(ug, uvv, g_, b_):
            return _gelu(ug), _ln(_gelu(uvv), g_, b_)
        (gate, vln), vjp_sg = jax.vjp(f_sg, uv[:, :D], uv[:, D:], lng_ref[...], lnb_ref[...])
        vlnb = vln.astype(MXU)
        lane = lax.broadcasted_iota(jnp.int32, (Q, LANE), 1)
        dsgb = jnp.zeros((Q, LANE), F32)
        for _, g, rs, ls in _sg_blocks():
            s = jnp.dot(sgw_ref[g], vlnb[rs, ls], preferred_element_type=F32) + sgbt_ref[:, g:g + 1]
            dyb = dysg_ref[rs, ls]
            dgate_s[rs, ls] = dyb * s
            ds = dyb * gate[rs, ls]
            dvln_s[rs, ls] = _tn_dot(sgw_ref[g], ds)
            dsgw_ref[g] += _nt(ds, vlnb[rs, ls])
            dsgb = dsgb + jnp.where(lane == g, jnp.sum(ds, axis=1, keepdims=True), 0.0)
        dsgb_ref[...] += dsgb
        dug, duvv, dlng, dlnb = vjp_sg((dgate_s[...], dvln_s[...]))
        pl_ref[0:1] += dlng
        pl_ref[1:2] += dlnb
        duv_ref[:, :D] = dug.astype(ACT)
        duv_ref[:, D:] = duvv.astype(ACT)
        ddt = (ddf_ref[...] + ddb_ref[...]).astype(MXU)
        ddt_ref[...] = ddt.astype(ACT)
        du = (_nt(dz_ref[...], wz_ref[...]) + _nt(dug, wuv_ref[:, :D]) + _nt(duvv, wuv_ref[:, D:])
              + _nt(dxbcp_ref[...], wxbc_ref[...]) + _nt(ddt, wdt_ref[...]))
        hv = jnp.where(i < nct, c_ref[...], x_ref[...])
        u, vjp = jax.vjp(_mod, hv, mp_ref[0:1], mp_ref[1:2], mp_ref[2:3])
        ub_ref[...] = u.astype(ACT)
        dhn, dg, dsh, dsc = vjp(du)
        dho_ref[...] = dres_ref[...] + dhn
        is_ctx = i < nct
        for k, val in enumerate((dg, dsh, dsc)):
            pg2_ref[0, k:k + 1] += jnp.where(is_ctx, val, 0.0)
            pg2_ref[1, k:k + 1] += jnp.where(is_ctx, 0.0, val)

    mspec = pl.BlockSpec((None, 8, D), lambda i: (jnp.where(i < nct, 0, 1), 0, 0))
    return _call(body, "hy1_bwd", (n_rows // T,),
                 [_ctx_spec(nct), _rows_lat(T, D, nct), _rows(T, 2 * D), _rows(T, D), _rows(T, XBC), _rows(T, LANE),
                  _rows(T, LANE), _rows(T, D),
                  _rows_lat(T, D, nct), mspec, _wfull(D, D), _wfull(D, 2 * D), _wfull(D, XBC), _wfull(D, LANE),
                  _full(1, D), _full(1, D), _full(SGG, Q, Q), _full(Q, LANE)],
                 [_rows_lat(T, D, nct), _rows(T, D), _rows(T, 2 * D), _rows(T, LANE), _full(2, 8, D), _full(8, D),
                  _full(SGG, Q, Q), _full(Q, LANE)],
                 [_sds((n_lat, D)), _sds((n_rows, D), ACT), _sds((n_rows, 2 * D), ACT), _sds((n_rows, LANE), ACT),
                  _sds((2, 8, D)), _sds((8, D)), _sds((SGG, Q, Q)), _sds((Q, LANE))],
                 scratch=[pltpu.VMEM((T, D), F32), pltpu.VMEM((T, D), F32)],
                 )(ctx, x, uv, dz, dxbcp, ddf, ddb, dysg, dres, mp2, wz, wuv, wxbc, wdt, lng, lnb, sgw, sgbt)


def _seq_edges(i, nct, nt):
    return (i == 0) | (i == nct), (i == nct - 1) | (i == nt - 1)


def _cv5_fwd(xbcp, w, b, nct):
    n_rows = xbcp.shape[0]
    nt = n_rows // T
    hb = 8

    def body(p_ref, c_ref, n_ref, w_ref, b_ref, o_ref, ext):
        first, last = _seq_edges(pl.program_id(0), nct, nt)
        _fill_ext(ext, p_ref, c_ref, n_ref, hb, first, last)
        for c in range(XBC // LANE):
            lanes = slice(c * LANE, (c + 1) * LANE)
            o_ref[:, lanes] = _silu(_conv(ext, w_ref, SK, hb, lanes) + b_ref[:, lanes])

    hp, hn = _halo(XBC, hb, n_rows)
    return _call(body, "cv5_fwd", (nt,), [hp, _rows(T, XBC), hn, _full(8, XBC), _full(1, XBC)],
                 _rows(T, XBC), _sds((n_rows, XBC)), scratch=[pltpu.VMEM((T + 2 * hb, XBC), F32)])(xbcp, xbcp, xbcp, w, b)


def _cv5_bwd1(xbcp, dxf, dxb, w, b, nct):
    n_rows = xbcp.shape[0]
    nt = n_rows // T
    hb = 8

    def body(p_ref, c_ref, n_ref, dxf_ref, dxb_ref, w_ref, b_ref, o_ref, pg_ref, ext):
        i = pl.program_id(0)
        first, last = _seq_edges(i, nct, nt)
        _fill_ext(ext, p_ref, c_ref, n_ref, hb, first, last)

        @pl.when(i == 0)
        def _():
            pg_ref[...] = jnp.zeros_like(pg_ref)
        for c in range(XBC // LANE):
            lanes = slice(c * LANE, (c + 1) * LANE)
            cv = _conv(ext, w_ref, SK, hb, lanes) + b_ref[:, lanes]
            sg = jax.nn.sigmoid(cv)
            dcv = (dxf_ref[:, lanes] + dxb_ref[:, lanes]) * (sg * (1.0 + cv * (1.0 - sg)))
            o_ref[:, lanes] = dcv
            pg_ref[0:1, lanes] += _sum0(dcv)

    hp, hn = _halo(XBC, hb, n_rows)
    return _call(body, "cv5_bwd1", (nt,),
                 [hp, _rows(T, XBC), hn, _rows(T, XBC), _rows(T, XBC), _full(8, XBC), _full(1, XBC)],
                 [_rows(T, XBC), _full(8, XBC)], [_sds((n_rows, XBC)), _sds((8, XBC))],
                 scratch=[pltpu.VMEM((T + 2 * hb, XBC), F32)])(xbcp, xbcp, xbcp, dxf, dxb, w, b)


def _cv5_bwd2(dcv, xbcp, w, nct):
    n_rows = xbcp.shape[0]
    nt = n_rows // T
    hb = 8

    def body(dp_ref, dc_ref, dn_ref, xp_ref, xc_ref, xn_ref, w_ref, o_ref, dw_ref, dext, xext):
        i = pl.program_id(0)
        first, last = _seq_edges(i, nct, nt)
        _fill_ext(dext, dp_ref, dc_ref, dn_ref, hb, first, last)
        _fill_ext(xext, xp_ref, xc_ref, xn_ref, hb, first, last)

        @pl.when(i == 0)
        def _():
            dw_ref[...] = jnp.zeros_like(dw_ref)
        for c in range(XBC // LANE):
            lanes = slice(c * LANE, (c + 1) * LANE)
            o_ref[:, lanes] = _conv_tr(dext, w_ref, SK, hb, lanes).astype(ACT)
            _conv_dw(dw_ref, dc_ref, xext, SK, hb, lanes)

    hp, hn = _halo(XBC, hb, n_rows)
    return _call(body, "cv5_bwd2", (nt,),
                 [hp, _rows(T, XBC), hn, hp, _rows(T, XBC), hn, _full(8, XBC)],
                 [_rows(T, XBC), _full(8, XBC)], [_sds((n_rows, XBC), ACT), _sds((8, XBC))],
                 scratch=[pltpu.VMEM((T + 2 * hb, XBC), F32), pltpu.VMEM((T + 2 * hb, XBC), F32)],
                 )(dcv, dcv, dcv, xbcp, xbcp, xbcp, w)


def _scan_order(nc, ncc, rev):
    if not rev:
        return lambda s: s
    return lambda s: jnp.where(s < ncc, ncc - 1 - s, nc - 1 - (s - ncc))


def _ssd_prep(dtr, sp, rev):
    dt = jax.nn.softplus(dtr + sp[0:1])
    a_neg = -jnp.exp(sp[1:2])
    r = lax.broadcasted_iota(jnp.int32, (Q, Q), 0)
    c = lax.broadcasted_iota(jnp.int32, (Q, Q), 1)
    msk = (c >= r) if rev else (c <= r)
    tri = msk.astype(F32)
    acs = jnp.dot(tri, dt * a_neg, precision=HI, preferred_element_type=F32)
    last = 0 if rev else Q - 1
    return dt, a_neg, acs, msk, tri, last


def _pair_sel(arr, lo, m, lane_lt):
    h0 = lo + 2 * m
    return jnp.where(lane_lt, arr[:, h0:h0 + 1], arr[:, h0 + 1:h0 + 2])


def _head_lanes(row, lo, g):
    lane = lax.broadcasted_iota(jnp.int32, (1, 512), 1)
    out = jnp.zeros((1, 512), F32)
    for k in range(8):
        h = lo + 8 * g + k
        out = jnp.where((lane >= 64 * k) & (lane < 64 * (k + 1)), row[:, h:h + 1], out)
    return out


def _halves(v, lane_lt):
    return jnp.concatenate([jnp.where(lane_lt, v, 0.0), jnp.where(lane_lt, 0.0, v)], axis=0)


def _ssd_fwd(xbc, dtr, sp, ncc, rev):
    n_rows = xbc.shape[0]
    nc = n_rows // Q
    lo = 16 if rev else 0
    order = _scan_order(nc, ncc, rev)

    def body(x_ref, dtr_ref, sp_ref, y_ref, hin_ref, st):
        @pl.when(pl.program_id(0) == 0)
        def _():
            st[...] = jnp.zeros_like(st)
        dt, _, acs, msk, _, last = _ssd_prep(dtr_ref[...], sp_ref[...], rev)
        acs_t, dt_t = acs.T, dt.T
        eacs = jnp.exp(acs)
        eal = jnp.exp(acs[last:last + 1, :])
        tew = jnp.exp(acs[last:last + 1, :] - acs) * dt
        lane_lt = lax.broadcasted_iota(jnp.int32, (Q, LANE), 1) < 64
        for g in range(2):
            gl = slice(g * 512, (g + 1) * 512)
            bg = x_ref[:, 1024 + g * 128:1152 + g * 128]
            cg = x_ref[:, 1280 + g * 128:1408 + g * 128]
            s_g = _nt(cg, bg)
            h_t = st[:, gl]
            hin_ref[:, gl] = h_t
            yoff = _nn(cg, h_t)
            xw = []
            for mm in range(4):
                m = 4 * g + mm
                ls = slice(m * LANE, (m + 1) * LANE)
                x2 = x_ref[:, ls]
                ws = []
                for hh in range(2):
                    h = lo + 2 * m + hh
                    lm = jnp.exp(jnp.where(msk, acs[:, h:h + 1] - acs_t[h:h + 1, :], -jnp.inf))
                    ws.append(s_g * lm * dt_t[h:h + 1, :])
                y2 = _nn(jnp.concatenate(ws, axis=1), _halves(x2, lane_lt))
                y_ref[:, ls] = y2 + yoff[:, mm * LANE:(mm + 1) * LANE] * _pair_sel(eacs, lo, m, lane_lt)
                xw.append(x2 * _pair_sel(tew, lo, m, lane_lt))
            st[:, gl] = _head_lanes(eal, lo, g) * h_t + _tn_dot(bg, jnp.concatenate(xw, axis=1))

    return _call(body, "ssd_fwd_r" if rev else "ssd_fwd_f", (nc,),
                 [pl.BlockSpec((Q, XBC), lambda s: (order(s), 0)), pl.BlockSpec((Q, LANE), lambda s: (order(s), 0)),
                  _full(8, LANE)],
                 [pl.BlockSpec((Q, D), lambda s: (order(s), 0)), pl.BlockSpec((None, LANE, D), lambda s: (order(s), 0, 0))],
                 [_sds((n_rows, D)), _sds((nc, LANE, D))], scratch=[pltpu.VMEM((LANE, D), F32)])(xbc, dtr, sp)


def _ssd_bwd(xbc, dtr, dy, hin, sp, dl, eh, ncc, rev):
    n_rows = xbc.shape[0]
    nc = n_rows // Q
    lo = 16 if rev else 0
    fwd_order = _scan_order(nc, ncc, rev)
    order = lambda s: fwd_order(nc - 1 - s)
    with_skip = not rev

    def body(x_ref, dtr_ref, dy_ref, hin_ref, sp_ref, dl_ref, eh_ref, dx_ref, ddtr_ref, pg_ref, dst):
        @pl.when(pl.program_id(0) == 0)
        def _():
            dst[...] = jnp.zeros_like(dst)
            pg_ref[...] = jnp.zeros_like(pg_ref)
        dtr_v = dtr_ref[...]
        dt, a_neg, acs, msk, tri, last = _ssd_prep(dtr_v, sp_ref[...], rev)
        acs_t = acs.T
        r = lax.broadcasted_iota(jnp.int32, (Q, Q), 0)
        c = lax.broadcasted_iota(jnp.int32, (Q, Q), 1)
        msk_t = (c <= r) if rev else (c >= r)
        eacs = jnp.exp(acs)
        eal = jnp.exp(acs[last:last + 1, :])
        te = jnp.exp(acs[last:last + 1, :] - acs)
        lane = lax.broadcasted_iota(jnp.int32, (Q, LANE), 1)
        lane1 = lax.broadcasted_iota(jnp.int32, (1, LANE), 1)
        lane_lt = lane < 64
        dacs = jnp.zeros((Q, LANE), F32)
        ddt_x = jnp.zeros((Q, LANE), F32)
        dlast = jnp.zeros((1, LANE), F32)
        hs_rows = []
        sub16 = lax.broadcasted_iota(jnp.int32, (16, Q), 0)
        dacs_t = jnp.zeros((16, Q), F32)
        for g in range(2):
            gl = slice(g * 512, (g + 1) * 512)
            bg = x_ref[:, 1024 + g * 128:1152 + g * 128]
            cg = x_ref[:, 1280 + g * 128:1408 + g * 128]
            s_g = _nt(cg, bg)
            s_gt = _nt(bg, cg)
            h_t, dh_t = hin_ref[:, gl], dst[:, gl]
            bh = _nn(bg, dh_t)
            yoff = _nn(cg, h_t)
            d_s = jnp.zeros((Q, Q), F32)
            edy, exd = [], []
            for mm in range(4):
                m = 4 * g + mm
                ls = slice(m * LANE, (m + 1) * LANE)
                x2, dy2 = x_ref[:, ls], dy_ref[:, ls]
                bh2 = bh[:, mm * LANE:(mm + 1) * LANE]
                dtm, em, eam = (_pair_sel(v, lo, m, lane_lt) for v in (dt, te, eacs))
                xd2 = x2 * dtm
                lms, mts = [], []
                for hh in range(2):
                    h = lo + 2 * m + hh
                    col, row = acs[:, h:h + 1], acs_t[h:h + 1, :]
                    lms.append(jnp.exp(jnp.where(msk, col - row, -jnp.inf)))
                    mts.append(s_gt * jnp.exp(jnp.where(msk_t, row - col, -jnp.inf)))
                dy_st = _halves(dy2, lane_lt)
                dxd2 = em * bh2 + _nn(jnp.concatenate(mts, axis=1), dy_st)
                dm_st = _nt(dy_st, xd2)
                dmt_st = _nt(_halves(xd2, lane_lt), dy2)
                d_s = d_s + dm_st[:Q] * lms[0] + dm_st[Q:] * lms[1]
                v1, v2, v3 = dy2 * yoff[:, mm * LANE:(mm + 1) * LANE] * eam, dxd2 * x2, xd2 * bh2 * em
                for hh in range(2):
                    h = lo + 2 * m + hh
                    half = lane_lt == (hh == 0)
                    g_rows = _sum0(dmt_st[hh * Q:(hh + 1) * Q] * mts[hh]) - _sum0(dm_st[hh * Q:(hh + 1) * Q] * s_g * lms[hh])
                    dacs_t = jnp.where(sub16 == 2 * m + hh, g_rows, dacs_t)
                    r1 = jnp.sum(jnp.where(half, v1, 0.0), axis=1, keepdims=True)
                    r2 = jnp.sum(jnp.where(half, v2, 0.0), axis=1, keepdims=True)
                    r3 = jnp.sum(jnp.where(half, v3, 0.0), axis=1, keepdims=True)
                    dacs = dacs + jnp.where(lane == h, r1 - r3, 0.0)
                    ddt_x = ddt_x + jnp.where(lane == h, r2, 0.0)
                    dlast = dlast + jnp.where(lane1 == h, _sum0(r3), 0.0)
                dx2 = dxd2 * dtm
                if with_skip:
                    dx2 = dx2 + dl_ref[:, ls] * dy2
                dx_ref[:, ls] = dx2
                edy.append(eam * dy2)
                exd.append(em * xd2)
            edy, exd = jnp.concatenate(edy, axis=1), jnp.concatenate(exd, axis=1)
            hs_rows.append(_sum0(h_t * dh_t))
            dst[:, gl] = _head_lanes(eal, lo, g) * dh_t + _tn_dot(cg, edy)
            dx_ref[:, 1024 + g * 128:1152 + g * 128] = _tn_dot(d_s, cg) + _nt(exd, dh_t)
            dx_ref[:, 1280 + g * 128:1408 + g * 128] = _nn(d_s, bg) + _nt(edy, h_t)
        hs = jnp.broadcast_to(jnp.concatenate(hs_rows, axis=1), (8, D))
        hsum = jnp.dot(hs, eh_ref[...], precision=HI, preferred_element_type=F32)[0:1]
        dlast = dlast + eal * hsum
        dacs = dacs + jnp.concatenate([jnp.zeros((lo, Q), F32)] * (lo > 0) + [dacs_t, jnp.zeros((LANE - 16 - lo, Q), F32)],
                                      axis=0).T
        rowi = lax.broadcasted_iota(jnp.int32, (Q, LANE), 0)
        dacs = dacs + jnp.where(rowi == last, dlast, 0.0)
        da = lax.dot_general(tri, dacs, (((0,), (0,)), ((), ())), precision=HI, preferred_element_type=F32)
        ddt = ddt_x + da * a_neg
        mine = (lane >= lo) & (lane < lo + 16)
        ddtr = jnp.where(mine, ddt * jax.nn.sigmoid(dtr_v + sp_ref[0:1]), 0.0)
        ddtr_ref[...] = ddtr
        pg_ref[0:1] += _sum0(ddtr)
        pg_ref[1:2] += jnp.where(mine[0:1], _sum0(da * dt) * a_neg, 0.0)

    blk = lambda w_: pl.BlockSpec((Q, w_), lambda s: (order(s), 0))
    return _call(body, "ssd_bwd_r" if rev else "ssd_bwd_f", (nc,),
                 [blk(XBC), blk(LANE), blk(D), pl.BlockSpec((None, LANE, D), lambda s: (order(s), 0, 0)),
                  _full(8, LANE), _full(1, D), _full(D, LANE)],
                 [blk(XBC), blk(LANE), _full(8, LANE)],
                 [_sds((n_rows, XBC)), _sds((n_rows, LANE)), _sds((8, LANE))],
                 scratch=[pltpu.VMEM((LANE, D), F32)])(xbc, dtr, dy, hin, sp, dl, eh)


def _hy4_fwd(h, yf, yb, xbc, z, ysg, mp, dl, ng, wout, nct):
    n_rows = h.shape[0]

    def body(h_ref, yf_ref, yb_ref, xs_ref, z_ref, ysg_ref, mp_ref, dl_ref, ng_ref, wout_ref, hn_ref, yssd_ref, out_ref):
        ytot = yf_ref[...] + yb_ref[...] + dl_ref[...] * xs_ref[...]
        yssd = _gate_norm(ytot, z_ref[...], ng_ref[...]).astype(MXU)
        yssd_ref[...] = yssd.astype(ACT)
        out = (jnp.dot(yssd, wout_ref[0:D, :], preferred_element_type=F32)
               + jnp.dot(ysg_ref[...].astype(MXU), wout_ref[D:2 * D, :], preferred_element_type=F32))
        out_ref[...] = out
        hn_ref[...] = h_ref[...] + mp_ref[3:4] * out

    return _call(body, "hy4_fwd", (n_rows // T,),
                 [_rows(T, D), _rows(T, D, nct), _rows(T, D, nct), _rows(T, D, nct), _rows(T, D, nct), _rows(T, D),
                  _full(8, D), _full(1, D), _full(1, D), _wfull(2 * D, D)],
                 [_rows(T, D), _rows(T, D), _rows(T, D)],
                 [_sds((n_rows, D)), _sds((n_rows, D), ACT), _sds((n_rows, D))])(h, yf, yb, xbc, z, ysg, mp, dl, ng, wout)


def _hy4_bwd(dh, out, yf, yb, xbc, z, mp, dl, ng, wout, nct):
    n_lat = dh.shape[0]
    n_rows = yf.shape[0]

    def body(dh_ref, out_ref, yf_ref, yb_ref, xs_ref, z_ref, mp_ref, dl_ref, ng_ref, wout_ref,
             dy_ref, dz_ref, dysg_ref, doutb_ref, pg_ref):
        i = pl.program_id(0)

        @pl.when(i == 0)
        def _():
            pg_ref[...] = jnp.zeros_like(pg_ref)

        @pl.when(i < nct)
        def _():
            dy_ref[...] = jnp.zeros_like(dy_ref)
            dz_ref[...] = jnp.zeros_like(dz_ref)
            dysg_ref[...] = jnp.zeros_like(dysg_ref)
            doutb_ref[...] = jnp.zeros_like(doutb_ref)

        @pl.when(i >= nct)
        def _():
            dhp = dh_ref[...]
            doutb = (mp_ref[3:4] * dhp).astype(MXU)
            doutb_ref[...] = doutb.astype(ACT)
            dysg_ref[...] = _nt(doutb, wout_ref[D:2 * D, :])
            dyssd = _nt(doutb, wout_ref[0:D, :])
            xs = xs_ref[...]
            ytot = yf_ref[...] + yb_ref[...] + dl_ref[...] * xs
            _, vjp = jax.vjp(_gate_norm, ytot, z_ref[...], ng_ref[...])
            dytot, dz, dng = vjp(dyssd)
            dy_ref[...] = dytot
            dz_ref[...] = dz.astype(ACT)
            pg_ref[0:1] += _sum0(dhp * out_ref[...])
            pg_ref[1:2] += dng
            pg_ref[2:3] += _sum0(dytot * xs)

    return _call(body, "hy4_bwd", (n_rows // T,),
                 [_rows_lat(T, D, nct), _rows_lat(T, D, nct), _rows(T, D), _rows(T, D), _rows(T, D), _rows(T, D),
                  _full(8, D), _full(1, D), _full(1, D), _wfull(2 * D, D)],
                 [_rows(T, D), _rows(T, D), _rows(T, D), _rows_lat(T, D, nct), _full(8, D)],
                 [_sds((n_rows, D)), _sds((n_rows, D), ACT), _sds((n_rows, D)), _sds((n_lat, D), ACT), _sds((8, D))],
                 )(dh, out, yf, yb, xbc, z, mp, dl, ng, wout)


def _loss_bwd(h, tgt, fng):
    n_rows = h.shape[0]

    def body(h_ref, t_ref, g_ref, dh_ref, pg_ref, ls_ref):
        @pl.when(pl.program_id(0) == 0)
        def _():
            pg_ref[...] = jnp.zeros_like(pg_ref)
            ls_ref[...] = jnp.zeros_like(ls_ref)
        hv = h_ref[...]
        g = g_ref[...]
        r = lax.rsqrt(jnp.mean(hv * hv, axis=-1, keepdims=True) + EPS)
        n = hv * r
        e = n * g - t_ref[...]
        ls_ref[...] += 0.5 * jnp.sum(jnp.sum(e * e, axis=1, keepdims=True), axis=0, keepdims=True) * (1.0 / D)
        dyv = e * (1.0 / D)
        pg_ref[0:1] += _sum0(dyv * n)
        dn = dyv * g
        dh_ref[...] = r * (dn - n * jnp.mean(dn * n, axis=-1, keepdims=True))

    return _call(body, "loss_bwd", (n_rows // T,), [_rows(T, D), _rows(T, D), _full(1, D)],
                 [_rows(T, D), _full(8, D), _full(8, LANE)],
                 [_sds((n_rows, D)), _sds((8, D)), _sds((8, LANE))])(h, tgt, fng)


def _pad_rows(a, rows):
    return jnp.concatenate([a, jnp.zeros((rows - a.shape[0],) + a.shape[1:], a.dtype)], axis=0)


def _mp(*rows):
    return _pad_rows(jnp.stack(rows, axis=0), 8)


def _local_step(x, ctx, tgt, ada, cada0, w, late_w=None, early_grads=None, small_grads=None):
    n_lat, n_ctx = x.shape[0], ctx.shape[0]
    nct, ncc = n_ctx // T, n_ctx // Q
    a0 = [ada[0, k * D:(k + 1) * D] for k in range(6)]
    a1 = [ada[1, k * D:(k + 1) * D] for k in range(6)]
    c0 = [cada0[k * D:(k + 1) * D] for k in range(6)]
    g = {}

    mp2 = jnp.stack([_mp(w["norm_mix_g"][0], c0[0], c0[1]), _mp(w["norm_mix_g"][0], a0[0], a0[1], a0[2])], axis=0)
    mp_l0 = mp2[1]
    sgbt = _pad_cols(w["sg_b"][0].T, LANE)
    lng, lnb = w["sg_ln_g"][0][None], w["sg_ln_b"][0][None]
    z, uv, xbcp, dtr, ysg = _hy1_fwd(ctx, x, mp2, w["wz"], w["wuv"], w["wxbc"], w["wdt"], lng, lnb, w["sg_w"], sgbt, nct)
    cw = _pad_rows(w["ssd_conv_w"][0], 8)
    cb = w["ssd_conv_b"][0][None]
    xbc = _cv5_fwd(xbcp, cw, cb, nct)
    sp = _pad_rows(jnp.stack([_pad_cols(w["ssd_dt_bias"][0].reshape(1, 32), LANE)[0],
                              _pad_cols(w["ssd_a_log"][0].reshape(1, 32), LANE)[0]], axis=0), 8)
    dl = jnp.repeat(w["ssd_d"][0], 64)[None]
    ng = w["ssd_norm_g"][0][None]
    yf, hin_f = _ssd_fwd(xbc, dtr, sp, ncc, False)
    yb, hin_b = _ssd_fwd(xbc, dtr, sp, ncc, True)
    if late_w is not None:
        w = {**w, **late_w(yb)}
    h1, yssd, out0 = _hy4_fwd(x, yf, yb, xbc, z, ysg, mp_l0, dl, ng, w["hy_w_out"], nct)

    mpm0 = _mp(w["norm_mlp_g"][0], a0[3], a0[4], a0[5])
    h2, am0, ym0 = _mlp_fwd(h1, mpm0, w["wpack"], 0, "mlp0_fwd")

    mpc = _mp(w["norm_mix_g"][1], a1[0], a1[1], a1[2])
    wdw = _pad_rows(w["cf_w_dw"][0], 32)
    glu, acf = _cf1_fwd(h2, mpc, w["cf_w_pw1"], w["cf_b_pw1"])
    h3, cv, scf, ycf = _cf2_fwd(h2, glu, mpc, wdw, w["cf_b_dw"], w["cf_ln_g"], w["cf_ln_b"], w["cf_w_pw2"], w["cf_b_pw2"])

    mpm1 = _mp(w["norm_mlp_g"][1], a1[3], a1[4], a1[5])
    h4, am1, ym1 = _mlp_fwd(h3, mpm1, w["wpack"], 1, "mlp1_fwd")

    dh4, pg_f, ls = _loss_bwd(h4, tgt, w["final_norm_g"][None])
    loss = ls[0, 0]
    g["final_norm_g"] = pg_f[0]

    gp = {}
    dh3, da1, dy1, u1, pgm1 = _mlp_bwd(dh4, h3, am1, ym1, mpm1, w["wpack"], 1, "mlp1_bwd")
    gw1_1 = _tn(u1, da1, "tn_mlp1_w1", shard=("col", 1024))
    gw2_1 = _tn(am1, dy1, "tn_mlp1_w2", relu2=True, shard=("row", 1024))

    dcv, dycf, pgc2 = _cf2_bwd(dh3, ycf, cv, mpc, w["cf_ln_g"], w["cf_ln_b"], w["cf_w_pw2"])
    gp["cf_w_pw2"] = _tn(scf, dycf, "tn_cf_pw2", shard=("row", 256))
    dh2, dacf, ucf, pgc1, pbc1, dwdw = _cf1_bwd(dh3, h2, acf, dcv, glu, mpc, wdw, w["cf_w_pw1"])
    gp["cf_w_pw1"] = _tn(ucf, dacf, "tn_cf_pw1", shard=("col", 512)).reshape(4, 512, 1024)
    g["cf_b_pw2"], g["cf_ln_g"], g["cf_ln_b"], g["cf_b_dw"] = pgc2[1], pgc2[2], pgc2[3], pgc2[4]
    g["cf_b_pw1"] = pbc1[0]
    g["cf_w_dw"] = dwdw[:CK]

    dh1, da0, dy0, u0, pgm0 = _mlp_bwd(dh2, h1, am0, ym0, mpm0, w["wpack"], 0, "mlp0_bwd")
    gp["mlp_w1"] = jnp.concatenate([_tn(u0, da0, "tn_mlp0_w1", shard=("col", 1024)), gw1_1], axis=1)
    gp["mlp_w2"] = jnp.concatenate([_tn(am0, dy0, "tn_mlp0_w2", relu2=True, shard=("row", 1024)), gw2_1], axis=1)
    g["norm_mlp_g"] = jnp.stack([pgm0[0], pgm1[0]])

    dyt, dz, dysg, doutb, pg4 = _hy4_bwd(dh1, out0, yf, yb, xbc, z, mp_l0, dl, ng, w["hy_w_out"], nct)
    gp["hy_w_out"] = jnp.concatenate([_tn(yssd, doutb, "tn_out_ssd", shard=("row", 512)),
                                      _tn(ysg, doutb, "tn_out_sg", shard=("row", 512))], axis=0)
    if early_grads is not None:
        sp = sp + early_grads(gp)
    head_of_lane = jnp.arange(D, dtype=jnp.int32)[:, None] // 64
    col = jnp.arange(LANE, dtype=jnp.int32)[None, :]
    dxf, ddf, pgsf = _ssd_bwd(xbc, dtr, dyt, hin_f, sp, dl, (col == head_of_lane).astype(F32), ncc, False)
    dxb, ddb, pgsb = _ssd_bwd(xbc, dtr, dyt, hin_b, sp, dl, (col == head_of_lane + 16).astype(F32), ncc, True)
    dcv5, pgcb = _cv5_bwd1(xbcp, dxf, dxb, cw, cb, nct)
    dxbcp, dcw = _cv5_bwd2(dcv5, xbcp, cw, nct)
    dx, ucat, duv, ddt, pg2, pln, dsgw, dsgbt = _hy1_bwd(
        ctx, x, uv, dz, dxbcp, ddf, ddb, dysg, dh1, mp2, w["wz"], w["wuv"], w["wxbc"], w["wdt"], lng, lnb, w["sg_w"], sgbt, nct)
    g["ssd_conv_w"], g["ssd_conv_b"] = dcw[:SK], pgcb[0]
    pgs = pgsf + pgsb
    g["ssd_dt_bias"], g["ssd_a_log"] = pgs[0, :32].reshape(2, 16), pgs[1, :32].reshape(2, 16)
    g["ssd_d"] = jnp.sum(pg4[2].reshape(16, 64), axis=1)
    g["ssd_norm_g"] = pg4[1]
    g["sg_ln_g"], g["sg_ln_b"] = pln[0], pln[1]
    g["sg_w"], g["sg_b"] = dsgw, dsgbt[:, :SGG].T
    g["norm_mix_g"] = jnp.stack([pg2[0, 0] + pg2[1, 0], pgc1[0]])

    zero = jnp.zeros((D,), F32)
    d_ada = jnp.stack([jnp.concatenate([pg2[1, 1], pg2[1, 2], pg4[0], pgm0[1], pgm0[2], pgm0[3]]),
                       jnp.concatenate([pgc1[1], pgc1[2], pgc2[0], pgm1[1], pgm1[2], pgm1[3]])])
    d_cada0 = jnp.concatenate([pg2[0, 1], pg2[0, 2], zero, zero, zero, zero])
    if small_grads is not None:
        ucat, _ = lax.optimization_barrier((ucat, small_grads(g, d_ada, d_cada0)))
    gp["hy_w_in"] = _win_join(_tn(ucat, dz, "tn_in_z"), _tn(ucat, dxbcp, "tn_in_xbc"), _tn(ucat, ddt, "tn_in_dt"),
                              _tn(ucat, duv, "tn_in_uv"))
    g["pieces"] = gp
    return loss, dx, g, d_ada, d_cada0


def _pad_cols(a, cols):
    return jnp.concatenate([a, jnp.zeros(a.shape[:-1] + (cols - a.shape[-1],), a.dtype)], axis=-1)


MESH = pl.DeviceIdType.MESH
ANY = pl.BlockSpec(memory_space=pl.ANY)
IN_VMEM = pl.BlockSpec(memory_space=pltpu.VMEM)


def _coords():
    return lax.axis_index("x"), lax.axis_index("y"), lax.axis_index("c")


def _ag8(x, name):
    r, wd = x.shape

    def body(x_ref, o_ref, send, recv, lsem):
        mx, my, mc = _coords()
        me = 4 * mx + 2 * my + mc
        mine = pltpu.make_async_copy(x_ref, o_ref.at[me], lsem)
        mine.start()
        sent, peers = [], []
        for k in range(1, 8):
            px = 1 - mx if k & 4 else mx
            py = 1 - my if k & 2 else my
            pc = 1 - mc if k & 1 else mc
            cp = pltpu.make_async_remote_copy(src_ref=x_ref, dst_ref=o_ref.at[me], send_sem=send.at[k - 1],
                                              recv_sem=recv.at[k - 1], device_id=(px, py, pc), device_id_type=MESH)
            cp.start()
            sent.append(cp)
            peers.append((4 * px + 2 * py + pc, (px, py, pc)))
        for k in range(1, 8):
            slot, peer = peers[k - 1]
            pltpu.make_async_remote_copy(src_ref=x_ref, dst_ref=o_ref.at[slot], send_sem=send.at[k - 1],
                                         recv_sem=recv.at[k - 1], device_id=peer, device_id_type=MESH).wait_recv()
        for cp in sent:
            cp.wait_send()
        mine.wait()

    return pl.pallas_call(
        body, name=name, out_shape=_sds((8, r, wd), x.dtype), in_specs=[IN_VMEM], out_specs=IN_VMEM,
        scratch_shapes=[pltpu.SemaphoreType.DMA((7,)), pltpu.SemaphoreType.DMA((7,)), pltpu.SemaphoreType.DMA(())],
        compiler_params=pltpu.CompilerParams(vmem_limit_bytes=VMEM_LIMIT))(x)


HBM = pl.BlockSpec(memory_space=pltpu.HBM)
SEM = pl.BlockSpec(memory_space=pltpu.SEMAPHORE)
EFFECT = pltpu.SideEffectType.DATAFLOW_SIDE_EFFECTING


def _x4_peers(in_ref, land_ref, send, recv, a2a):
    mx, my, mc = _coords()
    me = 2 * mx + my
    out = []
    for k in range(1, 4):
        px = 1 - mx if k & 2 else mx
        py = 1 - my if k & 1 else my
        pj = 2 * px + py
        mk = functools.partial(pltpu.make_async_remote_copy, src_ref=in_ref.at[pj] if a2a else in_ref,
                               send_sem=send.at[k - 1], recv_sem=recv.at[k - 1], device_id=(px, py, mc), device_id_type=MESH)
        out.append((mk(dst_ref=land_ref.at[me]), mk(dst_ref=land_ref.at[pj])))
    return out


def _x4_start(buf, name, a2a):
    r, wd = buf.shape[-2:]

    def body(in_ref, land_ref, send, recv, in_thru, land_thru, token):
        for start, _ in _x4_peers(in_ref, land_ref, send, recv, a2a):
            start.start()
        token[...] = jnp.zeros_like(token)

    land = lax.empty((4, r, wd), buf.dtype)
    return pl.pallas_call(
        body, name=name,
        out_shape=(pltpu.SemaphoreType.DMA((3,)), pltpu.SemaphoreType.DMA((3,)), pltpu.HBM(buf.shape, buf.dtype),
                   pltpu.HBM(land.shape, land.dtype), _sds((8, LANE))),
        in_specs=(HBM, HBM), out_specs=(SEM, SEM, HBM, HBM, IN_VMEM), input_output_aliases={0: 2, 1: 3},
        compiler_params=pltpu.CompilerParams(has_side_effects=EFFECT),
    )(pltpu.with_memory_space_constraint(buf, pltpu.HBM), pltpu.with_memory_space_constraint(land, pltpu.HBM))


def _x4_wait(send, recv, buf_thru, land_thru, after, name, a2a):
    def body(in_ref, land_ref, send_ref, recv_ref, after_ref, in_dead, got_ref):
        for _, arrive in _x4_peers(in_ref, land_ref, send_ref, recv_ref, a2a):
            arrive.wait_send()
            arrive.wait_recv()

    return pl.pallas_call(
        body, name=name, out_shape=(pltpu.HBM(buf_thru.shape, buf_thru.dtype), pltpu.HBM(land_thru.shape, land_thru.dtype)),
        in_specs=(HBM, HBM, SEM, SEM, ANY), out_specs=(HBM, HBM), input_output_aliases={0: 0, 1: 1},
        compiler_params=pltpu.CompilerParams(has_side_effects=EFFECT),
    )(buf_thru, land_thru, send, recv, after)


def _ag8_peers(x_ref, land_ref, send, recv):
    mx, my, mc = _coords()
    me = 4 * mx + 2 * my + mc
    out = []
    for k in range(1, 8):
        px = 1 - mx if k & 4 else mx
        py = 1 - my if k & 2 else my
        pc = 1 - mc if k & 1 else mc
        mk = functools.partial(pltpu.make_async_remote_copy, src_ref=x_ref, send_sem=send.at[k - 1], recv_sem=recv.at[k - 1],
                               device_id=(px, py, pc), device_id_type=MESH)
        out.append((mk(dst_ref=land_ref.at[me]), mk(dst_ref=land_ref.at[4 * px + 2 * py + pc])))
    return out


def _split_start(x, land_shape, peers, n_copies, name):
    def body(x_ref, land_ref, send, recv, x_thru, land_thru, token):
        for start, _ in peers(x_ref, land_ref, send, recv):
            start.start()
        token[...] = jnp.zeros_like(token)

    land = lax.empty(land_shape, x.dtype)
    return pl.pallas_call(
        body, name=name,
        out_shape=(pltpu.SemaphoreType.DMA((n_copies,)), pltpu.SemaphoreType.DMA((n_copies,)), pltpu.HBM(x.shape, x.dtype),
                   pltpu.HBM(land.shape, land.dtype), _sds((8, LANE))),
        in_specs=(HBM, HBM), out_specs=(SEM, SEM, HBM, HBM, IN_VMEM), input_output_aliases={0: 2, 1: 3},
        compiler_params=pltpu.CompilerParams(has_side_effects=EFFECT),
    )(pltpu.with_memory_space_constraint(x, pltpu.HBM), pltpu.with_memory_space_constraint(land, pltpu.HBM))


def _split_wait(handle, after, peers, name):
    send, recv, x_thru, land_thru, _ = handle

    def body(x_ref, land_ref, send_ref, recv_ref, after_ref, x_dead, got_ref):
        for _, arrive in peers(x_ref, land_ref, send_ref, recv_ref):
            arrive.wait_send()
            arrive.wait_recv()

    return pl.pallas_call(
        body, name=name, out_shape=(pltpu.HBM(x_thru.shape, x_thru.dtype), pltpu.HBM(land_thru.shape, land_thru.dtype)),
        in_specs=(HBM, HBM, SEM, SEM, ANY), out_specs=(HBM, HBM), input_output_aliases={0: 0, 1: 1},
        compiler_params=pltpu.CompilerParams(has_side_effects=EFFECT),
    )(x_thru, land_thru, send, recv, after)


def _sib_peers(x_ref, land_ref, send, recv):
    mx, my, mc = _coords()
    cp = pltpu.make_async_remote_copy(src_ref=x_ref, dst_ref=land_ref, send_sem=send.at[0], recv_sem=recv.at[0],
                                      device_id=(mx, my, 1 - mc), device_id_type=MESH)
    return [(cp, cp)]


def _xchg_sib(x, name):
    def body(in_ref, o_ref, send, recv):
        mx, my, mc = _coords()
        cp = pltpu.make_async_remote_copy(src_ref=in_ref, dst_ref=o_ref, send_sem=send, recv_sem=recv,
                                          device_id=(mx, my, 1 - mc), device_id_type=MESH)
        cp.start()
        cp.wait_recv()
        cp.wait_send()

    return pl.pallas_call(
        body, name=name, out_shape=_sds(x.shape, x.dtype), in_specs=[ANY], out_specs=ANY,
        scratch_shapes=[pltpu.SemaphoreType.DMA(()), pltpu.SemaphoreType.DMA(())])(x)


def _sum_slots(gat, slots, name, tr=None):
    n, r, wd = gat.shape
    tr = r if tr is None else tr

    def body(g_ref, o_ref):
        acc = g_ref[slots[0]].astype(F32)
        for s in slots[1:]:
            acc = acc + g_ref[s].astype(F32)
        o_ref[...] = acc

    return _call(body, name, (r // tr,), [pl.BlockSpec((n, tr, wd), lambda i: (0, i, 0))], _rows(tr, wd), _sds((r, wd)))(gat)


def _add(a, b, name, tr):
    def body(a_ref, b_ref, o_ref):
        o_ref[...] = a_ref[...] + b_ref[...]

    r, wd = a.shape
    return _call(body, name, (r // tr,), [_rows(tr, wd), _rows(tr, wd)], _rows(tr, wd), _sds((r, wd)))(a, b)


def _ada_fwd(x16, ada_w_loc, ada_b_loc):
    nloc = ada_w_loc.shape[-1]

    def body(x_ref, w_ref, b_ref, s_ref, o_ref):
        s = _silu(x_ref[...])
        s_ref[...] = s
        o_ref[...] = jnp.dot(s, w_ref[...], precision=HI, preferred_element_type=F32) + b_ref[...]

    return _call(body, "ada_fwd", (2,),
                 [_full(16, D), pl.BlockSpec((None, D, nloc), lambda l: (l, 0, 0)), pl.BlockSpec((None, 1, nloc), lambda l: (l, 0, 0))],
                 [_full(16, D), pl.BlockSpec((None, 16, nloc), lambda l: (l, 0, 0))],
                 [_sds((16, D)), _sds((2, 16, nloc))])(x16, ada_w_loc, ada_b_loc[:, None, :])


def _ada_bwd(s16, d_loc, ada_w_loc):
    nloc = ada_w_loc.shape[-1]

    def body(s_ref, d_ref, w_ref, gw_ref, cp_ref):
        gw_ref[...] = lax.dot_general(s_ref[...], d_ref[...], (((0,), (0,)), ((), ())), precision=HI,
                                      preferred_element_type=F32)

        @pl.when(pl.program_id(0) == 0)
        def _():
            cp_ref[...] = lax.dot_general(d_ref[8:16, :], w_ref[...], (((1,), (1,)), ((), ())), precision=HI,
                                          preferred_element_type=F32)

    return _call(body, "ada_bwd", (2,),
                 [_full(16, D), pl.BlockSpec((None, 16, nloc), lambda l: (l, 0, 0)), pl.BlockSpec((None, D, nloc), lambda l: (l, 0, 0))],
                 [pl.BlockSpec((None, D, nloc), lambda l: (l, 0, 0)), _full(8, D)],
                 [_sds((2, D, nloc)), _sds((8, D))])(s16, d_loc, ada_w_loc)


def _cctx_grad(dscc, c_ctx):
    def body(d_ref, c_ref, o_ref):
        _, vjp = jax.vjp(_silu, c_ref[...])
        o_ref[...] = vjp(d_ref[...])[0]

    return _call(body, "cctx_grad", (1,), [_full(8, D), _full(8, D)], _full(8, D), _sds((8, D)))(dscc, c_ctx)


def _adamw_math(w, g, m, v):
    mn = ADAM_B1 * m + (1.0 - ADAM_B1) * g
    vn = ADAM_B2 * v + (1.0 - ADAM_B2) * jnp.square(g)
    c1 = 1.0 - ADAM_B1 ** ADAM_STEP
    c2 = 1.0 - ADAM_B2 ** ADAM_STEP
    return -ADAM_LR * ((mn / c1) / (jnp.sqrt(vn / c2) + ADAM_EPS) + ADAM_WD * w), mn, vn


def _adamw(w, g, m, v, name):
    n_l, r, wd = w.shape
    tr = 256 if r % 256 == 0 else r

    def body(w_ref, g_ref, m_ref, v_ref, d_ref, mo_ref, vo_ref):
        d_ref[...], mo_ref[...], vo_ref[...] = _adamw_math(w_ref[...], g_ref[...], m_ref[...], v_ref[...])

    spec = pl.BlockSpec((None, tr, wd), lambda a, i: (a, i, 0))
    return tuple(_call(body, name, (n_l, r // tr), [spec] * 4, [spec] * 3, [_sds(w.shape)] * 3)(w, g, m, v))


def _adamw_rows(w, part, sib, m, v, r0, name):
    rows = w.shape[0]
    tr = 256

    def body(w_ref, p_ref, s_ref, m_ref, v_ref, g_ref, d_ref, mo_ref, vo_ref):
        g = p_ref[...] + s_ref[...]
        g_ref[...] = g
        d_ref[...], mo_ref[...], vo_ref[...] = _adamw_math(w_ref[...], g, m_ref[...], v_ref[...])

    here, there = _rows(tr, ROW), _rows(tr, ROW, r0 // tr)
    return tuple(_call(body, name, (rows // tr,), [here, there, there, here, here], [here] * 4, [_sds(w.shape)] * 4)(
        w, part, sib, m, v))


def _adamw_small(ws, gs, ms, vs, name):
    n = len(ws)
    shapes = [a.shape for a in ws]
    as2d = lambda a: a.reshape(-1, a.shape[-1])

    def body(*refs):
        ins, outs = refs[:4 * n], refs[4 * n:]
        for k in range(n):
            res = _adamw_math(ins[k][...], ins[n + k][...], ins[2 * n + k][...], ins[3 * n + k][...])
            for j in range(3):
                outs[j * n + k][...] = res[j]

    flat = [as2d(a) for group in (ws, gs, ms, vs) for a in group]
    specs = [_full(*a.shape) for a in flat]
    outs = _call(body, name, (1,), specs, specs[:n] * 3, [_sds(a.shape) for a in flat[:n]] * 3)(*flat)
    return tuple([outs[j * n + k].reshape(shapes[k]) for k in range(n)] for j in range(3))


ROW = 1024


def _nrows(size):
    return -(-size // ROW)


def _pack(arrs, rows_total, dtype=F32):
    parts = []
    for a in arrs:
        flat = a.reshape(-1).astype(dtype)
        pad = _nrows(flat.shape[0]) * ROW - flat.shape[0]
        parts.append(flat if pad == 0 else jnp.concatenate([flat, jnp.zeros((pad,), dtype)]))
    flat = jnp.concatenate(parts)
    out = flat.reshape(-1, ROW)
    return _pad_rows(out, rows_total)


def _unpack(buf, shapes):
    lead = buf.shape[:-2]
    out, r0 = [], 0
    for shp in shapes:
        size = 1
        for s in shp:
            size *= s
        nr = _nrows(size)
        piece = lax.slice_in_dim(buf, r0, r0 + nr, axis=len(lead))
        out.append(piece.reshape(lead + (nr * ROW,))[..., :size].reshape(lead + tuple(shp)))
        r0 += nr
    return out


SLOT = 16


def _slot_rows(size):
    return _round_up(size // ROW, SLOT)


def _pack_rows(arrs, rows_total, dtype):
    parts, used = [], 0
    for a in arrs:
        part = a.astype(dtype).reshape(-1, ROW)
        extra = _slot_rows(a.size) - part.shape[0]
        parts.append(part if extra == 0 else jnp.pad(part, ((0, extra), (0, 0))))
        used += _slot_rows(a.size)
    if rows_total > used:
        parts.append(jnp.zeros((rows_total - used, ROW), dtype))
    return jnp.concatenate(parts, axis=0)


def _unpack_rows(buf, shapes):
    lead = buf.shape[:-2]
    out, r0 = [], 0
    for shp in shapes:
        size = 1
        for s in shp:
            size *= s
        piece = lax.slice_in_dim(buf, r0, r0 + size // ROW, axis=len(lead))
        out.append(piece.reshape(lead + tuple(shp)))
        r0 += _slot_rows(size)
    return out


def _round_up(n, k):
    return -(-n // k) * k


WEIGHTS = ['c_ctx', 'ada_w', 'ada_b', 'norm_mix_g', 'norm_mlp_g', 'mlp_w1', 'mlp_w2', 'hy_w_in', 'ssd_conv_w', 'ssd_conv_b',
           'ssd_dt_bias', 'ssd_a_log', 'ssd_d', 'ssd_norm_g', 'sg_ln_g', 'sg_ln_b', 'sg_w', 'sg_b', 'hy_w_out', 'cf_w_pw1',
           'cf_b_pw1', 'cf_w_dw', 'cf_b_dw', 'cf_ln_g', 'cf_ln_b', 'cf_w_pw2', 'cf_b_pw2', 'final_norm_g']
BIG = {'mlp_w1': 2, 'mlp_w2': 1, 'hy_w_in': 2, 'hy_w_out': 1, 'cf_w_pw1': 2, 'cf_w_pw2': 1}
SMALL_SHARD = ['ssd_conv_w', 'cf_b_pw1', 'cf_w_dw', 'cf_b_dw', 'cf_ln_g', 'cf_ln_b', 'cf_b_pw2']
REP = ['norm_mix_g', 'norm_mlp_g', 'ssd_conv_b', 'ssd_dt_bias', 'ssd_a_log', 'ssd_d', 'ssd_norm_g', 'sg_ln_g', 'sg_ln_b',
       'sg_w', 'sg_b', 'final_norm_g']


def _gather_shards(stacked, axis):
    return jnp.concatenate([stacked[j] for j in range(4)], axis=axis)


def kernel(x, c, ctx, c_ctx, ada_w, ada_b, norm_mix_g, norm_mlp_g, mlp_w1, mlp_w2, hy_w_in, ssd_conv_w, ssd_conv_b, ssd_dt_bias, ssd_a_log, ssd_d, ssd_norm_g, sg_ln_g, sg_ln_b, sg_w, sg_b, hy_w_out, cf_w_pw1, cf_b_pw1, cf_w_dw, cf_b_dw, cf_ln_g, cf_ln_b, cf_w_pw2, cf_b_pw2, final_norm_g, loss_target, m_c_ctx, m_ada_w, m_ada_b, m_norm_mix_g, m_norm_mlp_g, m_mlp_w1, m_mlp_w2, m_hy_w_in, m_ssd_conv_w, m_ssd_conv_b, m_ssd_dt_bias, m_ssd_a_log, m_ssd_d, m_ssd_norm_g, m_sg_ln_g, m_sg_ln_b, m_sg_w, m_sg_b, m_hy_w_out, m_cf_w_pw1, m_cf_b_pw1, m_cf_w_dw, m_cf_b_dw, m_cf_ln_g, m_cf_ln_b, m_cf_w_pw2, m_cf_b_pw2, m_final_norm_g, v_c_ctx, v_ada_w, v_ada_b, v_norm_mix_g, v_norm_mlp_g, v_mlp_w1, v_mlp_w2, v_hy_w_in, v_ssd_conv_w, v_ssd_conv_b, v_ssd_dt_bias, v_ssd_a_log, v_ssd_d, v_ssd_norm_g, v_sg_ln_g, v_sg_ln_b, v_sg_w, v_sg_b, v_hy_w_out, v_cf_w_pw1, v_cf_b_pw1, v_cf_w_dw, v_cf_b_dw, v_cf_ln_g, v_cf_ln_b, v_cf_w_pw2, v_cf_b_pw2, v_final_norm_g):
    args = locals()
    wl = {n: args[n] for n in WEIGHTS}
    ml = {n: args["m_" + n] for n in WEIGHTS}
    vl = {n: args["v_" + n] for n in WEIGHTS}
    mx, my, mc = _coords()
    me = 4 * mx + 2 * my + mc
    shard = 2 * mx + my
    even = (0, 2, 4, 6)

    def start_gather(names, name, tie=None):
        rows = sum(_slot_rows(wl[n].size) for n in names)
        buf = _pack_rows([wl[n] for n in names], rows, MXU)
        if tie is not None:
            buf, _ = lax.optimization_barrier((buf, tie))
        return _x4_start(buf, name, a2a=False)

    def finish_gather(handle, names, after, name):
        send, recv, own, land, _ = handle
        own, land = _x4_wait(send, recv, own, land, after, name, a2a=False)
        got = lax.dynamic_update_slice(land, own[None], (shard, 0, 0))
        shapes = [wl[n].shape for n in names]
        wfull = {n: _gather_shards(st, BIG[n]) for n, st in zip(names, _unpack_rows(got, shapes))}
        return wfull, got

    rest_names = ["mlp_w1", "mlp_w2", "hy_w_out", "cf_w_pw1", "cf_w_pw2"]
    h_in = _x4_start(hy_w_in[0].astype(MXU), "agw_in_start", a2a=False)
    c = c + h_in[4][0, 0]

    small_shapes = [wl[n].shape for n in SMALL_SHARD]
    blk1 = _pack([c] + [wl[n] for n in SMALL_SHARD], 24)
    got1 = _ag8(blk1, "ag_cond")
    x16 = _pad_rows(jnp.concatenate([got1[:, 0, :], c_ctx[None]], axis=0), 16)
    small_full = {}
    for n, parts in zip(SMALL_SHARD, _unpack(got1[:, 1:, :], small_shapes)):
        small_full[n] = jnp.concatenate([parts[s] for s in even], axis=-1)

    nloc = ada_w.shape[-1]
    ada_b_loc = lax.dynamic_slice_in_dim(ada_b, shard * nloc, nloc, axis=1)
    s16, ada_loc = _ada_fwd(x16, ada_w, ada_b_loc)
    got2 = _ag8(ada_loc.reshape(32, nloc), "ag_ada").reshape(8, 2, 16, nloc)
    ada_full = jnp.concatenate([got2[s] for s in even], axis=-1)
    ada_me = lax.dynamic_slice_in_dim(ada_full, me, 1, axis=1)[:, 0, :]
    cada0 = ada_full[0, 8, :]

    w = {n: wl[n] for n in WEIGHTS if n not in BIG and n not in SMALL_SHARD}
    w.update(small_full)
    h_rest = start_gather(rest_names, "agw_rest_start", tie=ada_me)
    send, recv, own, land, _ = h_in
    own, land = _x4_wait(send, recv, own, land, h_rest[4], "agw_in_wait", a2a=False)
    w["wz"], w["wxbc"], w["wdt"], w["wuv"] = _win_split(lax.dynamic_update_slice(land, own[None], (shard, 0, 0)))
    w["sg_w"] = sg_w[0].astype(MXU)

    def late_w(after):
        wfull, got = finish_gather(h_rest, rest_names, after, "agw_rest_wait")
        return {"hy_w_out": wfull["hy_w_out"][0], "wpack": got,
                "cf_w_pw1": wfull["cf_w_pw1"][0], "cf_w_pw2": wfull["cf_w_pw2"][0]}

    full_shape = {n: wl[n].shape for n in WEIGHTS}
    for n in BIG:
        full_shape[n] = tuple(s * 4 if a == BIG[n] else s for a, s in enumerate(wl[n].shape))
    for n in SMALL_SHARD:
        full_shape[n] = wl[n].shape[:-1] + (wl[n].shape[-1] * 4,)

    early_names = ["mlp_w1", "mlp_w2", "cf_w_pw1", "cf_w_pw2", "hy_w_out"]
    early = {}

    def early_grads(gp):
        used = sum(gp[n].shape[1] for n in early_names)
        parts = [gp[n] for n in early_names] + [jnp.zeros((4, _round_up(used, 512) - used, ROW), jnp.bfloat16)]
        early["h"] = _x4_start(jnp.concatenate(parts, axis=1), "a2a_early_start", a2a=True)
        return early["h"][4][0, 0]

    sm_names = REP + SMALL_SHARD
    r_ada = sum(_nrows(wl[n].size * (4 if n in SMALL_SHARD else 1)) for n in sm_names)
    small = {}

    def small_grads(g_, d_ada_, d_cada0_):
        buf = _pack([g_[n] for n in sm_names] + [d_ada_, d_cada0_], _round_up(r_ada + 18, 16), jnp.bfloat16)
        small["h"] = _split_start(buf, (8,) + buf.shape, _ag8_peers, 7, "ag_small_start")
        return small["h"][4][0, 0]

    loss_part, dx, g, d_ada, d_cada0 = _local_step(x[0], ctx[0], loss_target[0], ada_me, cada0, w, late_w, early_grads,
                                                   small_grads)
    gp_last = g["pieces"]["hy_w_in"]

    delta, new_m, new_v, grads = {}, {}, {}, {}
    h_last = _x4_start(gp_last, "a2a_last_start", a2a=True)
    send, recv, own, land, _ = early["h"]
    own, land = _x4_wait(send, recv, own, land, h_last[4], "a2a_early_wait", a2a=True)
    got = lax.dynamic_update_slice(land, lax.dynamic_slice_in_dim(own, shard, 1, axis=0), (shard, 0, 0))
    part_early = _sum_slots(got, (0, 1, 2, 3), "sum_grads_early", tr=512)
    h_swap = _split_start(part_early, part_early.shape, _sib_peers, 1, "swap_early_start")

    small_in, land3 = _split_wait(small["h"], h_swap[4], _ag8_peers, "ag_small_wait")
    got3 = lax.dynamic_update_slice(land3, small_in[None], (me, 0, 0))
    tot3 = _sum_slots(got3, tuple(range(8)), "sum_small")
    sm_tot = _unpack(tot3, [full_shape[n] for n in sm_names] + [(2, 6 * D), (6 * D,)])
    grads.update(zip(sm_names, sm_tot[:-2]))
    for n in SMALL_SHARD:
        k = wl[n].shape[-1]
        grads[n] = lax.dynamic_slice_in_dim(grads[n], shard * k, k, axis=grads[n].ndim - 1)
    dada_tot, dcada_tot = sm_tot[-2], sm_tot[-1]
    grads["ada_b"] = dada_tot.at[0].add(dcada_tot)
    dada_all = got3[:, r_ada:r_ada + 12, :].astype(F32).reshape(8, 2, 6 * D)
    d16 = jnp.concatenate([jnp.transpose(dada_all, (1, 0, 2)),
                           jnp.stack([dcada_tot, jnp.zeros_like(dcada_tot)])[:, None, :],
                           jnp.zeros((2, 7, 6 * D), F32)], axis=1)
    d_loc = lax.dynamic_slice_in_dim(d16, shard * nloc, nloc, axis=2)
    grads["ada_w"], cpart = _ada_bwd(s16, d_loc, ada_w)
    cpart = lax.dynamic_update_slice(cpart, jnp.full((1, D), loss_part, F32), (1, 0))
    h_cctx = _split_start(cpart, (8,) + cpart.shape, _ag8_peers, 7, "ag_cctx_start")
    grads["ada_w"], _ = lax.optimization_barrier((grads["ada_w"], h_cctx[4]))

    delta["ada_w"], new_m["ada_w"], new_v["ada_w"] = _adamw(ada_w, grads["ada_w"], ml["ada_w"], vl["ada_w"], "adamw_ada_w")
    small_names = ["ada_b"] + REP + SMALL_SHARD
    outs = _adamw_small(*([src[n].reshape(wl[n].shape) for n in small_names] for src in (wl, grads, ml, vl)), "adamw_small")
    for dst, vals in zip((delta, new_m, new_v), outs):
        dst.update(zip(small_names, vals))

    def step(n, tot):
        grads[n] = tot.reshape(wl[n].shape)
        delta[n], new_m[n], new_v[n] = _adamw(wl[n], grads[n], ml[n], vl[n], "adamw_" + n)

    part_early, sib_early = _split_wait(h_swap, delta["ada_w"], _sib_peers, "swap_early_wait")
    r0 = 0
    for n in early_names:
        nr, shp = _slot_rows(wl[n].size), wl[n].shape
        if shp[-1] == ROW:
            outs = _adamw_rows(wl[n].reshape(-1, ROW), part_early, sib_early, ml[n].reshape(-1, ROW), vl[n].reshape(-1, ROW),
                               r0, "adamw_" + n)
            grads[n], delta[n], new_m[n], new_v[n] = (o.reshape(shp) for o in outs)
        else:
            step(n, _add(lax.slice_in_dim(part_early, r0, r0 + nr), lax.slice_in_dim(sib_early, r0, r0 + nr), "add_" + n, nr))
        r0 += nr
    send, recv, own, land, _ = h_last
    own, land = _x4_wait(send, recv, own, land, delta["mlp_w1"], "a2a_last_wait", a2a=True)
    got = lax.dynamic_update_slice(land, lax.dynamic_slice_in_dim(own, shard, 1, axis=0), (shard, 0, 0))
    part_last = _sum_slots(got, (0, 1, 2, 3), "sum_grads_last", tr=512)
    step("hy_w_in", _add(part_last, _xchg_sib(part_last, "swap_grads_last"), "add_grads_last", 512))

    c_in, land4 = _split_wait(h_cctx, delta["hy_w_in"], _ag8_peers, "ag_cctx_wait")
    got4 = lax.dynamic_update_slice(land4, c_in[None], (me, 0, 0))
    loss = _sum_slots(got4, tuple(range(8)), "sum_loss")[1, 0]
    grads["c_ctx"] = _cctx_grad(_sum_slots(got4, even, "sum_cctx"), _pad_rows(c_ctx[None], 8))[0]
    outs = _adamw_small(*([src["c_ctx"]] for src in (wl, grads, ml, vl)), "adamw_c_ctx")
    delta["c_ctx"], new_m["c_ctx"], new_v["c_ctx"] = (o[0] for o in outs)

    return (loss, dx[None], *[grads[n].reshape(wl[n].shape) for n in WEIGHTS], *[delta[n] for n in WEIGHTS],
            *[new_m[n] for n in WEIGHTS], *[new_v[n] for n in WEIGHTS])
```

```python
import functools

import jax
import jax.numpy as jnp
from jax import lax
from jax.experimental import pallas as pl
from jax.experimental.pallas import tpu as pltpu

F32 = jnp.float32
MXU = jnp.bfloat16
ACT = jnp.bfloat16
HI = lax.Precision.HIGHEST
EPS = 1e-6

D = 1024
HID = 4096
XBC = 1536
Q = 128
SGG = 8
CK = 31
SK = 5
T = 256
LANE = 128
VMEM_LIMIT = 56 * 1024 * 1024

ADAM_LR, ADAM_B1, ADAM_B2, ADAM_EPS, ADAM_WD, ADAM_STEP = 0.001, 0.9, 0.999, 1e-08, 0.01, 10


def _call(body, name, grid, in_specs, out_specs, out_shape, scratch=()):
    return pl.pallas_call(
        body, name=name, grid=grid, in_specs=in_specs, out_specs=out_specs, out_shape=out_shape,
        scratch_shapes=list(scratch),
        compiler_params=pltpu.CompilerParams(dimension_semantics=("arbitrary",) * len(grid),
                                             vmem_limit_bytes=VMEM_LIMIT))


def _sds(shape, dt=F32):
    return jax.ShapeDtypeStruct(tuple(shape), dt)


def _rows(t, w, off=0, lane_blk=0):
    return pl.BlockSpec((t, w), lambda i: (i + off, lane_blk))


def _rows_lat(t, w, nct):
    return pl.BlockSpec((t, w), lambda i: (jnp.maximum(i - nct, 0), 0))


def _full(*shape):
    return pl.BlockSpec(shape, lambda *_: (0,) * len(shape))


def _wfull(*shape):
    return pl.BlockSpec(shape, lambda *_: (0,) * len(shape), pipeline_mode=pl.Buffered(1))


def _halo(w, hb, nrows):
    r, nb = T // hb, nrows // hb
    prev = pl.BlockSpec((hb, w), lambda i: (jnp.maximum(i * r - 1, 0), 0))
    nxt = pl.BlockSpec((hb, w), lambda i: (jnp.minimum((i + 1) * r, nb - 1), 0))
    return prev, nxt


def _nn(a, b):
    return jnp.dot(a.astype(MXU), b.astype(MXU), preferred_element_type=F32)


def _nt(a, b):
    return lax.dot_general(a.astype(MXU), b.astype(MXU), (((1,), (1,)), ((), ())), preferred_element_type=F32)


def _tn_dot(a, b):
    return lax.dot_general(a.astype(MXU), b.astype(MXU), (((0,), (0,)), ((), ())), preferred_element_type=F32)


def _sum0(x):
    return jnp.sum(x, axis=0, keepdims=True)


def _silu(x):
    return x * jax.nn.sigmoid(x)


def _gelu(x):
    return jax.nn.gelu(x, approximate=True)


def _mod(h, g, sh, sc):
    n = h * lax.rsqrt(jnp.mean(h * h, axis=-1, keepdims=True) + EPS)
    return n * g * (1.0 + sc) + sh


def _ln(x, g, b):
    xc = x - jnp.mean(x, axis=-1, keepdims=True)
    return xc * lax.rsqrt(jnp.mean(xc * xc, axis=-1, keepdims=True) + EPS) * g + b


def _gate_norm(ytot, z, ng):
    yg = ytot * _silu(z)
    halves = []
    for k in range(2):
        seg = yg[:, k * 512:(k + 1) * 512]
        halves.append(seg * lax.rsqrt(jnp.mean(seg * seg, axis=-1, keepdims=True) + EPS) * ng[:, k * 512:(k + 1) * 512])
    return jnp.concatenate(halves, axis=-1)


def _fill_ext(ext_ref, prev_ref, cur_ref, next_ref, hb, first, last):
    ext_ref[0:hb, :] = jnp.where(first, 0.0, prev_ref[...])
    ext_ref[hb:hb + T, :] = cur_ref[...]
    ext_ref[hb + T:hb + T + hb, :] = jnp.where(last, 0.0, next_ref[...])


def _conv(ext_ref, w_ref, k_taps, hb, lanes):
    off = hb - k_taps // 2
    acc = ext_ref[pl.ds(off, T), lanes] * w_ref[0:1, lanes]
    for k in range(1, k_taps):
        acc = acc + ext_ref[pl.ds(off + k, T), lanes] * w_ref[k:k + 1, lanes]
    return acc


def _conv_tr(ext_ref, w_ref, k_taps, hb, lanes):
    off = hb + k_taps // 2
    acc = ext_ref[pl.ds(off, T), lanes] * w_ref[0:1, lanes]
    for k in range(1, k_taps):
        acc = acc + ext_ref[pl.ds(off - k, T), lanes] * w_ref[k:k + 1, lanes]
    return acc


def _conv_wide(ext_ref, w_ref, k_taps, hb, lanes, flip=False):
    base = hb - k_taps // 2
    acc = None
    for b in range(8):
        taps = [k for k in range(k_taps) if (base + k) % 8 == b]
        if not taps:
            continue
        p = None
        for k in taps:
            wi = (k_taps - 1 - k) if flip else k
            term = ext_ref[pl.ds(base + k - b, T + 8), lanes] * w_ref[wi:wi + 1, lanes]
            p = term if p is None else p + term
        acc = p[b:b + T] if acc is None else acc + p[b:b + T]
    return acc


def _conv_dw_wide(dw_ref, d_ref, xext_ref, k_taps, hb, lanes):
    base = hb - k_taps // 2
    d = d_ref[:, lanes]
    for b in range(8):
        taps = [k for k in range(k_taps) if (base + k) % 8 == b]
        if not taps:
            continue
        lo_off = base + taps[0] - b
        span = base + taps[-1] - b - lo_off
        xs = xext_ref[pl.ds(lo_off + b, T + span), lanes]
        for k in taps:
            a = base + k - b - lo_off
            dw_ref[k:k + 1, lanes] += _sum0(d * xs[a:a + T])


def _conv_dw(dw_ref, d_ref, xext_ref, k_taps, hb, lanes):
    off = hb - k_taps // 2
    d = d_ref[:, lanes]
    for k in range(k_taps):
        dw_ref[k:k + 1, lanes] += _sum0(d * xext_ref[pl.ds(off + k, T), lanes])


def _tn(a, b, name, relu2=False, shard=None):
    m_rows, ka = a.shape
    n = b.shape[1]
    tm = next(t for t in (1024, 768, 512, 256) if m_rows % t == 0)
    tk = min(ka, 1024)
    tn = n if n <= 1024 else next(t for t in (1024, 768, 512, 384, 256, 128) if n % t == 0)
    n_m = m_rows // tm
    per = 1 if shard is None else (tn if shard[0] == "col" else tk) // shard[1]

    def body(a_ref, b_ref, o_ref, *acc):
        acc_ref = acc[0] if acc else o_ref

        @pl.when(pl.program_id(2) == 0)
        def _():
            acc_ref[...] = jnp.zeros_like(acc_ref)
        av = a_ref[...]
        if relu2:
            av = jnp.square(jnp.maximum(av.astype(F32), 0.0))
        acc_ref[...] += _tn_dot(av, b_ref[...])
        if acc:
            @pl.when(pl.program_id(2) == n_m - 1)
            def _():
                for s in range(per):
                    if shard[0] == "col":
                        o_ref[s] = acc_ref[:, s * shard[1]:(s + 1) * shard[1]].astype(o_ref.dtype)
                    else:
                        o_ref[s] = acc_ref[s * shard[1]:(s + 1) * shard[1], :].astype(o_ref.dtype)

    in_specs = [pl.BlockSpec((tm, tk), lambda k, j, m: (m, k)), pl.BlockSpec((tm, tn), lambda k, j, m: (m, j))]
    if shard is None:
        return _call(body, name, (ka // tk, n // tn, n_m), in_specs,
                     pl.BlockSpec((tk, tn), lambda k, j, m: (k, j)), _sds((ka, n)))(a, b)
    if shard[0] == "col":
        out_spec = pl.BlockSpec((per, tk, shard[1]), lambda k, j, m: (j, k, 0))
        out_shape = _sds((n // shard[1], ka, shard[1]), jnp.bfloat16)
    else:
        out_spec = pl.BlockSpec((per, shard[1], tn), lambda k, j, m: (k, 0, j))
        out_shape = _sds((ka // shard[1], shard[1], n), jnp.bfloat16)
    return _call(body, name, (ka // tk, n // tn, n_m), in_specs, out_spec, out_shape,
                 scratch=[pltpu.VMEM((tk, tn), F32)])(a, b)


def _mlp_fwd(h, mp, wpack, layer, name, tail=None):
    n_rows = h.shape[0]

    def body(h_ref, mp_ref, w1_ref, w2_ref, *rest):
        if tail is None:
            hn_ref, a_ref, y_ref = rest
        else:
            t_ref, g_ref, hn_ref, a_ref, y_ref, pg_ref, ls_ref = rest
        hv = h_ref[...]
        u = _mod(hv, mp_ref[0:1], mp_ref[1:2], mp_ref[2:3]).astype(MXU)
        acc = jnp.zeros((T, D), F32)
        for j in range(HID // 1024):
            cs = slice(j * 1024, (j + 1) * 1024)
            a = jnp.dot(u, w1_ref[j], preferred_element_type=F32)
            a_ref[:, cs] = a.astype(ACT)
            acc = acc + jnp.dot(jnp.square(jnp.maximum(a, 0.0)).astype(MXU), w2_ref[j], preferred_element_type=F32)
        y_ref[...] = acc
        hn = hv + mp_ref[3:4] * acc
        hn_ref[...] = hn if tail is None else _loss_tail(hn, t_ref[...], g_ref[...], pg_ref, ls_ref)

    ins = [_rows(T, D), _full(8, D), _mlp_wspec(layer), _mlp_wspec(2 + layer)]
    outs = [_rows(T, D), _rows(T, HID), _rows(T, D)]
    shapes = [_sds((n_rows, D)), _sds((n_rows, HID), ACT), _sds((n_rows, D))]
    if tail is None:
        return _call(body, name, (n_rows // T,), ins, outs, shapes)(h, mp, wpack, wpack)
    return _call(body, name, (n_rows // T,), ins + [_rows(T, D), _full(1, D)], outs + [_full(8, D), _full(8, LANE)],
                 shapes + [_sds((8, D)), _sds((8, LANE))])(h, mp, wpack, wpack, *tail)


def _mlp_wspec(row_block):
    return pl.BlockSpec((4, 1024, 1024), lambda i: (0, row_block, 0), pipeline_mode=pl.Buffered(1))


def _mlp_bwd(dh, h, a, y, mp, wpack, layer, name):
    n_rows = h.shape[0]

    def body(dh_ref, h_ref, a_ref, y_ref, mp_ref, w1_ref, w2_ref, dho_ref, da_ref, dyb_ref, ub_ref, pg_ref):
        dhp = dh_ref[...]
        u, vjp = jax.vjp(_mod, h_ref[...], mp_ref[0:1], mp_ref[1:2], mp_ref[2:3])
        ub_ref[...] = u.astype(ACT)
        dyb = (mp_ref[3:4] * dhp).astype(MXU)
        dyb_ref[...] = dyb.astype(ACT)
        du = jnp.zeros((T, D), F32)
        for j in range(HID // 1024):
            cs = slice(j * 1024, (j + 1) * 1024)
            dp = _nt(dyb, w2_ref[j])
            da = dp * 2.0 * jnp.maximum(a_ref[:, cs].astype(F32), 0.0)
            da_ref[:, cs] = da.astype(ACT)
            du = du + _nt(da, w1_ref[j])
        dhn, dg, dsh, dsc = vjp(du)
        dho_ref[...] = dhp + dhn

        @pl.when(pl.program_id(0) == 0)
        def _():
            pg_ref[...] = jnp.zeros_like(pg_ref)
        pg_ref[0:1] += dg
        pg_ref[1:2] += dsh
        pg_ref[2:3] += dsc
        pg_ref[3:4] += _sum0(dhp * y_ref[...])

    return _call(body, name, (n_rows // T,),
                 [_rows(T, D), _rows(T, D), _rows(T, HID), _rows(T, D), _full(8, D), _mlp_wspec(layer), _mlp_wspec(2 + layer)],
                 [_rows(T, D), _rows(T, HID), _rows(T, D), _rows(T, D), _full(8, D)],
                 [_sds((n_rows, D)), _sds((n_rows, HID), ACT), _sds((n_rows, D), ACT), _sds((n_rows, D), ACT),
                  _sds((8, D))])(dh, h, a, y, mp, wpack, wpack)


def _cf1_fwd(h, mp, w1, b1):
    n_rows = h.shape[0]

    def body(h_ref, mp_ref, w1_ref, b1_ref, glu_ref, a_ref):
        u = _mod(h_ref[...], mp_ref[0:1], mp_ref[1:2], mp_ref[2:3]).astype(MXU)
        a = jnp.dot(u, w1_ref[...], preferred_element_type=F32) + b1_ref[...]
        a_ref[...] = a.astype(ACT)
        glu_ref[...] = a[:, :D] * jax.nn.sigmoid(a[:, D:])

    return _call(body, "cf1_fwd", (n_rows // T,),
                 [_rows(T, D), _full(8, D), _wfull(D, 2 * D), _full(1, 2 * D)],
                 [_rows(T, D), _rows(T, 2 * D)],
                 [_sds((n_rows, D)), _sds((n_rows, 2 * D), ACT)])(h, mp, w1, b1)


def _cf2_fwd(h, glu, mp, wdw, bdw, lng, lnb, w2, b2):
    n_rows = h.shape[0]
    nt = n_rows // T
    hb = 16

    def body(h_ref, gp_ref, gc_ref, gn_ref, mp_ref, wdw_ref, bdw_ref, lng_ref, lnb_ref, w2_ref, b2_ref,
             hn_ref, cv_ref, sb_ref, y_ref, ext):
        i = pl.program_id(0)
        _fill_ext(ext, gp_ref, gc_ref, gn_ref, hb, i == 0, i == nt - 1)
        for c in range(D // LANE):
            lanes = slice(c * LANE, (c + 1) * LANE)
            cv_ref[:, lanes] = _conv_wide(ext, wdw_ref, CK, hb, lanes) + bdw_ref[:, lanes]
        s = _silu(_ln(cv_ref[...], lng_ref[...], lnb_ref[...])).astype(MXU)
        sb_ref[...] = s.astype(ACT)
        y = jnp.dot(s, w2_ref[...], preferred_element_type=F32) + b2_ref[...]
        y_ref[...] = y
        hn_ref[...] = h_ref[...] + mp_ref[3:4] * y

    gp, gn = _halo(D, hb, n_rows)
    return _call(body, "cf2_fwd", (nt,),
                 [_rows(T, D), gp, _rows(T, D), gn, _full(8, D), _full(32, D), _full(1, D), _full(1, D), _full(1, D),
                  _wfull(D, D), _full(1, D)],
                 [_rows(T, D), _rows(T, D), _rows(T, D), _rows(T, D)],
                 [_sds((n_rows, D)), _sds((n_rows, D)), _sds((n_rows, D), ACT), _sds((n_rows, D))],
                 scratch=[pltpu.VMEM((T + 2 * hb, D), F32)])(h, glu, glu, glu, mp, wdw, bdw, lng, lnb, w2, b2)


def _cf2_bwd(dh, y, cv, mp, lng, lnb, w2):
    n_rows = dh.shape[0]

    def body(dh_ref, y_ref, cv_ref, mp_ref, lng_ref, lnb_ref, w2_ref, dcv_ref, dyb_ref, pg_ref):
        dhp = dh_ref[...]
        dy = mp_ref[3:4] * dhp
        dyb = dy.astype(MXU)
        dyb_ref[...] = dyb.astype(ACT)
        ds = _nt(dyb, w2_ref[...])
        _, vjp = jax.vjp(lambda cv_, g_, b_: _silu(_ln(cv_, g_, b_)), cv_ref[...], lng_ref[...], lnb_ref[...])
        dcv, dlng, dlnb = vjp(ds)
        dcv_ref[...] = dcv

        @pl.when(pl.program_id(0) == 0)
        def _():
            pg_ref[...] = jnp.zeros_like(pg_ref)
        pg_ref[0:1] += _sum0(dhp * y_ref[...])
        pg_ref[1:2] += _sum0(dy)
        pg_ref[2:3] += dlng
        pg_ref[3:4] += dlnb
        pg_ref[4:5] += _sum0(dcv)

    return _call(body, "cf2_bwd", (n_rows // T,),
                 [_rows(T, D), _rows(T, D), _rows(T, D), _full(8, D), _full(1, D), _full(1, D), _wfull(D, D)],
                 [_rows(T, D), _rows(T, D), _full(8, D)],
                 [_sds((n_rows, D)), _sds((n_rows, D), ACT), _sds((8, D))])(dh, y, cv, mp, lng, lnb, w2)


def _cf1_bwd(dh, h, a, dcv, glu, mp, wdw, w1):
    n_rows = h.shape[0]
    nt = n_rows // T
    hb = 16

    def body(dh_ref, h_ref, a_ref, dp_ref, dc_ref, dn_ref, gp_ref, gc_ref, gn_ref, mp_ref, wdw_ref, w1_ref,
             dho_ref, da_ref, ub_ref, pg_ref, pb_ref, dw_ref, dext, gext, dglu):
        i = pl.program_id(0)

        @pl.when(i == 0)
        def _():
            pg_ref[...] = jnp.zeros_like(pg_ref)
            pb_ref[...] = jnp.zeros_like(pb_ref)
            dw_ref[...] = jnp.zeros_like(dw_ref)
        _fill_ext(dext, dp_ref, dc_ref, dn_ref, hb, i == 0, i == nt - 1)
        _fill_ext(gext, gp_ref, gc_ref, gn_ref, hb, i == 0, i == nt - 1)
        for c in range(D // LANE):
            lanes = slice(c * LANE, (c + 1) * LANE)
            dglu[:, lanes] = _conv_wide(dext, wdw_ref, CK, hb, lanes, flip=True)
            _conv_dw_wide(dw_ref, dc_ref, gext, CK, hb, lanes)
        av = a_ref[...].astype(F32)
        _, vjp_glu = jax.vjp(lambda a1, a2: a1 * jax.nn.sigmoid(a2), av[:, :D], av[:, D:])
        da1, da2 = vjp_glu(dglu[...])
        da_ref[:, :D] = da1.astype(ACT)
        da_ref[:, D:] = da2.astype(ACT)
        pb_ref[0:1, :D] += _sum0(da1)
        pb_ref[0:1, D:] += _sum0(da2)
        du = _nt(da1, w1_ref[:, :D]) + _nt(da2, w1_ref[:, D:])
        u, vjp = jax.vjp(_mod, h_ref[...], mp_ref[0:1], mp_ref[1:2], mp_ref[2:3])
        ub_ref[...] = u.astype(ACT)
        dhn, dg, dsh, dsc = vjp(du)
        dho_ref[...] = dh_ref[...] + dhn
        pg_ref[0:1] += dg
        pg_ref[1:2] += dsh
        pg_ref[2:3] += dsc

    hp, hn = _halo(D, hb, n_rows)
    return _call(body, "cf1_bwd", (nt,),
                 [_rows(T, D), _rows(T, D), _rows(T, 2 * D), hp, _rows(T, D), hn, hp, _rows(T, D), hn,
                  _full(8, D), _full(32, D), _wfull(D, 2 * D)],
                 [_rows(T, D), _rows(T, 2 * D), _rows(T, D), _full(8, D), _full(8, 2 * D), _full(32, D)],
                 [_sds((n_rows, D)), _sds((n_rows, 2 * D), ACT), _sds((n_rows, D), ACT), _sds((8, D)),
                  _sds((8, 2 * D)), _sds((32, D))],
                 scratch=[pltpu.VMEM((T + 2 * hb, D), F32), pltpu.VMEM((T + 2 * hb, D), F32), pltpu.VMEM((T, D), F32)],
                 )(dh, h, a, dcv, dcv, dcv, glu, glu, glu, mp, wdw, w1)


def _sg_blocks():
    return [(c, g, slice(c * Q, (c + 1) * Q), slice(g * LANE, (g + 1) * LANE)) for c in range(T // Q) for g in range(SGG)]


IN_W = D + XBC + 32 + 2 * D
IN_LOC = IN_W // 4


def _win_split(shards):
    o1, o2, o3 = D, D + XBC, D + XBC + 32
    tr = 256

    def cols(s_ref, lo, hi):
        parts = []
        for j in range(4):
            a, b = max(lo, j * IN_LOC), min(hi, (j + 1) * IN_LOC)
            if a < b:
                parts.append(s_ref[j][:, a - j * IN_LOC:b - j * IN_LOC])
        return parts[0] if len(parts) == 1 else jnp.concatenate(parts, axis=1)

    def body(s_ref, wz_ref, wxbc_ref, wdt_ref, wuv_ref):
        wz_ref[...] = cols(s_ref, 0, o1)
        wxbc_ref[...] = cols(s_ref, o1, o2)
        dt = cols(s_ref, o2, o3)
        wdt_ref[...] = jnp.concatenate([dt, jnp.zeros((tr, LANE - 32), dt.dtype)], axis=1)
        wuv_ref[...] = cols(s_ref, o3, IN_W)

    dt_ = shards.dtype
    return _call(body, "win_split", (D // tr,), [pl.BlockSpec((4, tr, IN_LOC), lambda i: (0, i, 0))],
                 [_rows(tr, D), _rows(tr, XBC), _rows(tr, LANE), _rows(tr, 2 * D)],
                 [_sds((D, D), dt_), _sds((D, XBC), dt_), _sds((D, LANE), dt_), _sds((D, 2 * D), dt_)])(shards)


def _win_join(gz, gxbc, gdt, guv):
    tr = 256
    bounds = (0, D, D + XBC, D + XBC + 32, IN_W)

    def body(gz_ref, gx_ref, gd_ref, gu_ref, o_ref):
        segs = (gz_ref, gx_ref, gd_ref, gu_ref)
        for j in range(4):
            parts = []
            for k in range(4):
                a, b = max(bounds[k], j * IN_LOC), min(bounds[k + 1], (j + 1) * IN_LOC)
                if a < b:
                    parts.append(segs[k][:, a - bounds[k]:b - bounds[k]])
            full = parts[0] if len(parts) == 1 else jnp.concatenate(parts, axis=1)
            o_ref[j] = full.astype(jnp.bfloat16)

    return _call(body, "win_join", (D // tr,), [_rows(tr, D), _rows(tr, XBC), _rows(tr, LANE), _rows(tr, 2 * D)],
                 pl.BlockSpec((4, tr, IN_LOC), lambda i: (0, i, 0)), _sds((4, D, IN_LOC), jnp.bfloat16))(gz, gxbc, gdt, guv)


def _ctx_spec(nct):
    return pl.BlockSpec((T, D), lambda i: (jnp.minimum(i, nct - 1), 0))


def _hy1_fwd(ctx, x, mp2, wz, wuv, wxbc, wdt, lng, lnb, sgw, sgbt, nct):
    n_lat = x.shape[0]
    n_rows = ctx.shape[0] + n_lat

    def body(c_ref, x_ref, mp_ref, wz_ref, wuv_ref, wxbc_ref, wdt_ref, lng_ref, lnb_ref, sgw_ref, sgbt_ref,
             z_ref, uv_ref, xbcp_ref, dtr_ref, ysg_ref):
        hv = jnp.where(pl.program_id(0) < nct, c_ref[...], x_ref[...])
        u = _mod(hv, mp_ref[0:1], mp_ref[1:2], mp_ref[2:3]).astype(MXU)
        z_ref[...] = jnp.dot(u, wz_ref[...], preferred_element_type=F32)
        xbcp_ref[...] = jnp.dot(u, wxbc_ref[...], preferred_element_type=F32)
        dtr_ref[...] = jnp.dot(u, wdt_ref[...], preferred_element_type=F32)
        uv = jnp.dot(u, wuv_ref[...], preferred_element_type=F32)
        uv_ref[...] = uv
        gate = _gelu(uv[:, :D])
        vln = _ln(_gelu(uv[:, D:]), lng_ref[...], lnb_ref[...]).astype(MXU)
        for _, g, rs, ls in _sg_blocks():
            s = jnp.dot(sgw_ref[g], vln[rs, ls], preferred_element_type=F32) + sgbt_ref[:, g:g + 1]
            ysg_ref[rs, ls] = (gate[rs, ls] * s).astype(ACT)

    mspec = pl.BlockSpec((None, 8, D), lambda i: (jnp.where(i < nct, 0, 1), 0, 0))
    return _call(body, "hy1_fwd", (n_rows // T,),
                 [_ctx_spec(nct), _rows_lat(T, D, nct), mspec, _wfull(D, D), _wfull(D, 2 * D), _wfull(D, XBC), _wfull(D, LANE),
                  _full(1, D), _full(1, D), _full(SGG, Q, Q), _full(Q, LANE)],
                 [_rows(T, D), _rows(T, 2 * D), _rows(T, XBC), _rows(T, LANE), _rows_lat(T, D, nct)],
                 [_sds((n_rows, D)), _sds((n_rows, 2 * D)), _sds((n_rows, XBC)), _sds((n_rows, LANE)),
                  _sds((n_lat, D), ACT)])(ctx, x, mp2, wz, wuv, wxbc, wdt, lng, lnb, sgw, sgbt)


def _hy1_bwd(ctx, x, uv, dz, dxbcp, ddf, ddb, dysg, dres, mp2, wz, wuv, wxbc, wdt, lng, lnb, sgw, sgbt, nct):
    n_lat = dres.shape[0]
    n_rows = ctx.shape[0] + n_lat

    def body(c_ref, x_ref, uv_ref, dz_ref, dxbcp_ref, ddf_ref, ddb_ref, dysg_ref, dres_ref, mp_ref, wz_ref, wuv_ref,
             wxbc_ref, wdt_ref, lng_ref, lnb_ref, sgw_ref, sgbt_ref,
             dho_ref, ub_ref, duv_ref, ddt_ref, pg2_ref, pl_ref, dsgw_ref, dsgb_ref, dgate_s, dvln_s):
        i = pl.program_id(0)

        @pl.when(i == 0)
        def _():
            pg2_ref[...] = jnp.zeros_like(pg2_ref)
            pl_ref[...] = jnp.zeros_like(pl_ref)
            dsgw_ref[...] = jnp.zeros_like(dsgw_ref)
            dsgb_ref[...] = jnp.zeros_like(dsgb_ref)
        uv = uv_ref[...]

        def f_sg(ug, uvv, g_, b_):
            return _gelu(ug), _ln(_gelu(uvv), g_, b_)
        (gate, vln), vjp_sg = jax.vjp(f_sg, uv[:, :D], uv[:, D:], lng_ref[...], lnb_ref[...])
        vlnb = vln.astype(MXU)
        lane = lax.broadcasted_iota(jnp.int32, (Q, LANE), 1)
        dsgb = jnp.zeros((Q, LANE), F32)
        for _, g, rs, ls in _sg_blocks():
            s = jnp.dot(sgw_ref[g], vlnb[rs, ls], preferred_element_type=F32) + sgbt_ref[:, g:g + 1]
            dyb = dysg_ref[rs, ls]
            dgate_s[rs, ls] = dyb * s
            ds = dyb * gate[rs, ls]
            dvln_s[rs, ls] = _tn_dot(sgw_ref[g], ds)
            dsgw_ref[g] += _nt(ds, vlnb[rs, ls])
            dsgb = dsgb + jnp.where(lane == g, jnp.sum(ds, axis=1, keepdims=True), 0.0)
        dsgb_ref[...] += dsgb
        dug, duvv, dlng, dlnb = vjp_sg((dgate_s[...], dvln_s[...]))
        pl_ref[0:1] += dlng
        pl_ref[1:2] += dlnb
        duv_ref[:, :D] = dug.astype(ACT)
        duv_ref[:, D:] = duvv.astype(ACT)
        ddt = (ddf_ref[...] + ddb_ref[...]).astype(MXU)
        ddt_ref[...] = ddt.astype(ACT)
        du = (_nt(dz_ref[...], wz_ref[...]) + _nt(dug, wuv_ref[:, :D]) + _nt(duvv, wuv_ref[:, D:])
              + _nt(dxbcp_ref[...], wxbc_ref[...]) + _nt(ddt, wdt_ref[...]))
        hv = jnp.where(i < nct, c_ref[...], x_ref[...])
        u, vjp = jax.vjp(_mod, hv, mp_ref[0:1], mp_ref[1:2], mp_ref[2:3])
        ub_ref[...] = u.astype(ACT)
        dhn, dg, dsh, dsc = vjp(du)
        dho_ref[...] = dres_ref[...] + dhn
        is_ctx = i < nct
        for k, val in enumerate((dg, dsh, dsc)):
            pg2_ref[0, k:k + 1] += jnp.where(is_ctx, val, 0.0)
            pg2_ref[1, k:k + 1] += jnp.where(is_ctx, 0.0, val)

    mspec = pl.BlockSpec((None, 8, D), lambda i: (jnp.where(i < nct, 0, 1), 0, 0))
    return _call(body, "hy1_bwd", (n_rows // T,),
                 [_ctx_spec(nct), _rows_lat(T, D, nct), _rows(T, 2 * D), _rows(T, D), _rows(T, XBC), _rows(T, LANE),
                  _rows(T, LANE), _rows(T, D),
                  _rows_lat(T, D, nct), mspec, _wfull(D, D), _wfull(D, 2 * D), _wfull(D, XBC), _wfull(D, LANE),
                  _full(1, D), _full(1, D), _full(SGG, Q, Q), _full(Q, LANE)],
                 [_rows_lat(T, D, nct), _rows(T, D), _rows(T, 2 * D), _rows(T, LANE), _full(2, 8, D), _full(8, D),
                  _full(SGG, Q, Q), _full(Q, LANE)],
                 [_sds((n_lat, D)), _sds((n_rows, D), ACT), _sds((n_rows, 2 * D), ACT), _sds((n_rows, LANE), ACT),
                  _sds((2, 8, D)), _sds((8, D)), _sds((SGG, Q, Q)), _sds((Q, LANE))],
                 scratch=[pltpu.VMEM((T, D), F32), pltpu.VMEM((T, D), F32)],
                 )(ctx, x, uv, dz, dxbcp, ddf, ddb, dysg, dres, mp2, wz, wuv, wxbc, wdt, lng, lnb, sgw, sgbt)


def _seq_edges(i, nct, nt):
    return (i == 0) | (i == nct), (i == nct - 1) | (i == nt - 1)


def _cv5_fwd(xbcp, w, b, nct):
    n_rows = xbcp.shape[0]
    nt = n_rows // T
    hb = 8

    def body(p_ref, c_ref, n_ref, w_ref, b_ref, o_ref, ext):
        first, last = _seq_edges(pl.program_id(0), nct, nt)
        _fill_ext(ext, p_ref, c_ref, n_ref, hb, first, last)
        for c in range(XBC // LANE):
            lanes = slice(c * LANE, (c + 1) * LANE)
            o_ref[:, lanes] = _silu(_conv(ext, w_ref, SK, hb, lanes) + b_ref[:, lanes])

    hp, hn = _halo(XBC, hb, n_rows)
    return _call(body, "cv5_fwd", (nt,), [hp, _rows(T, XBC), hn, _full(8, XBC), _full(1, XBC)],
                 _rows(T, XBC), _sds((n_rows, XBC)), scratch=[pltpu.VMEM((T + 2 * hb, XBC), F32)])(xbcp, xbcp, xbcp, w, b)


def _cv5_bwd1(xbcp, dxf, dxb, w, b, nct):
    n_rows = xbcp.shape[0]
    nt = n_rows // T
    hb = 8

    def body(p_ref, c_ref, n_ref, dxf_ref, dxb_ref, w_ref, b_ref, o_ref, pg_ref, ext):
        i = pl.program_id(0)
        first, last = _seq_edges(i, nct, nt)
        _fill_ext(ext, p_ref, c_ref, n_ref, hb, first, last)

        @pl.when(i == 0)
        def _():
            pg_ref[...] = jnp.zeros_like(pg_ref)
        for c in range(XBC // LANE):
            lanes = slice(c * LANE, (c + 1) * LANE)
            cv = _conv(ext, w_ref, SK, hb, lanes) + b_ref[:, lanes]
            sg = jax.nn.sigmoid(cv)
            dcv = (dxf_ref[:, lanes] + dxb_ref[:, lanes]) * (sg * (1.0 + cv * (1.0 - sg)))
            o_ref[:, lanes] = dcv
            pg_ref[0:1, lanes] += _sum0(dcv)

    hp, hn = _halo(XBC, hb, n_rows)
    return _call(body, "cv5_bwd1", (nt,),
                 [hp, _rows(T, XBC), hn, _rows(T, XBC), _rows(T, XBC), _full(8, XBC), _full(1, XBC)],
                 [_rows(T, XBC), _full(8, XBC)], [_sds((n_rows, XBC)), _sds((8, XBC))],
                 scratch=[pltpu.VMEM((T + 2 * hb, XBC), F32)])(xbcp, xbcp, xbcp, dxf, dxb, w, b)


def _cv5_bwd2(dcv, xbcp, w, nct):
    n_rows = xbcp.shape[0]
    nt = n_rows // T
    hb = 8

    def body(dp_ref, dc_ref, dn_ref, xp_ref, xc_ref, xn_ref, w_ref, o_ref, dw_ref, dext, xext):
        i = pl.program_id(0)
        first, last = _seq_edges(i, nct, nt)
        _fill_ext(dext, dp_ref, dc_ref, dn_ref, hb, first, last)
        _fill_ext(xext, xp_ref, xc_ref, xn_ref, hb, first, last)

        @pl.when(i == 0)
        def _():
            dw_ref[...] = jnp.zeros_like(dw_ref)
        for c in range(XBC // LANE):
            lanes = slice(c * LANE, (c + 1) * LANE)
            o_ref[:, lanes] = _conv_tr(dext, w_ref, SK, hb, lanes).astype(ACT)
            _conv_dw(dw_ref, dc_ref, xext, SK, hb, lanes)

    hp, hn = _halo(XBC, hb, n_rows)
    return _call(body, "cv5_bwd2", (nt,),
                 [hp, _rows(T, XBC), hn, hp, _rows(T, XBC), hn, _full(8, XBC)],
                 [_rows(T, XBC), _full(8, XBC)], [_sds((n_rows, XBC), ACT), _sds((8, XBC))],
                 scratch=[pltpu.VMEM((T + 2 * hb, XBC), F32), pltpu.VMEM((T + 2 * hb, XBC), F32)],
                 )(dcv, dcv, dcv, xbcp, xbcp, xbcp, w)


def _scan_order(nc, ncc, rev):
    if not rev:
        return lambda s: s
    return lambda s: jnp.where(s < ncc, ncc - 1 - s, nc - 1 - (s - ncc))


def _ssd_prep(dtr, sp, rev):
    dt = jax.nn.softplus(dtr + sp[0:1])
    a_neg = -jnp.exp(sp[1:2])
    r = lax.broadcasted_iota(jnp.int32, (Q, Q), 0)
    c = lax.broadcasted_iota(jnp.int32, (Q, Q), 1)
    msk = (c >= r) if rev else (c <= r)
    tri = msk.astype(F32)
    acs = jnp.dot(tri, dt * a_neg, precision=HI, preferred_element_type=F32)
    last = 0 if rev else Q - 1
    return dt, a_neg, acs, msk, tri, last


def _pair_sel(arr, lo, m, lane_lt):
    h0 = lo + 2 * m
    return jnp.where(lane_lt, arr[:, h0:h0 + 1], arr[:, h0 + 1:h0 + 2])


def _head_lanes(row, lo, g):
    lane = lax.broadcasted_iota(jnp.int32, (1, 512), 1)
    out = jnp.zeros((1, 512), F32)
    for k in range(8):
        h = lo + 8 * g + k
        out = jnp.where((lane >= 64 * k) & (lane < 64 * (k + 1)), row[:, h:h + 1], out)
    return out


def _halves(v, lane_lt):
    return jnp.concatenate([jnp.where(lane_lt, v, 0.0), jnp.where(lane_lt, 0.0, v)], axis=0)


def _ssd_fwd(xbc, dtr, sp, ncc, rev):
    n_rows = xbc.shape[0]
    nc = n_rows // Q
    lo = 16 if rev else 0
    order = _scan_order(nc, ncc, rev)

    def body(x_ref, dtr_ref, sp_ref, y_ref, hin_ref, st):
        @pl.when(pl.program_id(0) == 0)
        def _():
            st[...] = jnp.zeros_like(st)
        dt, _, acs, msk, _, last = _ssd_prep(dtr_ref[...], sp_ref[...], rev)
        acs_t, dt_t = acs.T, dt.T
        eacs = jnp.exp(acs)
        eal = jnp.exp(acs[last:last + 1, :])
        tew = jnp.exp(acs[last:last + 1, :] - acs) * dt
        lane_lt = lax.broadcasted_iota(jnp.int32, (Q, LANE), 1) < 64
        for g in range(2):
            gl = slice(g * 512, (g + 1) * 512)
            bg = x_ref[:, 1024 + g * 128:1152 + g * 128]
            cg = x_ref[:, 1280 + g * 128:1408 + g * 128]
            s_g = _nt(cg, bg)
            h_t = st[:, gl]
            hin_ref[:, gl] = h_t
            yoff = _nn(cg, h_t)
            xw = []
            for mm in range(4):
                m = 4 * g + mm
                ls = slice(m * LANE, (m + 1) * LANE)
                x2 = x_ref[:, ls]
                ws = []
                for hh in range(2):
                    h = lo + 2 * m + hh
                    lm = jnp.exp(jnp.where(msk, acs[:, h:h + 1] - acs_t[h:h + 1, :], -jnp.inf))
                    ws.append(s_g * lm * dt_t[h:h + 1, :])
                y2 = _nn(jnp.concatenate(ws, axis=1), _halves(x2, lane_lt))
                y_ref[:, ls] = y2 + yoff[:, mm * LANE:(mm + 1) * LANE] * _pair_sel(eacs, lo, m, lane_lt)
                xw.append(x2 * _pair_sel(tew, lo, m, lane_lt))
            st[:, gl] = _head_lanes(eal, lo, g) * h_t + _tn_dot(bg, jnp.concatenate(xw, axis=1))

    return _call(body, "ssd_fwd_r" if rev else "ssd_fwd_f", (nc,),
                 [pl.BlockSpec((Q, XBC), lambda s: (order(s), 0)), pl.BlockSpec((Q, LANE), lambda s: (order(s), 0)),
                  _full(8, LANE)],
                 [pl.BlockSpec((Q, D), lambda s: (order(s), 0)), pl.BlockSpec((None, LANE, D), lambda s: (order(s), 0, 0))],
                 [_sds((n_rows, D)), _sds((nc, LANE, D))], scratch=[pltpu.VMEM((LANE, D), F32)])(xbc, dtr, sp)


def _ssd_bwd(xbc, dtr, dy, hin, sp, dl, eh, ncc, rev):
    n_rows = xbc.shape[0]
    nc = n_rows // Q
    lo = 16 if rev else 0
    fwd_order = _scan_order(nc, ncc, rev)
    order = lambda s: fwd_order(nc - 1 - s)
    with_skip = not rev

    def body(x_ref, dtr_ref, dy_ref, hin_ref, sp_ref, dl_ref, eh_ref, dx_ref, ddtr_ref, pg_ref, dst):
        @pl.when(pl.program_id(0) == 0)
        def _():
            dst[...] = jnp.zeros_like(dst)
            pg_ref[...] = jnp.zeros_like(pg_ref)
        dtr_v = dtr_ref[...]
        dt, a_neg, acs, msk, tri, last = _ssd_prep(dtr_v, sp_ref[...], rev)
        acs_t = acs.T
        r = lax.broadcasted_iota(jnp.int32, (Q, Q), 0)
        c = lax.broadcasted_iota(jnp.int32, (Q, Q), 1)
        msk_t = (c <= r) if rev else (c >= r)
        eacs = jnp.exp(acs)
        eal = jnp.exp(acs[last:last + 1, :])
        te = jnp.exp(acs[last:last + 1, :] - acs)
        lane = lax.broadcasted_iota(jnp.int32, (Q, LANE), 1)
        lane1 = lax.broadcasted_iota(jnp.int32, (1, LANE), 1)
        lane_lt = lane < 64
        dacs = jnp.zeros((Q, LANE), F32)
        ddt_x = jnp.zeros((Q, LANE), F32)
        dlast = jnp.zeros((1, LANE), F32)
        hs_rows = []
        sub16 = lax.broadcasted_iota(jnp.int32, (16, Q), 0)
        dacs_t = jnp.zeros((16, Q), F32)
        for g in range(2):
            gl = slice(g * 512, (g + 1) * 512)
            bg = x_ref[:, 1024 + g * 128:1152 + g * 128]
            cg = x_ref[:, 1280 + g * 128:1408 + g * 128]
            s_g = _nt(cg, bg)
            s_gt = _nt(bg, cg)
            h_t, dh_t = hin_ref[:, gl], dst[:, gl]
            bh = _nn(bg, dh_t)
            yoff = _nn(cg, h_t)
            d_s = jnp.zeros((Q, Q), F32)
            edy, exd = [], []
            for mm in range(4):
                m = 4 * g + mm
                ls = slice(m * LANE, (m + 1) * LANE)
                x2, dy2 = x_ref[:, ls], dy_ref[:, ls]
                bh2 = bh[:, mm * LANE:(mm + 1) * LANE]
                dtm, em, eam = (_pair_sel(v, lo, m, lane_lt) for v in (dt, te, eacs))
                xd2 = x2 * dtm
                lms, mts = [], []
                for hh in range(2):
                    h = lo + 2 * m + hh
                    col, row = acs[:, h:h + 1], acs_t[h:h + 1, :]
                    lms.append(jnp.exp(jnp.where(msk, col - row, -jnp.inf)))
                    mts.append(s_gt * jnp.exp(jnp.where(msk_t, row - col, -jnp.inf)))
                dy_st = _halves(dy2, lane_lt)
                dxd2 = em * bh2 + _nn(jnp.concatenate(mts, axis=1), dy_st)
                dm_st = _nt(dy_st, xd2)
                dmt_st = _nt(_halves(xd2, lane_lt), dy2)
                d_s = d_s + dm_st[:Q] * lms[0] + dm_st[Q:] * lms[1]
                v1, v2, v3 = dy2 * yoff[:, mm * LANE:(mm + 1) * LANE] * eam, dxd2 * x2, xd2 * bh2 * em
                for hh in range(2):
                    h = lo + 2 * m + hh
                    half = lane_lt == (hh == 0)
                    g_rows = _sum0(dmt_st[hh * Q:(hh + 1) * Q] * mts[hh]) - _sum0(dm_st[hh * Q:(hh + 1) * Q] * s_g * lms[hh])
                    dacs_t = jnp.where(sub16 == 2 * m + hh, g_rows, dacs_t)
                    r1 = jnp.sum(jnp.where(half, v1, 0.0), axis=1, keepdims=True)
                    r2 = jnp.sum(jnp.where(half, v2, 0.0), axis=1, keepdims=True)
                    r3 = jnp.sum(jnp.where(half, v3, 0.0), axis=1, keepdims=True)
                    dacs = dacs + jnp.where(lane == h, r1 - r3, 0.0)
                    ddt_x = ddt_x + jnp.where(lane == h, r2, 0.0)
                    dlast = dlast + jnp.where(lane1 == h, _sum0(r3), 0.0)
                dx2 = dxd2 * dtm
                if with_skip:
                    dx2 = dx2 + dl_ref[:, ls] * dy2
                dx_ref[:, ls] = dx2
                edy.append(eam * dy2)
                exd.append(em * xd2)
            edy, exd = jnp.concatenate(edy, axis=1), jnp.concatenate(exd, axis=1)
            hs_rows.append(_sum0(h_t * dh_t))
            dst[:, gl] = _head_lanes(eal, lo, g) * dh_t + _tn_dot(cg, edy)
            dx_ref[:, 1024 + g * 128:1152 + g * 128] = _tn_dot(d_s, cg) + _nt(exd, dh_t)
            dx_ref[:, 1280 + g * 128:1408 + g * 128] = _nn(d_s, bg) + _nt(edy, h_t)
        hs = jnp.broadcast_to(jnp.concatenate(hs_rows, axis=1), (8, D))
        hsum = jnp.dot(hs, eh_ref[...], precision=HI, preferred_element_type=F32)[0:1]
        dlast = dlast + eal * hsum
        dacs = dacs + jnp.concatenate([jnp.zeros((lo, Q), F32)] * (lo > 0) + [dacs_t, jnp.zeros((LANE - 16 - lo, Q), F32)],
                                      axis=0).T
        rowi = lax.broadcasted_iota(jnp.int32, (Q, LANE), 0)
        dacs = dacs + jnp.where(rowi == last, dlast, 0.0)
        da = lax.dot_general(tri, dacs, (((0,), (0,)), ((), ())), precision=HI, preferred_element_type=F32)
        ddt = ddt_x + da * a_neg
        mine = (lane >= lo) & (lane < lo + 16)
        ddtr = jnp.where(mine, ddt * jax.nn.sigmoid(dtr_v + sp_ref[0:1]), 0.0)
        ddtr_ref[...] = ddtr
        pg_ref[0:1] += _sum0(ddtr)
        pg_ref[1:2] += jnp.where(mine[0:1], _sum0(da * dt) * a_neg, 0.0)

    blk = lambda w_: pl.BlockSpec((Q, w_), lambda s: (order(s), 0))
    return _call(body, "ssd_bwd_r" if rev else "ssd_bwd_f", (nc,),
                 [blk(XBC), blk(LANE), blk(D), pl.BlockSpec((None, LANE, D), lambda s: (order(s), 0, 0)),
                  _full(8, LANE), _full(1, D), _full(D, LANE)],
                 [blk(XBC), blk(LANE), _full(8, LANE)],
                 [_sds((n_rows, XBC)), _sds((n_rows, LANE)), _sds((8, LANE))],
                 scratch=[pltpu.VMEM((LANE, D), F32)])(xbc, dtr, dy, hin, sp, dl, eh)


def _hy4_fwd(h, yf, yb, xbc, z, ysg, mp, dl, ng, wout, nct):
    n_rows = h.shape[0]

    def body(h_ref, yf_ref, yb_ref, xs_ref, z_ref, ysg_ref, mp_ref, dl_ref, ng_ref, wout_ref, hn_ref, yssd_ref, out_ref):
        ytot = yf_ref[...] + yb_ref[...] + dl_ref[...] * xs_ref[...]
        yssd = _gate_norm(ytot, z_ref[...], ng_ref[...]).astype(MXU)
        yssd_ref[...] = yssd.astype(ACT)
        out = (jnp.dot(yssd, wout_ref[0:D, :], preferred_element_type=F32)
               + jnp.dot(ysg_ref[...].astype(MXU), wout_ref[D:2 * D, :], preferred_element_type=F32))
        out_ref[...] = out
        hn_ref[...] = h_ref[...] + mp_ref[3:4] * out

    return _call(body, "hy4_fwd", (n_rows // T,),
                 [_rows(T, D), _rows(T, D, nct), _rows(T, D, nct), _rows(T, D, nct), _rows(T, D, nct), _rows(T, D),
                  _full(8, D), _full(1, D), _full(1, D), _wfull(2 * D, D)],
                 [_rows(T, D), _rows(T, D), _rows(T, D)],
                 [_sds((n_rows, D)), _sds((n_rows, D), ACT), _sds((n_rows, D))])(h, yf, yb, xbc, z, ysg, mp, dl, ng, wout)


def _hy4_bwd(dh, out, yf, yb, xbc, z, mp, dl, ng, wout, nct):
    n_lat = dh.shape[0]
    n_rows = yf.shape[0]

    def body(dh_ref, out_ref, yf_ref, yb_ref, xs_ref, z_ref, mp_ref, dl_ref, ng_ref, wout_ref,
             dy_ref, dz_ref, dysg_ref, doutb_ref, pg_ref):
        i = pl.program_id(0)

        @pl.when(i == 0)
        def _():
            pg_ref[...] = jnp.zeros_like(pg_ref)

        @pl.when(i < nct)
        def _():
            dy_ref[...] = jnp.zeros_like(dy_ref)
            dz_ref[...] = jnp.zeros_like(dz_ref)
            dysg_ref[...] = jnp.zeros_like(dysg_ref)
            doutb_ref[...] = jnp.zeros_like(doutb_ref)

        @pl.when(i >= nct)
        def _():
            dhp = dh_ref[...]
            doutb = (mp_ref[3:4] * dhp).astype(MXU)
            doutb_ref[...] = doutb.astype(ACT)
            dysg_ref[...] = _nt(doutb, wout_ref[D:2 * D, :])
            dyssd = _nt(doutb, wout_ref[0:D, :])
            xs = xs_ref[...]
            ytot = yf_ref[...] + yb_ref[...] + dl_ref[...] * xs
            _, vjp = jax.vjp(_gate_norm, ytot, z_ref[...], ng_ref[...])
            dytot, dz, dng = vjp(dyssd)
            dy_ref[...] = dytot
            dz_ref[...] = dz.astype(ACT)
            pg_ref[0:1] += _sum0(dhp * out_ref[...])
            pg_ref[1:2] += dng
            pg_ref[2:3] += _sum0(dytot * xs)

    return _call(body, "hy4_bwd", (n_rows // T,),
                 [_rows_lat(T, D, nct), _rows_lat(T, D, nct), _rows(T, D), _rows(T, D), _rows(T, D), _rows(T, D),
                  _full(8, D), _full(1, D), _full(1, D), _wfull(2 * D, D)],
                 [_rows(T, D), _rows(T, D), _rows(T, D), _rows_lat(T, D, nct), _full(8, D)],
                 [_sds((n_rows, D)), _sds((n_rows, D), ACT), _sds((n_rows, D)), _sds((n_lat, D), ACT), _sds((8, D))],
                 )(dh, out, yf, yb, xbc, z, mp, dl, ng, wout)


def _loss_tail(hv, tgt, g, pg_ref, ls_ref):
    @pl.when(pl.program_id(0) == 0)
    def _():
        pg_ref[...] = jnp.zeros_like(pg_ref)
        ls_ref[...] = jnp.zeros_like(ls_ref)
    r = lax.rsqrt(jnp.mean(hv * hv, axis=-1, keepdims=True) + EPS)
    n = hv * r
    e = n * g - tgt
    ls_ref[...] += 0.5 * jnp.sum(jnp.sum(e * e, axis=1, keepdims=True), axis=0, keepdims=True) * (1.0 / D)
    dyv = e * (1.0 / D)
    pg_ref[0:1] += _sum0(dyv * n)
    dn = dyv * g
    return r * (dn - n * jnp.mean(dn * n, axis=-1, keepdims=True))


def _pad_rows(a, rows):
    return jnp.concatenate([a, jnp.zeros((rows - a.shape[0],) + a.shape[1:], a.dtype)], axis=0)


def _mp(*rows):
    return _pad_rows(jnp.stack(rows, axis=0), 8)


def _local_step(x, ctx, tgt, ada, cada0, w, late_w=None, early_grads=None, small_grads=None):
    n_lat, n_ctx = x.shape[0], ctx.shape[0]
    nct, ncc = n_ctx // T, n_ctx // Q
    a0 = [ada[0, k * D:(k + 1) * D] for k in range(6)]
    a1 = [ada[1, k * D:(k + 1) * D] for k in range(6)]
    c0 = [cada0[k * D:(k + 1) * D] for k in range(6)]
    g = {}

    mp2 = jnp.stack([_mp(w["norm_mix_g"][0], c0[0], c0[1]), _mp(w["norm_mix_g"][0], a0[0], a0[1], a0[2])], axis=0)
    mp_l0 = mp2[1]
    sgbt = _pad_cols(w["sg_b"][0].T, LANE)
    lng, lnb = w["sg_ln_g"][0][None], w["sg_ln_b"][0][None]
    z, uv, xbcp, dtr, ysg = _hy1_fwd(ctx, x, mp2, w["wz"], w["wuv"], w["wxbc"], w["wdt"], lng, lnb, w["sg_w"], sgbt, nct)
    cw = _pad_rows(w["ssd_conv_w"][0], 8)
    cb = w["ssd_conv_b"][0][None]
    xbc = _cv5_fwd(xbcp, cw, cb, nct)
    sp = _pad_rows(jnp.stack([_pad_cols(w["ssd_dt_bias"][0].reshape(1, 32), LANE)[0],
                              _pad_cols(w["ssd_a_log"][0].reshape(1, 32), LANE)[0]], axis=0), 8)
    dl = jnp.repeat(w["ssd_d"][0], 64)[None]
    ng = w["ssd_norm_g"][0][None]
    yf, hin_f = _ssd_fwd(xbc, dtr, sp, ncc, False)
    yb, hin_b = _ssd_fwd(xbc, dtr, sp, ncc, True)
    if late_w is not None:
        w = {**w, **late_w(yb)}
    h1, yssd, out0 = _hy4_fwd(x, yf, yb, xbc, z, ysg, mp_l0, dl, ng, w["hy_w_out"], nct)

    mpm0 = _mp(w["norm_mlp_g"][0], a0[3], a0[4], a0[5])
    h2, am0, ym0 = _mlp_fwd(h1, mpm0, w["wpack"], 0, "mlp0_fwd")

    mpc = _mp(w["norm_mix_g"][1], a1[0], a1[1], a1[2])
    wdw = _pad_rows(w["cf_w_dw"][0], 32)
    glu, acf = _cf1_fwd(h2, mpc, w["cf_w_pw1"], w["cf_b_pw1"])
    h3, cv, scf, ycf = _cf2_fwd(h2, glu, mpc, wdw, w["cf_b_dw"], w["cf_ln_g"], w["cf_ln_b"], w["cf_w_pw2"], w["cf_b_pw2"])

    mpm1 = _mp(w["norm_mlp_g"][1], a1[3], a1[4], a1[5])
    dh4, am1, ym1, pg_f, ls = _mlp_fwd(h3, mpm1, w["wpack"], 1, "mlp1_fwd", tail=(tgt, w["final_norm_g"][None]))

    loss = ls[0, 0]
    g["final_norm_g"] = pg_f[0]

    gp = {}
    dh3, da1, dy1, u1, pgm1 = _mlp_bwd(dh4, h3, am1, ym1, mpm1, w["wpack"], 1, "mlp1_bwd")
    gw1_1 = _tn(u1, da1, "tn_mlp1_w1", shard=("col", 1024))
    gw2_1 = _tn(am1, dy1, "tn_mlp1_w2", relu2=True, shard=("row", 1024))

    dcv, dycf, pgc2 = _cf2_bwd(dh3, ycf, cv, mpc, w["cf_ln_g"], w["cf_ln_b"], w["cf_w_pw2"])
    gp["cf_w_pw2"] = _tn(scf, dycf, "tn_cf_pw2", shard=("row", 256))
    dh2, dacf, ucf, pgc1, pbc1, dwdw = _cf1_bwd(dh3, h2, acf, dcv, glu, mpc, wdw, w["cf_w_pw1"])
    gp["cf_w_pw1"] = _tn(ucf, dacf, "tn_cf_pw1", shard=("col", 512)).reshape(4, 512, 1024)
    g["cf_b_pw2"], g["cf_ln_g"], g["cf_ln_b"], g["cf_b_dw"] = pgc2[1], pgc2[2], pgc2[3], pgc2[4]
    g["cf_b_pw1"] = pbc1[0]
    g["cf_w_dw"] = dwdw[:CK]

    dh1, da0, dy0, u0, pgm0 = _mlp_bwd(dh2, h1, am0, ym0, mpm0, w["wpack"], 0, "mlp0_bwd")
    gp["mlp_w1"] = jnp.concatenate([_tn(u0, da0, "tn_mlp0_w1", shard=("col", 1024)), gw1_1], axis=1)
    gp["mlp_w2"] = jnp.concatenate([_tn(am0, dy0, "tn_mlp0_w2", relu2=True, shard=("row", 1024)), gw2_1], axis=1)
    g["norm_mlp_g"] = jnp.stack([pgm0[0], pgm1[0]])

    dyt, dz, dysg, doutb, pg4 = _hy4_bwd(dh1, out0, yf, yb, xbc, z, mp_l0, dl, ng, w["hy_w_out"], nct)
    gp["hy_w_out"] = jnp.concatenate([_tn(yssd, doutb, "tn_out_ssd", shard=("row", 512)),
                                      _tn(ysg, doutb, "tn_out_sg", shard=("row", 512))], axis=0)
    if early_grads is not None:
        sp = sp + early_grads(gp)
    head_of_lane = jnp.arange(D, dtype=jnp.int32)[:, None] // 64
    col = jnp.arange(LANE, dtype=jnp.int32)[None, :]
    dxf, ddf, pgsf = _ssd_bwd(xbc, dtr, dyt, hin_f, sp, dl, (col == head_of_lane).astype(F32), ncc, False)
    dxb, ddb, pgsb = _ssd_bwd(xbc, dtr, dyt, hin_b, sp, dl, (col == head_of_lane + 16).astype(F32), ncc, True)
    dcv5, pgcb = _cv5_bwd1(xbcp, dxf, dxb, cw, cb, nct)
    dxbcp, dcw = _cv5_bwd2(dcv5, xbcp, cw, nct)
    dx, ucat, duv, ddt, pg2, pln, dsgw, dsgbt = _hy1_bwd(
        ctx, x, uv, dz, dxbcp, ddf, ddb, dysg, dh1, mp2, w["wz"], w["wuv"], w["wxbc"], w["wdt"], lng, lnb, w["sg_w"], sgbt, nct)
    g["ssd_conv_w"], g["ssd_conv_b"] = dcw[:SK], pgcb[0]
    pgs = pgsf + pgsb
    g["ssd_dt_bias"], g["ssd_a_log"] = pgs[0, :32].reshape(2, 16), pgs[1, :32].reshape(2, 16)
    g["ssd_d"] = jnp.sum(pg4[2].reshape(16, 64), axis=1)
    g["ssd_norm_g"] = pg4[1]
    g["sg_ln_g"], g["sg_ln_b"] = pln[0], pln[1]
    g["sg_w"], g["sg_b"] = dsgw, dsgbt[:, :SGG].T
    g["norm_mix_g"] = jnp.stack([pg2[0, 0] + pg2[1, 0], pgc1[0]])

    zero = jnp.zeros((D,), F32)
    d_ada = jnp.stack([jnp.concatenate([pg2[1, 1], pg2[1, 2], pg4[0], pgm0[1], pgm0[2], pgm0[3]]),
                       jnp.concatenate([pgc1[1], pgc1[2], pgc2[0], pgm1[1], pgm1[2], pgm1[3]])])
    d_cada0 = jnp.concatenate([pg2[0, 1], pg2[0, 2], zero, zero, zero, zero])
    if small_grads is not None:
        ucat, _ = lax.optimization_barrier((ucat, small_grads(g, d_ada, d_cada0)))
    gp["hy_w_in"] = _win_join(_tn(ucat, dz, "tn_in_z"), _tn(ucat, dxbcp, "tn_in_xbc"), _tn(ucat, ddt, "tn_in_dt"),
                              _tn(ucat, duv, "tn_in_uv"))
    g["pieces"] = gp
    return loss, dx, g, d_ada, d_cada0


def _pad_cols(a, cols):
    return jnp.concatenate([a, jnp.zeros(a.shape[:-1] + (cols - a.shape[-1],), a.dtype)], axis=-1)


MESH = pl.DeviceIdType.MESH
ANY = pl.BlockSpec(memory_space=pl.ANY)
IN_VMEM = pl.BlockSpec(memory_space=pltpu.VMEM)


def _coords():
    return lax.axis_index("x"), lax.axis_index("y"), lax.axis_index("c")


def _ag8(x, name):
    r, wd = x.shape

    def body(x_ref, o_ref, send, recv, lsem):
        mx, my, mc = _coords()
        me = 4 * mx + 2 * my + mc
        mine = pltpu.make_async_copy(x_ref, o_ref.at[me], lsem)
        mine.start()
        sent, peers = [], []
        for k in range(1, 8):
            px = 1 - mx if k & 4 else mx
            py = 1 - my if k & 2 else my
            pc = 1 - mc if k & 1 else mc
            cp = pltpu.make_async_remote_copy(src_ref=x_ref, dst_ref=o_ref.at[me], send_sem=send.at[k - 1],
                                              recv_sem=recv.at[k - 1], device_id=(px, py, pc), device_id_type=MESH)
            cp.start()
            sent.append(cp)
            peers.append((4 * px + 2 * py + pc, (px, py, pc)))
        for k in range(1, 8):
            slot, peer = peers[k - 1]
            pltpu.make_async_remote_copy(src_ref=x_ref, dst_ref=o_ref.at[slot], send_sem=send.at[k - 1],
                                         recv_sem=recv.at[k - 1], device_id=peer, device_id_type=MESH).wait_recv()
        for cp in sent:
            cp.wait_send()
        mine.wait()

    return pl.pallas_call(
        body, name=name, out_shape=_sds((8, r, wd), x.dtype), in_specs=[IN_VMEM], out_specs=IN_VMEM,
        scratch_shapes=[pltpu.SemaphoreType.DMA((7,)), pltpu.SemaphoreType.DMA((7,)), pltpu.SemaphoreType.DMA(())],
        compiler_params=pltpu.CompilerParams(vmem_limit_bytes=VMEM_LIMIT))(x)


HBM = pl.BlockSpec(memory_space=pltpu.HBM)
SEM = pl.BlockSpec(memory_space=pltpu.SEMAPHORE)
EFFECT = pltpu.SideEffectType.DATAFLOW_SIDE_EFFECTING


def _x4_peers(in_ref, land_ref, send, recv, a2a):
    mx, my, mc = _coords()
    me = 2 * mx + my
    out = []
    for k in range(1, 4):
        px = 1 - mx if k & 2 else mx
        py = 1 - my if k & 1 else my
        pj = 2 * px + py
        mk = functools.partial(pltpu.make_async_remote_copy, src_ref=in_ref.at[pj] if a2a else in_ref,
                               send_sem=send.at[k - 1], recv_sem=recv.at[k - 1], device_id=(px, py, mc), device_id_type=MESH)
        out.append((mk(dst_ref=land_ref.at[me]), mk(dst_ref=land_ref.at[pj])))
    return out


def _x4_start(buf, name, a2a):
    r, wd = buf.shape[-2:]

    def body(in_ref, land_ref, send, recv, in_thru, land_thru, token):
        for start, _ in _x4_peers(in_ref, land_ref, send, recv, a2a):
            start.start()
        token[...] = jnp.zeros_like(token)

    land = lax.empty((4, r, wd), buf.dtype)
    return pl.pallas_call(
        body, name=name,
        out_shape=(pltpu.SemaphoreType.DMA((3,)), pltpu.SemaphoreType.DMA((3,)), pltpu.HBM(buf.shape, buf.dtype),
                   pltpu.HBM(land.shape, land.dtype), _sds((8, LANE))),
        in_specs=(HBM, HBM), out_specs=(SEM, SEM, HBM, HBM, IN_VMEM), input_output_aliases={0: 2, 1: 3},
        compiler_params=pltpu.CompilerParams(has_side_effects=EFFECT),
    )(pltpu.with_memory_space_constraint(buf, pltpu.HBM), pltpu.with_memory_space_constraint(land, pltpu.HBM))


def _x4_wait(send, recv, buf_thru, land_thru, after, name, a2a):
    def body(in_ref, land_ref, send_ref, recv_ref, after_ref, in_dead, got_ref):
        for _, arrive in _x4_peers(in_ref, land_ref, send_ref, recv_ref, a2a):
            arrive.wait_send()
            arrive.wait_recv()

    return pl.pallas_call(
        body, name=name, out_shape=(pltpu.HBM(buf_thru.shape, buf_thru.dtype), pltpu.HBM(land_thru.shape, land_thru.dtype)),
        in_specs=(HBM, HBM, SEM, SEM, ANY), out_specs=(HBM, HBM), input_output_aliases={0: 0, 1: 1},
        compiler_params=pltpu.CompilerParams(has_side_effects=EFFECT),
    )(buf_thru, land_thru, send, recv, after)


def _ag8_peers(x_ref, land_ref, send, recv):
    mx, my, mc = _coords()
    me = 4 * mx + 2 * my + mc
    out = []
    for k in range(1, 8):
        px = 1 - mx if k & 4 else mx
        py = 1 - my if k & 2 else my
        pc = 1 - mc if k & 1 else mc
        mk = functools.partial(pltpu.make_async_remote_copy, src_ref=x_ref, send_sem=send.at[k - 1], recv_sem=recv.at[k - 1],
                               device_id=(px, py, pc), device_id_type=MESH)
        out.append((mk(dst_ref=land_ref.at[me]), mk(dst_ref=land_ref.at[4 * px + 2 * py + pc])))
    return out


def _split_start(x, land_shape, peers, n_copies, name):
    def body(x_ref, land_ref, send, recv, x_thru, land_thru, token):
        for start, _ in peers(x_ref, land_ref, send, recv):
            start.start()
        token[...] = jnp.zeros_like(token)

    land = lax.empty(land_shape, x.dtype)
    return pl.pallas_call(
        body, name=name,
        out_shape=(pltpu.SemaphoreType.DMA((n_copies,)), pltpu.SemaphoreType.DMA((n_copies,)), pltpu.HBM(x.shape, x.dtype),
                   pltpu.HBM(land.shape, land.dtype), _sds((8, LANE))),
        in_specs=(HBM, HBM), out_specs=(SEM, SEM, HBM, HBM, IN_VMEM), input_output_aliases={0: 2, 1: 3},
        compiler_params=pltpu.CompilerParams(has_side_effects=EFFECT),
    )(pltpu.with_memory_space_constraint(x, pltpu.HBM), pltpu.with_memory_space_constraint(land, pltpu.HBM))


def _split_wait(handle, after, peers, name):
    send, recv, x_thru, land_thru, _ = handle

    def body(x_ref, land_ref, send_ref, recv_ref, after_ref, x_dead, got_ref):
        for _, arrive in peers(x_ref, land_ref, send_ref, recv_ref):
            arrive.wait_send()
            arrive.wait_recv()

    return pl.pallas_call(
        body, name=name, out_shape=(pltpu.HBM(x_thru.shape, x_thru.dtype), pltpu.HBM(land_thru.shape, land_thru.dtype)),
        in_specs=(HBM, HBM, SEM, SEM, ANY), out_specs=(HBM, HBM), input_output_aliases={0: 0, 1: 1},
        compiler_params=pltpu.CompilerParams(has_side_effects=EFFECT),
    )(x_thru, land_thru, send, recv, after)


def _sib_peers(x_ref, land_ref, send, recv):
    mx, my, mc = _coords()
    cp = pltpu.make_async_remote_copy(src_ref=x_ref, dst_ref=land_ref, send_sem=send.at[0], recv_sem=recv.at[0],
                                      device_id=(mx, my, 1 - mc), device_id_type=MESH)
    return [(cp, cp)]


def _xchg_sib(x, name):
    def body(in_ref, o_ref, send, recv):
        mx, my, mc = _coords()
        cp = pltpu.make_async_remote_copy(src_ref=in_ref, dst_ref=o_ref, send_sem=send, recv_sem=recv,
                                          device_id=(mx, my, 1 - mc), device_id_type=MESH)
        cp.start()
        cp.wait_recv()
        cp.wait_send()

    return pl.pallas_call(
        body, name=name, out_shape=_sds(x.shape, x.dtype), in_specs=[ANY], out_specs=ANY,
        scratch_shapes=[pltpu.SemaphoreType.DMA(()), pltpu.SemaphoreType.DMA(())])(x)


def _sum_slots(gat, slots, name, tr=None):
    n, r, wd = gat.shape
    tr = r if tr is None else tr

    def body(g_ref, o_ref):
        acc = g_ref[slots[0]].astype(F32)
        for s in slots[1:]:
            acc = acc + g_ref[s].astype(F32)
        o_ref[...] = acc

    return _call(body, name, (r // tr,), [pl.BlockSpec((n, tr, wd), lambda i: (0, i, 0))], _rows(tr, wd), _sds((r, wd)))(gat)


def _add(a, b, name, tr):
    def body(a_ref, b_ref, o_ref):
        o_ref[...] = a_ref[...] + b_ref[...]

    r, wd = a.shape
    return _call(body, name, (r // tr,), [_rows(tr, wd), _rows(tr, wd)], _rows(tr, wd), _sds((r, wd)))(a, b)


def _ada_fwd(x16, ada_w_loc, ada_b_loc):
    nloc = ada_w_loc.shape[-1]

    def body(x_ref, w_ref, b_ref, s_ref, o_ref):
        s = _silu(x_ref[...])
        s_ref[...] = s
        o_ref[...] = jnp.dot(s, w_ref[...], precision=HI, preferred_element_type=F32) + b_ref[...]

    return _call(body, "ada_fwd", (2,),
                 [_full(16, D), pl.BlockSpec((None, D, nloc), lambda l: (l, 0, 0)), pl.BlockSpec((None, 1, nloc), lambda l: (l, 0, 0))],
                 [_full(16, D), pl.BlockSpec((None, 16, nloc), lambda l: (l, 0, 0))],
                 [_sds((16, D)), _sds((2, 16, nloc))])(x16, ada_w_loc, ada_b_loc[:, None, :])


def _ada_bwd(s16, d_loc, ada_w_loc):
    nloc = ada_w_loc.shape[-1]

    def body(s_ref, d_ref, w_ref, gw_ref, cp_ref):
        gw_ref[...] = lax.dot_general(s_ref[...], d_ref[...], (((0,), (0,)), ((), ())), precision=HI,
                                      preferred_element_type=F32)

        @pl.when(pl.program_id(0) == 0)
        def _():
            cp_ref[...] = lax.dot_general(d_ref[8:16, :], w_ref[...], (((1,), (1,)), ((), ())), precision=HI,
                                          preferred_element_type=F32)

    return _call(body, "ada_bwd", (2,),
                 [_full(16, D), pl.BlockSpec((None, 16, nloc), lambda l: (l, 0, 0)), pl.BlockSpec((None, D, nloc), lambda l: (l, 0, 0))],
                 [pl.BlockSpec((None, D, nloc), lambda l: (l, 0, 0)), _full(8, D)],
                 [_sds((2, D, nloc)), _sds((8, D))])(s16, d_loc, ada_w_loc)


def _cctx_grad(dscc, c_ctx):
    def body(d_ref, c_ref, o_ref):
        _, vjp = jax.vjp(_silu, c_ref[...])
        o_ref[...] = vjp(d_ref[...])[0]

    return _call(body, "cctx_grad", (1,), [_full(8, D), _full(8, D)], _full(8, D), _sds((8, D)))(dscc, c_ctx)


def _adamw_math(w, g, m, v):
    mn = ADAM_B1 * m + (1.0 - ADAM_B1) * g
    vn = ADAM_B2 * v + (1.0 - ADAM_B2) * jnp.square(g)
    c1 = 1.0 - ADAM_B1 ** ADAM_STEP
    c2 = 1.0 - ADAM_B2 ** ADAM_STEP
    return -ADAM_LR * ((mn / c1) / (jnp.sqrt(vn / c2) + ADAM_EPS) + ADAM_WD * w), mn, vn


def _adamw(w, g, m, v, name):
    n_l, r, wd = w.shape
    tr = 256 if r % 256 == 0 else r

    def body(w_ref, g_ref, m_ref, v_ref, d_ref, mo_ref, vo_ref):
        d_ref[...], mo_ref[...], vo_ref[...] = _adamw_math(w_ref[...], g_ref[...], m_ref[...], v_ref[...])

    spec = pl.BlockSpec((None, tr, wd), lambda a, i: (a, i, 0))
    return tuple(_call(body, name, (n_l, r // tr), [spec] * 4, [spec] * 3, [_sds(w.shape)] * 3)(w, g, m, v))


def _adamw_rows(w, part, sib, m, v, r0, name):
    rows = w.shape[0]
    tr = 256

    def body(w_ref, p_ref, s_ref, m_ref, v_ref, g_ref, d_ref, mo_ref, vo_ref):
        g = p_ref[...] + s_ref[...]
        g_ref[...] = g
        d_ref[...], mo_ref[...], vo_ref[...] = _adamw_math(w_ref[...], g, m_ref[...], v_ref[...])

    here, there = _rows(tr, ROW), _rows(tr, ROW, r0 // tr)
    return tuple(_call(body, name, (rows // tr,), [here, there, there, here, here], [here] * 4, [_sds(w.shape)] * 4)(
        w, part, sib, m, v))


def _adamw_small(ws, gs, ms, vs, name):
    n = len(ws)
    shapes = [a.shape for a in ws]
    as2d = lambda a: a.reshape(-1, a.shape[-1])

    def body(*refs):
        ins, outs = refs[:4 * n], refs[4 * n:]
        for k in range(n):
            res = _adamw_math(ins[k][...], ins[n + k][...], ins[2 * n + k][...], ins[3 * n + k][...])
            for j in range(3):
                outs[j * n + k][...] = res[j]

    flat = [as2d(a) for group in (ws, gs, ms, vs) for a in group]
    specs = [_full(*a.shape) for a in flat]
    outs = _call(body, name, (1,), specs, specs[:n] * 3, [_sds(a.shape) for a in flat[:n]] * 3)(*flat)
    return tuple([outs[j * n + k].reshape(shapes[k]) for k in range(n)] for j in range(3))


ROW = 1024


def _nrows(size):
    return -(-size // ROW)


def _pack(arrs, rows_total, dtype=F32):
    parts = []
    for a in arrs:
        flat = a.reshape(-1).astype(dtype)
        pad = _nrows(flat.shape[0]) * ROW - flat.shape[0]
        parts.append(flat if pad == 0 else jnp.concatenate([flat, jnp.zeros((pad,), dtype)]))
    flat = jnp.concatenate(parts)
    out = flat.reshape(-1, ROW)
    return _pad_rows(out, rows_total)


def _unpack(buf, shapes):
    lead = buf.shape[:-2]
    out, r0 = [], 0
    for shp in shapes:
        size = 1
        for s in shp:
            size *= s
        nr = _nrows(size)
        piece = lax.slice_in_dim(buf, r0, r0 + nr, axis=len(lead))
        out.append(piece.reshape(lead + (nr * ROW,))[..., :size].reshape(lead + tuple(shp)))
        r0 += nr
    return out


SLOT = 16


def _slot_rows(size):
    return _round_up(size // ROW, SLOT)


def _pack_rows(arrs, rows_total, dtype):
    parts, used = [], 0
    for a in arrs:
        part = a.astype(dtype).reshape(-1, ROW)
        extra = _slot_rows(a.size) - part.shape[0]
        parts.append(part if extra == 0 else jnp.pad(part, ((0, extra), (0, 0))))
        used += _slot_rows(a.size)
    if rows_total > used:
        parts.append(jnp.zeros((rows_total - used, ROW), dtype))
    return jnp.concatenate(parts, axis=0)


def _unpack_rows(buf, shapes):
    lead = buf.shape[:-2]
    out, r0 = [], 0
    for shp in shapes:
        size = 1
        for s in shp:
            size *= s
        piece = lax.slice_in_dim(buf, r0, r0 + size // ROW, axis=len(lead))
        out.append(piece.reshape(lead + tuple(shp)))
        r0 += _slot_rows(size)
    return out


def _round_up(n, k):
    return -(-n // k) * k


WEIGHTS = ['c_ctx', 'ada_w', 'ada_b', 'norm_mix_g', 'norm_mlp_g', 'mlp_w1', 'mlp_w2', 'hy_w_in', 'ssd_conv_w', 'ssd_conv_b',
           'ssd_dt_bias', 'ssd_a_log', 'ssd_d', 'ssd_norm_g', 'sg_ln_g', 'sg_ln_b', 'sg_w', 'sg_b', 'hy_w_out', 'cf_w_pw1',
           'cf_b_pw1', 'cf_w_dw', 'cf_b_dw', 'cf_ln_g', 'cf_ln_b', 'cf_w_pw2', 'cf_b_pw2', 'final_norm_g']
BIG = {'mlp_w1': 2, 'mlp_w2': 1, 'hy_w_in': 2, 'hy_w_out': 1, 'cf_w_pw1': 2, 'cf_w_pw2': 1}
SMALL_SHARD = ['ssd_conv_w', 'cf_b_pw1', 'cf_w_dw', 'cf_b_dw', 'cf_ln_g', 'cf_ln_b', 'cf_b_pw2']
REP = ['norm_mix_g', 'norm_mlp_g', 'ssd_conv_b', 'ssd_dt_bias', 'ssd_a_log', 'ssd_d', 'ssd_norm_g', 'sg_ln_g', 'sg_ln_b',
       'sg_w', 'sg_b', 'final_norm_g']


def _gather_shards(stacked, axis):
    return jnp.concatenate([stacked[j] for j in range(4)], axis=axis)


def kernel(x, c, ctx, c_ctx, ada_w, ada_b, norm_mix_g, norm_mlp_g, mlp_w1, mlp_w2, hy_w_in, ssd_conv_w, ssd_conv_b, ssd_dt_bias, ssd_a_log, ssd_d, ssd_norm_g, sg_ln_g, sg_ln_b, sg_w, sg_b, hy_w_out, cf_w_pw1, cf_b_pw1, cf_w_dw, cf_b_dw, cf_ln_g, cf_ln_b, cf_w_pw2, cf_b_pw2, final_norm_g, loss_target, m_c_ctx, m_ada_w, m_ada_b, m_norm_mix_g, m_norm_mlp_g, m_mlp_w1, m_mlp_w2, m_hy_w_in, m_ssd_conv_w, m_ssd_conv_b, m_ssd_dt_bias, m_ssd_a_log, m_ssd_d, m_ssd_norm_g, m_sg_ln_g, m_sg_ln_b, m_sg_w, m_sg_b, m_hy_w_out, m_cf_w_pw1, m_cf_b_pw1, m_cf_w_dw, m_cf_b_dw, m_cf_ln_g, m_cf_ln_b, m_cf_w_pw2, m_cf_b_pw2, m_final_norm_g, v_c_ctx, v_ada_w, v_ada_b, v_norm_mix_g, v_norm_mlp_g, v_mlp_w1, v_mlp_w2, v_hy_w_in, v_ssd_conv_w, v_ssd_conv_b, v_ssd_dt_bias, v_ssd_a_log, v_ssd_d, v_ssd_norm_g, v_sg_ln_g, v_sg_ln_b, v_sg_w, v_sg_b, v_hy_w_out, v_cf_w_pw1, v_cf_b_pw1, v_cf_w_dw, v_cf_b_dw, v_cf_ln_g, v_cf_ln_b, v_cf_w_pw2, v_cf_b_pw2, v_final_norm_g):
    args = locals()
    wl = {n: args[n] for n in WEIGHTS}
    ml = {n: args["m_" + n] for n in WEIGHTS}
    vl = {n: args["v_" + n] for n in WEIGHTS}
    mx, my, mc = _coords()
    me = 4 * mx + 2 * my + mc
    shard = 2 * mx + my
    even = (0, 2, 4, 6)

    def start_gather(names, name, tie=None):
        rows = sum(_slot_rows(wl[n].size) for n in names)
        buf = _pack_rows([wl[n] for n in names], rows, MXU)
        if tie is not None:
            buf, _ = lax.optimization_barrier((buf, tie))
        return _x4_start(buf, name, a2a=False)

    def finish_gather(handle, names, after, name):
        send, recv, own, land, _ = handle
        own, land = _x4_wait(send, recv, own, land, after, name, a2a=False)
        got = lax.dynamic_update_slice(land, own[None], (shard, 0, 0))
        shapes = [wl[n].shape for n in names]
        wfull = {n: _gather_shards(st, BIG[n]) for n, st in zip(names, _unpack_rows(got, shapes))}
        return wfull, got

    rest_names = ["mlp_w1", "mlp_w2", "hy_w_out", "cf_w_pw1", "cf_w_pw2"]
    h_in = _x4_start(hy_w_in[0].astype(MXU), "agw_in_start", a2a=False)
    c = c + h_in[4][0, 0]

    small_shapes = [wl[n].shape for n in SMALL_SHARD]
    blk1 = _pack([c] + [wl[n] for n in SMALL_SHARD], 24)
    got1 = _ag8(blk1, "ag_cond")
    x16 = _pad_rows(jnp.concatenate([got1[:, 0, :], c_ctx[None]], axis=0), 16)
    small_full = {}
    for n, parts in zip(SMALL_SHARD, _unpack(got1[:, 1:, :], small_shapes)):
        small_full[n] = jnp.concatenate([parts[s] for s in even], axis=-1)

    nloc = ada_w.shape[-1]
    ada_b_loc = lax.dynamic_slice_in_dim(ada_b, shard * nloc, nloc, axis=1)
    s16, ada_loc = _ada_fwd(x16, ada_w, ada_b_loc)
    got2 = _ag8(ada_loc.reshape(32, nloc), "ag_ada").reshape(8, 2, 16, nloc)
    ada_full = jnp.concatenate([got2[s] for s in even], axis=-1)
    ada_me = lax.dynamic_slice_in_dim(ada_full, me, 1, axis=1)[:, 0, :]
    cada0 = ada_full[0, 8, :]

    w = {n: wl[n] for n in WEIGHTS if n not in BIG and n not in SMALL_SHARD}
    w.update(small_full)
    h_rest = start_gather(rest_names, "agw_rest_start", tie=ada_me)
    send, recv, own, land, _ = h_in
    own, land = _x4_wait(send, recv, own, land, h_rest[4], "agw_in_wait", a2a=False)
    w["wz"], w["wxbc"], w["wdt"], w["wuv"] = _win_split(lax.dynamic_update_slice(land, own[None], (shard, 0, 0)))
    w["sg_w"] = sg_w[0].astype(MXU)

    def late_w(after):
        wfull, got = finish_gather(h_rest, rest_names, after, "agw_rest_wait")
        return {"hy_w_out": wfull["hy_w_out"][0], "wpack": got,
                "cf_w_pw1": wfull["cf_w_pw1"][0], "cf_w_pw2": wfull["cf_w_pw2"][0]}

    full_shape = {n: wl[n].shape for n in WEIGHTS}
    for n in BIG:
        full_shape[n] = tuple(s * 4 if a == BIG[n] else s for a, s in enumerate(wl[n].shape))
    for n in SMALL_SHARD:
        full_shape[n] = wl[n].shape[:-1] + (wl[n].shape[-1] * 4,)

    early_names = ["mlp_w1", "mlp_w2", "cf_w_pw1", "cf_w_pw2", "hy_w_out"]
    early = {}

    def early_grads(gp):
        used = sum(gp[n].shape[1] for n in early_names)
        parts = [gp[n] for n in early_names] + [jnp.zeros((4, _round_up(used, 512) - used, ROW), jnp.bfloat16)]
        early["h"] = _x4_start(jnp.concatenate(parts, axis=1), "a2a_early_start", a2a=True)
        return early["h"][4][0, 0]

    sm_names = REP + SMALL_SHARD
    r_ada = sum(_nrows(wl[n].size * (4 if n in SMALL_SHARD else 1)) for n in sm_names)
    small = {}

    def small_grads(g_, d_ada_, d_cada0_):
        buf = _pack([g_[n] for n in sm_names] + [d_ada_, d_cada0_], _round_up(r_ada + 18, 16), jnp.bfloat16)
        small["h"] = _split_start(buf, (8,) + buf.shape, _ag8_peers, 7, "ag_small_start")
        return small["h"][4][0, 0]

    loss_part, dx, g, d_ada, d_cada0 = _local_step(x[0], ctx[0], loss_target[0], ada_me, cada0, w, late_w, early_grads,
                                                   small_grads)
    gp_last = g["pieces"]["hy_w_in"]

    delta, new_m, new_v, grads = {}, {}, {}, {}
    h_last = _x4_start(gp_last, "a2a_last_start", a2a=True)
    send, recv, own, land, _ = early["h"]
    own, land = _x4_wait(send, recv, own, land, h_last[4], "a2a_early_wait", a2a=True)
    got = lax.dynamic_update_slice(land, lax.dynamic_slice_in_dim(own, shard, 1, axis=0), (shard, 0, 0))
    part_early = _sum_slots(got, (0, 1, 2, 3), "sum_grads_early", tr=512)
    h_swap = _split_start(part_early, part_early.shape, _sib_peers, 1, "swap_early_start")

    small_in, land3 = _split_wait(small["h"], h_swap[4], _ag8_peers, "ag_small_wait")
    got3 = lax.dynamic_update_slice(land3, small_in[None], (me, 0, 0))
    tot3 = _sum_slots(got3, tuple(range(8)), "sum_small")
    sm_tot = _unpack(tot3, [full_shape[n] for n in sm_names] + [(2, 6 * D), (6 * D,)])
    grads.update(zip(sm_names, sm_tot[:-2]))
    for n in SMALL_SHARD:
        k = wl[n].shape[-1]
        grads[n] = lax.dynamic_slice_in_dim(grads[n], shard * k, k, axis=grads[n].ndim - 1)
    dada_tot, dcada_tot = sm_tot[-2], sm_tot[-1]
    grads["ada_b"] = dada_tot.at[0].add(dcada_tot)
    dada_all = got3[:, r_ada:r_ada + 12, :].astype(F32).reshape(8, 2, 6 * D)
    d16 = jnp.concatenate([jnp.transpose(dada_all, (1, 0, 2)),
                           jnp.stack([dcada_tot, jnp.zeros_like(dcada_tot)])[:, None, :],
                           jnp.zeros((2, 7, 6 * D), F32)], axis=1)
    d_loc = lax.dynamic_slice_in_dim(d16, shard * nloc, nloc, axis=2)
    grads["ada_w"], cpart = _ada_bwd(s16, d_loc, ada_w)
    cpart = lax.dynamic_update_slice(cpart, jnp.full((1, D), loss_part, F32), (1, 0))
    h_cctx = _split_start(cpart, (8,) + cpart.shape, _ag8_peers, 7, "ag_cctx_start")
    grads["ada_w"], _ = lax.optimization_barrier((grads["ada_w"], h_cctx[4]))

    delta["ada_w"], new_m["ada_w"], new_v["ada_w"] = _adamw(ada_w, grads["ada_w"], ml["ada_w"], vl["ada_w"], "adamw_ada_w")
    small_names = ["ada_b"] + REP + SMALL_SHARD
    outs = _adamw_small(*([src[n].reshape(wl[n].shape) for n in small_names] for src in (wl, grads, ml, vl)), "adamw_small")
    for dst, vals in zip((delta, new_m, new_v), outs):
        dst.update(zip(small_names, vals))

    def step(n, tot):
        grads[n] = tot.reshape(wl[n].shape)
        delta[n], new_m[n], new_v[n] = _adamw(wl[n], grads[n], ml[n], vl[n], "adamw_" + n)

    part_early, sib_early = _split_wait(h_swap, delta["ada_w"], _sib_peers, "swap_early_wait")
    r0 = 0
    for n in early_names:
        nr, shp = _slot_rows(wl[n].size), wl[n].shape
        if shp[-1] == ROW:
            outs = _adamw_rows(wl[n].reshape(-1, ROW), part_early, sib_early, ml[n].reshape(-1, ROW), vl[n].reshape(-1, ROW),
                               r0, "adamw_" + n)
            grads[n], delta[n], new_m[n], new_v[n] = (o.reshape(shp) for o in outs)
        else:
            step(n, _add(lax.slice_in_dim(part_early, r0, r0 + nr), lax.slice_in_dim(sib_early, r0, r0 + nr), "add_" + n, nr))
        r0 += nr
    send, recv, own, land, _ = h_last
    own, land = _x4_wait(send, recv, own, land, delta["mlp_w1"], "a2a_last_wait", a2a=True)
    got = lax.dynamic_update_slice(land, lax.dynamic_slice_in_dim(own, shard, 1, axis=0), (shard, 0, 0))
    part_last = _sum_slots(got, (0, 1, 2, 3), "sum_grads_last", tr=512)
    step("hy_w_in", _add(part_last, _xchg_sib(part_last, "swap_grads_last"), "add_grads_last", 512))

    c_in, land4 = _split_wait(h_cctx, delta["hy_w_in"], _ag8_peers, "ag_cctx_wait")
    got4 = lax.dynamic_update_slice(land4, c_in[None], (me, 0, 0))
    loss = _sum_slots(got4, tuple(range(8)), "sum_loss")[1, 0]
    grads["c_ctx"] = _cctx_grad(_sum_slots(got4, even, "sum_cctx"), _pad_rows(c_ctx[None], 8))[0]
    outs = _adamw_small(*([src["c_ctx"]] for src in (wl, grads, ml, vl)), "adamw_c_ctx")
    delta["c_ctx"], new_m["c_ctx"], new_v["c_ctx"] = (o[0] for o in outs)

    return (loss, dx[None], *[grads[n].reshape(wl[n].shape) for n in WEIGHTS], *[delta[n] for n in WEIGHTS],
            *[new_m[n] for n in WEIGHTS], *[new_v[n] for n in WEIGHTS])
```

```python
import functools

import jax
import jax.numpy as jnp
from jax import lax
from jax.experimental import pallas as pl
from jax.experimental.pallas import tpu as pltpu

F32 = jnp.float32
MXU = jnp.bfloat16
ACT = jnp.bfloat16
HI = lax.Precision.HIGHEST
EPS = 1e-6

D = 1024
HID = 4096
XBC = 1536
Q = 128
SGG = 8
CK = 31
SK = 5
T = 256
LANE = 128
VMEM_LIMIT = 56 * 1024 * 1024

ADAM_LR, ADAM_B1, ADAM_B2, ADAM_EPS, ADAM_WD, ADAM_STEP = 0.001, 0.9, 0.999, 1e-08, 0.01, 10


def _call(body, name, grid, in_specs, out_specs, out_shape, scratch=()):
    return pl.pallas_call(
        body, name=name, grid=grid, in_specs=in_specs, out_specs=out_specs, out_shape=out_shape,
        scratch_shapes=list(scratch),
        compiler_params=pltpu.CompilerParams(dimension_semantics=("arbitrary",) * len(grid),
                                             vmem_limit_bytes=VMEM_LIMIT))


def _sds(shape, dt=F32):
    return jax.ShapeDtypeStruct(tuple(shape), dt)


def _rows(t, w, off=0, lane_blk=0):
    return pl.BlockSpec((t, w), lambda i: (i + off, lane_blk))


def _rows_lat(t, w, nct):
    return pl.BlockSpec((t, w), lambda i: (jnp.maximum(i - nct, 0), 0))


def _full(*shape):
    return pl.BlockSpec(shape, lambda *_: (0,) * len(shape))


def _wfull(*shape):
    return pl.BlockSpec(shape, lambda *_: (0,) * len(shape), pipeline_mode=pl.Buffered(1))


def _halo(w, hb, nrows):
    r, nb = T // hb, nrows // hb
    prev = pl.BlockSpec((hb, w), lambda i: (jnp.maximum(i * r - 1, 0), 0))
    nxt = pl.BlockSpec((hb, w), lambda i: (jnp.minimum((i + 1) * r, nb - 1), 0))
    return prev, nxt


def _nn(a, b):
    return jnp.dot(a.astype(MXU), b.astype(MXU), preferred_element_type=F32)


def _nt(a, b):
    return lax.dot_general(a.astype(MXU), b.astype(MXU), (((1,), (1,)), ((), ())), preferred_element_type=F32)


def _tn_dot(a, b):
    return lax.dot_general(a.astype(MXU), b.astype(MXU), (((0,), (0,)), ((), ())), preferred_element_type=F32)


def _sum0(x):
    return jnp.sum(x, axis=0, keepdims=True)


def _silu(x):
    return x * jax.nn.sigmoid(x)


def _gelu(x):
    return jax.nn.gelu(x, approximate=True)


def _mod(h, g, sh, sc):
    n = h * lax.rsqrt(jnp.mean(h * h, axis=-1, keepdims=True) + EPS)
    return n * g * (1.0 + sc) + sh


def _ln(x, g, b):
    xc = x - jnp.mean(x, axis=-1, keepdims=True)
    return xc * lax.rsqrt(jnp.mean(xc * xc, axis=-1, keepdims=True) + EPS) * g + b


def _gate_norm(ytot, z, ng):
    yg = ytot * _silu(z)
    halves = []
    for k in range(2):
        seg = yg[:, k * 512:(k + 1) * 512]
        halves.append(seg * lax.rsqrt(jnp.mean(seg * seg, axis=-1, keepdims=True) + EPS) * ng[:, k * 512:(k + 1) * 512])
    return jnp.concatenate(halves, axis=-1)


def _fill_ext(ext_ref, prev_ref, cur_ref, next_ref, hb, first, last):
    ext_ref[0:hb, :] = jnp.where(first, 0.0, prev_ref[...])
    ext_ref[hb:hb + T, :] = cur_ref[...]
    ext_ref[hb + T:hb + T + hb, :] = jnp.where(last, 0.0, next_ref[...])


def _conv(ext_ref, w_ref, k_taps, hb, lanes):
    off = hb - k_taps // 2
    acc = ext_ref[pl.ds(off, T), lanes] * w_ref[0:1, lanes]
    for k in range(1, k_taps):
        acc = acc + ext_ref[pl.ds(off + k, T), lanes] * w_ref[k:k + 1, lanes]
    return acc


def _conv_tr(ext_ref, w_ref, k_taps, hb, lanes):
    off = hb + k_taps // 2
    acc = ext_ref[pl.ds(off, T), lanes] * w_ref[0:1, lanes]
    for k in range(1, k_taps):
        acc = acc + ext_ref[pl.ds(off - k, T), lanes] * w_ref[k:k + 1, lanes]
    return acc


def _conv_wide(ext_ref, w_ref, k_taps, hb, lanes, flip=False):
    base = hb - k_taps // 2
    acc = None
    for b in range(8):
        taps = [k for k in range(k_taps) if (base + k) % 8 == b]
        if not taps:
            continue
        p = None
        for k in taps:
            wi = (k_taps - 1 - k) if flip else k
            term = ext_ref[pl.ds(base + k - b, T + 8), lanes] * w_ref[wi:wi + 1, lanes]
            p = term if p is None else p + term
        acc = p[b:b + T] if acc is None else acc + p[b:b + T]
    return acc


def _conv_dw_wide(dw_ref, d_ref, xext_ref, k_taps, hb, lanes):
    base = hb - k_taps // 2
    d = d_ref[:, lanes]
    for b in range(8):
        taps = [k for k in range(k_taps) if (base + k) % 8 == b]
        if not taps:
            continue
        lo_off = base + taps[0] - b
        span = base + taps[-1] - b - lo_off
        xs = xext_ref[pl.ds(lo_off + b, T + span), lanes]
        for k in taps:
            a = base + k - b - lo_off
            dw_ref[k:k + 1, lanes] += _sum0(d * xs[a:a + T])


def _conv_dw(dw_ref, d_ref, xext_ref, k_taps, hb, lanes):
    off = hb - k_taps // 2
    d = d_ref[:, lanes]
    for k in range(k_taps):
        dw_ref[k:k + 1, lanes] += _sum0(d * xext_ref[pl.ds(off + k, T), lanes])


def _tn(a, b, name, relu2=False, shard=None):
    m_rows, ka = a.shape
    n = b.shape[1]
    tm = next(t for t in (1024, 768, 512, 256) if m_rows % t == 0)
    tk = min(ka, 1024)
    tn = n if n <= 1024 else next(t for t in (1024, 768, 512, 384, 256, 128) if n % t == 0)
    n_m = m_rows // tm
    per = 1 if shard is None else (tn if shard[0] == "col" else tk) // shard[1]

    def body(a_ref, b_ref, o_ref, *acc):
        acc_ref = acc[0] if acc else o_ref

        @pl.when(pl.program_id(2) == 0)
        def _():
            acc_ref[...] = jnp.zeros_like(acc_ref)
        av = a_ref[...]
        if relu2:
            av = jnp.square(jnp.maximum(av.astype(F32), 0.0))
        acc_ref[...] += _tn_dot(av, b_ref[...])
        if acc:
            @pl.when(pl.program_id(2) == n_m - 1)
            def _():
                for s in range(per):
                    if shard[0] == "col":
                        o_ref[s] = acc_ref[:, s * shard[1]:(s + 1) * shard[1]].astype(o_ref.dtype)
                    else:
                        o_ref[s] = acc_ref[s * shard[1]:(s + 1) * shard[1], :].astype(o_ref.dtype)

    in_specs = [pl.BlockSpec((tm, tk), lambda k, j, m: (m, k)), pl.BlockSpec((tm, tn), lambda k, j, m: (m, j))]
    if shard is None:
        return _call(body, name, (ka // tk, n // tn, n_m), in_specs,
                     pl.BlockSpec((tk, tn), lambda k, j, m: (k, j)), _sds((ka, n)))(a, b)
    if shard[0] == "col":
        out_spec = pl.BlockSpec((per, tk, shard[1]), lambda k, j, m: (j, k, 0))
        out_shape = _sds((n // shard[1], ka, shard[1]), jnp.bfloat16)
    else:
        out_spec = pl.BlockSpec((per, shard[1], tn), lambda k, j, m: (k, 0, j))
        out_shape = _sds((ka // shard[1], shard[1], n), jnp.bfloat16)
    return _call(body, name, (ka // tk, n // tn, n_m), in_specs, out_spec, out_shape,
                 scratch=[pltpu.VMEM((tk, tn), F32)])(a, b)


def _mlp_fwd(h, mp, wpack, layer, name, tail=None):
    n_rows = h.shape[0]

    def body(h_ref, mp_ref, w1_ref, w2_ref, *rest):
        if tail is None:
            hn_ref, a_ref, y_ref = rest
        else:
            t_ref, g_ref, hn_ref, a_ref, y_ref, pg_ref, ls_ref = rest
        hv = h_ref[...]
        u = _mod(hv, mp_ref[0:1], mp_ref[1:2], mp_ref[2:3]).astype(MXU)
        acc = jnp.zeros((T, D), F32)
        for j in range(HID // 1024):
            cs = slice(j * 1024, (j + 1) * 1024)
            a = jnp.dot(u, w1_ref[j], preferred_element_type=F32)
            a_ref[:, cs] = a.astype(ACT)
            acc = acc + jnp.dot(jnp.square(jnp.maximum(a, 0.0)).astype(MXU), w2_ref[j], preferred_element_type=F32)
        y_ref[...] = acc
        hn = hv + mp_ref[3:4] * acc
        hn_ref[...] = hn if tail is None else _loss_tail(hn, t_ref[...], g_ref[...], pg_ref, ls_ref)

    ins = [_rows(T, D), _full(8, D), _mlp_wspec(layer), _mlp_wspec(2 + layer)]
    outs = [_rows(T, D), _rows(T, HID), _rows(T, D)]
    shapes = [_sds((n_rows, D)), _sds((n_rows, HID), ACT), _sds((n_rows, D))]
    if tail is None:
        return _call(body, name, (n_rows // T,), ins, outs, shapes)(h, mp, wpack, wpack)
    return _call(body, name, (n_rows // T,), ins + [_rows(T, D), _full(1, D)], outs + [_full(8, D), _full(8, LANE)],
                 shapes + [_sds((8, D)), _sds((8, LANE))])(h, mp, wpack, wpack, *tail)


def _mlp_wspec(row_block):
    return pl.BlockSpec((4, 1024, 1024), lambda i: (0, row_block, 0), pipeline_mode=pl.Buffered(1))


def _mlp_bwd(dh, h, a, y, mp, wpack, layer, name, tail=None):
    n_rows = h.shape[0]

    def body(dh_ref, h_ref, a_ref, y_ref, mp_ref, w1_ref, w2_ref, *rest):
        if tail is None:
            dho_ref, da_ref, dyb_ref, ub_ref, pg_ref = rest
        else:
            (yc_ref, cv_ref, mpc_ref, lng_ref, lnb_ref, wc_ref,
             dho_ref, da_ref, dyb_ref, ub_ref, pg_ref, dcv_ref, dyc_ref, pgc_ref) = rest
        dhp = dh_ref[...]
        u, vjp = jax.vjp(_mod, h_ref[...], mp_ref[0:1], mp_ref[1:2], mp_ref[2:3])
        ub_ref[...] = u.astype(ACT)
        dyb = (mp_ref[3:4] * dhp).astype(MXU)
        dyb_ref[...] = dyb.astype(ACT)
        du = jnp.zeros((T, D), F32)
        for j in range(HID // 1024):
            cs = slice(j * 1024, (j + 1) * 1024)
            dp = _nt(dyb, w2_ref[j])
            da = dp * 2.0 * jnp.maximum(a_ref[:, cs].astype(F32), 0.0)
            da_ref[:, cs] = da.astype(ACT)
            du = du + _nt(da, w1_ref[j])
        dhn, dg, dsh, dsc = vjp(du)
        dho = dhp + dhn
        dho_ref[...] = dho

        @pl.when(pl.program_id(0) == 0)
        def _():
            pg_ref[...] = jnp.zeros_like(pg_ref)
        pg_ref[0:1] += dg
        pg_ref[1:2] += dsh
        pg_ref[2:3] += dsc
        pg_ref[3:4] += _sum0(dhp * y_ref[...])
        if tail is not None:
            dyc = mpc_ref[3:4] * dho
            dycb = dyc.astype(MXU)
            dyc_ref[...] = dycb.astype(ACT)
            _, vjp_c = jax.vjp(lambda cv_, g_, b_: _silu(_ln(cv_, g_, b_)), cv_ref[...], lng_ref[...], lnb_ref[...])
            dcv, dlng, dlnb = vjp_c(_nt(dycb, wc_ref[...]))
            dcv_ref[...] = dcv

            @pl.when(pl.program_id(0) == 0)
            def _():
                pgc_ref[...] = jnp.zeros_like(pgc_ref)
            pgc_ref[0:1] += _sum0(dho * yc_ref[...])
            pgc_ref[1:2] += _sum0(dyc)
            pgc_ref[2:3] += dlng
            pgc_ref[3:4] += dlnb
            pgc_ref[4:5] += _sum0(dcv)

    ins = [_rows(T, D), _rows(T, D), _rows(T, HID), _rows(T, D), _full(8, D), _mlp_wspec(layer), _mlp_wspec(2 + layer)]
    outs = [_rows(T, D), _rows(T, HID), _rows(T, D), _rows(T, D), _full(8, D)]
    shapes = [_sds((n_rows, D)), _sds((n_rows, HID), ACT), _sds((n_rows, D), ACT), _sds((n_rows, D), ACT), _sds((8, D))]
    if tail is None:
        return _call(body, name, (n_rows // T,), ins, outs, shapes)(dh, h, a, y, mp, wpack, wpack)
    ins += [_rows(T, D), _rows(T, D), _full(8, D), _full(1, D), _full(1, D), _wfull(D, D)]
    outs += [_rows(T, D), _rows(T, D), _full(8, D)]
    shapes += [_sds((n_rows, D)), _sds((n_rows, D), ACT), _sds((8, D))]
    return _call(body, name, (n_rows // T,), ins, outs, shapes)(dh, h, a, y, mp, wpack, wpack, *tail)


def _cf1_fwd(h, mp, w1, b1):
    n_rows = h.shape[0]

    def body(h_ref, mp_ref, w1_ref, b1_ref, glu_ref, a_ref):
        u = _mod(h_ref[...], mp_ref[0:1], mp_ref[1:2], mp_ref[2:3]).astype(MXU)
        a = jnp.dot(u, w1_ref[...], preferred_element_type=F32) + b1_ref[...]
        a_ref[...] = a.astype(ACT)
        glu_ref[...] = a[:, :D] * jax.nn.sigmoid(a[:, D:])

    return _call(body, "cf1_fwd", (n_rows // T,),
                 [_rows(T, D), _full(8, D), _wfull(D, 2 * D), _full(1, 2 * D)],
                 [_rows(T, D), _rows(T, 2 * D)],
                 [_sds((n_rows, D)), _sds((n_rows, 2 * D), ACT)])(h, mp, w1, b1)


def _cf2_fwd(h, glu, mp, wdw, bdw, lng, lnb, w2, b2):
    n_rows = h.shape[0]
    nt = n_rows // T
    hb = 16

    def body(h_ref, gp_ref, gc_ref, gn_ref, mp_ref, wdw_ref, bdw_ref, lng_ref, lnb_ref, w2_ref, b2_ref,
             hn_ref, cv_ref, sb_ref, y_ref, ext):
        i = pl.program_id(0)
        _fill_ext(ext, gp_ref, gc_ref, gn_ref, hb, i == 0, i == nt - 1)
        for c in range(D // LANE):
            lanes = slice(c * LANE, (c + 1) * LANE)
            cv_ref[:, lanes] = _conv_wide(ext, wdw_ref, CK, hb, lanes) + bdw_ref[:, lanes]
        s = _silu(_ln(cv_ref[...], lng_ref[...], lnb_ref[...])).astype(MXU)
        sb_ref[...] = s.astype(ACT)
        y = jnp.dot(s, w2_ref[...], preferred_element_type=F32) + b2_ref[...]
        y_ref[...] = y
        hn_ref[...] = h_ref[...] + mp_ref[3:4] * y

    gp, gn = _halo(D, hb, n_rows)
    return _call(body, "cf2_fwd", (nt,),
                 [_rows(T, D), gp, _rows(T, D), gn, _full(8, D), _full(32, D), _full(1, D), _full(1, D), _full(1, D),
                  _wfull(D, D), _full(1, D)],
                 [_rows(T, D), _rows(T, D), _rows(T, D), _rows(T, D)],
                 [_sds((n_rows, D)), _sds((n_rows, D)), _sds((n_rows, D), ACT), _sds((n_rows, D))],
                 scratch=[pltpu.VMEM((T + 2 * hb, D), F32)])(h, glu, glu, glu, mp, wdw, bdw, lng, lnb, w2, b2)


def _cf2_bwd(dh, y, cv, mp, lng, lnb, w2):
    n_rows = dh.shape[0]

    def body(dh_ref, y_ref, cv_ref, mp_ref, lng_ref, lnb_ref, w2_ref, dcv_ref, dyb_ref, pg_ref):
        dhp = dh_ref[...]
        dy = mp_ref[3:4] * dhp
        dyb = dy.astype(MXU)
        dyb_ref[...] = dyb.astype(ACT)
        ds = _nt(dyb, w2_ref[...])
        _, vjp = jax.vjp(lambda cv_, g_, b_: _silu(_ln(cv_, g_, b_)), cv_ref[...], lng_ref[...], lnb_ref[...])
        dcv, dlng, dlnb = vjp(ds)
        dcv_ref[...] = dcv

        @pl.when(pl.program_id(0) == 0)
        def _():
            pg_ref[...] = jnp.zeros_like(pg_ref)
        pg_ref[0:1] += _sum0(dhp * y_ref[...])
        pg_ref[1:2] += _sum0(dy)
        pg_ref[2:3] += dlng
        pg_ref[3:4] += dlnb
        pg_ref[4:5] += _sum0(dcv)

    return _call(body, "cf2_bwd", (n_rows // T,),
                 [_rows(T, D), _rows(T, D), _rows(T, D), _full(8, D), _full(1, D), _full(1, D), _wfull(D, D)],
                 [_rows(T, D), _rows(T, D), _full(8, D)],
                 [_sds((n_rows, D)), _sds((n_rows, D), ACT), _sds((8, D))])(dh, y, cv, mp, lng, lnb, w2)


def _cf1_bwd(dh, h, a, dcv, glu, mp, wdw, w1):
    n_rows = h.shape[0]
    nt = n_rows // T
    hb = 16

    def body(dh_ref, h_ref, a_ref, dp_ref, dc_ref, dn_ref, gp_ref, gc_ref, gn_ref, mp_ref, wdw_ref, w1_ref,
             dho_ref, da_ref, ub_ref, pg_ref, pb_ref, dw_ref, dext, gext, dglu):
        i = pl.program_id(0)

        @pl.when(i == 0)
        def _():
            pg_ref[...] = jnp.zeros_like(pg_ref)
            pb_ref[...] = jnp.zeros_like(pb_ref)
            dw_ref[...] = jnp.zeros_like(dw_ref)
        _fill_ext(dext, dp_ref, dc_ref, dn_ref, hb, i == 0, i == nt - 1)
        _fill_ext(gext, gp_ref, gc_ref, gn_ref, hb, i == 0, i == nt - 1)
        for c in range(D // LANE):
            lanes = slice(c * LANE, (c + 1) * LANE)
            dglu[:, lanes] = _conv_wide(dext, wdw_ref, CK, hb, lanes, flip=True)
            _conv_dw_wide(dw_ref, dc_ref, gext, CK, hb, lanes)
        av = a_ref[...].astype(F32)
        _, vjp_glu = jax.vjp(lambda a1, a2: a1 * jax.nn.sigmoid(a2), av[:, :D], av[:, D:])
        da1, da2 = vjp_glu(dglu[...])
        da_ref[:, :D] = da1.astype(ACT)
        da_ref[:, D:] = da2.astype(ACT)
        pb_ref[0:1, :D] += _sum0(da1)
        pb_ref[0:1, D:] += _sum0(da2)
        du = _nt(da1, w1_ref[:, :D]) + _nt(da2, w1_ref[:, D:])
        u, vjp = jax.vjp(_mod, h_ref[...], mp_ref[0:1], mp_ref[1:2], mp_ref[2:3])
        ub_ref[...] = u.astype(ACT)
        dhn, dg, dsh, dsc = vjp(du)
        dho_ref[...] = dh_ref[...] + dhn
        pg_ref[0:1] += dg
        pg_ref[1:2] += dsh
        pg_ref[2:3] += dsc

    hp, hn = _halo(D, hb, n_rows)
    return _call(body, "cf1_bwd", (nt,),
                 [_rows(T, D), _rows(T, D), _rows(T, 2 * D), hp, _rows(T, D), hn, hp, _rows(T, D), hn,
                  _full(8, D), _full(32, D), _wfull(D, 2 * D)],
                 [_rows(T, D), _rows(T, 2 * D), _rows(T, D), _full(8, D), _full(8, 2 * D), _full(32, D)],
                 [_sds((n_rows, D)), _sds((n_rows, 2 * D), ACT), _sds((n_rows, D), ACT), _sds((8, D)),
                  _sds((8, 2 * D)), _sds((32, D))],
                 scratch=[pltpu.VMEM((T + 2 * hb, D), F32), pltpu.VMEM((T + 2 * hb, D), F32), pltpu.VMEM((T, D), F32)],
                 )(dh, h, a, dcv, dcv, dcv, glu, glu, glu, mp, wdw, w1)


def _sg_blocks():
    return [(c, g, slice(c * Q, (c + 1) * Q), slice(g * LANE, (g + 1) * LANE)) for c in range(T // Q) for g in range(SGG)]


IN_W = D + XBC + 32 + 2 * D
IN_LOC = IN_W // 4


def _win_split(shards):
    o1, o2, o3 = D, D + XBC, D + XBC + 32
    tr = 256

    def cols(s_ref, lo, hi):
        parts = []
        for j in range(4):
            a, b = max(lo, j * IN_LOC), min(hi, (j + 1) * IN_LOC)
            if a < b:
                parts.append(s_ref[j][:, a - j * IN_LOC:b - j * IN_LOC])
        return parts[0] if len(parts) == 1 else jnp.concatenate(parts, axis=1)

    def body(s_ref, wz_ref, wxbc_ref, wdt_ref, wuv_ref):
        wz_ref[...] = cols(s_ref, 0, o1)
        wxbc_ref[...] = cols(s_ref, o1, o2)
        dt = cols(s_ref, o2, o3)
        wdt_ref[...] = jnp.concatenate([dt, jnp.zeros((tr, LANE - 32), dt.dtype)], axis=1)
        wuv_ref[...] = cols(s_ref, o3, IN_W)

    dt_ = shards.dtype
    return _call(body, "win_split", (D // tr,), [pl.BlockSpec((4, tr, IN_LOC), lambda i: (0, i, 0))],
                 [_rows(tr, D), _rows(tr, XBC), _rows(tr, LANE), _rows(tr, 2 * D)],
                 [_sds((D, D), dt_), _sds((D, XBC), dt_), _sds((D, LANE), dt_), _sds((D, 2 * D), dt_)])(shards)


def _win_join(gz, gxbc, gdt, guv):
    tr = 256
    bounds = (0, D, D + XBC, D + XBC + 32, IN_W)

    def body(gz_ref, gx_ref, gd_ref, gu_ref, o_ref):
        segs = (gz_ref, gx_ref, gd_ref, gu_ref)
        for j in range(4):
            parts = []
            for k in range(4):
                a, b = max(bounds[k], j * IN_LOC), min(bounds[k + 1], (j + 1) * IN_LOC)
                if a < b:
                    parts.append(segs[k][:, a - bounds[k]:b - bounds[k]])
            full = parts[0] if len(parts) == 1 else jnp.concatenate(parts, axis=1)
            o_ref[j] = full.astype(jnp.bfloat16)

    return _call(body, "win_join", (D // tr,), [_rows(tr, D), _rows(tr, XBC), _rows(tr, LANE), _rows(tr, 2 * D)],
                 pl.BlockSpec((4, tr, IN_LOC), lambda i: (0, i, 0)), _sds((4, D, IN_LOC), jnp.bfloat16))(gz, gxbc, gdt, guv)


def _ctx_spec(nct):
    return pl.BlockSpec((T, D), lambda i: (jnp.minimum(i, nct - 1), 0))


def _hy1_fwd(ctx, x, mp2, wz, wuv, wxbc, wdt, lng, lnb, sgw, sgbt, nct):
    n_lat = x.shape[0]
    n_rows = ctx.shape[0] + n_lat

    def body(c_ref, x_ref, mp_ref, wz_ref, wuv_ref, wxbc_ref, wdt_ref, lng_ref, lnb_ref, sgw_ref, sgbt_ref,
             z_ref, uv_ref, xbcp_ref, dtr_ref, ysg_ref):
        hv = jnp.where(pl.program_id(0) < nct, c_ref[...], x_ref[...])
        u = _mod(hv, mp_ref[0:1], mp_ref[1:2], mp_ref[2:3]).astype(MXU)
        z_ref[...] = jnp.dot(u, wz_ref[...], preferred_element_type=F32)
        xbcp_ref[...] = jnp.dot(u, wxbc_ref[...], preferred_element_type=F32)
        dtr_ref[...] = jnp.dot(u, wdt_ref[...], preferred_element_type=F32)
        uv = jnp.dot(u, wuv_ref[...], preferred_element_type=F32)
        uv_ref[...] = uv
        gate = _gelu(uv[:, :D])
        vln = _ln(_gelu(uv[:, D:]), lng_ref[...], lnb_ref[...]).astype(MXU)
        for _, g, rs, ls in _sg_blocks():
            s = jnp.dot(sgw_ref[g], vln[rs, ls], preferred_element_type=F32) + sgbt_ref[:, g:g + 1]
            ysg_ref[rs, ls] = (gate[rs, ls] * s).astype(ACT)

    mspec = pl.BlockSpec((None, 8, D), lambda i: (jnp.where(i < nct, 0, 1), 0, 0))
    return _call(body, "hy1_fwd", (n_rows // T,),
                 [_ctx_spec(nct), _rows_lat(T, D, nct), mspec, _wfull(D, D), _wfull(D, 2 * D), _wfull(D, XBC), _wfull(D, LANE),
                  _full(1, D), _full(1, D), _full(SGG, Q, Q), _full(Q, LANE)],
                 [_rows(T, D), _rows(T, 2 * D), _rows(T, XBC), _rows(T, LANE), _rows_lat(T, D, nct)],
                 [_sds((n_rows, D)), _sds((n_rows, 2 * D)), _sds((n_rows, XBC)), _sds((n_rows, LANE)),
                  _sds((n_lat, D), ACT)])(ctx, x, mp2, wz, wuv, wxbc, wdt, lng, lnb, sgw, sgbt)


def _hy1_bwd(ctx, x, uv, dz, dxbcp, ddf, ddb, dysg, dres, mp2, wz, wuv, wxbc, wdt, lng, lnb, sgw, sgbt, nct):
    n_lat = dres.shape[0]
    n_rows = ctx.shape[0] + n_lat

    def body(c_ref, x_ref, uv_ref, dz_ref, dxbcp_ref, ddf_ref, ddb_ref, dysg_ref, dres_ref, mp_ref, wz_ref, wuv_ref,
             wxbc_ref, wdt_ref, lng_ref, lnb_ref, sgw_ref, sgbt_ref,
             dho_ref, ub_ref, duv_ref, ddt_ref, pg2_ref, pl_ref, dsgw_ref, dsgb_ref, dgate_s, dvln_s):
        i = pl.program_id(0)

        @pl.when(i == 0)
        def _():
            pg2_ref[...] = jnp.zeros_like(pg2_ref)
            pl_ref[...] = jnp.zeros_like(pl_ref)
            dsgw_ref[...] = jnp.zeros_like(dsgw_ref)
            dsgb_ref[...] = jnp.zeros_like(dsgb_ref)
        uv = uv_ref[...]

        def f_sg(ug, uvv, g_, b_):
            return _gelu(ug), _ln(_gelu(uvv), g_, b_)
        (gate, vln), vjp_sg = jax.vjp(f_sg, uv[:, :D], uv[:, D:], lng_ref[...], lnb_ref[...])
        vlnb = vln.astype(MXU)
        lane = lax.broadcasted_iota(jnp.int32, (Q, LANE), 1)
        dsgb = jnp.zeros((Q, LANE), F32)
        for _, g, rs, ls in _sg_blocks():
            s = jnp.dot(sgw_ref[g], vlnb[rs, ls], preferred_element_type=F32) + sgbt_ref[:, g:g + 1]
            dyb = dysg_ref[rs, ls]
            dgate_s[rs, ls] = dyb * s
            ds = dyb * gate[rs, ls]
            dvln_s[rs, ls] = _tn_dot(sgw_ref[g], ds)
            dsgw_ref[g] += _nt(ds, vlnb[rs, ls])
            dsgb = dsgb + jnp.where(lane == g, jnp.sum(ds, axis=1, keepdims=True), 0.0)
        dsgb_ref[...] += dsgb
        dug, duvv, dlng, dlnb = vjp_sg((dgate_s[...], dvln_s[...]))
        pl_ref[0:1] += dlng
        pl_ref[1:2] += dlnb
        duv_ref[:, :D] = dug.astype(ACT)
        duv_ref[:, D:] = duvv.astype(ACT)
        ddt = (ddf_ref[...] + ddb_ref[...]).astype(MXU)
        ddt_ref[...] = ddt.astype(ACT)
        du = (_nt(dz_ref[...], wz_ref[...]) + _nt(dug, wuv_ref[:, :D]) + _nt(duvv, wuv_ref[:, D:])
              + _nt(dxbcp_ref[...], wxbc_ref[...]) + _nt(ddt, wdt_ref[...]))
        hv = jnp.where(i < nct, c_ref[...], x_ref[...])
        u, vjp = jax.vjp(_mod, hv, mp_ref[0:1], mp_ref[1:2], mp_ref[2:3])
        ub_ref[...] = u.astype(ACT)
        dhn, dg, dsh, dsc = vjp(du)
        dho_ref[...] = dres_ref[...] + dhn
        is_ctx = i < nct
        for k, val in enumerate((dg, dsh, dsc)):
            pg2_ref[0, k:k + 1] += jnp.where(is_ctx, val, 0.0)
            pg2_ref[1, k:k + 1] += jnp.where(is_ctx, 0.0, val)

    mspec = pl.BlockSpec((None, 8, D), lambda i: (jnp.where(i < nct, 0, 1), 0, 0))
    return _call(body, "hy1_bwd", (n_rows // T,),
                 [_ctx_spec(nct), _rows_lat(T, D, nct), _rows(T, 2 * D), _rows(T, D), _rows(T, XBC), _rows(T, LANE),
                  _rows(T, LANE), _rows(T, D),
                  _rows_lat(T, D, nct), mspec, _wfull(D, D), _wfull(D, 2 * D), _wfull(D, XBC), _wfull(D, LANE),
                  _full(1, D), _full(1, D), _full(SGG, Q, Q), _full(Q, LANE)],
                 [_rows_lat(T, D, nct), _rows(T, D), _rows(T, 2 * D), _rows(T, LANE), _full(2, 8, D), _full(8, D),
                  _full(SGG, Q, Q), _full(Q, LANE)],
                 [_sds((n_lat, D)), _sds((n_rows, D), ACT), _sds((n_rows, 2 * D), ACT), _sds((n_rows, LANE), ACT),
                  _sds((2, 8, D)), _sds((8, D)), _sds((SGG, Q, Q)), _sds((Q, LANE))],
                 scratch=[pltpu.VMEM((T, D), F32), pltpu.VMEM((T, D), F32)],
                 )(ctx, x, uv, dz, dxbcp, ddf, ddb, dysg, dres, mp2, wz, wuv, wxbc, wdt, lng, lnb, sgw, sgbt)


def _seq_edges(i, nct, nt):
    return (i == 0) | (i == nct), (i == nct - 1) | (i == nt - 1)


def _cv5_fwd(xbcp, w, b, nct):
    n_rows = xbcp.shape[0]
    nt = n_rows // T
    hb = 8

    def body(p_ref, c_ref, n_ref, w_ref, b_ref, o_ref, ext):
        first, last = _seq_edges(pl.program_id(0), nct, nt)
        _fill_ext(ext, p_ref, c_ref, n_ref, hb, first, last)
        for c in range(XBC // LANE):
            lanes = slice(c * LANE, (c + 1) * LANE)
            o_ref[:, lanes] = _silu(_conv(ext, w_ref, SK, hb, lanes) + b_ref[:, lanes])

    hp, hn = _halo(XBC, hb, n_rows)
    return _call(body, "cv5_fwd", (nt,), [hp, _rows(T, XBC), hn, _full(8, XBC), _full(1, XBC)],
                 _rows(T, XBC), _sds((n_rows, XBC)), scratch=[pltpu.VMEM((T + 2 * hb, XBC), F32)])(xbcp, xbcp, xbcp, w, b)


def _cv5_bwd1(xbcp, dxf, dxb, w, b, nct):
    n_rows = xbcp.shape[0]
    nt = n_rows // T
    hb = 8

    def body(p_ref, c_ref, n_ref, dxf_ref, dxb_ref, w_ref, b_ref, o_ref, pg_ref, ext):
        i = pl.program_id(0)
        first, last = _seq_edges(i, nct, nt)
        _fill_ext(ext, p_ref, c_ref, n_ref, hb, first, last)

        @pl.when(i == 0)
        def _():
            pg_ref[...] = jnp.zeros_like(pg_ref)
        for c in range(XBC // LANE):
            lanes = slice(c * LANE, (c + 1) * LANE)
            cv = _conv(ext, w_ref, SK, hb, lanes) + b_ref[:, lanes]
            sg = jax.nn.sigmoid(cv)
            dcv = (dxf_ref[:, lanes] + dxb_ref[:, lanes]) * (sg * (1.0 + cv * (1.0 - sg)))
            o_ref[:, lanes] = dcv
            pg_ref[0:1, lanes] += _sum0(dcv)

    hp, hn = _halo(XBC, hb, n_rows)
    return _call(body, "cv5_bwd1", (nt,),
                 [hp, _rows(T, XBC), hn, _rows(T, XBC), _rows(T, XBC), _full(8, XBC), _full(1, XBC)],
                 [_rows(T, XBC), _full(8, XBC)], [_sds((n_rows, XBC)), _sds((8, XBC))],
                 scratch=[pltpu.VMEM((T + 2 * hb, XBC), F32)])(xbcp, xbcp, xbcp, dxf, dxb, w, b)


def _cv5_bwd2(dcv, xbcp, w, nct):
    n_rows = xbcp.shape[0]
    nt = n_rows // T
    hb = 8

    def body(dp_ref, dc_ref, dn_ref, xp_ref, xc_ref, xn_ref, w_ref, o_ref, dw_ref, dext, xext):
        i = pl.program_id(0)
        first, last = _seq_edges(i, nct, nt)
        _fill_ext(dext, dp_ref, dc_ref, dn_ref, hb, first, last)
        _fill_ext(xext, xp_ref, xc_ref, xn_ref, hb, first, last)

        @pl.when(i == 0)
        def _():
            dw_ref[...] = jnp.zeros_like(dw_ref)
        for c in range(XBC // LANE):
            lanes = slice(c * LANE, (c + 1) * LANE)
            o_ref[:, lanes] = _conv_tr(dext, w_ref, SK, hb, lanes).astype(ACT)
            _conv_dw(dw_ref, dc_ref, xext, SK, hb, lanes)

    hp, hn = _halo(XBC, hb, n_rows)
    return _call(body, "cv5_bwd2", (nt,),
                 [hp, _rows(T, XBC), hn, hp, _rows(T, XBC), hn, _full(8, XBC)],
                 [_rows(T, XBC), _full(8, XBC)], [_sds((n_rows, XBC), ACT), _sds((8, XBC))],
                 scratch=[pltpu.VMEM((T + 2 * hb, XBC), F32), pltpu.VMEM((T + 2 * hb, XBC), F32)],
                 )(dcv, dcv, dcv, xbcp, xbcp, xbcp, w)


def _scan_order(nc, ncc, rev):
    if not rev:
        return lambda s: s
    return lambda s: jnp.where(s < ncc, ncc - 1 - s, nc - 1 - (s - ncc))


def _ssd_prep(dtr, sp, rev):
    dt = jax.nn.softplus(dtr + sp[0:1])
    a_neg = -jnp.exp(sp[1:2])
    r = lax.broadcasted_iota(jnp.int32, (Q, Q), 0)
    c = lax.broadcasted_iota(jnp.int32, (Q, Q), 1)
    msk = (c >= r) if rev else (c <= r)
    tri = msk.astype(F32)
    acs = jnp.dot(tri, dt * a_neg, precision=HI, preferred_element_type=F32)
    last = 0 if rev else Q - 1
    return dt, a_neg, acs, msk, tri, last


def _pair_sel(arr, lo, m, lane_lt):
    h0 = lo + 2 * m
    return jnp.where(lane_lt, arr[:, h0:h0 + 1], arr[:, h0 + 1:h0 + 2])


def _head_lanes(row, lo, g):
    lane = lax.broadcasted_iota(jnp.int32, (1, 512), 1)
    out = jnp.zeros((1, 512), F32)
    for k in range(8):
        h = lo + 8 * g + k
        out = jnp.where((lane >= 64 * k) & (lane < 64 * (k + 1)), row[:, h:h + 1], out)
    return out


def _halves(v, lane_lt):
    return jnp.concatenate([jnp.where(lane_lt, v, 0.0), jnp.where(lane_lt, 0.0, v)], axis=0)


def _ssd_fwd(xbc, dtr, sp, ncc, rev):
    n_rows = xbc.shape[0]
    nc = n_rows // Q
    lo = 16 if rev else 0
    order = _scan_order(nc, ncc, rev)

    def body(x_ref, dtr_ref, sp_ref, y_ref, hin_ref, st):
        @pl.when(pl.program_id(0) == 0)
        def _():
            st[...] = jnp.zeros_like(st)
        dt, _, acs, msk, _, last = _ssd_prep(dtr_ref[...], sp_ref[...], rev)
        acs_t, dt_t = acs.T, dt.T
        eacs = jnp.exp(acs)
        eal = jnp.exp(acs[last:last + 1, :])
        tew = jnp.exp(acs[last:last + 1, :] - acs) * dt
        lane_lt = lax.broadcasted_iota(jnp.int32, (Q, LANE), 1) < 64
        for g in range(2):
            gl = slice(g * 512, (g + 1) * 512)
            bg = x_ref[:, 1024 + g * 128:1152 + g * 128]
            cg = x_ref[:, 1280 + g * 128:1408 + g * 128]
            s_g = _nt(cg, bg)
            h_t = st[:, gl]
            hin_ref[:, gl] = h_t
            yoff = _nn(cg, h_t)
            xw = []
            for mm in range(4):
                m = 4 * g + mm
                ls = slice(m * LANE, (m + 1) * LANE)
                x2 = x_ref[:, ls]
                ws = []
                for hh in range(2):
                    h = lo + 2 * m + hh
                    lm = jnp.exp(jnp.where(msk, acs[:, h:h + 1] - acs_t[h:h + 1, :], -jnp.inf))
                    ws.append(s_g * lm * dt_t[h:h + 1, :])
                y2 = _nn(jnp.concatenate(ws, axis=1), _halves(x2, lane_lt))
                y_ref[:, ls] = y2 + yoff[:, mm * LANE:(mm + 1) * LANE] * _pair_sel(eacs, lo, m, lane_lt)
                xw.append(x2 * _pair_sel(tew, lo, m, lane_lt))
            st[:, gl] = _head_lanes(eal, lo, g) * h_t + _tn_dot(bg, jnp.concatenate(xw, axis=1))

    return _call(body, "ssd_fwd_r" if rev else "ssd_fwd_f", (nc,),
                 [pl.BlockSpec((Q, XBC), lambda s: (order(s), 0)), pl.BlockSpec((Q, LANE), lambda s: (order(s), 0)),
                  _full(8, LANE)],
                 [pl.BlockSpec((Q, D), lambda s: (order(s), 0)), pl.BlockSpec((None, LANE, D), lambda s: (order(s), 0, 0))],
                 [_sds((n_rows, D)), _sds((nc, LANE, D))], scratch=[pltpu.VMEM((LANE, D), F32)])(xbc, dtr, sp)


def _ssd_bwd(xbc, dtr, dy, hin, sp, dl, eh, ncc, rev):
    n_rows = xbc.shape[0]
    nc = n_rows // Q
    lo = 16 if rev else 0
    fwd_order = _scan_order(nc, ncc, rev)
    order = lambda s: fwd_order(nc - 1 - s)
    with_skip = not rev

    def body(x_ref, dtr_ref, dy_ref, hin_ref, sp_ref, dl_ref, eh_ref, dx_ref, ddtr_ref, pg_ref, dst):
        @pl.when(pl.program_id(0) == 0)
        def _():
            dst[...] = jnp.zeros_like(dst)
            pg_ref[...] = jnp.zeros_like(pg_ref)
        dtr_v = dtr_ref[...]
        dt, a_neg, acs, msk, tri, last = _ssd_prep(dtr_v, sp_ref[...], rev)
        acs_t = acs.T
        r = lax.broadcasted_iota(jnp.int32, (Q, Q), 0)
        c = lax.broadcasted_iota(jnp.int32, (Q, Q), 1)
        msk_t = (c <= r) if rev else (c >= r)
        eacs = jnp.exp(acs)
        eal = jnp.exp(acs[last:last + 1, :])
        te = jnp.exp(acs[last:last + 1, :] - acs)
        lane = lax.broadcasted_iota(jnp.int32, (Q, LANE), 1)
        lane1 = lax.broadcasted_iota(jnp.int32, (1, LANE), 1)
        lane_lt = lane < 64
        dacs = jnp.zeros((Q, LANE), F32)
        ddt_x = jnp.zeros((Q, LANE), F32)
        dlast = jnp.zeros((1, LANE), F32)
        hs_rows = []
        sub16 = lax.broadcasted_iota(jnp.int32, (16, Q), 0)
        dacs_t = jnp.zeros((16, Q), F32)
        for g in range(2):
            gl = slice(g * 512, (g + 1) * 512)
            bg = x_ref[:, 1024 + g * 128:1152 + g * 128]
            cg = x_ref[:, 1280 + g * 128:1408 + g * 128]
            s_g = _nt(cg, bg)
            s_gt = _nt(bg, cg)
            h_t, dh_t = hin_ref[:, gl], dst[:, gl]
            bh = _nn(bg, dh_t)
            yoff = _nn(cg, h_t)
            d_s = jnp.zeros((Q, Q), F32)
            edy, exd = [], []
            for mm in range(4):
                m = 4 * g + mm
                ls = slice(m * LANE, (m + 1) * LANE)
                x2, dy2 = x_ref[:, ls], dy_ref[:, ls]
                bh2 = bh[:, mm * LANE:(mm + 1) * LANE]
                dtm, em, eam = (_pair_sel(v, lo, m, lane_lt) for v in (dt, te, eacs))
                xd2 = x2 * dtm
                lms, mts = [], []
                for hh in range(2):
                    h = lo + 2 * m + hh
                    col, row = acs[:, h:h + 1], acs_t[h:h + 1, :]
                    lms.append(jnp.exp(jnp.where(msk, col - row, -jnp.inf)))
                    mts.append(s_gt * jnp.exp(jnp.where(msk_t, row - col, -jnp.inf)))
                dy_st = _halves(dy2, lane_lt)
                dxd2 = em * bh2 + _nn(jnp.concatenate(mts, axis=1), dy_st)
                dm_st = _nt(dy_st, xd2)
                dmt_st = _nt(_halves(xd2, lane_lt), dy2)
                d_s = d_s + dm_st[:Q] * lms[0] + dm_st[Q:] * lms[1]
                v1, v2, v3 = dy2 * yoff[:, mm * LANE:(mm + 1) * LANE] * eam, dxd2 * x2, xd2 * bh2 * em
                for hh in range(2):
                    h = lo + 2 * m + hh
                    half = lane_lt == (hh == 0)
                    g_rows = _sum0(dmt_st[hh * Q:(hh + 1) * Q] * mts[hh]) - _sum0(dm_st[hh * Q:(hh + 1) * Q] * s_g * lms[hh])
                    dacs_t = jnp.where(sub16 == 2 * m + hh, g_rows, dacs_t)
                    r1 = jnp.sum(jnp.where(half, v1, 0.0), axis=1, keepdims=True)
                    r2 = jnp.sum(jnp.where(half, v2, 0.0), axis=1, keepdims=True)
                    r3 = jnp.sum(jnp.where(half, v3, 0.0), axis=1, keepdims=True)
                    dacs = dacs + jnp.where(lane == h, r1 - r3, 0.0)
                    ddt_x = ddt_x + jnp.where(lane == h, r2, 0.0)
                    dlast = dlast + jnp.where(lane1 == h, _sum0(r3), 0.0)
                dx2 = dxd2 * dtm
                if with_skip:
                    dx2 = dx2 + dl_ref[:, ls] * dy2
                dx_ref[:, ls] = dx2
                edy.append(eam * dy2)
                exd.append(em * xd2)
            edy, exd = jnp.concatenate(edy, axis=1), jnp.concatenate(exd, axis=1)
            hs_rows.append(_sum0(h_t * dh_t))
            dst[:, gl] = _head_lanes(eal, lo, g) * dh_t + _tn_dot(cg, edy)
            dx_ref[:, 1024 + g * 128:1152 + g * 128] = _tn_dot(d_s, cg) + _nt(exd, dh_t)
            dx_ref[:, 1280 + g * 128:1408 + g * 128] = _nn(d_s, bg) + _nt(edy, h_t)
        hs = jnp.broadcast_to(jnp.concatenate(hs_rows, axis=1), (8, D))
        hsum = jnp.dot(hs, eh_ref[...], precision=HI, preferred_element_type=F32)[0:1]
        dlast = dlast + eal * hsum
        dacs = dacs + jnp.concatenate([jnp.zeros((lo, Q), F32)] * (lo > 0) + [dacs_t, jnp.zeros((LANE - 16 - lo, Q), F32)],
                                      axis=0).T
        rowi = lax.broadcasted_iota(jnp.int32, (Q, LANE), 0)
        dacs = dacs + jnp.where(rowi == last, dlast, 0.0)
        da = lax.dot_general(tri, dacs, (((0,), (0,)), ((), ())), precision=HI, preferred_element_type=F32)
        ddt = ddt_x + da * a_neg
        mine = (lane >= lo) & (lane < lo + 16)
        ddtr = jnp.where(mine, ddt * jax.nn.sigmoid(dtr_v + sp_ref[0:1]), 0.0)
        ddtr_ref[...] = ddtr
        pg_ref[0:1] += _sum0(ddtr)
        pg_ref[1:2] += jnp.where(mine[0:1], _sum0(da * dt) * a_neg, 0.0)

    blk = lambda w_: pl.BlockSpec((Q, w_), lambda s: (order(s), 0))
    return _call(body, "ssd_bwd_r" if rev else "ssd_bwd_f", (nc,),
                 [blk(XBC), blk(LANE), blk(D), pl.BlockSpec((None, LANE, D), lambda s: (order(s), 0, 0)),
                  _full(8, LANE), _full(1, D), _full(D, LANE)],
                 [blk(XBC), blk(LANE), _full(8, LANE)],
                 [_sds((n_rows, XBC)), _sds((n_rows, LANE)), _sds((8, LANE))],
                 scratch=[pltpu.VMEM((LANE, D), F32)])(xbc, dtr, dy, hin, sp, dl, eh)


def _hy4_fwd(h, yf, yb, xbc, z, ysg, mp, dl, ng, wout, nct):
    n_rows = h.shape[0]

    def body(h_ref, yf_ref, yb_ref, xs_ref, z_ref, ysg_ref, mp_ref, dl_ref, ng_ref, wout_ref, hn_ref, yssd_ref, out_ref):
        ytot = yf_ref[...] + yb_ref[...] + dl_ref[...] * xs_ref[...]
        yssd = _gate_norm(ytot, z_ref[...], ng_ref[...]).astype(MXU)
        yssd_ref[...] = yssd.astype(ACT)
        out = (jnp.dot(yssd, wout_ref[0:D, :], preferred_element_type=F32)
               + jnp.dot(ysg_ref[...].astype(MXU), wout_ref[D:2 * D, :], preferred_element_type=F32))
        out_ref[...] = out
        hn_ref[...] = h_ref[...] + mp_ref[3:4] * out

    return _call(body, "hy4_fwd", (n_rows // T,),
                 [_rows(T, D), _rows(T, D, nct), _rows(T, D, nct), _rows(T, D, nct), _rows(T, D, nct), _rows(T, D),
                  _full(8, D), _full(1, D), _full(1, D), _wfull(2 * D, D)],
                 [_rows(T, D), _rows(T, D), _rows(T, D)],
                 [_sds((n_rows, D)), _sds((n_rows, D), ACT), _sds((n_rows, D))])(h, yf, yb, xbc, z, ysg, mp, dl, ng, wout)


def _hy4_bwd(dh, out, yf, yb, xbc, z, mp, dl, ng, wout, nct):
    n_lat = dh.shape[0]
    n_rows = yf.shape[0]

    def body(dh_ref, out_ref, yf_ref, yb_ref, xs_ref, z_ref, mp_ref, dl_ref, ng_ref, wout_ref,
             dy_ref, dz_ref, dysg_ref, doutb_ref, pg_ref):
        i = pl.program_id(0)

        @pl.when(i == 0)
        def _():
            pg_ref[...] = jnp.zeros_like(pg_ref)

        @pl.when(i < nct)
        def _():
            dy_ref[...] = jnp.zeros_like(dy_ref)
            dz_ref[...] = jnp.zeros_like(dz_ref)
            dysg_ref[...] = jnp.zeros_like(dysg_ref)
            doutb_ref[...] = jnp.zeros_like(doutb_ref)

        @pl.when(i >= nct)
        def _():
            dhp = dh_ref[...]
            doutb = (mp_ref[3:4] * dhp).astype(MXU)
            doutb_ref[...] = doutb.astype(ACT)
            dysg_ref[...] = _nt(doutb, wout_ref[D:2 * D, :])
            dyssd = _nt(doutb, wout_ref[0:D, :])
            xs = xs_ref[...]
            ytot = yf_ref[...] + yb_ref[...] + dl_ref[...] * xs
            _, vjp = jax.vjp(_gate_norm, ytot, z_ref[...], ng_ref[...])
            dytot, dz, dng = vjp(dyssd)
            dy_ref[...] = dytot
            dz_ref[...] = dz.astype(ACT)
            pg_ref[0:1] += _sum0(dhp * out_ref[...])
            pg_ref[1:2] += dng
            pg_ref[2:3] += _sum0(dytot * xs)

    return _call(body, "hy4_bwd", (n_rows // T,),
                 [_rows_lat(T, D, nct), _rows_lat(T, D, nct), _rows(T, D), _rows(T, D), _rows(T, D), _rows(T, D),
                  _full(8, D), _full(1, D), _full(1, D), _wfull(2 * D, D)],
                 [_rows(T, D), _rows(T, D), _rows(T, D), _rows_lat(T, D, nct), _full(8, D)],
                 [_sds((n_rows, D)), _sds((n_rows, D), ACT), _sds((n_rows, D)), _sds((n_lat, D), ACT), _sds((8, D))],
                 )(dh, out, yf, yb, xbc, z, mp, dl, ng, wout)


def _loss_tail(hv, tgt, g, pg_ref, ls_ref):
    @pl.when(pl.program_id(0) == 0)
    def _():
        pg_ref[...] = jnp.zeros_like(pg_ref)
        ls_ref[...] = jnp.zeros_like(ls_ref)
    r = lax.rsqrt(jnp.mean(hv * hv, axis=-1, keepdims=True) + EPS)
    n = hv * r
    e = n * g - tgt
    ls_ref[...] += 0.5 * jnp.sum(jnp.sum(e * e, axis=1, keepdims=True), axis=0, keepdims=True) * (1.0 / D)
    dyv = e * (1.0 / D)
    pg_ref[0:1] += _sum0(dyv * n)
    dn = dyv * g
    return r * (dn - n * jnp.mean(dn * n, axis=-1, keepdims=True))


def _pad_rows(a, rows):
    return jnp.concatenate([a, jnp.zeros((rows - a.shape[0],) + a.shape[1:], a.dtype)], axis=0)


def _mp(*rows):
    return _pad_rows(jnp.stack(rows, axis=0), 8)


def _local_step(x, ctx, tgt, ada, cada0, w, late_w=None, early_grads=None, small_grads=None):
    n_lat, n_ctx = x.shape[0], ctx.shape[0]
    nct, ncc = n_ctx // T, n_ctx // Q
    a0 = [ada[0, k * D:(k + 1) * D] for k in range(6)]
    a1 = [ada[1, k * D:(k + 1) * D] for k in range(6)]
    c0 = [cada0[k * D:(k + 1) * D] for k in range(6)]
    g = {}

    mp2 = jnp.stack([_mp(w["norm_mix_g"][0], c0[0], c0[1]), _mp(w["norm_mix_g"][0], a0[0], a0[1], a0[2])], axis=0)
    mp_l0 = mp2[1]
    sgbt = _pad_cols(w["sg_b"][0].T, LANE)
    lng, lnb = w["sg_ln_g"][0][None], w["sg_ln_b"][0][None]
    z, uv, xbcp, dtr, ysg = _hy1_fwd(ctx, x, mp2, w["wz"], w["wuv"], w["wxbc"], w["wdt"], lng, lnb, w["sg_w"], sgbt, nct)
    cw = _pad_rows(w["ssd_conv_w"][0], 8)
    cb = w["ssd_conv_b"][0][None]
    xbc = _cv5_fwd(xbcp, cw, cb, nct)
    sp = _pad_rows(jnp.stack([_pad_cols(w["ssd_dt_bias"][0].reshape(1, 32), LANE)[0],
                              _pad_cols(w["ssd_a_log"][0].reshape(1, 32), LANE)[0]], axis=0), 8)
    dl = jnp.repeat(w["ssd_d"][0], 64)[None]
    ng = w["ssd_norm_g"][0][None]
    yf, hin_f = _ssd_fwd(xbc, dtr, sp, ncc, False)
    yb, hin_b = _ssd_fwd(xbc, dtr, sp, ncc, True)
    if late_w is not None:
        w = {**w, **late_w(yb)}
    h1, yssd, out0 = _hy4_fwd(x, yf, yb, xbc, z, ysg, mp_l0, dl, ng, w["hy_w_out"], nct)

    mpm0 = _mp(w["norm_mlp_g"][0], a0[3], a0[4], a0[5])
    h2, am0, ym0 = _mlp_fwd(h1, mpm0, w["wpack"], 0, "mlp0_fwd")

    mpc = _mp(w["norm_mix_g"][1], a1[0], a1[1], a1[2])
    wdw = _pad_rows(w["cf_w_dw"][0], 32)
    glu, acf = _cf1_fwd(h2, mpc, w["cf_w_pw1"], w["cf_b_pw1"])
    h3, cv, scf, ycf = _cf2_fwd(h2, glu, mpc, wdw, w["cf_b_dw"], w["cf_ln_g"], w["cf_ln_b"], w["cf_w_pw2"], w["cf_b_pw2"])

    mpm1 = _mp(w["norm_mlp_g"][1], a1[3], a1[4], a1[5])
    dh4, am1, ym1, pg_f, ls = _mlp_fwd(h3, mpm1, w["wpack"], 1, "mlp1_fwd", tail=(tgt, w["final_norm_g"][None]))

    loss = ls[0, 0]
    g["final_norm_g"] = pg_f[0]

    gp = {}
    dh3, da1, dy1, u1, pgm1, dcv, dycf, pgc2 = _mlp_bwd(
        dh4, h3, am1, ym1, mpm1, w["wpack"], 1, "mlp1_bwd",
        tail=(ycf, cv, mpc, w["cf_ln_g"], w["cf_ln_b"], w["cf_w_pw2"]))
    gw1_1 = _tn(u1, da1, "tn_mlp1_w1", shard=("col", 1024))
    gw2_1 = _tn(am1, dy1, "tn_mlp1_w2", relu2=True, shard=("row", 1024))

    gp["cf_w_pw2"] = _tn(scf, dycf, "tn_cf_pw2", shard=("row", 256))
    dh2, dacf, ucf, pgc1, pbc1, dwdw = _cf1_bwd(dh3, h2, acf, dcv, glu, mpc, wdw, w["cf_w_pw1"])
    gp["cf_w_pw1"] = _tn(ucf, dacf, "tn_cf_pw1", shard=("col", 512)).reshape(4, 512, 1024)
    g["cf_b_pw2"], g["cf_ln_g"], g["cf_ln_b"], g["cf_b_dw"] = pgc2[1], pgc2[2], pgc2[3], pgc2[4]
    g["cf_b_pw1"] = pbc1[0]
    g["cf_w_dw"] = dwdw[:CK]

    dh1, da0, dy0, u0, pgm0 = _mlp_bwd(dh2, h1, am0, ym0, mpm0, w["wpack"], 0, "mlp0_bwd")
    gp["mlp_w1"] = jnp.concatenate([_tn(u0, da0, "tn_mlp0_w1", shard=("col", 1024)), gw1_1], axis=1)
    gp["mlp_w2"] = jnp.concatenate([_tn(am0, dy0, "tn_mlp0_w2", relu2=True, shard=("row", 1024)), gw2_1], axis=1)
    g["norm_mlp_g"] = jnp.stack([pgm0[0], pgm1[0]])

    dyt, dz, dysg, doutb, pg4 = _hy4_bwd(dh1, out0, yf, yb, xbc, z, mp_l0, dl, ng, w["hy_w_out"], nct)
    gp["hy_w_out"] = jnp.concatenate([_tn(yssd, doutb, "tn_out_ssd", shard=("row", 512)),
                                      _tn(ysg, doutb, "tn_out_sg", shard=("row", 512))], axis=0)
    if early_grads is not None:
        sp = sp + early_grads(gp)
    head_of_lane = jnp.arange(D, dtype=jnp.int32)[:, None] // 64
    col = jnp.arange(LANE, dtype=jnp.int32)[None, :]
    dxf, ddf, pgsf = _ssd_bwd(xbc, dtr, dyt, hin_f, sp, dl, (col == head_of_lane).astype(F32), ncc, False)
    dxb, ddb, pgsb = _ssd_bwd(xbc, dtr, dyt, hin_b, sp, dl, (col == head_of_lane + 16).astype(F32), ncc, True)
    dcv5, pgcb = _cv5_bwd1(xbcp, dxf, dxb, cw, cb, nct)
    dxbcp, dcw = _cv5_bwd2(dcv5, xbcp, cw, nct)
    dx, ucat, duv, ddt, pg2, pln, dsgw, dsgbt = _hy1_bwd(
        ctx, x, uv, dz, dxbcp, ddf, ddb, dysg, dh1, mp2, w["wz"], w["wuv"], w["wxbc"], w["wdt"], lng, lnb, w["sg_w"], sgbt, nct)
    g["ssd_conv_w"], g["ssd_conv_b"] = dcw[:SK], pgcb[0]
    pgs = pgsf + pgsb
    g["ssd_dt_bias"], g["ssd_a_log"] = pgs[0, :32].reshape(2, 16), pgs[1, :32].reshape(2, 16)
    g["ssd_d"] = jnp.sum(pg4[2].reshape(16, 64), axis=1)
    g["ssd_norm_g"] = pg4[1]
    g["sg_ln_g"], g["sg_ln_b"] = pln[0], pln[1]
    g["sg_w"], g["sg_b"] = dsgw, dsgbt[:, :SGG].T
    g["norm_mix_g"] = jnp.stack([pg2[0, 0] + pg2[1, 0], pgc1[0]])

    zero = jnp.zeros((D,), F32)
    d_ada = jnp.stack([jnp.concatenate([pg2[1, 1], pg2[1, 2], pg4[0], pgm0[1], pgm0[2], pgm0[3]]),
                       jnp.concatenate([pgc1[1], pgc1[2], pgc2[0], pgm1[1], pgm1[2], pgm1[3]])])
    d_cada0 = jnp.concatenate([pg2[0, 1], pg2[0, 2], zero, zero, zero, zero])
    if small_grads is not None:
        ucat, _ = lax.optimization_barrier((ucat, small_grads(g, d_ada, d_cada0)))
    gp["hy_w_in"] = _win_join(_tn(ucat, dz, "tn_in_z"), _tn(ucat, dxbcp, "tn_in_xbc"), _tn(ucat, ddt, "tn_in_dt"),
                              _tn(ucat, duv, "tn_in_uv"))
    g["pieces"] = gp
    return loss, dx, g, d_ada, d_cada0


def _pad_cols(a, cols):
    return jnp.concatenate([a, jnp.zeros(a.shape[:-1] + (cols - a.shape[-1],), a.dtype)], axis=-1)


MESH = pl.DeviceIdType.MESH
ANY = pl.BlockSpec(memory_space=pl.ANY)
IN_VMEM = pl.BlockSpec(memory_space=pltpu.VMEM)


def _coords():
    return lax.axis_index("x"), lax.axis_index("y"), lax.axis_index("c")


def _ag8(x, name):
    r, wd = x.shape

    def body(x_ref, o_ref, send, recv, lsem):
        mx, my, mc = _coords()
        me = 4 * mx + 2 * my + mc
        mine = pltpu.make_async_copy(x_ref, o_ref.at[me], lsem)
        mine.start()
        sent, peers = [], []
        for k in range(1, 8):
            px = 1 - mx if k & 4 else mx
            py = 1 - my if k & 2 else my
            pc = 1 - mc if k & 1 else mc
            cp = pltpu.make_async_remote_copy(src_ref=x_ref, dst_ref=o_ref.at[me], send_sem=send.at[k - 1],
                                              recv_sem=recv.at[k - 1], device_id=(px, py, pc), device_id_type=MESH)
            cp.start()
            sent.append(cp)
            peers.append((4 * px + 2 * py + pc, (px, py, pc)))
        for k in range(1, 8):
            slot, peer = peers[k - 1]
            pltpu.make_async_remote_copy(src_ref=x_ref, dst_ref=o_ref.at[slot], send_sem=send.at[k - 1],
                                         recv_sem=recv.at[k - 1], device_id=peer, device_id_type=MESH).wait_recv()
        for cp in sent:
            cp.wait_send()
        mine.wait()

    return pl.pallas_call(
        body, name=name, out_shape=_sds((8, r, wd), x.dtype), in_specs=[IN_VMEM], out_specs=IN_VMEM,
        scratch_shapes=[pltpu.SemaphoreType.DMA((7,)), pltpu.SemaphoreType.DMA((7,)), pltpu.SemaphoreType.DMA(())],
        compiler_params=pltpu.CompilerParams(vmem_limit_bytes=VMEM_LIMIT))(x)


HBM = pl.BlockSpec(memory_space=pltpu.HBM)
SEM = pl.BlockSpec(memory_space=pltpu.SEMAPHORE)
EFFECT = pltpu.SideEffectType.DATAFLOW_SIDE_EFFECTING


def _x4_peers(in_ref, land_ref, send, recv, a2a):
    mx, my, mc = _coords()
    me = 2 * mx + my
    out = []
    for k in range(1, 4):
        px = 1 - mx if k & 2 else mx
        py = 1 - my if k & 1 else my
        pj = 2 * px + py
        mk = functools.partial(pltpu.make_async_remote_copy, src_ref=in_ref.at[pj] if a2a else in_ref,
                               send_sem=send.at[k - 1], recv_sem=recv.at[k - 1], device_id=(px, py, mc), device_id_type=MESH)
        out.append((mk(dst_ref=land_ref.at[me]), mk(dst_ref=land_ref.at[pj])))
    return out


def _x4_start(buf, name, a2a):
    r, wd = buf.shape[-2:]

    def body(in_ref, land_ref, send, recv, in_thru, land_thru, token):
        for start, _ in _x4_peers(in_ref, land_ref, send, recv, a2a):
            start.start()
        token[...] = jnp.zeros_like(token)

    land = lax.empty((4, r, wd), buf.dtype)
    return pl.pallas_call(
        body, name=name,
        out_shape=(pltpu.SemaphoreType.DMA((3,)), pltpu.SemaphoreType.DMA((3,)), pltpu.HBM(buf.shape, buf.dtype),
                   pltpu.HBM(land.shape, land.dtype), _sds((8, LANE))),
        in_specs=(HBM, HBM), out_specs=(SEM, SEM, HBM, HBM, IN_VMEM), input_output_aliases={0: 2, 1: 3},
        compiler_params=pltpu.CompilerParams(has_side_effects=EFFECT),
    )(pltpu.with_memory_space_constraint(buf, pltpu.HBM), pltpu.with_memory_space_constraint(land, pltpu.HBM))


def _x4_wait(send, recv, buf_thru, land_thru, after, name, a2a):
    def body(in_ref, land_ref, send_ref, recv_ref, after_ref, in_dead, got_ref):
        for _, arrive in _x4_peers(in_ref, land_ref, send_ref, recv_ref, a2a):
            arrive.wait_send()
            arrive.wait_recv()

    return pl.pallas_call(
        body, name=name, out_shape=(pltpu.HBM(buf_thru.shape, buf_thru.dtype), pltpu.HBM(land_thru.shape, land_thru.dtype)),
        in_specs=(HBM, HBM, SEM, SEM, ANY), out_specs=(HBM, HBM), input_output_aliases={0: 0, 1: 1},
        compiler_params=pltpu.CompilerParams(has_side_effects=EFFECT),
    )(buf_thru, land_thru, send, recv, after)


def _ag8_peers(x_ref, land_ref, send, recv):
    mx, my, mc = _coords()
    me = 4 * mx + 2 * my + mc
    out = []
    for k in range(1, 8):
        px = 1 - mx if k & 4 else mx
        py = 1 - my if k & 2 else my
        pc = 1 - mc if k & 1 else mc
        mk = functools.partial(pltpu.make_async_remote_copy, src_ref=x_ref, send_sem=send.at[k - 1], recv_sem=recv.at[k - 1],
                               device_id=(px, py, pc), device_id_type=MESH)
        out.append((mk(dst_ref=land_ref.at[me]), mk(dst_ref=land_ref.at[4 * px + 2 * py + pc])))
    return out


def _split_start(x, land_shape, peers, n_copies, name):
    def body(x_ref, land_ref, send, recv, x_thru, land_thru, token):
        for start, _ in peers(x_ref, land_ref, send, recv):
            start.start()
        token[...] = jnp.zeros_like(token)

    land = lax.empty(land_shape, x.dtype)
    return pl.pallas_call(
        body, name=name,
        out_shape=(pltpu.SemaphoreType.DMA((n_copies,)), pltpu.SemaphoreType.DMA((n_copies,)), pltpu.HBM(x.shape, x.dtype),
                   pltpu.HBM(land.shape, land.dtype), _sds((8, LANE))),
        in_specs=(HBM, HBM), out_specs=(SEM, SEM, HBM, HBM, IN_VMEM), input_output_aliases={0: 2, 1: 3},
        compiler_params=pltpu.CompilerParams(has_side_effects=EFFECT),
    )(pltpu.with_memory_space_constraint(x, pltpu.HBM), pltpu.with_memory_space_constraint(land, pltpu.HBM))


def _split_wait(handle, after, peers, name):
    send, recv, x_thru, land_thru, _ = handle

    def body(x_ref, land_ref, send_ref, recv_ref, after_ref, x_dead, got_ref):
        for _, arrive in peers(x_ref, land_ref, send_ref, recv_ref):
            arrive.wait_send()
            arrive.wait_recv()

    return pl.pallas_call(
        body, name=name, out_shape=(pltpu.HBM(x_thru.shape, x_thru.dtype), pltpu.HBM(land_thru.shape, land_thru.dtype)),
        in_specs=(HBM, HBM, SEM, SEM, ANY), out_specs=(HBM, HBM), input_output_aliases={0: 0, 1: 1},
        compiler_params=pltpu.CompilerParams(has_side_effects=EFFECT),
    )(x_thru, land_thru, send, recv, after)


def _sib_peers(x_ref, land_ref, send, recv):
    mx, my, mc = _coords()
    cp = pltpu.make_async_remote_copy(src_ref=x_ref, dst_ref=land_ref, send_sem=send.at[0], recv_sem=recv.at[0],
                                      device_id=(mx, my, 1 - mc), device_id_type=MESH)
    return [(cp, cp)]


def _xchg_sib(x, name):
    def body(in_ref, o_ref, send, recv):
        mx, my, mc = _coords()
        cp = pltpu.make_async_remote_copy(src_ref=in_ref, dst_ref=o_ref, send_sem=send, recv_sem=recv,
                                          device_id=(mx, my, 1 - mc), device_id_type=MESH)
        cp.start()
        cp.wait_recv()
        cp.wait_send()

    return pl.pallas_call(
        body, name=name, out_shape=_sds(x.shape, x.dtype), in_specs=[ANY], out_specs=ANY,
        scratch_shapes=[pltpu.SemaphoreType.DMA(()), pltpu.SemaphoreType.DMA(())])(x)


def _sum_slots(gat, slots, name, tr=None):
    n, r, wd = gat.shape
    tr = r if tr is None else tr

    def body(g_ref, o_ref):
        acc = g_ref[slots[0]].astype(F32)
        for s in slots[1:]:
            acc = acc + g_ref[s].astype(F32)
        o_ref[...] = acc

    return _call(body, name, (r // tr,), [pl.BlockSpec((n, tr, wd), lambda i: (0, i, 0))], _rows(tr, wd), _sds((r, wd)))(gat)


def _add(a, b, name, tr):
    def body(a_ref, b_ref, o_ref):
        o_ref[...] = a_ref[...] + b_ref[...]

    r, wd = a.shape
    return _call(body, name, (r // tr,), [_rows(tr, wd), _rows(tr, wd)], _rows(tr, wd), _sds((r, wd)))(a, b)


def _ada_fwd(x16, ada_w_loc, ada_b_loc):
    nloc = ada_w_loc.shape[-1]

    def body(x_ref, w_ref, b_ref, s_ref, o_ref):
        s = _silu(x_ref[...])
        s_ref[...] = s
        o_ref[...] = jnp.dot(s, w_ref[...], precision=HI, preferred_element_type=F32) + b_ref[...]

    return _call(body, "ada_fwd", (2,),
                 [_full(16, D), pl.BlockSpec((None, D, nloc), lambda l: (l, 0, 0)), pl.BlockSpec((None, 1, nloc), lambda l: (l, 0, 0))],
                 [_full(16, D), pl.BlockSpec((None, 16, nloc), lambda l: (l, 0, 0))],
                 [_sds((16, D)), _sds((2, 16, nloc))])(x16, ada_w_loc, ada_b_loc[:, None, :])


def _ada_bwd(s16, d_loc, ada_w_loc):
    nloc = ada_w_loc.shape[-1]

    def body(s_ref, d_ref, w_ref, gw_ref, cp_ref):
        gw_ref[...] = lax.dot_general(s_ref[...], d_ref[...], (((0,), (0,)), ((), ())), precision=HI,
                                      preferred_element_type=F32)

        @pl.when(pl.program_id(0) == 0)
        def _():
            cp_ref[...] = lax.dot_general(d_ref[8:16, :], w_ref[...], (((1,), (1,)), ((), ())), precision=HI,
                                          preferred_element_type=F32)

    return _call(body, "ada_bwd", (2,),
                 [_full(16, D), pl.BlockSpec((None, 16, nloc), lambda l: (l, 0, 0)), pl.BlockSpec((None, D, nloc), lambda l: (l, 0, 0))],
                 [pl.BlockSpec((None, D, nloc), lambda l: (l, 0, 0)), _full(8, D)],
                 [_sds((2, D, nloc)), _sds((8, D))])(s16, d_loc, ada_w_loc)


def _cctx_grad(dscc, c_ctx):
    def body(d_ref, c_ref, o_ref):
        _, vjp = jax.vjp(_silu, c_ref[...])
        o_ref[...] = vjp(d_ref[...])[0]

    return _call(body, "cctx_grad", (1,), [_full(8, D), _full(8, D)], _full(8, D), _sds((8, D)))(dscc, c_ctx)


def _adamw_math(w, g, m, v):
    mn = ADAM_B1 * m + (1.0 - ADAM_B1) * g
    vn = ADAM_B2 * v + (1.0 - ADAM_B2) * jnp.square(g)
    c1 = 1.0 - ADAM_B1 ** ADAM_STEP
    c2 = 1.0 - ADAM_B2 ** ADAM_STEP
    return -ADAM_LR * ((mn / c1) / (jnp.sqrt(vn / c2) + ADAM_EPS) + ADAM_WD * w), mn, vn


def _adamw(w, g, m, v, name):
    n_l, r, wd = w.shape
    tr = 256 if r % 256 == 0 else r

    def body(w_ref, g_ref, m_ref, v_ref, d_ref, mo_ref, vo_ref):
        d_ref[...], mo_ref[...], vo_ref[...] = _adamw_math(w_ref[...], g_ref[...], m_ref[...], v_ref[...])

    spec = pl.BlockSpec((None, tr, wd), lambda a, i: (a, i, 0))
    return tuple(_call(body, name, (n_l, r // tr), [spec] * 4, [spec] * 3, [_sds(w.shape)] * 3)(w, g, m, v))


def _adamw_rows(w, part, sib, m, v, r0, name):
    rows = w.shape[0]
    tr = 256

    def body(w_ref, p_ref, s_ref, m_ref, v_ref, g_ref, d_ref, mo_ref, vo_ref):
        g = p_ref[...] + s_ref[...]
        g_ref[...] = g
        d_ref[...], mo_ref[...], vo_ref[...] = _adamw_math(w_ref[...], g, m_ref[...], v_ref[...])

    here, there = _rows(tr, ROW), _rows(tr, ROW, r0 // tr)
    return tuple(_call(body, name, (rows // tr,), [here, there, there, here, here], [here] * 4, [_sds(w.shape)] * 4)(
        w, part, sib, m, v))


def _adamw_small(ws, gs, ms, vs, name):
    n = len(ws)
    shapes = [a.shape for a in ws]
    as2d = lambda a: a.reshape(-1, a.shape[-1])

    def body(*refs):
        ins, outs = refs[:4 * n], refs[4 * n:]
        for k in range(n):
            res = _adamw_math(ins[k][...], ins[n + k][...], ins[2 * n + k][...], ins[3 * n + k][...])
            for j in range(3):
                outs[j * n + k][...] = res[j]

    flat = [as2d(a) for group in (ws, gs, ms, vs) for a in group]
    specs = [_full(*a.shape) for a in flat]
    outs = _call(body, name, (1,), specs, specs[:n] * 3, [_sds(a.shape) for a in flat[:n]] * 3)(*flat)
    return tuple([outs[j * n + k].reshape(shapes[k]) for k in range(n)] for j in range(3))


ROW = 1024


def _nrows(size):
    return -(-size // ROW)


def _pack(arrs, rows_total, dtype=F32):
    parts = []
    for a in arrs:
        flat = a.reshape(-1).astype(dtype)
        pad = _nrows(flat.shape[0]) * ROW - flat.shape[0]
        parts.append(flat if pad == 0 else jnp.concatenate([flat, jnp.zeros((pad,), dtype)]))
    flat = jnp.concatenate(parts)
    out = flat.reshape(-1, ROW)
    return _pad_rows(out, rows_total)


def _unpack(buf, shapes):
    lead = buf.shape[:-2]
    out, r0 = [], 0
    for shp in shapes:
        size = 1
        for s in shp:
            size *= s
        nr = _nrows(size)
        piece = lax.slice_in_dim(buf, r0, r0 + nr, axis=len(lead))
        out.append(piece.reshape(lead + (nr * ROW,))[..., :size].reshape(lead + tuple(shp)))
        r0 += nr
    return out


SLOT = 16


def _slot_rows(size):
    return _round_up(size // ROW, SLOT)


def _pack_rows(arrs, rows_total, dtype):
    parts, used = [], 0
    for a in arrs:
        part = a.astype(dtype).reshape(-1, ROW)
        extra = _slot_rows(a.size) - part.shape[0]
        parts.append(part if extra == 0 else jnp.pad(part, ((0, extra), (0, 0))))
        used += _slot_rows(a.size)
    if rows_total > used:
        parts.append(jnp.zeros((rows_total - used, ROW), dtype))
    return jnp.concatenate(parts, axis=0)


def _unpack_rows(buf, shapes):
    lead = buf.shape[:-2]
    out, r0 = [], 0
    for shp in shapes:
        size = 1
        for s in shp:
            size *= s
        piece = lax.slice_in_dim(buf, r0, r0 + size // ROW, axis=len(lead))
        out.append(piece.reshape(lead + tuple(shp)))
        r0 += _slot_rows(size)
    return out


def _round_up(n, k):
    return -(-n // k) * k


WEIGHTS = ['c_ctx', 'ada_w', 'ada_b', 'norm_mix_g', 'norm_mlp_g', 'mlp_w1', 'mlp_w2', 'hy_w_in', 'ssd_conv_w', 'ssd_conv_b',
           'ssd_dt_bias', 'ssd_a_log', 'ssd_d', 'ssd_norm_g', 'sg_ln_g', 'sg_ln_b', 'sg_w', 'sg_b', 'hy_w_out', 'cf_w_pw1',
           'cf_b_pw1', 'cf_w_dw', 'cf_b_dw', 'cf_ln_g', 'cf_ln_b', 'cf_w_pw2', 'cf_b_pw2', 'final_norm_g']
BIG = {'mlp_w1': 2, 'mlp_w2': 1, 'hy_w_in': 2, 'hy_w_out': 1, 'cf_w_pw1': 2, 'cf_w_pw2': 1}
SMALL_SHARD = ['ssd_conv_w', 'cf_b_pw1', 'cf_w_dw', 'cf_b_dw', 'cf_ln_g', 'cf_ln_b', 'cf_b_pw2']
REP = ['norm_mix_g', 'norm_mlp_g', 'ssd_conv_b', 'ssd_dt_bias', 'ssd_a_log', 'ssd_d', 'ssd_norm_g', 'sg_ln_g', 'sg_ln_b',
       'sg_w', 'sg_b', 'final_norm_g']


def _gather_shards(stacked, axis):
    return jnp.concatenate([stacked[j] for j in range(4)], axis=axis)


def kernel(x, c, ctx, c_ctx, ada_w, ada_b, norm_mix_g, norm_mlp_g, mlp_w1, mlp_w2, hy_w_in, ssd_conv_w, ssd_conv_b, ssd_dt_bias, ssd_a_log, ssd_d, ssd_norm_g, sg_ln_g, sg_ln_b, sg_w, sg_b, hy_w_out, cf_w_pw1, cf_b_pw1, cf_w_dw, cf_b_dw, cf_ln_g, cf_ln_b, cf_w_pw2, cf_b_pw2, final_norm_g, loss_target, m_c_ctx, m_ada_w, m_ada_b, m_norm_mix_g, m_norm_mlp_g, m_mlp_w1, m_mlp_w2, m_hy_w_in, m_ssd_conv_w, m_ssd_conv_b, m_ssd_dt_bias, m_ssd_a_log, m_ssd_d, m_ssd_norm_g, m_sg_ln_g, m_sg_ln_b, m_sg_w, m_sg_b, m_hy_w_out, m_cf_w_pw1, m_cf_b_pw1, m_cf_w_dw, m_cf_b_dw, m_cf_ln_g, m_cf_ln_b, m_cf_w_pw2, m_cf_b_pw2, m_final_norm_g, v_c_ctx, v_ada_w, v_ada_b, v_norm_mix_g, v_norm_mlp_g, v_mlp_w1, v_mlp_w2, v_hy_w_in, v_ssd_conv_w, v_ssd_conv_b, v_ssd_dt_bias, v_ssd_a_log, v_ssd_d, v_ssd_norm_g, v_sg_ln_g, v_sg_ln_b, v_sg_w, v_sg_b, v_hy_w_out, v_cf_w_pw1, v_cf_b_pw1, v_cf_w_dw, v_cf_b_dw, v_cf_ln_g, v_cf_ln_b, v_cf_w_pw2, v_cf_b_pw2, v_final_norm_g):
    args = locals()
    wl = {n: args[n] for n in WEIGHTS}
    ml = {n: args["m_" + n] for n in WEIGHTS}
    vl = {n: args["v_" + n] for n in WEIGHTS}
    mx, my, mc = _coords()
    me = 4 * mx + 2 * my + mc
    shard = 2 * mx + my
    even = (0, 2, 4, 6)

    def start_gather(names, name, tie=None):
        rows = sum(_slot_rows(wl[n].size) for n in names)
        buf = _pack_rows([wl[n] for n in names], rows, MXU)
        if tie is not None:
            buf, _ = lax.optimization_barrier((buf, tie))
        return _x4_start(buf, name, a2a=False)

    def finish_gather(handle, names, after, name):
        send, recv, own, land, _ = handle
        own, land = _x4_wait(send, recv, own, land, after, name, a2a=False)
        got = lax.dynamic_update_slice(land, own[None], (shard, 0, 0))
        shapes = [wl[n].shape for n in names]
        wfull = {n: _gather_shards(st, BIG[n]) for n, st in zip(names, _unpack_rows(got, shapes))}
        return wfull, got

    rest_names = ["mlp_w1", "mlp_w2", "hy_w_out", "cf_w_pw1", "cf_w_pw2"]
    h_in = _x4_start(hy_w_in[0].astype(MXU), "agw_in_start", a2a=False)
    c = c + h_in[4][0, 0]

    small_shapes = [wl[n].shape for n in SMALL_SHARD]
    blk1 = _pack([c] + [wl[n] for n in SMALL_SHARD], 24)
    got1 = _ag8(blk1, "ag_cond")
    x16 = _pad_rows(jnp.concatenate([got1[:, 0, :], c_ctx[None]], axis=0), 16)
    small_full = {}
    for n, parts in zip(SMALL_SHARD, _unpack(got1[:, 1:, :], small_shapes)):
        small_full[n] = jnp.concatenate([parts[s] for s in even], axis=-1)

    nloc = ada_w.shape[-1]
    ada_b_loc = lax.dynamic_slice_in_dim(ada_b, shard * nloc, nloc, axis=1)
    s16, ada_loc = _ada_fwd(x16, ada_w, ada_b_loc)
    got2 = _ag8(ada_loc.reshape(32, nloc), "ag_ada").reshape(8, 2, 16, nloc)
    ada_full = jnp.concatenate([got2[s] for s in even], axis=-1)
    ada_me = lax.dynamic_slice_in_dim(ada_full, me, 1, axis=1)[:, 0, :]
    cada0 = ada_full[0, 8, :]

    w = {n: wl[n] for n in WEIGHTS if n not in BIG and n not in SMALL_SHARD}
    w.update(small_full)
    h_rest = start_gather(rest_names, "agw_rest_start", tie=ada_me)
    send, recv, own, land, _ = h_in
    own, land = _x4_wait(send, recv, own, land, h_rest[4], "agw_in_wait", a2a=False)
    w["wz"], w["wxbc"], w["wdt"], w["wuv"] = _win_split(lax.dynamic_update_slice(land, own[None], (shard, 0, 0)))
    w["sg_w"] = sg_w[0].astype(MXU)

    def late_w(after):
        wfull, got = finish_gather(h_rest, rest_names, after, "agw_rest_wait")
        return {"hy_w_out": wfull["hy_w_out"][0], "wpack": got,
                "cf_w_pw1": wfull["cf_w_pw1"][0], "cf_w_pw2": wfull["cf_w_pw2"][0]}

    full_shape = {n: wl[n].shape for n in WEIGHTS}
    for n in BIG:
        full_shape[n] = tuple(s * 4 if a == BIG[n] else s for a, s in enumerate(wl[n].shape))
    for n in SMALL_SHARD:
        full_shape[n] = wl[n].shape[:-1] + (wl[n].shape[-1] * 4,)

    early_names = ["mlp_w1", "mlp_w2", "cf_w_pw1", "cf_w_pw2", "hy_w_out"]
    early = {}

    def early_grads(gp):
        used = sum(gp[n].shape[1] for n in early_names)
        parts = [gp[n] for n in early_names] + [jnp.zeros((4, _round_up(used, 512) - used, ROW), jnp.bfloat16)]
        early["h"] = _x4_start(jnp.concatenate(parts, axis=1), "a2a_early_start", a2a=True)
        return early["h"][4][0, 0]

    sm_names = REP + SMALL_SHARD
    r_ada = sum(_nrows(wl[n].size * (4 if n in SMALL_SHARD else 1)) for n in sm_names)
    small = {}

    def small_grads(g_, d_ada_, d_cada0_):
        buf = _pack([g_[n] for n in sm_names] + [d_ada_, d_cada0_], _round_up(r_ada + 18, 16), jnp.bfloat16)
        small["h"] = _split_start(buf, (8,) + buf.shape, _ag8_peers, 7, "ag_small_start")
        return small["h"][4][0, 0]

    loss_part, dx, g, d_ada, d_cada0 = _local_step(x[0], ctx[0], loss_target[0], ada_me, cada0, w, late_w, early_grads,
                                                   small_grads)
    gp_last = g["pieces"]["hy_w_in"]

    delta, new_m, new_v, grads = {}, {}, {}, {}
    h_last = _x4_start(gp_last, "a2a_last_start", a2a=True)
    send, recv, own, land, _ = early["h"]
    own, land = _x4_wait(send, recv, own, land, h_last[4], "a2a_early_wait", a2a=True)
    got = lax.dynamic_update_slice(land, lax.dynamic_slice_in_dim(own, shard, 1, axis=0), (shard, 0, 0))
    part_early = _sum_slots(got, (0, 1, 2, 3), "sum_grads_early", tr=512)
    h_swap = _split_start(part_early, part_early.shape, _sib_peers, 1, "swap_early_start")

    small_in, land3 = _split_wait(small["h"], h_swap[4], _ag8_peers, "ag_small_wait")
    got3 = lax.dynamic_update_slice(land3, small_in[None], (me, 0, 0))
    tot3 = _sum_slots(got3, tuple(range(8)), "sum_small")
    sm_tot = _unpack(tot3, [full_shape[n] for n in sm_names] + [(2, 6 * D), (6 * D,)])
    grads.update(zip(sm_names, sm_tot[:-2]))
    for n in SMALL_SHARD:
        k = wl[n].shape[-1]
        grads[n] = lax.dynamic_slice_in_dim(grads[n], shard * k, k, axis=grads[n].ndim - 1)
    dada_tot, dcada_tot = sm_tot[-2], sm_tot[-1]
    grads["ada_b"] = dada_tot.at[0].add(dcada_tot)
    dada_all = got3[:, r_ada:r_ada + 12, :].astype(F32).reshape(8, 2, 6 * D)
    d16 = jnp.concatenate([jnp.transpose(dada_all, (1, 0, 2)),
                           jnp.stack([dcada_tot, jnp.zeros_like(dcada_tot)])[:, None, :],
                           jnp.zeros((2, 7, 6 * D), F32)], axis=1)
    d_loc = lax.dynamic_slice_in_dim(d16, shard * nloc, nloc, axis=2)
    grads["ada_w"], cpart = _ada_bwd(s16, d_loc, ada_w)
    cpart = lax.dynamic_update_slice(cpart, jnp.full((1, D), loss_part, F32), (1, 0))
    h_cctx = _split_start(cpart, (8,) + cpart.shape, _ag8_peers, 7, "ag_cctx_start")
    grads["ada_w"], _ = lax.optimization_barrier((grads["ada_w"], h_cctx[4]))

    delta["ada_w"], new_m["ada_w"], new_v["ada_w"] = _adamw(ada_w, grads["ada_w"], ml["ada_w"], vl["ada_w"], "adamw_ada_w")
    small_names = ["ada_b"] + REP + SMALL_SHARD
    outs = _adamw_small(*([src[n].reshape(wl[n].shape) for n in small_names] for src in (wl, grads, ml, vl)), "adamw_small")
    for dst, vals in zip((delta, new_m, new_v), outs):
        dst.update(zip(small_names, vals))

    def step(n, tot):
        grads[n] = tot.reshape(wl[n].shape)
        delta[n], new_m[n], new_v[n] = _adamw(wl[n], grads[n], ml[n], vl[n], "adamw_" + n)

    part_early, sib_early = _split_wait(h_swap, delta["ada_w"], _sib_peers, "swap_early_wait")
    r0 = 0
    for n in early_names:
        nr, shp = _slot_rows(wl[n].size), wl[n].shape
        if shp[-1] == ROW:
            outs = _adamw_rows(wl[n].reshape(-1, ROW), part_early, sib_early, ml[n].reshape(-1, ROW), vl[n].reshape(-1, ROW),
                               r0, "adamw_" + n)
            grads[n], delta[n], new_m[n], new_v[n] = (o.reshape(shp) for o in outs)
        else:
            step(n, _add(lax.slice_in_dim(part_early, r0, r0 + nr), lax.slice_in_dim(sib_early, r0, r0 + nr), "add_" + n, nr))
        r0 += nr
    send, recv, own, land, _ = h_last
    own, land = _x4_wait(send, recv, own, land, delta["mlp_w1"], "a2a_last_wait", a2a=True)
    got = lax.dynamic_update_slice(land, lax.dynamic_slice_in_dim(own, shard, 1, axis=0), (shard, 0, 0))
    part_last = _sum_slots(got, (0, 1, 2, 3), "sum_grads_last", tr=512)
    step("hy_w_in", _add(part_last, _xchg_sib(part_last, "swap_grads_last"), "add_grads_last", 512))

    c_in, land4 = _split_wait(h_cctx, delta["hy_w_in"], _ag8_peers, "ag_cctx_wait")
    got4 = lax.dynamic_update_slice(land4, c_in[None], (me, 0, 0))
    loss = _sum_slots(got4, tuple(range(8)), "sum_loss")[1, 0]
    grads["c_ctx"] = _cctx_grad(_sum_slots(got4, even, "sum_cctx"), _pad_rows(c_ctx[None], 8))[0]
    outs = _adamw_small(*([src["c_ctx"]] for src in (wl, grads, ml, vl)), "adamw_c_ctx")
    delta["c_ctx"], new_m["c_ctx"], new_v["c_ctx"] = (o[0] for o in outs)

    return (loss, dx[None], *[grads[n].reshape(wl[n].shape) for n in WEIGHTS], *[delta[n] for n in WEIGHTS],
            *[new_m[n] for n in WEIGHTS], *[new_v[n] for n in WEIGHTS])
```
